```python
import math
import jax, jax.numpy as jnp
from jax import lax
import numpy as np

D_MODEL = 1024
BATCH = 8
SEQ = 16384
DEPTH = 2

CHUNK = 64
Q_BLOCK = 128
A_HEADS = 8
A_HEAD_DIM = 64
A_WIDTH = A_HEADS * A_HEAD_DIM
B_EXPAND = 128
B_WIDTH = D_MODEL // 2
B_HEADS = B_WIDTH // B_EXPAND
B_DK = B_EXPAND
B_DV = B_WIDTH // B_HEADS
FFN_HIDDEN = int(math.ceil((8 * D_MODEL / 3) / 256) * 256)
ALPHA = (2 * DEPTH) ** 0.25
BETA = (8 * DEPTH) ** -0.25
LN_EPS = 1e-5
RMS_EPS = 1e-6

IN_WIDTHS = [A_WIDTH, A_WIDTH, A_WIDTH, A_HEADS,
             B_WIDTH, B_WIDTH, B_WIDTH, B_WIDTH,
             D_MODEL, D_MODEL]
IN_TOTAL = int(sum(IN_WIDTHS))
IN_SPLITS = [int(s) for s in np.cumsum(IN_WIDTHS)[:-1]]

kernel_name = "fox_hgrn2_gated_hybrid_deepnorm"


def layer_norm(x, g, b):
    xf = x.astype(jnp.float32)
    mu = jnp.mean(xf, axis=-1, keepdims=True)
    var = jnp.mean(jnp.square(xf - mu), axis=-1, keepdims=True)
    y = (xf - mu) * lax.rsqrt(var + LN_EPS) * g.astype(jnp.float32) + b.astype(jnp.float32)
    return y.astype(x.dtype)


def fox_attention(q, k, v, log_f):
    B, S, H, dh = q.shape
    nb = S // Q_BLOCK
    F = jnp.cumsum(log_f, axis=1).transpose(0, 2, 1)
    kh = k.transpose(0, 2, 1, 3)
    vh = v.transpose(0, 2, 1, 3)
    qb = q.reshape(B, nb, Q_BLOCK, H, dh).transpose(1, 0, 3, 2, 4)
    fq = F.reshape(B, H, nb, Q_BLOCK).transpose(2, 0, 1, 3)
    starts = jnp.arange(nb, dtype=jnp.int32) * Q_BLOCK
    key_pos = jnp.arange(S, dtype=jnp.int32)
    scale = dh ** -0.5

    def one_block(args):
        q_i, fq_i, start = args
        s = jnp.einsum('bhqd,bhkd->bhqk', q_i, kh).astype(jnp.float32) * scale
        s = s + fq_i[..., None] - F[:, :, None, :]
        q_pos = start + jnp.arange(Q_BLOCK, dtype=jnp.int32)
        causal = key_pos[None, :] <= q_pos[:, None]
        p = jax.nn.softmax(jnp.where(causal, s, -jnp.inf), axis=-1)
        return jnp.einsum('bhqk,bhkd->bhqd', p.astype(vh.dtype), vh)

    o = lax.map(one_block, (qb, fq, starts))
    return o.transpose(1, 0, 3, 2, 4).reshape(B, S, H * dh)


def hgrn2_chunkwise(q, k, v, log_f):
    B, S, H, dk = q.shape
    dv = v.shape[-1]
    nc = S // CHUNK

    def to_chunks(t):
        return t.reshape(B, nc, CHUNK, H, t.shape[-1]).transpose(1, 0, 3, 2, 4)

    qc, kc, vc, gc = to_chunks(q), to_chunks(k), to_chunks(v), to_chunks(log_f)
    incl = jnp.tril(jnp.ones((CHUNK, CHUNK), dtype=bool))

    def step(state, inp):
        q_i, k_i, v_i, g_i = inp
        b = jnp.cumsum(g_i, axis=2)
        o_inter = jnp.einsum('bhtd,bhde->bhte', q_i * jnp.exp(b), state)
        diff = b[:, :, :, None, :] - b[:, :, None, :, :]
        decay = jnp.exp(jnp.where(incl[None, None, :, :, None], diff, -jnp.inf))
        attn = jnp.einsum('bhtd,bhtsd,bhsd->bhts', q_i, decay, k_i)
        o_intra = jnp.einsum('bhts,bhse->bhte', attn, v_i)
        b_last = b[:, :, -1:, :]
        k_dec = k_i * jnp.exp(b_last - b)
        new_state = (jnp.exp(b_last[:, :, 0, :])[..., None] * state
                     + jnp.einsum('bhsd,bhse->bhde', k_dec, v_i))
        return new_state, o_inter + o_intra

    s0 = jnp.zeros((B, H, dk, dv), jnp.float32)
    _, o = lax.scan(step, s0, (qc, kc, vc, gc))
    return o.transpose(1, 0, 3, 2, 4).reshape(B, S, H, dv)


def hybrid_mixer(x, w_in, b_f, lb, norm_g, w_pa, w_pb, w_o):
    B, S, _ = x.shape
    proj = x @ w_in
    q_a, k_a, v_a, f_a, q_b, f_b, i_b, g_b, gate_a, gate_b = jnp.split(proj, IN_SPLITS, axis=-1)

    log_fa = jax.nn.log_sigmoid(f_a.astype(jnp.float32) + b_f.astype(jnp.float32))
    shp_a = (B, S, A_HEADS, A_HEAD_DIM)
    y_a = fox_attention(q_a.reshape(shp_a), k_a.reshape(shp_a), v_a.reshape(shp_a), log_fa)

    lbh = lb.astype(jnp.float32).reshape(B_HEADS, B_DK)
    f = lbh + (1.0 - lbh) * jax.nn.sigmoid(f_b.astype(jnp.float32).reshape(B, S, B_HEADS, B_DK))
    qh = jax.nn.silu(q_b.astype(jnp.float32)).reshape(B, S, B_HEADS, B_DK)
    vh = i_b.astype(jnp.float32).reshape(B, S, B_HEADS, B_DV)
    o_b = hgrn2_chunkwise(qh, 1.0 - f, vh, jnp.log(f))
    o_b = o_b * lax.rsqrt(jnp.mean(jnp.square(o_b), axis=-1, keepdims=True) + RMS_EPS)
    o_b = o_b * norm_g.astype(jnp.float32)
    y_b = (o_b.reshape(B, S, B_WIDTH) * jax.nn.sigmoid(g_b.astype(jnp.float32))).astype(x.dtype)

    p_a = y_a @ w_pa
    p_b = y_b @ w_pb
    merged = jax.nn.sigmoid(gate_a) * p_a + jax.nn.sigmoid(gate_b) * p_b
    return merged @ w_o


def swiglu_ffn(x, w_ff_in, w_ff_out):
    h = x @ w_ff_in
    u, g = jnp.split(h, 2, axis=-1)
    return (jax.nn.silu(g) * u) @ w_ff_out


def _fwd_setup_inputs(seed: int = 0) -> dict:
    key = jax.random.key(seed)
    ks = jax.random.split(key, 16)
    f32 = jnp.float32
    x = jax.random.normal(ks[0], (BATCH, SEQ, D_MODEL), f32)
    col_scale = np.ones((IN_TOTAL,), np.float32)
    col_scale[2 * A_WIDTH:3 * A_WIDTH] = BETA
    v_b_start = 3 * A_WIDTH + A_HEADS + 2 * B_WIDTH
    col_scale[v_b_start:v_b_start + B_WIDTH] = BETA
    w_in = jax.random.normal(ks[1], (DEPTH, D_MODEL, IN_TOTAL), f32) * (D_MODEL ** -0.5) * jnp.asarray(col_scale)
    b_fgate = jnp.linspace(1.0, 5.0, A_HEADS, dtype=f32)[None, :] + 0.1 * jax.random.normal(ks[2], (DEPTH, A_HEADS), f32)
    hgrn_lb_logits = 0.1 * jax.random.normal(ks[3], (DEPTH, B_WIDTH), f32)
    hgrn_norm_g = 1.0 + 0.02 * jax.random.normal(ks[4], (DEPTH, B_DV), f32)
    w_branch_a = jax.random.normal(ks[5], (DEPTH, A_WIDTH, D_MODEL), f32) * (A_WIDTH ** -0.5) * BETA
    w_branch_b = jax.random.normal(ks[6], (DEPTH, B_WIDTH, D_MODEL), f32) * (B_WIDTH ** -0.5) * BETA
    w_out = jax.random.normal(ks[7], (DEPTH, D_MODEL, D_MODEL), f32) * (D_MODEL ** -0.5) * BETA
    ln1_g = 1.0 + 0.02 * jax.random.normal(ks[8], (DEPTH, D_MODEL), f32)
    ln1_b = 0.02 * jax.random.normal(ks[9], (DEPTH, D_MODEL), f32)
    w_ff_in = jax.random.normal(ks[10], (DEPTH, D_MODEL, 2 * FFN_HIDDEN), f32) * (D_MODEL ** -0.5)
    w_ff_out = jax.random.normal(ks[11], (DEPTH, FFN_HIDDEN, D_MODEL), f32) * (FFN_HIDDEN ** -0.5) * BETA
    ln2_g = 1.0 + 0.02 * jax.random.normal(ks[12], (DEPTH, D_MODEL), f32)
    ln2_b = 0.02 * jax.random.normal(ks[13], (DEPTH, D_MODEL), f32)
    return {"x": x, "w_in": w_in, "b_fgate": b_fgate, "hgrn_lb_logits": hgrn_lb_logits,
            "hgrn_norm_g": hgrn_norm_g, "w_branch_a": w_branch_a, "w_branch_b": w_branch_b,
            "w_out": w_out, "ln1_g": ln1_g, "ln1_b": ln1_b, "w_ff_in": w_ff_in,
            "w_ff_out": w_ff_out, "ln2_g": ln2_g, "ln2_b": ln2_b}


def _fwd_reference(x, w_in, b_fgate, hgrn_lb_logits, hgrn_norm_g, w_branch_a, w_branch_b,
              w_out, ln1_g, ln1_b, w_ff_in, w_ff_out, ln2_g, ln2_b):
    sm = jax.nn.softmax(hgrn_lb_logits.astype(jnp.float32), axis=0)
    lower_bounds = jnp.cumsum(sm, axis=0) - sm[0:1]
    for l in range(DEPTH):
        mix = hybrid_mixer(x, w_in[l], b_fgate[l], lower_bounds[l], hgrn_norm_g[l],
                           w_branch_a[l], w_branch_b[l], w_out[l])
        x = layer_norm(ALPHA * x + mix, ln1_g[l], ln1_b[l])
        ffn = swiglu_ffn(x, w_ff_in[l], w_ff_out[l])
        x = layer_norm(ALPHA * x + ffn, ln2_g[l], ln2_b[l])
    return x


import jax as _jax
import jax.numpy as _jnp

TWIN_FORMAT = 'train_step'
FWD_PARAMS = ['x', 'w_in', 'b_fgate', 'hgrn_lb_logits', 'hgrn_norm_g', 'w_branch_a', 'w_branch_b', 'w_out', 'ln1_g', 'ln1_b', 'w_ff_in', 'w_ff_out', 'ln2_g', 'ln2_b']
TWIN_WEIGHTS = ['w_in', 'b_fgate', 'hgrn_lb_logits', 'hgrn_norm_g', 'w_branch_a', 'w_branch_b', 'w_out', 'ln1_g', 'ln1_b', 'w_ff_in', 'w_ff_out', 'ln2_g', 'ln2_b']
TWIN_DIFF_INPUT = 'x'
TWIN_INPUTS = ['x', 'w_in', 'b_fgate', 'hgrn_lb_logits', 'hgrn_norm_g', 'w_branch_a', 'w_branch_b', 'w_out', 'ln1_g', 'ln1_b', 'w_ff_in', 'w_ff_out', 'ln2_g', 'ln2_b', 'loss_target', 'm_w_in', 'm_b_fgate', 'm_hgrn_lb_logits', 'm_hgrn_norm_g', 'm_w_branch_a', 'm_w_branch_b', 'm_w_out', 'm_ln1_g', 'm_ln1_b', 'm_w_ff_in', 'm_w_ff_out', 'm_ln2_g', 'm_ln2_b', 'v_w_in', 'v_b_fgate', 'v_hgrn_lb_logits', 'v_hgrn_norm_g', 'v_w_branch_a', 'v_w_branch_b', 'v_w_out', 'v_ln1_g', 'v_ln1_b', 'v_w_ff_in', 'v_w_ff_out', 'v_ln2_g', 'v_ln2_b']
TWIN_OUTPUTS = ['loss', 'grad_x', 'grad_w_in', 'grad_b_fgate', 'grad_hgrn_lb_logits', 'grad_hgrn_norm_g', 'grad_w_branch_a', 'grad_w_branch_b', 'grad_w_out', 'grad_ln1_g', 'grad_ln1_b', 'grad_w_ff_in', 'grad_w_ff_out', 'grad_ln2_g', 'grad_ln2_b', 'delta_w_in', 'delta_b_fgate', 'delta_hgrn_lb_logits', 'delta_hgrn_norm_g', 'delta_w_branch_a', 'delta_w_branch_b', 'delta_w_out', 'delta_ln1_g', 'delta_ln1_b', 'delta_w_ff_in', 'delta_w_ff_out', 'delta_ln2_g', 'delta_ln2_b', 'new_m_w_in', 'new_m_b_fgate', 'new_m_hgrn_lb_logits', 'new_m_hgrn_norm_g', 'new_m_w_branch_a', 'new_m_w_branch_b', 'new_m_w_out', 'new_m_ln1_g', 'new_m_ln1_b', 'new_m_w_ff_in', 'new_m_w_ff_out', 'new_m_ln2_g', 'new_m_ln2_b', 'new_v_w_in', 'new_v_b_fgate', 'new_v_hgrn_lb_logits', 'new_v_hgrn_norm_g', 'new_v_w_branch_a', 'new_v_w_branch_b', 'new_v_w_out', 'new_v_ln1_g', 'new_v_ln1_b', 'new_v_w_ff_in', 'new_v_w_ff_out', 'new_v_ln2_g', 'new_v_ln2_b']
TWIN_LEAF_KINDS = {'loss': 'loss', 'grad_x': 'grad_x', 'grad_w_in': 'grad_w', 'grad_b_fgate': 'grad_w', 'grad_hgrn_lb_logits': 'grad_w', 'grad_hgrn_norm_g': 'grad_w', 'grad_w_branch_a': 'grad_w', 'grad_w_branch_b': 'grad_w', 'grad_w_out': 'grad_w', 'grad_ln1_g': 'grad_w', 'grad_ln1_b': 'grad_w', 'grad_w_ff_in': 'grad_w', 'grad_w_ff_out': 'grad_w', 'grad_ln2_g': 'grad_w', 'grad_ln2_b': 'grad_w', 'delta_w_in': 'delta_w', 'delta_b_fgate': 'delta_w', 'delta_hgrn_lb_logits': 'delta_w', 'delta_hgrn_norm_g': 'delta_w', 'delta_w_branch_a': 'delta_w', 'delta_w_branch_b': 'delta_w', 'delta_w_out': 'delta_w', 'delta_ln1_g': 'delta_w', 'delta_ln1_b': 'delta_w', 'delta_w_ff_in': 'delta_w', 'delta_w_ff_out': 'delta_w', 'delta_ln2_g': 'delta_w', 'delta_ln2_b': 'delta_w', 'new_m_w_in': 'new_m', 'new_m_b_fgate': 'new_m', 'new_m_hgrn_lb_logits': 'new_m', 'new_m_hgrn_norm_g': 'new_m', 'new_m_w_branch_a': 'new_m', 'new_m_w_branch_b': 'new_m', 'new_m_w_out': 'new_m', 'new_m_ln1_g': 'new_m', 'new_m_ln1_b': 'new_m', 'new_m_w_ff_in': 'new_m', 'new_m_w_ff_out': 'new_m', 'new_m_ln2_g': 'new_m', 'new_m_ln2_b': 'new_m', 'new_v_w_in': 'new_v', 'new_v_b_fgate': 'new_v', 'new_v_hgrn_lb_logits': 'new_v', 'new_v_hgrn_norm_g': 'new_v', 'new_v_w_branch_a': 'new_v', 'new_v_w_branch_b': 'new_v', 'new_v_w_out': 'new_v', 'new_v_ln1_g': 'new_v', 'new_v_ln1_b': 'new_v', 'new_v_w_ff_in': 'new_v', 'new_v_w_ff_out': 'new_v', 'new_v_ln2_g': 'new_v', 'new_v_ln2_b': 'new_v'}


def _forward(args):
    return _fwd_reference(*[args[k] for k in FWD_PARAMS])


def _output_shape():
    def fwd():
        inp = _fwd_setup_inputs(0)
        return _fwd_reference(*[inp[k] for k in FWD_PARAMS])
    out = _jax.eval_shape(fwd)
    return out.shape, out.dtype

N_MICROBATCH = 1
ADAM_LR = 0.001
ADAM_B1 = 0.9
ADAM_B2 = 0.999
ADAM_EPS = 1e-08
ADAM_WD = 0.01
ADAM_STEP = 10
PER_EXAMPLE_BATCH_AXIS = {'x': 0, 'loss_target': 0}
SHARED_INPUTS = []
_WEIGHT_DTYPES = {'w_in': _jnp.float32, 'b_fgate': _jnp.float32, 'hgrn_lb_logits': _jnp.float32, 'hgrn_norm_g': _jnp.float32, 'w_branch_a': _jnp.float32, 'w_branch_b': _jnp.float32, 'w_out': _jnp.float32, 'ln1_g': _jnp.float32, 'ln1_b': _jnp.float32, 'w_ff_in': _jnp.float32, 'w_ff_out': _jnp.float32, 'ln2_g': _jnp.float32, 'ln2_b': _jnp.float32}
MOMENT_SCALE = {'w_in': 1.805543e-02, 'b_fgate': 2.986212e-02, 'hgrn_lb_logits': 2.593239e-03, 'hgrn_norm_g': 5.166983e-02, 'w_branch_a': 1.271444e-02, 'w_branch_b': 3.837554e-02, 'w_out': 3.996833e-02, 'ln1_g': 3.872634e+00, 'ln1_b': 1.782440e+00, 'w_ff_in': 4.543746e-02, 'w_ff_out': 1.488183e-01, 'ln2_g': 9.074394e+01, 'ln2_b': 3.095493e+00}


def _to_microbatches(a, axis):
    t = _jnp.moveaxis(a, axis, 0)
    t = t.reshape((N_MICROBATCH, t.shape[0] // N_MICROBATCH) + t.shape[1:])
    return _jnp.moveaxis(t, 1, axis + 1)


def setup_inputs(seed: int = 0) -> dict:
    inp = _fwd_setup_inputs(seed)
    key = _jax.random.fold_in(_jax.random.key(seed), 7919)
    shape, _ = _output_shape()
    out = dict(inp)
    out["loss_target"] = _jax.random.normal(_jax.random.fold_in(key, 0), shape, _jnp.float32)
    for i, name in enumerate(TWIN_WEIGHTS):
        w = inp[name].astype(_jnp.float32)
        if MOMENT_SCALE is None:
            s = _jnp.sqrt(_jnp.mean(_jnp.square(w)) + 1e-30)
        else:
            s = MOMENT_SCALE[name]
        km, kv = _jax.random.split(_jax.random.fold_in(key, i + 1))
        out[name] = w
        out["m_" + name] = s * _jax.random.normal(km, w.shape, _jnp.float32)
        out["v_" + name] = (s * s) * _jax.random.uniform(kv, w.shape, _jnp.float32, 0.5, 1.5)
    if N_MICROBATCH > 1:
        for name, axis in PER_EXAMPLE_BATCH_AXIS.items():
            out[name] = _to_microbatches(out[name], axis)
    return {'x': out['x'], 'w_in': out['w_in'], 'b_fgate': out['b_fgate'], 'hgrn_lb_logits': out['hgrn_lb_logits'], 'hgrn_norm_g': out['hgrn_norm_g'], 'w_branch_a': out['w_branch_a'], 'w_branch_b': out['w_branch_b'], 'w_out': out['w_out'], 'ln1_g': out['ln1_g'], 'ln1_b': out['ln1_b'], 'w_ff_in': out['w_ff_in'], 'w_ff_out': out['w_ff_out'], 'ln2_g': out['ln2_g'], 'ln2_b': out['ln2_b'], 'loss_target': out['loss_target'], 'm_w_in': out['m_w_in'], 'm_b_fgate': out['m_b_fgate'], 'm_hgrn_lb_logits': out['m_hgrn_lb_logits'], 'm_hgrn_norm_g': out['m_hgrn_norm_g'], 'm_w_branch_a': out['m_w_branch_a'], 'm_w_branch_b': out['m_w_branch_b'], 'm_w_out': out['m_w_out'], 'm_ln1_g': out['m_ln1_g'], 'm_ln1_b': out['m_ln1_b'], 'm_w_ff_in': out['m_w_ff_in'], 'm_w_ff_out': out['m_w_ff_out'], 'm_ln2_g': out['m_ln2_g'], 'm_ln2_b': out['m_ln2_b'], 'v_w_in': out['v_w_in'], 'v_b_fgate': out['v_b_fgate'], 'v_hgrn_lb_logits': out['v_hgrn_lb_logits'], 'v_hgrn_norm_g': out['v_hgrn_norm_g'], 'v_w_branch_a': out['v_w_branch_a'], 'v_w_branch_b': out['v_w_branch_b'], 'v_w_out': out['v_w_out'], 'v_ln1_g': out['v_ln1_g'], 'v_ln1_b': out['v_ln1_b'], 'v_w_ff_in': out['v_w_ff_in'], 'v_w_ff_out': out['v_w_ff_out'], 'v_ln2_g': out['v_ln2_g'], 'v_ln2_b': out['v_ln2_b']}


def _loss(weights, diff, rest, loss_target):
    with _jax.named_scope("forward"):
        args = {**rest, TWIN_DIFF_INPUT: diff, **{k: w.astype(_WEIGHT_DTYPES[k]) for k, w in weights.items()}}
        y = _forward(args)
    with _jax.named_scope("loss_head"):
        err = _jnp.square(y.astype(_jnp.float32) - loss_target)
        return 0.5 * _jnp.sum(_jnp.mean(err, axis=-1)) if err.ndim else 0.5 * err


def _adamw(w, g, m, v):
    m = ADAM_B1 * m + (1.0 - ADAM_B1) * g
    v = ADAM_B2 * v + (1.0 - ADAM_B2) * _jnp.square(g)
    m_hat = m / (1.0 - ADAM_B1 ** ADAM_STEP)
    v_hat = v / (1.0 - ADAM_B2 ** ADAM_STEP)
    delta = -ADAM_LR * (m_hat / (_jnp.sqrt(v_hat) + ADAM_EPS) + ADAM_WD * w)
    return delta, m, v


def reference(x, w_in, b_fgate, hgrn_lb_logits, hgrn_norm_g, w_branch_a, w_branch_b, w_out, ln1_g, ln1_b, w_ff_in, w_ff_out, ln2_g, ln2_b, loss_target, m_w_in, m_b_fgate, m_hgrn_lb_logits, m_hgrn_norm_g, m_w_branch_a, m_w_branch_b, m_w_out, m_ln1_g, m_ln1_b, m_w_ff_in, m_w_ff_out, m_ln2_g, m_ln2_b, v_w_in, v_b_fgate, v_hgrn_lb_logits, v_hgrn_norm_g, v_w_branch_a, v_w_branch_b, v_w_out, v_ln1_g, v_ln1_b, v_w_ff_in, v_w_ff_out, v_ln2_g, v_ln2_b):
    given = dict(x=x, w_in=w_in, b_fgate=b_fgate, hgrn_lb_logits=hgrn_lb_logits, hgrn_norm_g=hgrn_norm_g, w_branch_a=w_branch_a, w_branch_b=w_branch_b, w_out=w_out, ln1_g=ln1_g, ln1_b=ln1_b, w_ff_in=w_ff_in, w_ff_out=w_ff_out, ln2_g=ln2_g, ln2_b=ln2_b, loss_target=loss_target, m_w_in=m_w_in, m_b_fgate=m_b_fgate, m_hgrn_lb_logits=m_hgrn_lb_logits, m_hgrn_norm_g=m_hgrn_norm_g, m_w_branch_a=m_w_branch_a, m_w_branch_b=m_w_branch_b, m_w_out=m_w_out, m_ln1_g=m_ln1_g, m_ln1_b=m_ln1_b, m_w_ff_in=m_w_ff_in, m_w_ff_out=m_w_ff_out, m_ln2_g=m_ln2_g, m_ln2_b=m_ln2_b, v_w_in=v_w_in, v_b_fgate=v_b_fgate, v_hgrn_lb_logits=v_hgrn_lb_logits, v_hgrn_norm_g=v_hgrn_norm_g, v_w_branch_a=v_w_branch_a, v_w_branch_b=v_w_branch_b, v_w_out=v_w_out, v_ln1_g=v_ln1_g, v_ln1_b=v_ln1_b, v_w_ff_in=v_w_ff_in, v_w_ff_out=v_w_ff_out, v_ln2_g=v_ln2_g, v_ln2_b=v_ln2_b)
    weights = {n: given[n] for n in TWIN_WEIGHTS}
    shared = {n: given[n] for n in SHARED_INPUTS}
    per_example = {n: given[n] for n in ['x']}
    grad_fn = _jax.value_and_grad(_loss, argnums=(0, 1))

    def one_microbatch(ex, loss_target):
        ex = dict(ex)
        diff = ex.pop(TWIN_DIFF_INPUT)
        return grad_fn(weights, diff, {**shared, **ex}, loss_target)

    if N_MICROBATCH == 1:
        loss, (grad_w, grad_x) = one_microbatch(per_example, given["loss_target"])
    else:
        def body(carry, xs):
            loss_sum, grad_sum = carry
            l_k, (gw_k, gx_k) = one_microbatch(xs[0], xs[1])
            with _jax.named_scope("update"):
                return (loss_sum + l_k, _jax.tree.map(_jnp.add, grad_sum, gw_k)), gx_k

        init = (_jnp.zeros((), _jnp.float32), _jax.tree.map(_jnp.zeros_like, weights))
        (loss, grad_w), grad_x = _jax.lax.scan(body, init, (per_example, given["loss_target"]))
    with _jax.named_scope("update"):
        delta_w, new_m, new_v = {}, {}, {}
        for n in TWIN_WEIGHTS:
            delta_w[n], new_m[n], new_v[n] = _adamw(weights[n], grad_w[n], given["m_" + n], given["v_" + n])
    return (loss, grad_x, *[grad_w[n] for n in TWIN_WEIGHTS], *[delta_w[n] for n in TWIN_WEIGHTS],
            *[new_m[n] for n in TWIN_WEIGHTS], *[new_v[n] for n in TWIN_WEIGHTS])
```

```python
import functools
import math

import jax
import jax.numpy as jnp
import numpy as np
from jax import lax
from jax.experimental import pallas as pl
from jax.experimental.pallas import tpu as pltpu

F32 = jnp.float32
BF16 = jnp.bfloat16

D_MODEL = 1024
DEPTH = 2
A_HEADS = 8
A_WIDTH = 512
B_WIDTH = 512
B_HEADS = 4
HD = 128
CHUNK = 64
SUB = 16
FFN_HIDDEN = 2816
IN_TOTAL = 5640
ALPHA = (2 * DEPTH) ** 0.25
LN_EPS = 1e-5
RMS_EPS = 1e-6
ADAM_LR = 0.001
ADAM_B1 = 0.9
ADAM_B2 = 0.999
ADAM_EPS = 1e-08
ADAM_WD = 0.01
ADAM_STEP = 10
EXP_CLAMP = 60.0

VMEM_LIMIT_BYTES = 56 * 1024 * 1024
N_CHIPS = 4
N_DEV = 8
MESH = pl.DeviceIdType.MESH

_DN = {
    "nn": (((1,), (0,)), ((), ())),
    "nt": (((1,), (1,)), ((), ())),
    "tn": (((0,), (0,)), ((), ())),
}


def _dot(a, b, mode="nn"):
    return lax.dot_general(a.astype(BF16), b.astype(BF16), _DN[mode], preferred_element_type=F32)


def _dot_hi(a, b, mode="nn"):
    return lax.dot_general(a, b, _DN[mode], precision=lax.Precision.HIGHEST, preferred_element_type=F32)


def _hdot(a, b, mode="nn"):
    return _dot_hi(a.astype(F32), b.astype(F32), mode)


def _params(*sem):
    return pltpu.CompilerParams(dimension_semantics=sem, vmem_limit_bytes=VMEM_LIMIT_BYTES)


def _sigmoid(x):
    return 1.0 / (1.0 + jnp.exp(-x))


def _matmul(a, b, mode, out_dtype, tm, tn, tk, name):
    if mode == "nn":
        (m, k), (k2, n) = a.shape, b.shape
    elif mode == "nt":
        (m, k), (n, k2) = a.shape, b.shape
    else:
        (k, m), (k2, n) = a.shape, b.shape
    assert k == k2, (a.shape, b.shape, mode)
    tm, tn, tk = min(tm, m), min(tn, n), min(tk, k)
    assert m % tm == 0 and n % tn == 0 and k % tk == 0, (a.shape, b.shape, tm, tn, tk)
    nk = k // tk
    if mode == "tn":
        a_spec = pl.BlockSpec((tk, tm), lambda j, i, kk: (kk, i))
    else:
        a_spec = pl.BlockSpec((tm, tk), lambda j, i, kk: (i, kk))
    if mode == "nt":
        b_spec = pl.BlockSpec((tn, tk), lambda j, i, kk: (j, kk))
    else:
        b_spec = pl.BlockSpec((tk, tn), lambda j, i, kk: (kk, j))
    use_acc = nk > 1 and out_dtype != F32

    def body(a_ref, b_ref, o_ref, *scratch):
        p = _dot(a_ref[...], b_ref[...], mode)
        if nk == 1:
            o_ref[...] = p.astype(out_dtype)
            return
        acc_ref = scratch[0] if use_acc else o_ref
        kk = pl.program_id(2)

        @pl.when(kk == 0)
        def _():
            acc_ref[...] = p

        @pl.when(kk > 0)
        def _():
            acc_ref[...] += p

        if use_acc:
            @pl.when(kk == nk - 1)
            def _():
                o_ref[...] = acc_ref[...].astype(out_dtype)

    return pl.pallas_call(
        body,
        name=name,
        grid=(n // tn, m // tm, nk),
        in_specs=[a_spec, b_spec],
        out_specs=pl.BlockSpec((tm, tn), lambda j, i, kk: (i, j)),
        out_shape=jax.ShapeDtypeStruct((m, n), out_dtype),
        scratch_shapes=[pltpu.VMEM((tm, tn), F32)] if use_acc else [],
        compiler_params=_params("parallel", "parallel", "arbitrary"),
    )(a, b)


def _mm_res_ln(a, w, res, g, b, name, tm=512):
    t, k = a.shape
    d = w.shape[1]
    tm = min(tm, t)

    def body(a_ref, w_ref, r_ref, g_ref, b_ref, y_ref, yb_ref, xh_ref, rs_ref):
        z = ALPHA * r_ref[...] + _dot(a_ref[...], w_ref[...])
        mu = jnp.mean(z, axis=-1, keepdims=True)
        zc = z - mu
        var = jnp.mean(zc * zc, axis=-1, keepdims=True)
        rstd = lax.rsqrt(var + LN_EPS)
        xh = zc * rstd
        y = xh * g_ref[...] + b_ref[...]
        y_ref[...] = y
        yb_ref[...] = y.astype(BF16)
        xh_ref[...] = xh
        rs_ref[...] = rstd

    row = lambda i: (i, 0)
    fix = lambda i: (0, 0)
    return pl.pallas_call(
        body,
        name=name,
        grid=(t // tm,),
        in_specs=[pl.BlockSpec((tm, k), row), pl.BlockSpec((k, d), fix), pl.BlockSpec((tm, d), row),
                  pl.BlockSpec((1, d), fix), pl.BlockSpec((1, d), fix)],
        out_specs=[pl.BlockSpec((tm, d), row), pl.BlockSpec((tm, d), row), pl.BlockSpec((tm, d), row),
                   pl.BlockSpec((tm, 1), row)],
        out_shape=[jax.ShapeDtypeStruct((t, d), F32), jax.ShapeDtypeStruct((t, d), BF16),
                   jax.ShapeDtypeStruct((t, d), F32), jax.ShapeDtypeStruct((t, 1), F32)],
        compiler_params=_params("parallel"),
    )(a, w, res, g.reshape(1, d), b.reshape(1, d))


def _ln_bwd(dys, coefs, xhat, rstd, g, name, tm=512):
    t, d = xhat.shape
    tm = min(tm, t)
    n_in = len(dys)

    def body(*refs):
        dy_refs = refs[:n_in]
        xh_ref, rs_ref, g_ref, dz_ref, dzb_ref, dg_ref, db_ref = refs[n_in:]
        dy = coefs[0] * dy_refs[0][...].astype(F32)
        for c, r in zip(coefs[1:], dy_refs[1:]):
            dy = dy + c * r[...].astype(F32)
        xh = xh_ref[...]
        dxh = dy * g_ref[...]
        m1 = jnp.mean(dxh, axis=-1, keepdims=True)
        m2 = jnp.mean(dxh * xh, axis=-1, keepdims=True)
        dz = rs_ref[...] * (dxh - m1 - xh * m2)
        dz_ref[...] = dz
        dzb_ref[...] = dz.astype(BF16)
        pg = jnp.sum(dy * xh, axis=0, keepdims=True)
        pb = jnp.sum(dy, axis=0, keepdims=True)

        @pl.when(pl.program_id(0) == 0)
        def _():
            dg_ref[...] = pg
            db_ref[...] = pb

        @pl.when(pl.program_id(0) > 0)
        def _():
            dg_ref[...] += pg
            db_ref[...] += pb

    row = lambda i: (i, 0)
    fix = lambda i: (0, 0)
    return pl.pallas_call(
        body,
        name=name,
        grid=(t // tm,),
        in_specs=[pl.BlockSpec((tm, d), row)] * n_in
        + [pl.BlockSpec((tm, d), row), pl.BlockSpec((tm, 1), row), pl.BlockSpec((1, d), fix)],
        out_specs=[pl.BlockSpec((tm, d), row), pl.BlockSpec((tm, d), row), pl.BlockSpec((1, d), fix),
                   pl.BlockSpec((1, d), fix)],
        out_shape=[jax.ShapeDtypeStruct((t, d), F32), jax.ShapeDtypeStruct((t, d), BF16),
                   jax.ShapeDtypeStruct((1, d), F32), jax.ShapeDtypeStruct((1, d), F32)],
        compiler_params=_params("arbitrary"),
    )(*dys, xhat, rstd, g.reshape(1, d))


def _loss_head(y, target, name="loss_head", tm=512):
    t, d = y.shape
    tm = min(tm, t)

    def body(y_ref, t_ref, dy_ref, l_ref):
        e = y_ref[...] - t_ref[...]
        dy_ref[...] = e * (1.0 / d)
        part = jnp.full((8, 128), 0.5 / d, F32) * jnp.sum(e * e)

        @pl.when(pl.program_id(0) == 0)
        def _():
            l_ref[...] = part

        @pl.when(pl.program_id(0) > 0)
        def _():
            l_ref[...] += part

    row = lambda i: (i, 0)
    return pl.pallas_call(
        body,
        name=name,
        grid=(t // tm,),
        in_specs=[pl.BlockSpec((tm, d), row), pl.BlockSpec((tm, d), row)],
        out_specs=[pl.BlockSpec((tm, d), row), pl.BlockSpec((8, 128), lambda i: (0, 0))],
        out_shape=[jax.ShapeDtypeStruct((t, d), F32), jax.ShapeDtypeStruct((8, 128), F32)],
        compiler_params=_params("arbitrary"),
    )(y, target)


def _swiglu_fwd(h, name, tm=512):
    t = h.shape[0]
    tm = min(tm, t)
    wb = FFN_HIDDEN // 2

    def body(u_ref, g_ref, a_ref):
        g = g_ref[...]
        a_ref[...] = (g * _sigmoid(g) * u_ref[...]).astype(BF16)

    return pl.pallas_call(
        body,
        name=name,
        grid=(t // tm, 2),
        in_specs=[pl.BlockSpec((tm, wb), lambda i, j: (i, j)), pl.BlockSpec((tm, wb), lambda i, j: (i, j + 2))],
        out_specs=pl.BlockSpec((tm, wb), lambda i, j: (i, j)),
        out_shape=jax.ShapeDtypeStruct((t, FFN_HIDDEN), BF16),
        compiler_params=_params("parallel", "parallel"),
    )(h, h)


def _swiglu_bwd(da, h, name, tm=512):
    t = h.shape[0]
    tm = min(tm, t)
    wb = FFN_HIDDEN // 2

    def body(da_ref, u_ref, g_ref, dh_ref):
        g = g_ref[...]
        sg = _sigmoid(g)
        da = da_ref[...]

        @pl.when(pl.program_id(1) < 2)
        def _():
            dh_ref[...] = (da * g * sg).astype(BF16)

        @pl.when(pl.program_id(1) >= 2)
        def _():
            dh_ref[...] = (da * u_ref[...] * (sg * (1.0 + g * (1.0 - sg)))).astype(BF16)

    return pl.pallas_call(
        body,
        name=name,
        grid=(t // tm, 4),
        in_specs=[pl.BlockSpec((tm, wb), lambda i, j: (i, j % 2)), pl.BlockSpec((tm, wb), lambda i, j: (i, j % 2)),
                  pl.BlockSpec((tm, wb), lambda i, j: (i, 2 + j % 2))],
        out_specs=pl.BlockSpec((tm, wb), lambda i, j: (i, j)),
        out_shape=jax.ShapeDtypeStruct((t, 2 * FFN_HIDDEN), BF16),
        compiler_params=_params("parallel", "parallel"),
    )(da, h, h)


def _merge_fwd(ya, yb, wpa, wpb, rest, name, tm=512):
    t = ya.shape[0]
    tm = min(tm, t)

    def body(ya_ref, yb_ref, wa_ref, wb_ref, ga_ref, gb_ref, o_ref):
        pa = _dot(ya_ref[...], wa_ref[...])
        pb = _dot(yb_ref[...], wb_ref[...])
        o_ref[...] = (_sigmoid(ga_ref[...]) * pa + _sigmoid(gb_ref[...]) * pb).astype(BF16)

    row = lambda i: (i, 0)
    fix = lambda i: (0, 0)
    return pl.pallas_call(
        body,
        name=name,
        grid=(t // tm,),
        in_specs=[pl.BlockSpec((tm, A_WIDTH), row), pl.BlockSpec((tm, B_WIDTH), row),
                  pl.BlockSpec((A_WIDTH, D_MODEL), fix), pl.BlockSpec((B_WIDTH, D_MODEL), fix),
                  pl.BlockSpec((tm, D_MODEL), lambda i: (i, 0)), pl.BlockSpec((tm, D_MODEL), lambda i: (i, 1))],
        out_specs=pl.BlockSpec((tm, D_MODEL), row),
        out_shape=jax.ShapeDtypeStruct((t, D_MODEL), BF16),
        compiler_params=_params("parallel"),
    )(ya, yb, wpa, wpb, rest, rest)


def _merge_bwd(dm, ya, yb, wpa, wpb, rest, name, tm=512):
    t = ya.shape[0]
    tm = min(tm, t)

    def body(dm_ref, ya_ref, yb_ref, wa_ref, wb_ref, ga_ref, gb_ref, dg_ref, dpa_ref, dpb_ref):
        dm_v = dm_ref[...]
        pa = _dot(ya_ref[...], wa_ref[...])
        pb = _dot(yb_ref[...], wb_ref[...])
        sa = _sigmoid(ga_ref[...])
        sb = _sigmoid(gb_ref[...])
        dg_ref[:, :D_MODEL] = (dm_v * pa * sa * (1.0 - sa)).astype(BF16)
        dg_ref[:, D_MODEL:] = (dm_v * pb * sb * (1.0 - sb)).astype(BF16)
        dpa_ref[...] = (dm_v * sa).astype(BF16)
        dpb_ref[...] = (dm_v * sb).astype(BF16)

    row = lambda i: (i, 0)
    fix = lambda i: (0, 0)
    return pl.pallas_call(
        body,
        name=name,
        grid=(t // tm,),
        in_specs=[pl.BlockSpec((tm, D_MODEL), row), pl.BlockSpec((tm, A_WIDTH), row), pl.BlockSpec((tm, B_WIDTH), row),
                  pl.BlockSpec((A_WIDTH, D_MODEL), fix), pl.BlockSpec((B_WIDTH, D_MODEL), fix),
                  pl.BlockSpec((tm, D_MODEL), lambda i: (i, 0)), pl.BlockSpec((tm, D_MODEL), lambda i: (i, 1))],
        out_specs=[pl.BlockSpec((tm, 2 * D_MODEL), row), pl.BlockSpec((tm, D_MODEL), row),
                   pl.BlockSpec((tm, D_MODEL), row)],
        out_shape=[jax.ShapeDtypeStruct((t, 2 * D_MODEL), BF16), jax.ShapeDtypeStruct((t, D_MODEL), BF16),
                   jax.ShapeDtypeStruct((t, D_MODEL), BF16)],
        compiler_params=_params("parallel"),
    )(dm, ya, yb, wpa, wpb, rest, rest)


FA_BLOCK = 4224 // 128 - 1


def _tri(n, lower):
    r = lax.broadcasted_iota(jnp.int32, (n, n), 0)
    c = lax.broadcasted_iota(jnp.int32, (n, n), 1)
    return jnp.where((r >= c) if lower else (r <= c), 1.0, 0.0).astype(F32)


def _fox_gate_fwd(rest, bf, name, tb=512):
    t = rest.shape[0]
    tb = min(tb, t)

    def body(fa_ref, bf_ref, f_ref, carry):
        @pl.when(pl.program_id(0) == 0)
        def _():
            carry[...] = jnp.zeros_like(carry)

        z = fa_ref[...] + bf_ref[...]
        logf = jnp.minimum(z, 0.0) - jnp.log(1.0 + jnp.exp(-jnp.abs(z)))
        f = _dot_hi(_tri(tb, True), logf) + carry[...]
        f_ref[...] = f
        carry[...] = f[tb - 1:tb, :]

    return pl.pallas_call(
        body,
        name=name,
        grid=(t // tb,),
        in_specs=[pl.BlockSpec((tb, 128), lambda i: (i, FA_BLOCK)), pl.BlockSpec((1, 128), lambda i: (0, 0))],
        out_specs=pl.BlockSpec((tb, 128), lambda i: (i, 0)),
        out_shape=jax.ShapeDtypeStruct((t, 128), F32),
        scratch_shapes=[pltpu.VMEM((1, 128), F32)],
        compiler_params=_params("arbitrary"),
    )(rest, bf)


def _fox_gate_bwd(d_f, rest, bf, name, tb=512):
    t = rest.shape[0]
    tb = min(tb, t)
    nb = t // tb

    def body(df_ref, fa_ref, bf_ref, dfa_ref, dbf_ref, carry):
        @pl.when(pl.program_id(0) == 0)
        def _():
            carry[...] = jnp.zeros_like(carry)

        dlogf = _dot_hi(_tri(tb, False), df_ref[...]) + carry[...]
        carry[...] = dlogf[0:1, :]
        z = fa_ref[...] + bf_ref[...]
        dz = dlogf * _sigmoid(-z)
        dfa_ref[...] = dz.astype(BF16)
        part = jnp.sum(dz, axis=0, keepdims=True)

        @pl.when(pl.program_id(0) == 0)
        def _():
            dbf_ref[...] = part

        @pl.when(pl.program_id(0) > 0)
        def _():
            dbf_ref[...] += part

    return pl.pallas_call(
        body,
        name=name,
        grid=(nb,),
        in_specs=[pl.BlockSpec((tb, 128), lambda i: (nb - 1 - i, 0)),
                  pl.BlockSpec((tb, 128), lambda i: (nb - 1 - i, FA_BLOCK)),
                  pl.BlockSpec((1, 128), lambda i: (0, 0))],
        out_specs=[pl.BlockSpec((tb, 128), lambda i: (nb - 1 - i, 0)), pl.BlockSpec((1, 128), lambda i: (0, 0))],
        out_shape=[jax.ShapeDtypeStruct((t, 128), BF16), jax.ShapeDtypeStruct((1, 128), F32)],
        scratch_shapes=[pltpu.VMEM((1, 128), F32)],
        compiler_params=_params("arbitrary"),
    )(d_f, rest, bf)


ATT_BLOCK = 512


def _head_mask(shape, j):
    lane = lax.broadcasted_iota(jnp.int32, shape, 1)
    return (lane < 64) if j == 0 else (lane >= 64)


def _fox_fwd(qkv, fcol, frow, name):
    t = qkv.shape[0]
    bq = min(ATT_BLOCK, t)
    nq = t // bq

    def body(q_ref, k_ref, v_ref, fc_ref, fr_ref, o_ref, lse_ref):
        i = pl.program_id(1)
        q = q_ref[...]
        outs = []
        for j in (0, 1):
            qj = jnp.where(_head_mask((bq, 128), j), q, jnp.zeros_like(q)) * 0.125
            fcj = fc_ref[:, 64 * j:64 * j + 1]

            def step(kb, carry, masked, qj=qj, fcj=fcj, j=j):
                m, l, acc = carry
                off = pl.multiple_of(kb * bq, bq)
                ks = k_ref[pl.ds(off, bq), :]
                vs = v_ref[pl.ds(off, bq), :]
                s = _dot(qj, ks, "nt") + fcj - fr_ref[j:j + 1, pl.ds(off, bq)]
                if masked:
                    r = lax.broadcasted_iota(jnp.int32, (bq, bq), 0)
                    c = lax.broadcasted_iota(jnp.int32, (bq, bq), 1)
                    s = jnp.where(r >= c, s, -jnp.inf)
                m_new = jnp.maximum(m, jnp.max(s, axis=1, keepdims=True))
                alpha = jnp.exp(m - m_new)
                p = jnp.exp(s - m_new)
                l = alpha * l + jnp.sum(p, axis=1, keepdims=True)
                acc = alpha * acc + _dot(p, vs)
                return m_new, l, acc

            carry = (jnp.full((bq, 1), -jnp.inf, F32), jnp.zeros((bq, 1), F32), jnp.zeros((bq, 128), F32))
            carry = lax.fori_loop(0, i, lambda kb, c, step=step: step(kb, c, False), carry)
            m, l, acc = step(i, carry, True)
            outs.append((acc / l, m + jnp.log(l)))
        msk = _head_mask((bq, 128), 0)
        o_ref[...] = jnp.where(msk, outs[0][0], outs[1][0]).astype(BF16)
        lse_ref[...] = jnp.where(msk, outs[0][1], outs[1][1])

    return pl.pallas_call(
        body,
        name=name,
        grid=(4, nq),
        in_specs=[pl.BlockSpec((bq, 128), lambda h, i: (i, h)),
                  pl.BlockSpec((t, 128), lambda h, i: (0, 4 + h)),
                  pl.BlockSpec((t, 128), lambda h, i: (0, 8 + h)),
                  pl.BlockSpec((bq, 128), lambda h, i: (i, h)),
                  pl.BlockSpec((None, 2, t), lambda h, i: (h, 0, 0))],
        out_specs=[pl.BlockSpec((bq, 128), lambda h, i: (i, h)), pl.BlockSpec((bq, 128), lambda h, i: (i, h))],
        out_shape=[jax.ShapeDtypeStruct((t, A_WIDTH), BF16), jax.ShapeDtypeStruct((t, A_WIDTH), F32)],
        compiler_params=_params("parallel", "parallel"),
    )(qkv, qkv, qkv, fcol, frow)


def _fox_bwd_dq(qkv, do, o, lse, fcol, frow, name):
    t = qkv.shape[0]
    bq = min(ATT_BLOCK, t)
    nq = t // bq

    def body(q_ref, k_ref, v_ref, do_ref, o_ref, lse_ref, fc_ref, fr_ref, dq_ref, dl_ref, rs_ref):
        i = pl.program_id(1)
        q = q_ref[...]
        do_v = do_ref[...]
        prod = do_v.astype(F32) * o_ref[...].astype(F32)
        outs = []
        for j in (0, 1):
            hm = _head_mask((bq, 128), j)
            qj = jnp.where(hm, q, jnp.zeros_like(q)) * 0.125
            doj = jnp.where(hm, do_v, jnp.zeros_like(do_v))
            dj = jnp.sum(jnp.where(hm, prod, 0.0), axis=1, keepdims=True)
            fcj = fc_ref[:, 64 * j:64 * j + 1]
            lsej = lse_ref[:, 64 * j:64 * j + 1]

            def step(kb, carry, masked, qj=qj, doj=doj, dj=dj, fcj=fcj, lsej=lsej, j=j):
                acc, rs = carry
                off = pl.multiple_of(kb * bq, bq)
                ks = k_ref[pl.ds(off, bq), :]
                vs = v_ref[pl.ds(off, bq), :]
                s = _dot(qj, ks, "nt") + fcj - fr_ref[j:j + 1, pl.ds(off, bq)]
                if masked:
                    r = lax.broadcasted_iota(jnp.int32, (bq, bq), 0)
                    c = lax.broadcasted_iota(jnp.int32, (bq, bq), 1)
                    s = jnp.where(r >= c, s, -jnp.inf)
                p = jnp.exp(s - lsej)
                dp = _dot(doj, vs, "nt")
                ds = p * (dp - dj)
                return acc + _dot(ds, ks), rs + jnp.sum(ds, axis=1, keepdims=True)

            carry = (jnp.zeros((bq, 128), F32), jnp.zeros((bq, 1), F32))
            carry = lax.fori_loop(0, i, lambda kb, c, step=step: step(kb, c, False), carry)
            acc, rs = step(i, carry, True)
            outs.append((acc * 0.125, dj, rs))
        msk = _head_mask((bq, 128), 0)
        dq_ref[...] = jnp.where(msk, outs[0][0], outs[1][0]).astype(BF16)
        dl_ref[...] = jnp.where(msk, outs[0][1], outs[1][1])
        rs_ref[...] = jnp.where(msk, outs[0][2], outs[1][2])

    blk = pl.BlockSpec((bq, 128), lambda h, i: (i, h))
    return pl.pallas_call(
        body,
        name=name,
        grid=(4, nq),
        in_specs=[blk, pl.BlockSpec((t, 128), lambda h, i: (0, 4 + h)), pl.BlockSpec((t, 128), lambda h, i: (0, 8 + h)),
                  blk, blk, blk, blk, pl.BlockSpec((None, 2, t), lambda h, i: (h, 0, 0))],
        out_specs=[blk, blk, blk],
        out_shape=[jax.ShapeDtypeStruct((t, A_WIDTH), BF16), jax.ShapeDtypeStruct((t, A_WIDTH), F32),
                   jax.ShapeDtypeStruct((t, A_WIDTH), F32)],
        compiler_params=_params("parallel", "parallel"),
    )(qkv, qkv, qkv, do, o, lse, fcol, frow)


def _fox_bwd_dkv(qkv, do, fcol, frow, lse_row, delta_row, name):
    t = qkv.shape[0]
    bk = min(ATT_BLOCK, t)
    nk = t // bk

    def body(k_ref, v_ref, fc_ref, q_ref, do_ref, fr_ref, lr_ref, dr_ref, dk_ref, dv_ref, cs_ref):
        jb = pl.program_id(1)
        k = k_ref[...]
        v = v_ref[...]
        outs = []
        for j in (0, 1):
            hm = _head_mask((bk, 128), j)
            kj = jnp.where(hm, k, jnp.zeros_like(k)) * 0.125
            vj = jnp.where(hm, v, jnp.zeros_like(v))
            fck = fc_ref[:, 64 * j:64 * j + 1]

            def step(ib, carry, masked, kj=kj, vj=vj, fck=fck, j=j):
                dk_acc, dv_acc, cs = carry
                off = pl.multiple_of(ib * bk, bk)
                qs = q_ref[pl.ds(off, bk), :]
                dos = do_ref[pl.ds(off, bk), :]
                st = _dot(kj, qs, "nt") + fr_ref[j:j + 1, pl.ds(off, bk)] - fck
                if masked:
                    r = lax.broadcasted_iota(jnp.int32, (bk, bk), 0)
                    c = lax.broadcasted_iota(jnp.int32, (bk, bk), 1)
                    st = jnp.where(c >= r, st, -jnp.inf)
                pt = jnp.exp(st - lr_ref[j:j + 1, pl.ds(off, bk)])
                dv_acc = dv_acc + _dot(pt, dos)
                dpt = _dot(vj, dos, "nt")
                dst = pt * (dpt - dr_ref[j:j + 1, pl.ds(off, bk)])
                dk_acc = dk_acc + _dot(dst, qs)
                return dk_acc, dv_acc, cs + jnp.sum(dst, axis=1, keepdims=True)

            carry = (jnp.zeros((bk, 128), F32), jnp.zeros((bk, 128), F32), jnp.zeros((bk, 1), F32))
            carry = step(jb, carry, True)
            dk_acc, dv_acc, cs = lax.fori_loop(jb + 1, nk, lambda ib, c, step=step: step(ib, c, False), carry)
            outs.append((dk_acc * 0.125, dv_acc, cs))
        msk = _head_mask((bk, 128), 0)
        dk_ref[...] = jnp.where(msk, outs[0][0], outs[1][0]).astype(BF16)
        dv_ref[...] = jnp.where(msk, outs[0][1], outs[1][1]).astype(BF16)
        cs_ref[...] = jnp.where(msk, outs[0][2], outs[1][2])

    row3 = pl.BlockSpec((None, 2, t), lambda h, i: (h, 0, 0))
    return pl.pallas_call(
        body,
        name=name,
        grid=(4, nk),
        in_specs=[pl.BlockSpec((bk, 128), lambda h, i: (i, 4 + h)), pl.BlockSpec((bk, 128), lambda h, i: (i, 8 + h)),
                  pl.BlockSpec((bk, 128), lambda h, i: (i, h)),
                  pl.BlockSpec((t, 128), lambda h, i: (0, h)), pl.BlockSpec((t, 128), lambda h, i: (0, h)),
                  row3, row3, row3],
        out_specs=[pl.BlockSpec((bk, 128), lambda h, i: (i, h))] * 3,
        out_shape=[jax.ShapeDtypeStruct((t, A_WIDTH), BF16), jax.ShapeDtypeStruct((t, A_WIDTH), BF16),
                   jax.ShapeDtypeStruct((t, A_WIDTH), F32)],
        compiler_params=_params("parallel", "parallel"),
    )(qkv, qkv, fcol, qkv, do, frow, lse_row, delta_row)


HG_ROWS = 256


def _hg_gates(hb_ref, rows, h, lbh):
    qb = hb_ref[rows, h * HD:(h + 1) * HD]
    fb = hb_ref[rows, B_WIDTH + h * HD:B_WIDTH + (h + 1) * HD]
    v = hb_ref[rows, 2 * B_WIDTH + h * HD:2 * B_WIDTH + (h + 1) * HD]
    gb = hb_ref[rows, 3 * B_WIDTH + h * HD:3 * B_WIDTH + (h + 1) * HD]
    sg = _sigmoid(fb)
    f = lbh + (1.0 - lbh) * sg
    sq = _sigmoid(qb)
    return qb, sq, qb * sq, sg, f, 1.0 - f, jnp.log(f), v, gb


def _hg_intra_factors(q, k, b):
    fac = []
    for i in range(CHUNK // SUB):
        bi = b[SUB * i:SUB * i + 1, :]
        eq = jnp.exp(b[SUB * i:SUB * (i + 1), :] - bi)
        ek = jnp.exp(jnp.minimum(bi - b, EXP_CLAMP))
        fac.append((eq, ek, q[SUB * i:SUB * (i + 1), :] * eq, k * ek))
    return fac


def _causal(n):
    r = lax.broadcasted_iota(jnp.int32, (n, n), 0)
    c = lax.broadcasted_iota(jnp.int32, (n, n), 1)
    return r >= c


def _hgrn_fwd(rest, lb, ng, name):
    t = rest.shape[0]
    bt = min(HG_ROWS, t)
    ncb = bt // CHUNK

    def body(hb_ref, lb_ref, ng_ref, y_ref, o_ref, st_ref, s_scr):
        @pl.when(pl.program_id(0) == 0)
        def _():
            s_scr[...] = jnp.zeros_like(s_scr)

        tril = _tri(CHUNK, True)
        causal = _causal(CHUNK)
        ones = jnp.ones((CHUNK, HD), F32)

        def chunk(c, carry):
            rows = pl.ds(pl.multiple_of(c * CHUNK, CHUNK), CHUNK)
            for h in range(B_HEADS):
                lbh = lb_ref[:, h * HD:(h + 1) * HD]
                _, _, q, _, _, k, g, v, gb = _hg_gates(hb_ref, rows, h, lbh)
                b = _dot_hi(tril, g)
                s0 = s_scr[h]
                st_ref[c, h] = s0
                o = _hdot(q * jnp.exp(b), s0)
                a = jnp.concatenate([_hdot(qe, ke, "nt") for _, _, qe, ke in _hg_intra_factors(q, k, b)], axis=0)
                o = o + _hdot(jnp.where(causal, a, 0.0), v)
                blast = b[CHUNK - 1:CHUNK, :]
                kd = k * jnp.exp(blast - b)
                eb = jnp.exp(_dot_hi(g, ones, "tn"))
                s_scr[h] = eb * s0 + _hdot(kd, v, "tn")
                r = lax.rsqrt(jnp.mean(o * o, axis=-1, keepdims=True) + RMS_EPS)
                o_ref[rows, h * HD:(h + 1) * HD] = o
                y_ref[rows, h * HD:(h + 1) * HD] = (o * r * ng_ref[...] * _sigmoid(gb)).astype(BF16)
            return carry

        lax.fori_loop(0, ncb, chunk, 0)

    return pl.pallas_call(
        body,
        name=name,
        grid=(t // bt,),
        in_specs=[pl.BlockSpec((bt, 4 * B_WIDTH), lambda i: (i, 1)), pl.BlockSpec((1, B_WIDTH), lambda i: (0, 0)),
                  pl.BlockSpec((1, HD), lambda i: (0, 0))],
        out_specs=[pl.BlockSpec((bt, B_WIDTH), lambda i: (i, 0)), pl.BlockSpec((bt, B_WIDTH), lambda i: (i, 0)),
                   pl.BlockSpec((ncb, B_HEADS, HD, HD), lambda i: (i, 0, 0, 0))],
        out_shape=[jax.ShapeDtypeStruct((t, B_WIDTH), BF16), jax.ShapeDtypeStruct((t, B_WIDTH), F32),
                   jax.ShapeDtypeStruct((t // CHUNK, B_HEADS, HD, HD), F32)],
        scratch_shapes=[pltpu.VMEM((B_HEADS, HD, HD), F32)],
        compiler_params=_params("arbitrary"),
    )(rest, lb, ng)


def _hgrn_bwd(dy, rest, o_saved, states, lb, ng, name):
    t = rest.shape[0]
    bt = min(HG_ROWS, t)
    ncb = bt // CHUNK
    nb = t // bt

    def body(dy_ref, hb_ref, o_ref, st_ref, lb_ref, ng_ref, dh_ref, dlb_ref, dng_ref, ds_scr):
        @pl.when(pl.program_id(0) == 0)
        def _():
            ds_scr[...] = jnp.zeros_like(ds_scr)
            dlb_ref[...] = jnp.zeros_like(dlb_ref)
            dng_ref[...] = jnp.zeros_like(dng_ref)

        tril = _tri(CHUNK, True)
        triu = _tri(CHUNK, False)
        causal = _causal(CHUNK)
        ones = jnp.ones((CHUNK, HD), F32)
        ones8 = jnp.ones((8, HD), F32)
        last_row = lax.broadcasted_iota(jnp.int32, (CHUNK, HD), 0) == CHUNK - 1

        def chunk(cc, carry):
            c = ncb - 1 - cc
            rows = pl.ds(pl.multiple_of(c * CHUNK, CHUNK), CHUNK)
            for h in range(B_HEADS):
                cols = slice(h * HD, (h + 1) * HD)
                lbh = lb_ref[:, cols]
                qb, sq, q, sg, f, k, g, v, gb = _hg_gates(hb_ref, rows, h, lbh)
                o = o_ref[rows, cols]
                dyv = dy_ref[rows, cols].astype(F32)
                ngv = ng_ref[...]
                r = lax.rsqrt(jnp.mean(o * o, axis=-1, keepdims=True) + RMS_EPS)
                sgb = _sigmoid(gb)
                don = dyv * sgb
                dgb = dyv * (o * r * ngv) * sgb * (1.0 - sgb)
                dng_ref[...] += jnp.sum(don * o * r, axis=0, keepdims=True)
                doh = don * ngv
                do = r * (doh - o * (r * r) * jnp.mean(doh * o, axis=-1, keepdims=True))
                b = _dot_hi(tril, g)
                ebt = jnp.exp(b)
                s0 = st_ref[c, h]
                ds1 = ds_scr[h]
                blast = b[CHUNK - 1:CHUNK, :]
                ekd = jnp.exp(blast - b)
                kd = k * ekd
                eb = jnp.exp(_dot_hi(g, ones, "tn"))
                fac = _hg_intra_factors(q, k, b)
                a = jnp.concatenate([_hdot(qe, ke, "nt") for _, _, qe, ke in fac], axis=0)
                a = jnp.where(causal, a, 0.0)
                da = jnp.where(causal, _hdot(do, v, "nt"), 0.0)
                dv = _hdot(a, do, "tn") + _hdot(kd, ds1)
                dq = ebt * _hdot(do, s0, "nt")
                dq = dq + jnp.concatenate(
                    [eq * _hdot(da[SUB * i:SUB * (i + 1), :], ke) for i, (eq, _, _, ke) in enumerate(fac)], axis=0)
                dk_state = ekd * _hdot(v, ds1, "nt")
                dk = dk_state
                for i, (_, ek, qe, _) in enumerate(fac):
                    dk = dk + ek * _hdot(da[SUB * i:SUB * (i + 1), :], qe, "tn")
                ds_scr[h] = _hdot(q * ebt, do, "tn") + eb * ds1
                extra = jnp.exp(blast) * _dot_hi(ones8, ds1 * s0, "nt")[0:1, :] \
                    + jnp.sum(k * dk_state, axis=0, keepdims=True)
                db = q * dq - k * dk + jnp.where(last_row, extra, 0.0)
                dg = _dot_hi(triu, db)
                df = dg / f - dk
                dlb_ref[:, cols] += jnp.sum(df * (1.0 - sg), axis=0, keepdims=True)
                dfb = df * (1.0 - lbh) * sg * (1.0 - sg)
                dqb = dq * (sq * (1.0 + qb * (1.0 - sq)))
                dh_ref[rows, h * HD:(h + 1) * HD] = dqb.astype(BF16)
                dh_ref[rows, B_WIDTH + h * HD:B_WIDTH + (h + 1) * HD] = dfb.astype(BF16)
                dh_ref[rows, 2 * B_WIDTH + h * HD:2 * B_WIDTH + (h + 1) * HD] = dv.astype(BF16)
                dh_ref[rows, 3 * B_WIDTH + h * HD:3 * B_WIDTH + (h + 1) * HD] = dgb.astype(BF16)
            return carry

        lax.fori_loop(0, ncb, chunk, 0)

    rev = lambda i: (nb - 1 - i, 0)
    return pl.pallas_call(
        body,
        name=name,
        grid=(nb,),
        in_specs=[pl.BlockSpec((bt, B_WIDTH), rev), pl.BlockSpec((bt, 4 * B_WIDTH), lambda i: (nb - 1 - i, 1)),
                  pl.BlockSpec((bt, B_WIDTH), rev),
                  pl.BlockSpec((ncb, B_HEADS, HD, HD), lambda i: (nb - 1 - i, 0, 0, 0)),
                  pl.BlockSpec((1, B_WIDTH), lambda i: (0, 0)), pl.BlockSpec((1, HD), lambda i: (0, 0))],
        out_specs=[pl.BlockSpec((bt, 4 * B_WIDTH), rev), pl.BlockSpec((1, B_WIDTH), lambda i: (0, 0)),
                   pl.BlockSpec((1, HD), lambda i: (0, 0))],
        out_shape=[jax.ShapeDtypeStruct((t, 4 * B_WIDTH), BF16), jax.ShapeDtypeStruct((1, B_WIDTH), F32),
                   jax.ShapeDtypeStruct((1, HD), F32)],
        scratch_shapes=[pltpu.VMEM((B_HEADS, HD, HD), F32)],
        compiler_params=_params("arbitrary"),
    )(dy, rest, o_saved, states, lb, ng)


def _axpy2(c0, a0, c1, a1, name, tm=512):
    t, d = a0.shape
    tm = min(tm, t)

    def body(a_ref, b_ref, o_ref):
        o_ref[...] = c0 * a_ref[...] + c1 * b_ref[...]

    row = lambda i: (i, 0)
    return pl.pallas_call(
        body, name=name, grid=(t // tm,),
        in_specs=[pl.BlockSpec((tm, d), row), pl.BlockSpec((tm, d), row)],
        out_specs=pl.BlockSpec((tm, d), row),
        out_shape=jax.ShapeDtypeStruct((t, d), F32),
        compiler_params=_params("parallel"),
    )(a0, a1)


def _split_w_in(w_in_l):
    wqkv = w_in_l[:, :3 * A_WIDTH]
    wfa = jnp.pad(w_in_l[:, 3 * A_WIDTH:3 * A_WIDTH + A_HEADS], ((0, 0), (0, 128 - A_HEADS)))
    whb = w_in_l[:, 3 * A_WIDTH + A_HEADS:3 * A_WIDTH + A_HEADS + 4 * B_WIDTH]
    wgt = w_in_l[:, 3 * A_WIDTH + A_HEADS + 4 * B_WIDTH:]
    return wqkv, jnp.concatenate([wgt, whb, wfa], axis=1)


def _merge_w_in_grad(dwall):
    o = 3 * A_WIDTH
    return jnp.concatenate([dwall[:, :o], dwall[:, o + 4096:o + 4096 + A_HEADS], dwall[:, o + 2048:o + 4096],
                            dwall[:, o:o + 2048]], axis=1)


def _rows_of_heads(a):
    t = a.shape[0]
    return a[:, ::64].T.reshape(4, 2, t)


def _layer_fwd(x, xb, w, sp, l):
    t = x.shape[0]
    n = f"l{l}_"
    wqkv, wrest = _split_w_in(w["w_in"])
    qkv = _matmul(xb, wqkv, "nn", BF16, 512, 768, D_MODEL, n + "proj_qkv")
    rest = _matmul(xb, wrest, "nn", F32, 512, 1408, D_MODEL, n + "proj_rest")
    bf = jnp.pad(sp["b_fgate"], (0, 128 - A_HEADS)).reshape(1, 128)
    fcum = _fox_gate_fwd(rest, bf, n + "fox_gate_fwd")
    fcol = jnp.repeat(fcum[:, :A_HEADS], 64, axis=1)
    frow = fcum[:, :A_HEADS].T.reshape(4, 2, t)
    ya, lse = _fox_fwd(qkv, fcol, frow, n + "fox_fwd")
    lb = sp["lb"].reshape(1, B_WIDTH)
    ng = sp["norm_g"].reshape(1, HD)
    yb, ob, states = _hgrn_fwd(rest, lb, ng, n + "hgrn_fwd")
    merged = _merge_fwd(ya, yb, w["w_pa"], w["w_pb"], rest, n + "merge_fwd")
    x1, x1b, xh1, rs1 = _mm_res_ln(merged, w["w_out"], x, sp["ln1_g"], sp["ln1_b"], n + "out_ln1")
    h = _matmul(x1b, w["w_ff_in"], "nn", F32, 512, 1408, D_MODEL, n + "ffn_in")
    a = _swiglu_fwd(h, n + "swiglu_fwd")
    x2, x2b, xh2, rs2 = _mm_res_ln(a, w["w_ff_out"], x1, sp["ln2_g"], sp["ln2_b"], n + "ffn_out_ln2")
    saved = dict(xb=xb, wqkv=wqkv, wrest=wrest, qkv=qkv, rest=rest, bf=bf, fcol=fcol, frow=frow, ya=ya, lse=lse,
                 lb=lb, ng=ng, yb=yb, ob=ob, states=states, merged=merged, x1b=x1b, xh1=xh1, rs1=rs1, h=h, a=a,
                 xh2=xh2, rs2=rs2)
    return x2, x2b, saved


def _layer_bwd(dys, coefs, w, sp, s, l):
    n = f"l{l}_"
    dz2, dz2b, dg2, db2 = _ln_bwd(dys, coefs, s["xh2"], s["rs2"], sp["ln2_g"], n + "ln2_bwd")
    da = _matmul(dz2b, w["w_ff_out"], "nt", F32, 512, 1408, D_MODEL, n + "ffn_out_dx")
    d_wffout = _matmul(s["a"], dz2b, "tn", F32, 1408, 1024, 512, n + "ffn_out_dw")
    dh = _swiglu_bwd(da, s["h"], n + "swiglu_bwd")
    dx1f = _matmul(dh, w["w_ff_in"], "nt", F32, 512, 1024, 1408, n + "ffn_in_dx")
    d_wffin = _matmul(s["x1b"], dh, "tn", F32, 1024, 1408, 512, n + "ffn_in_dw")
    dz1, dz1b, dg1, db1 = _ln_bwd([dz2, dx1f], [ALPHA, 1.0], s["xh1"], s["rs1"], sp["ln1_g"], n + "ln1_bwd")
    dmerged = _matmul(dz1b, w["w_out"], "nt", F32, 512, 1024, D_MODEL, n + "out_dx")
    d_wout = _matmul(s["merged"], dz1b, "tn", F32, 1024, 1024, 512, n + "out_dw")
    dgates, dpa, dpb = _merge_bwd(dmerged, s["ya"], s["yb"], w["w_pa"], w["w_pb"], s["rest"], n + "merge_bwd")
    dya = _matmul(dpa, w["w_pa"], "nt", BF16, 512, 512, D_MODEL, n + "pa_dx")
    d_wpa = _matmul(s["ya"], dpa, "tn", F32, 512, 1024, 512, n + "pa_dw")
    dyb = _matmul(dpb, w["w_pb"], "nt", F32, 512, 512, D_MODEL, n + "pb_dx")
    d_wpb = _matmul(s["yb"], dpb, "tn", F32, 512, 1024, 512, n + "pb_dw")
    dq, delta, rsum = _fox_bwd_dq(s["qkv"], dya, s["ya"], s["lse"], s["fcol"], s["frow"], n + "fox_bwd_dq")
    dk, dv, csum = _fox_bwd_dkv(s["qkv"], dya, s["fcol"], s["frow"], _rows_of_heads(s["lse"]), _rows_of_heads(delta),
                                n + "fox_bwd_dkv")
    d_fcum = jnp.pad(rsum[:, ::64] - csum[:, ::64], ((0, 0), (0, 128 - A_HEADS)))
    dfa, dbf = _fox_gate_bwd(d_fcum, s["rest"], s["bf"], n + "fox_gate_bwd")
    dhb, dlb, dng = _hgrn_bwd(dyb, s["rest"], s["ob"], s["states"], s["lb"], s["ng"], n + "hgrn_bwd")
    dproj = jnp.concatenate([dq, dk, dv, dgates, dhb, dfa], axis=1)
    wall = jnp.concatenate([s["wqkv"], s["wrest"]], axis=1)
    dxp = _matmul(dproj, wall, "nt", F32, 512, 1024, 1920, n + "proj_dx")
    d_wall = _matmul(s["xb"], dproj, "tn", F32, 1024, 1152, 512, n + "proj_dw")
    grads = dict(w_in=_merge_w_in_grad(d_wall), w_pa=d_wpa, w_pb=d_wpb, w_out=d_wout, w_ff_in=d_wffin,
                 w_ff_out=d_wffout, b_fgate=dbf[0, :A_HEADS], lb=dlb[0], norm_g=dng[0], ln1_g=dg1[0], ln1_b=db1[0],
                 ln2_g=dg2[0], ln2_b=db2[0])
    return [dz1, dxp], [ALPHA, 1.0], grads


def _lower_bounds(logits):
    sm = jax.nn.softmax(logits.astype(F32), axis=0)
    return jnp.cumsum(sm, axis=0) - sm[0:1]


def _local_step(x, target, wfull, small):
    lbs, lb_vjp = jax.vjp(_lower_bounds, small["hgrn_lb_logits"])
    h, hb = x, x.astype(BF16)
    saved, sps = [], []
    for l in range(DEPTH):
        sp = dict(b_fgate=small["b_fgate"][l], lb=lbs[l], norm_g=small["hgrn_norm_g"][l], ln1_g=small["ln1_g"][l],
                  ln1_b=small["ln1_b"][l], ln2_g=small["ln2_g"][l], ln2_b=small["ln2_b"][l])
        h, hb, s = _layer_fwd(h, hb, wfull[l], sp, l)
        saved.append(s)
        sps.append(sp)
    dy, lpart = _loss_head(h, target)
    dys, coefs = [dy], [1.0]
    grads = [None] * DEPTH
    for l in reversed(range(DEPTH)):
        dys, coefs, grads[l] = _layer_bwd(dys, coefs, wfull[l], sps[l], saved[l], l)
    grad_x = _axpy2(coefs[0], dys[0], coefs[1], dys[1], "grad_x")
    d_logits = lb_vjp(jnp.stack([grads[l]["lb"] for l in range(DEPTH)]))[0]
    return lpart[0, 0], grad_x, grads, d_logits


_BIG = [("w_in", "w_in", (D_MODEL, IN_TOTAL), 1), ("w_branch_a", "w_pa", (A_WIDTH, D_MODEL), 1),
        ("w_branch_b", "w_pb", (B_WIDTH, D_MODEL), 1), ("w_out", "w_out", (D_MODEL, D_MODEL), 0),
        ("w_ff_in", "w_ff_in", (D_MODEL, 2 * FFN_HIDDEN), 1), ("w_ff_out", "w_ff_out", (FFN_HIDDEN, D_MODEL), 0)]
_SMALL = [("b_fgate", A_HEADS), ("hgrn_lb_logits", B_WIDTH), ("hgrn_norm_g", HD), ("ln1_g", D_MODEL),
          ("ln1_b", D_MODEL), ("ln2_g", D_MODEL), ("ln2_b", D_MODEL)]
PACK_ROWS = 32768
SMALL_ROWS = 80


def _shard_shape(shape, axis):
    return tuple(s // N_CHIPS if a == axis else s for a, s in enumerate(shape))


def _pack_big(per_name):
    flat = jnp.concatenate([per_name[name].reshape(-1, 128) for name, _, _, _ in _BIG], axis=0)
    return jnp.pad(flat, ((0, PACK_ROWS - flat.shape[0]), (0, 0)))


def _unpack_big(slab):
    out, r = {}, 0
    for name, _, shape, axis in _BIG:
        shp = _shard_shape(shape, axis)
        n = shp[0] * shp[1] // 128
        out[name] = slab[r:r + n].reshape(shp)
        r += n
    return out


def _pack_small(per_name):
    flat = jnp.concatenate([per_name[name].reshape(-1) for name, _ in _SMALL])
    return jnp.pad(flat, (0, SMALL_ROWS * 128 - flat.shape[0])).reshape(SMALL_ROWS, 128)


def _unpack_small(slab):
    flat, out, r = slab.reshape(-1), {}, 0
    for name, n in _SMALL:
        out[name] = flat[r:r + DEPTH * n].reshape(DEPTH, n)
        r += DEPTH * n
    return out


_ANY = pl.BlockSpec(memory_space=pl.ANY)


def _place():
    return lax.axis_index("x"), lax.axis_index("y"), lax.axis_index("c")


def _other_chips(x, y):
    return [(1 - x, y), (x, 1 - y), (1 - x, 1 - y)]


def _chip_exchange(mine_of, out_ref, send_sems, recv_sems, local_sem):
    x, y, c = _place()
    q = 2 * x + y
    local = pltpu.make_async_copy(mine_of(q), out_ref.at[q], local_sem)
    local.start()
    sends = []
    for k, (px, py) in enumerate(_other_chips(x, y)):
        cp = pltpu.make_async_remote_copy(src_ref=mine_of(2 * px + py), dst_ref=out_ref.at[q], send_sem=send_sems.at[k],
                                          recv_sem=recv_sems.at[k], device_id=(px, py, c), device_id_type=MESH)
        cp.start()
        sends.append(cp)
    for k, (px, py) in enumerate(_other_chips(x, y)):
        pltpu.make_async_remote_copy(src_ref=mine_of(q), dst_ref=out_ref.at[2 * px + py], send_sem=send_sems.at[k],
                                     recv_sem=recv_sems.at[k], device_id=(px, py, c), device_id_type=MESH).wait_recv()
    for cp in sends:
        cp.wait_send()
    local.wait()


def _gather_weights(mine):
    def body(in_ref, out_ref, send_sems, recv_sems, local_sem):
        _chip_exchange(lambda q: in_ref, out_ref, send_sems, recv_sems, local_sem)

    return pl.pallas_call(
        body, name="gather_weights", in_specs=[_ANY], out_specs=_ANY,
        out_shape=jax.ShapeDtypeStruct((N_CHIPS,) + mine.shape, mine.dtype),
        scratch_shapes=[pltpu.SemaphoreType.DMA((3,)), pltpu.SemaphoreType.DMA((3,)), pltpu.SemaphoreType.DMA(())],
    )(mine)


def _pair_exchange(g):
    def body(g_ref, a_ref, send_sem, recv_sem):
        x, y, c = _place()
        cp = pltpu.make_async_remote_copy(src_ref=g_ref.at[1 - c], dst_ref=a_ref, send_sem=send_sem, recv_sem=recv_sem,
                                          device_id=(x, y, 1 - c), device_id_type=MESH)
        cp.start()
        cp.wait()

    return pl.pallas_call(
        body, name="grad_pair_exchange", in_specs=[_ANY], out_specs=_ANY,
        out_shape=jax.ShapeDtypeStruct(g.shape[1:], g.dtype),
        scratch_shapes=[pltpu.SemaphoreType.DMA(()), pltpu.SemaphoreType.DMA(())],
    )(g)


def _pair_sum(g, a, layer, tb=2048):
    rows = a.shape[0] * a.shape[1]
    g2 = g.reshape(DEPTH, rows, 128)
    a2 = a.reshape(rows, 128)

    def body(l_ref, g_ref, a_ref, o_ref):
        o_ref[...] = g_ref[...] + a_ref[...]

    out = pl.pallas_call(
        body, name="grad_pair_sum",
        grid_spec=pltpu.PrefetchScalarGridSpec(
            num_scalar_prefetch=1, grid=(rows // tb,),
            in_specs=[pl.BlockSpec((None, tb, 128), lambda i, l_ref: (l_ref[0], i, 0)),
                      pl.BlockSpec((tb, 128), lambda i, l_ref: (i, 0))],
            out_specs=pl.BlockSpec((tb, 128), lambda i, l_ref: (i, 0))),
        out_shape=jax.ShapeDtypeStruct((rows, 128), F32),
        compiler_params=_params("parallel"),
    )(layer.reshape(1).astype(jnp.int32), g2, a2)
    return out.reshape(a.shape)


def _shard_exchange(p):
    def body(p_ref, b_ref, send_sems, recv_sems, local_sem):
        _chip_exchange(lambda q: p_ref.at[q], b_ref, send_sems, recv_sems, local_sem)

    return pl.pallas_call(
        body, name="grad_shard_exchange", in_specs=[_ANY], out_specs=_ANY,
        out_shape=jax.ShapeDtypeStruct(p.shape, p.dtype),
        scratch_shapes=[pltpu.SemaphoreType.DMA((3,)), pltpu.SemaphoreType.DMA((3,)), pltpu.SemaphoreType.DMA(())],
    )(p)


def _sum4(b, tb=2048):
    rows = b.shape[1]

    def body(b_ref, o_ref):
        o_ref[...] = ((b_ref[0] + b_ref[1]) + b_ref[2]) + b_ref[3]

    return pl.pallas_call(
        body, name="grad_chip_sum", grid=(rows // tb,),
        in_specs=[pl.BlockSpec((N_CHIPS, tb, 128), lambda i: (0, i, 0))],
        out_specs=pl.BlockSpec((tb, 128), lambda i: (i, 0)),
        out_shape=jax.ShapeDtypeStruct((rows, 128), F32),
        compiler_params=_params("parallel"),
    )(b)


def _result_exchange(gc):
    def body(g_ref, o_ref, send_sem, recv_sem, local_sem):
        x, y, c = _place()
        local = pltpu.make_async_copy(g_ref, o_ref.at[c], local_sem)
        local.start()
        cp = pltpu.make_async_remote_copy(src_ref=g_ref, dst_ref=o_ref.at[c], send_sem=send_sem, recv_sem=recv_sem,
                                          device_id=(x, y, 1 - c), device_id_type=MESH)
        cp.start()
        pltpu.make_async_remote_copy(src_ref=g_ref, dst_ref=o_ref.at[1 - c], send_sem=send_sem, recv_sem=recv_sem,
                                     device_id=(x, y, 1 - c), device_id_type=MESH).wait_recv()
        cp.wait_send()
        local.wait()

    return pl.pallas_call(
        body, name="grad_result_exchange", in_specs=[_ANY], out_specs=_ANY,
        out_shape=jax.ShapeDtypeStruct((DEPTH,) + gc.shape, gc.dtype),
        scratch_shapes=[pltpu.SemaphoreType.DMA(()), pltpu.SemaphoreType.DMA(()), pltpu.SemaphoreType.DMA(())],
    )(gc)


def _allreduce_small(v):
    def body(v_ref, o_ref, buf, send_sems, recv_sems):
        x, y, c = _place()
        me = 4 * x + 2 * y + c
        buf[me] = v_ref[...]
        peers = []
        for k in range(1, N_DEV):
            px = 1 - x if k & 4 else x
            py = 1 - y if k & 2 else y
            pc = 1 - c if k & 1 else c
            peers.append((px, py, pc))
        sends = []
        for k, peer in enumerate(peers):
            cp = pltpu.make_async_remote_copy(src_ref=v_ref, dst_ref=buf.at[me], send_sem=send_sems.at[k],
                                              recv_sem=recv_sems.at[k], device_id=peer, device_id_type=MESH)
            cp.start()
            sends.append(cp)
        for k, (px, py, pc) in enumerate(peers):
            pltpu.make_async_remote_copy(src_ref=v_ref, dst_ref=buf.at[4 * px + 2 * py + pc], send_sem=send_sems.at[k],
                                         recv_sem=recv_sems.at[k], device_id=(px, py, pc),
                                         device_id_type=MESH).wait_recv()
        for cp in sends:
            cp.wait_send()
        acc = buf[0]
        for i in range(1, N_DEV):
            acc = acc + buf[i]
        o_ref[...] = acc

    vm = pl.BlockSpec(memory_space=pltpu.VMEM)
    return pl.pallas_call(
        body, name="small_allreduce", in_specs=[vm], out_specs=vm,
        out_shape=jax.ShapeDtypeStruct(v.shape, F32),
        scratch_shapes=[pltpu.VMEM((N_DEV,) + v.shape, F32), pltpu.SemaphoreType.DMA((N_DEV - 1,)),
                        pltpu.SemaphoreType.DMA((N_DEV - 1,))],
    )(v)


def _adamw(w, g, m, v, name, tb=2048):
    rows = w.shape[0]
    tb = min(tb, rows)
    c1 = 1.0 - ADAM_B1
    c2 = 1.0 - ADAM_B2
    bc1 = 1.0 - ADAM_B1 ** ADAM_STEP
    bc2 = 1.0 - ADAM_B2 ** ADAM_STEP

    def body(w_ref, g_ref, m_ref, v_ref, d_ref, nm_ref, nv_ref):
        gv = g_ref[...]
        nm = ADAM_B1 * m_ref[...] + c1 * gv
        nv = ADAM_B2 * v_ref[...] + c2 * (gv * gv)
        nm_ref[...] = nm
        nv_ref[...] = nv
        d_ref[...] = -ADAM_LR * ((nm / bc1) / (jnp.sqrt(nv / bc2) + ADAM_EPS) + ADAM_WD * w_ref[...])

    blk = pl.BlockSpec((tb, 128), lambda i: (i, 0))
    return pl.pallas_call(
        body, name=name, grid=(rows // tb,), in_specs=[blk] * 4, out_specs=[blk] * 3,
        out_shape=[jax.ShapeDtypeStruct((rows, 128), F32)] * 3,
        compiler_params=_params("parallel"),
    )(w, g, m, v)


def kernel(x, w_in, b_fgate, hgrn_lb_logits, hgrn_norm_g, w_branch_a, w_branch_b, w_out, ln1_g, ln1_b, w_ff_in, w_ff_out, ln2_g, ln2_b, loss_target, m_w_in, m_b_fgate, m_hgrn_lb_logits, m_hgrn_norm_g, m_w_branch_a, m_w_branch_b, m_w_out, m_ln1_g, m_ln1_b, m_w_ff_in, m_w_ff_out, m_ln2_g, m_ln2_b, v_w_in, v_b_fgate, v_hgrn_lb_logits, v_hgrn_norm_g, v_w_branch_a, v_w_branch_b, v_w_out, v_ln1_g, v_ln1_b, v_w_ff_in, v_w_ff_out, v_ln2_g, v_ln2_b):
    weights = dict(w_in=w_in, b_fgate=b_fgate, hgrn_lb_logits=hgrn_lb_logits, hgrn_norm_g=hgrn_norm_g,
                   w_branch_a=w_branch_a, w_branch_b=w_branch_b, w_out=w_out, ln1_g=ln1_g, ln1_b=ln1_b,
                   w_ff_in=w_ff_in, w_ff_out=w_ff_out, ln2_g=ln2_g, ln2_b=ln2_b)
    mom1 = dict(w_in=m_w_in, b_fgate=m_b_fgate, hgrn_lb_logits=m_hgrn_lb_logits, hgrn_norm_g=m_hgrn_norm_g,
                w_branch_a=m_w_branch_a, w_branch_b=m_w_branch_b, w_out=m_w_out, ln1_g=m_ln1_g, ln1_b=m_ln1_b,
                w_ff_in=m_w_ff_in, w_ff_out=m_w_ff_out, ln2_g=m_ln2_g, ln2_b=m_ln2_b)
    mom2 = dict(w_in=v_w_in, b_fgate=v_b_fgate, hgrn_lb_logits=v_hgrn_lb_logits, hgrn_norm_g=v_hgrn_norm_g,
                w_branch_a=v_w_branch_a, w_branch_b=v_w_branch_b, w_out=v_w_out, ln1_g=v_ln1_g, ln1_b=v_ln1_b,
                w_ff_in=v_w_ff_in, w_ff_out=v_w_ff_out, ln2_g=v_ln2_g, ln2_b=v_ln2_b)
    core = lax.axis_index("c")

    def pack_layers(tree):
        return jnp.stack([_pack_big({name: tree[name][l] for name, _, _, _ in _BIG}) for l in range(DEPTH)])

    w_packed = pack_layers(weights)
    gathered = _gather_weights(w_packed.astype(BF16))
    wfull = []
    for l in range(DEPTH):
        shards = [_unpack_big(gathered[q, l]) for q in range(N_CHIPS)]
        wfull.append({key: jnp.concatenate([shards[q][name] for q in range(N_CHIPS)], axis=axis)
                      for name, key, _, axis in _BIG})
    small = {name: weights[name] for name, _ in _SMALL}

    loss_part, grad_x, grads, d_logits = _local_step(x[0], loss_target[0], wfull, small)

    def shard_of(full, axis, q):
        n = full.shape[axis] // N_CHIPS
        return lax.slice_in_dim(full, q * n, (q + 1) * n, axis=axis)

    g_all = jnp.stack([jnp.stack([_pack_big({name: shard_of(grads[l][key], axis, q) for name, key, _, axis in _BIG})
                                  for q in range(N_CHIPS)]) for l in range(DEPTH)])
    received = _pair_exchange(g_all)
    pair = _pair_sum(g_all, received, core)
    by_chip = _shard_exchange(pair)
    g_layer = _sum4(by_chip)
    g_packed = _result_exchange(g_layer)

    rows = DEPTH * PACK_ROWS
    delta, new_m, new_v = _adamw(w_packed.reshape(rows, 128), g_packed.reshape(rows, 128),
                                 pack_layers(mom1).reshape(rows, 128), pack_layers(mom2).reshape(rows, 128), "adamw_big")

    def unpack_layers(slab):
        per = [_unpack_big(slab.reshape(DEPTH, PACK_ROWS, 128)[l]) for l in range(DEPTH)]
        return {name: jnp.stack([per[l][name] for l in range(DEPTH)]) for name, _, _, _ in _BIG}

    out_g, out_d, out_m, out_v = (unpack_layers(t) for t in (g_packed, delta, new_m, new_v))

    small_grads = {name: jnp.stack([grads[l][key] for l in range(DEPTH)])
                   for name, key in [("b_fgate", "b_fgate"), ("hgrn_norm_g", "norm_g"), ("ln1_g", "ln1_g"),
                                     ("ln1_b", "ln1_b"), ("ln2_g", "ln2_g"), ("ln2_b", "ln2_b")]}
    small_grads["hgrn_lb_logits"] = d_logits
    gs = _allreduce_small(_pack_small(small_grads))
    ds, ms, vs = _adamw(_pack_small(small), gs, _pack_small({n: mom1[n] for n, _ in _SMALL}),
                        _pack_small({n: mom2[n] for n, _ in _SMALL}), "adamw_small")
    for tree, slab in ((out_g, gs), (out_d, ds), (out_m, ms), (out_v, vs)):
        tree.update(_unpack_small(slab))

    loss = lax.psum(loss_part, ("x", "y", "c"))
    order = ["w_in", "b_fgate", "hgrn_lb_logits", "hgrn_norm_g", "w_branch_a", "w_branch_b", "w_out", "ln1_g", "ln1_b",
             "w_ff_in", "w_ff_out", "ln2_g", "ln2_b"]
    return (loss, grad_x[None], *[out_g[n] for n in order], *[out_d[n] for n in order],
            *[out_m[n] for n in order], *[out_v[n] for n in order])
```

```python
import functools
import math

import jax
import jax.numpy as jnp
import numpy as np
from jax import lax
from jax.experimental import pallas as pl
from jax.experimental.pallas import tpu as pltpu

F32 = jnp.float32
BF16 = jnp.bfloat16

D_MODEL = 1024
DEPTH = 2
A_HEADS = 8
A_WIDTH = 512
B_WIDTH = 512
B_HEADS = 4
HD = 128
CHUNK = 64
SUB = 16
FFN_HIDDEN = 2816
IN_TOTAL = 5640
ALPHA = (2 * DEPTH) ** 0.25
LN_EPS = 1e-5
RMS_EPS = 1e-6
ADAM_LR = 0.001
ADAM_B1 = 0.9
ADAM_B2 = 0.999
ADAM_EPS = 1e-08
ADAM_WD = 0.01
ADAM_STEP = 10
EXP_CLAMP = 60.0

VMEM_LIMIT_BYTES = 56 * 1024 * 1024
N_CHIPS = 4
N_DEV = 8
MESH = pl.DeviceIdType.MESH

_DN = {
    "nn": (((1,), (0,)), ((), ())),
    "nt": (((1,), (1,)), ((), ())),
    "tn": (((0,), (0,)), ((), ())),
}


def _dot(a, b, mode="nn"):
    return lax.dot_general(a.astype(BF16), b.astype(BF16), _DN[mode], preferred_element_type=F32)


def _dot_hi(a, b, mode="nn"):
    return lax.dot_general(a, b, _DN[mode], precision=lax.Precision.HIGHEST, preferred_element_type=F32)


def _hdot(a, b, mode="nn"):
    return _dot_hi(a.astype(F32), b.astype(F32), mode)


def _params(*sem):
    return pltpu.CompilerParams(dimension_semantics=sem, vmem_limit_bytes=VMEM_LIMIT_BYTES)


def _sigmoid(x):
    return 1.0 / (1.0 + jnp.exp(-x))


def _matmul(a, b, mode, out_dtype, tm, tn, tk, name):
    if mode == "nn":
        (m, k), (k2, n) = a.shape, b.shape
    elif mode == "nt":
        (m, k), (n, k2) = a.shape, b.shape
    else:
        (k, m), (k2, n) = a.shape, b.shape
    assert k == k2, (a.shape, b.shape, mode)
    tm, tn, tk = min(tm, m), min(tn, n), min(tk, k)
    assert m % tm == 0 and n % tn == 0 and k % tk == 0, (a.shape, b.shape, tm, tn, tk)
    nk = k // tk
    if mode == "tn":
        a_spec = pl.BlockSpec((tk, tm), lambda j, i, kk: (kk, i))
    else:
        a_spec = pl.BlockSpec((tm, tk), lambda j, i, kk: (i, kk))
    if mode == "nt":
        b_spec = pl.BlockSpec((tn, tk), lambda j, i, kk: (j, kk))
    else:
        b_spec = pl.BlockSpec((tk, tn), lambda j, i, kk: (kk, j))
    use_acc = nk > 1 and out_dtype != F32

    def body(a_ref, b_ref, o_ref, *scratch):
        p = _dot(a_ref[...], b_ref[...], mode)
        if nk == 1:
            o_ref[...] = p.astype(out_dtype)
            return
        acc_ref = scratch[0] if use_acc else o_ref
        kk = pl.program_id(2)

        @pl.when(kk == 0)
        def _():
            acc_ref[...] = p

        @pl.when(kk > 0)
        def _():
            acc_ref[...] += p

        if use_acc:
            @pl.when(kk == nk - 1)
            def _():
                o_ref[...] = acc_ref[...].astype(out_dtype)

    return pl.pallas_call(
        body,
        name=name,
        grid=(n // tn, m // tm, nk),
        in_specs=[a_spec, b_spec],
        out_specs=pl.BlockSpec((tm, tn), lambda j, i, kk: (i, j)),
        out_shape=jax.ShapeDtypeStruct((m, n), out_dtype),
        scratch_shapes=[pltpu.VMEM((tm, tn), F32)] if use_acc else [],
        compiler_params=_params("parallel", "parallel", "arbitrary"),
    )(a, b)


def _mm_res_ln(a, w, res, g, b, name, tm=512):
    t, k = a.shape
    d = w.shape[1]
    tm = min(tm, t)

    def body(a_ref, w_ref, r_ref, g_ref, b_ref, y_ref, yb_ref, xh_ref, rs_ref):
        z = ALPHA * r_ref[...] + _dot(a_ref[...], w_ref[...])
        mu = jnp.mean(z, axis=-1, keepdims=True)
        zc = z - mu
        var = jnp.mean(zc * zc, axis=-1, keepdims=True)
        rstd = lax.rsqrt(var + LN_EPS)
        xh = zc * rstd
        y = xh * g_ref[...] + b_ref[...]
        y_ref[...] = y
        yb_ref[...] = y.astype(BF16)
        xh_ref[...] = xh
        rs_ref[...] = rstd

    row = lambda i: (i, 0)
    fix = lambda i: (0, 0)
    return pl.pallas_call(
        body,
        name=name,
        grid=(t // tm,),
        in_specs=[pl.BlockSpec((tm, k), row), pl.BlockSpec((k, d), fix), pl.BlockSpec((tm, d), row),
                  pl.BlockSpec((1, d), fix), pl.BlockSpec((1, d), fix)],
        out_specs=[pl.BlockSpec((tm, d), row), pl.BlockSpec((tm, d), row), pl.BlockSpec((tm, d), row),
                   pl.BlockSpec((tm, 1), row)],
        out_shape=[jax.ShapeDtypeStruct((t, d), F32), jax.ShapeDtypeStruct((t, d), BF16),
                   jax.ShapeDtypeStruct((t, d), F32), jax.ShapeDtypeStruct((t, 1), F32)],
        compiler_params=_params("parallel"),
    )(a, w, res, g.reshape(1, d), b.reshape(1, d))


def _ln_bwd(dys, coefs, xhat, rstd, g, name, tm=512):
    t, d = xhat.shape
    tm = min(tm, t)
    n_in = len(dys)

    def body(*refs):
        dy_refs = refs[:n_in]
        xh_ref, rs_ref, g_ref, dz_ref, dzb_ref, dg_ref, db_ref = refs[n_in:]
        dy = coefs[0] * dy_refs[0][...].astype(F32)
        for c, r in zip(coefs[1:], dy_refs[1:]):
            dy = dy + c * r[...].astype(F32)
        xh = xh_ref[...]
        dxh = dy * g_ref[...]
        m1 = jnp.mean(dxh, axis=-1, keepdims=True)
        m2 = jnp.mean(dxh * xh, axis=-1, keepdims=True)
        dz = rs_ref[...] * (dxh - m1 - xh * m2)
        dz_ref[...] = dz
        dzb_ref[...] = dz.astype(BF16)
        pg = jnp.sum(dy * xh, axis=0, keepdims=True)
        pb = jnp.sum(dy, axis=0, keepdims=True)

        @pl.when(pl.program_id(0) == 0)
        def _():
            dg_ref[...] = pg
            db_ref[...] = pb

        @pl.when(pl.program_id(0) > 0)
        def _():
            dg_ref[...] += pg
            db_ref[...] += pb

    row = lambda i: (i, 0)
    fix = lambda i: (0, 0)
    return pl.pallas_call(
        body,
        name=name,
        grid=(t // tm,),
        in_specs=[pl.BlockSpec((tm, d), row)] * n_in
        + [pl.BlockSpec((tm, d), row), pl.BlockSpec((tm, 1), row), pl.BlockSpec((1, d), fix)],
        out_specs=[pl.BlockSpec((tm, d), row), pl.BlockSpec((tm, d), row), pl.BlockSpec((1, d), fix),
                   pl.BlockSpec((1, d), fix)],
        out_shape=[jax.ShapeDtypeStruct((t, d), F32), jax.ShapeDtypeStruct((t, d), BF16),
                   jax.ShapeDtypeStruct((1, d), F32), jax.ShapeDtypeStruct((1, d), F32)],
        compiler_params=_params("arbitrary"),
    )(*dys, xhat, rstd, g.reshape(1, d))


def _loss_head(y, target, name="loss_head", tm=512):
    t, d = y.shape
    tm = min(tm, t)

    def body(y_ref, t_ref, dy_ref, l_ref):
        e = y_ref[...] - t_ref[...]
        dy_ref[...] = e * (1.0 / d)
        part = jnp.full((8, 128), 0.5 / d, F32) * jnp.sum(e * e)

        @pl.when(pl.program_id(0) == 0)
        def _():
            l_ref[...] = part

        @pl.when(pl.program_id(0) > 0)
        def _():
            l_ref[...] += part

    row = lambda i: (i, 0)
    return pl.pallas_call(
        body,
        name=name,
        grid=(t // tm,),
        in_specs=[pl.BlockSpec((tm, d), row), pl.BlockSpec((tm, d), row)],
        out_specs=[pl.BlockSpec((tm, d), row), pl.BlockSpec((8, 128), lambda i: (0, 0))],
        out_shape=[jax.ShapeDtypeStruct((t, d), F32), jax.ShapeDtypeStruct((8, 128), F32)],
        compiler_params=_params("arbitrary"),
    )(y, target)


def _swiglu_fwd(h, name, tm=512):
    t = h.shape[0]
    tm = min(tm, t)
    wb = FFN_HIDDEN // 2

    def body(u_ref, g_ref, a_ref):
        g = g_ref[...]
        a_ref[...] = (g * _sigmoid(g) * u_ref[...]).astype(BF16)

    return pl.pallas_call(
        body,
        name=name,
        grid=(t // tm, 2),
        in_specs=[pl.BlockSpec((tm, wb), lambda i, j: (i, j)), pl.BlockSpec((tm, wb), lambda i, j: (i, j + 2))],
        out_specs=pl.BlockSpec((tm, wb), lambda i, j: (i, j)),
        out_shape=jax.ShapeDtypeStruct((t, FFN_HIDDEN), BF16),
        compiler_params=_params("parallel", "parallel"),
    )(h, h)


def _swiglu_bwd(da, h, name, tm=512):
    t = h.shape[0]
    tm = min(tm, t)
    wb = FFN_HIDDEN // 2

    def body(da_ref, u_ref, g_ref, dh_ref):
        g = g_ref[...]
        sg = _sigmoid(g)
        da = da_ref[...]

        @pl.when(pl.program_id(1) < 2)
        def _():
            dh_ref[...] = (da * g * sg).astype(BF16)

        @pl.when(pl.program_id(1) >= 2)
        def _():
            dh_ref[...] = (da * u_ref[...] * (sg * (1.0 + g * (1.0 - sg)))).astype(BF16)

    return pl.pallas_call(
        body,
        name=name,
        grid=(t // tm, 4),
        in_specs=[pl.BlockSpec((tm, wb), lambda i, j: (i, j % 2)), pl.BlockSpec((tm, wb), lambda i, j: (i, j % 2)),
                  pl.BlockSpec((tm, wb), lambda i, j: (i, 2 + j % 2))],
        out_specs=pl.BlockSpec((tm, wb), lambda i, j: (i, j)),
        out_shape=jax.ShapeDtypeStruct((t, 2 * FFN_HIDDEN), BF16),
        compiler_params=_params("parallel", "parallel"),
    )(da, h, h)


def _merge_fwd(ya, yb, wpa, wpb, rest, name, tm=512):
    t = ya.shape[0]
    tm = min(tm, t)

    def body(ya_ref, yb_ref, wa_ref, wb_ref, ga_ref, gb_ref, o_ref):
        pa = _dot(ya_ref[...], wa_ref[...])
        pb = _dot(yb_ref[...], wb_ref[...])
        o_ref[...] = (_sigmoid(ga_ref[...]) * pa + _sigmoid(gb_ref[...]) * pb).astype(BF16)

    row = lambda i: (i, 0)
    fix = lambda i: (0, 0)
    return pl.pallas_call(
        body,
        name=name,
        grid=(t // tm,),
        in_specs=[pl.BlockSpec((tm, A_WIDTH), row), pl.BlockSpec((tm, B_WIDTH), row),
                  pl.BlockSpec((A_WIDTH, D_MODEL), fix), pl.BlockSpec((B_WIDTH, D_MODEL), fix),
                  pl.BlockSpec((tm, D_MODEL), lambda i: (i, 0)), pl.BlockSpec((tm, D_MODEL), lambda i: (i, 1))],
        out_specs=pl.BlockSpec((tm, D_MODEL), row),
        out_shape=jax.ShapeDtypeStruct((t, D_MODEL), BF16),
        compiler_params=_params("parallel"),
    )(ya, yb, wpa, wpb, rest, rest)


def _merge_bwd(dm, ya, yb, wpa, wpb, rest, name, tm=512):
    t = ya.shape[0]
    tm = min(tm, t)

    def body(dm_ref, ya_ref, yb_ref, wa_ref, wb_ref, ga_ref, gb_ref, dg_ref, dpa_ref, dpb_ref):
        dm_v = dm_ref[...]
        pa = _dot(ya_ref[...], wa_ref[...])
        pb = _dot(yb_ref[...], wb_ref[...])
        sa = _sigmoid(ga_ref[...])
        sb = _sigmoid(gb_ref[...])
        dg_ref[:, :D_MODEL] = (dm_v * pa * sa * (1.0 - sa)).astype(BF16)
        dg_ref[:, D_MODEL:] = (dm_v * pb * sb * (1.0 - sb)).astype(BF16)
        dpa_ref[...] = (dm_v * sa).astype(BF16)
        dpb_ref[...] = (dm_v * sb).astype(BF16)

    row = lambda i: (i, 0)
    fix = lambda i: (0, 0)
    return pl.pallas_call(
        body,
        name=name,
        grid=(t // tm,),
        in_specs=[pl.BlockSpec((tm, D_MODEL), row), pl.BlockSpec((tm, A_WIDTH), row), pl.BlockSpec((tm, B_WIDTH), row),
                  pl.BlockSpec((A_WIDTH, D_MODEL), fix), pl.BlockSpec((B_WIDTH, D_MODEL), fix),
                  pl.BlockSpec((tm, D_MODEL), lambda i: (i, 0)), pl.BlockSpec((tm, D_MODEL), lambda i: (i, 1))],
        out_specs=[pl.BlockSpec((tm, 2 * D_MODEL), row), pl.BlockSpec((tm, D_MODEL), row),
                   pl.BlockSpec((tm, D_MODEL), row)],
        out_shape=[jax.ShapeDtypeStruct((t, 2 * D_MODEL), BF16), jax.ShapeDtypeStruct((t, D_MODEL), BF16),
                   jax.ShapeDtypeStruct((t, D_MODEL), BF16)],
        compiler_params=_params("parallel"),
    )(dm, ya, yb, wpa, wpb, rest, rest)


FA_BLOCK = 4224 // 128 - 1


def _tri(n, lower):
    r = lax.broadcasted_iota(jnp.int32, (n, n), 0)
    c = lax.broadcasted_iota(jnp.int32, (n, n), 1)
    return jnp.where((r >= c) if lower else (r <= c), 1.0, 0.0).astype(F32)


def _fox_gate_fwd(rest, bf, name, tb=512):
    t = rest.shape[0]
    tb = min(tb, t)

    def body(fa_ref, bf_ref, f_ref, carry):
        @pl.when(pl.program_id(0) == 0)
        def _():
            carry[...] = jnp.zeros_like(carry)

        z = fa_ref[...] + bf_ref[...]
        logf = jnp.minimum(z, 0.0) - jnp.log(1.0 + jnp.exp(-jnp.abs(z)))
        f = _dot_hi(_tri(tb, True), logf) + carry[...]
        f_ref[...] = f
        carry[...] = f[tb - 1:tb, :]

    return pl.pallas_call(
        body,
        name=name,
        grid=(t // tb,),
        in_specs=[pl.BlockSpec((tb, 128), lambda i: (i, FA_BLOCK)), pl.BlockSpec((1, 128), lambda i: (0, 0))],
        out_specs=pl.BlockSpec((tb, 128), lambda i: (i, 0)),
        out_shape=jax.ShapeDtypeStruct((t, 128), F32),
        scratch_shapes=[pltpu.VMEM((1, 128), F32)],
        compiler_params=_params("arbitrary"),
    )(rest, bf)


def _fox_gate_bwd(d_f, rest, bf, name, tb=512):
    t = rest.shape[0]
    tb = min(tb, t)
    nb = t // tb

    def body(df_ref, fa_ref, bf_ref, dfa_ref, dbf_ref, carry):
        @pl.when(pl.program_id(0) == 0)
        def _():
            carry[...] = jnp.zeros_like(carry)

        dlogf = _dot_hi(_tri(tb, False), df_ref[...]) + carry[...]
        carry[...] = dlogf[0:1, :]
        z = fa_ref[...] + bf_ref[...]
        dz = dlogf * _sigmoid(-z)
        dfa_ref[...] = dz.astype(BF16)
        part = jnp.sum(dz, axis=0, keepdims=True)

        @pl.when(pl.program_id(0) == 0)
        def _():
            dbf_ref[...] = part

        @pl.when(pl.program_id(0) > 0)
        def _():
            dbf_ref[...] += part

    return pl.pallas_call(
        body,
        name=name,
        grid=(nb,),
        in_specs=[pl.BlockSpec((tb, 128), lambda i: (nb - 1 - i, 0)),
                  pl.BlockSpec((tb, 128), lambda i: (nb - 1 - i, FA_BLOCK)),
                  pl.BlockSpec((1, 128), lambda i: (0, 0))],
        out_specs=[pl.BlockSpec((tb, 128), lambda i: (nb - 1 - i, 0)), pl.BlockSpec((1, 128), lambda i: (0, 0))],
        out_shape=[jax.ShapeDtypeStruct((t, 128), BF16), jax.ShapeDtypeStruct((1, 128), F32)],
        scratch_shapes=[pltpu.VMEM((1, 128), F32)],
        compiler_params=_params("arbitrary"),
    )(d_f, rest, bf)


ATT_BLOCK = 512


def _head_mask(shape, j):
    lane = lax.broadcasted_iota(jnp.int32, shape, 1)
    return (lane < 64) if j == 0 else (lane >= 64)


def _split3(x):
    h = x.astype(BF16).astype(F32)
    r = x - h
    m = r.astype(BF16).astype(F32)
    return h, m, (r - m).astype(BF16).astype(F32)


def _aug_lanes(tb, j):
    lane = lax.broadcasted_iota(jnp.int32, (tb, 128), 1)
    own = (lane < 64) if j == 0 else (lane >= 64)
    return own, lane - 64 * (1 - j)


def _aug_query(own, li, q, bias):
    h, m, l = _split3(bias)
    spare = jnp.where(li == 0, h, jnp.where(li == 1, m, jnp.where(li == 2, l, jnp.where(li < 6, 1.0, 0.0))))
    return jnp.where(own, q, spare).astype(BF16)


def _fox_prep_fwd(qkv, fcol, name, tb=512):
    t = qkv.shape[0]
    tb = min(tb, t)

    def body(q_ref, k_ref, v_ref, fc_ref, qa_ref, ka_ref, va_ref):
        fsw = pltpu.roll(fc_ref[...], 64, 1)
        q = q_ref[...].astype(F32) * 0.125
        k = k_ref[...].astype(F32)
        v = v_ref[...].astype(F32)
        h, m, l = _split3(fsw)
        for j in (0, 1):
            own, li = _aug_lanes(tb, j)
            cols = slice(128 * j, 128 * (j + 1))
            qa_ref[:, cols] = _aug_query(own, li, q, fsw)
            ks = jnp.where(li < 3, 1.0, jnp.where(li == 3, -h, jnp.where(li == 4, -m, jnp.where(li == 5, -l, 0.0))))
            ka_ref[:, cols] = jnp.where(own, k, ks).astype(BF16)
            va_ref[:, cols] = jnp.where(own, v, 1.0).astype(BF16)

    blk = pl.BlockSpec((tb, 256), lambda i, h: (i, h))
    return pl.pallas_call(
        body, name=name, grid=(t // tb, 4),
        in_specs=[pl.BlockSpec((tb, 128), lambda i, h: (i, h)), pl.BlockSpec((tb, 128), lambda i, h: (i, 4 + h)),
                  pl.BlockSpec((tb, 128), lambda i, h: (i, 8 + h)), pl.BlockSpec((tb, 128), lambda i, h: (i, h))],
        out_specs=[blk, blk, blk],
        out_shape=[jax.ShapeDtypeStruct((t, 2 * A_WIDTH), BF16)] * 3,
        compiler_params=_params("parallel", "parallel"),
    )(qkv, qkv, qkv, fcol)


def _fox_prep_bwd(qkv, fcol, lse, do, o, name, tb=512):
    t = qkv.shape[0]
    tb = min(tb, t)

    def body(q_ref, fc_ref, lse_ref, do_ref, o_ref, qb_ref, dob_ref):
        gsw = pltpu.roll(fc_ref[...] - lse_ref[...], 64, 1)
        q = q_ref[...].astype(F32) * 0.125
        do_v = do_ref[...].astype(F32)
        prod = do_v * o_ref[...].astype(F32)
        for j in (0, 1):
            own, li = _aug_lanes(tb, j)
            cols = slice(128 * j, 128 * (j + 1))
            qb_ref[:, cols] = _aug_query(own, li, q, gsw)
            delta = jnp.sum(jnp.where(own, prod, 0.0), axis=1, keepdims=True)
            h, m, l = _split3(jnp.broadcast_to(delta, (tb, 128)))
            ds = jnp.where(li == 0, -h, jnp.where(li == 1, -m, jnp.where(li == 2, -l, 0.0)))
            dob_ref[:, cols] = jnp.where(own, do_v, ds).astype(BF16)

    pair = pl.BlockSpec((tb, 128), lambda i, h: (i, h))
    blk = pl.BlockSpec((tb, 256), lambda i, h: (i, h))
    return pl.pallas_call(
        body, name=name, grid=(t // tb, 4),
        in_specs=[pair, pair, pair, pair, pair],
        out_specs=[blk, blk],
        out_shape=[jax.ShapeDtypeStruct((t, 2 * A_WIDTH), BF16)] * 2,
        compiler_params=_params("parallel", "parallel"),
    )(qkv, fcol, lse, do, o)


def _tile_mask(n, transposed):
    r = lax.broadcasted_iota(jnp.int32, (n, n), 0)
    c = lax.broadcasted_iota(jnp.int32, (n, n), 1)
    return (c >= r) if transposed else (r >= c)


def _fox_fwd(qa, ka, va, name):
    t = qa.shape[0]
    bq = min(ATT_BLOCK, t)
    nq = t // bq

    def body(q_ref, k_ref, v_ref, o_ref, lse_ref):
        i = pl.program_id(1)
        outs = []
        for j in (0, 1):
            cols = slice(128 * j, 128 * (j + 1))
            qj = q_ref[:, cols]

            def step(kb, carry, masked, qj=qj, cols=cols):
                m, acc = carry
                rows = pl.ds(pl.multiple_of(kb * bq, bq), bq)
                s = _dot(qj, k_ref[rows, cols], "nt")
                if masked:
                    s = jnp.where(_tile_mask(bq, False), s, -jnp.inf)
                m_new = jnp.maximum(m, jnp.max(s, axis=1, keepdims=True))
                acc = jnp.exp(m - m_new) * acc + _dot(jnp.exp(s - m_new), v_ref[rows, cols])
                return m_new, acc

            carry = (jnp.full((bq, 1), -jnp.inf, F32), jnp.zeros((bq, 128), F32))
            carry = lax.fori_loop(0, i, lambda kb, c, step=step: step(kb, c, False), carry)
            m, acc = step(i, carry, True)
            spare = 64 * (1 - j)
            l = acc[:, spare:spare + 1]
            outs.append((acc / l, m + jnp.log(l)))
        msk = _head_mask((bq, 128), 0)
        o_ref[...] = jnp.where(msk, outs[0][0], outs[1][0]).astype(BF16)
        lse_ref[...] = jnp.where(msk, outs[0][1], outs[1][1])

    res = pl.BlockSpec((t, 256), lambda h, i: (0, h))
    out = pl.BlockSpec((bq, 128), lambda h, i: (i, h))
    return pl.pallas_call(
        body,
        name=name,
        grid=(4, nq),
        in_specs=[pl.BlockSpec((bq, 256), lambda h, i: (i, h)), res, res],
        out_specs=[out, out],
        out_shape=[jax.ShapeDtypeStruct((t, A_WIDTH), BF16), jax.ShapeDtypeStruct((t, A_WIDTH), F32)],
        compiler_params=_params("parallel", "parallel"),
    )(qa, ka, va)


def _fox_bwd_dq(qb, ka, va, dob, name):
    t = qb.shape[0]
    bq = min(ATT_BLOCK, t)
    nq = t // bq

    def body(q_ref, k_ref, v_ref, do_ref, dq_ref, rs_ref):
        i = pl.program_id(1)
        outs = []
        for j in (0, 1):
            cols = slice(128 * j, 128 * (j + 1))
            qj = q_ref[:, cols]
            doj = do_ref[:, cols]

            def step(kb, acc, masked, qj=qj, doj=doj, cols=cols):
                rows = pl.ds(pl.multiple_of(kb * bq, bq), bq)
                ks = k_ref[rows, cols]
                s = _dot(qj, ks, "nt")
                if masked:
                    s = jnp.where(_tile_mask(bq, False), s, -jnp.inf)
                ds = jnp.exp(s) * _dot(doj, v_ref[rows, cols], "nt")
                return acc + _dot(ds, ks)

            acc = lax.fori_loop(0, i, lambda kb, c, step=step: step(kb, c, False), jnp.zeros((bq, 128), F32))
            acc = step(i, acc, True)
            spare = 64 * (1 - j)
            outs.append((acc * 0.125, acc[:, spare:spare + 1]))
        msk = _head_mask((bq, 128), 0)
        dq_ref[...] = jnp.where(msk, outs[0][0], outs[1][0]).astype(BF16)
        rs_ref[...] = jnp.where(msk, outs[0][1], outs[1][1])

    blk = pl.BlockSpec((bq, 256), lambda h, i: (i, h))
    res = pl.BlockSpec((t, 256), lambda h, i: (0, h))
    out = pl.BlockSpec((bq, 128), lambda h, i: (i, h))
    return pl.pallas_call(
        body,
        name=name,
        grid=(4, nq),
        in_specs=[blk, res, res, blk],
        out_specs=[out, out],
        out_shape=[jax.ShapeDtypeStruct((t, A_WIDTH), BF16), jax.ShapeDtypeStruct((t, A_WIDTH), F32)],
        compiler_params=_params("parallel", "parallel"),
    )(qb, ka, va, dob)


def _fox_bwd_dkv(qb, ka, va, dob, name):
    t = qb.shape[0]
    bk = min(ATT_BLOCK, t)
    nk = t // bk

    def body(k_ref, v_ref, q_ref, do_ref, dk_ref, dv_ref, cs_ref):
        jb = pl.program_id(1)
        outs = []
        for j in (0, 1):
            cols = slice(128 * j, 128 * (j + 1))
            kj = k_ref[:, cols]
            vj = v_ref[:, cols]

            def step(ib, carry, masked, kj=kj, vj=vj, cols=cols):
                dk_acc, dv_acc = carry
                rows = pl.ds(pl.multiple_of(ib * bk, bk), bk)
                qs = q_ref[rows, cols]
                dos = do_ref[rows, cols]
                st = _dot(kj, qs, "nt")
                if masked:
                    st = jnp.where(_tile_mask(bk, True), st, -jnp.inf)
                pt = jnp.exp(st)
                dv_acc = dv_acc + _dot(pt, dos)
                dk_acc = dk_acc + _dot(pt * _dot(vj, dos, "nt"), qs)
                return dk_acc, dv_acc

            carry = step(jb, (jnp.zeros((bk, 128), F32), jnp.zeros((bk, 128), F32)), True)
            dk_acc, dv_acc = lax.fori_loop(jb + 1, nk, lambda ib, c, step=step: step(ib, c, False), carry)
            spare = 64 * (1 - j)
            outs.append((dk_acc, dv_acc, dk_acc[:, spare + 3:spare + 4]))
        msk = _head_mask((bk, 128), 0)
        dk_ref[...] = jnp.where(msk, outs[0][0], outs[1][0]).astype(BF16)
        dv_ref[...] = jnp.where(msk, outs[0][1], outs[1][1]).astype(BF16)
        cs_ref[...] = jnp.where(msk, outs[0][2], outs[1][2])

    blk = pl.BlockSpec((bk, 256), lambda h, i: (i, h))
    res = pl.BlockSpec((t, 256), lambda h, i: (0, h))
    out = pl.BlockSpec((bk, 128), lambda h, i: (i, h))
    return pl.pallas_call(
        body,
        name=name,
        grid=(4, nk),
        in_specs=[blk, blk, res, res],
        out_specs=[out, out, out],
        out_shape=[jax.ShapeDtypeStruct((t, A_WIDTH), BF16), jax.ShapeDtypeStruct((t, A_WIDTH), BF16),
                   jax.ShapeDtypeStruct((t, A_WIDTH), F32)],
        compiler_params=_params("parallel", "parallel"),
    )(ka, va, qb, dob)


HG_ROWS = 256


def _hg_gates(hb_ref, rows, h, lbh):
    qb = hb_ref[rows, h * HD:(h + 1) * HD]
    fb = hb_ref[rows, B_WIDTH + h * HD:B_WIDTH + (h + 1) * HD]
    v = hb_ref[rows, 2 * B_WIDTH + h * HD:2 * B_WIDTH + (h + 1) * HD]
    gb = hb_ref[rows, 3 * B_WIDTH + h * HD:3 * B_WIDTH + (h + 1) * HD]
    sg = _sigmoid(fb)
    f = lbh + (1.0 - lbh) * sg
    sq = _sigmoid(qb)
    return qb, sq, qb * sq, sg, f, 1.0 - f, jnp.log(f), v, gb


def _hg_intra_factors(q, k, b):
    fac = []
    for i in range(CHUNK // SUB):
        bi = b[SUB * i:SUB * i + 1, :]
        eq = jnp.exp(b[SUB * i:SUB * (i + 1), :] - bi)
        ek = jnp.exp(jnp.minimum(bi - b, EXP_CLAMP))
        fac.append((eq, ek, q[SUB * i:SUB * (i + 1), :] * eq, k * ek))
    return fac


def _causal(n):
    r = lax.broadcasted_iota(jnp.int32, (n, n), 0)
    c = lax.broadcasted_iota(jnp.int32, (n, n), 1)
    return r >= c


def _hgrn_fwd(rest, lb, ng, name):
    t = rest.shape[0]
    bt = min(HG_ROWS, t)
    ncb = bt // CHUNK

    def body(hb_ref, lb_ref, ng_ref, y_ref, o_ref, st_ref, s_scr):
        @pl.when(pl.program_id(0) == 0)
        def _():
            s_scr[...] = jnp.zeros_like(s_scr)

        tril = _tri(CHUNK, True)
        causal = _causal(CHUNK)
        ones = jnp.ones((CHUNK, HD), F32)

        def chunk(c, carry):
            rows = pl.ds(pl.multiple_of(c * CHUNK, CHUNK), CHUNK)
            for h in range(B_HEADS):
                lbh = lb_ref[:, h * HD:(h + 1) * HD]
                _, _, q, _, _, k, g, v, gb = _hg_gates(hb_ref, rows, h, lbh)
                b = _dot_hi(tril, g)
                s0 = s_scr[h]
                st_ref[c, h] = s0
                o = _dot(q * jnp.exp(b), s0)
                a = jnp.concatenate([_dot(qe, ke, "nt") for _, _, qe, ke in _hg_intra_factors(q, k, b)], axis=0)
                o = o + _dot(jnp.where(causal, a, 0.0), v)
                blast = b[CHUNK - 1:CHUNK, :]
                kd = k * jnp.exp(blast - b)
                eb = jnp.exp(_dot_hi(g, ones, "tn"))
                s_scr[h] = eb * s0 + _dot(kd, v, "tn")
                r = lax.rsqrt(jnp.mean(o * o, axis=-1, keepdims=True) + RMS_EPS)
                o_ref[rows, h * HD:(h + 1) * HD] = o
                y_ref[rows, h * HD:(h + 1) * HD] = (o * r * ng_ref[...] * _sigmoid(gb)).astype(BF16)
            return carry

        lax.fori_loop(0, ncb, chunk, 0)

    return pl.pallas_call(
        body,
        name=name,
        grid=(t // bt,),
        in_specs=[pl.BlockSpec((bt, 4 * B_WIDTH), lambda i: (i, 1)), pl.BlockSpec((1, B_WIDTH), lambda i: (0, 0)),
                  pl.BlockSpec((1, HD), lambda i: (0, 0))],
        out_specs=[pl.BlockSpec((bt, B_WIDTH), lambda i: (i, 0)), pl.BlockSpec((bt, B_WIDTH), lambda i: (i, 0)),
                   pl.BlockSpec((ncb, B_HEADS, HD, HD), lambda i: (i, 0, 0, 0))],
        out_shape=[jax.ShapeDtypeStruct((t, B_WIDTH), BF16), jax.ShapeDtypeStruct((t, B_WIDTH), F32),
                   jax.ShapeDtypeStruct((t // CHUNK, B_HEADS, HD, HD), F32)],
        scratch_shapes=[pltpu.VMEM((B_HEADS, HD, HD), F32)],
        compiler_params=_params("arbitrary"),
    )(rest, lb, ng)


def _hgrn_bwd(dy, rest, o_saved, states, lb, ng, name):
    t = rest.shape[0]
    bt = min(HG_ROWS, t)
    ncb = bt // CHUNK
    nb = t // bt

    def body(dy_ref, hb_ref, o_ref, st_ref, lb_ref, ng_ref, dh_ref, dlb_ref, dng_ref, ds_scr):
        @pl.when(pl.program_id(0) == 0)
        def _():
            ds_scr[...] = jnp.zeros_like(ds_scr)
            dlb_ref[...] = jnp.zeros_like(dlb_ref)
            dng_ref[...] = jnp.zeros_like(dng_ref)

        tril = _tri(CHUNK, True)
        triu = _tri(CHUNK, False)
        causal = _causal(CHUNK)
        ones = jnp.ones((CHUNK, HD), F32)
        ones8 = jnp.ones((8, HD), F32)
        last_row = lax.broadcasted_iota(jnp.int32, (CHUNK, HD), 0) == CHUNK - 1

        def chunk(cc, carry):
            c = ncb - 1 - cc
            rows = pl.ds(pl.multiple_of(c * CHUNK, CHUNK), CHUNK)
            for h in range(B_HEADS):
                cols = slice(h * HD, (h + 1) * HD)
                lbh = lb_ref[:, cols]
                qb, sq, q, sg, f, k, g, v, gb = _hg_gates(hb_ref, rows, h, lbh)
                o = o_ref[rows, cols]
                dyv = dy_ref[rows, cols].astype(F32)
                ngv = ng_ref[...]
                r = lax.rsqrt(jnp.mean(o * o, axis=-1, keepdims=True) + RMS_EPS)
                sgb = _sigmoid(gb)
                don = dyv * sgb
                dgb = dyv * (o * r * ngv) * sgb * (1.0 - sgb)
                dng_ref[...] += jnp.sum(don * o * r, axis=0, keepdims=True)
                doh = don * ngv
                do = r * (doh - o * (r * r) * jnp.mean(doh * o, axis=-1, keepdims=True))
                b = _dot_hi(tril, g)
                ebt = jnp.exp(b)
                s0 = st_ref[c, h]
                ds1 = ds_scr[h]
                blast = b[CHUNK - 1:CHUNK, :]
                ekd = jnp.exp(blast - b)
                kd = k * ekd
                eb = jnp.exp(_dot_hi(g, ones, "tn"))
                fac = _hg_intra_factors(q, k, b)
                a = jnp.concatenate([_dot(qe, ke, "nt") for _, _, qe, ke in fac], axis=0)
                a = jnp.where(causal, a, 0.0)
                da = jnp.where(causal, _dot(do, v, "nt"), 0.0)
                dv = _dot(a, do, "tn") + _dot(kd, ds1)
                dq = ebt * _dot(do, s0, "nt")
                dq = dq + jnp.concatenate(
                    [eq * _hdot(da[SUB * i:SUB * (i + 1), :], ke) for i, (eq, _, _, ke) in enumerate(fac)], axis=0)
                dk_state = ekd * _dot(v, ds1, "nt")
                dk = dk_state
                for i, (_, ek, qe, _) in enumerate(fac):
                    dk = dk + ek * _hdot(da[SUB * i:SUB * (i + 1), :], qe, "tn")
                ds_scr[h] = _dot(q * ebt, do, "tn") + eb * ds1
                extra = jnp.exp(blast) * _dot_hi(ones8, ds1 * s0, "nt")[0:1, :] \
                    + jnp.sum(k * dk_state, axis=0, keepdims=True)
                db = q * dq - k * dk + jnp.where(last_row, extra, 0.0)
                dg = _dot_hi(triu, db)
                df = dg / f - dk
                dlb_ref[:, cols] += jnp.sum(df * (1.0 - sg), axis=0, keepdims=True)
                dfb = df * (1.0 - lbh) * sg * (1.0 - sg)
                dqb = dq * (sq * (1.0 + qb * (1.0 - sq)))
                dh_ref[rows, h * HD:(h + 1) * HD] = dqb.astype(BF16)
                dh_ref[rows, B_WIDTH + h * HD:B_WIDTH + (h + 1) * HD] = dfb.astype(BF16)
                dh_ref[rows, 2 * B_WIDTH + h * HD:2 * B_WIDTH + (h + 1) * HD] = dv.astype(BF16)
                dh_ref[rows, 3 * B_WIDTH + h * HD:3 * B_WIDTH + (h + 1) * HD] = dgb.astype(BF16)
            return carry

        lax.fori_loop(0, ncb, chunk, 0)

    rev = lambda i: (nb - 1 - i, 0)
    return pl.pallas_call(
        body,
        name=name,
        grid=(nb,),
        in_specs=[pl.BlockSpec((bt, B_WIDTH), rev), pl.BlockSpec((bt, 4 * B_WIDTH), lambda i: (nb - 1 - i, 1)),
                  pl.BlockSpec((bt, B_WIDTH), rev),
                  pl.BlockSpec((ncb, B_HEADS, HD, HD), lambda i: (nb - 1 - i, 0, 0, 0)),
                  pl.BlockSpec((1, B_WIDTH), lambda i: (0, 0)), pl.BlockSpec((1, HD), lambda i: (0, 0))],
        out_specs=[pl.BlockSpec((bt, 4 * B_WIDTH), rev), pl.BlockSpec((1, B_WIDTH), lambda i: (0, 0)),
                   pl.BlockSpec((1, HD), lambda i: (0, 0))],
        out_shape=[jax.ShapeDtypeStruct((t, 4 * B_WIDTH), BF16), jax.ShapeDtypeStruct((1, B_WIDTH), F32),
                   jax.ShapeDtypeStruct((1, HD), F32)],
        scratch_shapes=[pltpu.VMEM((B_HEADS, HD, HD), F32)],
        compiler_params=_params("arbitrary"),
    )(dy, rest, o_saved, states, lb, ng)


def _axpy2(c0, a0, c1, a1, name, tm=512):
    t, d = a0.shape
    tm = min(tm, t)

    def body(a_ref, b_ref, o_ref):
        o_ref[...] = c0 * a_ref[...] + c1 * b_ref[...]

    row = lambda i: (i, 0)
    return pl.pallas_call(
        body, name=name, grid=(t // tm,),
        in_specs=[pl.BlockSpec((tm, d), row), pl.BlockSpec((tm, d), row)],
        out_specs=pl.BlockSpec((tm, d), row),
        out_shape=jax.ShapeDtypeStruct((t, d), F32),
        compiler_params=_params("parallel"),
    )(a0, a1)


def _split_w_in(w_in_l):
    wqkv = w_in_l[:, :3 * A_WIDTH]
    wfa = jnp.pad(w_in_l[:, 3 * A_WIDTH:3 * A_WIDTH + A_HEADS], ((0, 0), (0, 128 - A_HEADS)))
    whb = w_in_l[:, 3 * A_WIDTH + A_HEADS:3 * A_WIDTH + A_HEADS + 4 * B_WIDTH]
    wgt = w_in_l[:, 3 * A_WIDTH + A_HEADS + 4 * B_WIDTH:]
    return wqkv, jnp.concatenate([wgt, whb, wfa], axis=1)


def _merge_w_in_grad(dwall):
    o = 3 * A_WIDTH
    return jnp.concatenate([dwall[:, :o], dwall[:, o + 4096:o + 4096 + A_HEADS], dwall[:, o + 2048:o + 4096],
                            dwall[:, o:o + 2048]], axis=1)


def _layer_fwd(x, xb, w, sp, l):
    t = x.shape[0]
    n = f"l{l}_"
    wqkv, wrest = _split_w_in(w["w_in"])
    qkv = _matmul(xb, wqkv, "nn", BF16, 512, 768, D_MODEL, n + "proj_qkv")
    rest = _matmul(xb, wrest, "nn", F32, 512, 1408, D_MODEL, n + "proj_rest")
    bf = jnp.pad(sp["b_fgate"], (0, 128 - A_HEADS)).reshape(1, 128)
    fcum = _fox_gate_fwd(rest, bf, n + "fox_gate_fwd")
    fcol = jnp.repeat(fcum[:, :A_HEADS], 64, axis=1)
    qa, ka, va = _fox_prep_fwd(qkv, fcol, n + "fox_prep_fwd")
    ya, lse = _fox_fwd(qa, ka, va, n + "fox_fwd")
    lb = sp["lb"].reshape(1, B_WIDTH)
    ng = sp["norm_g"].reshape(1, HD)
    yb, ob, states = _hgrn_fwd(rest, lb, ng, n + "hgrn_fwd")
    merged = _merge_fwd(ya, yb, w["w_pa"], w["w_pb"], rest, n + "merge_fwd")
    x1, x1b, xh1, rs1 = _mm_res_ln(merged, w["w_out"], x, sp["ln1_g"], sp["ln1_b"], n + "out_ln1")
    h = _matmul(x1b, w["w_ff_in"], "nn", F32, 512, 1408, D_MODEL, n + "ffn_in")
    a = _swiglu_fwd(h, n + "swiglu_fwd")
    x2, x2b, xh2, rs2 = _mm_res_ln(a, w["w_ff_out"], x1, sp["ln2_g"], sp["ln2_b"], n + "ffn_out_ln2")
    saved = dict(xb=xb, wqkv=wqkv, wrest=wrest, qkv=qkv, rest=rest, bf=bf, fcol=fcol, ka=ka, va=va, ya=ya, lse=lse,
                 lb=lb, ng=ng, yb=yb, ob=ob, states=states, merged=merged, x1b=x1b, xh1=xh1, rs1=rs1, h=h, a=a,
                 xh2=xh2, rs2=rs2)
    return x2, x2b, saved


def _layer_bwd(dys, coefs, w, sp, s, l):
    n = f"l{l}_"
    dz2, dz2b, dg2, db2 = _ln_bwd(dys, coefs, s["xh2"], s["rs2"], sp["ln2_g"], n + "ln2_bwd")
    da = _matmul(dz2b, w["w_ff_out"], "nt", F32, 512, 1408, D_MODEL, n + "ffn_out_dx")
    d_wffout = _matmul(s["a"], dz2b, "tn", F32, 1408, 1024, 512, n + "ffn_out_dw")
    dh = _swiglu_bwd(da, s["h"], n + "swiglu_bwd")
    dx1f = _matmul(dh, w["w_ff_in"], "nt", F32, 512, 1024, 1408, n + "ffn_in_dx")
    d_wffin = _matmul(s["x1b"], dh, "tn", F32, 1024, 1408, 512, n + "ffn_in_dw")
    dz1, dz1b, dg1, db1 = _ln_bwd([dz2, dx1f], [ALPHA, 1.0], s["xh1"], s["rs1"], sp["ln1_g"], n + "ln1_bwd")
    dmerged = _matmul(dz1b, w["w_out"], "nt", F32, 512, 1024, D_MODEL, n + "out_dx")
    d_wout = _matmul(s["merged"], dz1b, "tn", F32, 1024, 1024, 512, n + "out_dw")
    dgates, dpa, dpb = _merge_bwd(dmerged, s["ya"], s["yb"], w["w_pa"], w["w_pb"], s["rest"], n + "merge_bwd")
    dya = _matmul(dpa, w["w_pa"], "nt", BF16, 512, 512, D_MODEL, n + "pa_dx")
    d_wpa = _matmul(s["ya"], dpa, "tn", F32, 512, 1024, 512, n + "pa_dw")
    dyb = _matmul(dpb, w["w_pb"], "nt", F32, 512, 512, D_MODEL, n + "pb_dx")
    d_wpb = _matmul(s["yb"], dpb, "tn", F32, 512, 1024, 512, n + "pb_dw")
    qb, dob = _fox_prep_bwd(s["qkv"], s["fcol"], s["lse"], dya, s["ya"], n + "fox_prep_bwd")
    dq, rsum = _fox_bwd_dq(qb, s["ka"], s["va"], dob, n + "fox_bwd_dq")
    dk, dv, csum = _fox_bwd_dkv(qb, s["ka"], s["va"], dob, n + "fox_bwd_dkv")
    d_fcum = jnp.pad(rsum[:, ::64] - csum[:, ::64], ((0, 0), (0, 128 - A_HEADS)))
    dfa, dbf = _fox_gate_bwd(d_fcum, s["rest"], s["bf"], n + "fox_gate_bwd")
    dhb, dlb, dng = _hgrn_bwd(dyb, s["rest"], s["ob"], s["states"], s["lb"], s["ng"], n + "hgrn_bwd")
    dproj = jnp.concatenate([dq, dk, dv, dgates, dhb, dfa], axis=1)
    wall = jnp.concatenate([s["wqkv"], s["wrest"]], axis=1)
    dxp = _matmul(dproj, wall, "nt", F32, 512, 1024, 1920, n + "proj_dx")
    d_wall = _matmul(s["xb"], dproj, "tn", F32, 1024, 1152, 512, n + "proj_dw")
    grads = dict(w_in=_merge_w_in_grad(d_wall), w_pa=d_wpa, w_pb=d_wpb, w_out=d_wout, w_ff_in=d_wffin,
                 w_ff_out=d_wffout, b_fgate=dbf[0, :A_HEADS], lb=dlb[0], norm_g=dng[0], ln1_g=dg1[0], ln1_b=db1[0],
                 ln2_g=dg2[0], ln2_b=db2[0])
    return [dz1, dxp], [ALPHA, 1.0], grads


def _lower_bounds(logits):
    sm = jax.nn.softmax(logits.astype(F32), axis=0)
    return jnp.cumsum(sm, axis=0) - sm[0:1]


def _local_step(x, target, wfull, small):
    lbs, lb_vjp = jax.vjp(_lower_bounds, small["hgrn_lb_logits"])
    h, hb = x, x.astype(BF16)
    saved, sps = [], []
    for l in range(DEPTH):
        sp = dict(b_fgate=small["b_fgate"][l], lb=lbs[l], norm_g=small["hgrn_norm_g"][l], ln1_g=small["ln1_g"][l],
                  ln1_b=small["ln1_b"][l], ln2_g=small["ln2_g"][l], ln2_b=small["ln2_b"][l])
        h, hb, s = _layer_fwd(h, hb, wfull[l], sp, l)
        saved.append(s)
        sps.append(sp)
    dy, lpart = _loss_head(h, target)
    dys, coefs = [dy], [1.0]
    grads = [None] * DEPTH
    for l in reversed(range(DEPTH)):
        dys, coefs, grads[l] = _layer_bwd(dys, coefs, wfull[l], sps[l], saved[l], l)
    grad_x = _axpy2(coefs[0], dys[0], coefs[1], dys[1], "grad_x")
    d_logits = lb_vjp(jnp.stack([grads[l]["lb"] for l in range(DEPTH)]))[0]
    return lpart[0, 0], grad_x, grads, d_logits


_BIG = [("w_in", "w_in", (D_MODEL, IN_TOTAL), 1), ("w_branch_a", "w_pa", (A_WIDTH, D_MODEL), 1),
        ("w_branch_b", "w_pb", (B_WIDTH, D_MODEL), 1), ("w_out", "w_out", (D_MODEL, D_MODEL), 0),
        ("w_ff_in", "w_ff_in", (D_MODEL, 2 * FFN_HIDDEN), 1), ("w_ff_out", "w_ff_out", (FFN_HIDDEN, D_MODEL), 0)]
_SMALL = [("b_fgate", A_HEADS), ("hgrn_lb_logits", B_WIDTH), ("hgrn_norm_g", HD), ("ln1_g", D_MODEL),
          ("ln1_b", D_MODEL), ("ln2_g", D_MODEL), ("ln2_b", D_MODEL)]
PACK_ROWS = 32768
SMALL_ROWS = 80


def _shard_shape(shape, axis):
    return tuple(s // N_CHIPS if a == axis else s for a, s in enumerate(shape))


def _pack_big(per_name):
    flat = jnp.concatenate([per_name[name].reshape(-1, 128) for name, _, _, _ in _BIG], axis=0)
    return jnp.pad(flat, ((0, PACK_ROWS - flat.shape[0]), (0, 0)))


def _unpack_big(slab):
    out, r = {}, 0
    for name, _, shape, axis in _BIG:
        shp = _shard_shape(shape, axis)
        n = shp[0] * shp[1] // 128
        out[name] = slab[r:r + n].reshape(shp)
        r += n
    return out


def _pack_small(per_name):
    flat = jnp.concatenate([per_name[name].reshape(-1) for name, _ in _SMALL])
    return jnp.pad(flat, (0, SMALL_ROWS * 128 - flat.shape[0])).reshape(SMALL_ROWS, 128)


def _unpack_small(slab):
    flat, out, r = slab.reshape(-1), {}, 0
    for name, n in _SMALL:
        out[name] = flat[r:r + DEPTH * n].reshape(DEPTH, n)
        r += DEPTH * n
    return out


_ANY = pl.BlockSpec(memory_space=pl.ANY)


def _place():
    return lax.axis_index("x"), lax.axis_index("y"), lax.axis_index("c")


def _other_chips(x, y):
    return [(1 - x, y), (x, 1 - y), (1 - x, 1 - y)]


def _chip_exchange(mine_of, out_ref, send_sems, recv_sems, local_sem):
    x, y, c = _place()
    q = 2 * x + y
    local = pltpu.make_async_copy(mine_of(q), out_ref.at[q], local_sem)
    local.start()
    sends = []
    for k, (px, py) in enumerate(_other_chips(x, y)):
        cp = pltpu.make_async_remote_copy(src_ref=mine_of(2 * px + py), dst_ref=out_ref.at[q], send_sem=send_sems.at[k],
                                          recv_sem=recv_sems.at[k], device_id=(px, py, c), device_id_type=MESH)
        cp.start()
        sends.append(cp)
    for k, (px, py) in enumerate(_other_chips(x, y)):
        pltpu.make_async_remote_copy(src_ref=mine_of(q), dst_ref=out_ref.at[2 * px + py], send_sem=send_sems.at[k],
                                     recv_sem=recv_sems.at[k], device_id=(px, py, c), device_id_type=MESH).wait_recv()
    for cp in sends:
        cp.wait_send()
    local.wait()


def _gather_weights(mine):
    def body(in_ref, out_ref, send_sems, recv_sems, local_sem):
        _chip_exchange(lambda q: in_ref, out_ref, send_sems, recv_sems, local_sem)

    return pl.pallas_call(
        body, name="gather_weights", in_specs=[_ANY], out_specs=_ANY,
        out_shape=jax.ShapeDtypeStruct((N_CHIPS,) + mine.shape, mine.dtype),
        scratch_shapes=[pltpu.SemaphoreType.DMA((3,)), pltpu.SemaphoreType.DMA((3,)), pltpu.SemaphoreType.DMA(())],
    )(mine)


def _pair_exchange(g):
    def body(g_ref, a_ref, send_sem, recv_sem):
        x, y, c = _place()
        cp = pltpu.make_async_remote_copy(src_ref=g_ref.at[1 - c], dst_ref=a_ref, send_sem=send_sem, recv_sem=recv_sem,
                                          device_id=(x, y, 1 - c), device_id_type=MESH)
        cp.start()
        cp.wait()

    return pl.pallas_call(
        body, name="grad_pair_exchange", in_specs=[_ANY], out_specs=_ANY,
        out_shape=jax.ShapeDtypeStruct(g.shape[1:], g.dtype),
        scratch_shapes=[pltpu.SemaphoreType.DMA(()), pltpu.SemaphoreType.DMA(())],
    )(g)


def _pair_sum(g, a, layer, tb=2048):
    rows = a.shape[0] * a.shape[1]
    g2 = g.reshape(DEPTH, rows, 128)
    a2 = a.reshape(rows, 128)

    def body(l_ref, g_ref, a_ref, o_ref):
        o_ref[...] = g_ref[...] + a_ref[...]

    out = pl.pallas_call(
        body, name="grad_pair_sum",
        grid_spec=pltpu.PrefetchScalarGridSpec(
            num_scalar_prefetch=1, grid=(rows // tb,),
            in_specs=[pl.BlockSpec((None, tb, 128), lambda i, l_ref: (l_ref[0], i, 0)),
                      pl.BlockSpec((tb, 128), lambda i, l_ref: (i, 0))],
            out_specs=pl.BlockSpec((tb, 128), lambda i, l_ref: (i, 0))),
        out_shape=jax.ShapeDtypeStruct((rows, 128), F32),
        compiler_params=_params("parallel"),
    )(layer.reshape(1).astype(jnp.int32), g2, a2)
    return out.reshape(a.shape)


def _shard_exchange(p):
    def body(p_ref, b_ref, send_sems, recv_sems, local_sem):
        _chip_exchange(lambda q: p_ref.at[q], b_ref, send_sems, recv_sems, local_sem)

    return pl.pallas_call(
        body, name="grad_shard_exchange", in_specs=[_ANY], out_specs=_ANY,
        out_shape=jax.ShapeDtypeStruct(p.shape, p.dtype),
        scratch_shapes=[pltpu.SemaphoreType.DMA((3,)), pltpu.SemaphoreType.DMA((3,)), pltpu.SemaphoreType.DMA(())],
    )(p)


def _sum4(b, tb=2048):
    rows = b.shape[1]

    def body(b_ref, o_ref):
        o_ref[...] = ((b_ref[0] + b_ref[1]) + b_ref[2]) + b_ref[3]

    return pl.pallas_call(
        body, name="grad_chip_sum", grid=(rows // tb,),
        in_specs=[pl.BlockSpec((N_CHIPS, tb, 128), lambda i: (0, i, 0))],
        out_specs=pl.BlockSpec((tb, 128), lambda i: (i, 0)),
        out_shape=jax.ShapeDtypeStruct((rows, 128), F32),
        compiler_params=_params("parallel"),
    )(b)


def _result_exchange(gc):
    def body(g_ref, o_ref, send_sem, recv_sem, local_sem):
        x, y, c = _place()
        local = pltpu.make_async_copy(g_ref, o_ref.at[c], local_sem)
        local.start()
        cp = pltpu.make_async_remote_copy(src_ref=g_ref, dst_ref=o_ref.at[c], send_sem=send_sem, recv_sem=recv_sem,
                                          device_id=(x, y, 1 - c), device_id_type=MESH)
        cp.start()
        pltpu.make_async_remote_copy(src_ref=g_ref, dst_ref=o_ref.at[1 - c], send_sem=send_sem, recv_sem=recv_sem,
                                     device_id=(x, y, 1 - c), device_id_type=MESH).wait_recv()
        cp.wait_send()
        local.wait()

    return pl.pallas_call(
        body, name="grad_result_exchange", in_specs=[_ANY], out_specs=_ANY,
        out_shape=jax.ShapeDtypeStruct((DEPTH,) + gc.shape, gc.dtype),
        scratch_shapes=[pltpu.SemaphoreType.DMA(()), pltpu.SemaphoreType.DMA(()), pltpu.SemaphoreType.DMA(())],
    )(gc)


def _allreduce_small(v):
    def body(v_ref, o_ref, buf, send_sems, recv_sems):
        x, y, c = _place()
        me = 4 * x + 2 * y + c
        buf[me] = v_ref[...]
        peers = []
        for k in range(1, N_DEV):
            px = 1 - x if k & 4 else x
            py = 1 - y if k & 2 else y
            pc = 1 - c if k & 1 else c
            peers.append((px, py, pc))
        sends = []
        for k, peer in enumerate(peers):
            cp = pltpu.make_async_remote_copy(src_ref=v_ref, dst_ref=buf.at[me], send_sem=send_sems.at[k],
                                              recv_sem=recv_sems.at[k], device_id=peer, device_id_type=MESH)
            cp.start()
            sends.append(cp)
        for k, (px, py, pc) in enumerate(peers):
            pltpu.make_async_remote_copy(src_ref=v_ref, dst_ref=buf.at[4 * px + 2 * py + pc], send_sem=send_sems.at[k],
                                         recv_sem=recv_sems.at[k], device_id=(px, py, pc),
                                         device_id_type=MESH).wait_recv()
        for cp in sends:
            cp.wait_send()
        acc = buf[0]
        for i in range(1, N_DEV):
            acc = acc + buf[i]
        o_ref[...] = acc

    vm = pl.BlockSpec(memory_space=pltpu.VMEM)
    return pl.pallas_call(
        body, name="small_allreduce", in_specs=[vm], out_specs=vm,
        out_shape=jax.ShapeDtypeStruct(v.shape, F32),
        scratch_shapes=[pltpu.VMEM((N_DEV,) + v.shape, F32), pltpu.SemaphoreType.DMA((N_DEV - 1,)),
                        pltpu.SemaphoreType.DMA((N_DEV - 1,))],
    )(v)


def _adamw(w, g, m, v, name, tb=2048):
    rows = w.shape[0]
    tb = min(tb, rows)
    c1 = 1.0 - ADAM_B1
    c2 = 1.0 - ADAM_B2
    bc1 = 1.0 - ADAM_B1 ** ADAM_STEP
    bc2 = 1.0 - ADAM_B2 ** ADAM_STEP

    def body(w_ref, g_ref, m_ref, v_ref, d_ref, nm_ref, nv_ref):
        gv = g_ref[...]
        nm = ADAM_B1 * m_ref[...] + c1 * gv
        nv = ADAM_B2 * v_ref[...] + c2 * (gv * gv)
        nm_ref[...] = nm
        nv_ref[...] = nv
        d_ref[...] = -ADAM_LR * ((nm / bc1) / (jnp.sqrt(nv / bc2) + ADAM_EPS) + ADAM_WD * w_ref[...])

    blk = pl.BlockSpec((tb, 128), lambda i: (i, 0))
    return pl.pallas_call(
        body, name=name, grid=(rows // tb,), in_specs=[blk] * 4, out_specs=[blk] * 3,
        out_shape=[jax.ShapeDtypeStruct((rows, 128), F32)] * 3,
        compiler_params=_params("parallel"),
    )(w, g, m, v)


def kernel(x, w_in, b_fgate, hgrn_lb_logits, hgrn_norm_g, w_branch_a, w_branch_b, w_out, ln1_g, ln1_b, w_ff_in, w_ff_out, ln2_g, ln2_b, loss_target, m_w_in, m_b_fgate, m_hgrn_lb_logits, m_hgrn_norm_g, m_w_branch_a, m_w_branch_b, m_w_out, m_ln1_g, m_ln1_b, m_w_ff_in, m_w_ff_out, m_ln2_g, m_ln2_b, v_w_in, v_b_fgate, v_hgrn_lb_logits, v_hgrn_norm_g, v_w_branch_a, v_w_branch_b, v_w_out, v_ln1_g, v_ln1_b, v_w_ff_in, v_w_ff_out, v_ln2_g, v_ln2_b):
    weights = dict(w_in=w_in, b_fgate=b_fgate, hgrn_lb_logits=hgrn_lb_logits, hgrn_norm_g=hgrn_norm_g,
                   w_branch_a=w_branch_a, w_branch_b=w_branch_b, w_out=w_out, ln1_g=ln1_g, ln1_b=ln1_b,
                   w_ff_in=w_ff_in, w_ff_out=w_ff_out, ln2_g=ln2_g, ln2_b=ln2_b)
    mom1 = dict(w_in=m_w_in, b_fgate=m_b_fgate, hgrn_lb_logits=m_hgrn_lb_logits, hgrn_norm_g=m_hgrn_norm_g,
                w_branch_a=m_w_branch_a, w_branch_b=m_w_branch_b, w_out=m_w_out, ln1_g=m_ln1_g, ln1_b=m_ln1_b,
                w_ff_in=m_w_ff_in, w_ff_out=m_w_ff_out, ln2_g=m_ln2_g, ln2_b=m_ln2_b)
    mom2 = dict(w_in=v_w_in, b_fgate=v_b_fgate, hgrn_lb_logits=v_hgrn_lb_logits, hgrn_norm_g=v_hgrn_norm_g,
                w_branch_a=v_w_branch_a, w_branch_b=v_w_branch_b, w_out=v_w_out, ln1_g=v_ln1_g, ln1_b=v_ln1_b,
                w_ff_in=v_w_ff_in, w_ff_out=v_w_ff_out, ln2_g=v_ln2_g, ln2_b=v_ln2_b)
    core = lax.axis_index("c")

    def pack_layers(tree):
        return jnp.stack([_pack_big({name: tree[name][l] for name, _, _, _ in _BIG}) for l in range(DEPTH)])

    w_packed = pack_layers(weights)
    gathered = _gather_weights(w_packed.astype(BF16))
    wfull = []
    for l in range(DEPTH):
        shards = [_unpack_big(gathered[q, l]) for q in range(N_CHIPS)]
        wfull.append({key: jnp.concatenate([shards[q][name] for q in range(N_CHIPS)], axis=axis)
                      for name, key, _, axis in _BIG})
    small = {name: weights[name] for name, _ in _SMALL}

    loss_part, grad_x, grads, d_logits = _local_step(x[0], loss_target[0], wfull, small)

    def shard_of(full, axis, q):
        n = full.shape[axis] // N_CHIPS
        return lax.slice_in_dim(full, q * n, (q + 1) * n, axis=axis)

    g_all = jnp.stack([jnp.stack([_pack_big({name: shard_of(grads[l][key], axis, q) for name, key, _, axis in _BIG})
                                  for q in range(N_CHIPS)]) for l in range(DEPTH)])
    received = _pair_exchange(g_all)
    pair = _pair_sum(g_all, received, core)
    by_chip = _shard_exchange(pair)
    g_layer = _sum4(by_chip)
    g_packed = _result_exchange(g_layer)

    rows = DEPTH * PACK_ROWS
    delta, new_m, new_v = _adamw(w_packed.reshape(rows, 128), g_packed.reshape(rows, 128),
                                 pack_layers(mom1).reshape(rows, 128), pack_layers(mom2).reshape(rows, 128), "adamw_big")

    def unpack_layers(slab):
        per = [_unpack_big(slab.reshape(DEPTH, PACK_ROWS, 128)[l]) for l in range(DEPTH)]
        return {name: jnp.stack([per[l][name] for l in range(DEPTH)]) for name, _, _, _ in _BIG}

    out_g, out_d, out_m, out_v = (unpack_layers(t) for t in (g_packed, delta, new_m, new_v))

    small_grads = {name: jnp.stack([grads[l][key] for l in range(DEPTH)])
                   for name, key in [("b_fgate", "b_fgate"), ("hgrn_norm_g", "norm_g"), ("ln1_g", "ln1_g"),
                                     ("ln1_b", "ln1_b"), ("ln2_g", "ln2_g"), ("ln2_b", "ln2_b")]}
    small_grads["hgrn_lb_logits"] = d_logits
    gs = _allreduce_small(_pack_small(small_grads))
    ds, ms, vs = _adamw(_pack_small(small), gs, _pack_small({n: mom1[n] for n, _ in _SMALL}),
                        _pack_small({n: mom2[n] for n, _ in _SMALL}), "adamw_small")
    for tree, slab in ((out_g, gs), (out_d, ds), (out_m, ms), (out_v, vs)):
        tree.update(_unpack_small(slab))

    loss = lax.psum(loss_part, ("x", "y", "c"))
    order = ["w_in", "b_fgate", "hgrn_lb_logits", "hgrn_norm_g", "w_branch_a", "w_branch_b", "w_out", "ln1_g", "ln1_b",
             "w_ff_in", "w_ff_out", "ln2_g", "ln2_b"]
    return (loss, grad_x[None], *[out_g[n] for n in order], *[out_d[n] for n in order],
            *[out_m[n] for n in order], *[out_v[n] for n in order])
```

```python
import functools
import math

import jax
import jax.numpy as jnp
import numpy as np
from jax import lax
from jax.experimental import pallas as pl
from jax.experimental.pallas import tpu as pltpu

F32 = jnp.float32
BF16 = jnp.bfloat16

D_MODEL = 1024
DEPTH = 2
A_HEADS = 8
A_WIDTH = 512
B_WIDTH = 512
B_HEADS = 4
HD = 128
CHUNK = 64
SUB = 16
FFN_HIDDEN = 2816
IN_TOTAL = 5640
ALPHA = (2 * DEPTH) ** 0.25
LN_EPS = 1e-5
RMS_EPS = 1e-6
ADAM_LR = 0.001
ADAM_B1 = 0.9
ADAM_B2 = 0.999
ADAM_EPS = 1e-08
ADAM_WD = 0.01
ADAM_STEP = 10
EXP_CLAMP = 60.0

VMEM_LIMIT_BYTES = 56 * 1024 * 1024
N_CHIPS = 4
N_DEV = 8
MESH = pl.DeviceIdType.MESH

_DN = {
    "nn": (((1,), (0,)), ((), ())),
    "nt": (((1,), (1,)), ((), ())),
    "tn": (((0,), (0,)), ((), ())),
}


def _dot(a, b, mode="nn"):
    return lax.dot_general(a.astype(BF16), b.astype(BF16), _DN[mode], preferred_element_type=F32)


def _dot_hi(a, b, mode="nn"):
    return lax.dot_general(a, b, _DN[mode], precision=lax.Precision.HIGHEST, preferred_element_type=F32)


def _hdot(a, b, mode="nn"):
    return _dot_hi(a.astype(F32), b.astype(F32), mode)


def _params(*sem):
    return pltpu.CompilerParams(dimension_semantics=sem, vmem_limit_bytes=VMEM_LIMIT_BYTES)


def _sigmoid(x):
    return 1.0 / (1.0 + jnp.exp(-x))


def _matmul(a, b, mode, out_dtype, tm, tn, tk, name):
    if mode == "nn":
        (m, k), (k2, n) = a.shape, b.shape
    elif mode == "nt":
        (m, k), (n, k2) = a.shape, b.shape
    else:
        (k, m), (k2, n) = a.shape, b.shape
    assert k == k2, (a.shape, b.shape, mode)
    tm, tn, tk = min(tm, m), min(tn, n), min(tk, k)
    assert m % tm == 0 and n % tn == 0 and k % tk == 0, (a.shape, b.shape, tm, tn, tk)
    nk = k // tk
    if mode == "tn":
        a_spec = pl.BlockSpec((tk, tm), lambda j, i, kk: (kk, i))
    else:
        a_spec = pl.BlockSpec((tm, tk), lambda j, i, kk: (i, kk))
    if mode == "nt":
        b_spec = pl.BlockSpec((tn, tk), lambda j, i, kk: (j, kk))
    else:
        b_spec = pl.BlockSpec((tk, tn), lambda j, i, kk: (kk, j))
    use_acc = nk > 1 and out_dtype != F32

    def body(a_ref, b_ref, o_ref, *scratch):
        p = _dot(a_ref[...], b_ref[...], mode)
        if nk == 1:
            o_ref[...] = p.astype(out_dtype)
            return
        acc_ref = scratch[0] if use_acc else o_ref
        kk = pl.program_id(2)

        @pl.when(kk == 0)
        def _():
            acc_ref[...] = p

        @pl.when(kk > 0)
        def _():
            acc_ref[...] += p

        if use_acc:
            @pl.when(kk == nk - 1)
            def _():
                o_ref[...] = acc_ref[...].astype(out_dtype)

    return pl.pallas_call(
        body,
        name=name,
        grid=(n // tn, m // tm, nk),
        in_specs=[a_spec, b_spec],
        out_specs=pl.BlockSpec((tm, tn), lambda j, i, kk: (i, j)),
        out_shape=jax.ShapeDtypeStruct((m, n), out_dtype),
        scratch_shapes=[pltpu.VMEM((tm, tn), F32)] if use_acc else [],
        compiler_params=_params("parallel", "parallel", "arbitrary"),
    )(a, b)


def _mm_res_ln(a, w, res, g, b, name, tm=512):
    t, k = a.shape
    d = w.shape[1]
    tm = min(tm, t)

    def body(a_ref, w_ref, r_ref, g_ref, b_ref, y_ref, yb_ref, xh_ref, rs_ref):
        z = ALPHA * r_ref[...] + _dot(a_ref[...], w_ref[...])
        mu = jnp.mean(z, axis=-1, keepdims=True)
        zc = z - mu
        var = jnp.mean(zc * zc, axis=-1, keepdims=True)
        rstd = lax.rsqrt(var + LN_EPS)
        xh = zc * rstd
        y = xh * g_ref[...] + b_ref[...]
        y_ref[...] = y
        yb_ref[...] = y.astype(BF16)
        xh_ref[...] = xh
        rs_ref[...] = rstd

    row = lambda i: (i, 0)
    fix = lambda i: (0, 0)
    return pl.pallas_call(
        body,
        name=name,
        grid=(t // tm,),
        in_specs=[pl.BlockSpec((tm, k), row), pl.BlockSpec((k, d), fix), pl.BlockSpec((tm, d), row),
                  pl.BlockSpec((1, d), fix), pl.BlockSpec((1, d), fix)],
        out_specs=[pl.BlockSpec((tm, d), row), pl.BlockSpec((tm, d), row), pl.BlockSpec((tm, d), row),
                   pl.BlockSpec((tm, 1), row)],
        out_shape=[jax.ShapeDtypeStruct((t, d), F32), jax.ShapeDtypeStruct((t, d), BF16),
                   jax.ShapeDtypeStruct((t, d), F32), jax.ShapeDtypeStruct((t, 1), F32)],
        compiler_params=_params("parallel"),
    )(a, w, res, g.reshape(1, d), b.reshape(1, d))


def _ln_bwd(dys, coefs, xhat, rstd, g, name, tm=512):
    t, d = xhat.shape
    tm = min(tm, t)
    n_in = len(dys)

    def body(*refs):
        dy_refs = refs[:n_in]
        xh_ref, rs_ref, g_ref, dz_ref, dzb_ref, dg_ref, db_ref = refs[n_in:]
        dy = coefs[0] * dy_refs[0][...].astype(F32)
        for c, r in zip(coefs[1:], dy_refs[1:]):
            dy = dy + c * r[...].astype(F32)
        xh = xh_ref[...]
        dxh = dy * g_ref[...]
        m1 = jnp.mean(dxh, axis=-1, keepdims=True)
        m2 = jnp.mean(dxh * xh, axis=-1, keepdims=True)
        dz = rs_ref[...] * (dxh - m1 - xh * m2)
        dz_ref[...] = dz
        dzb_ref[...] = dz.astype(BF16)
        pg = jnp.sum(dy * xh, axis=0, keepdims=True)
        pb = jnp.sum(dy, axis=0, keepdims=True)

        @pl.when(pl.program_id(0) == 0)
        def _():
            dg_ref[...] = pg
            db_ref[...] = pb

        @pl.when(pl.program_id(0) > 0)
        def _():
            dg_ref[...] += pg
            db_ref[...] += pb

    row = lambda i: (i, 0)
    fix = lambda i: (0, 0)
    return pl.pallas_call(
        body,
        name=name,
        grid=(t // tm,),
        in_specs=[pl.BlockSpec((tm, d), row)] * n_in
        + [pl.BlockSpec((tm, d), row), pl.BlockSpec((tm, 1), row), pl.BlockSpec((1, d), fix)],
        out_specs=[pl.BlockSpec((tm, d), row), pl.BlockSpec((tm, d), row), pl.BlockSpec((1, d), fix),
                   pl.BlockSpec((1, d), fix)],
        out_shape=[jax.ShapeDtypeStruct((t, d), F32), jax.ShapeDtypeStruct((t, d), BF16),
                   jax.ShapeDtypeStruct((1, d), F32), jax.ShapeDtypeStruct((1, d), F32)],
        compiler_params=_params("arbitrary"),
    )(*dys, xhat, rstd, g.reshape(1, d))


def _loss_head(y, target, name="loss_head", tm=512):
    t, d = y.shape
    tm = min(tm, t)

    def body(y_ref, t_ref, dy_ref, l_ref):
        e = y_ref[...] - t_ref[...]
        dy_ref[...] = e * (1.0 / d)
        part = jnp.full((8, 128), 0.5 / d, F32) * jnp.sum(e * e)

        @pl.when(pl.program_id(0) == 0)
        def _():
            l_ref[...] = part

        @pl.when(pl.program_id(0) > 0)
        def _():
            l_ref[...] += part

    row = lambda i: (i, 0)
    return pl.pallas_call(
        body,
        name=name,
        grid=(t // tm,),
        in_specs=[pl.BlockSpec((tm, d), row), pl.BlockSpec((tm, d), row)],
        out_specs=[pl.BlockSpec((tm, d), row), pl.BlockSpec((8, 128), lambda i: (0, 0))],
        out_shape=[jax.ShapeDtypeStruct((t, d), F32), jax.ShapeDtypeStruct((8, 128), F32)],
        compiler_params=_params("arbitrary"),
    )(y, target)


def _swiglu_fwd(h, name, tm=512):
    t = h.shape[0]
    tm = min(tm, t)
    wb = FFN_HIDDEN // 2

    def body(u_ref, g_ref, a_ref):
        g = g_ref[...]
        a_ref[...] = (g * _sigmoid(g) * u_ref[...]).astype(BF16)

    return pl.pallas_call(
        body,
        name=name,
        grid=(t // tm, 2),
        in_specs=[pl.BlockSpec((tm, wb), lambda i, j: (i, j)), pl.BlockSpec((tm, wb), lambda i, j: (i, j + 2))],
        out_specs=pl.BlockSpec((tm, wb), lambda i, j: (i, j)),
        out_shape=jax.ShapeDtypeStruct((t, FFN_HIDDEN), BF16),
        compiler_params=_params("parallel", "parallel"),
    )(h, h)


def _swiglu_bwd(da, h, name, tm=512):
    t = h.shape[0]
    tm = min(tm, t)
    wb = FFN_HIDDEN // 2

    def body(da_ref, u_ref, g_ref, dh_ref):
        g = g_ref[...]
        sg = _sigmoid(g)
        da = da_ref[...]

        @pl.when(pl.program_id(1) < 2)
        def _():
            dh_ref[...] = (da * g * sg).astype(BF16)

        @pl.when(pl.program_id(1) >= 2)
        def _():
            dh_ref[...] = (da * u_ref[...] * (sg * (1.0 + g * (1.0 - sg)))).astype(BF16)

    return pl.pallas_call(
        body,
        name=name,
        grid=(t // tm, 4),
        in_specs=[pl.BlockSpec((tm, wb), lambda i, j: (i, j % 2)), pl.BlockSpec((tm, wb), lambda i, j: (i, j % 2)),
                  pl.BlockSpec((tm, wb), lambda i, j: (i, 2 + j % 2))],
        out_specs=pl.BlockSpec((tm, wb), lambda i, j: (i, j)),
        out_shape=jax.ShapeDtypeStruct((t, 2 * FFN_HIDDEN), BF16),
        compiler_params=_params("parallel", "parallel"),
    )(da, h, h)


def _merge_fwd(ya, yb, wpa, wpb, rest, name, tm=512):
    t = ya.shape[0]
    tm = min(tm, t)

    def body(ya_ref, yb_ref, wa_ref, wb_ref, ga_ref, gb_ref, o_ref):
        pa = _dot(ya_ref[...], wa_ref[...])
        pb = _dot(yb_ref[...], wb_ref[...])
        o_ref[...] = (_sigmoid(ga_ref[...]) * pa + _sigmoid(gb_ref[...]) * pb).astype(BF16)

    row = lambda i: (i, 0)
    fix = lambda i: (0, 0)
    return pl.pallas_call(
        body,
        name=name,
        grid=(t // tm,),
        in_specs=[pl.BlockSpec((tm, A_WIDTH), row), pl.BlockSpec((tm, B_WIDTH), row),
                  pl.BlockSpec((A_WIDTH, D_MODEL), fix), pl.BlockSpec((B_WIDTH, D_MODEL), fix),
                  pl.BlockSpec((tm, D_MODEL), lambda i: (i, 0)), pl.BlockSpec((tm, D_MODEL), lambda i: (i, 1))],
        out_specs=pl.BlockSpec((tm, D_MODEL), row),
        out_shape=jax.ShapeDtypeStruct((t, D_MODEL), BF16),
        compiler_params=_params("parallel"),
    )(ya, yb, wpa, wpb, rest, rest)


def _merge_bwd(dm, ya, yb, wpa, wpb, rest, name, tm=512):
    t = ya.shape[0]
    tm = min(tm, t)

    def body(dm_ref, ya_ref, yb_ref, wa_ref, wb_ref, ga_ref, gb_ref, dg_ref, dpa_ref, dpb_ref):
        dm_v = dm_ref[...]
        pa = _dot(ya_ref[...], wa_ref[...])
        pb = _dot(yb_ref[...], wb_ref[...])
        sa = _sigmoid(ga_ref[...])
        sb = _sigmoid(gb_ref[...])
        dg_ref[:, :D_MODEL] = (dm_v * pa * sa * (1.0 - sa)).astype(BF16)
        dg_ref[:, D_MODEL:] = (dm_v * pb * sb * (1.0 - sb)).astype(BF16)
        dpa_ref[...] = (dm_v * sa).astype(BF16)
        dpb_ref[...] = (dm_v * sb).astype(BF16)

    row = lambda i: (i, 0)
    fix = lambda i: (0, 0)
    return pl.pallas_call(
        body,
        name=name,
        grid=(t // tm,),
        in_specs=[pl.BlockSpec((tm, D_MODEL), row), pl.BlockSpec((tm, A_WIDTH), row), pl.BlockSpec((tm, B_WIDTH), row),
                  pl.BlockSpec((A_WIDTH, D_MODEL), fix), pl.BlockSpec((B_WIDTH, D_MODEL), fix),
                  pl.BlockSpec((tm, D_MODEL), lambda i: (i, 0)), pl.BlockSpec((tm, D_MODEL), lambda i: (i, 1))],
        out_specs=[pl.BlockSpec((tm, 2 * D_MODEL), row), pl.BlockSpec((tm, D_MODEL), row),
                   pl.BlockSpec((tm, D_MODEL), row)],
        out_shape=[jax.ShapeDtypeStruct((t, 2 * D_MODEL), BF16), jax.ShapeDtypeStruct((t, D_MODEL), BF16),
                   jax.ShapeDtypeStruct((t, D_MODEL), BF16)],
        compiler_params=_params("parallel"),
    )(dm, ya, yb, wpa, wpb, rest, rest)


FA_BLOCK = 4224 // 128 - 1


def _tri(n, lower):
    r = lax.broadcasted_iota(jnp.int32, (n, n), 0)
    c = lax.broadcasted_iota(jnp.int32, (n, n), 1)
    return jnp.where((r >= c) if lower else (r <= c), 1.0, 0.0).astype(F32)


def _fox_gate_fwd(rest, bf, name, tb=512):
    t = rest.shape[0]
    tb = min(tb, t)

    def body(fa_ref, bf_ref, f_ref, carry):
        @pl.when(pl.program_id(0) == 0)
        def _():
            carry[...] = jnp.zeros_like(carry)

        z = fa_ref[...] + bf_ref[...]
        logf = jnp.minimum(z, 0.0) - jnp.log(1.0 + jnp.exp(-jnp.abs(z)))
        f = _dot_hi(_tri(tb, True), logf) + carry[...]
        f_ref[...] = f
        carry[...] = f[tb - 1:tb, :]

    return pl.pallas_call(
        body,
        name=name,
        grid=(t // tb,),
        in_specs=[pl.BlockSpec((tb, 128), lambda i: (i, FA_BLOCK)), pl.BlockSpec((1, 128), lambda i: (0, 0))],
        out_specs=pl.BlockSpec((tb, 128), lambda i: (i, 0)),
        out_shape=jax.ShapeDtypeStruct((t, 128), F32),
        scratch_shapes=[pltpu.VMEM((1, 128), F32)],
        compiler_params=_params("arbitrary"),
    )(rest, bf)


def _fox_gate_bwd(d_f, rest, bf, name, tb=512):
    t = rest.shape[0]
    tb = min(tb, t)
    nb = t // tb

    def body(df_ref, fa_ref, bf_ref, dfa_ref, dbf_ref, carry):
        @pl.when(pl.program_id(0) == 0)
        def _():
            carry[...] = jnp.zeros_like(carry)

        dlogf = _dot_hi(_tri(tb, False), df_ref[...]) + carry[...]
        carry[...] = dlogf[0:1, :]
        z = fa_ref[...] + bf_ref[...]
        dz = dlogf * _sigmoid(-z)
        dfa_ref[...] = dz.astype(BF16)
        part = jnp.sum(dz, axis=0, keepdims=True)

        @pl.when(pl.program_id(0) == 0)
        def _():
            dbf_ref[...] = part

        @pl.when(pl.program_id(0) > 0)
        def _():
            dbf_ref[...] += part

    return pl.pallas_call(
        body,
        name=name,
        grid=(nb,),
        in_specs=[pl.BlockSpec((tb, 128), lambda i: (nb - 1 - i, 0)),
                  pl.BlockSpec((tb, 128), lambda i: (nb - 1 - i, FA_BLOCK)),
                  pl.BlockSpec((1, 128), lambda i: (0, 0))],
        out_specs=[pl.BlockSpec((tb, 128), lambda i: (nb - 1 - i, 0)), pl.BlockSpec((1, 128), lambda i: (0, 0))],
        out_shape=[jax.ShapeDtypeStruct((t, 128), BF16), jax.ShapeDtypeStruct((1, 128), F32)],
        scratch_shapes=[pltpu.VMEM((1, 128), F32)],
        compiler_params=_params("arbitrary"),
    )(d_f, rest, bf)


ATT_BLOCK = 512


def _head_mask(shape, j):
    lane = lax.broadcasted_iota(jnp.int32, shape, 1)
    return (lane < 64) if j == 0 else (lane >= 64)


def _split3(x):
    h = x.astype(BF16).astype(F32)
    r = x - h
    m = r.astype(BF16).astype(F32)
    return h, m, (r - m).astype(BF16).astype(F32)


def _aug_lanes(tb, j):
    lane = lax.broadcasted_iota(jnp.int32, (tb, 128), 1)
    own = (lane < 64) if j == 0 else (lane >= 64)
    return own, lane - 64 * (1 - j)


def _aug_query(own, li, q, bias):
    h, m, l = _split3(bias)
    spare = jnp.where(li == 0, h, jnp.where(li == 1, m, jnp.where(li == 2, l, jnp.where(li < 6, 1.0, 0.0))))
    return jnp.where(own, q, spare).astype(BF16)


def _fox_prep_fwd(qkv, fcol, name, tb=512):
    t = qkv.shape[0]
    tb = min(tb, t)

    def body(q_ref, k_ref, v_ref, fc_ref, qa_ref, ka_ref, va_ref):
        fsw = pltpu.roll(fc_ref[...], 64, 1)
        q = q_ref[...].astype(F32) * 0.125
        k = k_ref[...].astype(F32)
        v = v_ref[...].astype(F32)
        h, m, l = _split3(fsw)
        for j in (0, 1):
            own, li = _aug_lanes(tb, j)
            cols = slice(128 * j, 128 * (j + 1))
            qa_ref[:, cols] = _aug_query(own, li, q, fsw)
            ks = jnp.where(li < 3, 1.0, jnp.where(li == 3, -h, jnp.where(li == 4, -m, jnp.where(li == 5, -l, 0.0))))
            ka_ref[:, cols] = jnp.where(own, k, ks).astype(BF16)
            va_ref[:, cols] = jnp.where(own, v, 1.0).astype(BF16)

    blk = pl.BlockSpec((tb, 256), lambda i, h: (i, h))
    return pl.pallas_call(
        body, name=name, grid=(t // tb, 4),
        in_specs=[pl.BlockSpec((tb, 128), lambda i, h: (i, h)), pl.BlockSpec((tb, 128), lambda i, h: (i, 4 + h)),
                  pl.BlockSpec((tb, 128), lambda i, h: (i, 8 + h)), pl.BlockSpec((tb, 128), lambda i, h: (i, h))],
        out_specs=[blk, blk, blk],
        out_shape=[jax.ShapeDtypeStruct((t, 2 * A_WIDTH), BF16)] * 3,
        compiler_params=_params("parallel", "parallel"),
    )(qkv, qkv, qkv, fcol)


def _fox_prep_bwd(qkv, fcol, lse, do, o, name, tb=512):
    t = qkv.shape[0]
    tb = min(tb, t)

    def body(q_ref, fc_ref, lse_ref, do_ref, o_ref, qb_ref, dob_ref):
        gsw = pltpu.roll(fc_ref[...] - lse_ref[...], 64, 1)
        q = q_ref[...].astype(F32) * 0.125
        do_v = do_ref[...].astype(F32)
        prod = do_v * o_ref[...].astype(F32)
        for j in (0, 1):
            own, li = _aug_lanes(tb, j)
            cols = slice(128 * j, 128 * (j + 1))
            qb_ref[:, cols] = _aug_query(own, li, q, gsw)
            delta = jnp.sum(jnp.where(own, prod, 0.0), axis=1, keepdims=True)
            h, m, l = _split3(jnp.broadcast_to(delta, (tb, 128)))
            ds = jnp.where(li == 0, -h, jnp.where(li == 1, -m, jnp.where(li == 2, -l, 0.0)))
            dob_ref[:, cols] = jnp.where(own, do_v, ds).astype(BF16)

    pair = pl.BlockSpec((tb, 128), lambda i, h: (i, h))
    blk = pl.BlockSpec((tb, 256), lambda i, h: (i, h))
    return pl.pallas_call(
        body, name=name, grid=(t // tb, 4),
        in_specs=[pair, pair, pair, pair, pair],
        out_specs=[blk, blk],
        out_shape=[jax.ShapeDtypeStruct((t, 2 * A_WIDTH), BF16)] * 2,
        compiler_params=_params("parallel", "parallel"),
    )(qkv, fcol, lse, do, o)


def _tile_mask(n, transposed):
    r = lax.broadcasted_iota(jnp.int32, (n, n), 0)
    c = lax.broadcasted_iota(jnp.int32, (n, n), 1)
    return (c >= r) if transposed else (r >= c)


UNDERFLOW = -110.0


def _fox_block_ranges(qkv, fcum):
    t = qkv.shape[0]
    blk = min(ATT_BLOCK, t)
    nb = t // blk
    q2 = jnp.sum(jnp.square(qkv[:, :A_WIDTH].astype(F32)).reshape(t, A_HEADS, 64), axis=-1)
    k2 = jnp.sum(jnp.square(qkv[:, A_WIDTH:2 * A_WIDTH].astype(F32)).reshape(t, A_HEADS, 64), axis=-1)
    bound = 2.0 * jnp.sqrt(jnp.max(q2, axis=0) * jnp.max(k2, axis=0)) * 0.125
    f = fcum[:, :A_HEADS]
    first = f[0::blk].T
    last = f[blk - 1::blk].T
    dead = (bound[:, None, None] + first[:, :, None] - last[:, None, :]) < UNDERFLOW
    qi = jnp.arange(nb)[None, :, None]
    kj = jnp.arange(nb)[None, None, :]
    dead = dead & (kj < qi)
    kstart = jnp.sum(dead, axis=2).astype(jnp.int32)
    qend = (kj[0] + jnp.sum((~dead) & (qi > kj), axis=1)).astype(jnp.int32)
    return kstart.reshape(-1), qend.reshape(-1)


def _fox_fwd(qa, ka, va, kstart, name):
    t = qa.shape[0]
    bq = min(ATT_BLOCK, t)
    nq = t // bq

    def body(ks_ref, q_ref, k_ref, v_ref, o_ref, lse_ref):
        i = pl.program_id(1)
        outs = []
        for j in (0, 1):
            cols = slice(128 * j, 128 * (j + 1))
            qj = q_ref[:, cols]
            k0 = ks_ref[(2 * pl.program_id(0) + j) * nq + i]

            def step(kb, carry, masked, qj=qj, cols=cols):
                m, acc = carry
                rows = pl.ds(pl.multiple_of(kb * bq, bq), bq)
                s = _dot(qj, k_ref[rows, cols], "nt")
                if masked:
                    s = jnp.where(_tile_mask(bq, False), s, -jnp.inf)
                m_new = jnp.maximum(m, jnp.max(s, axis=1, keepdims=True))
                acc = jnp.exp(m - m_new) * acc + _dot(jnp.exp(s - m_new), v_ref[rows, cols])
                return m_new, acc

            carry = (jnp.full((bq, 1), -jnp.inf, F32), jnp.zeros((bq, 128), F32))
            carry = lax.fori_loop(k0, i, lambda kb, c, step=step: step(kb, c, False), carry)
            m, acc = step(i, carry, True)
            spare = 64 * (1 - j)
            l = acc[:, spare:spare + 1]
            outs.append((acc / l, m + jnp.log(l)))
        msk = _head_mask((bq, 128), 0)
        o_ref[...] = jnp.where(msk, outs[0][0], outs[1][0]).astype(BF16)
        lse_ref[...] = jnp.where(msk, outs[0][1], outs[1][1])

    res = pl.BlockSpec((t, 256), lambda h, i, tbl: (0, h))
    out = pl.BlockSpec((bq, 128), lambda h, i, tbl: (i, h))
    return pl.pallas_call(
        body,
        name=name,
        grid_spec=pltpu.PrefetchScalarGridSpec(
            num_scalar_prefetch=1, grid=(4, nq),
            in_specs=[pl.BlockSpec((bq, 256), lambda h, i, tbl: (i, h)), res, res],
            out_specs=[out, out]),
        out_shape=[jax.ShapeDtypeStruct((t, A_WIDTH), BF16), jax.ShapeDtypeStruct((t, A_WIDTH), F32)],
        compiler_params=_params("parallel", "parallel"),
    )(kstart, qa, ka, va)


def _fox_bwd_dq(qb, ka, va, dob, kstart, name):
    t = qb.shape[0]
    bq = min(ATT_BLOCK, t)
    nq = t // bq

    def body(ks_ref, q_ref, k_ref, v_ref, do_ref, dq_ref, rs_ref):
        i = pl.program_id(1)
        outs = []
        for j in (0, 1):
            cols = slice(128 * j, 128 * (j + 1))
            qj = q_ref[:, cols]
            doj = do_ref[:, cols]
            k0 = ks_ref[(2 * pl.program_id(0) + j) * nq + i]

            def step(kb, acc, masked, qj=qj, doj=doj, cols=cols):
                rows = pl.ds(pl.multiple_of(kb * bq, bq), bq)
                ks = k_ref[rows, cols]
                s = _dot(qj, ks, "nt")
                if masked:
                    s = jnp.where(_tile_mask(bq, False), s, -jnp.inf)
                ds = jnp.exp(s) * _dot(doj, v_ref[rows, cols], "nt")
                return acc + _dot(ds, ks)

            acc = lax.fori_loop(k0, i, lambda kb, c, step=step: step(kb, c, False), jnp.zeros((bq, 128), F32))
            acc = step(i, acc, True)
            spare = 64 * (1 - j)
            outs.append((acc * 0.125, acc[:, spare:spare + 1]))
        msk = _head_mask((bq, 128), 0)
        dq_ref[...] = jnp.where(msk, outs[0][0], outs[1][0]).astype(BF16)
        rs_ref[...] = jnp.where(msk, outs[0][1], outs[1][1])

    blk = pl.BlockSpec((bq, 256), lambda h, i, tbl: (i, h))
    res = pl.BlockSpec((t, 256), lambda h, i, tbl: (0, h))
    out = pl.BlockSpec((bq, 128), lambda h, i, tbl: (i, h))
    return pl.pallas_call(
        body,
        name=name,
        grid_spec=pltpu.PrefetchScalarGridSpec(
            num_scalar_prefetch=1, grid=(4, nq), in_specs=[blk, res, res, blk], out_specs=[out, out]),
        out_shape=[jax.ShapeDtypeStruct((t, A_WIDTH), BF16), jax.ShapeDtypeStruct((t, A_WIDTH), F32)],
        compiler_params=_params("parallel", "parallel"),
    )(kstart, qb, ka, va, dob)


def _fox_bwd_dkv(qb, ka, va, dob, qend, name):
    t = qb.shape[0]
    bk = min(ATT_BLOCK, t)
    nk = t // bk

    def body(qe_ref, k_ref, v_ref, q_ref, do_ref, dk_ref, dv_ref, cs_ref):
        jb = pl.program_id(1)
        outs = []
        for j in (0, 1):
            cols = slice(128 * j, 128 * (j + 1))
            kj = k_ref[:, cols]
            vj = v_ref[:, cols]
            i1 = qe_ref[(2 * pl.program_id(0) + j) * nk + jb] + 1

            def step(ib, carry, masked, kj=kj, vj=vj, cols=cols):
                dk_acc, dv_acc = carry
                rows = pl.ds(pl.multiple_of(ib * bk, bk), bk)
                qs = q_ref[rows, cols]
                dos = do_ref[rows, cols]
                st = _dot(kj, qs, "nt")
                if masked:
                    st = jnp.where(_tile_mask(bk, True), st, -jnp.inf)
                pt = jnp.exp(st)
                dv_acc = dv_acc + _dot(pt, dos)
                dk_acc = dk_acc + _dot(pt * _dot(vj, dos, "nt"), qs)
                return dk_acc, dv_acc

            carry = step(jb, (jnp.zeros((bk, 128), F32), jnp.zeros((bk, 128), F32)), True)
            dk_acc, dv_acc = lax.fori_loop(jb + 1, i1, lambda ib, c, step=step: step(ib, c, False), carry)
            spare = 64 * (1 - j)
            outs.append((dk_acc, dv_acc, dk_acc[:, spare + 3:spare + 4]))
        msk = _head_mask((bk, 128), 0)
        dk_ref[...] = jnp.where(msk, outs[0][0], outs[1][0]).astype(BF16)
        dv_ref[...] = jnp.where(msk, outs[0][1], outs[1][1]).astype(BF16)
        cs_ref[...] = jnp.where(msk, outs[0][2], outs[1][2])

    blk = pl.BlockSpec((bk, 256), lambda h, i, tbl: (i, h))
    res = pl.BlockSpec((t, 256), lambda h, i, tbl: (0, h))
    out = pl.BlockSpec((bk, 128), lambda h, i, tbl: (i, h))
    return pl.pallas_call(
        body,
        name=name,
        grid_spec=pltpu.PrefetchScalarGridSpec(
            num_scalar_prefetch=1, grid=(4, nk), in_specs=[blk, blk, res, res], out_specs=[out, out, out]),
        out_shape=[jax.ShapeDtypeStruct((t, A_WIDTH), BF16), jax.ShapeDtypeStruct((t, A_WIDTH), BF16),
                   jax.ShapeDtypeStruct((t, A_WIDTH), F32)],
        compiler_params=_params("parallel", "parallel"),
    )(qend, ka, va, qb, dob)


HG_ROWS = 256


def _hg_gates(hb_ref, rows, h, lbh):
    qb = hb_ref[rows, h * HD:(h + 1) * HD]
    fb = hb_ref[rows, B_WIDTH + h * HD:B_WIDTH + (h + 1) * HD]
    v = hb_ref[rows, 2 * B_WIDTH + h * HD:2 * B_WIDTH + (h + 1) * HD]
    gb = hb_ref[rows, 3 * B_WIDTH + h * HD:3 * B_WIDTH + (h + 1) * HD]
    sg = _sigmoid(fb)
    f = lbh + (1.0 - lbh) * sg
    sq = _sigmoid(qb)
    return qb, sq, qb * sq, sg, f, 1.0 - f, jnp.log(f), v, gb


def _hg_intra_factors(q, k, b):
    fac = []
    for i in range(CHUNK // SUB):
        bi = b[SUB * i:SUB * i + 1, :]
        eq = jnp.exp(b[SUB * i:SUB * (i + 1), :] - bi)
        ek = jnp.exp(jnp.minimum(bi - b, EXP_CLAMP))
        fac.append((eq, ek, q[SUB * i:SUB * (i + 1), :] * eq, k * ek))
    return fac


def _causal(n):
    r = lax.broadcasted_iota(jnp.int32, (n, n), 0)
    c = lax.broadcasted_iota(jnp.int32, (n, n), 1)
    return r >= c


def _hgrn_fwd(rest, lb, ng, name):
    t = rest.shape[0]
    bt = min(HG_ROWS, t)
    ncb = bt // CHUNK

    def body(hb_ref, lb_ref, ng_ref, y_ref, o_ref, st_ref, s_scr):
        @pl.when(pl.program_id(0) == 0)
        def _():
            s_scr[...] = jnp.zeros_like(s_scr)

        tril = _tri(CHUNK, True)
        causal = _causal(CHUNK)
        ones = jnp.ones((CHUNK, HD), F32)

        def chunk(c, carry):
            rows = pl.ds(pl.multiple_of(c * CHUNK, CHUNK), CHUNK)
            for h in range(B_HEADS):
                lbh = lb_ref[:, h * HD:(h + 1) * HD]
                _, _, q, _, _, k, g, v, gb = _hg_gates(hb_ref, rows, h, lbh)
                b = _dot_hi(tril, g)
                s0 = s_scr[h]
                st_ref[c, h] = s0
                o = _dot(q * jnp.exp(b), s0)
                a = jnp.concatenate([_dot(qe, ke, "nt") for _, _, qe, ke in _hg_intra_factors(q, k, b)], axis=0)
                o = o + _dot(jnp.where(causal, a, 0.0), v)
                blast = b[CHUNK - 1:CHUNK, :]
                kd = k * jnp.exp(blast - b)
                eb = jnp.exp(_dot_hi(g, ones, "tn"))
                s_scr[h] = eb * s0 + _dot(kd, v, "tn")
                r = lax.rsqrt(jnp.mean(o * o, axis=-1, keepdims=True) + RMS_EPS)
                o_ref[rows, h * HD:(h + 1) * HD] = o
                y_ref[rows, h * HD:(h + 1) * HD] = (o * r * ng_ref[...] * _sigmoid(gb)).astype(BF16)
            return carry

        lax.fori_loop(0, ncb, chunk, 0)

    return pl.pallas_call(
        body,
        name=name,
        grid=(t // bt,),
        in_specs=[pl.BlockSpec((bt, 4 * B_WIDTH), lambda i: (i, 1)), pl.BlockSpec((1, B_WIDTH), lambda i: (0, 0)),
                  pl.BlockSpec((1, HD), lambda i: (0, 0))],
        out_specs=[pl.BlockSpec((bt, B_WIDTH), lambda i: (i, 0)), pl.BlockSpec((bt, B_WIDTH), lambda i: (i, 0)),
                   pl.BlockSpec((ncb, B_HEADS, HD, HD), lambda i: (i, 0, 0, 0))],
        out_shape=[jax.ShapeDtypeStruct((t, B_WIDTH), BF16), jax.ShapeDtypeStruct((t, B_WIDTH), F32),
                   jax.ShapeDtypeStruct((t // CHUNK, B_HEADS, HD, HD), F32)],
        scratch_shapes=[pltpu.VMEM((B_HEADS, HD, HD), F32)],
        compiler_params=_params("arbitrary"),
    )(rest, lb, ng)


def _hgrn_bwd(dy, rest, o_saved, states, lb, ng, name):
    t = rest.shape[0]
    bt = min(HG_ROWS, t)
    ncb = bt // CHUNK
    nb = t // bt

    def body(dy_ref, hb_ref, o_ref, st_ref, lb_ref, ng_ref, dh_ref, dlb_ref, dng_ref, ds_scr):
        @pl.when(pl.program_id(0) == 0)
        def _():
            ds_scr[...] = jnp.zeros_like(ds_scr)
            dlb_ref[...] = jnp.zeros_like(dlb_ref)
            dng_ref[...] = jnp.zeros_like(dng_ref)

        tril = _tri(CHUNK, True)
        triu = _tri(CHUNK, False)
        causal = _causal(CHUNK)
        ones = jnp.ones((CHUNK, HD), F32)
        ones8 = jnp.ones((8, HD), F32)
        last_row = lax.broadcasted_iota(jnp.int32, (CHUNK, HD), 0) == CHUNK - 1

        def chunk(cc, carry):
            c = ncb - 1 - cc
            rows = pl.ds(pl.multiple_of(c * CHUNK, CHUNK), CHUNK)
            for h in range(B_HEADS):
                cols = slice(h * HD, (h + 1) * HD)
                lbh = lb_ref[:, cols]
                qb, sq, q, sg, f, k, g, v, gb = _hg_gates(hb_ref, rows, h, lbh)
                o = o_ref[rows, cols]
                dyv = dy_ref[rows, cols].astype(F32)
                ngv = ng_ref[...]
                r = lax.rsqrt(jnp.mean(o * o, axis=-1, keepdims=True) + RMS_EPS)
                sgb = _sigmoid(gb)
                don = dyv * sgb
                dgb = dyv * (o * r * ngv) * sgb * (1.0 - sgb)
                dng_ref[...] += jnp.sum(don * o * r, axis=0, keepdims=True)
                doh = don * ngv
                do = r * (doh - o * (r * r) * jnp.mean(doh * o, axis=-1, keepdims=True))
                b = _dot_hi(tril, g)
                ebt = jnp.exp(b)
                s0 = st_ref[c, h]
                ds1 = ds_scr[h]
                blast = b[CHUNK - 1:CHUNK, :]
                ekd = jnp.exp(blast - b)
                kd = k * ekd
                eb = jnp.exp(_dot_hi(g, ones, "tn"))
                fac = _hg_intra_factors(q, k, b)
                a = jnp.concatenate([_dot(qe, ke, "nt") for _, _, qe, ke in fac], axis=0)
                a = jnp.where(causal, a, 0.0)
                da = jnp.where(causal, _dot(do, v, "nt"), 0.0)
                dv = _dot(a, do, "tn") + _dot(kd, ds1)
                dq = ebt * _dot(do, s0, "nt")
                dq = dq + jnp.concatenate(
                    [eq * _hdot(da[SUB * i:SUB * (i + 1), :], ke) for i, (eq, _, _, ke) in enumerate(fac)], axis=0)
                dk_state = ekd * _dot(v, ds1, "nt")
                dk = dk_state
                for i, (_, ek, qe, _) in enumerate(fac):
                    dk = dk + ek * _hdot(da[SUB * i:SUB * (i + 1), :], qe, "tn")
                ds_scr[h] = _dot(q * ebt, do, "tn") + eb * ds1
                extra = jnp.exp(blast) * _dot_hi(ones8, ds1 * s0, "nt")[0:1, :] \
                    + jnp.sum(k * dk_state, axis=0, keepdims=True)
                db = q * dq - k * dk + jnp.where(last_row, extra, 0.0)
                dg = _dot_hi(triu, db)
                df = dg / f - dk
                dlb_ref[:, cols] += jnp.sum(df * (1.0 - sg), axis=0, keepdims=True)
                dfb = df * (1.0 - lbh) * sg * (1.0 - sg)
                dqb = dq * (sq * (1.0 + qb * (1.0 - sq)))
                dh_ref[rows, h * HD:(h + 1) * HD] = dqb.astype(BF16)
                dh_ref[rows, B_WIDTH + h * HD:B_WIDTH + (h + 1) * HD] = dfb.astype(BF16)
                dh_ref[rows, 2 * B_WIDTH + h * HD:2 * B_WIDTH + (h + 1) * HD] = dv.astype(BF16)
                dh_ref[rows, 3 * B_WIDTH + h * HD:3 * B_WIDTH + (h + 1) * HD] = dgb.astype(BF16)
            return carry

        lax.fori_loop(0, ncb, chunk, 0)

    rev = lambda i: (nb - 1 - i, 0)
    return pl.pallas_call(
        body,
        name=name,
        grid=(nb,),
        in_specs=[pl.BlockSpec((bt, B_WIDTH), rev), pl.BlockSpec((bt, 4 * B_WIDTH), lambda i: (nb - 1 - i, 1)),
                  pl.BlockSpec((bt, B_WIDTH), rev),
                  pl.BlockSpec((ncb, B_HEADS, HD, HD), lambda i: (nb - 1 - i, 0, 0, 0)),
                  pl.BlockSpec((1, B_WIDTH), lambda i: (0, 0)), pl.BlockSpec((1, HD), lambda i: (0, 0))],
        out_specs=[pl.BlockSpec((bt, 4 * B_WIDTH), rev), pl.BlockSpec((1, B_WIDTH), lambda i: (0, 0)),
                   pl.BlockSpec((1, HD), lambda i: (0, 0))],
        out_shape=[jax.ShapeDtypeStruct((t, 4 * B_WIDTH), BF16), jax.ShapeDtypeStruct((1, B_WIDTH), F32),
                   jax.ShapeDtypeStruct((1, HD), F32)],
        scratch_shapes=[pltpu.VMEM((B_HEADS, HD, HD), F32)],
        compiler_params=_params("arbitrary"),
    )(dy, rest, o_saved, states, lb, ng)


def _axpy2(c0, a0, c1, a1, name, tm=512):
    t, d = a0.shape
    tm = min(tm, t)

    def body(a_ref, b_ref, o_ref):
        o_ref[...] = c0 * a_ref[...] + c1 * b_ref[...]

    row = lambda i: (i, 0)
    return pl.pallas_call(
        body, name=name, grid=(t // tm,),
        in_specs=[pl.BlockSpec((tm, d), row), pl.BlockSpec((tm, d), row)],
        out_specs=pl.BlockSpec((tm, d), row),
        out_shape=jax.ShapeDtypeStruct((t, d), F32),
        compiler_params=_params("parallel"),
    )(a0, a1)


def _split_w_in(w_in_l):
    wqkv = w_in_l[:, :3 * A_WIDTH]
    wfa = jnp.pad(w_in_l[:, 3 * A_WIDTH:3 * A_WIDTH + A_HEADS], ((0, 0), (0, 128 - A_HEADS)))
    whb = w_in_l[:, 3 * A_WIDTH + A_HEADS:3 * A_WIDTH + A_HEADS + 4 * B_WIDTH]
    wgt = w_in_l[:, 3 * A_WIDTH + A_HEADS + 4 * B_WIDTH:]
    return wqkv, jnp.concatenate([wgt, whb, wfa], axis=1)


def _merge_w_in_grad(dwall):
    o = 3 * A_WIDTH
    return jnp.concatenate([dwall[:, :o], dwall[:, o + 4096:o + 4096 + A_HEADS], dwall[:, o + 2048:o + 4096],
                            dwall[:, o:o + 2048]], axis=1)


def _layer_fwd(x, xb, w, sp, l):
    t = x.shape[0]
    n = f"l{l}_"
    wqkv, wrest = _split_w_in(w["w_in"])
    qkv = _matmul(xb, wqkv, "nn", BF16, 512, 768, D_MODEL, n + "proj_qkv")
    rest = _matmul(xb, wrest, "nn", F32, 512, 1408, D_MODEL, n + "proj_rest")
    bf = jnp.pad(sp["b_fgate"], (0, 128 - A_HEADS)).reshape(1, 128)
    fcum = _fox_gate_fwd(rest, bf, n + "fox_gate_fwd")
    fcol = jnp.repeat(fcum[:, :A_HEADS], 64, axis=1)
    qa, ka, va = _fox_prep_fwd(qkv, fcol, n + "fox_prep_fwd")
    kstart, qend = _fox_block_ranges(qkv, fcum)
    ya, lse = _fox_fwd(qa, ka, va, kstart, n + "fox_fwd")
    lb = sp["lb"].reshape(1, B_WIDTH)
    ng = sp["norm_g"].reshape(1, HD)
    yb, ob, states = _hgrn_fwd(rest, lb, ng, n + "hgrn_fwd")
    merged = _merge_fwd(ya, yb, w["w_pa"], w["w_pb"], rest, n + "merge_fwd")
    x1, x1b, xh1, rs1 = _mm_res_ln(merged, w["w_out"], x, sp["ln1_g"], sp["ln1_b"], n + "out_ln1")
    h = _matmul(x1b, w["w_ff_in"], "nn", F32, 512, 1408, D_MODEL, n + "ffn_in")
    a = _swiglu_fwd(h, n + "swiglu_fwd")
    x2, x2b, xh2, rs2 = _mm_res_ln(a, w["w_ff_out"], x1, sp["ln2_g"], sp["ln2_b"], n + "ffn_out_ln2")
    saved = dict(xb=xb, wqkv=wqkv, wrest=wrest, qkv=qkv, rest=rest, bf=bf, fcol=fcol, ka=ka, va=va, ya=ya, lse=lse,
                 kstart=kstart, qend=qend,
                 lb=lb, ng=ng, yb=yb, ob=ob, states=states, merged=merged, x1b=x1b, xh1=xh1, rs1=rs1, h=h, a=a,
                 xh2=xh2, rs2=rs2)
    return x2, x2b, saved


def _layer_bwd(dys, coefs, w, sp, s, l):
    n = f"l{l}_"
    dz2, dz2b, dg2, db2 = _ln_bwd(dys, coefs, s["xh2"], s["rs2"], sp["ln2_g"], n + "ln2_bwd")
    da = _matmul(dz2b, w["w_ff_out"], "nt", F32, 512, 1408, D_MODEL, n + "ffn_out_dx")
    d_wffout = _matmul(s["a"], dz2b, "tn", F32, 1408, 1024, 512, n + "ffn_out_dw")
    dh = _swiglu_bwd(da, s["h"], n + "swiglu_bwd")
    dx1f = _matmul(dh, w["w_ff_in"], "nt", F32, 512, 1024, 1408, n + "ffn_in_dx")
    d_wffin = _matmul(s["x1b"], dh, "tn", F32, 1024, 1408, 512, n + "ffn_in_dw")
    dz1, dz1b, dg1, db1 = _ln_bwd([dz2, dx1f], [ALPHA, 1.0], s["xh1"], s["rs1"], sp["ln1_g"], n + "ln1_bwd")
    dmerged = _matmul(dz1b, w["w_out"], "nt", F32, 512, 1024, D_MODEL, n + "out_dx")
    d_wout = _matmul(s["merged"], dz1b, "tn", F32, 1024, 1024, 512, n + "out_dw")
    dgates, dpa, dpb = _merge_bwd(dmerged, s["ya"], s["yb"], w["w_pa"], w["w_pb"], s["rest"], n + "merge_bwd")
    dya = _matmul(dpa, w["w_pa"], "nt", BF16, 512, 512, D_MODEL, n + "pa_dx")
    d_wpa = _matmul(s["ya"], dpa, "tn", F32, 512, 1024, 512, n + "pa_dw")
    dyb = _matmul(dpb, w["w_pb"], "nt", F32, 512, 512, D_MODEL, n + "pb_dx")
    d_wpb = _matmul(s["yb"], dpb, "tn", F32, 512, 1024, 512, n + "pb_dw")
    qb, dob = _fox_prep_bwd(s["qkv"], s["fcol"], s["lse"], dya, s["ya"], n + "fox_prep_bwd")
    dq, rsum = _fox_bwd_dq(qb, s["ka"], s["va"], dob, s["kstart"], n + "fox_bwd_dq")
    dk, dv, csum = _fox_bwd_dkv(qb, s["ka"], s["va"], dob, s["qend"], n + "fox_bwd_dkv")
    d_fcum = jnp.pad(rsum[:, ::64] - csum[:, ::64], ((0, 0), (0, 128 - A_HEADS)))
    dfa, dbf = _fox_gate_bwd(d_fcum, s["rest"], s["bf"], n + "fox_gate_bwd")
    dhb, dlb, dng = _hgrn_bwd(dyb, s["rest"], s["ob"], s["states"], s["lb"], s["ng"], n + "hgrn_bwd")
    dproj = jnp.concatenate([dq, dk, dv, dgates, dhb, dfa], axis=1)
    wall = jnp.concatenate([s["wqkv"], s["wrest"]], axis=1)
    dxp = _matmul(dproj, wall, "nt", F32, 512, 1024, 1920, n + "proj_dx")
    d_wall = _matmul(s["xb"], dproj, "tn", F32, 1024, 1152, 512, n + "proj_dw")
    grads = dict(w_in=_merge_w_in_grad(d_wall), w_pa=d_wpa, w_pb=d_wpb, w_out=d_wout, w_ff_in=d_wffin,
                 w_ff_out=d_wffout, b_fgate=dbf[0, :A_HEADS], lb=dlb[0], norm_g=dng[0], ln1_g=dg1[0], ln1_b=db1[0],
                 ln2_g=dg2[0], ln2_b=db2[0])
    return [dz1, dxp], [ALPHA, 1.0], grads


def _lower_bounds(logits):
    sm = jax.nn.softmax(logits.astype(F32), axis=0)
    return jnp.cumsum(sm, axis=0) - sm[0:1]


def _local_step(x, target, wfull, small):
    lbs, lb_vjp = jax.vjp(_lower_bounds, small["hgrn_lb_logits"])
    h, hb = x, x.astype(BF16)
    saved, sps = [], []
    for l in range(DEPTH):
        sp = dict(b_fgate=small["b_fgate"][l], lb=lbs[l], norm_g=small["hgrn_norm_g"][l], ln1_g=small["ln1_g"][l],
                  ln1_b=small["ln1_b"][l], ln2_g=small["ln2_g"][l], ln2_b=small["ln2_b"][l])
        h, hb, s = _layer_fwd(h, hb, wfull[l], sp, l)
        saved.append(s)
        sps.append(sp)
    dy, lpart = _loss_head(h, target)
    dys, coefs = [dy], [1.0]
    grads = [None] * DEPTH
    for l in reversed(range(DEPTH)):
        dys, coefs, grads[l] = _layer_bwd(dys, coefs, wfull[l], sps[l], saved[l], l)
    grad_x = _axpy2(coefs[0], dys[0], coefs[1], dys[1], "grad_x")
    d_logits = lb_vjp(jnp.stack([grads[l]["lb"] for l in range(DEPTH)]))[0]
    return lpart[0, 0], grad_x, grads, d_logits


_BIG = [("w_in", "w_in", (D_MODEL, IN_TOTAL), 1), ("w_branch_a", "w_pa", (A_WIDTH, D_MODEL), 1),
        ("w_branch_b", "w_pb", (B_WIDTH, D_MODEL), 1), ("w_out", "w_out", (D_MODEL, D_MODEL), 0),
        ("w_ff_in", "w_ff_in", (D_MODEL, 2 * FFN_HIDDEN), 1), ("w_ff_out", "w_ff_out", (FFN_HIDDEN, D_MODEL), 0)]
_SMALL = [("b_fgate", A_HEADS), ("hgrn_lb_logits", B_WIDTH), ("hgrn_norm_g", HD), ("ln1_g", D_MODEL),
          ("ln1_b", D_MODEL), ("ln2_g", D_MODEL), ("ln2_b", D_MODEL)]
PACK_ROWS = 32768
SMALL_ROWS = 80


def _shard_shape(shape, axis):
    return tuple(s // N_CHIPS if a == axis else s for a, s in enumerate(shape))


def _pack_big(per_name):
    flat = jnp.concatenate([per_name[name].reshape(-1, 128) for name, _, _, _ in _BIG], axis=0)
    return jnp.pad(flat, ((0, PACK_ROWS - flat.shape[0]), (0, 0)))


def _unpack_big(slab):
    out, r = {}, 0
    for name, _, shape, axis in _BIG:
        shp = _shard_shape(shape, axis)
        n = shp[0] * shp[1] // 128
        out[name] = slab[r:r + n].reshape(shp)
        r += n
    return out


def _pack_small(per_name):
    flat = jnp.concatenate([per_name[name].reshape(-1) for name, _ in _SMALL])
    return jnp.pad(flat, (0, SMALL_ROWS * 128 - flat.shape[0])).reshape(SMALL_ROWS, 128)


def _unpack_small(slab):
    flat, out, r = slab.reshape(-1), {}, 0
    for name, n in _SMALL:
        out[name] = flat[r:r + DEPTH * n].reshape(DEPTH, n)
        r += DEPTH * n
    return out


_ANY = pl.BlockSpec(memory_space=pl.ANY)


def _place():
    return lax.axis_index("x"), lax.axis_index("y"), lax.axis_index("c")


def _other_chips(x, y):
    return [(1 - x, y), (x, 1 - y), (1 - x, 1 - y)]


def _chip_exchange(mine_of, out_ref, send_sems, recv_sems, local_sem):
    x, y, c = _place()
    q = 2 * x + y
    local = pltpu.make_async_copy(mine_of(q), out_ref.at[q], local_sem)
    local.start()
    sends = []
    for k, (px, py) in enumerate(_other_chips(x, y)):
        cp = pltpu.make_async_remote_copy(src_ref=mine_of(2 * px + py), dst_ref=out_ref.at[q], send_sem=send_sems.at[k],
                                          recv_sem=recv_sems.at[k], device_id=(px, py, c), device_id_type=MESH)
        cp.start()
        sends.append(cp)
    for k, (px, py) in enumerate(_other_chips(x, y)):
        pltpu.make_async_remote_copy(src_ref=mine_of(q), dst_ref=out_ref.at[2 * px + py], send_sem=send_sems.at[k],
                                     recv_sem=recv_sems.at[k], device_id=(px, py, c), device_id_type=MESH).wait_recv()
    for cp in sends:
        cp.wait_send()
    local.wait()


def _gather_weights(mine):
    def body(in_ref, out_ref, send_sems, recv_sems, local_sem, pair_send, pair_recv):
        x, y, c = _place()
        _chip_exchange(lambda q: in_ref.at[c], out_ref.at[c], send_sems, recv_sems, local_sem)
        sibling = (x, y, 1 - c)
        fwd = pltpu.make_async_remote_copy(src_ref=out_ref.at[c], dst_ref=out_ref.at[c], send_sem=pair_send,
                                           recv_sem=pair_recv, device_id=sibling, device_id_type=MESH)
        fwd.start()
        pltpu.make_async_remote_copy(src_ref=out_ref.at[1 - c], dst_ref=out_ref.at[1 - c], send_sem=pair_send,
                                     recv_sem=pair_recv, device_id=sibling, device_id_type=MESH).wait_recv()
        fwd.wait_send()

    return pl.pallas_call(
        body, name="gather_weights", in_specs=[_ANY], out_specs=_ANY,
        out_shape=jax.ShapeDtypeStruct((DEPTH, N_CHIPS) + mine.shape[1:], mine.dtype),
        scratch_shapes=[pltpu.SemaphoreType.DMA((3,)), pltpu.SemaphoreType.DMA((3,)), pltpu.SemaphoreType.DMA(()),
                        pltpu.SemaphoreType.DMA(()), pltpu.SemaphoreType.DMA(())],
    )(mine)


def _pair_exchange(g):
    def body(g_ref, a_ref, send_sem, recv_sem):
        x, y, c = _place()
        cp = pltpu.make_async_remote_copy(src_ref=g_ref.at[1 - c], dst_ref=a_ref, send_sem=send_sem, recv_sem=recv_sem,
                                          device_id=(x, y, 1 - c), device_id_type=MESH)
        cp.start()
        cp.wait()

    return pl.pallas_call(
        body, name="grad_pair_exchange", in_specs=[_ANY], out_specs=_ANY,
        out_shape=jax.ShapeDtypeStruct(g.shape[1:], g.dtype),
        scratch_shapes=[pltpu.SemaphoreType.DMA(()), pltpu.SemaphoreType.DMA(())],
    )(g)


def _pair_sum(g, a, layer, tb=2048):
    rows = a.shape[0] * a.shape[1]
    g2 = g.reshape(DEPTH, rows, 128)
    a2 = a.reshape(rows, 128)

    def body(l_ref, g_ref, a_ref, o_ref):
        o_ref[...] = (g_ref[...] + a_ref[...]).astype(BF16)

    out = pl.pallas_call(
        body, name="grad_pair_sum",
        grid_spec=pltpu.PrefetchScalarGridSpec(
            num_scalar_prefetch=1, grid=(rows // tb,),
            in_specs=[pl.BlockSpec((None, tb, 128), lambda i, l_ref: (l_ref[0], i, 0)),
                      pl.BlockSpec((tb, 128), lambda i, l_ref: (i, 0))],
            out_specs=pl.BlockSpec((tb, 128), lambda i, l_ref: (i, 0))),
        out_shape=jax.ShapeDtypeStruct((rows, 128), BF16),
        compiler_params=_params("parallel"),
    )(layer.reshape(1).astype(jnp.int32), g2, a2)
    return out.reshape(a.shape)


def _shard_exchange(p):
    def body(p_ref, b_ref, send_sems, recv_sems, local_sem):
        _chip_exchange(lambda q: p_ref.at[q], b_ref, send_sems, recv_sems, local_sem)

    return pl.pallas_call(
        body, name="grad_shard_exchange", in_specs=[_ANY], out_specs=_ANY,
        out_shape=jax.ShapeDtypeStruct(p.shape, p.dtype),
        scratch_shapes=[pltpu.SemaphoreType.DMA((3,)), pltpu.SemaphoreType.DMA((3,)), pltpu.SemaphoreType.DMA(())],
    )(p)


def _sum4(b, tb=2048):
    rows = b.shape[1]

    def body(b_ref, o_ref):
        o_ref[...] = ((b_ref[0].astype(F32) + b_ref[1].astype(F32)) + b_ref[2].astype(F32)) + b_ref[3].astype(F32)

    return pl.pallas_call(
        body, name="grad_chip_sum", grid=(rows // tb,),
        in_specs=[pl.BlockSpec((N_CHIPS, tb, 128), lambda i: (0, i, 0))],
        out_specs=pl.BlockSpec((tb, 128), lambda i: (i, 0)),
        out_shape=jax.ShapeDtypeStruct((rows, 128), F32),
        compiler_params=_params("parallel"),
    )(b)


def _result_exchange(gc):
    def body(g_ref, o_ref, send_sem, recv_sem, local_sem):
        x, y, c = _place()
        local = pltpu.make_async_copy(g_ref, o_ref.at[c], local_sem)
        local.start()
        cp = pltpu.make_async_remote_copy(src_ref=g_ref, dst_ref=o_ref.at[c], send_sem=send_sem, recv_sem=recv_sem,
                                          device_id=(x, y, 1 - c), device_id_type=MESH)
        cp.start()
        pltpu.make_async_remote_copy(src_ref=g_ref, dst_ref=o_ref.at[1 - c], send_sem=send_sem, recv_sem=recv_sem,
                                     device_id=(x, y, 1 - c), device_id_type=MESH).wait_recv()
        cp.wait_send()
        local.wait()

    return pl.pallas_call(
        body, name="grad_result_exchange", in_specs=[_ANY], out_specs=_ANY,
        out_shape=jax.ShapeDtypeStruct((DEPTH,) + gc.shape, gc.dtype),
        scratch_shapes=[pltpu.SemaphoreType.DMA(()), pltpu.SemaphoreType.DMA(()), pltpu.SemaphoreType.DMA(())],
    )(gc)


def _allreduce_small(v):
    def body(v_ref, o_ref, buf, send_sems, recv_sems):
        x, y, c = _place()
        me = 4 * x + 2 * y + c
        buf[me] = v_ref[...]
        peers = []
        for k in range(1, N_DEV):
            px = 1 - x if k & 4 else x
            py = 1 - y if k & 2 else y
            pc = 1 - c if k & 1 else c
            peers.append((px, py, pc))
        sends = []
        for k, peer in enumerate(peers):
            cp = pltpu.make_async_remote_copy(src_ref=v_ref, dst_ref=buf.at[me], send_sem=send_sems.at[k],
                                              recv_sem=recv_sems.at[k], device_id=peer, device_id_type=MESH)
            cp.start()
            sends.append(cp)
        for k, (px, py, pc) in enumerate(peers):
            pltpu.make_async_remote_copy(src_ref=v_ref, dst_ref=buf.at[4 * px + 2 * py + pc], send_sem=send_sems.at[k],
                                         recv_sem=recv_sems.at[k], device_id=(px, py, pc),
                                         device_id_type=MESH).wait_recv()
        for cp in sends:
            cp.wait_send()
        acc = buf[0]
        for i in range(1, N_DEV):
            acc = acc + buf[i]
        o_ref[...] = acc

    vm = pl.BlockSpec(memory_space=pltpu.VMEM)
    return pl.pallas_call(
        body, name="small_allreduce", in_specs=[vm], out_specs=vm,
        out_shape=jax.ShapeDtypeStruct(v.shape, F32),
        scratch_shapes=[pltpu.VMEM((N_DEV,) + v.shape, F32), pltpu.SemaphoreType.DMA((N_DEV - 1,)),
                        pltpu.SemaphoreType.DMA((N_DEV - 1,))],
    )(v)


def _adamw(w, g, m, v, name, tb=2048):
    rows = w.shape[0]
    tb = min(tb, rows)
    c1 = 1.0 - ADAM_B1
    c2 = 1.0 - ADAM_B2
    bc1 = 1.0 - ADAM_B1 ** ADAM_STEP
    bc2 = 1.0 - ADAM_B2 ** ADAM_STEP

    def body(w_ref, g_ref, m_ref, v_ref, d_ref, nm_ref, nv_ref):
        gv = g_ref[...]
        nm = ADAM_B1 * m_ref[...] + c1 * gv
        nv = ADAM_B2 * v_ref[...] + c2 * (gv * gv)
        nm_ref[...] = nm
        nv_ref[...] = nv
        d_ref[...] = -ADAM_LR * ((nm / bc1) / (jnp.sqrt(nv / bc2) + ADAM_EPS) + ADAM_WD * w_ref[...])

    blk = pl.BlockSpec((tb, 128), lambda i: (i, 0))
    return pl.pallas_call(
        body, name=name, grid=(rows // tb,), in_specs=[blk] * 4, out_specs=[blk] * 3,
        out_shape=[jax.ShapeDtypeStruct((rows, 128), F32)] * 3,
        compiler_params=_params("parallel"),
    )(w, g, m, v)


def kernel(x, w_in, b_fgate, hgrn_lb_logits, hgrn_norm_g, w_branch_a, w_branch_b, w_out, ln1_g, ln1_b, w_ff_in, w_ff_out, ln2_g, ln2_b, loss_target, m_w_in, m_b_fgate, m_hgrn_lb_logits, m_hgrn_norm_g, m_w_branch_a, m_w_branch_b, m_w_out, m_ln1_g, m_ln1_b, m_w_ff_in, m_w_ff_out, m_ln2_g, m_ln2_b, v_w_in, v_b_fgate, v_hgrn_lb_logits, v_hgrn_norm_g, v_w_branch_a, v_w_branch_b, v_w_out, v_ln1_g, v_ln1_b, v_w_ff_in, v_w_ff_out, v_ln2_g, v_ln2_b):
    weights = dict(w_in=w_in, b_fgate=b_fgate, hgrn_lb_logits=hgrn_lb_logits, hgrn_norm_g=hgrn_norm_g,
                   w_branch_a=w_branch_a, w_branch_b=w_branch_b, w_out=w_out, ln1_g=ln1_g, ln1_b=ln1_b,
                   w_ff_in=w_ff_in, w_ff_out=w_ff_out, ln2_g=ln2_g, ln2_b=ln2_b)
    mom1 = dict(w_in=m_w_in, b_fgate=m_b_fgate, hgrn_lb_logits=m_hgrn_lb_logits, hgrn_norm_g=m_hgrn_norm_g,
                w_branch_a=m_w_branch_a, w_branch_b=m_w_branch_b, w_out=m_w_out, ln1_g=m_ln1_g, ln1_b=m_ln1_b,
                w_ff_in=m_w_ff_in, w_ff_out=m_w_ff_out, ln2_g=m_ln2_g, ln2_b=m_ln2_b)
    mom2 = dict(w_in=v_w_in, b_fgate=v_b_fgate, hgrn_lb_logits=v_hgrn_lb_logits, hgrn_norm_g=v_hgrn_norm_g,
                w_branch_a=v_w_branch_a, w_branch_b=v_w_branch_b, w_out=v_w_out, ln1_g=v_ln1_g, ln1_b=v_ln1_b,
                w_ff_in=v_w_ff_in, w_ff_out=v_w_ff_out, ln2_g=v_ln2_g, ln2_b=v_ln2_b)
    core = lax.axis_index("c")

    def pack_layers(tree):
        return jnp.stack([_pack_big({name: tree[name][l] for name, _, _, _ in _BIG}) for l in range(DEPTH)])

    w_packed = pack_layers(weights)
    gathered = _gather_weights(w_packed.astype(BF16))
    wfull = []
    for l in range(DEPTH):
        shards = [_unpack_big(gathered[l, q]) for q in range(N_CHIPS)]
        wfull.append({key: jnp.concatenate([shards[q][name] for q in range(N_CHIPS)], axis=axis)
                      for name, key, _, axis in _BIG})
    small = {name: weights[name] for name, _ in _SMALL}

    loss_part, grad_x, grads, d_logits = _local_step(x[0], loss_target[0], wfull, small)

    def shard_of(full, axis, q):
        n = full.shape[axis] // N_CHIPS
        return lax.slice_in_dim(full, q * n, (q + 1) * n, axis=axis)

    g_all = jnp.stack([jnp.stack([_pack_big({name: shard_of(grads[l][key], axis, q) for name, key, _, axis in _BIG})
                                  for q in range(N_CHIPS)]) for l in range(DEPTH)])
    received = _pair_exchange(g_all)
    pair = _pair_sum(g_all, received, core)
    by_chip = _shard_exchange(pair)
    g_layer = _sum4(by_chip)
    g_packed = _result_exchange(g_layer)

    rows = DEPTH * PACK_ROWS
    delta, new_m, new_v = _adamw(w_packed.reshape(rows, 128), g_packed.reshape(rows, 128),
                                 pack_layers(mom1).reshape(rows, 128), pack_layers(mom2).reshape(rows, 128), "adamw_big")

    def unpack_layers(slab):
        per = [_unpack_big(slab.reshape(DEPTH, PACK_ROWS, 128)[l]) for l in range(DEPTH)]
        return {name: jnp.stack([per[l][name] for l in range(DEPTH)]) for name, _, _, _ in _BIG}

    out_g, out_d, out_m, out_v = (unpack_layers(t) for t in (g_packed, delta, new_m, new_v))

    small_grads = {name: jnp.stack([grads[l][key] for l in range(DEPTH)])
                   for name, key in [("b_fgate", "b_fgate"), ("hgrn_norm_g", "norm_g"), ("ln1_g", "ln1_g"),
                                     ("ln1_b", "ln1_b"), ("ln2_g", "ln2_g"), ("ln2_b", "ln2_b")]}
    small_grads["hgrn_lb_logits"] = d_logits
    gs = _allreduce_small(_pack_small(small_grads))
    ds, ms, vs = _adamw(_pack_small(small), gs, _pack_small({n: mom1[n] for n, _ in _SMALL}),
                        _pack_small({n: mom2[n] for n, _ in _SMALL}), "adamw_small")
    for tree, slab in ((out_g, gs), (out_d, ds), (out_m, ms), (out_v, vs)):
        tree.update(_unpack_small(slab))

    loss = lax.psum(loss_part, ("x", "y", "c"))
    order = ["w_in", "b_fgate", "hgrn_lb_logits", "hgrn_norm_g", "w_branch_a", "w_branch_b", "w_out", "ln1_g", "ln1_b",
             "w_ff_in", "w_ff_out", "ln2_g", "ln2_b"]
    return (loss, grad_x[None], *[out_g[n] for n in order], *[out_d[n] for n in order],
            *[out_m[n] for n in order], *[out_v[n] for n in order])
```

```python
import functools
import math

import jax
import jax.numpy as jnp
import numpy as np
from jax import lax
from jax.experimental import pallas as pl
from jax.experimental.pallas import tpu as pltpu

F32 = jnp.float32
BF16 = jnp.bfloat16

D_MODEL = 1024
DEPTH = 2
A_HEADS = 8
A_WIDTH = 512
B_WIDTH = 512
B_HEADS = 4
HD = 128
CHUNK = 64
SUB = 16
FFN_HIDDEN = 2816
IN_TOTAL = 5640
ALPHA = (2 * DEPTH) ** 0.25
LN_EPS = 1e-5
RMS_EPS = 1e-6
ADAM_LR = 0.001
ADAM_B1 = 0.9
ADAM_B2 = 0.999
ADAM_EPS = 1e-08
ADAM_WD = 0.01
ADAM_STEP = 10
EXP_CLAMP = 60.0

VMEM_LIMIT_BYTES = 56 * 1024 * 1024
N_CHIPS = 4
N_DEV = 8
MESH = pl.DeviceIdType.MESH

_DN = {
    "nn": (((1,), (0,)), ((), ())),
    "nt": (((1,), (1,)), ((), ())),
    "tn": (((0,), (0,)), ((), ())),
}


def _dot(a, b, mode="nn"):
    return lax.dot_general(a.astype(BF16), b.astype(BF16), _DN[mode], preferred_element_type=F32)


def _dot_hi(a, b, mode="nn"):
    return lax.dot_general(a, b, _DN[mode], precision=lax.Precision.HIGHEST, preferred_element_type=F32)


def _hdot(a, b, mode="nn"):
    return _dot_hi(a.astype(F32), b.astype(F32), mode)


def _params(*sem):
    return pltpu.CompilerParams(dimension_semantics=sem, vmem_limit_bytes=VMEM_LIMIT_BYTES)


def _sigmoid(x):
    return 1.0 / (1.0 + jnp.exp(-x))


def _matmul(a, b, mode, out_dtype, tm, tn, tk, name):
    if mode == "nn":
        (m, k), (k2, n) = a.shape, b.shape
    elif mode == "nt":
        (m, k), (n, k2) = a.shape, b.shape
    else:
        (k, m), (k2, n) = a.shape, b.shape
    assert k == k2, (a.shape, b.shape, mode)
    tm, tn, tk = min(tm, m), min(tn, n), min(tk, k)
    assert m % tm == 0 and n % tn == 0 and k % tk == 0, (a.shape, b.shape, tm, tn, tk)
    nk = k // tk
    if mode == "tn":
        a_spec = pl.BlockSpec((tk, tm), lambda j, i, kk: (kk, i))
    else:
        a_spec = pl.BlockSpec((tm, tk), lambda j, i, kk: (i, kk))
    if mode == "nt":
        b_spec = pl.BlockSpec((tn, tk), lambda j, i, kk: (j, kk))
    else:
        b_spec = pl.BlockSpec((tk, tn), lambda j, i, kk: (kk, j))
    use_acc = nk > 1 and out_dtype != F32

    def body(a_ref, b_ref, o_ref, *scratch):
        p = _dot(a_ref[...], b_ref[...], mode)
        if nk == 1:
            o_ref[...] = p.astype(out_dtype)
            return
        acc_ref = scratch[0] if use_acc else o_ref
        kk = pl.program_id(2)

        @pl.when(kk == 0)
        def _():
            acc_ref[...] = p

        @pl.when(kk > 0)
        def _():
            acc_ref[...] += p

        if use_acc:
            @pl.when(kk == nk - 1)
            def _():
                o_ref[...] = acc_ref[...].astype(out_dtype)

    return pl.pallas_call(
        body,
        name=name,
        grid=(n // tn, m // tm, nk),
        in_specs=[a_spec, b_spec],
        out_specs=pl.BlockSpec((tm, tn), lambda j, i, kk: (i, j)),
        out_shape=jax.ShapeDtypeStruct((m, n), out_dtype),
        scratch_shapes=[pltpu.VMEM((tm, tn), F32)] if use_acc else [],
        compiler_params=_params("parallel", "parallel", "arbitrary"),
    )(a, b)


def _mm_res_ln(a, w, res, g, b, name, tm=512):
    t, k = a.shape
    d = w.shape[1]
    tm = min(tm, t)

    def body(a_ref, w_ref, r_ref, g_ref, b_ref, y_ref, yb_ref, xh_ref, rs_ref):
        z = ALPHA * r_ref[...] + _dot(a_ref[...], w_ref[...])
        mu = jnp.mean(z, axis=-1, keepdims=True)
        zc = z - mu
        var = jnp.mean(zc * zc, axis=-1, keepdims=True)
        rstd = lax.rsqrt(var + LN_EPS)
        xh = zc * rstd
        y = xh * g_ref[...] + b_ref[...]
        y_ref[...] = y
        yb_ref[...] = y.astype(BF16)
        xh_ref[...] = xh
        rs_ref[...] = rstd

    row = lambda i: (i, 0)
    fix = lambda i: (0, 0)
    return pl.pallas_call(
        body,
        name=name,
        grid=(t // tm,),
        in_specs=[pl.BlockSpec((tm, k), row), pl.BlockSpec((k, d), fix), pl.BlockSpec((tm, d), row),
                  pl.BlockSpec((1, d), fix), pl.BlockSpec((1, d), fix)],
        out_specs=[pl.BlockSpec((tm, d), row), pl.BlockSpec((tm, d), row), pl.BlockSpec((tm, d), row),
                   pl.BlockSpec((tm, 1), row)],
        out_shape=[jax.ShapeDtypeStruct((t, d), F32), jax.ShapeDtypeStruct((t, d), BF16),
                   jax.ShapeDtypeStruct((t, d), F32), jax.ShapeDtypeStruct((t, 1), F32)],
        compiler_params=_params("parallel"),
    )(a, w, res, g.reshape(1, d), b.reshape(1, d))


def _ln_bwd(dys, coefs, xhat, rstd, g, name, tm=512):
    t, d = xhat.shape
    tm = min(tm, t)
    n_in = len(dys)

    def body(*refs):
        dy_refs = refs[:n_in]
        xh_ref, rs_ref, g_ref, dz_ref, dzb_ref, dg_ref, db_ref = refs[n_in:]
        dy = coefs[0] * dy_refs[0][...].astype(F32)
        for c, r in zip(coefs[1:], dy_refs[1:]):
            dy = dy + c * r[...].astype(F32)
        xh = xh_ref[...]
        dxh = dy * g_ref[...]
        m1 = jnp.mean(dxh, axis=-1, keepdims=True)
        m2 = jnp.mean(dxh * xh, axis=-1, keepdims=True)
        dz = rs_ref[...] * (dxh - m1 - xh * m2)
        dz_ref[...] = dz
        dzb_ref[...] = dz.astype(BF16)
        pg = jnp.sum(dy * xh, axis=0, keepdims=True)
        pb = jnp.sum(dy, axis=0, keepdims=True)

        @pl.when(pl.program_id(0) == 0)
        def _():
            dg_ref[...] = pg
            db_ref[...] = pb

        @pl.when(pl.program_id(0) > 0)
        def _():
            dg_ref[...] += pg
            db_ref[...] += pb

    row = lambda i: (i, 0)
    fix = lambda i: (0, 0)
    return pl.pallas_call(
        body,
        name=name,
        grid=(t // tm,),
        in_specs=[pl.BlockSpec((tm, d), row)] * n_in
        + [pl.BlockSpec((tm, d), row), pl.BlockSpec((tm, 1), row), pl.BlockSpec((1, d), fix)],
        out_specs=[pl.BlockSpec((tm, d), row), pl.BlockSpec((tm, d), row), pl.BlockSpec((1, d), fix),
                   pl.BlockSpec((1, d), fix)],
        out_shape=[jax.ShapeDtypeStruct((t, d), F32), jax.ShapeDtypeStruct((t, d), BF16),
                   jax.ShapeDtypeStruct((1, d), F32), jax.ShapeDtypeStruct((1, d), F32)],
        compiler_params=_params("arbitrary"),
    )(*dys, xhat, rstd, g.reshape(1, d))


def _loss_head(y, target, name="loss_head", tm=512):
    t, d = y.shape
    tm = min(tm, t)

    def body(y_ref, t_ref, dy_ref, l_ref):
        e = y_ref[...] - t_ref[...]
        dy_ref[...] = e * (1.0 / d)
        part = jnp.full((8, 128), 0.5 / d, F32) * jnp.sum(e * e)

        @pl.when(pl.program_id(0) == 0)
        def _():
            l_ref[...] = part

        @pl.when(pl.program_id(0) > 0)
        def _():
            l_ref[...] += part

    row = lambda i: (i, 0)
    return pl.pallas_call(
        body,
        name=name,
        grid=(t // tm,),
        in_specs=[pl.BlockSpec((tm, d), row), pl.BlockSpec((tm, d), row)],
        out_specs=[pl.BlockSpec((tm, d), row), pl.BlockSpec((8, 128), lambda i: (0, 0))],
        out_shape=[jax.ShapeDtypeStruct((t, d), F32), jax.ShapeDtypeStruct((8, 128), F32)],
        compiler_params=_params("arbitrary"),
    )(y, target)


def _swiglu_fwd(h, name, tm=512):
    t = h.shape[0]
    tm = min(tm, t)
    wb = FFN_HIDDEN // 2

    def body(u_ref, g_ref, a_ref):
        g = g_ref[...]
        a_ref[...] = (g * _sigmoid(g) * u_ref[...]).astype(BF16)

    return pl.pallas_call(
        body,
        name=name,
        grid=(t // tm, 2),
        in_specs=[pl.BlockSpec((tm, wb), lambda i, j: (i, j)), pl.BlockSpec((tm, wb), lambda i, j: (i, j + 2))],
        out_specs=pl.BlockSpec((tm, wb), lambda i, j: (i, j)),
        out_shape=jax.ShapeDtypeStruct((t, FFN_HIDDEN), BF16),
        compiler_params=_params("parallel", "parallel"),
    )(h, h)


def _swiglu_bwd(da, h, name, tm=512):
    t = h.shape[0]
    tm = min(tm, t)
    wb = FFN_HIDDEN // 2

    def body(da_ref, u_ref, g_ref, dh_ref):
        g = g_ref[...]
        sg = _sigmoid(g)
        da = da_ref[...]

        @pl.when(pl.program_id(1) < 2)
        def _():
            dh_ref[...] = (da * g * sg).astype(BF16)

        @pl.when(pl.program_id(1) >= 2)
        def _():
            dh_ref[...] = (da * u_ref[...] * (sg * (1.0 + g * (1.0 - sg)))).astype(BF16)

    return pl.pallas_call(
        body,
        name=name,
        grid=(t // tm, 4),
        in_specs=[pl.BlockSpec((tm, wb), lambda i, j: (i, j % 2)), pl.BlockSpec((tm, wb), lambda i, j: (i, j % 2)),
                  pl.BlockSpec((tm, wb), lambda i, j: (i, 2 + j % 2))],
        out_specs=pl.BlockSpec((tm, wb), lambda i, j: (i, j)),
        out_shape=jax.ShapeDtypeStruct((t, 2 * FFN_HIDDEN), BF16),
        compiler_params=_params("parallel", "parallel"),
    )(da, h, h)


def _merge_fwd(ya, yb, wpa, wpb, rest, name, tm=512):
    t = ya.shape[0]
    tm = min(tm, t)

    def body(ya_ref, yb_ref, wa_ref, wb_ref, ga_ref, gb_ref, o_ref):
        pa = _dot(ya_ref[...], wa_ref[...])
        pb = _dot(yb_ref[...], wb_ref[...])
        o_ref[...] = (_sigmoid(ga_ref[...]) * pa + _sigmoid(gb_ref[...]) * pb).astype(BF16)

    row = lambda i: (i, 0)
    fix = lambda i: (0, 0)
    return pl.pallas_call(
        body,
        name=name,
        grid=(t // tm,),
        in_specs=[pl.BlockSpec((tm, A_WIDTH), row), pl.BlockSpec((tm, B_WIDTH), row),
                  pl.BlockSpec((A_WIDTH, D_MODEL), fix), pl.BlockSpec((B_WIDTH, D_MODEL), fix),
                  pl.BlockSpec((tm, D_MODEL), lambda i: (i, 0)), pl.BlockSpec((tm, D_MODEL), lambda i: (i, 1))],
        out_specs=pl.BlockSpec((tm, D_MODEL), row),
        out_shape=jax.ShapeDtypeStruct((t, D_MODEL), BF16),
        compiler_params=_params("parallel"),
    )(ya, yb, wpa, wpb, rest, rest)


def _merge_bwd(dm, ya, yb, wpa, wpb, rest, name, tm=512):
    t = ya.shape[0]
    tm = min(tm, t)

    def body(dm_ref, ya_ref, yb_ref, wa_ref, wb_ref, ga_ref, gb_ref, dg_ref, dpa_ref, dpb_ref):
        dm_v = dm_ref[...]
        pa = _dot(ya_ref[...], wa_ref[...])
        pb = _dot(yb_ref[...], wb_ref[...])
        sa = _sigmoid(ga_ref[...])
        sb = _sigmoid(gb_ref[...])
        dg_ref[:, :D_MODEL] = (dm_v * pa * sa * (1.0 - sa)).astype(BF16)
        dg_ref[:, D_MODEL:] = (dm_v * pb * sb * (1.0 - sb)).astype(BF16)
        dpa_ref[...] = (dm_v * sa).astype(BF16)
        dpb_ref[...] = (dm_v * sb).astype(BF16)

    row = lambda i: (i, 0)
    fix = lambda i: (0, 0)
    return pl.pallas_call(
        body,
        name=name,
        grid=(t // tm,),
        in_specs=[pl.BlockSpec((tm, D_MODEL), row), pl.BlockSpec((tm, A_WIDTH), row), pl.BlockSpec((tm, B_WIDTH), row),
                  pl.BlockSpec((A_WIDTH, D_MODEL), fix), pl.BlockSpec((B_WIDTH, D_MODEL), fix),
                  pl.BlockSpec((tm, D_MODEL), lambda i: (i, 0)), pl.BlockSpec((tm, D_MODEL), lambda i: (i, 1))],
        out_specs=[pl.BlockSpec((tm, 2 * D_MODEL), row), pl.BlockSpec((tm, D_MODEL), row),
                   pl.BlockSpec((tm, D_MODEL), row)],
        out_shape=[jax.ShapeDtypeStruct((t, 2 * D_MODEL), BF16), jax.ShapeDtypeStruct((t, D_MODEL), BF16),
                   jax.ShapeDtypeStruct((t, D_MODEL), BF16)],
        compiler_params=_params("parallel"),
    )(dm, ya, yb, wpa, wpb, rest, rest)


FA_BLOCK = 4224 // 128 - 1


def _tri(n, lower):
    r = lax.broadcasted_iota(jnp.int32, (n, n), 0)
    c = lax.broadcasted_iota(jnp.int32, (n, n), 1)
    return jnp.where((r >= c) if lower else (r <= c), 1.0, 0.0).astype(F32)


def _fox_gate_fwd(rest, bf, name, tb=512):
    t = rest.shape[0]
    tb = min(tb, t)

    def body(fa_ref, bf_ref, f_ref, carry):
        @pl.when(pl.program_id(0) == 0)
        def _():
            carry[...] = jnp.zeros_like(carry)

        z = fa_ref[...] + bf_ref[...]
        logf = jnp.minimum(z, 0.0) - jnp.log(1.0 + jnp.exp(-jnp.abs(z)))
        f = _dot_hi(_tri(tb, True), logf) + carry[...]
        f_ref[...] = f
        carry[...] = f[tb - 1:tb, :]

    return pl.pallas_call(
        body,
        name=name,
        grid=(t // tb,),
        in_specs=[pl.BlockSpec((tb, 128), lambda i: (i, FA_BLOCK)), pl.BlockSpec((1, 128), lambda i: (0, 0))],
        out_specs=pl.BlockSpec((tb, 128), lambda i: (i, 0)),
        out_shape=jax.ShapeDtypeStruct((t, 128), F32),
        scratch_shapes=[pltpu.VMEM((1, 128), F32)],
        compiler_params=_params("arbitrary"),
    )(rest, bf)


def _fox_gate_bwd(d_f, rest, bf, name, tb=512):
    t = rest.shape[0]
    tb = min(tb, t)
    nb = t // tb

    def body(df_ref, fa_ref, bf_ref, dfa_ref, dbf_ref, carry):
        @pl.when(pl.program_id(0) == 0)
        def _():
            carry[...] = jnp.zeros_like(carry)

        dlogf = _dot_hi(_tri(tb, False), df_ref[...]) + carry[...]
        carry[...] = dlogf[0:1, :]
        z = fa_ref[...] + bf_ref[...]
        dz = dlogf * _sigmoid(-z)
        dfa_ref[...] = dz.astype(BF16)
        part = jnp.sum(dz, axis=0, keepdims=True)

        @pl.when(pl.program_id(0) == 0)
        def _():
            dbf_ref[...] = part

        @pl.when(pl.program_id(0) > 0)
        def _():
            dbf_ref[...] += part

    return pl.pallas_call(
        body,
        name=name,
        grid=(nb,),
        in_specs=[pl.BlockSpec((tb, 128), lambda i: (nb - 1 - i, 0)),
                  pl.BlockSpec((tb, 128), lambda i: (nb - 1 - i, FA_BLOCK)),
                  pl.BlockSpec((1, 128), lambda i: (0, 0))],
        out_specs=[pl.BlockSpec((tb, 128), lambda i: (nb - 1 - i, 0)), pl.BlockSpec((1, 128), lambda i: (0, 0))],
        out_shape=[jax.ShapeDtypeStruct((t, 128), BF16), jax.ShapeDtypeStruct((1, 128), F32)],
        scratch_shapes=[pltpu.VMEM((1, 128), F32)],
        compiler_params=_params("arbitrary"),
    )(d_f, rest, bf)


ATT_BLOCK = 512


def _head_mask(shape, j):
    lane = lax.broadcasted_iota(jnp.int32, shape, 1)
    return (lane < 64) if j == 0 else (lane >= 64)


def _split3(x):
    h = x.astype(BF16).astype(F32)
    r = x - h
    m = r.astype(BF16).astype(F32)
    return h, m, (r - m).astype(BF16).astype(F32)


def _aug_lanes(tb, j):
    lane = lax.broadcasted_iota(jnp.int32, (tb, 128), 1)
    own = (lane < 64) if j == 0 else (lane >= 64)
    return own, lane - 64 * (1 - j)


def _aug_query(own, li, q, bias):
    h, m, l = _split3(bias)
    spare = jnp.where(li == 0, h, jnp.where(li == 1, m, jnp.where(li == 2, l, jnp.where(li < 6, 1.0, 0.0))))
    return jnp.where(own, q, spare).astype(BF16)


def _fox_prep_fwd(qkv, fcol, name, tb=512):
    t = qkv.shape[0]
    tb = min(tb, t)

    def body(q_ref, k_ref, v_ref, fc_ref, qa_ref, ka_ref, va_ref):
        fsw = pltpu.roll(fc_ref[...], 64, 1)
        q = q_ref[...].astype(F32) * 0.125
        k = k_ref[...].astype(F32)
        v = v_ref[...].astype(F32)
        h, m, l = _split3(fsw)
        for j in (0, 1):
            own, li = _aug_lanes(tb, j)
            cols = slice(128 * j, 128 * (j + 1))
            qa_ref[:, cols] = _aug_query(own, li, q, fsw)
            ks = jnp.where(li < 3, 1.0, jnp.where(li == 3, -h, jnp.where(li == 4, -m, jnp.where(li == 5, -l, 0.0))))
            ka_ref[:, cols] = jnp.where(own, k, ks).astype(BF16)
            va_ref[:, cols] = jnp.where(own, v, 1.0).astype(BF16)

    blk = pl.BlockSpec((tb, 256), lambda i, h: (i, h))
    return pl.pallas_call(
        body, name=name, grid=(t // tb, 4),
        in_specs=[pl.BlockSpec((tb, 128), lambda i, h: (i, h)), pl.BlockSpec((tb, 128), lambda i, h: (i, 4 + h)),
                  pl.BlockSpec((tb, 128), lambda i, h: (i, 8 + h)), pl.BlockSpec((tb, 128), lambda i, h: (i, h))],
        out_specs=[blk, blk, blk],
        out_shape=[jax.ShapeDtypeStruct((t, 2 * A_WIDTH), BF16)] * 3,
        compiler_params=_params("parallel", "parallel"),
    )(qkv, qkv, qkv, fcol)


def _fox_prep_bwd(qkv, fcol, lse, do, o, name, tb=512):
    t = qkv.shape[0]
    tb = min(tb, t)

    def body(q_ref, fc_ref, lse_ref, do_ref, o_ref, qb_ref, dob_ref):
        gsw = pltpu.roll(fc_ref[...] - lse_ref[...], 64, 1)
        q = q_ref[...].astype(F32) * 0.125
        do_v = do_ref[...].astype(F32)
        prod = do_v * o_ref[...].astype(F32)
        for j in (0, 1):
            own, li = _aug_lanes(tb, j)
            cols = slice(128 * j, 128 * (j + 1))
            qb_ref[:, cols] = _aug_query(own, li, q, gsw)
            delta = jnp.sum(jnp.where(own, prod, 0.0), axis=1, keepdims=True)
            h, m, l = _split3(jnp.broadcast_to(delta, (tb, 128)))
            ds = jnp.where(li == 0, -h, jnp.where(li == 1, -m, jnp.where(li == 2, -l, 0.0)))
            dob_ref[:, cols] = jnp.where(own, do_v, ds).astype(BF16)

    pair = pl.BlockSpec((tb, 128), lambda i, h: (i, h))
    blk = pl.BlockSpec((tb, 256), lambda i, h: (i, h))
    return pl.pallas_call(
        body, name=name, grid=(t // tb, 4),
        in_specs=[pair, pair, pair, pair, pair],
        out_specs=[blk, blk],
        out_shape=[jax.ShapeDtypeStruct((t, 2 * A_WIDTH), BF16)] * 2,
        compiler_params=_params("parallel", "parallel"),
    )(qkv, fcol, lse, do, o)


def _tile_mask(n, transposed):
    r = lax.broadcasted_iota(jnp.int32, (n, n), 0)
    c = lax.broadcasted_iota(jnp.int32, (n, n), 1)
    return (c >= r) if transposed else (r >= c)


UNDERFLOW = -110.0


def _fox_block_ranges(qkv, fcum):
    t = qkv.shape[0]
    blk = min(ATT_BLOCK, t)
    nb = t // blk
    q2 = jnp.sum(jnp.square(qkv[:, :A_WIDTH].astype(F32)).reshape(t, A_HEADS, 64), axis=-1)
    k2 = jnp.sum(jnp.square(qkv[:, A_WIDTH:2 * A_WIDTH].astype(F32)).reshape(t, A_HEADS, 64), axis=-1)
    bound = 2.0 * jnp.sqrt(jnp.max(q2, axis=0) * jnp.max(k2, axis=0)) * 0.125
    f = fcum[:, :A_HEADS]
    first = f[0::blk].T
    last = f[blk - 1::blk].T
    dead = (bound[:, None, None] + first[:, :, None] - last[:, None, :]) < UNDERFLOW
    qi = jnp.arange(nb)[None, :, None]
    kj = jnp.arange(nb)[None, None, :]
    dead = dead & (kj < qi)
    kstart = jnp.sum(dead, axis=2).astype(jnp.int32)
    qend = (kj[0] + jnp.sum((~dead) & (qi > kj), axis=1)).astype(jnp.int32)
    return kstart.reshape(-1), qend.reshape(-1)


def _fox_fwd(qa, ka, va, kstart, name):
    t = qa.shape[0]
    bq = min(ATT_BLOCK, t)
    nq = t // bq

    def body(ks_ref, q_ref, k_ref, v_ref, o_ref, lse_ref):
        i = pl.program_id(1)
        hp = pl.program_id(0)
        k0 = jnp.minimum(ks_ref[2 * hp * nq + i], ks_ref[(2 * hp + 1) * nq + i])

        def step(kb, carry, masked):
            rows = pl.ds(pl.multiple_of(kb * bq, bq), bq)
            new = []
            for j in (0, 1):
                cols = slice(128 * j, 128 * (j + 1))
                m, acc = carry[2 * j], carry[2 * j + 1]
                s = _dot(q_ref[:, cols], k_ref[rows, cols], "nt")
                if masked:
                    s = jnp.where(_tile_mask(bq, False), s, -jnp.inf)
                m_new = jnp.maximum(m, jnp.max(s, axis=1, keepdims=True))
                new += [m_new, jnp.exp(m - m_new) * acc + _dot(jnp.exp(s - m_new), v_ref[rows, cols])]
            return tuple(new)

        carry = (jnp.full((bq, 1), -jnp.inf, F32), jnp.zeros((bq, 128), F32)) * 2
        carry = lax.fori_loop(k0, i, lambda kb, c: step(kb, c, False), carry)
        carry = step(i, carry, True)
        outs = []
        for j in (0, 1):
            m, acc = carry[2 * j], carry[2 * j + 1]
            spare = 64 * (1 - j)
            l = acc[:, spare:spare + 1]
            outs.append((acc / l, m + jnp.log(l)))
        msk = _head_mask((bq, 128), 0)
        o_ref[...] = jnp.where(msk, outs[0][0], outs[1][0]).astype(BF16)
        lse_ref[...] = jnp.where(msk, outs[0][1], outs[1][1])

    res = pl.BlockSpec((t, 256), lambda h, i, tbl: (0, h))
    out = pl.BlockSpec((bq, 128), lambda h, i, tbl: (i, h))
    return pl.pallas_call(
        body,
        name=name,
        grid_spec=pltpu.PrefetchScalarGridSpec(
            num_scalar_prefetch=1, grid=(4, nq),
            in_specs=[pl.BlockSpec((bq, 256), lambda h, i, tbl: (i, h)), res, res],
            out_specs=[out, out]),
        out_shape=[jax.ShapeDtypeStruct((t, A_WIDTH), BF16), jax.ShapeDtypeStruct((t, A_WIDTH), F32)],
        compiler_params=_params("parallel", "parallel"),
    )(kstart, qa, ka, va)


def _fox_bwd_dq(qb, ka, va, dob, kstart, name):
    t = qb.shape[0]
    bq = min(ATT_BLOCK, t)
    nq = t // bq

    def body(ks_ref, q_ref, k_ref, v_ref, do_ref, dq_ref, rs_ref):
        i = pl.program_id(1)
        hp = pl.program_id(0)
        k0 = jnp.minimum(ks_ref[2 * hp * nq + i], ks_ref[(2 * hp + 1) * nq + i])

        def step(kb, accs, masked):
            rows = pl.ds(pl.multiple_of(kb * bq, bq), bq)
            new = []
            for j in (0, 1):
                cols = slice(128 * j, 128 * (j + 1))
                ks = k_ref[rows, cols]
                s = _dot(q_ref[:, cols], ks, "nt")
                if masked:
                    s = jnp.where(_tile_mask(bq, False), s, -jnp.inf)
                ds = jnp.exp(s) * _dot(do_ref[:, cols], v_ref[rows, cols], "nt")
                new.append(accs[j] + _dot(ds, ks))
            return tuple(new)

        accs = lax.fori_loop(k0, i, lambda kb, c: step(kb, c, False), (jnp.zeros((bq, 128), F32),) * 2)
        accs = step(i, accs, True)
        outs = []
        for j in (0, 1):
            spare = 64 * (1 - j)
            outs.append((accs[j] * 0.125, accs[j][:, spare:spare + 1]))
        msk = _head_mask((bq, 128), 0)
        dq_ref[...] = jnp.where(msk, outs[0][0], outs[1][0]).astype(BF16)
        rs_ref[...] = jnp.where(msk, outs[0][1], outs[1][1])

    blk = pl.BlockSpec((bq, 256), lambda h, i, tbl: (i, h))
    res = pl.BlockSpec((t, 256), lambda h, i, tbl: (0, h))
    out = pl.BlockSpec((bq, 128), lambda h, i, tbl: (i, h))
    return pl.pallas_call(
        body,
        name=name,
        grid_spec=pltpu.PrefetchScalarGridSpec(
            num_scalar_prefetch=1, grid=(4, nq), in_specs=[blk, res, res, blk], out_specs=[out, out]),
        out_shape=[jax.ShapeDtypeStruct((t, A_WIDTH), BF16), jax.ShapeDtypeStruct((t, A_WIDTH), F32)],
        compiler_params=_params("parallel", "parallel"),
    )(kstart, qb, ka, va, dob)


def _fox_bwd_dkv(qb, ka, va, dob, qend, name):
    t = qb.shape[0]
    bk = min(ATT_BLOCK, t)
    nk = t // bk

    def body(qe_ref, k_ref, v_ref, q_ref, do_ref, dk_ref, dv_ref, cs_ref):
        jb = pl.program_id(1)
        hp = pl.program_id(0)
        i1 = jnp.maximum(qe_ref[2 * hp * nk + jb], qe_ref[(2 * hp + 1) * nk + jb]) + 1

        def step(ib, carry, masked):
            rows = pl.ds(pl.multiple_of(ib * bk, bk), bk)
            new = []
            for j in (0, 1):
                cols = slice(128 * j, 128 * (j + 1))
                qs = q_ref[rows, cols]
                dos = do_ref[rows, cols]
                st = _dot(k_ref[:, cols], qs, "nt")
                if masked:
                    st = jnp.where(_tile_mask(bk, True), st, -jnp.inf)
                pt = jnp.exp(st)
                new += [carry[2 * j] + _dot(pt * _dot(v_ref[:, cols], dos, "nt"), qs), carry[2 * j + 1] + _dot(pt, dos)]
            return tuple(new)

        carry = step(jb, (jnp.zeros((bk, 128), F32),) * 4, True)
        carry = lax.fori_loop(jb + 1, i1, lambda ib, c: step(ib, c, False), carry)
        outs = []
        for j in (0, 1):
            spare = 64 * (1 - j)
            dk_acc, dv_acc = carry[2 * j], carry[2 * j + 1]
            outs.append((dk_acc, dv_acc, dk_acc[:, spare + 3:spare + 4]))
        msk = _head_mask((bk, 128), 0)
        dk_ref[...] = jnp.where(msk, outs[0][0], outs[1][0]).astype(BF16)
        dv_ref[...] = jnp.where(msk, outs[0][1], outs[1][1]).astype(BF16)
        cs_ref[...] = jnp.where(msk, outs[0][2], outs[1][2])

    blk = pl.BlockSpec((bk, 256), lambda h, i, tbl: (i, h))
    res = pl.BlockSpec((t, 256), lambda h, i, tbl: (0, h))
    out = pl.BlockSpec((bk, 128), lambda h, i, tbl: (i, h))
    return pl.pallas_call(
        body,
        name=name,
        grid_spec=pltpu.PrefetchScalarGridSpec(
            num_scalar_prefetch=1, grid=(4, nk), in_specs=[blk, blk, res, res], out_specs=[out, out, out]),
        out_shape=[jax.ShapeDtypeStruct((t, A_WIDTH), BF16), jax.ShapeDtypeStruct((t, A_WIDTH), BF16),
                   jax.ShapeDtypeStruct((t, A_WIDTH), F32)],
        compiler_params=_params("parallel", "parallel"),
    )(qend, ka, va, qb, dob)


HG_ROWS = 256


def _hg_gates(hb_ref, rows, h, lbh):
    qb = hb_ref[rows, h * HD:(h + 1) * HD]
    fb = hb_ref[rows, B_WIDTH + h * HD:B_WIDTH + (h + 1) * HD]
    v = hb_ref[rows, 2 * B_WIDTH + h * HD:2 * B_WIDTH + (h + 1) * HD]
    gb = hb_ref[rows, 3 * B_WIDTH + h * HD:3 * B_WIDTH + (h + 1) * HD]
    sg = _sigmoid(fb)
    f = lbh + (1.0 - lbh) * sg
    sq = _sigmoid(qb)
    return qb, sq, qb * sq, sg, f, 1.0 - f, jnp.log(f), v, gb


def _hg_intra_factors(q, k, b):
    fac = []
    for i in range(CHUNK // SUB):
        bi = b[SUB * i:SUB * i + 1, :]
        eq = jnp.exp(b[SUB * i:SUB * (i + 1), :] - bi)
        ek = jnp.exp(jnp.minimum(bi - b, EXP_CLAMP))
        fac.append((eq, ek, q[SUB * i:SUB * (i + 1), :] * eq, k * ek))
    return fac


def _causal(n):
    r = lax.broadcasted_iota(jnp.int32, (n, n), 0)
    c = lax.broadcasted_iota(jnp.int32, (n, n), 1)
    return r >= c


def _hgrn_fwd(rest, lb, ng, name):
    t = rest.shape[0]
    bt = min(HG_ROWS, t)
    ncb = bt // CHUNK

    def body(hb_ref, lb_ref, ng_ref, y_ref, o_ref, st_ref, s_scr):
        @pl.when(pl.program_id(0) == 0)
        def _():
            s_scr[...] = jnp.zeros_like(s_scr)

        tril = _tri(CHUNK, True)
        causal = _causal(CHUNK)
        ones = jnp.ones((CHUNK, HD), F32)

        def chunk(c, carry):
            rows = pl.ds(pl.multiple_of(c * CHUNK, CHUNK), CHUNK)
            for h in range(B_HEADS):
                lbh = lb_ref[:, h * HD:(h + 1) * HD]
                _, _, q, _, _, k, g, v, gb = _hg_gates(hb_ref, rows, h, lbh)
                b = _dot_hi(tril, g)
                s0 = s_scr[h]
                st_ref[c, h] = s0
                o = _dot(q * jnp.exp(b), s0)
                a = jnp.concatenate([_dot(qe, ke, "nt") for _, _, qe, ke in _hg_intra_factors(q, k, b)], axis=0)
                o = o + _dot(jnp.where(causal, a, 0.0), v)
                blast = b[CHUNK - 1:CHUNK, :]
                kd = k * jnp.exp(blast - b)
                eb = jnp.exp(_dot_hi(g, ones, "tn"))
                s_scr[h] = eb * s0 + _dot(kd, v, "tn")
                r = lax.rsqrt(jnp.mean(o * o, axis=-1, keepdims=True) + RMS_EPS)
                o_ref[rows, h * HD:(h + 1) * HD] = o
                y_ref[rows, h * HD:(h + 1) * HD] = (o * r * ng_ref[...] * _sigmoid(gb)).astype(BF16)
            return carry

        lax.fori_loop(0, ncb, chunk, 0)

    return pl.pallas_call(
        body,
        name=name,
        grid=(t // bt,),
        in_specs=[pl.BlockSpec((bt, 4 * B_WIDTH), lambda i: (i, 1)), pl.BlockSpec((1, B_WIDTH), lambda i: (0, 0)),
                  pl.BlockSpec((1, HD), lambda i: (0, 0))],
        out_specs=[pl.BlockSpec((bt, B_WIDTH), lambda i: (i, 0)), pl.BlockSpec((bt, B_WIDTH), lambda i: (i, 0)),
                   pl.BlockSpec((ncb, B_HEADS, HD, HD), lambda i: (i, 0, 0, 0))],
        out_shape=[jax.ShapeDtypeStruct((t, B_WIDTH), BF16), jax.ShapeDtypeStruct((t, B_WIDTH), F32),
                   jax.ShapeDtypeStruct((t // CHUNK, B_HEADS, HD, HD), F32)],
        scratch_shapes=[pltpu.VMEM((B_HEADS, HD, HD), F32)],
        compiler_params=_params("arbitrary"),
    )(rest, lb, ng)


def _hgrn_bwd(dy, rest, o_saved, states, lb, ng, name):
    t = rest.shape[0]
    bt = min(HG_ROWS, t)
    ncb = bt // CHUNK
    nb = t // bt

    def body(dy_ref, hb_ref, o_ref, st_ref, lb_ref, ng_ref, dh_ref, dlb_ref, dng_ref, ds_scr):
        @pl.when(pl.program_id(0) == 0)
        def _():
            ds_scr[...] = jnp.zeros_like(ds_scr)
            dlb_ref[...] = jnp.zeros_like(dlb_ref)
            dng_ref[...] = jnp.zeros_like(dng_ref)

        tril = _tri(CHUNK, True)
        triu = _tri(CHUNK, False)
        causal = _causal(CHUNK)
        ones = jnp.ones((CHUNK, HD), F32)
        ones8 = jnp.ones((8, HD), F32)
        last_row = lax.broadcasted_iota(jnp.int32, (CHUNK, HD), 0) == CHUNK - 1

        def chunk(cc, carry):
            c = ncb - 1 - cc
            rows = pl.ds(pl.multiple_of(c * CHUNK, CHUNK), CHUNK)
            for h in range(B_HEADS):
                cols = slice(h * HD, (h + 1) * HD)
                lbh = lb_ref[:, cols]
                qb, sq, q, sg, f, k, g, v, gb = _hg_gates(hb_ref, rows, h, lbh)
                o = o_ref[rows, cols]
                dyv = dy_ref[rows, cols].astype(F32)
                ngv = ng_ref[...]
                r = lax.rsqrt(jnp.mean(o * o, axis=-1, keepdims=True) + RMS_EPS)
                sgb = _sigmoid(gb)
                don = dyv * sgb
                dgb = dyv * (o * r * ngv) * sgb * (1.0 - sgb)
                dng_ref[...] += jnp.sum(don * o * r, axis=0, keepdims=True)
                doh = don * ngv
                do = r * (doh - o * (r * r) * jnp.mean(doh * o, axis=-1, keepdims=True))
                b = _dot_hi(tril, g)
                ebt = jnp.exp(b)
                s0 = st_ref[c, h]
                ds1 = ds_scr[h]
                blast = b[CHUNK - 1:CHUNK, :]
                ekd = jnp.exp(blast - b)
                kd = k * ekd
                eb = jnp.exp(_dot_hi(g, ones, "tn"))
                fac = _hg_intra_factors(q, k, b)
                a = jnp.concatenate([_dot(qe, ke, "nt") for _, _, qe, ke in fac], axis=0)
                a = jnp.where(causal, a, 0.0)
                da = jnp.where(causal, _dot(do, v, "nt"), 0.0)
                dv = _dot(a, do, "tn") + _dot(kd, ds1)
                dq = ebt * _dot(do, s0, "nt")
                dq = dq + jnp.concatenate(
                    [eq * _hdot(da[SUB * i:SUB * (i + 1), :], ke) for i, (eq, _, _, ke) in enumerate(fac)], axis=0)
                dk_state = ekd * _dot(v, ds1, "nt")
                dk = dk_state
                for i, (_, ek, qe, _) in enumerate(fac):
                    dk = dk + ek * _hdot(da[SUB * i:SUB * (i + 1), :], qe, "tn")
                ds_scr[h] = _dot(q * ebt, do, "tn") + eb * ds1
                extra = jnp.exp(blast) * _dot_hi(ones8, ds1 * s0, "nt")[0:1, :] \
                    + jnp.sum(k * dk_state, axis=0, keepdims=True)
                db = q * dq - k * dk + jnp.where(last_row, extra, 0.0)
                dg = _dot_hi(triu, db)
                df = dg / f - dk
                dlb_ref[:, cols] += jnp.sum(df * (1.0 - sg), axis=0, keepdims=True)
                dfb = df * (1.0 - lbh) * sg * (1.0 - sg)
                dqb = dq * (sq * (1.0 + qb * (1.0 - sq)))
                dh_ref[rows, h * HD:(h + 1) * HD] = dqb.astype(BF16)
                dh_ref[rows, B_WIDTH + h * HD:B_WIDTH + (h + 1) * HD] = dfb.astype(BF16)
                dh_ref[rows, 2 * B_WIDTH + h * HD:2 * B_WIDTH + (h + 1) * HD] = dv.astype(BF16)
                dh_ref[rows, 3 * B_WIDTH + h * HD:3 * B_WIDTH + (h + 1) * HD] = dgb.astype(BF16)
            return carry

        lax.fori_loop(0, ncb, chunk, 0)

    rev = lambda i: (nb - 1 - i, 0)
    return pl.pallas_call(
        body,
        name=name,
        grid=(nb,),
        in_specs=[pl.BlockSpec((bt, B_WIDTH), rev), pl.BlockSpec((bt, 4 * B_WIDTH), lambda i: (nb - 1 - i, 1)),
                  pl.BlockSpec((bt, B_WIDTH), rev),
                  pl.BlockSpec((ncb, B_HEADS, HD, HD), lambda i: (nb - 1 - i, 0, 0, 0)),
                  pl.BlockSpec((1, B_WIDTH), lambda i: (0, 0)), pl.BlockSpec((1, HD), lambda i: (0, 0))],
        out_specs=[pl.BlockSpec((bt, 4 * B_WIDTH), rev), pl.BlockSpec((1, B_WIDTH), lambda i: (0, 0)),
                   pl.BlockSpec((1, HD), lambda i: (0, 0))],
        out_shape=[jax.ShapeDtypeStruct((t, 4 * B_WIDTH), BF16), jax.ShapeDtypeStruct((1, B_WIDTH), F32),
                   jax.ShapeDtypeStruct((1, HD), F32)],
        scratch_shapes=[pltpu.VMEM((B_HEADS, HD, HD), F32)],
        compiler_params=_params("arbitrary"),
    )(dy, rest, o_saved, states, lb, ng)


def _axpy2(c0, a0, c1, a1, name, tm=512):
    t, d = a0.shape
    tm = min(tm, t)

    def body(a_ref, b_ref, o_ref):
        o_ref[...] = c0 * a_ref[...] + c1 * b_ref[...]

    row = lambda i: (i, 0)
    return pl.pallas_call(
        body, name=name, grid=(t // tm,),
        in_specs=[pl.BlockSpec((tm, d), row), pl.BlockSpec((tm, d), row)],
        out_specs=pl.BlockSpec((tm, d), row),
        out_shape=jax.ShapeDtypeStruct((t, d), F32),
        compiler_params=_params("parallel"),
    )(a0, a1)


def _split_w_in(w_in_l):
    wqkv = w_in_l[:, :3 * A_WIDTH]
    wfa = jnp.pad(w_in_l[:, 3 * A_WIDTH:3 * A_WIDTH + A_HEADS], ((0, 0), (0, 128 - A_HEADS)))
    whb = w_in_l[:, 3 * A_WIDTH + A_HEADS:3 * A_WIDTH + A_HEADS + 4 * B_WIDTH]
    wgt = w_in_l[:, 3 * A_WIDTH + A_HEADS + 4 * B_WIDTH:]
    return wqkv, jnp.concatenate([wgt, whb, wfa], axis=1)


def _merge_w_in_grad(dwall):
    o = 3 * A_WIDTH
    return jnp.concatenate([dwall[:, :o], dwall[:, o + 4096:o + 4096 + A_HEADS], dwall[:, o + 2048:o + 4096],
                            dwall[:, o:o + 2048]], axis=1)


def _layer_fwd(x, xb, w, sp, l):
    t = x.shape[0]
    n = f"l{l}_"
    wqkv, wrest = _split_w_in(w["w_in"])
    qkv = _matmul(xb, wqkv, "nn", BF16, 512, 768, D_MODEL, n + "proj_qkv")
    rest = _matmul(xb, wrest, "nn", F32, 512, 1408, D_MODEL, n + "proj_rest")
    bf = jnp.pad(sp["b_fgate"], (0, 128 - A_HEADS)).reshape(1, 128)
    fcum = _fox_gate_fwd(rest, bf, n + "fox_gate_fwd")
    fcol = jnp.repeat(fcum[:, :A_HEADS], 64, axis=1)
    qa, ka, va = _fox_prep_fwd(qkv, fcol, n + "fox_prep_fwd")
    kstart, qend = _fox_block_ranges(qkv, fcum)
    ya, lse = _fox_fwd(qa, ka, va, kstart, n + "fox_fwd")
    lb = sp["lb"].reshape(1, B_WIDTH)
    ng = sp["norm_g"].reshape(1, HD)
    yb, ob, states = _hgrn_fwd(rest, lb, ng, n + "hgrn_fwd")
    merged = _merge_fwd(ya, yb, w["w_pa"], w["w_pb"], rest, n + "merge_fwd")
    x1, x1b, xh1, rs1 = _mm_res_ln(merged, w["w_out"], x, sp["ln1_g"], sp["ln1_b"], n + "out_ln1")
    h = _matmul(x1b, w["w_ff_in"], "nn", F32, 512, 1408, D_MODEL, n + "ffn_in")
    a = _swiglu_fwd(h, n + "swiglu_fwd")
    x2, x2b, xh2, rs2 = _mm_res_ln(a, w["w_ff_out"], x1, sp["ln2_g"], sp["ln2_b"], n + "ffn_out_ln2")
    saved = dict(xb=xb, wqkv=wqkv, wrest=wrest, qkv=qkv, rest=rest, bf=bf, fcol=fcol, ka=ka, va=va, ya=ya, lse=lse,
                 kstart=kstart, qend=qend,
                 lb=lb, ng=ng, yb=yb, ob=ob, states=states, merged=merged, x1b=x1b, xh1=xh1, rs1=rs1, h=h, a=a,
                 xh2=xh2, rs2=rs2)
    return x2, x2b, saved


def _layer_bwd(dys, coefs, w, sp, s, l):
    n = f"l{l}_"
    dz2, dz2b, dg2, db2 = _ln_bwd(dys, coefs, s["xh2"], s["rs2"], sp["ln2_g"], n + "ln2_bwd")
    da = _matmul(dz2b, w["w_ff_out"], "nt", F32, 512, 1408, D_MODEL, n + "ffn_out_dx")
    d_wffout = _matmul(s["a"], dz2b, "tn", F32, 1408, 1024, 512, n + "ffn_out_dw")
    dh = _swiglu_bwd(da, s["h"], n + "swiglu_bwd")
    dx1f = _matmul(dh, w["w_ff_in"], "nt", F32, 512, 1024, 1408, n + "ffn_in_dx")
    d_wffin = _matmul(s["x1b"], dh, "tn", F32, 1024, 1408, 512, n + "ffn_in_dw")
    dz1, dz1b, dg1, db1 = _ln_bwd([dz2, dx1f], [ALPHA, 1.0], s["xh1"], s["rs1"], sp["ln1_g"], n + "ln1_bwd")
    dmerged = _matmul(dz1b, w["w_out"], "nt", F32, 512, 1024, D_MODEL, n + "out_dx")
    d_wout = _matmul(s["merged"], dz1b, "tn", F32, 1024, 1024, 512, n + "out_dw")
    dgates, dpa, dpb = _merge_bwd(dmerged, s["ya"], s["yb"], w["w_pa"], w["w_pb"], s["rest"], n + "merge_bwd")
    dya = _matmul(dpa, w["w_pa"], "nt", BF16, 512, 512, D_MODEL, n + "pa_dx")
    d_wpa = _matmul(s["ya"], dpa, "tn", F32, 512, 1024, 512, n + "pa_dw")
    dyb = _matmul(dpb, w["w_pb"], "nt", F32, 512, 512, D_MODEL, n + "pb_dx")
    d_wpb = _matmul(s["yb"], dpb, "tn", F32, 512, 1024, 512, n + "pb_dw")
    qb, dob = _fox_prep_bwd(s["qkv"], s["fcol"], s["lse"], dya, s["ya"], n + "fox_prep_bwd")
    dq, rsum = _fox_bwd_dq(qb, s["ka"], s["va"], dob, s["kstart"], n + "fox_bwd_dq")
    dk, dv, csum = _fox_bwd_dkv(qb, s["ka"], s["va"], dob, s["qend"], n + "fox_bwd_dkv")
    d_fcum = jnp.pad(rsum[:, ::64] - csum[:, ::64], ((0, 0), (0, 128 - A_HEADS)))
    dfa, dbf = _fox_gate_bwd(d_fcum, s["rest"], s["bf"], n + "fox_gate_bwd")
    dhb, dlb, dng = _hgrn_bwd(dyb, s["rest"], s["ob"], s["states"], s["lb"], s["ng"], n + "hgrn_bwd")
    dproj = jnp.concatenate([dq, dk, dv, dgates, dhb, dfa], axis=1)
    wall = jnp.concatenate([s["wqkv"], s["wrest"]], axis=1)
    dxp = _matmul(dproj, wall, "nt", F32, 512, 1024, 1920, n + "proj_dx")
    d_wall = _matmul(s["xb"], dproj, "tn", F32, 1024, 1152, 512, n + "proj_dw")
    grads = dict(w_in=_merge_w_in_grad(d_wall), w_pa=d_wpa, w_pb=d_wpb, w_out=d_wout, w_ff_in=d_wffin,
                 w_ff_out=d_wffout, b_fgate=dbf[0, :A_HEADS], lb=dlb[0], norm_g=dng[0], ln1_g=dg1[0], ln1_b=db1[0],
                 ln2_g=dg2[0], ln2_b=db2[0])
    return [dz1, dxp], [ALPHA, 1.0], grads


def _lower_bounds(logits):
    sm = jax.nn.softmax(logits.astype(F32), axis=0)
    return jnp.cumsum(sm, axis=0) - sm[0:1]


def _local_step(x, target, wfull, small):
    lbs, lb_vjp = jax.vjp(_lower_bounds, small["hgrn_lb_logits"])
    h, hb = x, x.astype(BF16)
    saved, sps = [], []
    for l in range(DEPTH):
        sp = dict(b_fgate=small["b_fgate"][l], lb=lbs[l], norm_g=small["hgrn_norm_g"][l], ln1_g=small["ln1_g"][l],
                  ln1_b=small["ln1_b"][l], ln2_g=small["ln2_g"][l], ln2_b=small["ln2_b"][l])
        h, hb, s = _layer_fwd(h, hb, wfull[l], sp, l)
        saved.append(s)
        sps.append(sp)
    dy, lpart = _loss_head(h, target)
    dys, coefs = [dy], [1.0]
    grads = [None] * DEPTH
    for l in reversed(range(DEPTH)):
        dys, coefs, grads[l] = _layer_bwd(dys, coefs, wfull[l], sps[l], saved[l], l)
    grad_x = _axpy2(coefs[0], dys[0], coefs[1], dys[1], "grad_x")
    d_logits = lb_vjp(jnp.stack([grads[l]["lb"] for l in range(DEPTH)]))[0]
    return lpart[0, 0], grad_x, grads, d_logits


_BIG = [("w_in", "w_in", (D_MODEL, IN_TOTAL), 1), ("w_branch_a", "w_pa", (A_WIDTH, D_MODEL), 1),
        ("w_branch_b", "w_pb", (B_WIDTH, D_MODEL), 1), ("w_out", "w_out", (D_MODEL, D_MODEL), 0),
        ("w_ff_in", "w_ff_in", (D_MODEL, 2 * FFN_HIDDEN), 1), ("w_ff_out", "w_ff_out", (FFN_HIDDEN, D_MODEL), 0)]
_SMALL = [("b_fgate", A_HEADS), ("hgrn_lb_logits", B_WIDTH), ("hgrn_norm_g", HD), ("ln1_g", D_MODEL),
          ("ln1_b", D_MODEL), ("ln2_g", D_MODEL), ("ln2_b", D_MODEL)]
N_BIG = len(_BIG)
SMALL_ROWS = 80


def _by_chip(full, axis):
    if axis == 0:
        return full.reshape(N_CHIPS, full.shape[0] // N_CHIPS, full.shape[1])
    n = full.shape[1] // N_CHIPS
    return jnp.stack([full[:, q * n:(q + 1) * n] for q in range(N_CHIPS)])


def _from_chips(shards, axis):
    if axis == 0:
        return shards.reshape(N_CHIPS * shards.shape[1], shards.shape[2])
    return jnp.concatenate([shards[q] for q in range(N_CHIPS)], axis=1)


def _pack_small(per_name):
    flat = jnp.concatenate([per_name[name].reshape(-1) for name, _ in _SMALL])
    return jnp.pad(flat, (0, SMALL_ROWS * 128 - flat.shape[0])).reshape(SMALL_ROWS, 128)


def _unpack_small(slab):
    flat, out, r = slab.reshape(-1), {}, 0
    for name, n in _SMALL:
        out[name] = flat[r:r + DEPTH * n].reshape(DEPTH, n)
        r += DEPTH * n
    return out


_ANY = pl.BlockSpec(memory_space=pl.ANY)


def _place():
    return lax.axis_index("x"), lax.axis_index("y"), lax.axis_index("c")


def _other_chips(x, y):
    return [(1 - x, y), (x, 1 - y), (1 - x, 1 - y)]


def _chip_exchange(mine_of, out_refs, send_sems, recv_sems, local_sems):
    x, y, c = _place()
    q = 2 * x + y
    started = []
    for w, out_ref in enumerate(out_refs):
        local = pltpu.make_async_copy(mine_of(w, q), out_ref.at[q], local_sems.at[w])
        local.start()
        started.append(local)
    sends = []
    for k, (px, py) in enumerate(_other_chips(x, y)):
        for w, out_ref in enumerate(out_refs):
            cp = pltpu.make_async_remote_copy(src_ref=mine_of(w, 2 * px + py), dst_ref=out_ref.at[q],
                                              send_sem=send_sems.at[3 * w + k], recv_sem=recv_sems.at[3 * w + k],
                                              device_id=(px, py, c), device_id_type=MESH)
            cp.start()
            sends.append(cp)
    for k, (px, py) in enumerate(_other_chips(x, y)):
        for w, out_ref in enumerate(out_refs):
            pltpu.make_async_remote_copy(src_ref=mine_of(w, q), dst_ref=out_ref.at[2 * px + py],
                                         send_sem=send_sems.at[3 * w + k], recv_sem=recv_sems.at[3 * w + k],
                                         device_id=(px, py, c), device_id_type=MESH).wait_recv()
    for cp in sends:
        cp.wait_send()
    for local in started:
        local.wait()


def _sem_scratch(n):
    return [pltpu.SemaphoreType.DMA((3 * n,)), pltpu.SemaphoreType.DMA((3 * n,)), pltpu.SemaphoreType.DMA((n,))]


def _gather_weights(mine):
    n = len(mine)

    def body(*refs):
        in_refs, out_refs = refs[:n], refs[n:2 * n]
        send_sems, recv_sems, local_sems, pair_send, pair_recv = refs[2 * n:]
        x, y, c = _place()
        _chip_exchange(lambda w, q: in_refs[w].at[c], [o.at[c] for o in out_refs], send_sems, recv_sems, local_sems)
        sibling = (x, y, 1 - c)
        fwds = []
        for w, o in enumerate(out_refs):
            cp = pltpu.make_async_remote_copy(src_ref=o.at[c], dst_ref=o.at[c], send_sem=pair_send.at[w],
                                              recv_sem=pair_recv.at[w], device_id=sibling, device_id_type=MESH)
            cp.start()
            fwds.append(cp)
        for w, o in enumerate(out_refs):
            pltpu.make_async_remote_copy(src_ref=o.at[1 - c], dst_ref=o.at[1 - c], send_sem=pair_send.at[w],
                                         recv_sem=pair_recv.at[w], device_id=sibling, device_id_type=MESH).wait_recv()
        for cp in fwds:
            cp.wait_send()

    return pl.pallas_call(
        body, name="gather_weights", in_specs=[_ANY] * n, out_specs=[_ANY] * n,
        out_shape=[jax.ShapeDtypeStruct((DEPTH, N_CHIPS) + m.shape[1:], m.dtype) for m in mine],
        scratch_shapes=_sem_scratch(n) + [pltpu.SemaphoreType.DMA((n,)), pltpu.SemaphoreType.DMA((n,))],
    )(*mine)


def _pair_exchange(gs):
    n = len(gs)

    def body(*refs):
        g_refs, a_refs, send_sems, recv_sems = refs[:n], refs[n:2 * n], refs[2 * n], refs[2 * n + 1]
        x, y, c = _place()
        cps = []
        for w in range(n):
            cp = pltpu.make_async_remote_copy(src_ref=g_refs[w].at[1 - c], dst_ref=a_refs[w], send_sem=send_sems.at[w],
                                              recv_sem=recv_sems.at[w], device_id=(x, y, 1 - c), device_id_type=MESH)
            cp.start()
            cps.append(cp)
        for cp in cps:
            cp.wait()

    return pl.pallas_call(
        body, name="grad_pair_exchange", in_specs=[_ANY] * n, out_specs=[_ANY] * n,
        out_shape=[jax.ShapeDtypeStruct(g.shape[1:], g.dtype) for g in gs],
        scratch_shapes=[pltpu.SemaphoreType.DMA((n,)), pltpu.SemaphoreType.DMA((n,))],
    )(*gs)


def _row_block(rows):
    return math.gcd(rows, 256)


def _pair_sum(g, a, layer, name):
    _, nq, rows, cols = g.shape
    tb = _row_block(rows)

    def body(l_ref, g_ref, a_ref, o_ref):
        o_ref[...] = (g_ref[...] + a_ref[...]).astype(BF16)

    return pl.pallas_call(
        body, name=name,
        grid_spec=pltpu.PrefetchScalarGridSpec(
            num_scalar_prefetch=1, grid=(nq, rows // tb),
            in_specs=[pl.BlockSpec((None, None, tb, cols), lambda q, i, l_ref: (l_ref[0], q, i, 0)),
                      pl.BlockSpec((None, tb, cols), lambda q, i, l_ref: (q, i, 0))],
            out_specs=pl.BlockSpec((None, tb, cols), lambda q, i, l_ref: (q, i, 0))),
        out_shape=jax.ShapeDtypeStruct((nq, rows, cols), BF16),
        compiler_params=_params("parallel", "parallel"),
    )(layer.reshape(1).astype(jnp.int32), g, a)


def _shard_exchange(ps):
    n = len(ps)

    def body(*refs):
        p_refs, b_refs = refs[:n], refs[n:2 * n]
        send_sems, recv_sems, local_sems = refs[2 * n:]
        _chip_exchange(lambda w, q: p_refs[w].at[q], b_refs, send_sems, recv_sems, local_sems)

    return pl.pallas_call(
        body, name="grad_shard_exchange", in_specs=[_ANY] * n, out_specs=[_ANY] * n,
        out_shape=[jax.ShapeDtypeStruct(p.shape, p.dtype) for p in ps],
        scratch_shapes=_sem_scratch(n),
    )(*ps)


def _sum4(b, name):
    _, rows, cols = b.shape
    tb = _row_block(rows)

    def body(b_ref, o_ref):
        o_ref[...] = ((b_ref[0].astype(F32) + b_ref[1].astype(F32)) + b_ref[2].astype(F32)) + b_ref[3].astype(F32)

    return pl.pallas_call(
        body, name=name, grid=(rows // tb,),
        in_specs=[pl.BlockSpec((N_CHIPS, tb, cols), lambda i: (0, i, 0))],
        out_specs=pl.BlockSpec((tb, cols), lambda i: (i, 0)),
        out_shape=jax.ShapeDtypeStruct((rows, cols), F32),
        compiler_params=_params("parallel"),
    )(b)


def _result_exchange(gcs):
    n = len(gcs)

    def body(*refs):
        g_refs, o_refs, send_sems, recv_sems = refs[:n], refs[n:2 * n], refs[2 * n], refs[2 * n + 1]
        x, y, c = _place()
        cps = []
        for w in range(n):
            cp = pltpu.make_async_remote_copy(src_ref=g_refs[w], dst_ref=o_refs[w], send_sem=send_sems.at[w],
                                              recv_sem=recv_sems.at[w], device_id=(x, y, 1 - c), device_id_type=MESH)
            cp.start()
            cps.append(cp)
        for cp in cps:
            cp.wait()

    return pl.pallas_call(
        body, name="grad_result_exchange", in_specs=[_ANY] * n, out_specs=[_ANY] * n,
        out_shape=[jax.ShapeDtypeStruct(g.shape, g.dtype) for g in gcs],
        scratch_shapes=[pltpu.SemaphoreType.DMA((n,)), pltpu.SemaphoreType.DMA((n,))],
    )(*gcs)


def _allreduce_small(v):
    def body(v_ref, o_ref, buf, send_sems, recv_sems):
        x, y, c = _place()
        me = 4 * x + 2 * y + c
        buf[me] = v_ref[...]
        peers = []
        for k in range(1, N_DEV):
            px = 1 - x if k & 4 else x
            py = 1 - y if k & 2 else y
            pc = 1 - c if k & 1 else c
            peers.append((px, py, pc))
        sends = []
        for k, peer in enumerate(peers):
            cp = pltpu.make_async_remote_copy(src_ref=v_ref, dst_ref=buf.at[me], send_sem=send_sems.at[k],
                                              recv_sem=recv_sems.at[k], device_id=peer, device_id_type=MESH)
            cp.start()
            sends.append(cp)
        for k, (px, py, pc) in enumerate(peers):
            pltpu.make_async_remote_copy(src_ref=v_ref, dst_ref=buf.at[4 * px + 2 * py + pc], send_sem=send_sems.at[k],
                                         recv_sem=recv_sems.at[k], device_id=(px, py, pc),
                                         device_id_type=MESH).wait_recv()
        for cp in sends:
            cp.wait_send()
        acc = buf[0]
        for i in range(1, N_DEV):
            acc = acc + buf[i]
        o_ref[...] = acc

    vm = pl.BlockSpec(memory_space=pltpu.VMEM)
    return pl.pallas_call(
        body, name="small_allreduce", in_specs=[vm], out_specs=vm,
        out_shape=jax.ShapeDtypeStruct(v.shape, F32),
        scratch_shapes=[pltpu.VMEM((N_DEV,) + v.shape, F32), pltpu.SemaphoreType.DMA((N_DEV - 1,)),
                        pltpu.SemaphoreType.DMA((N_DEV - 1,))],
    )(v)


def _adam_update(w, g, m, v):
    nm = ADAM_B1 * m + (1.0 - ADAM_B1) * g
    nv = ADAM_B2 * v + (1.0 - ADAM_B2) * (g * g)
    m_hat = nm / (1.0 - ADAM_B1 ** ADAM_STEP)
    v_hat = nv / (1.0 - ADAM_B2 ** ADAM_STEP)
    return -ADAM_LR * (m_hat / (jnp.sqrt(v_hat) + ADAM_EPS) + ADAM_WD * w), nm, nv


def _adamw_small(w, g, m, v, name):
    def body(w_ref, g_ref, m_ref, v_ref, d_ref, nm_ref, nv_ref):
        d_ref[...], nm_ref[...], nv_ref[...] = _adam_update(w_ref[...], g_ref[...], m_ref[...], v_ref[...])

    vm = pl.BlockSpec(memory_space=pltpu.VMEM)
    return pl.pallas_call(
        body, name=name, in_specs=[vm] * 4, out_specs=[vm] * 3,
        out_shape=[jax.ShapeDtypeStruct(w.shape, F32)] * 3,
    )(w, g, m, v)


def _adamw_big(w, m, v, g_own, g_other, layer, name):
    _, rows, cols = w.shape
    tb = _row_block(rows)

    def body(l_ref, w_ref, m_ref, v_ref, go_ref, gx_ref, g_ref, d_ref, nm_ref, nv_ref):
        gv = jnp.where(pl.program_id(0) == l_ref[0], go_ref[...], gx_ref[...])
        g_ref[...] = gv
        d_ref[...], nm_ref[...], nv_ref[...] = _adam_update(w_ref[...], gv, m_ref[...], v_ref[...])

    per_layer = pl.BlockSpec((None, tb, cols), lambda l, i, l_ref: (l, i, 0))
    shared = pl.BlockSpec((tb, cols), lambda l, i, l_ref: (i, 0))
    return pl.pallas_call(
        body, name=name,
        grid_spec=pltpu.PrefetchScalarGridSpec(
            num_scalar_prefetch=1, grid=(DEPTH, rows // tb),
            in_specs=[per_layer, per_layer, per_layer, shared, shared], out_specs=[per_layer] * 4),
        out_shape=[jax.ShapeDtypeStruct(w.shape, F32)] * 4,
        compiler_params=_params("parallel", "parallel"),
    )(layer.reshape(1).astype(jnp.int32), w, m, v, g_own, g_other)


def kernel(x, w_in, b_fgate, hgrn_lb_logits, hgrn_norm_g, w_branch_a, w_branch_b, w_out, ln1_g, ln1_b, w_ff_in, w_ff_out, ln2_g, ln2_b, loss_target, m_w_in, m_b_fgate, m_hgrn_lb_logits, m_hgrn_norm_g, m_w_branch_a, m_w_branch_b, m_w_out, m_ln1_g, m_ln1_b, m_w_ff_in, m_w_ff_out, m_ln2_g, m_ln2_b, v_w_in, v_b_fgate, v_hgrn_lb_logits, v_hgrn_norm_g, v_w_branch_a, v_w_branch_b, v_w_out, v_ln1_g, v_ln1_b, v_w_ff_in, v_w_ff_out, v_ln2_g, v_ln2_b):
    weights = dict(w_in=w_in, b_fgate=b_fgate, hgrn_lb_logits=hgrn_lb_logits, hgrn_norm_g=hgrn_norm_g,
                   w_branch_a=w_branch_a, w_branch_b=w_branch_b, w_out=w_out, ln1_g=ln1_g, ln1_b=ln1_b,
                   w_ff_in=w_ff_in, w_ff_out=w_ff_out, ln2_g=ln2_g, ln2_b=ln2_b)
    mom1 = dict(w_in=m_w_in, b_fgate=m_b_fgate, hgrn_lb_logits=m_hgrn_lb_logits, hgrn_norm_g=m_hgrn_norm_g,
                w_branch_a=m_w_branch_a, w_branch_b=m_w_branch_b, w_out=m_w_out, ln1_g=m_ln1_g, ln1_b=m_ln1_b,
                w_ff_in=m_w_ff_in, w_ff_out=m_w_ff_out, ln2_g=m_ln2_g, ln2_b=m_ln2_b)
    mom2 = dict(w_in=v_w_in, b_fgate=v_b_fgate, hgrn_lb_logits=v_hgrn_lb_logits, hgrn_norm_g=v_hgrn_norm_g,
                w_branch_a=v_w_branch_a, w_branch_b=v_w_branch_b, w_out=v_w_out, ln1_g=v_ln1_g, ln1_b=v_ln1_b,
                w_ff_in=v_w_ff_in, w_ff_out=v_w_ff_out, ln2_g=v_ln2_g, ln2_b=v_ln2_b)
    core = lax.axis_index("c")

    gathered = _gather_weights([weights[name].astype(BF16) for name, _, _, _ in _BIG])
    wfull = [{key: _from_chips(gathered[w][l], axis) for w, (_, key, _, axis) in enumerate(_BIG)}
             for l in range(DEPTH)]
    small = {name: weights[name] for name, _ in _SMALL}

    loss_part, grad_x, grads, d_logits = _local_step(x[0], loss_target[0], wfull, small)

    g_all = [jnp.stack([_by_chip(grads[l][key], axis) for l in range(DEPTH)]) for _, key, _, axis in _BIG]
    received = _pair_exchange(g_all)
    pair = [_pair_sum(g_all[w], received[w], core, f"grad_pair_sum_{w}") for w in range(N_BIG)]
    by_chip = _shard_exchange(pair)
    g_layer = [_sum4(by_chip[w], f"grad_chip_sum_{w}") for w in range(N_BIG)]
    g_other = _result_exchange(g_layer)
    out_g, out_d, out_m, out_v = {}, {}, {}, {}
    for w, (name, _, _, _) in enumerate(_BIG):
        out_g[name], out_d[name], out_m[name], out_v[name] = _adamw_big(
            weights[name], mom1[name], mom2[name], g_layer[w], g_other[w], core, f"adamw_{name}")

    small_grads = {name: jnp.stack([grads[l][key] for l in range(DEPTH)])
                   for name, key in [("b_fgate", "b_fgate"), ("hgrn_norm_g", "norm_g"), ("ln1_g", "ln1_g"),
                                     ("ln1_b", "ln1_b"), ("ln2_g", "ln2_g"), ("ln2_b", "ln2_b")]}
    small_grads["hgrn_lb_logits"] = d_logits
    gs = _allreduce_small(_pack_small(small_grads))
    ds, ms, vs = _adamw_small(_pack_small(small), gs, _pack_small({n: mom1[n] for n, _ in _SMALL}),
                              _pack_small({n: mom2[n] for n, _ in _SMALL}), "adamw_small")
    for tree, slab in ((out_g, gs), (out_d, ds), (out_m, ms), (out_v, vs)):
        tree.update(_unpack_small(slab))

    loss = lax.psum(loss_part, ("x", "y", "c"))
    order = ["w_in", "b_fgate", "hgrn_lb_logits", "hgrn_norm_g", "w_branch_a", "w_branch_b", "w_out", "ln1_g", "ln1_b",
             "w_ff_in", "w_ff_out", "ln2_g", "ln2_b"]
    return (loss, grad_x[None], *[out_g[n] for n in order], *[out_d[n] for n in order],
            *[out_m[n] for n in order], *[out_v[n] for n in order])
```

```python
import functools
import math

import jax
import jax.numpy as jnp
import numpy as np
from jax import lax
from jax.experimental import pallas as pl
from jax.experimental.pallas import tpu as pltpu

F32 = jnp.float32
BF16 = jnp.bfloat16

D_MODEL = 1024
DEPTH = 2
A_HEADS = 8
A_WIDTH = 512
B_WIDTH = 512
B_HEADS = 4
HD = 128
CHUNK = 64
SUB = 16
FFN_HIDDEN = 2816
IN_TOTAL = 5640
ALPHA = (2 * DEPTH) ** 0.25
LN_EPS = 1e-5
RMS_EPS = 1e-6
ADAM_LR = 0.001
ADAM_B1 = 0.9
ADAM_B2 = 0.999
ADAM_EPS = 1e-08
ADAM_WD = 0.01
ADAM_STEP = 10
EXP_CLAMP = 60.0

VMEM_LIMIT_BYTES = 56 * 1024 * 1024
MM_ROWS = 1024
DW_ROWS = 2048
N_CHIPS = 4
N_DEV = 8
MESH = pl.DeviceIdType.MESH

_DN = {
    "nn": (((1,), (0,)), ((), ())),
    "nt": (((1,), (1,)), ((), ())),
    "tn": (((0,), (0,)), ((), ())),
}


def _dot(a, b, mode="nn"):
    return lax.dot_general(a.astype(BF16), b.astype(BF16), _DN[mode], preferred_element_type=F32)


def _dot_hi(a, b, mode="nn"):
    return lax.dot_general(a, b, _DN[mode], precision=lax.Precision.HIGHEST, preferred_element_type=F32)


def _hdot(a, b, mode="nn"):
    return _dot_hi(a.astype(F32), b.astype(F32), mode)


def _params(*sem):
    return pltpu.CompilerParams(dimension_semantics=sem, vmem_limit_bytes=VMEM_LIMIT_BYTES)


def _sigmoid(x):
    return 1.0 / (1.0 + jnp.exp(-x))


def _matmul(a, b, mode, out_dtype, tm, tn, tk, name):
    if mode == "nn":
        (m, k), (k2, n) = a.shape, b.shape
    elif mode == "nt":
        (m, k), (n, k2) = a.shape, b.shape
    else:
        (k, m), (k2, n) = a.shape, b.shape
    assert k == k2, (a.shape, b.shape, mode)
    tm, tn, tk = min(tm, m), min(tn, n), min(tk, k)
    assert m % tm == 0 and n % tn == 0 and k % tk == 0, (a.shape, b.shape, tm, tn, tk)
    nk = k // tk
    if mode == "tn":
        a_spec = pl.BlockSpec((tk, tm), lambda j, i, kk: (kk, i))
    else:
        a_spec = pl.BlockSpec((tm, tk), lambda j, i, kk: (i, kk))
    if mode == "nt":
        b_spec = pl.BlockSpec((tn, tk), lambda j, i, kk: (j, kk))
    else:
        b_spec = pl.BlockSpec((tk, tn), lambda j, i, kk: (kk, j))
    use_acc = nk > 1 and out_dtype != F32

    def body(a_ref, b_ref, o_ref, *scratch):
        p = _dot(a_ref[...], b_ref[...], mode)
        if nk == 1:
            o_ref[...] = p.astype(out_dtype)
            return
        acc_ref = scratch[0] if use_acc else o_ref
        kk = pl.program_id(2)

        @pl.when(kk == 0)
        def _():
            acc_ref[...] = p

        @pl.when(kk > 0)
        def _():
            acc_ref[...] += p

        if use_acc:
            @pl.when(kk == nk - 1)
            def _():
                o_ref[...] = acc_ref[...].astype(out_dtype)

    return pl.pallas_call(
        body,
        name=name,
        grid=(n // tn, m // tm, nk),
        in_specs=[a_spec, b_spec],
        out_specs=pl.BlockSpec((tm, tn), lambda j, i, kk: (i, j)),
        out_shape=jax.ShapeDtypeStruct((m, n), out_dtype),
        scratch_shapes=[pltpu.VMEM((tm, tn), F32)] if use_acc else [],
        compiler_params=_params("parallel", "parallel", "arbitrary"),
    )(a, b)


def _mm_res_ln(a, w, res, g, b, name, tm=512):
    t, k = a.shape
    d = w.shape[1]
    tm = min(tm, t)

    def body(a_ref, w_ref, r_ref, g_ref, b_ref, y_ref, yb_ref, xh_ref, rs_ref):
        z = ALPHA * r_ref[...] + _dot(a_ref[...], w_ref[...])
        mu = jnp.mean(z, axis=-1, keepdims=True)
        zc = z - mu
        var = jnp.mean(zc * zc, axis=-1, keepdims=True)
        rstd = lax.rsqrt(var + LN_EPS)
        xh = zc * rstd
        y = xh * g_ref[...] + b_ref[...]
        y_ref[...] = y
        yb_ref[...] = y.astype(BF16)
        xh_ref[...] = xh
        rs_ref[...] = rstd

    row = lambda i: (i, 0)
    fix = lambda i: (0, 0)
    return pl.pallas_call(
        body,
        name=name,
        grid=(t // tm,),
        in_specs=[pl.BlockSpec((tm, k), row), pl.BlockSpec((k, d), fix), pl.BlockSpec((tm, d), row),
                  pl.BlockSpec((1, d), fix), pl.BlockSpec((1, d), fix)],
        out_specs=[pl.BlockSpec((tm, d), row), pl.BlockSpec((tm, d), row), pl.BlockSpec((tm, d), row),
                   pl.BlockSpec((tm, 1), row)],
        out_shape=[jax.ShapeDtypeStruct((t, d), F32), jax.ShapeDtypeStruct((t, d), BF16),
                   jax.ShapeDtypeStruct((t, d), F32), jax.ShapeDtypeStruct((t, 1), F32)],
        compiler_params=_params("parallel"),
    )(a, w, res, g.reshape(1, d), b.reshape(1, d))


def _ln_bwd(dys, coefs, xhat, rstd, g, name, tm=512):
    t, d = xhat.shape
    tm = min(tm, t)
    n_in = len(dys)

    def body(*refs):
        dy_refs = refs[:n_in]
        xh_ref, rs_ref, g_ref, dz_ref, dzb_ref, dg_ref, db_ref = refs[n_in:]
        dy = coefs[0] * dy_refs[0][...].astype(F32)
        for c, r in zip(coefs[1:], dy_refs[1:]):
            dy = dy + c * r[...].astype(F32)
        xh = xh_ref[...]
        dxh = dy * g_ref[...]
        m1 = jnp.mean(dxh, axis=-1, keepdims=True)
        m2 = jnp.mean(dxh * xh, axis=-1, keepdims=True)
        dz = rs_ref[...] * (dxh - m1 - xh * m2)
        dz_ref[...] = dz
        dzb_ref[...] = dz.astype(BF16)
        pg = jnp.sum(dy * xh, axis=0, keepdims=True)
        pb = jnp.sum(dy, axis=0, keepdims=True)

        @pl.when(pl.program_id(0) == 0)
        def _():
            dg_ref[...] = pg
            db_ref[...] = pb

        @pl.when(pl.program_id(0) > 0)
        def _():
            dg_ref[...] += pg
            db_ref[...] += pb

    row = lambda i: (i, 0)
    fix = lambda i: (0, 0)
    return pl.pallas_call(
        body,
        name=name,
        grid=(t // tm,),
        in_specs=[pl.BlockSpec((tm, d), row)] * n_in
        + [pl.BlockSpec((tm, d), row), pl.BlockSpec((tm, 1), row), pl.BlockSpec((1, d), fix)],
        out_specs=[pl.BlockSpec((tm, d), row), pl.BlockSpec((tm, d), row), pl.BlockSpec((1, d), fix),
                   pl.BlockSpec((1, d), fix)],
        out_shape=[jax.ShapeDtypeStruct((t, d), F32), jax.ShapeDtypeStruct((t, d), BF16),
                   jax.ShapeDtypeStruct((1, d), F32), jax.ShapeDtypeStruct((1, d), F32)],
        compiler_params=_params("arbitrary"),
    )(*dys, xhat, rstd, g.reshape(1, d))


def _loss_head(y, target, name="loss_head", tm=512):
    t, d = y.shape
    tm = min(tm, t)

    def body(y_ref, t_ref, dy_ref, l_ref):
        e = y_ref[...] - t_ref[...]
        dy_ref[...] = e * (1.0 / d)
        part = jnp.full((8, 128), 0.5 / d, F32) * jnp.sum(e * e)

        @pl.when(pl.program_id(0) == 0)
        def _():
            l_ref[...] = part

        @pl.when(pl.program_id(0) > 0)
        def _():
            l_ref[...] += part

    row = lambda i: (i, 0)
    return pl.pallas_call(
        body,
        name=name,
        grid=(t // tm,),
        in_specs=[pl.BlockSpec((tm, d), row), pl.BlockSpec((tm, d), row)],
        out_specs=[pl.BlockSpec((tm, d), row), pl.BlockSpec((8, 128), lambda i: (0, 0))],
        out_shape=[jax.ShapeDtypeStruct((t, d), F32), jax.ShapeDtypeStruct((8, 128), F32)],
        compiler_params=_params("arbitrary"),
    )(y, target)


FFN_COLS = FFN_HIDDEN // 2


def _ffn_in_swiglu(xb, wu, wg, name, tm=512):
    t, d = xb.shape
    tm = min(tm, t)

    def body(x_ref, wu_ref, wg_ref, a_ref, u_ref, g_ref):
        x = x_ref[...]
        u = _dot(x, wu_ref[...])
        g = _dot(x, wg_ref[...])
        u_ref[...] = u
        g_ref[...] = g
        a_ref[...] = (g * _sigmoid(g) * u).astype(BF16)

    wspec = pl.BlockSpec((d, FFN_COLS), lambda j, i: (0, j))
    out = pl.BlockSpec((tm, FFN_COLS), lambda j, i: (i, j))
    return pl.pallas_call(
        body,
        name=name,
        grid=(FFN_HIDDEN // FFN_COLS, t // tm),
        in_specs=[pl.BlockSpec((tm, d), lambda j, i: (i, 0)), wspec, wspec],
        out_specs=[out, out, out],
        out_shape=[jax.ShapeDtypeStruct((t, FFN_HIDDEN), BF16), jax.ShapeDtypeStruct((t, FFN_HIDDEN), F32),
                   jax.ShapeDtypeStruct((t, FFN_HIDDEN), F32)],
        compiler_params=_params("parallel", "parallel"),
    )(xb, wu, wg)


def _ffn_out_dx_swiglu(dzb, w_ff_out, u, g, name, tm=512):
    t, d = dzb.shape
    tm = min(tm, t)

    def body(dz_ref, w_ref, u_ref, g_ref, du_ref, dg_ref):
        da = _dot(dz_ref[...], w_ref[...], "nt")
        gv = g_ref[...]
        sg = _sigmoid(gv)
        du_ref[...] = (da * gv * sg).astype(BF16)
        dg_ref[...] = (da * u_ref[...] * (sg * (1.0 + gv * (1.0 - sg)))).astype(BF16)

    blk = pl.BlockSpec((tm, FFN_COLS), lambda j, i: (i, j))
    return pl.pallas_call(
        body,
        name=name,
        grid=(FFN_HIDDEN // FFN_COLS, t // tm),
        in_specs=[pl.BlockSpec((tm, d), lambda j, i: (i, 0)), pl.BlockSpec((FFN_COLS, d), lambda j, i: (j, 0)), blk, blk],
        out_specs=[blk, blk],
        out_shape=[jax.ShapeDtypeStruct((t, FFN_HIDDEN), BF16)] * 2,
        compiler_params=_params("parallel", "parallel"),
    )(dzb, w_ff_out, u, g)


def _merge_fwd(ya, yb, wpa, wpb, rest, name, tm=512):
    t = ya.shape[0]
    tm = min(tm, t)

    def body(ya_ref, yb_ref, wa_ref, wb_ref, ga_ref, gb_ref, o_ref):
        pa = _dot(ya_ref[...], wa_ref[...])
        pb = _dot(yb_ref[...], wb_ref[...])
        o_ref[...] = (_sigmoid(ga_ref[...]) * pa + _sigmoid(gb_ref[...]) * pb).astype(BF16)

    row = lambda i: (i, 0)
    fix = lambda i: (0, 0)
    return pl.pallas_call(
        body,
        name=name,
        grid=(t // tm,),
        in_specs=[pl.BlockSpec((tm, A_WIDTH), row), pl.BlockSpec((tm, B_WIDTH), row),
                  pl.BlockSpec((A_WIDTH, D_MODEL), fix), pl.BlockSpec((B_WIDTH, D_MODEL), fix),
                  pl.BlockSpec((tm, D_MODEL), lambda i: (i, 0)), pl.BlockSpec((tm, D_MODEL), lambda i: (i, 1))],
        out_specs=pl.BlockSpec((tm, D_MODEL), row),
        out_shape=jax.ShapeDtypeStruct((t, D_MODEL), BF16),
        compiler_params=_params("parallel"),
    )(ya, yb, wpa, wpb, rest, rest)


def _merge_bwd(dzb, w_out, ya, yb, wpa, wpb, rest, name, tm=512):
    t = ya.shape[0]
    tm = min(tm, t)

    def body(dz_ref, wo_ref, ya_ref, yb_ref, wa_ref, wb_ref, ga_ref, gb_ref, dg_ref, dpa_ref, dpb_ref, dya_ref,
             dyb_ref):
        dm_v = _dot(dz_ref[...], wo_ref[...], "nt")
        pa = _dot(ya_ref[...], wa_ref[...])
        pb = _dot(yb_ref[...], wb_ref[...])
        sa = _sigmoid(ga_ref[...])
        sb = _sigmoid(gb_ref[...])
        dg_ref[:, :D_MODEL] = (dm_v * pa * sa * (1.0 - sa)).astype(BF16)
        dg_ref[:, D_MODEL:] = (dm_v * pb * sb * (1.0 - sb)).astype(BF16)
        dpa = (dm_v * sa).astype(BF16)
        dpb = (dm_v * sb).astype(BF16)
        dpa_ref[...] = dpa
        dpb_ref[...] = dpb
        dya_ref[...] = _dot(dpa, wa_ref[...], "nt").astype(BF16)
        dyb_ref[...] = _dot(dpb, wb_ref[...], "nt")

    row = lambda i: (i, 0)
    fix = lambda i: (0, 0)
    return pl.pallas_call(
        body,
        name=name,
        grid=(t // tm,),
        in_specs=[pl.BlockSpec((tm, D_MODEL), row), pl.BlockSpec((D_MODEL, D_MODEL), fix),
                  pl.BlockSpec((tm, A_WIDTH), row), pl.BlockSpec((tm, B_WIDTH), row),
                  pl.BlockSpec((A_WIDTH, D_MODEL), fix), pl.BlockSpec((B_WIDTH, D_MODEL), fix),
                  pl.BlockSpec((tm, D_MODEL), lambda i: (i, 0)), pl.BlockSpec((tm, D_MODEL), lambda i: (i, 1))],
        out_specs=[pl.BlockSpec((tm, 2 * D_MODEL), row), pl.BlockSpec((tm, D_MODEL), row),
                   pl.BlockSpec((tm, D_MODEL), row), pl.BlockSpec((tm, A_WIDTH), row), pl.BlockSpec((tm, B_WIDTH), row)],
        out_shape=[jax.ShapeDtypeStruct((t, 2 * D_MODEL), BF16), jax.ShapeDtypeStruct((t, D_MODEL), BF16),
                   jax.ShapeDtypeStruct((t, D_MODEL), BF16), jax.ShapeDtypeStruct((t, A_WIDTH), BF16),
                   jax.ShapeDtypeStruct((t, B_WIDTH), F32)],
        compiler_params=_params("parallel"),
    )(dzb, w_out, ya, yb, wpa, wpb, rest, rest)


FA_BLOCK = 4224 // 128 - 1


def _tri(n, lower):
    r = lax.broadcasted_iota(jnp.int32, (n, n), 0)
    c = lax.broadcasted_iota(jnp.int32, (n, n), 1)
    return jnp.where((r >= c) if lower else (r <= c), 1.0, 0.0).astype(F32)


def _fox_gate_fwd(rest, bf, name, tb=512):
    t = rest.shape[0]
    tb = min(tb, t)

    def body(fa_ref, bf_ref, f_ref, carry):
        @pl.when(pl.program_id(0) == 0)
        def _():
            carry[...] = jnp.zeros_like(carry)

        z = fa_ref[...] + bf_ref[...]
        logf = jnp.minimum(z, 0.0) - jnp.log(1.0 + jnp.exp(-jnp.abs(z)))
        f = _dot_hi(_tri(tb, True), logf) + carry[...]
        f_ref[...] = f
        carry[...] = f[tb - 1:tb, :]

    return pl.pallas_call(
        body,
        name=name,
        grid=(t // tb,),
        in_specs=[pl.BlockSpec((tb, 128), lambda i: (i, FA_BLOCK)), pl.BlockSpec((1, 128), lambda i: (0, 0))],
        out_specs=pl.BlockSpec((tb, 128), lambda i: (i, 0)),
        out_shape=jax.ShapeDtypeStruct((t, 128), F32),
        scratch_shapes=[pltpu.VMEM((1, 128), F32)],
        compiler_params=_params("arbitrary"),
    )(rest, bf)


def _fox_gate_bwd(d_f, rest, bf, name, tb=512):
    t = rest.shape[0]
    tb = min(tb, t)
    nb = t // tb

    def body(df_ref, fa_ref, bf_ref, dfa_ref, dbf_ref, carry):
        @pl.when(pl.program_id(0) == 0)
        def _():
            carry[...] = jnp.zeros_like(carry)

        dlogf = _dot_hi(_tri(tb, False), df_ref[...]) + carry[...]
        carry[...] = dlogf[0:1, :]
        z = fa_ref[...] + bf_ref[...]
        dz = dlogf * _sigmoid(-z)
        dfa_ref[...] = dz.astype(BF16)
        part = jnp.sum(dz, axis=0, keepdims=True)

        @pl.when(pl.program_id(0) == 0)
        def _():
            dbf_ref[...] = part

        @pl.when(pl.program_id(0) > 0)
        def _():
            dbf_ref[...] += part

    return pl.pallas_call(
        body,
        name=name,
        grid=(nb,),
        in_specs=[pl.BlockSpec((tb, 128), lambda i: (nb - 1 - i, 0)),
                  pl.BlockSpec((tb, 128), lambda i: (nb - 1 - i, FA_BLOCK)),
                  pl.BlockSpec((1, 128), lambda i: (0, 0))],
        out_specs=[pl.BlockSpec((tb, 128), lambda i: (nb - 1 - i, 0)), pl.BlockSpec((1, 128), lambda i: (0, 0))],
        out_shape=[jax.ShapeDtypeStruct((t, 128), BF16), jax.ShapeDtypeStruct((1, 128), F32)],
        scratch_shapes=[pltpu.VMEM((1, 128), F32)],
        compiler_params=_params("arbitrary"),
    )(d_f, rest, bf)


ATT_BLOCK = 512


def _head_mask(shape, j):
    lane = lax.broadcasted_iota(jnp.int32, shape, 1)
    return (lane < 64) if j == 0 else (lane >= 64)


def _split3(x):
    h = x.astype(BF16).astype(F32)
    r = x - h
    m = r.astype(BF16).astype(F32)
    return h, m, (r - m).astype(BF16).astype(F32)


def _aug_lanes(tb, j):
    lane = lax.broadcasted_iota(jnp.int32, (tb, 128), 1)
    own = (lane < 64) if j == 0 else (lane >= 64)
    return own, lane - 64 * (1 - j)


def _aug_query(own, li, q, bias):
    h, m, l = _split3(bias)
    spare = jnp.where(li == 0, h, jnp.where(li == 1, m, jnp.where(li == 2, l, jnp.where(li < 6, 1.0, 0.0))))
    return jnp.where(own, q, spare).astype(BF16)


def _fox_prep_fwd(qkv, fcol, name, tb=512):
    t = qkv.shape[0]
    tb = min(tb, t)

    def body(q_ref, k_ref, v_ref, fc_ref, qa_ref, ka_ref, va_ref):
        fsw = pltpu.roll(fc_ref[...], 64, 1)
        q = q_ref[...].astype(F32) * 0.125
        k = k_ref[...].astype(F32)
        v = v_ref[...].astype(F32)
        h, m, l = _split3(fsw)
        for j in (0, 1):
            own, li = _aug_lanes(tb, j)
            cols = slice(128 * j, 128 * (j + 1))
            qa_ref[:, cols] = _aug_query(own, li, q, fsw)
            ks = jnp.where(li < 3, 1.0, jnp.where(li == 3, -h, jnp.where(li == 4, -m, jnp.where(li == 5, -l, 0.0))))
            ka_ref[:, cols] = jnp.where(own, k, ks).astype(BF16)
            va_ref[:, cols] = jnp.where(own, v, 1.0).astype(BF16)

    blk = pl.BlockSpec((tb, 256), lambda i, h: (i, h))
    return pl.pallas_call(
        body, name=name, grid=(t // tb, 4),
        in_specs=[pl.BlockSpec((tb, 128), lambda i, h: (i, h)), pl.BlockSpec((tb, 128), lambda i, h: (i, 4 + h)),
                  pl.BlockSpec((tb, 128), lambda i, h: (i, 8 + h)), pl.BlockSpec((tb, 128), lambda i, h: (i, h))],
        out_specs=[blk, blk, blk],
        out_shape=[jax.ShapeDtypeStruct((t, 2 * A_WIDTH), BF16)] * 3,
        compiler_params=_params("parallel", "parallel"),
    )(qkv, qkv, qkv, fcol)


def _fox_prep_bwd(qkv, fcol, lse, do, o, name, tb=512):
    t = qkv.shape[0]
    tb = min(tb, t)

    def body(q_ref, fc_ref, lse_ref, do_ref, o_ref, qb_ref, dob_ref):
        gsw = pltpu.roll(fc_ref[...] - lse_ref[...], 64, 1)
        q = q_ref[...].astype(F32) * 0.125
        do_v = do_ref[...].astype(F32)
        prod = do_v * o_ref[...].astype(F32)
        for j in (0, 1):
            own, li = _aug_lanes(tb, j)
            cols = slice(128 * j, 128 * (j + 1))
            qb_ref[:, cols] = _aug_query(own, li, q, gsw)
            delta = jnp.sum(jnp.where(own, prod, 0.0), axis=1, keepdims=True)
            h, m, l = _split3(jnp.broadcast_to(delta, (tb, 128)))
            ds = jnp.where(li == 0, -h, jnp.where(li == 1, -m, jnp.where(li == 2, -l, 0.0)))
            dob_ref[:, cols] = jnp.where(own, do_v, ds).astype(BF16)

    pair = pl.BlockSpec((tb, 128), lambda i, h: (i, h))
    blk = pl.BlockSpec((tb, 256), lambda i, h: (i, h))
    return pl.pallas_call(
        body, name=name, grid=(t // tb, 4),
        in_specs=[pair, pair, pair, pair, pair],
        out_specs=[blk, blk],
        out_shape=[jax.ShapeDtypeStruct((t, 2 * A_WIDTH), BF16)] * 2,
        compiler_params=_params("parallel", "parallel"),
    )(qkv, fcol, lse, do, o)


def _tile_mask(n, transposed):
    r = lax.broadcasted_iota(jnp.int32, (n, n), 0)
    c = lax.broadcasted_iota(jnp.int32, (n, n), 1)
    return (c >= r) if transposed else (r >= c)


UNDERFLOW = -110.0


def _fox_block_ranges(qkv, fcum):
    t = qkv.shape[0]
    blk = min(ATT_BLOCK, t)
    nb = t // blk
    q2 = jnp.sum(jnp.square(qkv[:, :A_WIDTH].astype(F32)).reshape(t, A_HEADS, 64), axis=-1)
    k2 = jnp.sum(jnp.square(qkv[:, A_WIDTH:2 * A_WIDTH].astype(F32)).reshape(t, A_HEADS, 64), axis=-1)
    bound = 2.0 * jnp.sqrt(jnp.max(q2, axis=0) * jnp.max(k2, axis=0)) * 0.125
    f = fcum[:, :A_HEADS]
    first = f[0::blk].T
    last = f[blk - 1::blk].T
    dead = (bound[:, None, None] + first[:, :, None] - last[:, None, :]) < UNDERFLOW
    qi = jnp.arange(nb)[None, :, None]
    kj = jnp.arange(nb)[None, None, :]
    dead = dead & (kj < qi)
    kstart = jnp.sum(dead, axis=2).astype(jnp.int32)
    qend = (kj[0] + jnp.sum((~dead) & (qi > kj), axis=1)).astype(jnp.int32)
    return kstart.reshape(-1), qend.reshape(-1)


def _fox_fwd(qa, ka, va, kstart, name):
    t = qa.shape[0]
    bq = min(ATT_BLOCK, t)
    nq = t // bq

    def body(ks_ref, q_ref, k_ref, v_ref, o_ref, lse_ref):
        i = pl.program_id(1)
        hp = pl.program_id(0)
        k0 = jnp.minimum(ks_ref[2 * hp * nq + i], ks_ref[(2 * hp + 1) * nq + i])

        def step(kb, carry, masked):
            rows = pl.ds(pl.multiple_of(kb * bq, bq), bq)
            new = []
            for j in (0, 1):
                cols = slice(128 * j, 128 * (j + 1))
                m, acc = carry[2 * j], carry[2 * j + 1]
                s = _dot(q_ref[:, cols], k_ref[rows, cols], "nt")
                if masked:
                    s = jnp.where(_tile_mask(bq, False), s, -jnp.inf)
                m_new = jnp.maximum(m, jnp.max(s, axis=1, keepdims=True))
                new += [m_new, jnp.exp(m - m_new) * acc + _dot(jnp.exp(s - m_new), v_ref[rows, cols])]
            return tuple(new)

        carry = (jnp.full((bq, 1), -jnp.inf, F32), jnp.zeros((bq, 128), F32)) * 2
        carry = lax.fori_loop(k0, i, lambda kb, c: step(kb, c, False), carry)
        carry = step(i, carry, True)
        outs = []
        for j in (0, 1):
            m, acc = carry[2 * j], carry[2 * j + 1]
            spare = 64 * (1 - j)
            l = acc[:, spare:spare + 1]
            outs.append((acc / l, m + jnp.log(l)))
        msk = _head_mask((bq, 128), 0)
        o_ref[...] = jnp.where(msk, outs[0][0], outs[1][0]).astype(BF16)
        lse_ref[...] = jnp.where(msk, outs[0][1], outs[1][1])

    res = pl.BlockSpec((t, 256), lambda h, i, tbl: (0, h))
    out = pl.BlockSpec((bq, 128), lambda h, i, tbl: (i, h))
    return pl.pallas_call(
        body,
        name=name,
        grid_spec=pltpu.PrefetchScalarGridSpec(
            num_scalar_prefetch=1, grid=(4, nq),
            in_specs=[pl.BlockSpec((bq, 256), lambda h, i, tbl: (i, h)), res, res],
            out_specs=[out, out]),
        out_shape=[jax.ShapeDtypeStruct((t, A_WIDTH), BF16), jax.ShapeDtypeStruct((t, A_WIDTH), F32)],
        compiler_params=_params("parallel", "parallel"),
    )(kstart, qa, ka, va)


def _fox_bwd_dq(qb, ka, va, dob, kstart, name):
    t = qb.shape[0]
    bq = min(ATT_BLOCK, t)
    nq = t // bq

    def body(ks_ref, q_ref, k_ref, v_ref, do_ref, dq_ref, rs_ref):
        i = pl.program_id(1)
        hp = pl.program_id(0)
        k0 = jnp.minimum(ks_ref[2 * hp * nq + i], ks_ref[(2 * hp + 1) * nq + i])

        def step(kb, accs, masked):
            rows = pl.ds(pl.multiple_of(kb * bq, bq), bq)
            new = []
            for j in (0, 1):
                cols = slice(128 * j, 128 * (j + 1))
                ks = k_ref[rows, cols]
                s = _dot(q_ref[:, cols], ks, "nt")
                if masked:
                    s = jnp.where(_tile_mask(bq, False), s, -jnp.inf)
                ds = jnp.exp(s) * _dot(do_ref[:, cols], v_ref[rows, cols], "nt")
                new.append(accs[j] + _dot(ds, ks))
            return tuple(new)

        accs = lax.fori_loop(k0, i, lambda kb, c: step(kb, c, False), (jnp.zeros((bq, 128), F32),) * 2)
        accs = step(i, accs, True)
        outs = []
        for j in (0, 1):
            spare = 64 * (1 - j)
            outs.append((accs[j] * 0.125, accs[j][:, spare:spare + 1]))
        msk = _head_mask((bq, 128), 0)
        dq_ref[...] = jnp.where(msk, outs[0][0], outs[1][0]).astype(BF16)
        rs_ref[...] = jnp.where(msk, outs[0][1], outs[1][1])

    blk = pl.BlockSpec((bq, 256), lambda h, i, tbl: (i, h))
    res = pl.BlockSpec((t, 256), lambda h, i, tbl: (0, h))
    out = pl.BlockSpec((bq, 128), lambda h, i, tbl: (i, h))
    return pl.pallas_call(
        body,
        name=name,
        grid_spec=pltpu.PrefetchScalarGridSpec(
            num_scalar_prefetch=1, grid=(4, nq), in_specs=[blk, res, res, blk], out_specs=[out, out]),
        out_shape=[jax.ShapeDtypeStruct((t, A_WIDTH), BF16), jax.ShapeDtypeStruct((t, A_WIDTH), F32)],
        compiler_params=_params("parallel", "parallel"),
    )(kstart, qb, ka, va, dob)


def _fox_bwd_dkv(qb, ka, va, dob, qend, name):
    t = qb.shape[0]
    bk = min(ATT_BLOCK, t)
    nk = t // bk

    def body(qe_ref, k_ref, v_ref, q_ref, do_ref, dk_ref, dv_ref, cs_ref):
        jb = pl.program_id(1)
        hp = pl.program_id(0)
        i1 = jnp.maximum(qe_ref[2 * hp * nk + jb], qe_ref[(2 * hp + 1) * nk + jb]) + 1

        def step(ib, carry, masked):
            rows = pl.ds(pl.multiple_of(ib * bk, bk), bk)
            new = []
            for j in (0, 1):
                cols = slice(128 * j, 128 * (j + 1))
                qs = q_ref[rows, cols]
                dos = do_ref[rows, cols]
                st = _dot(k_ref[:, cols], qs, "nt")
                if masked:
                    st = jnp.where(_tile_mask(bk, True), st, -jnp.inf)
                pt = jnp.exp(st)
                new += [carry[2 * j] + _dot(pt * _dot(v_ref[:, cols], dos, "nt"), qs), carry[2 * j + 1] + _dot(pt, dos)]
            return tuple(new)

        carry = step(jb, (jnp.zeros((bk, 128), F32),) * 4, True)
        carry = lax.fori_loop(jb + 1, i1, lambda ib, c: step(ib, c, False), carry)
        outs = []
        for j in (0, 1):
            spare = 64 * (1 - j)
            dk_acc, dv_acc = carry[2 * j], carry[2 * j + 1]
            outs.append((dk_acc, dv_acc, dk_acc[:, spare + 3:spare + 4]))
        msk = _head_mask((bk, 128), 0)
        dk_ref[...] = jnp.where(msk, outs[0][0], outs[1][0]).astype(BF16)
        dv_ref[...] = jnp.where(msk, outs[0][1], outs[1][1]).astype(BF16)
        cs_ref[...] = jnp.where(msk, outs[0][2], outs[1][2])

    blk = pl.BlockSpec((bk, 256), lambda h, i, tbl: (i, h))
    res = pl.BlockSpec((t, 256), lambda h, i, tbl: (0, h))
    out = pl.BlockSpec((bk, 128), lambda h, i, tbl: (i, h))
    return pl.pallas_call(
        body,
        name=name,
        grid_spec=pltpu.PrefetchScalarGridSpec(
            num_scalar_prefetch=1, grid=(4, nk), in_specs=[blk, blk, res, res], out_specs=[out, out, out]),
        out_shape=[jax.ShapeDtypeStruct((t, A_WIDTH), BF16), jax.ShapeDtypeStruct((t, A_WIDTH), BF16),
                   jax.ShapeDtypeStruct((t, A_WIDTH), F32)],
        compiler_params=_params("parallel", "parallel"),
    )(qend, ka, va, qb, dob)


HG_ROWS = 256


def _hg_gates(hb_ref, rows, h, lbh):
    qb = hb_ref[rows, h * HD:(h + 1) * HD]
    fb = hb_ref[rows, B_WIDTH + h * HD:B_WIDTH + (h + 1) * HD]
    v = hb_ref[rows, 2 * B_WIDTH + h * HD:2 * B_WIDTH + (h + 1) * HD]
    gb = hb_ref[rows, 3 * B_WIDTH + h * HD:3 * B_WIDTH + (h + 1) * HD]
    sg = _sigmoid(fb)
    f = lbh + (1.0 - lbh) * sg
    sq = _sigmoid(qb)
    return qb, sq, qb * sq, sg, f, 1.0 - f, jnp.log(f), v, gb


def _hg_intra_factors(q, k, b):
    fac = []
    for i in range(CHUNK // SUB):
        bi = b[SUB * i:SUB * i + 1, :]
        eq = jnp.exp(b[SUB * i:SUB * (i + 1), :] - bi)
        ek = jnp.exp(jnp.minimum(bi - b, EXP_CLAMP))
        fac.append((eq, ek, q[SUB * i:SUB * (i + 1), :] * eq, k * ek))
    return fac


def _causal(n):
    r = lax.broadcasted_iota(jnp.int32, (n, n), 0)
    c = lax.broadcasted_iota(jnp.int32, (n, n), 1)
    return r >= c


def _hgrn_fwd(rest, lb, ng, name):
    t = rest.shape[0]
    bt = min(HG_ROWS, t)
    ncb = bt // CHUNK

    def body(hb_ref, lb_ref, ng_ref, y_ref, o_ref, st_ref, s_scr):
        @pl.when(pl.program_id(0) == 0)
        def _():
            s_scr[...] = jnp.zeros_like(s_scr)

        tril = _tri(CHUNK, True)
        causal = _causal(CHUNK)
        ones = jnp.ones((CHUNK, HD), F32)

        def chunk(c, carry):
            rows = pl.ds(pl.multiple_of(c * CHUNK, CHUNK), CHUNK)
            for h in range(B_HEADS):
                lbh = lb_ref[:, h * HD:(h + 1) * HD]
                _, _, q, _, _, k, g, v, gb = _hg_gates(hb_ref, rows, h, lbh)
                b = _dot_hi(tril, g)
                s0 = s_scr[h]
                st_ref[c, h] = s0
                o = _dot(q * jnp.exp(b), s0)
                a = jnp.concatenate([_dot(qe, ke, "nt") for _, _, qe, ke in _hg_intra_factors(q, k, b)], axis=0)
                o = o + _dot(jnp.where(causal, a, 0.0), v)
                blast = b[CHUNK - 1:CHUNK, :]
                kd = k * jnp.exp(blast - b)
                eb = jnp.exp(_dot_hi(g, ones, "tn"))
                s_scr[h] = eb * s0 + _dot(kd, v, "tn")
                r = lax.rsqrt(jnp.mean(o * o, axis=-1, keepdims=True) + RMS_EPS)
                o_ref[rows, h * HD:(h + 1) * HD] = o
                y_ref[rows, h * HD:(h + 1) * HD] = (o * r * ng_ref[...] * _sigmoid(gb)).astype(BF16)
            return carry

        lax.fori_loop(0, ncb, chunk, 0)

    return pl.pallas_call(
        body,
        name=name,
        grid=(t // bt,),
        in_specs=[pl.BlockSpec((bt, 4 * B_WIDTH), lambda i: (i, 1)), pl.BlockSpec((1, B_WIDTH), lambda i: (0, 0)),
                  pl.BlockSpec((1, HD), lambda i: (0, 0))],
        out_specs=[pl.BlockSpec((bt, B_WIDTH), lambda i: (i, 0)), pl.BlockSpec((bt, B_WIDTH), lambda i: (i, 0)),
                   pl.BlockSpec((ncb, B_HEADS, HD, HD), lambda i: (i, 0, 0, 0))],
        out_shape=[jax.ShapeDtypeStruct((t, B_WIDTH), BF16), jax.ShapeDtypeStruct((t, B_WIDTH), F32),
                   jax.ShapeDtypeStruct((t // CHUNK, B_HEADS, HD, HD), F32)],
        scratch_shapes=[pltpu.VMEM((B_HEADS, HD, HD), F32)],
        compiler_params=_params("arbitrary"),
    )(rest, lb, ng)


def _hgrn_bwd(dy, rest, o_saved, states, lb, ng, name):
    t = rest.shape[0]
    bt = min(HG_ROWS, t)
    ncb = bt // CHUNK
    nb = t // bt

    def body(dy_ref, hb_ref, o_ref, st_ref, lb_ref, ng_ref, dh_ref, dlb_ref, dng_ref, ds_scr):
        @pl.when(pl.program_id(0) == 0)
        def _():
            ds_scr[...] = jnp.zeros_like(ds_scr)
            dlb_ref[...] = jnp.zeros_like(dlb_ref)
            dng_ref[...] = jnp.zeros_like(dng_ref)

        tril = _tri(CHUNK, True)
        triu = _tri(CHUNK, False)
        causal = _causal(CHUNK)
        ones = jnp.ones((CHUNK, HD), F32)
        ones8 = jnp.ones((8, HD), F32)
        last_row = lax.broadcasted_iota(jnp.int32, (CHUNK, HD), 0) == CHUNK - 1

        def chunk(cc, carry):
            c = ncb - 1 - cc
            rows = pl.ds(pl.multiple_of(c * CHUNK, CHUNK), CHUNK)
            for h in range(B_HEADS):
                cols = slice(h * HD, (h + 1) * HD)
                lbh = lb_ref[:, cols]
                qb, sq, q, sg, f, k, g, v, gb = _hg_gates(hb_ref, rows, h, lbh)
                o = o_ref[rows, cols]
                dyv = dy_ref[rows, cols].astype(F32)
                ngv = ng_ref[...]
                r = lax.rsqrt(jnp.mean(o * o, axis=-1, keepdims=True) + RMS_EPS)
                sgb = _sigmoid(gb)
                don = dyv * sgb
                dgb = dyv * (o * r * ngv) * sgb * (1.0 - sgb)
                dng_ref[...] += jnp.sum(don * o * r, axis=0, keepdims=True)
                doh = don * ngv
                do = r * (doh - o * (r * r) * jnp.mean(doh * o, axis=-1, keepdims=True))
                b = _dot_hi(tril, g)
                ebt = jnp.exp(b)
                s0 = st_ref[c, h]
                ds1 = ds_scr[h]
                blast = b[CHUNK - 1:CHUNK, :]
                ekd = jnp.exp(blast - b)
                kd = k * ekd
                eb = jnp.exp(_dot_hi(g, ones, "tn"))
                fac = _hg_intra_factors(q, k, b)
                a = jnp.concatenate([_dot(qe, ke, "nt") for _, _, qe, ke in fac], axis=0)
                a = jnp.where(causal, a, 0.0)
                da = jnp.where(causal, _dot(do, v, "nt"), 0.0)
                dv = _dot(a, do, "tn") + _dot(kd, ds1)
                dq = ebt * _dot(do, s0, "nt")
                dq = dq + jnp.concatenate(
                    [eq * _hdot(da[SUB * i:SUB * (i + 1), :], ke) for i, (eq, _, _, ke) in enumerate(fac)], axis=0)
                dk_state = ekd * _dot(v, ds1, "nt")
                dk = dk_state
                for i, (_, ek, qe, _) in enumerate(fac):
                    dk = dk + ek * _hdot(da[SUB * i:SUB * (i + 1), :], qe, "tn")
                ds_scr[h] = _dot(q * ebt, do, "tn") + eb * ds1
                extra = jnp.exp(blast) * _dot_hi(ones8, ds1 * s0, "nt")[0:1, :] \
                    + jnp.sum(k * dk_state, axis=0, keepdims=True)
                db = q * dq - k * dk + jnp.where(last_row, extra, 0.0)
                dg = _dot_hi(triu, db)
                df = dg / f - dk
                dlb_ref[:, cols] += jnp.sum(df * (1.0 - sg), axis=0, keepdims=True)
                dfb = df * (1.0 - lbh) * sg * (1.0 - sg)
                dqb = dq * (sq * (1.0 + qb * (1.0 - sq)))
                dh_ref[rows, h * HD:(h + 1) * HD] = dqb.astype(BF16)
                dh_ref[rows, B_WIDTH + h * HD:B_WIDTH + (h + 1) * HD] = dfb.astype(BF16)
                dh_ref[rows, 2 * B_WIDTH + h * HD:2 * B_WIDTH + (h + 1) * HD] = dv.astype(BF16)
                dh_ref[rows, 3 * B_WIDTH + h * HD:3 * B_WIDTH + (h + 1) * HD] = dgb.astype(BF16)
            return carry

        lax.fori_loop(0, ncb, chunk, 0)

    rev = lambda i: (nb - 1 - i, 0)
    return pl.pallas_call(
        body,
        name=name,
        grid=(nb,),
        in_specs=[pl.BlockSpec((bt, B_WIDTH), rev), pl.BlockSpec((bt, 4 * B_WIDTH), lambda i: (nb - 1 - i, 1)),
                  pl.BlockSpec((bt, B_WIDTH), rev),
                  pl.BlockSpec((ncb, B_HEADS, HD, HD), lambda i: (nb - 1 - i, 0, 0, 0)),
                  pl.BlockSpec((1, B_WIDTH), lambda i: (0, 0)), pl.BlockSpec((1, HD), lambda i: (0, 0))],
        out_specs=[pl.BlockSpec((bt, 4 * B_WIDTH), rev), pl.BlockSpec((1, B_WIDTH), lambda i: (0, 0)),
                   pl.BlockSpec((1, HD), lambda i: (0, 0))],
        out_shape=[jax.ShapeDtypeStruct((t, 4 * B_WIDTH), BF16), jax.ShapeDtypeStruct((1, B_WIDTH), F32),
                   jax.ShapeDtypeStruct((1, HD), F32)],
        scratch_shapes=[pltpu.VMEM((B_HEADS, HD, HD), F32)],
        compiler_params=_params("arbitrary"),
    )(dy, rest, o_saved, states, lb, ng)


def _axpy2(c0, a0, c1, a1, name, tm=512):
    t, d = a0.shape
    tm = min(tm, t)

    def body(a_ref, b_ref, o_ref):
        o_ref[...] = c0 * a_ref[...] + c1 * b_ref[...]

    row = lambda i: (i, 0)
    return pl.pallas_call(
        body, name=name, grid=(t // tm,),
        in_specs=[pl.BlockSpec((tm, d), row), pl.BlockSpec((tm, d), row)],
        out_specs=pl.BlockSpec((tm, d), row),
        out_shape=jax.ShapeDtypeStruct((t, d), F32),
        compiler_params=_params("parallel"),
    )(a0, a1)


def _split_w_in(w_in_l):
    wqkv = w_in_l[:, :3 * A_WIDTH]
    wfa = jnp.pad(w_in_l[:, 3 * A_WIDTH:3 * A_WIDTH + A_HEADS], ((0, 0), (0, 128 - A_HEADS)))
    whb = w_in_l[:, 3 * A_WIDTH + A_HEADS:3 * A_WIDTH + A_HEADS + 4 * B_WIDTH]
    wgt = w_in_l[:, 3 * A_WIDTH + A_HEADS + 4 * B_WIDTH:]
    return wqkv, jnp.concatenate([wgt, whb, wfa], axis=1)


def _merge_w_in_grad(dwall):
    o = 3 * A_WIDTH
    return jnp.concatenate([dwall[:, :o], dwall[:, o + 4096:o + 4096 + A_HEADS], dwall[:, o + 2048:o + 4096],
                            dwall[:, o:o + 2048]], axis=1)


def _layer_fwd(x, xb, w, sp, l):
    t = x.shape[0]
    n = f"l{l}_"
    wqkv, wrest = _split_w_in(w["w_in"])
    qkv = _matmul(xb, wqkv, "nn", BF16, MM_ROWS, 768, D_MODEL, n + "proj_qkv")
    rest = _matmul(xb, wrest, "nn", F32, MM_ROWS, 1408, D_MODEL, n + "proj_rest")
    bf = jnp.pad(sp["b_fgate"], (0, 128 - A_HEADS)).reshape(1, 128)
    fcum = _fox_gate_fwd(rest, bf, n + "fox_gate_fwd")
    fcol = jnp.repeat(fcum[:, :A_HEADS], 64, axis=1)
    qa, ka, va = _fox_prep_fwd(qkv, fcol, n + "fox_prep_fwd")
    kstart, qend = _fox_block_ranges(qkv, fcum)
    ya, lse = _fox_fwd(qa, ka, va, kstart, n + "fox_fwd")
    lb = sp["lb"].reshape(1, B_WIDTH)
    ng = sp["norm_g"].reshape(1, HD)
    yb, ob, states = _hgrn_fwd(rest, lb, ng, n + "hgrn_fwd")
    merged = _merge_fwd(ya, yb, w["w_pa"], w["w_pb"], rest, n + "merge_fwd")
    x1, x1b, xh1, rs1 = _mm_res_ln(merged, w["w_out"], x, sp["ln1_g"], sp["ln1_b"], n + "out_ln1")
    wu, wg = w["w_ff_in"][:, :FFN_HIDDEN], w["w_ff_in"][:, FFN_HIDDEN:]
    a, hu, hg = _ffn_in_swiglu(x1b, wu, wg, n + "ffn_in_swiglu")
    x2, x2b, xh2, rs2 = _mm_res_ln(a, w["w_ff_out"], x1, sp["ln2_g"], sp["ln2_b"], n + "ffn_out_ln2")
    saved = dict(xb=xb, wqkv=wqkv, wrest=wrest, qkv=qkv, rest=rest, bf=bf, fcol=fcol, ka=ka, va=va, ya=ya, lse=lse,
                 kstart=kstart, qend=qend,
                 lb=lb, ng=ng, yb=yb, ob=ob, states=states, merged=merged, x1b=x1b, xh1=xh1, rs1=rs1, a=a,
                 wu=wu, wg=wg, hu=hu, hg=hg,
                 xh2=xh2, rs2=rs2)
    return x2, x2b, saved


def _layer_bwd(dys, coefs, w, sp, s, l):
    n = f"l{l}_"
    dz2, dz2b, dg2, db2 = _ln_bwd(dys, coefs, s["xh2"], s["rs2"], sp["ln2_g"], n + "ln2_bwd")
    du, dg = _ffn_out_dx_swiglu(dz2b, w["w_ff_out"], s["hu"], s["hg"], n + "ffn_out_dx_swiglu")
    d_wffout = _matmul(s["a"], dz2b, "tn", F32, 1408, 1024, DW_ROWS, n + "ffn_out_dw")
    dx1u = _matmul(du, s["wu"], "nt", F32, MM_ROWS, 1024, FFN_HIDDEN, n + "ffn_in_dx_u")
    dx1g = _matmul(dg, s["wg"], "nt", F32, MM_ROWS, 1024, FFN_HIDDEN, n + "ffn_in_dx_g")
    d_wffin = jnp.concatenate([_matmul(s["x1b"], du, "tn", F32, 1024, 1408, DW_ROWS, n + "ffn_in_dw_u"),
                               _matmul(s["x1b"], dg, "tn", F32, 1024, 1408, DW_ROWS, n + "ffn_in_dw_g")], axis=1)
    dz1, dz1b, dg1, db1 = _ln_bwd([dz2, dx1u, dx1g], [ALPHA, 1.0, 1.0], s["xh1"], s["rs1"], sp["ln1_g"],
                                  n + "ln1_bwd")
    d_wout = _matmul(s["merged"], dz1b, "tn", F32, 1024, 1024, DW_ROWS, n + "out_dw")
    dgates, dpa, dpb, dya, dyb = _merge_bwd(dz1b, w["w_out"], s["ya"], s["yb"], w["w_pa"], w["w_pb"], s["rest"],
                                  n + "merge_bwd")
    d_wpa = _matmul(s["ya"], dpa, "tn", F32, 512, 1024, DW_ROWS, n + "pa_dw")
    d_wpb = _matmul(s["yb"], dpb, "tn", F32, 512, 1024, DW_ROWS, n + "pb_dw")
    qb, dob = _fox_prep_bwd(s["qkv"], s["fcol"], s["lse"], dya, s["ya"], n + "fox_prep_bwd")
    dq, rsum = _fox_bwd_dq(qb, s["ka"], s["va"], dob, s["kstart"], n + "fox_bwd_dq")
    dk, dv, csum = _fox_bwd_dkv(qb, s["ka"], s["va"], dob, s["qend"], n + "fox_bwd_dkv")
    d_fcum = jnp.pad(rsum[:, ::64] - csum[:, ::64], ((0, 0), (0, 128 - A_HEADS)))
    dfa, dbf = _fox_gate_bwd(d_fcum, s["rest"], s["bf"], n + "fox_gate_bwd")
    dhb, dlb, dng = _hgrn_bwd(dyb, s["rest"], s["ob"], s["states"], s["lb"], s["ng"], n + "hgrn_bwd")
    dproj = jnp.concatenate([dq, dk, dv, dgates, dhb, dfa], axis=1)
    wall = jnp.concatenate([s["wqkv"], s["wrest"]], axis=1)
    dxp = _matmul(dproj, wall, "nt", F32, MM_ROWS, 1024, 1920, n + "proj_dx")
    d_wall = _matmul(s["xb"], dproj, "tn", F32, 1024, 1152, DW_ROWS, n + "proj_dw")
    grads = dict(w_in=_merge_w_in_grad(d_wall), w_pa=d_wpa, w_pb=d_wpb, w_out=d_wout, w_ff_in=d_wffin,
                 w_ff_out=d_wffout, b_fgate=dbf[0, :A_HEADS], lb=dlb[0], norm_g=dng[0], ln1_g=dg1[0], ln1_b=db1[0],
                 ln2_g=dg2[0], ln2_b=db2[0])
    return [dz1, dxp], [ALPHA, 1.0], grads


def _lower_bounds(logits):
    sm = jax.nn.softmax(logits.astype(F32), axis=0)
    return jnp.cumsum(sm, axis=0) - sm[0:1]


def _local_step(x, target, wfull, small):
    lbs, lb_vjp = jax.vjp(_lower_bounds, small["hgrn_lb_logits"])
    h, hb = x, x.astype(BF16)
    saved, sps = [], []
    for l in range(DEPTH):
        sp = dict(b_fgate=small["b_fgate"][l], lb=lbs[l], norm_g=small["hgrn_norm_g"][l], ln1_g=small["ln1_g"][l],
                  ln1_b=small["ln1_b"][l], ln2_g=small["ln2_g"][l], ln2_b=small["ln2_b"][l])
        h, hb, s = _layer_fwd(h, hb, wfull[l], sp, l)
        saved.append(s)
        sps.append(sp)
    dy, lpart = _loss_head(h, target)
    dys, coefs = [dy], [1.0]
    grads = [None] * DEPTH
    for l in reversed(range(DEPTH)):
        dys, coefs, grads[l] = _layer_bwd(dys, coefs, wfull[l], sps[l], saved[l], l)
    grad_x = _axpy2(coefs[0], dys[0], coefs[1], dys[1], "grad_x")
    d_logits = lb_vjp(jnp.stack([grads[l]["lb"] for l in range(DEPTH)]))[0]
    return lpart[0, 0], grad_x, grads, d_logits


_BIG = [("w_in", "w_in", (D_MODEL, IN_TOTAL), 1), ("w_branch_a", "w_pa", (A_WIDTH, D_MODEL), 1),
        ("w_branch_b", "w_pb", (B_WIDTH, D_MODEL), 1), ("w_out", "w_out", (D_MODEL, D_MODEL), 0),
        ("w_ff_in", "w_ff_in", (D_MODEL, 2 * FFN_HIDDEN), 1), ("w_ff_out", "w_ff_out", (FFN_HIDDEN, D_MODEL), 0)]
_SMALL = [("b_fgate", A_HEADS), ("hgrn_lb_logits", B_WIDTH), ("hgrn_norm_g", HD), ("ln1_g", D_MODEL),
          ("ln1_b", D_MODEL), ("ln2_g", D_MODEL), ("ln2_b", D_MODEL)]
N_BIG = len(_BIG)
SMALL_ROWS = 80


def _by_chip(full, axis):
    if axis == 0:
        return full.reshape(N_CHIPS, full.shape[0] // N_CHIPS, full.shape[1])
    n = full.shape[1] // N_CHIPS
    return jnp.stack([full[:, q * n:(q + 1) * n] for q in range(N_CHIPS)])


def _from_chips(shards, axis):
    if axis == 0:
        return shards.reshape(N_CHIPS * shards.shape[1], shards.shape[2])
    return jnp.concatenate([shards[q] for q in range(N_CHIPS)], axis=1)


def _pack_small(per_name):
    flat = jnp.concatenate([per_name[name].reshape(-1) for name, _ in _SMALL])
    return jnp.pad(flat, (0, SMALL_ROWS * 128 - flat.shape[0])).reshape(SMALL_ROWS, 128)


def _unpack_small(slab):
    flat, out, r = slab.reshape(-1), {}, 0
    for name, n in _SMALL:
        out[name] = flat[r:r + DEPTH * n].reshape(DEPTH, n)
        r += DEPTH * n
    return out


_ANY = pl.BlockSpec(memory_space=pl.ANY)


def _place():
    return lax.axis_index("x"), lax.axis_index("y"), lax.axis_index("c")


def _other_chips(x, y):
    return [(1 - x, y), (x, 1 - y), (1 - x, 1 - y)]


def _chip_exchange(mine_of, out_refs, send_sems, recv_sems, local_sems):
    x, y, c = _place()
    q = 2 * x + y
    started = []
    for w, out_ref in enumerate(out_refs):
        local = pltpu.make_async_copy(mine_of(w, q), out_ref.at[q], local_sems.at[w])
        local.start()
        started.append(local)
    sends = []
    for k, (px, py) in enumerate(_other_chips(x, y)):
        for w, out_ref in enumerate(out_refs):
            cp = pltpu.make_async_remote_copy(src_ref=mine_of(w, 2 * px + py), dst_ref=out_ref.at[q],
                                              send_sem=send_sems.at[3 * w + k], recv_sem=recv_sems.at[3 * w + k],
                                              device_id=(px, py, c), device_id_type=MESH)
            cp.start()
            sends.append(cp)
    for k, (px, py) in enumerate(_other_chips(x, y)):
        for w, out_ref in enumerate(out_refs):
            pltpu.make_async_remote_copy(src_ref=mine_of(w, q), dst_ref=out_ref.at[2 * px + py],
                                         send_sem=send_sems.at[3 * w + k], recv_sem=recv_sems.at[3 * w + k],
                                         device_id=(px, py, c), device_id_type=MESH).wait_recv()
    for cp in sends:
        cp.wait_send()
    for local in started:
        local.wait()


def _sem_scratch(n):
    return [pltpu.SemaphoreType.DMA((3 * n,)), pltpu.SemaphoreType.DMA((3 * n,)), pltpu.SemaphoreType.DMA((n,))]


def _gather_weights(mine):
    n = len(mine)

    def body(*refs):
        in_refs, out_refs = refs[:n], refs[n:2 * n]
        send_sems, recv_sems, local_sems, pair_send, pair_recv = refs[2 * n:]
        x, y, c = _place()
        _chip_exchange(lambda w, q: in_refs[w].at[c], [o.at[c] for o in out_refs], send_sems, recv_sems, local_sems)
        sibling = (x, y, 1 - c)
        fwds = []
        for w, o in enumerate(out_refs):
            cp = pltpu.make_async_remote_copy(src_ref=o.at[c], dst_ref=o.at[c], send_sem=pair_send.at[w],
                                              recv_sem=pair_recv.at[w], device_id=sibling, device_id_type=MESH)
            cp.start()
            fwds.append(cp)
        for w, o in enumerate(out_refs):
            pltpu.make_async_remote_copy(src_ref=o.at[1 - c], dst_ref=o.at[1 - c], send_sem=pair_send.at[w],
                                         recv_sem=pair_recv.at[w], device_id=sibling, device_id_type=MESH).wait_recv()
        for cp in fwds:
            cp.wait_send()

    return pl.pallas_call(
        body, name="gather_weights", in_specs=[_ANY] * n, out_specs=[_ANY] * n,
        out_shape=[jax.ShapeDtypeStruct((DEPTH, N_CHIPS) + m.shape[1:], m.dtype) for m in mine],
        scratch_shapes=_sem_scratch(n) + [pltpu.SemaphoreType.DMA((n,)), pltpu.SemaphoreType.DMA((n,))],
    )(*mine)


def _pair_exchange(gs):
    n = len(gs)

    def body(*refs):
        g_refs, a_refs, send_sems, recv_sems = refs[:n], refs[n:2 * n], refs[2 * n], refs[2 * n + 1]
        x, y, c = _place()
        cps = []
        for w in range(n):
            cp = pltpu.make_async_remote_copy(src_ref=g_refs[w].at[1 - c], dst_ref=a_refs[w], send_sem=send_sems.at[w],
                                              recv_sem=recv_sems.at[w], device_id=(x, y, 1 - c), device_id_type=MESH)
            cp.start()
            cps.append(cp)
        for cp in cps:
            cp.wait()

    return pl.pallas_call(
        body, name="grad_pair_exchange", in_specs=[_ANY] * n, out_specs=[_ANY] * n,
        out_shape=[jax.ShapeDtypeStruct(g.shape[1:], g.dtype) for g in gs],
        scratch_shapes=[pltpu.SemaphoreType.DMA((n,)), pltpu.SemaphoreType.DMA((n,))],
    )(*gs)


def _row_block(rows):
    return math.gcd(rows, 256)


def _pair_sum(g, a, layer, name):
    _, nq, rows, cols = g.shape
    tb = _row_block(rows)

    def body(l_ref, g_ref, a_ref, o_ref):
        o_ref[...] = (g_ref[...] + a_ref[...]).astype(BF16)

    return pl.pallas_call(
        body, name=name,
        grid_spec=pltpu.PrefetchScalarGridSpec(
            num_scalar_prefetch=1, grid=(nq, rows // tb),
            in_specs=[pl.BlockSpec((None, None, tb, cols), lambda q, i, l_ref: (l_ref[0], q, i, 0)),
                      pl.BlockSpec((None, tb, cols), lambda q, i, l_ref: (q, i, 0))],
            out_specs=pl.BlockSpec((None, tb, cols), lambda q, i, l_ref: (q, i, 0))),
        out_shape=jax.ShapeDtypeStruct((nq, rows, cols), BF16),
        compiler_params=_params("parallel", "parallel"),
    )(layer.reshape(1).astype(jnp.int32), g, a)


def _shard_exchange(ps):
    n = len(ps)

    def body(*refs):
        p_refs, b_refs = refs[:n], refs[n:2 * n]
        send_sems, recv_sems, local_sems = refs[2 * n:]
        _chip_exchange(lambda w, q: p_refs[w].at[q], b_refs, send_sems, recv_sems, local_sems)

    return pl.pallas_call(
        body, name="grad_shard_exchange", in_specs=[_ANY] * n, out_specs=[_ANY] * n,
        out_shape=[jax.ShapeDtypeStruct(p.shape, p.dtype) for p in ps],
        scratch_shapes=_sem_scratch(n),
    )(*ps)


def _sum4(b, name):
    _, rows, cols = b.shape
    tb = _row_block(rows)

    def body(b_ref, o_ref):
        o_ref[...] = ((b_ref[0].astype(F32) + b_ref[1].astype(F32)) + b_ref[2].astype(F32)) + b_ref[3].astype(F32)

    return pl.pallas_call(
        body, name=name, grid=(rows // tb,),
        in_specs=[pl.BlockSpec((N_CHIPS, tb, cols), lambda i: (0, i, 0))],
        out_specs=pl.BlockSpec((tb, cols), lambda i: (i, 0)),
        out_shape=jax.ShapeDtypeStruct((rows, cols), F32),
        compiler_params=_params("parallel"),
    )(b)


def _result_exchange(gcs):
    n = len(gcs)

    def body(*refs):
        g_refs, o_refs, send_sems, recv_sems = refs[:n], refs[n:2 * n], refs[2 * n], refs[2 * n + 1]
        x, y, c = _place()
        cps = []
        for w in range(n):
            cp = pltpu.make_async_remote_copy(src_ref=g_refs[w], dst_ref=o_refs[w], send_sem=send_sems.at[w],
                                              recv_sem=recv_sems.at[w], device_id=(x, y, 1 - c), device_id_type=MESH)
            cp.start()
            cps.append(cp)
        for cp in cps:
            cp.wait()

    return pl.pallas_call(
        body, name="grad_result_exchange", in_specs=[_ANY] * n, out_specs=[_ANY] * n,
        out_shape=[jax.ShapeDtypeStruct(g.shape, g.dtype) for g in gcs],
        scratch_shapes=[pltpu.SemaphoreType.DMA((n,)), pltpu.SemaphoreType.DMA((n,))],
    )(*gcs)


def _allreduce_small(v):
    def body(v_ref, o_ref, buf, send_sems, recv_sems):
        x, y, c = _place()
        me = 4 * x + 2 * y + c
        buf[me] = v_ref[...]
        peers = []
        for k in range(1, N_DEV):
            px = 1 - x if k & 4 else x
            py = 1 - y if k & 2 else y
            pc = 1 - c if k & 1 else c
            peers.append((px, py, pc))
        sends = []
        for k, peer in enumerate(peers):
            cp = pltpu.make_async_remote_copy(src_ref=v_ref, dst_ref=buf.at[me], send_sem=send_sems.at[k],
                                              recv_sem=recv_sems.at[k], device_id=peer, device_id_type=MESH)
            cp.start()
            sends.append(cp)
        for k, (px, py, pc) in enumerate(peers):
            pltpu.make_async_remote_copy(src_ref=v_ref, dst_ref=buf.at[4 * px + 2 * py + pc], send_sem=send_sems.at[k],
                                         recv_sem=recv_sems.at[k], device_id=(px, py, pc),
                                         device_id_type=MESH).wait_recv()
        for cp in sends:
            cp.wait_send()
        acc = buf[0]
        for i in range(1, N_DEV):
            acc = acc + buf[i]
        o_ref[...] = acc

    vm = pl.BlockSpec(memory_space=pltpu.VMEM)
    return pl.pallas_call(
        body, name="small_allreduce", in_specs=[vm], out_specs=vm,
        out_shape=jax.ShapeDtypeStruct(v.shape, F32),
        scratch_shapes=[pltpu.VMEM((N_DEV,) + v.shape, F32), pltpu.SemaphoreType.DMA((N_DEV - 1,)),
                        pltpu.SemaphoreType.DMA((N_DEV - 1,))],
    )(v)


def _adam_update(w, g, m, v):
    nm = ADAM_B1 * m + (1.0 - ADAM_B1) * g
    nv = ADAM_B2 * v + (1.0 - ADAM_B2) * (g * g)
    m_hat = nm / (1.0 - ADAM_B1 ** ADAM_STEP)
    v_hat = nv / (1.0 - ADAM_B2 ** ADAM_STEP)
    return -ADAM_LR * (m_hat / (jnp.sqrt(v_hat) + ADAM_EPS) + ADAM_WD * w), nm, nv


def _adamw_small(w, g, m, v, name):
    def body(w_ref, g_ref, m_ref, v_ref, d_ref, nm_ref, nv_ref):
        d_ref[...], nm_ref[...], nv_ref[...] = _adam_update(w_ref[...], g_ref[...], m_ref[...], v_ref[...])

    vm = pl.BlockSpec(memory_space=pltpu.VMEM)
    return pl.pallas_call(
        body, name=name, in_specs=[vm] * 4, out_specs=[vm] * 3,
        out_shape=[jax.ShapeDtypeStruct(w.shape, F32)] * 3,
    )(w, g, m, v)


def _adamw_big(w, m, v, g_own, g_other, layer, name):
    _, rows, cols = w.shape
    tb = _row_block(rows)

    def body(l_ref, w_ref, m_ref, v_ref, go_ref, gx_ref, g_ref, d_ref, nm_ref, nv_ref):
        gv = jnp.where(pl.program_id(0) == l_ref[0], go_ref[...], gx_ref[...])
        g_ref[...] = gv
        d_ref[...], nm_ref[...], nv_ref[...] = _adam_update(w_ref[...], gv, m_ref[...], v_ref[...])

    per_layer = pl.BlockSpec((None, tb, cols), lambda l, i, l_ref: (l, i, 0))
    shared = pl.BlockSpec((tb, cols), lambda l, i, l_ref: (i, 0))
    return pl.pallas_call(
        body, name=name,
        grid_spec=pltpu.PrefetchScalarGridSpec(
            num_scalar_prefetch=1, grid=(DEPTH, rows // tb),
            in_specs=[per_layer, per_layer, per_layer, shared, shared], out_specs=[per_layer] * 4),
        out_shape=[jax.ShapeDtypeStruct(w.shape, F32)] * 4,
        compiler_params=_params("parallel", "parallel"),
    )(layer.reshape(1).astype(jnp.int32), w, m, v, g_own, g_other)


def kernel(x, w_in, b_fgate, hgrn_lb_logits, hgrn_norm_g, w_branch_a, w_branch_b, w_out, ln1_g, ln1_b, w_ff_in, w_ff_out, ln2_g, ln2_b, loss_target, m_w_in, m_b_fgate, m_hgrn_lb_logits, m_hgrn_norm_g, m_w_branch_a, m_w_branch_b, m_w_out, m_ln1_g, m_ln1_b, m_w_ff_in, m_w_ff_out, m_ln2_g, m_ln2_b, v_w_in, v_b_fgate, v_hgrn_lb_logits, v_hgrn_norm_g, v_w_branch_a, v_w_branch_b, v_w_out, v_ln1_g, v_ln1_b, v_w_ff_in, v_w_ff_out, v_ln2_g, v_ln2_b):
    weights = dict(w_in=w_in, b_fgate=b_fgate, hgrn_lb_logits=hgrn_lb_logits, hgrn_norm_g=hgrn_norm_g,
                   w_branch_a=w_branch_a, w_branch_b=w_branch_b, w_out=w_out, ln1_g=ln1_g, ln1_b=ln1_b,
                   w_ff_in=w_ff_in, w_ff_out=w_ff_out, ln2_g=ln2_g, ln2_b=ln2_b)
    mom1 = dict(w_in=m_w_in, b_fgate=m_b_fgate, hgrn_lb_logits=m_hgrn_lb_logits, hgrn_norm_g=m_hgrn_norm_g,
                w_branch_a=m_w_branch_a, w_branch_b=m_w_branch_b, w_out=m_w_out, ln1_g=m_ln1_g, ln1_b=m_ln1_b,
                w_ff_in=m_w_ff_in, w_ff_out=m_w_ff_out, ln2_g=m_ln2_g, ln2_b=m_ln2_b)
    mom2 = dict(w_in=v_w_in, b_fgate=v_b_fgate, hgrn_lb_logits=v_hgrn_lb_logits, hgrn_norm_g=v_hgrn_norm_g,
                w_branch_a=v_w_branch_a, w_branch_b=v_w_branch_b, w_out=v_w_out, ln1_g=v_ln1_g, ln1_b=v_ln1_b,
                w_ff_in=v_w_ff_in, w_ff_out=v_w_ff_out, ln2_g=v_ln2_g, ln2_b=v_ln2_b)
    core = lax.axis_index("c")

    gathered = _gather_weights([weights[name].astype(BF16) for name, _, _, _ in _BIG])
    wfull = [{key: _from_chips(gathered[w][l], axis) for w, (_, key, _, axis) in enumerate(_BIG)}
             for l in range(DEPTH)]
    small = {name: weights[name] for name, _ in _SMALL}

    loss_part, grad_x, grads, d_logits = _local_step(x[0], loss_target[0], wfull, small)

    g_all = [jnp.stack([_by_chip(grads[l][key], axis) for l in range(DEPTH)]) for _, key, _, axis in _BIG]
    received = _pair_exchange(g_all)
    pair = [_pair_sum(g_all[w], received[w], core, f"grad_pair_sum_{w}") for w in range(N_BIG)]
    by_chip = _shard_exchange(pair)
    g_layer = [_sum4(by_chip[w], f"grad_chip_sum_{w}") for w in range(N_BIG)]
    g_other = _result_exchange(g_layer)
    out_g, out_d, out_m, out_v = {}, {}, {}, {}
    for w, (name, _, _, _) in enumerate(_BIG):
        out_g[name], out_d[name], out_m[name], out_v[name] = _adamw_big(
            weights[name], mom1[name], mom2[name], g_layer[w], g_other[w], core, f"adamw_{name}")

    small_grads = {name: jnp.stack([grads[l][key] for l in range(DEPTH)])
                   for name, key in [("b_fgate", "b_fgate"), ("hgrn_norm_g", "norm_g"), ("ln1_g", "ln1_g"),
                                     ("ln1_b", "ln1_b"), ("ln2_g", "ln2_g"), ("ln2_b", "ln2_b")]}
    small_grads["hgrn_lb_logits"] = d_logits
    gs = _allreduce_small(_pack_small(small_grads))
    ds, ms, vs = _adamw_small(_pack_small(small), gs, _pack_small({n: mom1[n] for n, _ in _SMALL}),
                              _pack_small({n: mom2[n] for n, _ in _SMALL}), "adamw_small")
    for tree, slab in ((out_g, gs), (out_d, ds), (out_m, ms), (out_v, vs)):
        tree.update(_unpack_small(slab))

    loss = lax.psum(loss_part, ("x", "y", "c"))
    order = ["w_in", "b_fgate", "hgrn_lb_logits", "hgrn_norm_g", "w_branch_a", "w_branch_b", "w_out", "ln1_g", "ln1_b",
             "w_ff_in", "w_ff_out", "ln2_g", "ln2_b"]
    return (loss, grad_x[None], *[out_g[n] for n in order], *[out_d[n] for n in order],
            *[out_m[n] for n in order], *[out_v[n] for n in order])
```

```python
import functools
import math

import jax
import jax.numpy as jnp
import numpy as np
from jax import lax
from jax.experimental import pallas as pl
from jax.experimental.pallas import tpu as pltpu

F32 = jnp.float32
BF16 = jnp.bfloat16

D_MODEL = 1024
DEPTH = 2
A_HEADS = 8
A_WIDTH = 512
B_WIDTH = 512
B_HEADS = 4
HD = 128
CHUNK = 64
SUB = 16
FFN_HIDDEN = 2816
IN_TOTAL = 5640
ALPHA = (2 * DEPTH) ** 0.25
LN_EPS = 1e-5
RMS_EPS = 1e-6
ADAM_LR = 0.001
ADAM_B1 = 0.9
ADAM_B2 = 0.999
ADAM_EPS = 1e-08
ADAM_WD = 0.01
ADAM_STEP = 10
EXP_CLAMP = 60.0

VMEM_LIMIT_BYTES = 56 * 1024 * 1024
MM_ROWS = 1024
DW_ROWS = 2048
N_CHIPS = 4
N_DEV = 8
MESH = pl.DeviceIdType.MESH

_DN = {
    "nn": (((1,), (0,)), ((), ())),
    "nt": (((1,), (1,)), ((), ())),
    "tn": (((0,), (0,)), ((), ())),
}


def _dot(a, b, mode="nn"):
    return lax.dot_general(a.astype(BF16), b.astype(BF16), _DN[mode], preferred_element_type=F32)


def _pieces(x):
    h = x.astype(BF16)
    r = x - h.astype(F32)
    m = r.astype(BF16)
    return h, m, (r - m.astype(F32)).astype(BF16)


def _dot_hi(a, b, mode="nn", exact="a"):
    if exact == "a":
        h, m, l = _pieces(b)
        return (_dot(a, l, mode) + _dot(a, m, mode)) + _dot(a, h, mode)
    h, m, l = _pieces(a)
    return (_dot(l, b, mode) + _dot(m, b, mode)) + _dot(h, b, mode)


def _hdot(a, b, mode="nn"):
    ah, al, _ = _pieces(a)
    bh, bl, _ = _pieces(b)
    return (_dot(al, bh, mode) + _dot(ah, bl, mode)) + _dot(ah, bh, mode)


def _params(*sem):
    return pltpu.CompilerParams(dimension_semantics=sem, vmem_limit_bytes=VMEM_LIMIT_BYTES)


def _sigmoid(x):
    return 1.0 / (1.0 + jnp.exp(-x))


def _matmul(a, b, mode, out_dtype, tm, tn, tk, name):
    if mode == "nn":
        (m, k), (k2, n) = a.shape, b.shape
    elif mode == "nt":
        (m, k), (n, k2) = a.shape, b.shape
    else:
        (k, m), (k2, n) = a.shape, b.shape
    assert k == k2, (a.shape, b.shape, mode)
    tm, tn, tk = min(tm, m), min(tn, n), min(tk, k)
    assert m % tm == 0 and n % tn == 0 and k % tk == 0, (a.shape, b.shape, tm, tn, tk)
    nk = k // tk
    if mode == "tn":
        a_spec = pl.BlockSpec((tk, tm), lambda j, i, kk: (kk, i))
    else:
        a_spec = pl.BlockSpec((tm, tk), lambda j, i, kk: (i, kk))
    if mode == "nt":
        b_spec = pl.BlockSpec((tn, tk), lambda j, i, kk: (j, kk))
    else:
        b_spec = pl.BlockSpec((tk, tn), lambda j, i, kk: (kk, j))
    use_acc = nk > 1 and out_dtype != F32

    def body(a_ref, b_ref, o_ref, *scratch):
        p = _dot(a_ref[...], b_ref[...], mode)
        if nk == 1:
            o_ref[...] = p.astype(out_dtype)
            return
        acc_ref = scratch[0] if use_acc else o_ref
        kk = pl.program_id(2)

        @pl.when(kk == 0)
        def _():
            acc_ref[...] = p

        @pl.when(kk > 0)
        def _():
            acc_ref[...] += p

        if use_acc:
            @pl.when(kk == nk - 1)
            def _():
                o_ref[...] = acc_ref[...].astype(out_dtype)

    return pl.pallas_call(
        body,
        name=name,
        grid=(n // tn, m // tm, nk),
        in_specs=[a_spec, b_spec],
        out_specs=pl.BlockSpec((tm, tn), lambda j, i, kk: (i, j)),
        out_shape=jax.ShapeDtypeStruct((m, n), out_dtype),
        scratch_shapes=[pltpu.VMEM((tm, tn), F32)] if use_acc else [],
        compiler_params=_params("parallel", "parallel", "arbitrary"),
    )(a, b)


def _mm_res_ln(a, w, res, g, b, name, tm=512):
    t, k = a.shape
    d = w.shape[1]
    tm = min(tm, t)

    def body(a_ref, w_ref, r_ref, g_ref, b_ref, y_ref, yb_ref, xh_ref, rs_ref):
        z = ALPHA * r_ref[...] + _dot(a_ref[...], w_ref[...])
        mu = jnp.mean(z, axis=-1, keepdims=True)
        zc = z - mu
        var = jnp.mean(zc * zc, axis=-1, keepdims=True)
        rstd = lax.rsqrt(var + LN_EPS)
        xh = zc * rstd
        y = xh * g_ref[...] + b_ref[...]
        y_ref[...] = y
        yb_ref[...] = y.astype(BF16)
        xh_ref[...] = xh
        rs_ref[...] = rstd

    row = lambda i: (i, 0)
    fix = lambda i: (0, 0)
    return pl.pallas_call(
        body,
        name=name,
        grid=(t // tm,),
        in_specs=[pl.BlockSpec((tm, k), row), pl.BlockSpec((k, d), fix), pl.BlockSpec((tm, d), row),
                  pl.BlockSpec((1, d), fix), pl.BlockSpec((1, d), fix)],
        out_specs=[pl.BlockSpec((tm, d), row), pl.BlockSpec((tm, d), row), pl.BlockSpec((tm, d), row),
                   pl.BlockSpec((tm, 1), row)],
        out_shape=[jax.ShapeDtypeStruct((t, d), F32), jax.ShapeDtypeStruct((t, d), BF16),
                   jax.ShapeDtypeStruct((t, d), F32), jax.ShapeDtypeStruct((t, 1), F32)],
        compiler_params=_params("parallel"),
    )(a, w, res, g.reshape(1, d), b.reshape(1, d))


def _ln_bwd(dys, coefs, xhat, rstd, g, name, tm=512):
    t, d = xhat.shape
    tm = min(tm, t)
    n_in = len(dys)

    def body(*refs):
        dy_refs = refs[:n_in]
        xh_ref, rs_ref, g_ref, dz_ref, dzb_ref, dg_ref, db_ref = refs[n_in:]
        dy = coefs[0] * dy_refs[0][...].astype(F32)
        for c, r in zip(coefs[1:], dy_refs[1:]):
            dy = dy + c * r[...].astype(F32)
        xh = xh_ref[...]
        dxh = dy * g_ref[...]
        m1 = jnp.mean(dxh, axis=-1, keepdims=True)
        m2 = jnp.mean(dxh * xh, axis=-1, keepdims=True)
        dz = rs_ref[...] * (dxh - m1 - xh * m2)
        dz_ref[...] = dz
        dzb_ref[...] = dz.astype(BF16)
        pg = jnp.sum(dy * xh, axis=0, keepdims=True)
        pb = jnp.sum(dy, axis=0, keepdims=True)

        @pl.when(pl.program_id(0) == 0)
        def _():
            dg_ref[...] = pg
            db_ref[...] = pb

        @pl.when(pl.program_id(0) > 0)
        def _():
            dg_ref[...] += pg
            db_ref[...] += pb

    row = lambda i: (i, 0)
    fix = lambda i: (0, 0)
    return pl.pallas_call(
        body,
        name=name,
        grid=(t // tm,),
        in_specs=[pl.BlockSpec((tm, d), row)] * n_in
        + [pl.BlockSpec((tm, d), row), pl.BlockSpec((tm, 1), row), pl.BlockSpec((1, d), fix)],
        out_specs=[pl.BlockSpec((tm, d), row), pl.BlockSpec((tm, d), row), pl.BlockSpec((1, d), fix),
                   pl.BlockSpec((1, d), fix)],
        out_shape=[jax.ShapeDtypeStruct((t, d), F32), jax.ShapeDtypeStruct((t, d), BF16),
                   jax.ShapeDtypeStruct((1, d), F32), jax.ShapeDtypeStruct((1, d), F32)],
        compiler_params=_params("arbitrary"),
    )(*dys, xhat, rstd, g.reshape(1, d))


def _loss_head(y, target, name="loss_head", tm=512):
    t, d = y.shape
    tm = min(tm, t)

    def body(y_ref, t_ref, dy_ref, l_ref):
        e = y_ref[...] - t_ref[...]
        dy_ref[...] = e * (1.0 / d)
        part = jnp.full((8, 128), 0.5 / d, F32) * jnp.sum(e * e)

        @pl.when(pl.program_id(0) == 0)
        def _():
            l_ref[...] = part

        @pl.when(pl.program_id(0) > 0)
        def _():
            l_ref[...] += part

    row = lambda i: (i, 0)
    return pl.pallas_call(
        body,
        name=name,
        grid=(t // tm,),
        in_specs=[pl.BlockSpec((tm, d), row), pl.BlockSpec((tm, d), row)],
        out_specs=[pl.BlockSpec((tm, d), row), pl.BlockSpec((8, 128), lambda i: (0, 0))],
        out_shape=[jax.ShapeDtypeStruct((t, d), F32), jax.ShapeDtypeStruct((8, 128), F32)],
        compiler_params=_params("arbitrary"),
    )(y, target)


FFN_COLS = FFN_HIDDEN // 2


def _ffn_in_swiglu(xb, wu, wg, name, tm=512):
    t, d = xb.shape
    tm = min(tm, t)

    def body(x_ref, wu_ref, wg_ref, a_ref, u_ref, g_ref):
        x = x_ref[...]
        u = _dot(x, wu_ref[...])
        g = _dot(x, wg_ref[...])
        u_ref[...] = u
        g_ref[...] = g
        a_ref[...] = (g * _sigmoid(g) * u).astype(BF16)

    wspec = pl.BlockSpec((d, FFN_COLS), lambda j, i: (0, j))
    out = pl.BlockSpec((tm, FFN_COLS), lambda j, i: (i, j))
    return pl.pallas_call(
        body,
        name=name,
        grid=(FFN_HIDDEN // FFN_COLS, t // tm),
        in_specs=[pl.BlockSpec((tm, d), lambda j, i: (i, 0)), wspec, wspec],
        out_specs=[out, out, out],
        out_shape=[jax.ShapeDtypeStruct((t, FFN_HIDDEN), BF16), jax.ShapeDtypeStruct((t, FFN_HIDDEN), F32),
                   jax.ShapeDtypeStruct((t, FFN_HIDDEN), F32)],
        compiler_params=_params("parallel", "parallel"),
    )(xb, wu, wg)


def _ffn_out_dx_swiglu(dzb, w_ff_out, u, g, name, tm=512):
    t, d = dzb.shape
    tm = min(tm, t)

    def body(dz_ref, w_ref, u_ref, g_ref, du_ref, dg_ref):
        da = _dot(dz_ref[...], w_ref[...], "nt")
        gv = g_ref[...]
        sg = _sigmoid(gv)
        du_ref[...] = (da * gv * sg).astype(BF16)
        dg_ref[...] = (da * u_ref[...] * (sg * (1.0 + gv * (1.0 - sg)))).astype(BF16)

    blk = pl.BlockSpec((tm, FFN_COLS), lambda j, i: (i, j))
    return pl.pallas_call(
        body,
        name=name,
        grid=(FFN_HIDDEN // FFN_COLS, t // tm),
        in_specs=[pl.BlockSpec((tm, d), lambda j, i: (i, 0)), pl.BlockSpec((FFN_COLS, d), lambda j, i: (j, 0)), blk, blk],
        out_specs=[blk, blk],
        out_shape=[jax.ShapeDtypeStruct((t, FFN_HIDDEN), BF16)] * 2,
        compiler_params=_params("parallel", "parallel"),
    )(dzb, w_ff_out, u, g)


def _merge_fwd(ya, yb, wpa, wpb, rest, name, tm=512):
    t = ya.shape[0]
    tm = min(tm, t)

    def body(ya_ref, yb_ref, wa_ref, wb_ref, ga_ref, gb_ref, o_ref):
        pa = _dot(ya_ref[...], wa_ref[...])
        pb = _dot(yb_ref[...], wb_ref[...])
        o_ref[...] = (_sigmoid(ga_ref[...]) * pa + _sigmoid(gb_ref[...]) * pb).astype(BF16)

    row = lambda i: (i, 0)
    fix = lambda i: (0, 0)
    return pl.pallas_call(
        body,
        name=name,
        grid=(t // tm,),
        in_specs=[pl.BlockSpec((tm, A_WIDTH), row), pl.BlockSpec((tm, B_WIDTH), row),
                  pl.BlockSpec((A_WIDTH, D_MODEL), fix), pl.BlockSpec((B_WIDTH, D_MODEL), fix),
                  pl.BlockSpec((tm, D_MODEL), lambda i: (i, 0)), pl.BlockSpec((tm, D_MODEL), lambda i: (i, 1))],
        out_specs=pl.BlockSpec((tm, D_MODEL), row),
        out_shape=jax.ShapeDtypeStruct((t, D_MODEL), BF16),
        compiler_params=_params("parallel"),
    )(ya, yb, wpa, wpb, rest, rest)


def _merge_bwd(dzb, w_out, ya, yb, wpa, wpb, rest, name, tm=512):
    t = ya.shape[0]
    tm = min(tm, t)

    def body(dz_ref, wo_ref, ya_ref, yb_ref, wa_ref, wb_ref, ga_ref, gb_ref, dg_ref, dpa_ref, dpb_ref, dya_ref,
             dyb_ref):
        dm_v = _dot(dz_ref[...], wo_ref[...], "nt")
        pa = _dot(ya_ref[...], wa_ref[...])
        pb = _dot(yb_ref[...], wb_ref[...])
        sa = _sigmoid(ga_ref[...])
        sb = _sigmoid(gb_ref[...])
        dg_ref[:, :D_MODEL] = (dm_v * pa * sa * (1.0 - sa)).astype(BF16)
        dg_ref[:, D_MODEL:] = (dm_v * pb * sb * (1.0 - sb)).astype(BF16)
        dpa = (dm_v * sa).astype(BF16)
        dpb = (dm_v * sb).astype(BF16)
        dpa_ref[...] = dpa
        dpb_ref[...] = dpb
        dya_ref[...] = _dot(dpa, wa_ref[...], "nt").astype(BF16)
        dyb_ref[...] = _dot(dpb, wb_ref[...], "nt")

    row = lambda i: (i, 0)
    fix = lambda i: (0, 0)
    return pl.pallas_call(
        body,
        name=name,
        grid=(t // tm,),
        in_specs=[pl.BlockSpec((tm, D_MODEL), row), pl.BlockSpec((D_MODEL, D_MODEL), fix),
                  pl.BlockSpec((tm, A_WIDTH), row), pl.BlockSpec((tm, B_WIDTH), row),
                  pl.BlockSpec((A_WIDTH, D_MODEL), fix), pl.BlockSpec((B_WIDTH, D_MODEL), fix),
                  pl.BlockSpec((tm, D_MODEL), lambda i: (i, 0)), pl.BlockSpec((tm, D_MODEL), lambda i: (i, 1))],
        out_specs=[pl.BlockSpec((tm, 2 * D_MODEL), row), pl.BlockSpec((tm, D_MODEL), row),
                   pl.BlockSpec((tm, D_MODEL), row), pl.BlockSpec((tm, A_WIDTH), row), pl.BlockSpec((tm, B_WIDTH), row)],
        out_shape=[jax.ShapeDtypeStruct((t, 2 * D_MODEL), BF16), jax.ShapeDtypeStruct((t, D_MODEL), BF16),
                   jax.ShapeDtypeStruct((t, D_MODEL), BF16), jax.ShapeDtypeStruct((t, A_WIDTH), BF16),
                   jax.ShapeDtypeStruct((t, B_WIDTH), F32)],
        compiler_params=_params("parallel"),
    )(dzb, w_out, ya, yb, wpa, wpb, rest, rest)


FA_BLOCK = 4224 // 128 - 1


def _tri(n, lower):
    r = lax.broadcasted_iota(jnp.int32, (n, n), 0)
    c = lax.broadcasted_iota(jnp.int32, (n, n), 1)
    return jnp.where((r >= c) if lower else (r <= c), 1.0, 0.0).astype(F32)


def _head_spread(expand):
    shape = (128, A_WIDTH) if expand else (A_WIDTH, 128)
    r = lax.broadcasted_iota(jnp.int32, shape, 0)
    c = lax.broadcasted_iota(jnp.int32, shape, 1)
    hit = ((c >= 64 * r) & (c < 64 * r + 64)) if expand else (r == 64 * c)
    return jnp.where(hit, 1.0, 0.0).astype(F32)


def _fox_gate_fwd(rest, bf, name, tb=512):
    t = rest.shape[0]
    tb = min(tb, t)

    def body(fa_ref, bf_ref, f_ref, fc_ref, carry):
        @pl.when(pl.program_id(0) == 0)
        def _():
            carry[...] = jnp.zeros_like(carry)

        z = fa_ref[...] + bf_ref[...]
        logf = jnp.minimum(z, 0.0) - jnp.log(1.0 + jnp.exp(-jnp.abs(z)))
        f = _dot_hi(_tri(tb, True), logf) + carry[...]
        f_ref[...] = f
        fc_ref[...] = _dot_hi(f, _head_spread(True), exact="b")
        carry[...] = f[tb - 1:tb, :]

    return pl.pallas_call(
        body,
        name=name,
        grid=(t // tb,),
        in_specs=[pl.BlockSpec((tb, 128), lambda i: (i, FA_BLOCK)), pl.BlockSpec((1, 128), lambda i: (0, 0))],
        out_specs=[pl.BlockSpec((tb, 128), lambda i: (i, 0)), pl.BlockSpec((tb, A_WIDTH), lambda i: (i, 0))],
        out_shape=[jax.ShapeDtypeStruct((t, 128), F32), jax.ShapeDtypeStruct((t, A_WIDTH), F32)],
        scratch_shapes=[pltpu.VMEM((1, 128), F32)],
        compiler_params=_params("arbitrary"),
    )(rest, bf)


def _fox_gate_bwd(rsum, csum, rest, bf, name, tb=512):
    t = rest.shape[0]
    tb = min(tb, t)
    nb = t // tb

    def body(rs_ref, cs_ref, fa_ref, bf_ref, dfa_ref, dbf_ref, carry):
        @pl.when(pl.program_id(0) == 0)
        def _():
            carry[...] = jnp.zeros_like(carry)

        d_f = _dot_hi(rs_ref[...] - cs_ref[...], _head_spread(False), exact="b")
        dlogf = _dot_hi(_tri(tb, False), d_f) + carry[...]
        carry[...] = dlogf[0:1, :]
        z = fa_ref[...] + bf_ref[...]
        dz = dlogf * _sigmoid(-z)
        dfa_ref[...] = dz.astype(BF16)
        part = jnp.sum(dz, axis=0, keepdims=True)

        @pl.when(pl.program_id(0) == 0)
        def _():
            dbf_ref[...] = part

        @pl.when(pl.program_id(0) > 0)
        def _():
            dbf_ref[...] += part

    return pl.pallas_call(
        body,
        name=name,
        grid=(nb,),
        in_specs=[pl.BlockSpec((tb, A_WIDTH), lambda i: (nb - 1 - i, 0)),
                  pl.BlockSpec((tb, A_WIDTH), lambda i: (nb - 1 - i, 0)),
                  pl.BlockSpec((tb, 128), lambda i: (nb - 1 - i, FA_BLOCK)),
                  pl.BlockSpec((1, 128), lambda i: (0, 0))],
        out_specs=[pl.BlockSpec((tb, 128), lambda i: (nb - 1 - i, 0)), pl.BlockSpec((1, 128), lambda i: (0, 0))],
        out_shape=[jax.ShapeDtypeStruct((t, 128), BF16), jax.ShapeDtypeStruct((1, 128), F32)],
        scratch_shapes=[pltpu.VMEM((1, 128), F32)],
        compiler_params=_params("arbitrary"),
    )(rsum, csum, rest, bf)


ATT_BLOCK = 512


def _head_mask(shape, j):
    lane = lax.broadcasted_iota(jnp.int32, shape, 1)
    return (lane < 64) if j == 0 else (lane >= 64)


def _split3(x):
    return tuple(p.astype(F32) for p in _pieces(x))


def _aug_lanes(tb, j):
    lane = lax.broadcasted_iota(jnp.int32, (tb, 128), 1)
    own = (lane < 64) if j == 0 else (lane >= 64)
    return own, lane - 64 * (1 - j)


def _aug_query(own, li, q, bias):
    h, m, l = _split3(bias)
    spare = jnp.where(li == 0, h, jnp.where(li == 1, m, jnp.where(li == 2, l, jnp.where(li < 6, 1.0, 0.0))))
    return jnp.where(own, q, spare).astype(BF16)


def _fox_prep_fwd(qkv, fcol, name, tb=512):
    t = qkv.shape[0]
    tb = min(tb, t)

    def body(q_ref, k_ref, v_ref, fc_ref, qa_ref, ka_ref, va_ref, qn_ref, kn_ref):
        fsw = pltpu.roll(fc_ref[...], 64, 1)
        q = q_ref[...].astype(F32)
        k = k_ref[...].astype(F32)
        v = v_ref[...].astype(F32)
        h, m, l = _split3(fsw)
        first = _head_mask((tb, 128), 0)
        for nrm_ref, x in ((qn_ref, q), (kn_ref, k)):
            n0 = jnp.max(jnp.sum(jnp.where(first, x * x, 0.0), axis=1, keepdims=True))
            n1 = jnp.max(jnp.sum(jnp.where(first, 0.0, x * x), axis=1, keepdims=True))
            nrm_ref[...] = jnp.where(_head_mask((8, 128), 0), n0, n1)
        for j in (0, 1):
            own, li = _aug_lanes(tb, j)
            cols = slice(128 * j, 128 * (j + 1))
            qa_ref[:, cols] = _aug_query(own, li, q * 0.125, fsw)
            ks = jnp.where(li < 3, 1.0, jnp.where(li == 3, -h, jnp.where(li == 4, -m, jnp.where(li == 5, -l, 0.0))))
            ka_ref[:, cols] = jnp.where(own, k, ks).astype(BF16)
            va_ref[:, cols] = jnp.where(own, v, 1.0).astype(BF16)

    blk = pl.BlockSpec((tb, 256), lambda i, h: (i, h))
    nrm = pl.BlockSpec((None, None, 8, 128), lambda i, h: (i, h, 0, 0))
    return pl.pallas_call(
        body, name=name, grid=(t // tb, 4),
        in_specs=[pl.BlockSpec((tb, 128), lambda i, h: (i, h)), pl.BlockSpec((tb, 128), lambda i, h: (i, 4 + h)),
                  pl.BlockSpec((tb, 128), lambda i, h: (i, 8 + h)), pl.BlockSpec((tb, 128), lambda i, h: (i, h))],
        out_specs=[blk, blk, blk, nrm, nrm],
        out_shape=[jax.ShapeDtypeStruct((t, 2 * A_WIDTH), BF16)] * 3
        + [jax.ShapeDtypeStruct((t // tb, 4, 8, 128), F32)] * 2,
        compiler_params=_params("parallel", "parallel"),
    )(qkv, qkv, qkv, fcol)


def _fox_prep_bwd(qkv, fcol, lse, do, o, name, tb=512):
    t = qkv.shape[0]
    tb = min(tb, t)

    def body(q_ref, fc_ref, lse_ref, do_ref, o_ref, qb_ref, dob_ref):
        gsw = pltpu.roll(fc_ref[...] - lse_ref[...], 64, 1)
        q = q_ref[...].astype(F32) * 0.125
        do_v = do_ref[...].astype(F32)
        prod = do_v * o_ref[...].astype(F32)
        for j in (0, 1):
            own, li = _aug_lanes(tb, j)
            cols = slice(128 * j, 128 * (j + 1))
            qb_ref[:, cols] = _aug_query(own, li, q, gsw)
            delta = jnp.sum(jnp.where(own, prod, 0.0), axis=1, keepdims=True)
            h, m, l = _split3(jnp.broadcast_to(delta, (tb, 128)))
            ds = jnp.where(li == 0, -h, jnp.where(li == 1, -m, jnp.where(li == 2, -l, 0.0)))
            dob_ref[:, cols] = jnp.where(own, do_v, ds).astype(BF16)

    pair = pl.BlockSpec((tb, 128), lambda i, h: (i, h))
    blk = pl.BlockSpec((tb, 256), lambda i, h: (i, h))
    return pl.pallas_call(
        body, name=name, grid=(t // tb, 4),
        in_specs=[pair, pair, pair, pair, pair],
        out_specs=[blk, blk],
        out_shape=[jax.ShapeDtypeStruct((t, 2 * A_WIDTH), BF16)] * 2,
        compiler_params=_params("parallel", "parallel"),
    )(qkv, fcol, lse, do, o)


def _tile_mask(n, transposed):
    r = lax.broadcasted_iota(jnp.int32, (n, n), 0)
    c = lax.broadcasted_iota(jnp.int32, (n, n), 1)
    return (c >= r) if transposed else (r >= c)


UNDERFLOW = -110.0


def _fox_block_ranges(qn, kn, fcum):
    t = fcum.shape[0]
    blk = min(ATT_BLOCK, t)
    nb = t // blk
    q2 = jnp.max(qn[:, :, 0, ::64].reshape(-1, A_HEADS), axis=0)
    k2 = jnp.max(kn[:, :, 0, ::64].reshape(-1, A_HEADS), axis=0)
    bound = 2.0 * jnp.sqrt(q2 * k2) * 0.125
    f = fcum[:, :A_HEADS]
    first = f[0::blk].T
    last = f[blk - 1::blk].T
    dead = (bound[:, None, None] + first[:, :, None] - last[:, None, :]) < UNDERFLOW
    qi = jnp.arange(nb)[None, :, None]
    kj = jnp.arange(nb)[None, None, :]
    dead = dead & (kj < qi)
    kstart = jnp.sum(dead, axis=2).astype(jnp.int32)
    qend = (kj[0] + jnp.sum((~dead) & (qi > kj), axis=1)).astype(jnp.int32)
    return kstart.reshape(-1), qend.reshape(-1)


def _fox_fwd(qa, ka, va, kstart, name):
    t = qa.shape[0]
    bq = min(ATT_BLOCK, t)
    nq = t // bq

    def body(ks_ref, q_ref, k_ref, v_ref, o_ref, lse_ref):
        i = pl.program_id(1)
        hp = pl.program_id(0)
        k0 = jnp.minimum(ks_ref[2 * hp * nq + i], ks_ref[(2 * hp + 1) * nq + i])

        def step(kb, carry, masked):
            rows = pl.ds(pl.multiple_of(kb * bq, bq), bq)
            new = []
            for j in (0, 1):
                cols = slice(128 * j, 128 * (j + 1))
                m, acc = carry[2 * j], carry[2 * j + 1]
                s = _dot(q_ref[:, cols], k_ref[rows, cols], "nt")
                if masked:
                    s = jnp.where(_tile_mask(bq, False), s, -jnp.inf)
                m_new = jnp.maximum(m, jnp.max(s, axis=1, keepdims=True))
                new += [m_new, jnp.exp(m - m_new) * acc + _dot(jnp.exp(s - m_new), v_ref[rows, cols])]
            return tuple(new)

        carry = (jnp.full((bq, 1), -jnp.inf, F32), jnp.zeros((bq, 128), F32)) * 2
        carry = lax.fori_loop(k0, i, lambda kb, c: step(kb, c, False), carry)
        carry = step(i, carry, True)
        outs = []
        for j in (0, 1):
            m, acc = carry[2 * j], carry[2 * j + 1]
            spare = 64 * (1 - j)
            l = acc[:, spare:spare + 1]
            outs.append((acc / l, m + jnp.log(l)))
        msk = _head_mask((bq, 128), 0)
        o_ref[...] = jnp.where(msk, outs[0][0], outs[1][0]).astype(BF16)
        lse_ref[...] = jnp.where(msk, outs[0][1], outs[1][1])

    res = pl.BlockSpec((t, 256), lambda h, i, tbl: (0, h))
    out = pl.BlockSpec((bq, 128), lambda h, i, tbl: (i, h))
    return pl.pallas_call(
        body,
        name=name,
        grid_spec=pltpu.PrefetchScalarGridSpec(
            num_scalar_prefetch=1, grid=(4, nq),
            in_specs=[pl.BlockSpec((bq, 256), lambda h, i, tbl: (i, h)), res, res],
            out_specs=[out, out]),
        out_shape=[jax.ShapeDtypeStruct((t, A_WIDTH), BF16), jax.ShapeDtypeStruct((t, A_WIDTH), F32)],
        compiler_params=_params("parallel", "parallel"),
    )(kstart, qa, ka, va)


def _fox_bwd_dq(qb, ka, va, dob, kstart, name):
    t = qb.shape[0]
    bq = min(ATT_BLOCK, t)
    nq = t // bq

    def body(ks_ref, q_ref, k_ref, v_ref, do_ref, dq_ref, rs_ref):
        i = pl.program_id(1)
        hp = pl.program_id(0)
        k0 = jnp.minimum(ks_ref[2 * hp * nq + i], ks_ref[(2 * hp + 1) * nq + i])

        def step(kb, accs, masked):
            rows = pl.ds(pl.multiple_of(kb * bq, bq), bq)
            new = []
            for j in (0, 1):
                cols = slice(128 * j, 128 * (j + 1))
                ks = k_ref[rows, cols]
                s = _dot(q_ref[:, cols], ks, "nt")
                if masked:
                    s = jnp.where(_tile_mask(bq, False), s, -jnp.inf)
                ds = jnp.exp(s) * _dot(do_ref[:, cols], v_ref[rows, cols], "nt")
                new.append(accs[j] + _dot(ds, ks))
            return tuple(new)

        accs = lax.fori_loop(k0, i, lambda kb, c: step(kb, c, False), (jnp.zeros((bq, 128), F32),) * 2)
        accs = step(i, accs, True)
        outs = []
        for j in (0, 1):
            spare = 64 * (1 - j)
            outs.append((accs[j] * 0.125, accs[j][:, spare:spare + 1]))
        msk = _head_mask((bq, 128), 0)
        dq_ref[...] = jnp.where(msk, outs[0][0], outs[1][0]).astype(BF16)
        rs_ref[...] = jnp.where(msk, outs[0][1], outs[1][1])

    blk = pl.BlockSpec((bq, 256), lambda h, i, tbl: (i, h))
    res = pl.BlockSpec((t, 256), lambda h, i, tbl: (0, h))
    out = pl.BlockSpec((bq, 128), lambda h, i, tbl: (i, h))
    return pl.pallas_call(
        body,
        name=name,
        grid_spec=pltpu.PrefetchScalarGridSpec(
            num_scalar_prefetch=1, grid=(4, nq), in_specs=[blk, res, res, blk], out_specs=[out, out]),
        out_shape=[jax.ShapeDtypeStruct((t, A_WIDTH), BF16), jax.ShapeDtypeStruct((t, A_WIDTH), F32)],
        compiler_params=_params("parallel", "parallel"),
    )(kstart, qb, ka, va, dob)


def _fox_bwd_dkv(qb, ka, va, dob, qend, name):
    t = qb.shape[0]
    bk = min(ATT_BLOCK, t)
    nk = t // bk

    def body(qe_ref, k_ref, v_ref, q_ref, do_ref, dk_ref, dv_ref, cs_ref):
        jb = pl.program_id(1)
        hp = pl.program_id(0)
        i1 = jnp.maximum(qe_ref[2 * hp * nk + jb], qe_ref[(2 * hp + 1) * nk + jb]) + 1

        def step(ib, carry, masked):
            rows = pl.ds(pl.multiple_of(ib * bk, bk), bk)
            new = []
            for j in (0, 1):
                cols = slice(128 * j, 128 * (j + 1))
                qs = q_ref[rows, cols]
                dos = do_ref[rows, cols]
                st = _dot(k_ref[:, cols], qs, "nt")
                if masked:
                    st = jnp.where(_tile_mask(bk, True), st, -jnp.inf)
                pt = jnp.exp(st)
                new += [carry[2 * j] + _dot(pt * _dot(v_ref[:, cols], dos, "nt"), qs), carry[2 * j + 1] + _dot(pt, dos)]
            return tuple(new)

        carry = step(jb, (jnp.zeros((bk, 128), F32),) * 4, True)
        carry = lax.fori_loop(jb + 1, i1, lambda ib, c: step(ib, c, False), carry)
        outs = []
        for j in (0, 1):
            spare = 64 * (1 - j)
            dk_acc, dv_acc = carry[2 * j], carry[2 * j + 1]
            outs.append((dk_acc, dv_acc, dk_acc[:, spare + 3:spare + 4]))
        msk = _head_mask((bk, 128), 0)
        dk_ref[...] = jnp.where(msk, outs[0][0], outs[1][0]).astype(BF16)
        dv_ref[...] = jnp.where(msk, outs[0][1], outs[1][1]).astype(BF16)
        cs_ref[...] = jnp.where(msk, outs[0][2], outs[1][2])

    blk = pl.BlockSpec((bk, 256), lambda h, i, tbl: (i, h))
    res = pl.BlockSpec((t, 256), lambda h, i, tbl: (0, h))
    out = pl.BlockSpec((bk, 128), lambda h, i, tbl: (i, h))
    return pl.pallas_call(
        body,
        name=name,
        grid_spec=pltpu.PrefetchScalarGridSpec(
            num_scalar_prefetch=1, grid=(4, nk), in_specs=[blk, blk, res, res], out_specs=[out, out, out]),
        out_shape=[jax.ShapeDtypeStruct((t, A_WIDTH), BF16), jax.ShapeDtypeStruct((t, A_WIDTH), BF16),
                   jax.ShapeDtypeStruct((t, A_WIDTH), F32)],
        compiler_params=_params("parallel", "parallel"),
    )(qend, ka, va, qb, dob)


HG_ROWS = 256


def _hg_gates(hb_ref, rows, h, lbh):
    qb = hb_ref[rows, h * HD:(h + 1) * HD]
    fb = hb_ref[rows, B_WIDTH + h * HD:B_WIDTH + (h + 1) * HD]
    v = hb_ref[rows, 2 * B_WIDTH + h * HD:2 * B_WIDTH + (h + 1) * HD]
    gb = hb_ref[rows, 3 * B_WIDTH + h * HD:3 * B_WIDTH + (h + 1) * HD]
    sg = _sigmoid(fb)
    f = lbh + (1.0 - lbh) * sg
    sq = _sigmoid(qb)
    return qb, sq, qb * sq, sg, f, 1.0 - f, jnp.log(f), v, gb


def _hg_intra_factors(q, k, b):
    fac = []
    for i in range(CHUNK // SUB):
        bi = b[SUB * i:SUB * i + 1, :]
        eq = jnp.exp(b[SUB * i:SUB * (i + 1), :] - bi)
        ek = jnp.exp(jnp.minimum(bi - b, EXP_CLAMP))
        fac.append((eq, ek, q[SUB * i:SUB * (i + 1), :] * eq, k * ek))
    return fac


def _causal(n):
    r = lax.broadcasted_iota(jnp.int32, (n, n), 0)
    c = lax.broadcasted_iota(jnp.int32, (n, n), 1)
    return r >= c


def _hgrn_fwd(rest, lb, ng, name):
    t = rest.shape[0]
    bt = min(HG_ROWS, t)
    ncb = bt // CHUNK

    def body(hb_ref, lb_ref, ng_ref, y_ref, o_ref, st_ref, s_scr):
        @pl.when(pl.program_id(0) == 0)
        def _():
            s_scr[...] = jnp.zeros_like(s_scr)

        tril = _tri(CHUNK, True)
        causal = _causal(CHUNK)
        ones = jnp.ones((CHUNK, HD), F32)

        def chunk(c, carry):
            rows = pl.ds(pl.multiple_of(c * CHUNK, CHUNK), CHUNK)
            for h in range(B_HEADS):
                lbh = lb_ref[:, h * HD:(h + 1) * HD]
                _, _, q, _, _, k, g, v, gb = _hg_gates(hb_ref, rows, h, lbh)
                b = _dot_hi(tril, g)
                s0 = s_scr[h]
                st_ref[c, h] = s0
                o = _dot(q * jnp.exp(b), s0)
                a = jnp.concatenate([_dot(qe, ke, "nt") for _, _, qe, ke in _hg_intra_factors(q, k, b)], axis=0)
                o = o + _dot(jnp.where(causal, a, 0.0), v)
                blast = b[CHUNK - 1:CHUNK, :]
                kd = k * jnp.exp(blast - b)
                eb = jnp.exp(_dot_hi(g, ones, "tn", exact="b"))
                s_scr[h] = eb * s0 + _dot(kd, v, "tn")
                r = lax.rsqrt(jnp.mean(o * o, axis=-1, keepdims=True) + RMS_EPS)
                o_ref[rows, h * HD:(h + 1) * HD] = o
                y_ref[rows, h * HD:(h + 1) * HD] = (o * r * ng_ref[...] * _sigmoid(gb)).astype(BF16)
            return carry

        lax.fori_loop(0, ncb, chunk, 0)

    return pl.pallas_call(
        body,
        name=name,
        grid=(t // bt,),
        in_specs=[pl.BlockSpec((bt, 4 * B_WIDTH), lambda i: (i, 1)), pl.BlockSpec((1, B_WIDTH), lambda i: (0, 0)),
                  pl.BlockSpec((1, HD), lambda i: (0, 0))],
        out_specs=[pl.BlockSpec((bt, B_WIDTH), lambda i: (i, 0)), pl.BlockSpec((bt, B_WIDTH), lambda i: (i, 0)),
                   pl.BlockSpec((ncb, B_HEADS, HD, HD), lambda i: (i, 0, 0, 0))],
        out_shape=[jax.ShapeDtypeStruct((t, B_WIDTH), BF16), jax.ShapeDtypeStruct((t, B_WIDTH), F32),
                   jax.ShapeDtypeStruct((t // CHUNK, B_HEADS, HD, HD), F32)],
        scratch_shapes=[pltpu.VMEM((B_HEADS, HD, HD), F32)],
        compiler_params=_params("arbitrary"),
    )(rest, lb, ng)


def _hgrn_bwd(dy, rest, o_saved, states, lb, ng, name):
    t = rest.shape[0]
    bt = min(HG_ROWS, t)
    ncb = bt // CHUNK
    nb = t // bt

    def body(dy_ref, hb_ref, o_ref, st_ref, lb_ref, ng_ref, dh_ref, dlb_ref, dng_ref, ds_scr):
        @pl.when(pl.program_id(0) == 0)
        def _():
            ds_scr[...] = jnp.zeros_like(ds_scr)
            dlb_ref[...] = jnp.zeros_like(dlb_ref)
            dng_ref[...] = jnp.zeros_like(dng_ref)

        tril = _tri(CHUNK, True)
        triu = _tri(CHUNK, False)
        causal = _causal(CHUNK)
        ones = jnp.ones((CHUNK, HD), F32)
        ones8 = jnp.ones((8, HD), F32)
        last_row = lax.broadcasted_iota(jnp.int32, (CHUNK, HD), 0) == CHUNK - 1

        def chunk(cc, carry):
            dng_acc, dlb_acc = carry[0], list(carry[1:])
            c = ncb - 1 - cc
            rows = pl.ds(pl.multiple_of(c * CHUNK, CHUNK), CHUNK)
            for h in range(B_HEADS):
                cols = slice(h * HD, (h + 1) * HD)
                lbh = lb_ref[:, cols]
                qb, sq, q, sg, f, k, g, v, gb = _hg_gates(hb_ref, rows, h, lbh)
                o = o_ref[rows, cols]
                dyv = dy_ref[rows, cols].astype(F32)
                ngv = ng_ref[...]
                r = lax.rsqrt(jnp.mean(o * o, axis=-1, keepdims=True) + RMS_EPS)
                sgb = _sigmoid(gb)
                don = dyv * sgb
                dgb = dyv * (o * r * ngv) * sgb * (1.0 - sgb)
                dng_acc = dng_acc + jnp.sum(don * o * r, axis=0, keepdims=True)
                doh = don * ngv
                do = r * (doh - o * (r * r) * jnp.mean(doh * o, axis=-1, keepdims=True))
                b = _dot_hi(tril, g)
                ebt = jnp.exp(b)
                s0 = st_ref[c, h]
                ds1 = ds_scr[h]
                blast = b[CHUNK - 1:CHUNK, :]
                ekd = jnp.exp(blast - b)
                kd = k * ekd
                eb = jnp.exp(_dot_hi(g, ones, "tn", exact="b"))
                fac = _hg_intra_factors(q, k, b)
                a = jnp.concatenate([_dot(qe, ke, "nt") for _, _, qe, ke in fac], axis=0)
                a = jnp.where(causal, a, 0.0)
                da = jnp.where(causal, _dot(do, v, "nt"), 0.0)
                dv = _dot(a, do, "tn") + _dot(kd, ds1)
                dq = ebt * _dot(do, s0, "nt")
                dq = dq + jnp.concatenate(
                    [eq * _hdot(da[SUB * i:SUB * (i + 1), :], ke) for i, (eq, _, _, ke) in enumerate(fac)], axis=0)
                dk_state = ekd * _dot(v, ds1, "nt")
                dk = dk_state
                for i, (_, ek, qe, _) in enumerate(fac):
                    dk = dk + ek * _hdot(da[SUB * i:SUB * (i + 1), :], qe, "tn")
                ds_scr[h] = _dot(q * ebt, do, "tn") + eb * ds1
                extra = jnp.exp(blast) * _dot_hi(ones8, ds1 * s0, "nt")[0:1, :] \
                    + jnp.sum(k * dk_state, axis=0, keepdims=True)
                db = q * dq - k * dk + jnp.where(last_row, extra, 0.0)
                dg = _dot_hi(triu, db)
                df = dg / f - dk
                dlb_acc[h] = dlb_acc[h] + jnp.sum(df * (1.0 - sg), axis=0, keepdims=True)
                dfb = df * (1.0 - lbh) * sg * (1.0 - sg)
                dqb = dq * (sq * (1.0 + qb * (1.0 - sq)))
                dh_ref[rows, h * HD:(h + 1) * HD] = dqb.astype(BF16)
                dh_ref[rows, B_WIDTH + h * HD:B_WIDTH + (h + 1) * HD] = dfb.astype(BF16)
                dh_ref[rows, 2 * B_WIDTH + h * HD:2 * B_WIDTH + (h + 1) * HD] = dv.astype(BF16)
                dh_ref[rows, 3 * B_WIDTH + h * HD:3 * B_WIDTH + (h + 1) * HD] = dgb.astype(BF16)
            return (dng_acc, *dlb_acc)

        sums = lax.fori_loop(0, ncb, chunk, (jnp.zeros((1, HD), F32),) * (1 + B_HEADS))
        dng_ref[...] += sums[0]
        for h in range(B_HEADS):
            dlb_ref[:, h * HD:(h + 1) * HD] += sums[1 + h]

    rev = lambda i: (nb - 1 - i, 0)
    return pl.pallas_call(
        body,
        name=name,
        grid=(nb,),
        in_specs=[pl.BlockSpec((bt, B_WIDTH), rev), pl.BlockSpec((bt, 4 * B_WIDTH), lambda i: (nb - 1 - i, 1)),
                  pl.BlockSpec((bt, B_WIDTH), rev),
                  pl.BlockSpec((ncb, B_HEADS, HD, HD), lambda i: (nb - 1 - i, 0, 0, 0)),
                  pl.BlockSpec((1, B_WIDTH), lambda i: (0, 0)), pl.BlockSpec((1, HD), lambda i: (0, 0))],
        out_specs=[pl.BlockSpec((bt, 4 * B_WIDTH), rev), pl.BlockSpec((1, B_WIDTH), lambda i: (0, 0)),
                   pl.BlockSpec((1, HD), lambda i: (0, 0))],
        out_shape=[jax.ShapeDtypeStruct((t, 4 * B_WIDTH), BF16), jax.ShapeDtypeStruct((1, B_WIDTH), F32),
                   jax.ShapeDtypeStruct((1, HD), F32)],
        scratch_shapes=[pltpu.VMEM((B_HEADS, HD, HD), F32)],
        compiler_params=_params("arbitrary"),
    )(dy, rest, o_saved, states, lb, ng)


def _axpy2(c0, a0, c1, a1, name, tm=512):
    t, d = a0.shape
    tm = min(tm, t)

    def body(a_ref, b_ref, o_ref):
        o_ref[...] = c0 * a_ref[...] + c1 * b_ref[...]

    row = lambda i: (i, 0)
    return pl.pallas_call(
        body, name=name, grid=(t // tm,),
        in_specs=[pl.BlockSpec((tm, d), row), pl.BlockSpec((tm, d), row)],
        out_specs=pl.BlockSpec((tm, d), row),
        out_shape=jax.ShapeDtypeStruct((t, d), F32),
        compiler_params=_params("parallel"),
    )(a0, a1)


def _split_w_in(w_in_l):
    wqkv = w_in_l[:, :3 * A_WIDTH]
    wfa = jnp.pad(w_in_l[:, 3 * A_WIDTH:3 * A_WIDTH + A_HEADS], ((0, 0), (0, 128 - A_HEADS)))
    whb = w_in_l[:, 3 * A_WIDTH + A_HEADS:3 * A_WIDTH + A_HEADS + 4 * B_WIDTH]
    wgt = w_in_l[:, 3 * A_WIDTH + A_HEADS + 4 * B_WIDTH:]
    return wqkv, jnp.concatenate([wgt, whb, wfa], axis=1)


def _merge_w_in_grad(dwall):
    o = 3 * A_WIDTH
    return jnp.concatenate([dwall[:, :o], dwall[:, o + 4096:o + 4096 + A_HEADS], dwall[:, o + 2048:o + 4096],
                            dwall[:, o:o + 2048]], axis=1)


def _layer_fwd(x, xb, w, sp, l):
    t = x.shape[0]
    n = f"l{l}_"
    wqkv, wrest = _split_w_in(w["w_in"])
    qkv = _matmul(xb, wqkv, "nn", BF16, MM_ROWS, 768, D_MODEL, n + "proj_qkv")
    rest = _matmul(xb, wrest, "nn", F32, MM_ROWS, 1408, D_MODEL, n + "proj_rest")
    bf = jnp.pad(sp["b_fgate"], (0, 128 - A_HEADS)).reshape(1, 128)
    fcum, fcol = _fox_gate_fwd(rest, bf, n + "fox_gate_fwd")
    qa, ka, va, qn, kn = _fox_prep_fwd(qkv, fcol, n + "fox_prep_fwd")
    kstart, qend = _fox_block_ranges(qn, kn, fcum)
    ya, lse = _fox_fwd(qa, ka, va, kstart, n + "fox_fwd")
    lb = sp["lb"].reshape(1, B_WIDTH)
    ng = sp["norm_g"].reshape(1, HD)
    yb, ob, states = _hgrn_fwd(rest, lb, ng, n + "hgrn_fwd")
    merged = _merge_fwd(ya, yb, w["w_pa"], w["w_pb"], rest, n + "merge_fwd")
    x1, x1b, xh1, rs1 = _mm_res_ln(merged, w["w_out"], x, sp["ln1_g"], sp["ln1_b"], n + "out_ln1")
    wu, wg = w["w_ff_in"][:, :FFN_HIDDEN], w["w_ff_in"][:, FFN_HIDDEN:]
    a, hu, hg = _ffn_in_swiglu(x1b, wu, wg, n + "ffn_in_swiglu")
    x2, x2b, xh2, rs2 = _mm_res_ln(a, w["w_ff_out"], x1, sp["ln2_g"], sp["ln2_b"], n + "ffn_out_ln2")
    saved = dict(xb=xb, wqkv=wqkv, wrest=wrest, qkv=qkv, rest=rest, bf=bf, fcol=fcol, ka=ka, va=va, ya=ya, lse=lse,
                 kstart=kstart, qend=qend,
                 lb=lb, ng=ng, yb=yb, ob=ob, states=states, merged=merged, x1b=x1b, xh1=xh1, rs1=rs1, a=a,
                 wu=wu, wg=wg, hu=hu, hg=hg,
                 xh2=xh2, rs2=rs2)
    return x2, x2b, saved


def _layer_bwd(dys, coefs, w, sp, s, l):
    n = f"l{l}_"
    dz2, dz2b, dg2, db2 = _ln_bwd(dys, coefs, s["xh2"], s["rs2"], sp["ln2_g"], n + "ln2_bwd")
    du, dg = _ffn_out_dx_swiglu(dz2b, w["w_ff_out"], s["hu"], s["hg"], n + "ffn_out_dx_swiglu")
    d_wffout = _matmul(s["a"], dz2b, "tn", F32, 1408, 1024, DW_ROWS, n + "ffn_out_dw")
    dx1u = _matmul(du, s["wu"], "nt", F32, MM_ROWS, 1024, FFN_HIDDEN, n + "ffn_in_dx_u")
    dx1g = _matmul(dg, s["wg"], "nt", F32, MM_ROWS, 1024, FFN_HIDDEN, n + "ffn_in_dx_g")
    d_wffin = jnp.concatenate([_matmul(s["x1b"], du, "tn", F32, 1024, 1408, DW_ROWS, n + "ffn_in_dw_u"),
                               _matmul(s["x1b"], dg, "tn", F32, 1024, 1408, DW_ROWS, n + "ffn_in_dw_g")], axis=1)
    dz1, dz1b, dg1, db1 = _ln_bwd([dz2, dx1u, dx1g], [ALPHA, 1.0, 1.0], s["xh1"], s["rs1"], sp["ln1_g"],
                                  n + "ln1_bwd")
    d_wout = _matmul(s["merged"], dz1b, "tn", F32, 1024, 1024, DW_ROWS, n + "out_dw")
    dgates, dpa, dpb, dya, dyb = _merge_bwd(dz1b, w["w_out"], s["ya"], s["yb"], w["w_pa"], w["w_pb"], s["rest"],
                                  n + "merge_bwd")
    d_wpa = _matmul(s["ya"], dpa, "tn", F32, 512, 1024, DW_ROWS, n + "pa_dw")
    d_wpb = _matmul(s["yb"], dpb, "tn", F32, 512, 1024, DW_ROWS, n + "pb_dw")
    qb, dob = _fox_prep_bwd(s["qkv"], s["fcol"], s["lse"], dya, s["ya"], n + "fox_prep_bwd")
    dq, rsum = _fox_bwd_dq(qb, s["ka"], s["va"], dob, s["kstart"], n + "fox_bwd_dq")
    dk, dv, csum = _fox_bwd_dkv(qb, s["ka"], s["va"], dob, s["qend"], n + "fox_bwd_dkv")
    dfa, dbf = _fox_gate_bwd(rsum, csum, s["rest"], s["bf"], n + "fox_gate_bwd")
    dhb, dlb, dng = _hgrn_bwd(dyb, s["rest"], s["ob"], s["states"], s["lb"], s["ng"], n + "hgrn_bwd")
    dproj = jnp.concatenate([dq, dk, dv, dgates, dhb, dfa], axis=1)
    wall = jnp.concatenate([s["wqkv"], s["wrest"]], axis=1)
    dxp = _matmul(dproj, wall, "nt", F32, MM_ROWS, 1024, 1920, n + "proj_dx")
    d_wall = _matmul(s["xb"], dproj, "tn", F32, 1024, 1152, DW_ROWS, n + "proj_dw")
    grads = dict(w_in=_merge_w_in_grad(d_wall), w_pa=d_wpa, w_pb=d_wpb, w_out=d_wout, w_ff_in=d_wffin,
                 w_ff_out=d_wffout, b_fgate=dbf[0, :A_HEADS], lb=dlb[0], norm_g=dng[0], ln1_g=dg1[0], ln1_b=db1[0],
                 ln2_g=dg2[0], ln2_b=db2[0])
    return [dz1, dxp], [ALPHA, 1.0], grads


def _lower_bounds(logits):
    sm = jax.nn.softmax(logits.astype(F32), axis=0)
    return jnp.cumsum(sm, axis=0) - sm[0:1]


def _local_step(x, target, wfull, small):
    lbs, lb_vjp = jax.vjp(_lower_bounds, small["hgrn_lb_logits"])
    h, hb = x, x.astype(BF16)
    saved, sps = [], []
    for l in range(DEPTH):
        sp = dict(b_fgate=small["b_fgate"][l], lb=lbs[l], norm_g=small["hgrn_norm_g"][l], ln1_g=small["ln1_g"][l],
                  ln1_b=small["ln1_b"][l], ln2_g=small["ln2_g"][l], ln2_b=small["ln2_b"][l])
        h, hb, s = _layer_fwd(h, hb, wfull[l], sp, l)
        saved.append(s)
        sps.append(sp)
    dy, lpart = _loss_head(h, target)
    dys, coefs = [dy], [1.0]
    grads = [None] * DEPTH
    for l in reversed(range(DEPTH)):
        dys, coefs, grads[l] = _layer_bwd(dys, coefs, wfull[l], sps[l], saved[l], l)
    grad_x = _axpy2(coefs[0], dys[0], coefs[1], dys[1], "grad_x")
    d_logits = lb_vjp(jnp.stack([grads[l]["lb"] for l in range(DEPTH)]))[0]
    return lpart[0, 0], grad_x, grads, d_logits


_BIG = [("w_in", "w_in", (D_MODEL, IN_TOTAL), 1), ("w_branch_a", "w_pa", (A_WIDTH, D_MODEL), 1),
        ("w_branch_b", "w_pb", (B_WIDTH, D_MODEL), 1), ("w_out", "w_out", (D_MODEL, D_MODEL), 0),
        ("w_ff_in", "w_ff_in", (D_MODEL, 2 * FFN_HIDDEN), 1), ("w_ff_out", "w_ff_out", (FFN_HIDDEN, D_MODEL), 0)]
_SMALL = [("b_fgate", A_HEADS), ("hgrn_lb_logits", B_WIDTH), ("hgrn_norm_g", HD), ("ln1_g", D_MODEL),
          ("ln1_b", D_MODEL), ("ln2_g", D_MODEL), ("ln2_b", D_MODEL)]
N_BIG = len(_BIG)
SMALL_ROWS = 80


def _by_chip(full, axis):
    if axis == 0:
        return full.reshape(N_CHIPS, full.shape[0] // N_CHIPS, full.shape[1])
    n = full.shape[1] // N_CHIPS
    return jnp.stack([full[:, q * n:(q + 1) * n] for q in range(N_CHIPS)])


def _from_chips(shards, axis):
    if axis == 0:
        return shards.reshape(N_CHIPS * shards.shape[1], shards.shape[2])
    return jnp.concatenate([shards[q] for q in range(N_CHIPS)], axis=1)


def _pack_small(per_name):
    flat = jnp.concatenate([per_name[name].reshape(-1) for name, _ in _SMALL])
    return jnp.pad(flat, (0, SMALL_ROWS * 128 - flat.shape[0])).reshape(SMALL_ROWS, 128)


def _unpack_small(slab):
    flat, out, r = slab.reshape(-1), {}, 0
    for name, n in _SMALL:
        out[name] = flat[r:r + DEPTH * n].reshape(DEPTH, n)
        r += DEPTH * n
    return out


_ANY = pl.BlockSpec(memory_space=pl.ANY)


def _place():
    return lax.axis_index("x"), lax.axis_index("y"), lax.axis_index("c")


def _other_chips(x, y):
    return [(1 - x, y), (x, 1 - y), (1 - x, 1 - y)]


def _chip_exchange(mine_of, out_refs, send_sems, recv_sems, local_sems):
    x, y, c = _place()
    q = 2 * x + y
    started = []
    for w, out_ref in enumerate(out_refs):
        local = pltpu.make_async_copy(mine_of(w, q), out_ref.at[q], local_sems.at[w])
        local.start()
        started.append(local)
    sends = []
    for k, (px, py) in enumerate(_other_chips(x, y)):
        for w, out_ref in enumerate(out_refs):
            cp = pltpu.make_async_remote_copy(src_ref=mine_of(w, 2 * px + py), dst_ref=out_ref.at[q],
                                              send_sem=send_sems.at[3 * w + k], recv_sem=recv_sems.at[3 * w + k],
                                              device_id=(px, py, c), device_id_type=MESH)
            cp.start()
            sends.append(cp)
    for k, (px, py) in enumerate(_other_chips(x, y)):
        for w, out_ref in enumerate(out_refs):
            pltpu.make_async_remote_copy(src_ref=mine_of(w, q), dst_ref=out_ref.at[2 * px + py],
                                         send_sem=send_sems.at[3 * w + k], recv_sem=recv_sems.at[3 * w + k],
                                         device_id=(px, py, c), device_id_type=MESH).wait_recv()
    for cp in sends:
        cp.wait_send()
    for local in started:
        local.wait()


def _sem_scratch(n):
    return [pltpu.SemaphoreType.DMA((3 * n,)), pltpu.SemaphoreType.DMA((3 * n,)), pltpu.SemaphoreType.DMA((n,))]


def _gather_weights(mine):
    n = len(mine)

    def body(*refs):
        in_refs, out_refs = refs[:n], refs[n:2 * n]
        send_sems, recv_sems, local_sems, pair_send, pair_recv = refs[2 * n:]
        x, y, c = _place()
        _chip_exchange(lambda w, q: in_refs[w].at[c], [o.at[c] for o in out_refs], send_sems, recv_sems, local_sems)
        sibling = (x, y, 1 - c)
        fwds = []
        for w, o in enumerate(out_refs):
            cp = pltpu.make_async_remote_copy(src_ref=o.at[c], dst_ref=o.at[c], send_sem=pair_send.at[w],
                                              recv_sem=pair_recv.at[w], device_id=sibling, device_id_type=MESH)
            cp.start()
            fwds.append(cp)
        for w, o in enumerate(out_refs):
            pltpu.make_async_remote_copy(src_ref=o.at[1 - c], dst_ref=o.at[1 - c], send_sem=pair_send.at[w],
                                         recv_sem=pair_recv.at[w], device_id=sibling, device_id_type=MESH).wait_recv()
        for cp in fwds:
            cp.wait_send()

    return pl.pallas_call(
        body, name="gather_weights", in_specs=[_ANY] * n, out_specs=[_ANY] * n,
        out_shape=[jax.ShapeDtypeStruct((DEPTH, N_CHIPS) + m.shape[1:], m.dtype) for m in mine],
        scratch_shapes=_sem_scratch(n) + [pltpu.SemaphoreType.DMA((n,)), pltpu.SemaphoreType.DMA((n,))],
    )(*mine)


def _pair_exchange(gs):
    n = len(gs)

    def body(*refs):
        g_refs, a_refs, send_sems, recv_sems = refs[:n], refs[n:2 * n], refs[2 * n], refs[2 * n + 1]
        x, y, c = _place()
        cps = []
        for w in range(n):
            cp = pltpu.make_async_remote_copy(src_ref=g_refs[w].at[1 - c], dst_ref=a_refs[w], send_sem=send_sems.at[w],
                                              recv_sem=recv_sems.at[w], device_id=(x, y, 1 - c), device_id_type=MESH)
            cp.start()
            cps.append(cp)
        for cp in cps:
            cp.wait()

    return pl.pallas_call(
        body, name="grad_pair_exchange", in_specs=[_ANY] * n, out_specs=[_ANY] * n,
        out_shape=[jax.ShapeDtypeStruct(g.shape[1:], g.dtype) for g in gs],
        scratch_shapes=[pltpu.SemaphoreType.DMA((n,)), pltpu.SemaphoreType.DMA((n,))],
    )(*gs)


def _row_block(rows):
    return math.gcd(rows, 256)


def _pair_sum(g, a, layer, name):
    _, nq, rows, cols = g.shape
    tb = _row_block(rows)

    def body(l_ref, g_ref, a_ref, o_ref):
        o_ref[...] = (g_ref[...] + a_ref[...]).astype(BF16)

    return pl.pallas_call(
        body, name=name,
        grid_spec=pltpu.PrefetchScalarGridSpec(
            num_scalar_prefetch=1, grid=(nq, rows // tb),
            in_specs=[pl.BlockSpec((None, None, tb, cols), lambda q, i, l_ref: (l_ref[0], q, i, 0)),
                      pl.BlockSpec((None, tb, cols), lambda q, i, l_ref: (q, i, 0))],
            out_specs=pl.BlockSpec((None, tb, cols), lambda q, i, l_ref: (q, i, 0))),
        out_shape=jax.ShapeDtypeStruct((nq, rows, cols), BF16),
        compiler_params=_params("parallel", "parallel"),
    )(layer.reshape(1).astype(jnp.int32), g, a)


def _shard_exchange(ps):
    n = len(ps)

    def body(*refs):
        p_refs, b_refs = refs[:n], refs[n:2 * n]
        send_sems, recv_sems, local_sems = refs[2 * n:]
        _chip_exchange(lambda w, q: p_refs[w].at[q], b_refs, send_sems, recv_sems, local_sems)

    return pl.pallas_call(
        body, name="grad_shard_exchange", in_specs=[_ANY] * n, out_specs=[_ANY] * n,
        out_shape=[jax.ShapeDtypeStruct(p.shape, p.dtype) for p in ps],
        scratch_shapes=_sem_scratch(n),
    )(*ps)


def _sum4(b, name):
    _, rows, cols = b.shape
    tb = _row_block(rows)

    def body(b_ref, o_ref):
        o_ref[...] = ((b_ref[0].astype(F32) + b_ref[1].astype(F32)) + b_ref[2].astype(F32)) + b_ref[3].astype(F32)

    return pl.pallas_call(
        body, name=name, grid=(rows // tb,),
        in_specs=[pl.BlockSpec((N_CHIPS, tb, cols), lambda i: (0, i, 0))],
        out_specs=pl.BlockSpec((tb, cols), lambda i: (i, 0)),
        out_shape=jax.ShapeDtypeStruct((rows, cols), F32),
        compiler_params=_params("parallel"),
    )(b)


def _result_exchange(gcs):
    n = len(gcs)

    def body(*refs):
        g_refs, o_refs, send_sems, recv_sems = refs[:n], refs[n:2 * n], refs[2 * n], refs[2 * n + 1]
        x, y, c = _place()
        cps = []
        for w in range(n):
            cp = pltpu.make_async_remote_copy(src_ref=g_refs[w], dst_ref=o_refs[w], send_sem=send_sems.at[w],
                                              recv_sem=recv_sems.at[w], device_id=(x, y, 1 - c), device_id_type=MESH)
            cp.start()
            cps.append(cp)
        for cp in cps:
            cp.wait()

    return pl.pallas_call(
        body, name="grad_result_exchange", in_specs=[_ANY] * n, out_specs=[_ANY] * n,
        out_shape=[jax.ShapeDtypeStruct(g.shape, g.dtype) for g in gcs],
        scratch_shapes=[pltpu.SemaphoreType.DMA((n,)), pltpu.SemaphoreType.DMA((n,))],
    )(*gcs)


def _allreduce_small(v):
    def body(v_ref, o_ref, buf, send_sems, recv_sems):
        x, y, c = _place()
        me = 4 * x + 2 * y + c
        buf[me] = v_ref[...]
        peers = []
        for k in range(1, N_DEV):
            px = 1 - x if k & 4 else x
            py = 1 - y if k & 2 else y
            pc = 1 - c if k & 1 else c
            peers.append((px, py, pc))
        sends = []
        for k, peer in enumerate(peers):
            cp = pltpu.make_async_remote_copy(src_ref=v_ref, dst_ref=buf.at[me], send_sem=send_sems.at[k],
                                              recv_sem=recv_sems.at[k], device_id=peer, device_id_type=MESH)
            cp.start()
            sends.append(cp)
        for k, (px, py, pc) in enumerate(peers):
            pltpu.make_async_remote_copy(src_ref=v_ref, dst_ref=buf.at[4 * px + 2 * py + pc], send_sem=send_sems.at[k],
                                         recv_sem=recv_sems.at[k], device_id=(px, py, pc),
                                         device_id_type=MESH).wait_recv()
        for cp in sends:
            cp.wait_send()
        acc = buf[0]
        for i in range(1, N_DEV):
            acc = acc + buf[i]
        o_ref[...] = acc

    vm = pl.BlockSpec(memory_space=pltpu.VMEM)
    return pl.pallas_call(
        body, name="small_allreduce", in_specs=[vm], out_specs=vm,
        out_shape=jax.ShapeDtypeStruct(v.shape, F32),
        scratch_shapes=[pltpu.VMEM((N_DEV,) + v.shape, F32), pltpu.SemaphoreType.DMA((N_DEV - 1,)),
                        pltpu.SemaphoreType.DMA((N_DEV - 1,))],
    )(v)


def _adam_update(w, g, m, v):
    nm = ADAM_B1 * m + (1.0 - ADAM_B1) * g
    nv = ADAM_B2 * v + (1.0 - ADAM_B2) * (g * g)
    m_hat = nm / (1.0 - ADAM_B1 ** ADAM_STEP)
    v_hat = nv / (1.0 - ADAM_B2 ** ADAM_STEP)
    return -ADAM_LR * (m_hat / (jnp.sqrt(v_hat) + ADAM_EPS) + ADAM_WD * w), nm, nv


def _adamw_small(w, g, m, v, name):
    def body(w_ref, g_ref, m_ref, v_ref, d_ref, nm_ref, nv_ref):
        d_ref[...], nm_ref[...], nv_ref[...] = _adam_update(w_ref[...], g_ref[...], m_ref[...], v_ref[...])

    vm = pl.BlockSpec(memory_space=pltpu.VMEM)
    return pl.pallas_call(
        body, name=name, in_specs=[vm] * 4, out_specs=[vm] * 3,
        out_shape=[jax.ShapeDtypeStruct(w.shape, F32)] * 3,
    )(w, g, m, v)


def _adamw_big(w, m, v, g_own, g_other, layer, name):
    _, rows, cols = w.shape
    tb = _row_block(rows)

    def body(l_ref, w_ref, m_ref, v_ref, go_ref, gx_ref, g_ref, d_ref, nm_ref, nv_ref):
        gv = jnp.where(pl.program_id(0) == l_ref[0], go_ref[...], gx_ref[...])
        g_ref[...] = gv
        d_ref[...], nm_ref[...], nv_ref[...] = _adam_update(w_ref[...], gv, m_ref[...], v_ref[...])

    per_layer = pl.BlockSpec((None, tb, cols), lambda l, i, l_ref: (l, i, 0))
    shared = pl.BlockSpec((tb, cols), lambda l, i, l_ref: (i, 0))
    return pl.pallas_call(
        body, name=name,
        grid_spec=pltpu.PrefetchScalarGridSpec(
            num_scalar_prefetch=1, grid=(DEPTH, rows // tb),
            in_specs=[per_layer, per_layer, per_layer, shared, shared], out_specs=[per_layer] * 4),
        out_shape=[jax.ShapeDtypeStruct(w.shape, F32)] * 4,
        compiler_params=_params("parallel", "parallel"),
    )(layer.reshape(1).astype(jnp.int32), w, m, v, g_own, g_other)


def kernel(x, w_in, b_fgate, hgrn_lb_logits, hgrn_norm_g, w_branch_a, w_branch_b, w_out, ln1_g, ln1_b, w_ff_in, w_ff_out, ln2_g, ln2_b, loss_target, m_w_in, m_b_fgate, m_hgrn_lb_logits, m_hgrn_norm_g, m_w_branch_a, m_w_branch_b, m_w_out, m_ln1_g, m_ln1_b, m_w_ff_in, m_w_ff_out, m_ln2_g, m_ln2_b, v_w_in, v_b_fgate, v_hgrn_lb_logits, v_hgrn_norm_g, v_w_branch_a, v_w_branch_b, v_w_out, v_ln1_g, v_ln1_b, v_w_ff_in, v_w_ff_out, v_ln2_g, v_ln2_b):
    weights = dict(w_in=w_in, b_fgate=b_fgate, hgrn_lb_logits=hgrn_lb_logits, hgrn_norm_g=hgrn_norm_g,
                   w_branch_a=w_branch_a, w_branch_b=w_branch_b, w_out=w_out, ln1_g=ln1_g, ln1_b=ln1_b,
                   w_ff_in=w_ff_in, w_ff_out=w_ff_out, ln2_g=ln2_g, ln2_b=ln2_b)
    mom1 = dict(w_in=m_w_in, b_fgate=m_b_fgate, hgrn_lb_logits=m_hgrn_lb_logits, hgrn_norm_g=m_hgrn_norm_g,
                w_branch_a=m_w_branch_a, w_branch_b=m_w_branch_b, w_out=m_w_out, ln1_g=m_ln1_g, ln1_b=m_ln1_b,
                w_ff_in=m_w_ff_in, w_ff_out=m_w_ff_out, ln2_g=m_ln2_g, ln2_b=m_ln2_b)
    mom2 = dict(w_in=v_w_in, b_fgate=v_b_fgate, hgrn_lb_logits=v_hgrn_lb_logits, hgrn_norm_g=v_hgrn_norm_g,
                w_branch_a=v_w_branch_a, w_branch_b=v_w_branch_b, w_out=v_w_out, ln1_g=v_ln1_g, ln1_b=v_ln1_b,
                w_ff_in=v_w_ff_in, w_ff_out=v_w_ff_out, ln2_g=v_ln2_g, ln2_b=v_ln2_b)
    core = lax.axis_index("c")

    gathered = _gather_weights([weights[name].astype(BF16) for name, _, _, _ in _BIG])
    wfull = [{key: _from_chips(gathered[w][l], axis) for w, (_, key, _, axis) in enumerate(_BIG)}
             for l in range(DEPTH)]
    small = {name: weights[name] for name, _ in _SMALL}

    loss_part, grad_x, grads, d_logits = _local_step(x[0], loss_target[0], wfull, small)

    g_all = [jnp.stack([_by_chip(grads[l][key], axis) for l in range(DEPTH)]) for _, key, _, axis in _BIG]
    received = _pair_exchange(g_all)
    pair = [_pair_sum(g_all[w], received[w], core, f"grad_pair_sum_{w}") for w in range(N_BIG)]
    by_chip = _shard_exchange(pair)
    g_layer = [_sum4(by_chip[w], f"grad_chip_sum_{w}") for w in range(N_BIG)]
    g_other = _result_exchange(g_layer)
    out_g, out_d, out_m, out_v = {}, {}, {}, {}
    for w, (name, _, _, _) in enumerate(_BIG):
        out_g[name], out_d[name], out_m[name], out_v[name] = _adamw_big(
            weights[name], mom1[name], mom2[name], g_layer[w], g_other[w], core, f"adamw_{name}")

    small_grads = {name: jnp.stack([grads[l][key] for l in range(DEPTH)])
                   for name, key in [("b_fgate", "b_fgate"), ("hgrn_norm_g", "norm_g"), ("ln1_g", "ln1_g"),
                                     ("ln1_b", "ln1_b"), ("ln2_g", "ln2_g"), ("ln2_b", "ln2_b")]}
    small_grads["hgrn_lb_logits"] = d_logits
    gs = _allreduce_small(_pack_small(small_grads))
    ds, ms, vs = _adamw_small(_pack_small(small), gs, _pack_small({n: mom1[n] for n, _ in _SMALL}),
                              _pack_small({n: mom2[n] for n, _ in _SMALL}), "adamw_small")
    for tree, slab in ((out_g, gs), (out_d, ds), (out_m, ms), (out_v, vs)):
        tree.update(_unpack_small(slab))

    loss = lax.psum(loss_part, ("x", "y", "c"))
    order = ["w_in", "b_fgate", "hgrn_lb_logits", "hgrn_norm_g", "w_branch_a", "w_branch_b", "w_out", "ln1_g", "ln1_b",
             "w_ff_in", "w_ff_out", "ln2_g", "ln2_b"]
    return (loss, grad_x[None], *[out_g[n] for n in order], *[out_d[n] for n in order],
            *[out_m[n] for n in order], *[out_v[n] for n in order])
```

```python
import functools
import math

import jax
import jax.numpy as jnp
import numpy as np
from jax import lax
from jax.experimental import pallas as pl
from jax.experimental.pallas import tpu as pltpu

F32 = jnp.float32
BF16 = jnp.bfloat16

D_MODEL = 1024
DEPTH = 2
A_HEADS = 8
A_WIDTH = 512
B_WIDTH = 512
B_HEADS = 4
HD = 128
CHUNK = 64
SUB = 16
FFN_HIDDEN = 2816
IN_TOTAL = 5640
ALPHA = (2 * DEPTH) ** 0.25
LN_EPS = 1e-5
RMS_EPS = 1e-6
ADAM_LR = 0.001
ADAM_B1 = 0.9
ADAM_B2 = 0.999
ADAM_EPS = 1e-08
ADAM_WD = 0.01
ADAM_STEP = 10
EXP_CLAMP = 60.0

VMEM_LIMIT_BYTES = 56 * 1024 * 1024
MM_ROWS = 1024
DW_ROWS = 2048
N_CHIPS = 4
N_DEV = 8
MESH = pl.DeviceIdType.MESH

_DN = {
    "nn": (((1,), (0,)), ((), ())),
    "nt": (((1,), (1,)), ((), ())),
    "tn": (((0,), (0,)), ((), ())),
}


def _dot(a, b, mode="nn"):
    return lax.dot_general(a.astype(BF16), b.astype(BF16), _DN[mode], preferred_element_type=F32)


def _pieces(x):
    h = x.astype(BF16)
    r = x - h.astype(F32)
    m = r.astype(BF16)
    return h, m, (r - m.astype(F32)).astype(BF16)


def _dot_hi(a, b, mode="nn", exact="a"):
    if exact == "a":
        h, m, l = _pieces(b)
        return (_dot(a, l, mode) + _dot(a, m, mode)) + _dot(a, h, mode)
    h, m, l = _pieces(a)
    return (_dot(l, b, mode) + _dot(m, b, mode)) + _dot(h, b, mode)


def _hdot(a, b, mode="nn"):
    bh, bl, _ = _pieces(b)
    return _dot(a, bl, mode) + _dot(a, bh, mode)


def _params(*sem):
    return pltpu.CompilerParams(dimension_semantics=sem, vmem_limit_bytes=VMEM_LIMIT_BYTES)


def _sigmoid(x):
    return 1.0 / (1.0 + jnp.exp(-x))


def _matmul(a, b, mode, out_dtype, tm, tn, tk, name):
    if mode == "nn":
        (m, k), (k2, n) = a.shape, b.shape
    elif mode == "nt":
        (m, k), (n, k2) = a.shape, b.shape
    else:
        (k, m), (k2, n) = a.shape, b.shape
    assert k == k2, (a.shape, b.shape, mode)
    tm, tn, tk = min(tm, m), min(tn, n), min(tk, k)
    assert m % tm == 0 and n % tn == 0 and k % tk == 0, (a.shape, b.shape, tm, tn, tk)
    nk = k // tk
    if mode == "tn":
        a_spec = pl.BlockSpec((tk, tm), lambda j, i, kk: (kk, i))
    else:
        a_spec = pl.BlockSpec((tm, tk), lambda j, i, kk: (i, kk))
    if mode == "nt":
        b_spec = pl.BlockSpec((tn, tk), lambda j, i, kk: (j, kk))
    else:
        b_spec = pl.BlockSpec((tk, tn), lambda j, i, kk: (kk, j))
    use_acc = nk > 1 and out_dtype != F32

    def body(a_ref, b_ref, o_ref, *scratch):
        p = _dot(a_ref[...], b_ref[...], mode)
        if nk == 1:
            o_ref[...] = p.astype(out_dtype)
            return
        acc_ref = scratch[0] if use_acc else o_ref
        kk = pl.program_id(2)

        @pl.when(kk == 0)
        def _():
            acc_ref[...] = p

        @pl.when(kk > 0)
        def _():
            acc_ref[...] += p

        if use_acc:
            @pl.when(kk == nk - 1)
            def _():
                o_ref[...] = acc_ref[...].astype(out_dtype)

    return pl.pallas_call(
        body,
        name=name,
        grid=(n // tn, m // tm, nk),
        in_specs=[a_spec, b_spec],
        out_specs=pl.BlockSpec((tm, tn), lambda j, i, kk: (i, j)),
        out_shape=jax.ShapeDtypeStruct((m, n), out_dtype),
        scratch_shapes=[pltpu.VMEM((tm, tn), F32)] if use_acc else [],
        compiler_params=_params("parallel", "parallel", "arbitrary"),
    )(a, b)


def _mm_res_ln(a, w, res, g, b, name, tm=512):
    t, k = a.shape
    d = w.shape[1]
    tm = min(tm, t)

    def body(a_ref, w_ref, r_ref, g_ref, b_ref, y_ref, yb_ref, xh_ref, rs_ref):
        z = ALPHA * r_ref[...] + _dot(a_ref[...], w_ref[...])
        mu = jnp.mean(z, axis=-1, keepdims=True)
        zc = z - mu
        var = jnp.mean(zc * zc, axis=-1, keepdims=True)
        rstd = lax.rsqrt(var + LN_EPS)
        xh = zc * rstd
        y = xh * g_ref[...] + b_ref[...]
        y_ref[...] = y
        yb_ref[...] = y.astype(BF16)
        xh_ref[...] = xh
        rs_ref[...] = rstd

    row = lambda i: (i, 0)
    fix = lambda i: (0, 0)
    return pl.pallas_call(
        body,
        name=name,
        grid=(t // tm,),
        in_specs=[pl.BlockSpec((tm, k), row), pl.BlockSpec((k, d), fix), pl.BlockSpec((tm, d), row),
                  pl.BlockSpec((1, d), fix), pl.BlockSpec((1, d), fix)],
        out_specs=[pl.BlockSpec((tm, d), row), pl.BlockSpec((tm, d), row), pl.BlockSpec((tm, d), row),
                   pl.BlockSpec((tm, 1), row)],
        out_shape=[jax.ShapeDtypeStruct((t, d), F32), jax.ShapeDtypeStruct((t, d), BF16),
                   jax.ShapeDtypeStruct((t, d), F32), jax.ShapeDtypeStruct((t, 1), F32)],
        compiler_params=_params("parallel"),
    )(a, w, res, g.reshape(1, d), b.reshape(1, d))


def _ln_bwd(dys, coefs, xhat, rstd, g, name, tm=512):
    t, d = xhat.shape
    tm = min(tm, t)
    n_in = len(dys)

    def body(*refs):
        dy_refs = refs[:n_in]
        xh_ref, rs_ref, g_ref, dz_ref, dzb_ref, dg_ref, db_ref = refs[n_in:]
        dy = coefs[0] * dy_refs[0][...].astype(F32)
        for c, r in zip(coefs[1:], dy_refs[1:]):
            dy = dy + c * r[...].astype(F32)
        xh = xh_ref[...]
        dxh = dy * g_ref[...]
        m1 = jnp.mean(dxh, axis=-1, keepdims=True)
        m2 = jnp.mean(dxh * xh, axis=-1, keepdims=True)
        dz = rs_ref[...] * (dxh - m1 - xh * m2)
        dz_ref[...] = dz
        dzb_ref[...] = dz.astype(BF16)
        pg = jnp.sum(dy * xh, axis=0, keepdims=True)
        pb = jnp.sum(dy, axis=0, keepdims=True)

        @pl.when(pl.program_id(0) == 0)
        def _():
            dg_ref[...] = pg
            db_ref[...] = pb

        @pl.when(pl.program_id(0) > 0)
        def _():
            dg_ref[...] += pg
            db_ref[...] += pb

    row = lambda i: (i, 0)
    fix = lambda i: (0, 0)
    return pl.pallas_call(
        body,
        name=name,
        grid=(t // tm,),
        in_specs=[pl.BlockSpec((tm, d), row)] * n_in
        + [pl.BlockSpec((tm, d), row), pl.BlockSpec((tm, 1), row), pl.BlockSpec((1, d), fix)],
        out_specs=[pl.BlockSpec((tm, d), row), pl.BlockSpec((tm, d), row), pl.BlockSpec((1, d), fix),
                   pl.BlockSpec((1, d), fix)],
        out_shape=[jax.ShapeDtypeStruct((t, d), F32), jax.ShapeDtypeStruct((t, d), BF16),
                   jax.ShapeDtypeStruct((1, d), F32), jax.ShapeDtypeStruct((1, d), F32)],
        compiler_params=_params("arbitrary"),
    )(*dys, xhat, rstd, g.reshape(1, d))


def _loss_head(y, target, name="loss_head", tm=512):
    t, d = y.shape
    tm = min(tm, t)

    def body(y_ref, t_ref, dy_ref, l_ref):
        e = y_ref[...] - t_ref[...]
        dy_ref[...] = e * (1.0 / d)
        part = jnp.full((8, 128), 0.5 / d, F32) * jnp.sum(e * e)

        @pl.when(pl.program_id(0) == 0)
        def _():
            l_ref[...] = part

        @pl.when(pl.program_id(0) > 0)
        def _():
            l_ref[...] += part

    row = lambda i: (i, 0)
    return pl.pallas_call(
        body,
        name=name,
        grid=(t // tm,),
        in_specs=[pl.BlockSpec((tm, d), row), pl.BlockSpec((tm, d), row)],
        out_specs=[pl.BlockSpec((tm, d), row), pl.BlockSpec((8, 128), lambda i: (0, 0))],
        out_shape=[jax.ShapeDtypeStruct((t, d), F32), jax.ShapeDtypeStruct((8, 128), F32)],
        compiler_params=_params("arbitrary"),
    )(y, target)


FFN_COLS = FFN_HIDDEN // 2


def _ffn_in_swiglu(xb, wu, wg, name, tm=512):
    t, d = xb.shape
    tm = min(tm, t)

    def body(x_ref, wu_ref, wg_ref, a_ref, u_ref, g_ref):
        x = x_ref[...]
        u = _dot(x, wu_ref[...])
        g = _dot(x, wg_ref[...])
        u_ref[...] = u
        g_ref[...] = g
        a_ref[...] = (g * _sigmoid(g) * u).astype(BF16)

    wspec = pl.BlockSpec((d, FFN_COLS), lambda j, i: (0, j))
    out = pl.BlockSpec((tm, FFN_COLS), lambda j, i: (i, j))
    return pl.pallas_call(
        body,
        name=name,
        grid=(FFN_HIDDEN // FFN_COLS, t // tm),
        in_specs=[pl.BlockSpec((tm, d), lambda j, i: (i, 0)), wspec, wspec],
        out_specs=[out, out, out],
        out_shape=[jax.ShapeDtypeStruct((t, FFN_HIDDEN), BF16), jax.ShapeDtypeStruct((t, FFN_HIDDEN), F32),
                   jax.ShapeDtypeStruct((t, FFN_HIDDEN), F32)],
        compiler_params=_params("parallel", "parallel"),
    )(xb, wu, wg)


def _ffn_out_dx_swiglu(dzb, w_ff_out, u, g, name, tm=512):
    t, d = dzb.shape
    tm = min(tm, t)

    def body(dz_ref, w_ref, u_ref, g_ref, du_ref, dg_ref):
        da = _dot(dz_ref[...], w_ref[...], "nt")
        gv = g_ref[...]
        sg = _sigmoid(gv)
        du_ref[...] = (da * gv * sg).astype(BF16)
        dg_ref[...] = (da * u_ref[...] * (sg * (1.0 + gv * (1.0 - sg)))).astype(BF16)

    blk = pl.BlockSpec((tm, FFN_COLS), lambda j, i: (i, j))
    return pl.pallas_call(
        body,
        name=name,
        grid=(FFN_HIDDEN // FFN_COLS, t // tm),
        in_specs=[pl.BlockSpec((tm, d), lambda j, i: (i, 0)), pl.BlockSpec((FFN_COLS, d), lambda j, i: (j, 0)), blk, blk],
        out_specs=[blk, blk],
        out_shape=[jax.ShapeDtypeStruct((t, FFN_HIDDEN), BF16)] * 2,
        compiler_params=_params("parallel", "parallel"),
    )(dzb, w_ff_out, u, g)


def _merge_fwd(ya, yb, wpa, wpb, rest, name, tm=512):
    t = ya.shape[0]
    tm = min(tm, t)

    def body(ya_ref, yb_ref, wa_ref, wb_ref, ga_ref, gb_ref, o_ref):
        pa = _dot(ya_ref[...], wa_ref[...])
        pb = _dot(yb_ref[...], wb_ref[...])
        o_ref[...] = (_sigmoid(ga_ref[...]) * pa + _sigmoid(gb_ref[...]) * pb).astype(BF16)

    row = lambda i: (i, 0)
    fix = lambda i: (0, 0)
    return pl.pallas_call(
        body,
        name=name,
        grid=(t // tm,),
        in_specs=[pl.BlockSpec((tm, A_WIDTH), row), pl.BlockSpec((tm, B_WIDTH), row),
                  pl.BlockSpec((A_WIDTH, D_MODEL), fix), pl.BlockSpec((B_WIDTH, D_MODEL), fix),
                  pl.BlockSpec((tm, D_MODEL), lambda i: (i, 0)), pl.BlockSpec((tm, D_MODEL), lambda i: (i, 1))],
        out_specs=pl.BlockSpec((tm, D_MODEL), row),
        out_shape=jax.ShapeDtypeStruct((t, D_MODEL), BF16),
        compiler_params=_params("parallel"),
    )(ya, yb, wpa, wpb, rest, rest)


def _merge_bwd(dzb, w_out, ya, yb, wpa, wpb, rest, name, tm=512):
    t = ya.shape[0]
    tm = min(tm, t)

    def body(dz_ref, wo_ref, ya_ref, yb_ref, wa_ref, wb_ref, ga_ref, gb_ref, dg_ref, dpa_ref, dpb_ref, dya_ref,
             dyb_ref):
        dm_v = _dot(dz_ref[...], wo_ref[...], "nt")
        pa = _dot(ya_ref[...], wa_ref[...])
        pb = _dot(yb_ref[...], wb_ref[...])
        sa = _sigmoid(ga_ref[...])
        sb = _sigmoid(gb_ref[...])
        dg_ref[:, :D_MODEL] = (dm_v * pa * sa * (1.0 - sa)).astype(BF16)
        dg_ref[:, D_MODEL:] = (dm_v * pb * sb * (1.0 - sb)).astype(BF16)
        dpa = (dm_v * sa).astype(BF16)
        dpb = (dm_v * sb).astype(BF16)
        dpa_ref[...] = dpa
        dpb_ref[...] = dpb
        dya_ref[...] = _dot(dpa, wa_ref[...], "nt").astype(BF16)
        dyb_ref[...] = _dot(dpb, wb_ref[...], "nt")

    row = lambda i: (i, 0)
    fix = lambda i: (0, 0)
    return pl.pallas_call(
        body,
        name=name,
        grid=(t // tm,),
        in_specs=[pl.BlockSpec((tm, D_MODEL), row), pl.BlockSpec((D_MODEL, D_MODEL), fix),
                  pl.BlockSpec((tm, A_WIDTH), row), pl.BlockSpec((tm, B_WIDTH), row),
                  pl.BlockSpec((A_WIDTH, D_MODEL), fix), pl.BlockSpec((B_WIDTH, D_MODEL), fix),
                  pl.BlockSpec((tm, D_MODEL), lambda i: (i, 0)), pl.BlockSpec((tm, D_MODEL), lambda i: (i, 1))],
        out_specs=[pl.BlockSpec((tm, 2 * D_MODEL), row), pl.BlockSpec((tm, D_MODEL), row),
                   pl.BlockSpec((tm, D_MODEL), row), pl.BlockSpec((tm, A_WIDTH), row), pl.BlockSpec((tm, B_WIDTH), row)],
        out_shape=[jax.ShapeDtypeStruct((t, 2 * D_MODEL), BF16), jax.ShapeDtypeStruct((t, D_MODEL), BF16),
                   jax.ShapeDtypeStruct((t, D_MODEL), BF16), jax.ShapeDtypeStruct((t, A_WIDTH), BF16),
                   jax.ShapeDtypeStruct((t, B_WIDTH), F32)],
        compiler_params=_params("parallel"),
    )(dzb, w_out, ya, yb, wpa, wpb, rest, rest)


FA_BLOCK = 4224 // 128 - 1


def _tri(n, lower):
    r = lax.broadcasted_iota(jnp.int32, (n, n), 0)
    c = lax.broadcasted_iota(jnp.int32, (n, n), 1)
    return jnp.where((r >= c) if lower else (r <= c), 1.0, 0.0).astype(F32)


def _head_spread(expand):
    shape = (128, A_WIDTH) if expand else (A_WIDTH, 128)
    r = lax.broadcasted_iota(jnp.int32, shape, 0)
    c = lax.broadcasted_iota(jnp.int32, shape, 1)
    hit = ((c >= 64 * r) & (c < 64 * r + 64)) if expand else (r == 64 * c)
    return jnp.where(hit, 1.0, 0.0).astype(F32)


def _fox_gate_fwd(rest, bf, name, tb=512):
    t = rest.shape[0]
    tb = min(tb, t)

    def body(fa_ref, bf_ref, f_ref, fc_ref, carry):
        @pl.when(pl.program_id(0) == 0)
        def _():
            carry[...] = jnp.zeros_like(carry)

        z = fa_ref[...] + bf_ref[...]
        logf = jnp.minimum(z, 0.0) - jnp.log(1.0 + jnp.exp(-jnp.abs(z)))
        f = _dot_hi(_tri(tb, True), logf) + carry[...]
        f_ref[...] = f
        fc_ref[...] = _dot_hi(f, _head_spread(True), exact="b")
        carry[...] = f[tb - 1:tb, :]

    return pl.pallas_call(
        body,
        name=name,
        grid=(t // tb,),
        in_specs=[pl.BlockSpec((tb, 128), lambda i: (i, FA_BLOCK)), pl.BlockSpec((1, 128), lambda i: (0, 0))],
        out_specs=[pl.BlockSpec((tb, 128), lambda i: (i, 0)), pl.BlockSpec((tb, A_WIDTH), lambda i: (i, 0))],
        out_shape=[jax.ShapeDtypeStruct((t, 128), F32), jax.ShapeDtypeStruct((t, A_WIDTH), F32)],
        scratch_shapes=[pltpu.VMEM((1, 128), F32)],
        compiler_params=_params("arbitrary"),
    )(rest, bf)


def _fox_gate_bwd(rsum, csum, rest, bf, name, tb=512):
    t = rest.shape[0]
    tb = min(tb, t)
    nb = t // tb

    def body(rs_ref, cs_ref, fa_ref, bf_ref, dfa_ref, dbf_ref, carry):
        @pl.when(pl.program_id(0) == 0)
        def _():
            carry[...] = jnp.zeros_like(carry)

        d_f = _dot_hi(rs_ref[...] - cs_ref[...], _head_spread(False), exact="b")
        dlogf = _dot_hi(_tri(tb, False), d_f) + carry[...]
        carry[...] = dlogf[0:1, :]
        z = fa_ref[...] + bf_ref[...]
        dz = dlogf * _sigmoid(-z)
        dfa_ref[...] = dz.astype(BF16)
        part = jnp.sum(dz, axis=0, keepdims=True)

        @pl.when(pl.program_id(0) == 0)
        def _():
            dbf_ref[...] = part

        @pl.when(pl.program_id(0) > 0)
        def _():
            dbf_ref[...] += part

    return pl.pallas_call(
        body,
        name=name,
        grid=(nb,),
        in_specs=[pl.BlockSpec((tb, A_WIDTH), lambda i: (nb - 1 - i, 0)),
                  pl.BlockSpec((tb, A_WIDTH), lambda i: (nb - 1 - i, 0)),
                  pl.BlockSpec((tb, 128), lambda i: (nb - 1 - i, FA_BLOCK)),
                  pl.BlockSpec((1, 128), lambda i: (0, 0))],
        out_specs=[pl.BlockSpec((tb, 128), lambda i: (nb - 1 - i, 0)), pl.BlockSpec((1, 128), lambda i: (0, 0))],
        out_shape=[jax.ShapeDtypeStruct((t, 128), BF16), jax.ShapeDtypeStruct((1, 128), F32)],
        scratch_shapes=[pltpu.VMEM((1, 128), F32)],
        compiler_params=_params("arbitrary"),
    )(rsum, csum, rest, bf)


ATT_BLOCK = 512


def _head_mask(shape, j):
    lane = lax.broadcasted_iota(jnp.int32, shape, 1)
    return (lane < 64) if j == 0 else (lane >= 64)


def _split3(x):
    return tuple(p.astype(F32) for p in _pieces(x))


def _aug_lanes(tb, j):
    lane = lax.broadcasted_iota(jnp.int32, (tb, 128), 1)
    own = (lane < 64) if j == 0 else (lane >= 64)
    return own, lane - 64 * (1 - j)


def _aug_query(own, li, q, bias):
    h, m, l = _split3(bias)
    spare = jnp.where(li == 0, h, jnp.where(li == 1, m, jnp.where(li == 2, l, jnp.where(li < 6, 1.0, 0.0))))
    return jnp.where(own, q, spare).astype(BF16)


def _fox_prep_fwd(qkv, fcol, name, tb=512):
    t = qkv.shape[0]
    tb = min(tb, t)

    def body(q_ref, k_ref, v_ref, fc_ref, qa_ref, ka_ref, va_ref, qn_ref, kn_ref):
        fsw = pltpu.roll(fc_ref[...], 64, 1)
        q = q_ref[...].astype(F32)
        k = k_ref[...].astype(F32)
        v = v_ref[...].astype(F32)
        h, m, l = _split3(fsw)
        first = _head_mask((tb, 128), 0)
        for nrm_ref, x in ((qn_ref, q), (kn_ref, k)):
            n0 = jnp.max(jnp.sum(jnp.where(first, x * x, 0.0), axis=1, keepdims=True))
            n1 = jnp.max(jnp.sum(jnp.where(first, 0.0, x * x), axis=1, keepdims=True))
            nrm_ref[...] = jnp.where(_head_mask((8, 128), 0), n0, n1)
        for j in (0, 1):
            own, li = _aug_lanes(tb, j)
            cols = slice(128 * j, 128 * (j + 1))
            qa_ref[:, cols] = _aug_query(own, li, q * 0.125, fsw)
            ks = jnp.where(li < 3, 1.0, jnp.where(li == 3, -h, jnp.where(li == 4, -m, jnp.where(li == 5, -l, 0.0))))
            ka_ref[:, cols] = jnp.where(own, k, ks).astype(BF16)
            va_ref[:, cols] = jnp.where(own, v, 1.0).astype(BF16)

    blk = pl.BlockSpec((tb, 256), lambda i, h: (i, h))
    nrm = pl.BlockSpec((None, None, 8, 128), lambda i, h: (i, h, 0, 0))
    return pl.pallas_call(
        body, name=name, grid=(t // tb, 4),
        in_specs=[pl.BlockSpec((tb, 128), lambda i, h: (i, h)), pl.BlockSpec((tb, 128), lambda i, h: (i, 4 + h)),
                  pl.BlockSpec((tb, 128), lambda i, h: (i, 8 + h)), pl.BlockSpec((tb, 128), lambda i, h: (i, h))],
        out_specs=[blk, blk, blk, nrm, nrm],
        out_shape=[jax.ShapeDtypeStruct((t, 2 * A_WIDTH), BF16)] * 3
        + [jax.ShapeDtypeStruct((t // tb, 4, 8, 128), F32)] * 2,
        compiler_params=_params("parallel", "parallel"),
    )(qkv, qkv, qkv, fcol)


def _fox_prep_bwd(qkv, fcol, lse, do, o, name, tb=512):
    t = qkv.shape[0]
    tb = min(tb, t)

    def body(q_ref, fc_ref, lse_ref, do_ref, o_ref, qb_ref, dob_ref):
        gsw = pltpu.roll(fc_ref[...] - lse_ref[...], 64, 1)
        q = q_ref[...].astype(F32) * 0.125
        do_v = do_ref[...].astype(F32)
        prod = do_v * o_ref[...].astype(F32)
        for j in (0, 1):
            own, li = _aug_lanes(tb, j)
            cols = slice(128 * j, 128 * (j + 1))
            qb_ref[:, cols] = _aug_query(own, li, q, gsw)
            delta = jnp.sum(jnp.where(own, prod, 0.0), axis=1, keepdims=True)
            h, m, l = _split3(jnp.broadcast_to(delta, (tb, 128)))
            ds = jnp.where(li == 0, -h, jnp.where(li == 1, -m, jnp.where(li == 2, -l, 0.0)))
            dob_ref[:, cols] = jnp.where(own, do_v, ds).astype(BF16)

    pair = pl.BlockSpec((tb, 128), lambda i, h: (i, h))
    blk = pl.BlockSpec((tb, 256), lambda i, h: (i, h))
    return pl.pallas_call(
        body, name=name, grid=(t // tb, 4),
        in_specs=[pair, pair, pair, pair, pair],
        out_specs=[blk, blk],
        out_shape=[jax.ShapeDtypeStruct((t, 2 * A_WIDTH), BF16)] * 2,
        compiler_params=_params("parallel", "parallel"),
    )(qkv, fcol, lse, do, o)


def _tile_mask(n, transposed):
    r = lax.broadcasted_iota(jnp.int32, (n, n), 0)
    c = lax.broadcasted_iota(jnp.int32, (n, n), 1)
    return (c >= r) if transposed else (r >= c)


UNDERFLOW = -110.0


def _fox_block_ranges(qn, kn, fcum):
    t = fcum.shape[0]
    blk = min(ATT_BLOCK, t)
    nb = t // blk
    q2 = jnp.max(qn[:, :, 0, ::64].reshape(-1, A_HEADS), axis=0)
    k2 = jnp.max(kn[:, :, 0, ::64].reshape(-1, A_HEADS), axis=0)
    bound = 2.0 * jnp.sqrt(q2 * k2) * 0.125
    f = fcum[:, :A_HEADS]
    first = f[0::blk].T
    last = f[blk - 1::blk].T
    dead = (bound[:, None, None] + first[:, :, None] - last[:, None, :]) < UNDERFLOW
    qi = jnp.arange(nb)[None, :, None]
    kj = jnp.arange(nb)[None, None, :]
    dead = dead & (kj < qi)
    kstart = jnp.sum(dead, axis=2).astype(jnp.int32)
    qend = (kj[0] + jnp.sum((~dead) & (qi > kj), axis=1)).astype(jnp.int32)
    return kstart.reshape(-1), qend.reshape(-1)


def _fox_fwd(qa, ka, va, kstart, name):
    t = qa.shape[0]
    bq = min(ATT_BLOCK, t)
    nq = t // bq

    def body(ks_ref, q_ref, k_ref, v_ref, o_ref, lse_ref):
        i = pl.program_id(1)
        hp = pl.program_id(0)
        k0 = [ks_ref[(2 * hp + j) * nq + i] for j in (0, 1)]
        both0 = jnp.maximum(k0[0], k0[1])

        def head(j, kb, m, acc, masked):
            rows = pl.ds(pl.multiple_of(kb * bq, bq), bq)
            cols = slice(128 * j, 128 * (j + 1))
            s = _dot(q_ref[:, cols], k_ref[rows, cols], "nt")
            if masked:
                s = jnp.where(_tile_mask(bq, False), s, -jnp.inf)
            m_new = jnp.maximum(m, jnp.max(s, axis=1, keepdims=True))
            return m_new, jnp.exp(m - m_new) * acc + _dot(jnp.exp(s - m_new), v_ref[rows, cols])

        def pair(kb, carry, masked):
            return head(0, kb, carry[0], carry[1], masked) + head(1, kb, carry[2], carry[3], masked)

        init = (jnp.full((bq, 1), -jnp.inf, F32), jnp.zeros((bq, 128), F32))
        alone = [lax.fori_loop(k0[j], both0, lambda kb, c, j=j: head(j, kb, c[0], c[1], False), init) for j in (0, 1)]
        carry = lax.fori_loop(both0, i, lambda kb, c: pair(kb, c, False), alone[0] + alone[1])
        carry = pair(i, carry, True)
        outs = []
        for j in (0, 1):
            m, acc = carry[2 * j], carry[2 * j + 1]
            spare = 64 * (1 - j)
            l = acc[:, spare:spare + 1]
            outs.append((acc / l, m + jnp.log(l)))
        msk = _head_mask((bq, 128), 0)
        o_ref[...] = jnp.where(msk, outs[0][0], outs[1][0]).astype(BF16)
        lse_ref[...] = jnp.where(msk, outs[0][1], outs[1][1])

    res = pl.BlockSpec((t, 256), lambda h, i, tbl: (0, h))
    out = pl.BlockSpec((bq, 128), lambda h, i, tbl: (i, h))
    return pl.pallas_call(
        body,
        name=name,
        grid_spec=pltpu.PrefetchScalarGridSpec(
            num_scalar_prefetch=1, grid=(4, nq),
            in_specs=[pl.BlockSpec((bq, 256), lambda h, i, tbl: (i, h)), res, res],
            out_specs=[out, out]),
        out_shape=[jax.ShapeDtypeStruct((t, A_WIDTH), BF16), jax.ShapeDtypeStruct((t, A_WIDTH), F32)],
        compiler_params=_params("parallel", "parallel"),
    )(kstart, qa, ka, va)


def _fox_bwd_dq(qb, ka, va, dob, kstart, name):
    t = qb.shape[0]
    bq = min(ATT_BLOCK, t)
    nq = t // bq

    def body(ks_ref, q_ref, k_ref, v_ref, do_ref, dq_ref, rs_ref):
        i = pl.program_id(1)
        hp = pl.program_id(0)
        k0 = [ks_ref[(2 * hp + j) * nq + i] for j in (0, 1)]
        both0 = jnp.maximum(k0[0], k0[1])

        def head(j, kb, acc, masked):
            rows = pl.ds(pl.multiple_of(kb * bq, bq), bq)
            cols = slice(128 * j, 128 * (j + 1))
            ks = k_ref[rows, cols]
            s = _dot(q_ref[:, cols], ks, "nt")
            if masked:
                s = jnp.where(_tile_mask(bq, False), s, -jnp.inf)
            ds = jnp.exp(s) * _dot(do_ref[:, cols], v_ref[rows, cols], "nt")
            return acc + _dot(ds, ks)

        def pair(kb, accs, masked):
            return head(0, kb, accs[0], masked), head(1, kb, accs[1], masked)

        zero = jnp.zeros((bq, 128), F32)
        alone = [lax.fori_loop(k0[j], both0, lambda kb, c, j=j: head(j, kb, c, False), zero) for j in (0, 1)]
        accs = lax.fori_loop(both0, i, lambda kb, c: pair(kb, c, False), tuple(alone))
        accs = pair(i, accs, True)
        outs = []
        for j in (0, 1):
            spare = 64 * (1 - j)
            outs.append((accs[j] * 0.125, accs[j][:, spare:spare + 1]))
        msk = _head_mask((bq, 128), 0)
        dq_ref[...] = jnp.where(msk, outs[0][0], outs[1][0]).astype(BF16)
        rs_ref[...] = jnp.where(msk, outs[0][1], outs[1][1])

    blk = pl.BlockSpec((bq, 256), lambda h, i, tbl: (i, h))
    res = pl.BlockSpec((t, 256), lambda h, i, tbl: (0, h))
    out = pl.BlockSpec((bq, 128), lambda h, i, tbl: (i, h))
    return pl.pallas_call(
        body,
        name=name,
        grid_spec=pltpu.PrefetchScalarGridSpec(
            num_scalar_prefetch=1, grid=(4, nq), in_specs=[blk, res, res, blk], out_specs=[out, out]),
        out_shape=[jax.ShapeDtypeStruct((t, A_WIDTH), BF16), jax.ShapeDtypeStruct((t, A_WIDTH), F32)],
        compiler_params=_params("parallel", "parallel"),
    )(kstart, qb, ka, va, dob)


def _fox_bwd_dkv(qb, ka, va, dob, qend, name):
    t = qb.shape[0]
    bk = min(ATT_BLOCK, t)
    nk = t // bk

    def body(qe_ref, k_ref, v_ref, q_ref, do_ref, dk_ref, dv_ref, cs_ref):
        jb = pl.program_id(1)
        hp = pl.program_id(0)
        i1 = [qe_ref[(2 * hp + j) * nk + jb] + 1 for j in (0, 1)]
        both1 = jnp.minimum(i1[0], i1[1])

        def head(j, ib, dk_acc, dv_acc, masked):
            rows = pl.ds(pl.multiple_of(ib * bk, bk), bk)
            cols = slice(128 * j, 128 * (j + 1))
            qs = q_ref[rows, cols]
            dos = do_ref[rows, cols]
            st = _dot(k_ref[:, cols], qs, "nt")
            if masked:
                st = jnp.where(_tile_mask(bk, True), st, -jnp.inf)
            pt = jnp.exp(st)
            return dk_acc + _dot(pt * _dot(v_ref[:, cols], dos, "nt"), qs), dv_acc + _dot(pt, dos)

        def pair(ib, carry, masked):
            return head(0, ib, carry[0], carry[1], masked) + head(1, ib, carry[2], carry[3], masked)

        carry = pair(jb, (jnp.zeros((bk, 128), F32),) * 4, True)
        carry = lax.fori_loop(jb + 1, both1, lambda ib, c: pair(ib, c, False), carry)
        alone = [lax.fori_loop(jnp.maximum(both1, jb + 1), i1[j],
                               lambda ib, c, j=j: head(j, ib, c[0], c[1], False), carry[2 * j:2 * j + 2])
                 for j in (0, 1)]
        carry = alone[0] + alone[1]
        outs = []
        for j in (0, 1):
            spare = 64 * (1 - j)
            dk_acc, dv_acc = carry[2 * j], carry[2 * j + 1]
            outs.append((dk_acc, dv_acc, dk_acc[:, spare + 3:spare + 4]))
        msk = _head_mask((bk, 128), 0)
        dk_ref[...] = jnp.where(msk, outs[0][0], outs[1][0]).astype(BF16)
        dv_ref[...] = jnp.where(msk, outs[0][1], outs[1][1]).astype(BF16)
        cs_ref[...] = jnp.where(msk, outs[0][2], outs[1][2])

    blk = pl.BlockSpec((bk, 256), lambda h, i, tbl: (i, h))
    res = pl.BlockSpec((t, 256), lambda h, i, tbl: (0, h))
    out = pl.BlockSpec((bk, 128), lambda h, i, tbl: (i, h))
    return pl.pallas_call(
        body,
        name=name,
        grid_spec=pltpu.PrefetchScalarGridSpec(
            num_scalar_prefetch=1, grid=(4, nk), in_specs=[blk, blk, res, res], out_specs=[out, out, out]),
        out_shape=[jax.ShapeDtypeStruct((t, A_WIDTH), BF16), jax.ShapeDtypeStruct((t, A_WIDTH), BF16),
                   jax.ShapeDtypeStruct((t, A_WIDTH), F32)],
        compiler_params=_params("parallel", "parallel"),
    )(qend, ka, va, qb, dob)


HG_ROWS = 256


def _hg_gates(hb_ref, rows, lbv):
    qb = hb_ref[rows, 0:B_WIDTH]
    fb = hb_ref[rows, B_WIDTH:2 * B_WIDTH]
    v = hb_ref[rows, 2 * B_WIDTH:3 * B_WIDTH]
    gb = hb_ref[rows, 3 * B_WIDTH:4 * B_WIDTH]
    sg = _sigmoid(fb)
    f = lbv + (1.0 - lbv) * sg
    sq = _sigmoid(qb)
    return qb, sq, qb * sq, sg, f, 1.0 - f, jnp.log(f), v, gb


def _hg_intra_factors(q, k, b):
    fac = []
    for i in range(CHUNK // SUB):
        bi = b[SUB * i:SUB * i + 1, :]
        eq = jnp.exp(b[SUB * i:SUB * (i + 1), :] - bi)
        ek = jnp.exp(jnp.minimum(bi - b, EXP_CLAMP))
        fac.append((eq, ek, q[SUB * i:SUB * (i + 1), :] * eq, k * ek))
    return fac


def _causal(n):
    r = lax.broadcasted_iota(jnp.int32, (n, n), 0)
    c = lax.broadcasted_iota(jnp.int32, (n, n), 1)
    return r >= c


def _hgrn_fwd(rest, lb, ng, name):
    t = rest.shape[0]
    bt = min(HG_ROWS, t)
    ncb = bt // CHUNK

    def body(hb_ref, lb_ref, ng_ref, y_ref, o_ref, st_ref, s_scr):
        @pl.when(pl.program_id(0) == 0)
        def _():
            s_scr[...] = jnp.zeros_like(s_scr)

        tril = _tri(CHUNK, True)
        causal = _causal(CHUNK)
        ones = jnp.ones((CHUNK, HD), F32)

        def chunk(c, carry):
            rows = pl.ds(pl.multiple_of(c * CHUNK, CHUNK), CHUNK)
            _, _, q_all, _, _, k_all, g_all, v_all, gb_all = _hg_gates(hb_ref, rows, lb_ref[...])
            b_all = _dot_hi(tril, g_all)
            qd_all = q_all * jnp.exp(b_all)
            kd_all = k_all * jnp.exp(b_all[CHUNK - 1:CHUNK, :] - b_all)
            eb_all = jnp.exp(_dot_hi(g_all, ones, "tn", exact="b"))
            sgb_all = _sigmoid(gb_all)
            for h in range(B_HEADS):
                cols = slice(h * HD, (h + 1) * HD)
                v = v_all[:, cols]
                s0 = s_scr[h]
                st_ref[c, h] = s0
                o = _dot(qd_all[:, cols], s0)
                fac = _hg_intra_factors(q_all[:, cols], k_all[:, cols], b_all[:, cols])
                a = jnp.concatenate([_dot(qe, ke, "nt") for _, _, qe, ke in fac], axis=0)
                o = o + _dot(jnp.where(causal, a, 0.0), v)
                s_scr[h] = eb_all[h * HD:(h + 1) * HD, :] * s0 + _dot(kd_all[:, cols], v, "tn")
                r = lax.rsqrt(jnp.mean(o * o, axis=-1, keepdims=True) + RMS_EPS)
                o_ref[rows, cols] = o
                y_ref[rows, cols] = (o * r * ng_ref[...] * sgb_all[:, cols]).astype(BF16)
            return carry

        lax.fori_loop(0, ncb, chunk, 0)

    return pl.pallas_call(
        body,
        name=name,
        grid=(t // bt,),
        in_specs=[pl.BlockSpec((bt, 4 * B_WIDTH), lambda i: (i, 1)), pl.BlockSpec((1, B_WIDTH), lambda i: (0, 0)),
                  pl.BlockSpec((1, HD), lambda i: (0, 0))],
        out_specs=[pl.BlockSpec((bt, B_WIDTH), lambda i: (i, 0)), pl.BlockSpec((bt, B_WIDTH), lambda i: (i, 0)),
                   pl.BlockSpec((ncb, B_HEADS, HD, HD), lambda i: (i, 0, 0, 0))],
        out_shape=[jax.ShapeDtypeStruct((t, B_WIDTH), BF16), jax.ShapeDtypeStruct((t, B_WIDTH), F32),
                   jax.ShapeDtypeStruct((t // CHUNK, B_HEADS, HD, HD), F32)],
        scratch_shapes=[pltpu.VMEM((B_HEADS, HD, HD), F32)],
        compiler_params=_params("arbitrary"),
    )(rest, lb, ng)


def _hgrn_bwd(dy, rest, o_saved, states, lb, ng, name):
    t = rest.shape[0]
    bt = min(HG_ROWS, t)
    ncb = bt // CHUNK
    nb = t // bt

    def body(dy_ref, hb_ref, o_ref, st_ref, lb_ref, ng_ref, dh_ref, dlb_ref, dng_ref, ds_scr):
        @pl.when(pl.program_id(0) == 0)
        def _():
            ds_scr[...] = jnp.zeros_like(ds_scr)
            dlb_ref[...] = jnp.zeros_like(dlb_ref)
            dng_ref[...] = jnp.zeros_like(dng_ref)

        tril = _tri(CHUNK, True)
        triu = _tri(CHUNK, False)
        causal = _causal(CHUNK)
        ones = jnp.ones((CHUNK, HD), F32)
        ones8 = jnp.ones((8, HD), F32)
        last_row = lax.broadcasted_iota(jnp.int32, (CHUNK, B_WIDTH), 0) == CHUNK - 1

        def chunk(cc, carry):
            dng_acc, dlb_acc = carry
            c = ncb - 1 - cc
            rows = pl.ds(pl.multiple_of(c * CHUNK, CHUNK), CHUNK)
            lbv = lb_ref[...]
            qb, sq, q_all, sg, f, k_all, g_all, v_all, gb = _hg_gates(hb_ref, rows, lbv)
            b_all = _dot_hi(tril, g_all)
            ebt_all = jnp.exp(b_all)
            blast = b_all[CHUNK - 1:CHUNK, :]
            ekd_all = jnp.exp(blast - b_all)
            eb_all = jnp.exp(_dot_hi(g_all, ones, "tn", exact="b"))
            sgb = _sigmoid(gb)
            dy_all = dy_ref[rows, :].astype(F32)
            don_all = dy_all * sgb
            ngv = ng_ref[...]
            dq_l, dk_l, dks_l, dv_l, on_l, prod_l = [], [], [], [], [], []
            for h in range(B_HEADS):
                cols = slice(h * HD, (h + 1) * HD)
                q, k, v = q_all[:, cols], k_all[:, cols], v_all[:, cols]
                o = o_ref[rows, cols]
                don = don_all[:, cols]
                r = lax.rsqrt(jnp.mean(o * o, axis=-1, keepdims=True) + RMS_EPS)
                on_l.append(o * r * ngv)
                dng_acc = dng_acc + jnp.sum(don * o * r, axis=0, keepdims=True)
                doh = don * ngv
                do = r * (doh - o * (r * r) * jnp.mean(doh * o, axis=-1, keepdims=True))
                ebt, ekd = ebt_all[:, cols], ekd_all[:, cols]
                s0 = st_ref[c, h]
                ds1 = ds_scr[h]
                fac = _hg_intra_factors(q, k, b_all[:, cols])
                a = jnp.concatenate([_dot(qe, ke, "nt") for _, _, qe, ke in fac], axis=0)
                a = jnp.where(causal, a, 0.0)
                da = jnp.where(causal, _dot(do, v, "nt"), 0.0)
                dv_l.append(_dot(a, do, "tn") + _dot(k * ekd, ds1))
                dq = ebt * _dot(do, s0, "nt")
                dq_l.append(dq + jnp.concatenate(
                    [eq * _hdot(da[SUB * i:SUB * (i + 1), :], ke) for i, (eq, _, _, ke) in enumerate(fac)], axis=0))
                dk_state = ekd * _dot(v, ds1, "nt")
                dk = dk_state
                for i, (_, ek, qe, _) in enumerate(fac):
                    dk = dk + ek * _hdot(da[SUB * i:SUB * (i + 1), :], qe, "tn")
                dk_l.append(dk)
                dks_l.append(dk_state)
                prod_l.append(ds1 * s0)
                ds_scr[h] = _dot(q * ebt, do, "tn") + eb_all[h * HD:(h + 1) * HD, :] * ds1
            dq_all, dk_all = jnp.concatenate(dq_l, axis=1), jnp.concatenate(dk_l, axis=1)
            extra = jnp.exp(blast) * _dot_hi(ones8, jnp.concatenate(prod_l, axis=0), "nt")[0:1, :] \
                + jnp.sum(k_all * jnp.concatenate(dks_l, axis=1), axis=0, keepdims=True)
            db = q_all * dq_all - k_all * dk_all + jnp.where(last_row, extra, 0.0)
            df = _dot_hi(triu, db) / f - dk_all
            dlb_acc = dlb_acc + jnp.sum(df * (1.0 - sg), axis=0, keepdims=True)
            dh_ref[rows, 0:B_WIDTH] = (dq_all * (sq * (1.0 + qb * (1.0 - sq)))).astype(BF16)
            dh_ref[rows, B_WIDTH:2 * B_WIDTH] = (df * (1.0 - lbv) * sg * (1.0 - sg)).astype(BF16)
            dh_ref[rows, 2 * B_WIDTH:3 * B_WIDTH] = jnp.concatenate(dv_l, axis=1).astype(BF16)
            dh_ref[rows, 3 * B_WIDTH:4 * B_WIDTH] = (dy_all * jnp.concatenate(on_l, axis=1)
                                                     * sgb * (1.0 - sgb)).astype(BF16)
            return dng_acc, dlb_acc

        dng_sum, dlb_sum = lax.fori_loop(0, ncb, chunk, (jnp.zeros((1, HD), F32), jnp.zeros((1, B_WIDTH), F32)))
        dng_ref[...] += dng_sum
        dlb_ref[...] += dlb_sum

    rev = lambda i: (nb - 1 - i, 0)
    return pl.pallas_call(
        body,
        name=name,
        grid=(nb,),
        in_specs=[pl.BlockSpec((bt, B_WIDTH), rev), pl.BlockSpec((bt, 4 * B_WIDTH), lambda i: (nb - 1 - i, 1)),
                  pl.BlockSpec((bt, B_WIDTH), rev),
                  pl.BlockSpec((ncb, B_HEADS, HD, HD), lambda i: (nb - 1 - i, 0, 0, 0)),
                  pl.BlockSpec((1, B_WIDTH), lambda i: (0, 0)), pl.BlockSpec((1, HD), lambda i: (0, 0))],
        out_specs=[pl.BlockSpec((bt, 4 * B_WIDTH), rev), pl.BlockSpec((1, B_WIDTH), lambda i: (0, 0)),
                   pl.BlockSpec((1, HD), lambda i: (0, 0))],
        out_shape=[jax.ShapeDtypeStruct((t, 4 * B_WIDTH), BF16), jax.ShapeDtypeStruct((1, B_WIDTH), F32),
                   jax.ShapeDtypeStruct((1, HD), F32)],
        scratch_shapes=[pltpu.VMEM((B_HEADS, HD, HD), F32)],
        compiler_params=_params("arbitrary"),
    )(dy, rest, o_saved, states, lb, ng)


def _axpy2(c0, a0, c1, a1, name, tm=512):
    t, d = a0.shape
    tm = min(tm, t)

    def body(a_ref, b_ref, o_ref):
        o_ref[...] = c0 * a_ref[...] + c1 * b_ref[...]

    row = lambda i: (i, 0)
    return pl.pallas_call(
        body, name=name, grid=(t // tm,),
        in_specs=[pl.BlockSpec((tm, d), row), pl.BlockSpec((tm, d), row)],
        out_specs=pl.BlockSpec((tm, d), row),
        out_shape=jax.ShapeDtypeStruct((t, d), F32),
        compiler_params=_params("parallel"),
    )(a0, a1)


def _split_w_in(w_in_l):
    wqkv = w_in_l[:, :3 * A_WIDTH]
    wfa = jnp.pad(w_in_l[:, 3 * A_WIDTH:3 * A_WIDTH + A_HEADS], ((0, 0), (0, 128 - A_HEADS)))
    whb = w_in_l[:, 3 * A_WIDTH + A_HEADS:3 * A_WIDTH + A_HEADS + 4 * B_WIDTH]
    wgt = w_in_l[:, 3 * A_WIDTH + A_HEADS + 4 * B_WIDTH:]
    return wqkv, jnp.concatenate([wgt, whb, wfa], axis=1)


def _merge_w_in_grad(dwall):
    o = 3 * A_WIDTH
    return jnp.concatenate([dwall[:, :o], dwall[:, o + 4096:o + 4096 + A_HEADS], dwall[:, o + 2048:o + 4096],
                            dwall[:, o:o + 2048]], axis=1)


def _layer_fwd(x, xb, w, sp, l):
    t = x.shape[0]
    n = f"l{l}_"
    wqkv, wrest = _split_w_in(w["w_in"])
    qkv = _matmul(xb, wqkv, "nn", BF16, MM_ROWS, 768, D_MODEL, n + "proj_qkv")
    rest = _matmul(xb, wrest, "nn", F32, MM_ROWS, 1408, D_MODEL, n + "proj_rest")
    bf = jnp.pad(sp["b_fgate"], (0, 128 - A_HEADS)).reshape(1, 128)
    fcum, fcol = _fox_gate_fwd(rest, bf, n + "fox_gate_fwd")
    qa, ka, va, qn, kn = _fox_prep_fwd(qkv, fcol, n + "fox_prep_fwd")
    kstart, qend = _fox_block_ranges(qn, kn, fcum)
    ya, lse = _fox_fwd(qa, ka, va, kstart, n + "fox_fwd")
    lb = sp["lb"].reshape(1, B_WIDTH)
    ng = sp["norm_g"].reshape(1, HD)
    yb, ob, states = _hgrn_fwd(rest, lb, ng, n + "hgrn_fwd")
    merged = _merge_fwd(ya, yb, w["w_pa"], w["w_pb"], rest, n + "merge_fwd")
    x1, x1b, xh1, rs1 = _mm_res_ln(merged, w["w_out"], x, sp["ln1_g"], sp["ln1_b"], n + "out_ln1")
    wu, wg = w["w_ff_in"][:, :FFN_HIDDEN], w["w_ff_in"][:, FFN_HIDDEN:]
    a, hu, hg = _ffn_in_swiglu(x1b, wu, wg, n + "ffn_in_swiglu")
    x2, x2b, xh2, rs2 = _mm_res_ln(a, w["w_ff_out"], x1, sp["ln2_g"], sp["ln2_b"], n + "ffn_out_ln2")
    saved = dict(xb=xb, wqkv=wqkv, wrest=wrest, qkv=qkv, rest=rest, bf=bf, fcol=fcol, ka=ka, va=va, ya=ya, lse=lse,
                 kstart=kstart, qend=qend,
                 lb=lb, ng=ng, yb=yb, ob=ob, states=states, merged=merged, x1b=x1b, xh1=xh1, rs1=rs1, a=a,
                 wu=wu, wg=wg, hu=hu, hg=hg,
                 xh2=xh2, rs2=rs2)
    return x2, x2b, saved


def _layer_bwd(dys, coefs, w, sp, s, l):
    n = f"l{l}_"
    dz2, dz2b, dg2, db2 = _ln_bwd(dys, coefs, s["xh2"], s["rs2"], sp["ln2_g"], n + "ln2_bwd")
    du, dg = _ffn_out_dx_swiglu(dz2b, w["w_ff_out"], s["hu"], s["hg"], n + "ffn_out_dx_swiglu")
    d_wffout = _matmul(s["a"], dz2b, "tn", F32, 1408, 1024, DW_ROWS, n + "ffn_out_dw")
    dx1u = _matmul(du, s["wu"], "nt", F32, MM_ROWS, 1024, FFN_HIDDEN, n + "ffn_in_dx_u")
    dx1g = _matmul(dg, s["wg"], "nt", F32, MM_ROWS, 1024, FFN_HIDDEN, n + "ffn_in_dx_g")
    d_wffin = jnp.concatenate([_matmul(s["x1b"], du, "tn", F32, 1024, 1408, DW_ROWS, n + "ffn_in_dw_u"),
                               _matmul(s["x1b"], dg, "tn", F32, 1024, 1408, DW_ROWS, n + "ffn_in_dw_g")], axis=1)
    dz1, dz1b, dg1, db1 = _ln_bwd([dz2, dx1u, dx1g], [ALPHA, 1.0, 1.0], s["xh1"], s["rs1"], sp["ln1_g"],
                                  n + "ln1_bwd")
    d_wout = _matmul(s["merged"], dz1b, "tn", F32, 1024, 1024, DW_ROWS, n + "out_dw")
    dgates, dpa, dpb, dya, dyb = _merge_bwd(dz1b, w["w_out"], s["ya"], s["yb"], w["w_pa"], w["w_pb"], s["rest"],
                                  n + "merge_bwd")
    d_wpa = _matmul(s["ya"], dpa, "tn", F32, 512, 1024, DW_ROWS, n + "pa_dw")
    d_wpb = _matmul(s["yb"], dpb, "tn", F32, 512, 1024, DW_ROWS, n + "pb_dw")
    qb, dob = _fox_prep_bwd(s["qkv"], s["fcol"], s["lse"], dya, s["ya"], n + "fox_prep_bwd")
    dq, rsum = _fox_bwd_dq(qb, s["ka"], s["va"], dob, s["kstart"], n + "fox_bwd_dq")
    dk, dv, csum = _fox_bwd_dkv(qb, s["ka"], s["va"], dob, s["qend"], n + "fox_bwd_dkv")
    dfa, dbf = _fox_gate_bwd(rsum, csum, s["rest"], s["bf"], n + "fox_gate_bwd")
    dhb, dlb, dng = _hgrn_bwd(dyb, s["rest"], s["ob"], s["states"], s["lb"], s["ng"], n + "hgrn_bwd")
    dproj = jnp.concatenate([dq, dk, dv, dgates, dhb, dfa], axis=1)
    wall = jnp.concatenate([s["wqkv"], s["wrest"]], axis=1)
    dxp = _matmul(dproj, wall, "nt", F32, MM_ROWS, 1024, 1920, n + "proj_dx")
    d_wall = _matmul(s["xb"], dproj, "tn", F32, 1024, 1152, DW_ROWS, n + "proj_dw")
    grads = dict(w_in=_merge_w_in_grad(d_wall), w_pa=d_wpa, w_pb=d_wpb, w_out=d_wout, w_ff_in=d_wffin,
                 w_ff_out=d_wffout, b_fgate=dbf[0, :A_HEADS], lb=dlb[0], norm_g=dng[0], ln1_g=dg1[0], ln1_b=db1[0],
                 ln2_g=dg2[0], ln2_b=db2[0])
    return [dz1, dxp], [ALPHA, 1.0], grads


def _lower_bounds(logits):
    sm = jax.nn.softmax(logits.astype(F32), axis=0)
    return jnp.cumsum(sm, axis=0) - sm[0:1]


def _local_step(x, target, wfull, small):
    lbs, lb_vjp = jax.vjp(_lower_bounds, small["hgrn_lb_logits"])
    h, hb = x, x.astype(BF16)
    saved, sps = [], []
    for l in range(DEPTH):
        sp = dict(b_fgate=small["b_fgate"][l], lb=lbs[l], norm_g=small["hgrn_norm_g"][l], ln1_g=small["ln1_g"][l],
                  ln1_b=small["ln1_b"][l], ln2_g=small["ln2_g"][l], ln2_b=small["ln2_b"][l])
        h, hb, s = _layer_fwd(h, hb, wfull[l], sp, l)
        saved.append(s)
        sps.append(sp)
    dy, lpart = _loss_head(h, target)
    dys, coefs = [dy], [1.0]
    grads = [None] * DEPTH
    for l in reversed(range(DEPTH)):
        dys, coefs, grads[l] = _layer_bwd(dys, coefs, wfull[l], sps[l], saved[l], l)
    grad_x = _axpy2(coefs[0], dys[0], coefs[1], dys[1], "grad_x")
    d_logits = lb_vjp(jnp.stack([grads[l]["lb"] for l in range(DEPTH)]))[0]
    return lpart[0, 0], grad_x, grads, d_logits


_BIG = [("w_in", "w_in", (D_MODEL, IN_TOTAL), 1), ("w_branch_a", "w_pa", (A_WIDTH, D_MODEL), 1),
        ("w_branch_b", "w_pb", (B_WIDTH, D_MODEL), 1), ("w_out", "w_out", (D_MODEL, D_MODEL), 0),
        ("w_ff_in", "w_ff_in", (D_MODEL, 2 * FFN_HIDDEN), 1), ("w_ff_out", "w_ff_out", (FFN_HIDDEN, D_MODEL), 0)]
_SMALL = [("b_fgate", A_HEADS), ("hgrn_lb_logits", B_WIDTH), ("hgrn_norm_g", HD), ("ln1_g", D_MODEL),
          ("ln1_b", D_MODEL), ("ln2_g", D_MODEL), ("ln2_b", D_MODEL)]
N_BIG = len(_BIG)
SMALL_ROWS = 80


def _by_chip(full, axis):
    if axis == 0:
        return full.reshape(N_CHIPS, full.shape[0] // N_CHIPS, full.shape[1])
    n = full.shape[1] // N_CHIPS
    return jnp.stack([full[:, q * n:(q + 1) * n] for q in range(N_CHIPS)])


def _from_chips(shards, axis):
    if axis == 0:
        return shards.reshape(N_CHIPS * shards.shape[1], shards.shape[2])
    return jnp.concatenate([shards[q] for q in range(N_CHIPS)], axis=1)


def _pack_small(per_name):
    flat = jnp.concatenate([per_name[name].reshape(-1) for name, _ in _SMALL])
    return jnp.pad(flat, (0, SMALL_ROWS * 128 - flat.shape[0])).reshape(SMALL_ROWS, 128)


def _unpack_small(slab):
    flat, out, r = slab.reshape(-1), {}, 0
    for name, n in _SMALL:
        out[name] = flat[r:r + DEPTH * n].reshape(DEPTH, n)
        r += DEPTH * n
    return out


_ANY = pl.BlockSpec(memory_space=pl.ANY)


def _place():
    return lax.axis_index("x"), lax.axis_index("y"), lax.axis_index("c")


def _other_chips(x, y):
    return [(1 - x, y), (x, 1 - y), (1 - x, 1 - y)]


def _chip_exchange(mine_of, out_refs, send_sems, recv_sems, local_sems):
    x, y, c = _place()
    q = 2 * x + y
    started = []
    for w, out_ref in enumerate(out_refs):
        local = pltpu.make_async_copy(mine_of(w, q), out_ref.at[q], local_sems.at[w])
        local.start()
        started.append(local)
    sends = []
    for k, (px, py) in enumerate(_other_chips(x, y)):
        for w, out_ref in enumerate(out_refs):
            cp = pltpu.make_async_remote_copy(src_ref=mine_of(w, 2 * px + py), dst_ref=out_ref.at[q],
                                              send_sem=send_sems.at[3 * w + k], recv_sem=recv_sems.at[3 * w + k],
                                              device_id=(px, py, c), device_id_type=MESH)
            cp.start()
            sends.append(cp)
    for k, (px, py) in enumerate(_other_chips(x, y)):
        for w, out_ref in enumerate(out_refs):
            pltpu.make_async_remote_copy(src_ref=mine_of(w, q), dst_ref=out_ref.at[2 * px + py],
                                         send_sem=send_sems.at[3 * w + k], recv_sem=recv_sems.at[3 * w + k],
                                         device_id=(px, py, c), device_id_type=MESH).wait_recv()
    for cp in sends:
        cp.wait_send()
    for local in started:
        local.wait()


def _sem_scratch(n):
    return [pltpu.SemaphoreType.DMA((3 * n,)), pltpu.SemaphoreType.DMA((3 * n,)), pltpu.SemaphoreType.DMA((n,))]


def _gather_weights(mine):
    n = len(mine)

    def body(*refs):
        in_refs, out_refs = refs[:n], refs[n:2 * n]
        send_sems, recv_sems, local_sems, pair_send, pair_recv = refs[2 * n:]
        x, y, c = _place()
        _chip_exchange(lambda w, q: in_refs[w].at[c], [o.at[c] for o in out_refs], send_sems, recv_sems, local_sems)
        sibling = (x, y, 1 - c)
        fwds = []
        for w, o in enumerate(out_refs):
            cp = pltpu.make_async_remote_copy(src_ref=o.at[c], dst_ref=o.at[c], send_sem=pair_send.at[w],
                                              recv_sem=pair_recv.at[w], device_id=sibling, device_id_type=MESH)
            cp.start()
            fwds.append(cp)
        for w, o in enumerate(out_refs):
            pltpu.make_async_remote_copy(src_ref=o.at[1 - c], dst_ref=o.at[1 - c], send_sem=pair_send.at[w],
                                         recv_sem=pair_recv.at[w], device_id=sibling, device_id_type=MESH).wait_recv()
        for cp in fwds:
            cp.wait_send()

    return pl.pallas_call(
        body, name="gather_weights", in_specs=[_ANY] * n, out_specs=[_ANY] * n,
        out_shape=[jax.ShapeDtypeStruct((DEPTH, N_CHIPS) + m.shape[1:], m.dtype) for m in mine],
        scratch_shapes=_sem_scratch(n) + [pltpu.SemaphoreType.DMA((n,)), pltpu.SemaphoreType.DMA((n,))],
    )(*mine)


def _pair_exchange(gs):
    n = len(gs)

    def body(*refs):
        g_refs, a_refs, send_sems, recv_sems = refs[:n], refs[n:2 * n], refs[2 * n], refs[2 * n + 1]
        x, y, c = _place()
        cps = []
        for w in range(n):
            cp = pltpu.make_async_remote_copy(src_ref=g_refs[w].at[1 - c], dst_ref=a_refs[w], send_sem=send_sems.at[w],
                                              recv_sem=recv_sems.at[w], device_id=(x, y, 1 - c), device_id_type=MESH)
            cp.start()
            cps.append(cp)
        for cp in cps:
            cp.wait()

    return pl.pallas_call(
        body, name="grad_pair_exchange", in_specs=[_ANY] * n, out_specs=[_ANY] * n,
        out_shape=[jax.ShapeDtypeStruct(g.shape[1:], g.dtype) for g in gs],
        scratch_shapes=[pltpu.SemaphoreType.DMA((n,)), pltpu.SemaphoreType.DMA((n,))],
    )(*gs)


def _row_block(rows):
    return math.gcd(rows, 256)


def _pair_sum(g, a, layer, name):
    _, nq, rows, cols = g.shape
    tb = _row_block(rows)

    def body(l_ref, g_ref, a_ref, o_ref):
        o_ref[...] = (g_ref[...] + a_ref[...]).astype(BF16)

    return pl.pallas_call(
        body, name=name,
        grid_spec=pltpu.PrefetchScalarGridSpec(
            num_scalar_prefetch=1, grid=(nq, rows // tb),
            in_specs=[pl.BlockSpec((None, None, tb, cols), lambda q, i, l_ref: (l_ref[0], q, i, 0)),
                      pl.BlockSpec((None, tb, cols), lambda q, i, l_ref: (q, i, 0))],
            out_specs=pl.BlockSpec((None, tb, cols), lambda q, i, l_ref: (q, i, 0))),
        out_shape=jax.ShapeDtypeStruct((nq, rows, cols), BF16),
        compiler_params=_params("parallel", "parallel"),
    )(layer.reshape(1).astype(jnp.int32), g, a)


def _shard_exchange(ps):
    n = len(ps)

    def body(*refs):
        p_refs, b_refs = refs[:n], refs[n:2 * n]
        send_sems, recv_sems, local_sems = refs[2 * n:]
        _chip_exchange(lambda w, q: p_refs[w].at[q], b_refs, send_sems, recv_sems, local_sems)

    return pl.pallas_call(
        body, name="grad_shard_exchange", in_specs=[_ANY] * n, out_specs=[_ANY] * n,
        out_shape=[jax.ShapeDtypeStruct(p.shape, p.dtype) for p in ps],
        scratch_shapes=_sem_scratch(n),
    )(*ps)


def _sum4(b, name):
    _, rows, cols = b.shape
    tb = _row_block(rows)

    def body(b_ref, o_ref):
        o_ref[...] = ((b_ref[0].astype(F32) + b_ref[1].astype(F32)) + b_ref[2].astype(F32)) + b_ref[3].astype(F32)

    return pl.pallas_call(
        body, name=name, grid=(rows // tb,),
        in_specs=[pl.BlockSpec((N_CHIPS, tb, cols), lambda i: (0, i, 0))],
        out_specs=pl.BlockSpec((tb, cols), lambda i: (i, 0)),
        out_shape=jax.ShapeDtypeStruct((rows, cols), F32),
        compiler_params=_params("parallel"),
    )(b)


def _result_exchange(gcs):
    n = len(gcs)

    def body(*refs):
        g_refs, o_refs, send_sems, recv_sems = refs[:n], refs[n:2 * n], refs[2 * n], refs[2 * n + 1]
        x, y, c = _place()
        cps = []
        for w in range(n):
            cp = pltpu.make_async_remote_copy(src_ref=g_refs[w], dst_ref=o_refs[w], send_sem=send_sems.at[w],
                                              recv_sem=recv_sems.at[w], device_id=(x, y, 1 - c), device_id_type=MESH)
            cp.start()
            cps.append(cp)
        for cp in cps:
            cp.wait()

    return pl.pallas_call(
        body, name="grad_result_exchange", in_specs=[_ANY] * n, out_specs=[_ANY] * n,
        out_shape=[jax.ShapeDtypeStruct(g.shape, g.dtype) for g in gcs],
        scratch_shapes=[pltpu.SemaphoreType.DMA((n,)), pltpu.SemaphoreType.DMA((n,))],
    )(*gcs)


def _allreduce_small(v):
    def body(v_ref, o_ref, buf, send_sems, recv_sems):
        x, y, c = _place()
        me = 4 * x + 2 * y + c
        buf[me] = v_ref[...]
        peers = []
        for k in range(1, N_DEV):
            px = 1 - x if k & 4 else x
            py = 1 - y if k & 2 else y
            pc = 1 - c if k & 1 else c
            peers.append((px, py, pc))
        sends = []
        for k, peer in enumerate(peers):
            cp = pltpu.make_async_remote_copy(src_ref=v_ref, dst_ref=buf.at[me], send_sem=send_sems.at[k],
                                              recv_sem=recv_sems.at[k], device_id=peer, device_id_type=MESH)
            cp.start()
            sends.append(cp)
        for k, (px, py, pc) in enumerate(peers):
            pltpu.make_async_remote_copy(src_ref=v_ref, dst_ref=buf.at[4 * px + 2 * py + pc], send_sem=send_sems.at[k],
                                         recv_sem=recv_sems.at[k], device_id=(px, py, pc),
                                         device_id_type=MESH).wait_recv()
        for cp in sends:
            cp.wait_send()
        acc = buf[0]
        for i in range(1, N_DEV):
            acc = acc + buf[i]
        o_ref[...] = acc

    vm = pl.BlockSpec(memory_space=pltpu.VMEM)
    return pl.pallas_call(
        body, name="small_allreduce", in_specs=[vm], out_specs=vm,
        out_shape=jax.ShapeDtypeStruct(v.shape, F32),
        scratch_shapes=[pltpu.VMEM((N_DEV,) + v.shape, F32), pltpu.SemaphoreType.DMA((N_DEV - 1,)),
                        pltpu.SemaphoreType.DMA((N_DEV - 1,))],
    )(v)


def _adam_update(w, g, m, v):
    nm = ADAM_B1 * m + (1.0 - ADAM_B1) * g
    nv = ADAM_B2 * v + (1.0 - ADAM_B2) * (g * g)
    m_hat = nm / (1.0 - ADAM_B1 ** ADAM_STEP)
    v_hat = nv / (1.0 - ADAM_B2 ** ADAM_STEP)
    return -ADAM_LR * (m_hat / (jnp.sqrt(v_hat) + ADAM_EPS) + ADAM_WD * w), nm, nv


def _adamw_small(w, g, m, v, name):
    def body(w_ref, g_ref, m_ref, v_ref, d_ref, nm_ref, nv_ref):
        d_ref[...], nm_ref[...], nv_ref[...] = _adam_update(w_ref[...], g_ref[...], m_ref[...], v_ref[...])

    vm = pl.BlockSpec(memory_space=pltpu.VMEM)
    return pl.pallas_call(
        body, name=name, in_specs=[vm] * 4, out_specs=[vm] * 3,
        out_shape=[jax.ShapeDtypeStruct(w.shape, F32)] * 3,
    )(w, g, m, v)


def _adamw_big(w, m, v, g_own, g_other, layer, name):
    _, rows, cols = w.shape
    tb = _row_block(rows)

    def body(l_ref, w_ref, m_ref, v_ref, go_ref, gx_ref, g_ref, d_ref, nm_ref, nv_ref):
        gv = jnp.where(pl.program_id(0) == l_ref[0], go_ref[...], gx_ref[...])
        g_ref[...] = gv
        d_ref[...], nm_ref[...], nv_ref[...] = _adam_update(w_ref[...], gv, m_ref[...], v_ref[...])

    per_layer = pl.BlockSpec((None, tb, cols), lambda l, i, l_ref: (l, i, 0))
    shared = pl.BlockSpec((tb, cols), lambda l, i, l_ref: (i, 0))
    return pl.pallas_call(
        body, name=name,
        grid_spec=pltpu.PrefetchScalarGridSpec(
            num_scalar_prefetch=1, grid=(DEPTH, rows // tb),
            in_specs=[per_layer, per_layer, per_layer, shared, shared], out_specs=[per_layer] * 4),
        out_shape=[jax.ShapeDtypeStruct(w.shape, F32)] * 4,
        compiler_params=_params("parallel", "parallel"),
    )(layer.reshape(1).astype(jnp.int32), w, m, v, g_own, g_other)


def kernel(x, w_in, b_fgate, hgrn_lb_logits, hgrn_norm_g, w_branch_a, w_branch_b, w_out, ln1_g, ln1_b, w_ff_in, w_ff_out, ln2_g, ln2_b, loss_target, m_w_in, m_b_fgate, m_hgrn_lb_logits, m_hgrn_norm_g, m_w_branch_a, m_w_branch_b, m_w_out, m_ln1_g, m_ln1_b, m_w_ff_in, m_w_ff_out, m_ln2_g, m_ln2_b, v_w_in, v_b_fgate, v_hgrn_lb_logits, v_hgrn_norm_g, v_w_branch_a, v_w_branch_b, v_w_out, v_ln1_g, v_ln1_b, v_w_ff_in, v_w_ff_out, v_ln2_g, v_ln2_b):
    weights = dict(w_in=w_in, b_fgate=b_fgate, hgrn_lb_logits=hgrn_lb_logits, hgrn_norm_g=hgrn_norm_g,
                   w_branch_a=w_branch_a, w_branch_b=w_branch_b, w_out=w_out, ln1_g=ln1_g, ln1_b=ln1_b,
                   w_ff_in=w_ff_in, w_ff_out=w_ff_out, ln2_g=ln2_g, ln2_b=ln2_b)
    mom1 = dict(w_in=m_w_in, b_fgate=m_b_fgate, hgrn_lb_logits=m_hgrn_lb_logits, hgrn_norm_g=m_hgrn_norm_g,
                w_branch_a=m_w_branch_a, w_branch_b=m_w_branch_b, w_out=m_w_out, ln1_g=m_ln1_g, ln1_b=m_ln1_b,
                w_ff_in=m_w_ff_in, w_ff_out=m_w_ff_out, ln2_g=m_ln2_g, ln2_b=m_ln2_b)
    mom2 = dict(w_in=v_w_in, b_fgate=v_b_fgate, hgrn_lb_logits=v_hgrn_lb_logits, hgrn_norm_g=v_hgrn_norm_g,
                w_branch_a=v_w_branch_a, w_branch_b=v_w_branch_b, w_out=v_w_out, ln1_g=v_ln1_g, ln1_b=v_ln1_b,
                w_ff_in=v_w_ff_in, w_ff_out=v_w_ff_out, ln2_g=v_ln2_g, ln2_b=v_ln2_b)
    core = lax.axis_index("c")

    gathered = _gather_weights([weights[name].astype(BF16) for name, _, _, _ in _BIG])
    wfull = [{key: _from_chips(gathered[w][l], axis) for w, (_, key, _, axis) in enumerate(_BIG)}
             for l in range(DEPTH)]
    small = {name: weights[name] for name, _ in _SMALL}

    loss_part, grad_x, grads, d_logits = _local_step(x[0], loss_target[0], wfull, small)

    g_all = [jnp.stack([_by_chip(grads[l][key], axis) for l in range(DEPTH)]) for _, key, _, axis in _BIG]
    received = _pair_exchange(g_all)
    pair = [_pair_sum(g_all[w], received[w], core, f"grad_pair_sum_{w}") for w in range(N_BIG)]
    by_chip = _shard_exchange(pair)
    g_layer = [_sum4(by_chip[w], f"grad_chip_sum_{w}") for w in range(N_BIG)]
    g_other = _result_exchange(g_layer)
    out_g, out_d, out_m, out_v = {}, {}, {}, {}
    for w, (name, _, _, _) in enumerate(_BIG):
        out_g[name], out_d[name], out_m[name], out_v[name] = _adamw_big(
            weights[name], mom1[name], mom2[name], g_layer[w], g_other[w], core, f"adamw_{name}")

    small_grads = {name: jnp.stack([grads[l][key] for l in range(DEPTH)])
                   for name, key in [("b_fgate", "b_fgate"), ("hgrn_norm_g", "norm_g"), ("ln1_g", "ln1_g"),
                                     ("ln1_b", "ln1_b"), ("ln2_g", "ln2_g"), ("ln2_b", "ln2_b")]}
    small_grads["hgrn_lb_logits"] = d_logits
    gs = _allreduce_small(_pack_small(small_grads))
    ds, ms, vs = _adamw_small(_pack_small(small), gs, _pack_small({n: mom1[n] for n, _ in _SMALL}),
                              _pack_small({n: mom2[n] for n, _ in _SMALL}), "adamw_small")
    for tree, slab in ((out_g, gs), (out_d, ds), (out_m, ms), (out_v, vs)):
        tree.update(_unpack_small(slab))

    loss = lax.psum(loss_part, ("x", "y", "c"))
    order = ["w_in", "b_fgate", "hgrn_lb_logits", "hgrn_norm_g", "w_branch_a", "w_branch_b", "w_out", "ln1_g", "ln1_b",
             "w_ff_in", "w_ff_out", "ln2_g", "ln2_b"]
    return (loss, grad_x[None], *[out_g[n] for n in order], *[out_d[n] for n in order],
            *[out_m[n] for n in order], *[out_v[n] for n in order])
```

```python
import functools
import math

import jax
import jax.numpy as jnp
import numpy as np
from jax import lax
from jax.experimental import pallas as pl
from jax.experimental.pallas import tpu as pltpu

F32 = jnp.float32
BF16 = jnp.bfloat16

D_MODEL = 1024
DEPTH = 2
A_HEADS = 8
A_WIDTH = 512
B_WIDTH = 512
B_HEADS = 4
HD = 128
CHUNK = 64
SUB = 16
FFN_HIDDEN = 2816
IN_TOTAL = 5640
ALPHA = (2 * DEPTH) ** 0.25
LN_EPS = 1e-5
RMS_EPS = 1e-6
ADAM_LR = 0.001
ADAM_B1 = 0.9
ADAM_B2 = 0.999
ADAM_EPS = 1e-08
ADAM_WD = 0.01
ADAM_STEP = 10
EXP_CLAMP = 60.0

VMEM_LIMIT_BYTES = 56 * 1024 * 1024
MM_ROWS = 1024
DW_ROWS = 2048
N_CHIPS = 4
N_DEV = 8
MESH = pl.DeviceIdType.MESH

_DN = {
    "nn": (((1,), (0,)), ((), ())),
    "nt": (((1,), (1,)), ((), ())),
    "tn": (((0,), (0,)), ((), ())),
}


def _dot(a, b, mode="nn"):
    return lax.dot_general(a.astype(BF16), b.astype(BF16), _DN[mode], preferred_element_type=F32)


def _pieces(x):
    h = x.astype(BF16)
    r = x - h.astype(F32)
    m = r.astype(BF16)
    return h, m, (r - m.astype(F32)).astype(BF16)


def _dot_hi(a, b, mode="nn", exact="a"):
    if exact == "a":
        h, m, l = _pieces(b)
        return (_dot(a, l, mode) + _dot(a, m, mode)) + _dot(a, h, mode)
    h, m, l = _pieces(a)
    return (_dot(l, b, mode) + _dot(m, b, mode)) + _dot(h, b, mode)


def _hdot(a, b, mode="nn"):
    bh, bl, _ = _pieces(b)
    return _dot(a, bl, mode) + _dot(a, bh, mode)


def _params(*sem):
    return pltpu.CompilerParams(dimension_semantics=sem, vmem_limit_bytes=VMEM_LIMIT_BYTES)


def _sigmoid(x):
    return 1.0 / (1.0 + jnp.exp(-x))


def _matmul(a, b, mode, out_dtype, tm, tn, tk, name):
    if mode == "nn":
        (m, k), (k2, n) = a.shape, b.shape
    elif mode == "nt":
        (m, k), (n, k2) = a.shape, b.shape
    else:
        (k, m), (k2, n) = a.shape, b.shape
    assert k == k2, (a.shape, b.shape, mode)
    tm, tn, tk = min(tm, m), min(tn, n), min(tk, k)
    assert m % tm == 0 and n % tn == 0 and k % tk == 0, (a.shape, b.shape, tm, tn, tk)
    nk = k // tk
    if mode == "tn":
        a_spec = pl.BlockSpec((tk, tm), lambda j, i, kk: (kk, i))
    else:
        a_spec = pl.BlockSpec((tm, tk), lambda j, i, kk: (i, kk))
    if mode == "nt":
        b_spec = pl.BlockSpec((tn, tk), lambda j, i, kk: (j, kk))
    else:
        b_spec = pl.BlockSpec((tk, tn), lambda j, i, kk: (kk, j))
    use_acc = nk > 1 and out_dtype != F32

    def body(a_ref, b_ref, o_ref, *scratch):
        p = _dot(a_ref[...], b_ref[...], mode)
        if nk == 1:
            o_ref[...] = p.astype(out_dtype)
            return
        acc_ref = scratch[0] if use_acc else o_ref
        kk = pl.program_id(2)

        @pl.when(kk == 0)
        def _():
            acc_ref[...] = p

        @pl.when(kk > 0)
        def _():
            acc_ref[...] += p

        if use_acc:
            @pl.when(kk == nk - 1)
            def _():
                o_ref[...] = acc_ref[...].astype(out_dtype)

    return pl.pallas_call(
        body,
        name=name,
        grid=(n // tn, m // tm, nk),
        in_specs=[a_spec, b_spec],
        out_specs=pl.BlockSpec((tm, tn), lambda j, i, kk: (i, j)),
        out_shape=jax.ShapeDtypeStruct((m, n), out_dtype),
        scratch_shapes=[pltpu.VMEM((tm, tn), F32)] if use_acc else [],
        compiler_params=_params("parallel", "parallel", "arbitrary"),
    )(a, b)


def _mm_res_ln(a, w, res, g, b, name, tm=512):
    t, k = a.shape
    d = w.shape[1]
    tm = min(tm, t)

    def body(a_ref, w_ref, r_ref, g_ref, b_ref, y_ref, yb_ref, xh_ref, rs_ref):
        z = ALPHA * r_ref[...] + _dot(a_ref[...], w_ref[...])
        mu = jnp.mean(z, axis=-1, keepdims=True)
        zc = z - mu
        var = jnp.mean(zc * zc, axis=-1, keepdims=True)
        rstd = lax.rsqrt(var + LN_EPS)
        xh = zc * rstd
        y = xh * g_ref[...] + b_ref[...]
        y_ref[...] = y
        yb_ref[...] = y.astype(BF16)
        xh_ref[...] = xh
        rs_ref[...] = rstd

    row = lambda i: (i, 0)
    fix = lambda i: (0, 0)
    return pl.pallas_call(
        body,
        name=name,
        grid=(t // tm,),
        in_specs=[pl.BlockSpec((tm, k), row), pl.BlockSpec((k, d), fix), pl.BlockSpec((tm, d), row),
                  pl.BlockSpec((1, d), fix), pl.BlockSpec((1, d), fix)],
        out_specs=[pl.BlockSpec((tm, d), row), pl.BlockSpec((tm, d), row), pl.BlockSpec((tm, d), row),
                   pl.BlockSpec((tm, 1), row)],
        out_shape=[jax.ShapeDtypeStruct((t, d), F32), jax.ShapeDtypeStruct((t, d), BF16),
                   jax.ShapeDtypeStruct((t, d), F32), jax.ShapeDtypeStruct((t, 1), F32)],
        compiler_params=_params("parallel"),
    )(a, w, res, g.reshape(1, d), b.reshape(1, d))


def _ln_bwd(dys, coefs, xhat, rstd, g, name, tm=512):
    t, d = xhat.shape
    tm = min(tm, t)
    n_in = len(dys)

    def body(*refs):
        dy_refs = refs[:n_in]
        xh_ref, rs_ref, g_ref, dz_ref, dzb_ref, dg_ref, db_ref = refs[n_in:]
        dy = coefs[0] * dy_refs[0][...].astype(F32)
        for c, r in zip(coefs[1:], dy_refs[1:]):
            dy = dy + c * r[...].astype(F32)
        xh = xh_ref[...]
        dxh = dy * g_ref[...]
        m1 = jnp.mean(dxh, axis=-1, keepdims=True)
        m2 = jnp.mean(dxh * xh, axis=-1, keepdims=True)
        dz = rs_ref[...] * (dxh - m1 - xh * m2)
        dz_ref[...] = dz
        dzb_ref[...] = dz.astype(BF16)
        pg = jnp.sum(dy * xh, axis=0, keepdims=True)
        pb = jnp.sum(dy, axis=0, keepdims=True)

        @pl.when(pl.program_id(0) == 0)
        def _():
            dg_ref[...] = pg
            db_ref[...] = pb

        @pl.when(pl.program_id(0) > 0)
        def _():
            dg_ref[...] += pg
            db_ref[...] += pb

    row = lambda i: (i, 0)
    fix = lambda i: (0, 0)
    return pl.pallas_call(
        body,
        name=name,
        grid=(t // tm,),
        in_specs=[pl.BlockSpec((tm, d), row)] * n_in
        + [pl.BlockSpec((tm, d), row), pl.BlockSpec((tm, 1), row), pl.BlockSpec((1, d), fix)],
        out_specs=[pl.BlockSpec((tm, d), row), pl.BlockSpec((tm, d), row), pl.BlockSpec((1, d), fix),
                   pl.BlockSpec((1, d), fix)],
        out_shape=[jax.ShapeDtypeStruct((t, d), F32), jax.ShapeDtypeStruct((t, d), BF16),
                   jax.ShapeDtypeStruct((1, d), F32), jax.ShapeDtypeStruct((1, d), F32)],
        compiler_params=_params("arbitrary"),
    )(*dys, xhat, rstd, g.reshape(1, d))


def _loss_head(y, target, name="loss_head", tm=512):
    t, d = y.shape
    tm = min(tm, t)

    def body(y_ref, t_ref, dy_ref, l_ref):
        e = y_ref[...] - t_ref[...]
        dy_ref[...] = e * (1.0 / d)
        part = jnp.full((8, 128), 0.5 / d, F32) * jnp.sum(e * e)

        @pl.when(pl.program_id(0) == 0)
        def _():
            l_ref[...] = part

        @pl.when(pl.program_id(0) > 0)
        def _():
            l_ref[...] += part

    row = lambda i: (i, 0)
    return pl.pallas_call(
        body,
        name=name,
        grid=(t // tm,),
        in_specs=[pl.BlockSpec((tm, d), row), pl.BlockSpec((tm, d), row)],
        out_specs=[pl.BlockSpec((tm, d), row), pl.BlockSpec((8, 128), lambda i: (0, 0))],
        out_shape=[jax.ShapeDtypeStruct((t, d), F32), jax.ShapeDtypeStruct((8, 128), F32)],
        compiler_params=_params("arbitrary"),
    )(y, target)


FFN_COLS = FFN_HIDDEN // 2


def _ffn_in_swiglu(xb, wu, wg, name, tm=512):
    t, d = xb.shape
    tm = min(tm, t)

    def body(x_ref, wu_ref, wg_ref, a_ref, u_ref, g_ref):
        x = x_ref[...]
        u = _dot(x, wu_ref[...])
        g = _dot(x, wg_ref[...])
        u_ref[...] = u
        g_ref[...] = g
        a_ref[...] = (g * _sigmoid(g) * u).astype(BF16)

    wspec = pl.BlockSpec((d, FFN_COLS), lambda j, i: (0, j))
    out = pl.BlockSpec((tm, FFN_COLS), lambda j, i: (i, j))
    return pl.pallas_call(
        body,
        name=name,
        grid=(FFN_HIDDEN // FFN_COLS, t // tm),
        in_specs=[pl.BlockSpec((tm, d), lambda j, i: (i, 0)), wspec, wspec],
        out_specs=[out, out, out],
        out_shape=[jax.ShapeDtypeStruct((t, FFN_HIDDEN), BF16), jax.ShapeDtypeStruct((t, FFN_HIDDEN), F32),
                   jax.ShapeDtypeStruct((t, FFN_HIDDEN), F32)],
        compiler_params=_params("parallel", "parallel"),
    )(xb, wu, wg)


def _ffn_out_dx_swiglu(dzb, w_ff_out, u, g, name, tm=512):
    t, d = dzb.shape
    tm = min(tm, t)

    def body(dz_ref, w_ref, u_ref, g_ref, du_ref, dg_ref):
        da = _dot(dz_ref[...], w_ref[...], "nt")
        gv = g_ref[...]
        sg = _sigmoid(gv)
        du_ref[...] = (da * gv * sg).astype(BF16)
        dg_ref[...] = (da * u_ref[...] * (sg * (1.0 + gv * (1.0 - sg)))).astype(BF16)

    blk = pl.BlockSpec((tm, FFN_COLS), lambda j, i: (i, j))
    return pl.pallas_call(
        body,
        name=name,
        grid=(FFN_HIDDEN // FFN_COLS, t // tm),
        in_specs=[pl.BlockSpec((tm, d), lambda j, i: (i, 0)), pl.BlockSpec((FFN_COLS, d), lambda j, i: (j, 0)), blk, blk],
        out_specs=[blk, blk],
        out_shape=[jax.ShapeDtypeStruct((t, FFN_HIDDEN), BF16)] * 2,
        compiler_params=_params("parallel", "parallel"),
    )(dzb, w_ff_out, u, g)


def _merge_fwd(ya, yb, wpa, wpb, rest, name, tm=512):
    t = ya.shape[0]
    tm = min(tm, t)

    def body(ya_ref, yb_ref, wa_ref, wb_ref, ga_ref, gb_ref, o_ref):
        pa = _dot(ya_ref[...], wa_ref[...])
        pb = _dot(yb_ref[...], wb_ref[...])
        o_ref[...] = (_sigmoid(ga_ref[...]) * pa + _sigmoid(gb_ref[...]) * pb).astype(BF16)

    row = lambda i: (i, 0)
    fix = lambda i: (0, 0)
    return pl.pallas_call(
        body,
        name=name,
        grid=(t // tm,),
        in_specs=[pl.BlockSpec((tm, A_WIDTH), row), pl.BlockSpec((tm, B_WIDTH), row),
                  pl.BlockSpec((A_WIDTH, D_MODEL), fix), pl.BlockSpec((B_WIDTH, D_MODEL), fix),
                  pl.BlockSpec((tm, D_MODEL), lambda i: (i, 0)), pl.BlockSpec((tm, D_MODEL), lambda i: (i, 1))],
        out_specs=pl.BlockSpec((tm, D_MODEL), row),
        out_shape=jax.ShapeDtypeStruct((t, D_MODEL), BF16),
        compiler_params=_params("parallel"),
    )(ya, yb, wpa, wpb, rest, rest)


def _merge_bwd(dzb, w_out, ya, yb, wpa, wpb, rest, name, tm=512):
    t = ya.shape[0]
    tm = min(tm, t)

    def body(dz_ref, wo_ref, ya_ref, yb_ref, wa_ref, wb_ref, ga_ref, gb_ref, dg_ref, dpa_ref, dpb_ref, dya_ref,
             dyb_ref):
        dm_v = _dot(dz_ref[...], wo_ref[...], "nt")
        pa = _dot(ya_ref[...], wa_ref[...])
        pb = _dot(yb_ref[...], wb_ref[...])
        sa = _sigmoid(ga_ref[...])
        sb = _sigmoid(gb_ref[...])
        dg_ref[:, :D_MODEL] = (dm_v * pa * sa * (1.0 - sa)).astype(BF16)
        dg_ref[:, D_MODEL:] = (dm_v * pb * sb * (1.0 - sb)).astype(BF16)
        dpa = (dm_v * sa).astype(BF16)
        dpb = (dm_v * sb).astype(BF16)
        dpa_ref[...] = dpa
        dpb_ref[...] = dpb
        dya_ref[...] = _dot(dpa, wa_ref[...], "nt").astype(BF16)
        dyb_ref[...] = _dot(dpb, wb_ref[...], "nt")

    row = lambda i: (i, 0)
    fix = lambda i: (0, 0)
    return pl.pallas_call(
        body,
        name=name,
        grid=(t // tm,),
        in_specs=[pl.BlockSpec((tm, D_MODEL), row), pl.BlockSpec((D_MODEL, D_MODEL), fix),
                  pl.BlockSpec((tm, A_WIDTH), row), pl.BlockSpec((tm, B_WIDTH), row),
                  pl.BlockSpec((A_WIDTH, D_MODEL), fix), pl.BlockSpec((B_WIDTH, D_MODEL), fix),
                  pl.BlockSpec((tm, D_MODEL), lambda i: (i, 0)), pl.BlockSpec((tm, D_MODEL), lambda i: (i, 1))],
        out_specs=[pl.BlockSpec((tm, 2 * D_MODEL), row), pl.BlockSpec((tm, D_MODEL), row),
                   pl.BlockSpec((tm, D_MODEL), row), pl.BlockSpec((tm, A_WIDTH), row), pl.BlockSpec((tm, B_WIDTH), row)],
        out_shape=[jax.ShapeDtypeStruct((t, 2 * D_MODEL), BF16), jax.ShapeDtypeStruct((t, D_MODEL), BF16),
                   jax.ShapeDtypeStruct((t, D_MODEL), BF16), jax.ShapeDtypeStruct((t, A_WIDTH), BF16),
                   jax.ShapeDtypeStruct((t, B_WIDTH), F32)],
        compiler_params=_params("parallel"),
    )(dzb, w_out, ya, yb, wpa, wpb, rest, rest)


FA_BLOCK = 4224 // 128 - 1


def _tri(n, lower):
    r = lax.broadcasted_iota(jnp.int32, (n, n), 0)
    c = lax.broadcasted_iota(jnp.int32, (n, n), 1)
    return jnp.where((r >= c) if lower else (r <= c), 1.0, 0.0).astype(F32)


def _head_spread(expand):
    shape = (128, A_WIDTH) if expand else (A_WIDTH, 128)
    r = lax.broadcasted_iota(jnp.int32, shape, 0)
    c = lax.broadcasted_iota(jnp.int32, shape, 1)
    hit = ((c >= 64 * r) & (c < 64 * r + 64)) if expand else (r == 64 * c)
    return jnp.where(hit, 1.0, 0.0).astype(F32)


def _fox_gate_fwd(rest, bf, name, tb=512):
    t = rest.shape[0]
    tb = min(tb, t)

    def body(fa_ref, bf_ref, f_ref, fc_ref, carry):
        @pl.when(pl.program_id(0) == 0)
        def _():
            carry[...] = jnp.zeros_like(carry)

        z = fa_ref[...] + bf_ref[...]
        logf = jnp.minimum(z, 0.0) - jnp.log(1.0 + jnp.exp(-jnp.abs(z)))
        f = _dot_hi(_tri(tb, True), logf) + carry[...]
        f_ref[...] = f
        fc_ref[...] = _dot_hi(f, _head_spread(True), exact="b")
        carry[...] = f[tb - 1:tb, :]

    return pl.pallas_call(
        body,
        name=name,
        grid=(t // tb,),
        in_specs=[pl.BlockSpec((tb, 128), lambda i: (i, FA_BLOCK)), pl.BlockSpec((1, 128), lambda i: (0, 0))],
        out_specs=[pl.BlockSpec((tb, 128), lambda i: (i, 0)), pl.BlockSpec((tb, A_WIDTH), lambda i: (i, 0))],
        out_shape=[jax.ShapeDtypeStruct((t, 128), F32), jax.ShapeDtypeStruct((t, A_WIDTH), F32)],
        scratch_shapes=[pltpu.VMEM((1, 128), F32)],
        compiler_params=_params("arbitrary"),
    )(rest, bf)


def _fox_gate_bwd(rsum, csum, rest, bf, name, tb=512):
    t = rest.shape[0]
    tb = min(tb, t)
    nb = t // tb

    def body(rs_ref, cs_ref, fa_ref, bf_ref, dfa_ref, dbf_ref, carry):
        @pl.when(pl.program_id(0) == 0)
        def _():
            carry[...] = jnp.zeros_like(carry)

        d_f = _dot_hi(rs_ref[...] - cs_ref[...], _head_spread(False), exact="b")
        dlogf = _dot_hi(_tri(tb, False), d_f) + carry[...]
        carry[...] = dlogf[0:1, :]
        z = fa_ref[...] + bf_ref[...]
        dz = dlogf * _sigmoid(-z)
        dfa_ref[...] = dz.astype(BF16)
        part = jnp.sum(dz, axis=0, keepdims=True)

        @pl.when(pl.program_id(0) == 0)
        def _():
            dbf_ref[...] = part

        @pl.when(pl.program_id(0) > 0)
        def _():
            dbf_ref[...] += part

    return pl.pallas_call(
        body,
        name=name,
        grid=(nb,),
        in_specs=[pl.BlockSpec((tb, A_WIDTH), lambda i: (nb - 1 - i, 0)),
                  pl.BlockSpec((tb, A_WIDTH), lambda i: (nb - 1 - i, 0)),
                  pl.BlockSpec((tb, 128), lambda i: (nb - 1 - i, FA_BLOCK)),
                  pl.BlockSpec((1, 128), lambda i: (0, 0))],
        out_specs=[pl.BlockSpec((tb, 128), lambda i: (nb - 1 - i, 0)), pl.BlockSpec((1, 128), lambda i: (0, 0))],
        out_shape=[jax.ShapeDtypeStruct((t, 128), BF16), jax.ShapeDtypeStruct((1, 128), F32)],
        scratch_shapes=[pltpu.VMEM((1, 128), F32)],
        compiler_params=_params("arbitrary"),
    )(rsum, csum, rest, bf)


ATT_BLOCK = 512


def _head_mask(shape, j):
    lane = lax.broadcasted_iota(jnp.int32, shape, 1)
    return (lane < 64) if j == 0 else (lane >= 64)


def _aug_lanes(tb, j):
    lane = lax.broadcasted_iota(jnp.int32, (tb, 128), 1)
    own = (lane < 64) if j == 0 else (lane >= 64)
    return own, lane - 64 * (1 - j)


def _aug_query(own, li, q, pieces):
    h, m, l = pieces
    one, zero = jnp.ones_like(h), jnp.zeros_like(h)
    spare = jnp.where(li == 0, h, jnp.where(li == 1, m, jnp.where(li == 2, l, jnp.where(li < 6, one, zero))))
    return jnp.where(own, q, spare)


def _fox_prep_fwd(qkv, fcol, name, tb=512):
    t = qkv.shape[0]
    tb = min(tb, t)

    def body(q_ref, k_ref, v_ref, fc_ref, qa_ref, ka_ref, va_ref, qn_ref, kn_ref):
        pieces = _pieces(pltpu.roll(fc_ref[...], 64, 1))
        h, m, l = pieces
        q, k, v = q_ref[...], k_ref[...], v_ref[...]
        first = _head_mask((tb, 128), 0)
        for nrm_ref, x in ((qn_ref, q.astype(F32)), (kn_ref, k.astype(F32))):
            n0 = jnp.max(jnp.sum(jnp.where(first, x * x, 0.0), axis=1, keepdims=True))
            n1 = jnp.max(jnp.sum(jnp.where(first, 0.0, x * x), axis=1, keepdims=True))
            nrm_ref[...] = jnp.where(_head_mask((8, 128), 0), n0, n1)
        one, zero = jnp.ones_like(h), jnp.zeros_like(h)
        for j in (0, 1):
            own, li = _aug_lanes(tb, j)
            cols = slice(128 * j, 128 * (j + 1))
            qa_ref[:, cols] = _aug_query(own, li, q * 0.125, pieces)
            ks = jnp.where(li < 3, one, jnp.where(li == 3, -h, jnp.where(li == 4, -m, jnp.where(li == 5, -l, zero))))
            ka_ref[:, cols] = jnp.where(own, k, ks)
            va_ref[:, cols] = jnp.where(own, v, one)

    blk = pl.BlockSpec((tb, 256), lambda i, h: (i, h))
    nrm = pl.BlockSpec((None, None, 8, 128), lambda i, h: (i, h, 0, 0))
    return pl.pallas_call(
        body, name=name, grid=(t // tb, 4),
        in_specs=[pl.BlockSpec((tb, 128), lambda i, h: (i, h)), pl.BlockSpec((tb, 128), lambda i, h: (i, 4 + h)),
                  pl.BlockSpec((tb, 128), lambda i, h: (i, 8 + h)), pl.BlockSpec((tb, 128), lambda i, h: (i, h))],
        out_specs=[blk, blk, blk, nrm, nrm],
        out_shape=[jax.ShapeDtypeStruct((t, 2 * A_WIDTH), BF16)] * 3
        + [jax.ShapeDtypeStruct((t // tb, 4, 8, 128), F32)] * 2,
        compiler_params=_params("parallel", "parallel"),
    )(qkv, qkv, qkv, fcol)


def _fox_prep_bwd(qkv, fcol, lse, do, o, name, tb=512):
    t = qkv.shape[0]
    tb = min(tb, t)

    def body(q_ref, fc_ref, lse_ref, do_ref, o_ref, qb_ref, dob_ref):
        pieces = _pieces(pltpu.roll(fc_ref[...] - lse_ref[...], 64, 1))
        q = q_ref[...] * 0.125
        do_v = do_ref[...]
        prod = do_v.astype(F32) * o_ref[...].astype(F32)
        for j in (0, 1):
            own, li = _aug_lanes(tb, j)
            cols = slice(128 * j, 128 * (j + 1))
            qb_ref[:, cols] = _aug_query(own, li, q, pieces)
            delta = jnp.sum(jnp.where(own, prod, 0.0), axis=1, keepdims=True)
            h, m, l = _pieces(jnp.broadcast_to(delta, (tb, 128)))
            ds = jnp.where(li == 0, -h, jnp.where(li == 1, -m, jnp.where(li == 2, -l, jnp.zeros_like(h))))
            dob_ref[:, cols] = jnp.where(own, do_v, ds)

    pair = pl.BlockSpec((tb, 128), lambda i, h: (i, h))
    blk = pl.BlockSpec((tb, 256), lambda i, h: (i, h))
    return pl.pallas_call(
        body, name=name, grid=(t // tb, 4),
        in_specs=[pair, pair, pair, pair, pair],
        out_specs=[blk, blk],
        out_shape=[jax.ShapeDtypeStruct((t, 2 * A_WIDTH), BF16)] * 2,
        compiler_params=_params("parallel", "parallel"),
    )(qkv, fcol, lse, do, o)


def _tile_mask(n, transposed):
    r = lax.broadcasted_iota(jnp.int32, (n, n), 0)
    c = lax.broadcasted_iota(jnp.int32, (n, n), 1)
    return (c >= r) if transposed else (r >= c)


UNDERFLOW = -110.0


def _fox_block_ranges(qn, kn, fcum):
    t = fcum.shape[0]
    blk = min(ATT_BLOCK, t)
    nb = t // blk
    q2 = jnp.max(qn[:, :, 0, ::64].reshape(-1, A_HEADS), axis=0)
    k2 = jnp.max(kn[:, :, 0, ::64].reshape(-1, A_HEADS), axis=0)
    bound = 2.0 * jnp.sqrt(q2 * k2) * 0.125
    f = fcum[:, :A_HEADS]
    first = f[0::blk].T
    last = f[blk - 1::blk].T
    dead = (bound[:, None, None] + first[:, :, None] - last[:, None, :]) < UNDERFLOW
    qi = jnp.arange(nb)[None, :, None]
    kj = jnp.arange(nb)[None, None, :]
    dead = dead & (kj < qi)
    kstart = jnp.sum(dead, axis=2).astype(jnp.int32)
    qend = (kj[0] + jnp.sum((~dead) & (qi > kj), axis=1)).astype(jnp.int32)
    return kstart.reshape(-1), qend.reshape(-1)


def _fox_fwd(qa, ka, va, kstart, name):
    t = qa.shape[0]
    bq = min(ATT_BLOCK, t)
    nq = t // bq

    def body(ks_ref, q_ref, k_ref, v_ref, o_ref, lse_ref):
        i = pl.program_id(1)
        hp = pl.program_id(0)
        k0 = [ks_ref[(2 * hp + j) * nq + i] for j in (0, 1)]
        both0 = jnp.maximum(k0[0], k0[1])

        def head(j, kb, m, acc, masked):
            rows = pl.ds(pl.multiple_of(kb * bq, bq), bq)
            cols = slice(128 * j, 128 * (j + 1))
            s = _dot(q_ref[:, cols], k_ref[rows, cols], "nt")
            if masked:
                s = jnp.where(_tile_mask(bq, False), s, -jnp.inf)
            m_new = jnp.maximum(m, jnp.max(s, axis=1, keepdims=True))
            return m_new, jnp.exp(m - m_new) * acc + _dot(jnp.exp(s - m_new), v_ref[rows, cols])

        def pair(kb, carry, masked):
            return head(0, kb, carry[0], carry[1], masked) + head(1, kb, carry[2], carry[3], masked)

        init = (jnp.full((bq, 1), -jnp.inf, F32), jnp.zeros((bq, 128), F32))
        alone = [lax.fori_loop(k0[j], both0, lambda kb, c, j=j: head(j, kb, c[0], c[1], False), init) for j in (0, 1)]
        carry = lax.fori_loop(both0, i, lambda kb, c: pair(kb, c, False), alone[0] + alone[1])
        carry = pair(i, carry, True)
        outs = []
        for j in (0, 1):
            m, acc = carry[2 * j], carry[2 * j + 1]
            spare = 64 * (1 - j)
            l = acc[:, spare:spare + 1]
            outs.append((acc / l, m + jnp.log(l)))
        msk = _head_mask((bq, 128), 0)
        o_ref[...] = jnp.where(msk, outs[0][0], outs[1][0]).astype(BF16)
        lse_ref[...] = jnp.where(msk, outs[0][1], outs[1][1])

    res = pl.BlockSpec((t, 256), lambda h, i, tbl: (0, h))
    out = pl.BlockSpec((bq, 128), lambda h, i, tbl: (i, h))
    return pl.pallas_call(
        body,
        name=name,
        grid_spec=pltpu.PrefetchScalarGridSpec(
            num_scalar_prefetch=1, grid=(4, nq),
            in_specs=[pl.BlockSpec((bq, 256), lambda h, i, tbl: (i, h)), res, res],
            out_specs=[out, out]),
        out_shape=[jax.ShapeDtypeStruct((t, A_WIDTH), BF16), jax.ShapeDtypeStruct((t, A_WIDTH), F32)],
        compiler_params=_params("parallel", "parallel"),
    )(kstart, qa, ka, va)


def _fox_bwd_dq(qb, ka, va, dob, kstart, name):
    t = qb.shape[0]
    bq = min(ATT_BLOCK, t)
    nq = t // bq

    def body(ks_ref, q_ref, k_ref, v_ref, do_ref, dq_ref, rs_ref):
        i = pl.program_id(1)
        hp = pl.program_id(0)
        k0 = [ks_ref[(2 * hp + j) * nq + i] for j in (0, 1)]
        both0 = jnp.maximum(k0[0], k0[1])

        def head(j, kb, acc, masked):
            rows = pl.ds(pl.multiple_of(kb * bq, bq), bq)
            cols = slice(128 * j, 128 * (j + 1))
            ks = k_ref[rows, cols]
            s = _dot(q_ref[:, cols], ks, "nt")
            if masked:
                s = jnp.where(_tile_mask(bq, False), s, -jnp.inf)
            ds = jnp.exp(s) * _dot(do_ref[:, cols], v_ref[rows, cols], "nt")
            return acc + _dot(ds, ks)

        def pair(kb, accs, masked):
            return head(0, kb, accs[0], masked), head(1, kb, accs[1], masked)

        zero = jnp.zeros((bq, 128), F32)
        alone = [lax.fori_loop(k0[j], both0, lambda kb, c, j=j: head(j, kb, c, False), zero) for j in (0, 1)]
        accs = lax.fori_loop(both0, i, lambda kb, c: pair(kb, c, False), tuple(alone))
        accs = pair(i, accs, True)
        outs = []
        for j in (0, 1):
            spare = 64 * (1 - j)
            outs.append((accs[j] * 0.125, accs[j][:, spare:spare + 1]))
        msk = _head_mask((bq, 128), 0)
        dq_ref[...] = jnp.where(msk, outs[0][0], outs[1][0]).astype(BF16)
        rs_ref[...] = jnp.where(msk, outs[0][1], outs[1][1])

    blk = pl.BlockSpec((bq, 256), lambda h, i, tbl: (i, h))
    res = pl.BlockSpec((t, 256), lambda h, i, tbl: (0, h))
    out = pl.BlockSpec((bq, 128), lambda h, i, tbl: (i, h))
    return pl.pallas_call(
        body,
        name=name,
        grid_spec=pltpu.PrefetchScalarGridSpec(
            num_scalar_prefetch=1, grid=(4, nq), in_specs=[blk, res, res, blk], out_specs=[out, out]),
        out_shape=[jax.ShapeDtypeStruct((t, A_WIDTH), BF16), jax.ShapeDtypeStruct((t, A_WIDTH), F32)],
        compiler_params=_params("parallel", "parallel"),
    )(kstart, qb, ka, va, dob)


def _fox_bwd_dkv(qb, ka, va, dob, qend, name):
    t = qb.shape[0]
    bk = min(ATT_BLOCK, t)
    nk = t // bk

    def body(qe_ref, k_ref, v_ref, q_ref, do_ref, dk_ref, dv_ref, cs_ref):
        jb = pl.program_id(1)
        hp = pl.program_id(0)
        i1 = [qe_ref[(2 * hp + j) * nk + jb] + 1 for j in (0, 1)]
        both1 = jnp.minimum(i1[0], i1[1])

        def head(j, ib, dk_acc, dv_acc, masked):
            rows = pl.ds(pl.multiple_of(ib * bk, bk), bk)
            cols = slice(128 * j, 128 * (j + 1))
            qs = q_ref[rows, cols]
            dos = do_ref[rows, cols]
            st = _dot(k_ref[:, cols], qs, "nt")
            if masked:
                st = jnp.where(_tile_mask(bk, True), st, -jnp.inf)
            pt = jnp.exp(st)
            return dk_acc + _dot(pt * _dot(v_ref[:, cols], dos, "nt"), qs), dv_acc + _dot(pt, dos)

        def pair(ib, carry, masked):
            return head(0, ib, carry[0], carry[1], masked) + head(1, ib, carry[2], carry[3], masked)

        carry = pair(jb, (jnp.zeros((bk, 128), F32),) * 4, True)
        carry = lax.fori_loop(jb + 1, both1, lambda ib, c: pair(ib, c, False), carry)
        alone = [lax.fori_loop(jnp.maximum(both1, jb + 1), i1[j],
                               lambda ib, c, j=j: head(j, ib, c[0], c[1], False), carry[2 * j:2 * j + 2])
                 for j in (0, 1)]
        carry = alone[0] + alone[1]
        outs = []
        for j in (0, 1):
            spare = 64 * (1 - j)
            dk_acc, dv_acc = carry[2 * j], carry[2 * j + 1]
            outs.append((dk_acc, dv_acc, dk_acc[:, spare + 3:spare + 4]))
        msk = _head_mask((bk, 128), 0)
        dk_ref[...] = jnp.where(msk, outs[0][0], outs[1][0]).astype(BF16)
        dv_ref[...] = jnp.where(msk, outs[0][1], outs[1][1]).astype(BF16)
        cs_ref[...] = jnp.where(msk, outs[0][2], outs[1][2])

    blk = pl.BlockSpec((bk, 256), lambda h, i, tbl: (i, h))
    res = pl.BlockSpec((t, 256), lambda h, i, tbl: (0, h))
    out = pl.BlockSpec((bk, 128), lambda h, i, tbl: (i, h))
    return pl.pallas_call(
        body,
        name=name,
        grid_spec=pltpu.PrefetchScalarGridSpec(
            num_scalar_prefetch=1, grid=(4, nk), in_specs=[blk, blk, res, res], out_specs=[out, out, out]),
        out_shape=[jax.ShapeDtypeStruct((t, A_WIDTH), BF16), jax.ShapeDtypeStruct((t, A_WIDTH), BF16),
                   jax.ShapeDtypeStruct((t, A_WIDTH), F32)],
        compiler_params=_params("parallel", "parallel"),
    )(qend, ka, va, qb, dob)


HG_ROWS = 256


def _hg_gates(hb_ref, rows, lbv):
    qb = hb_ref[rows, 0:B_WIDTH]
    fb = hb_ref[rows, B_WIDTH:2 * B_WIDTH]
    v = hb_ref[rows, 2 * B_WIDTH:3 * B_WIDTH]
    gb = hb_ref[rows, 3 * B_WIDTH:4 * B_WIDTH]
    sg = _sigmoid(fb)
    f = lbv + (1.0 - lbv) * sg
    sq = _sigmoid(qb)
    return qb, sq, qb * sq, sg, f, 1.0 - f, jnp.log(f), v, gb


def _hg_intra_factors(q, k, b):
    fac = []
    for i in range(CHUNK // SUB):
        bi = b[SUB * i:SUB * i + 1, :]
        eq = jnp.exp(b[SUB * i:SUB * (i + 1), :] - bi)
        ek = jnp.exp(jnp.minimum(bi - b, EXP_CLAMP))
        fac.append((eq, ek, q[SUB * i:SUB * (i + 1), :] * eq, k * ek))
    return fac


def _causal(n):
    r = lax.broadcasted_iota(jnp.int32, (n, n), 0)
    c = lax.broadcasted_iota(jnp.int32, (n, n), 1)
    return r >= c


def _hgrn_fwd(rest, lb, ng, name):
    t = rest.shape[0]
    bt = min(HG_ROWS, t)
    ncb = bt // CHUNK

    def body(hb_ref, lb_ref, ng_ref, y_ref, o_ref, st_ref, s_scr):
        @pl.when(pl.program_id(0) == 0)
        def _():
            s_scr[...] = jnp.zeros_like(s_scr)

        tril = _tri(CHUNK, True)
        causal = _causal(CHUNK)
        ones = jnp.ones((CHUNK, HD), F32)

        def chunk(c, carry):
            rows = pl.ds(pl.multiple_of(c * CHUNK, CHUNK), CHUNK)
            _, _, q_all, _, _, k_all, g_all, v_all, gb_all = _hg_gates(hb_ref, rows, lb_ref[...])
            b_all = _dot_hi(tril, g_all)
            qd_all = q_all * jnp.exp(b_all)
            kd_all = k_all * jnp.exp(b_all[CHUNK - 1:CHUNK, :] - b_all)
            eb_all = jnp.exp(_dot_hi(g_all, ones, "tn", exact="b"))
            sgb_all = _sigmoid(gb_all)
            for h in range(B_HEADS):
                cols = slice(h * HD, (h + 1) * HD)
                v = v_all[:, cols]
                s0 = s_scr[h]
                st_ref[c, h] = s0
                o = _dot(qd_all[:, cols], s0)
                fac = _hg_intra_factors(q_all[:, cols], k_all[:, cols], b_all[:, cols])
                a = jnp.concatenate([_dot(qe, ke, "nt") for _, _, qe, ke in fac], axis=0)
                o = o + _dot(jnp.where(causal, a, 0.0), v)
                s_scr[h] = eb_all[h * HD:(h + 1) * HD, :] * s0 + _dot(kd_all[:, cols], v, "tn")
                r = lax.rsqrt(jnp.mean(o * o, axis=-1, keepdims=True) + RMS_EPS)
                o_ref[rows, cols] = o
                y_ref[rows, cols] = (o * r * ng_ref[...] * sgb_all[:, cols]).astype(BF16)
            return carry

        lax.fori_loop(0, ncb, chunk, 0)

    return pl.pallas_call(
        body,
        name=name,
        grid=(t // bt,),
        in_specs=[pl.BlockSpec((bt, 4 * B_WIDTH), lambda i: (i, 1)), pl.BlockSpec((1, B_WIDTH), lambda i: (0, 0)),
                  pl.BlockSpec((1, HD), lambda i: (0, 0))],
        out_specs=[pl.BlockSpec((bt, B_WIDTH), lambda i: (i, 0)), pl.BlockSpec((bt, B_WIDTH), lambda i: (i, 0)),
                   pl.BlockSpec((ncb, B_HEADS, HD, HD), lambda i: (i, 0, 0, 0))],
        out_shape=[jax.ShapeDtypeStruct((t, B_WIDTH), BF16), jax.ShapeDtypeStruct((t, B_WIDTH), F32),
                   jax.ShapeDtypeStruct((t // CHUNK, B_HEADS, HD, HD), F32)],
        scratch_shapes=[pltpu.VMEM((B_HEADS, HD, HD), F32)],
        compiler_params=_params("arbitrary"),
    )(rest, lb, ng)


def _hgrn_bwd(dy, rest, o_saved, states, lb, ng, name):
    t = rest.shape[0]
    bt = min(HG_ROWS, t)
    ncb = bt // CHUNK
    nb = t // bt

    def body(dy_ref, hb_ref, o_ref, st_ref, lb_ref, ng_ref, dh_ref, dlb_ref, dng_ref, ds_scr):
        @pl.when(pl.program_id(0) == 0)
        def _():
            ds_scr[...] = jnp.zeros_like(ds_scr)
            dlb_ref[...] = jnp.zeros_like(dlb_ref)
            dng_ref[...] = jnp.zeros_like(dng_ref)

        tril = _tri(CHUNK, True)
        triu = _tri(CHUNK, False)
        causal = _causal(CHUNK)
        ones = jnp.ones((CHUNK, HD), F32)
        ones8 = jnp.ones((8, HD), F32)
        last_row = lax.broadcasted_iota(jnp.int32, (CHUNK, B_WIDTH), 0) == CHUNK - 1

        def chunk(cc, carry):
            dng_acc, dlb_acc = carry
            c = ncb - 1 - cc
            rows = pl.ds(pl.multiple_of(c * CHUNK, CHUNK), CHUNK)
            lbv = lb_ref[...]
            qb, sq, q_all, sg, f, k_all, g_all, v_all, gb = _hg_gates(hb_ref, rows, lbv)
            b_all = _dot_hi(tril, g_all)
            ebt_all = jnp.exp(b_all)
            blast = b_all[CHUNK - 1:CHUNK, :]
            ekd_all = jnp.exp(blast - b_all)
            eb_all = jnp.exp(_dot_hi(g_all, ones, "tn", exact="b"))
            sgb = _sigmoid(gb)
            dy_all = dy_ref[rows, :].astype(F32)
            don_all = dy_all * sgb
            ngv = ng_ref[...]
            dq_l, dk_l, dks_l, dv_l, on_l, prod_l = [], [], [], [], [], []
            for h in range(B_HEADS):
                cols = slice(h * HD, (h + 1) * HD)
                q, k, v = q_all[:, cols], k_all[:, cols], v_all[:, cols]
                o = o_ref[rows, cols]
                don = don_all[:, cols]
                r = lax.rsqrt(jnp.mean(o * o, axis=-1, keepdims=True) + RMS_EPS)
                on_l.append(o * r * ngv)
                dng_acc = dng_acc + jnp.sum(don * o * r, axis=0, keepdims=True)
                doh = don * ngv
                do = r * (doh - o * (r * r) * jnp.mean(doh * o, axis=-1, keepdims=True))
                ebt, ekd = ebt_all[:, cols], ekd_all[:, cols]
                s0 = st_ref[c, h]
                ds1 = ds_scr[h]
                fac = _hg_intra_factors(q, k, b_all[:, cols])
                a = jnp.concatenate([_dot(qe, ke, "nt") for _, _, qe, ke in fac], axis=0)
                a = jnp.where(causal, a, 0.0)
                da = jnp.where(causal, _dot(do, v, "nt"), 0.0)
                dv_l.append(_dot(a, do, "tn") + _dot(k * ekd, ds1))
                dq = ebt * _dot(do, s0, "nt")
                dq_l.append(dq + jnp.concatenate(
                    [eq * _hdot(da[SUB * i:SUB * (i + 1), :], ke) for i, (eq, _, _, ke) in enumerate(fac)], axis=0))
                dk_state = ekd * _dot(v, ds1, "nt")
                dk = dk_state
                for i, (_, ek, qe, _) in enumerate(fac):
                    dk = dk + ek * _hdot(da[SUB * i:SUB * (i + 1), :], qe, "tn")
                dk_l.append(dk)
                dks_l.append(dk_state)
                prod_l.append(ds1 * s0)
                ds_scr[h] = _dot(q * ebt, do, "tn") + eb_all[h * HD:(h + 1) * HD, :] * ds1
            dq_all, dk_all = jnp.concatenate(dq_l, axis=1), jnp.concatenate(dk_l, axis=1)
            extra = jnp.exp(blast) * _dot_hi(ones8, jnp.concatenate(prod_l, axis=0), "nt")[0:1, :] \
                + jnp.sum(k_all * jnp.concatenate(dks_l, axis=1), axis=0, keepdims=True)
            db = q_all * dq_all - k_all * dk_all + jnp.where(last_row, extra, 0.0)
            df = _dot_hi(triu, db) / f - dk_all
            dlb_acc = dlb_acc + jnp.sum(df * (1.0 - sg), axis=0, keepdims=True)
            dh_ref[rows, 0:B_WIDTH] = (dq_all * (sq * (1.0 + qb * (1.0 - sq)))).astype(BF16)
            dh_ref[rows, B_WIDTH:2 * B_WIDTH] = (df * (1.0 - lbv) * sg * (1.0 - sg)).astype(BF16)
            dh_ref[rows, 2 * B_WIDTH:3 * B_WIDTH] = jnp.concatenate(dv_l, axis=1).astype(BF16)
            dh_ref[rows, 3 * B_WIDTH:4 * B_WIDTH] = (dy_all * jnp.concatenate(on_l, axis=1)
                                                     * sgb * (1.0 - sgb)).astype(BF16)
            return dng_acc, dlb_acc

        dng_sum, dlb_sum = lax.fori_loop(0, ncb, chunk, (jnp.zeros((1, HD), F32), jnp.zeros((1, B_WIDTH), F32)))
        dng_ref[...] += dng_sum
        dlb_ref[...] += dlb_sum

    rev = lambda i: (nb - 1 - i, 0)
    return pl.pallas_call(
        body,
        name=name,
        grid=(nb,),
        in_specs=[pl.BlockSpec((bt, B_WIDTH), rev), pl.BlockSpec((bt, 4 * B_WIDTH), lambda i: (nb - 1 - i, 1)),
                  pl.BlockSpec((bt, B_WIDTH), rev),
                  pl.BlockSpec((ncb, B_HEADS, HD, HD), lambda i: (nb - 1 - i, 0, 0, 0)),
                  pl.BlockSpec((1, B_WIDTH), lambda i: (0, 0)), pl.BlockSpec((1, HD), lambda i: (0, 0))],
        out_specs=[pl.BlockSpec((bt, 4 * B_WIDTH), rev), pl.BlockSpec((1, B_WIDTH), lambda i: (0, 0)),
                   pl.BlockSpec((1, HD), lambda i: (0, 0))],
        out_shape=[jax.ShapeDtypeStruct((t, 4 * B_WIDTH), BF16), jax.ShapeDtypeStruct((1, B_WIDTH), F32),
                   jax.ShapeDtypeStruct((1, HD), F32)],
        scratch_shapes=[pltpu.VMEM((B_HEADS, HD, HD), F32)],
        compiler_params=_params("arbitrary"),
    )(dy, rest, o_saved, states, lb, ng)


def _axpy2(c0, a0, c1, a1, name, tm=512):
    t, d = a0.shape
    tm = min(tm, t)

    def body(a_ref, b_ref, o_ref):
        o_ref[...] = c0 * a_ref[...] + c1 * b_ref[...]

    row = lambda i: (i, 0)
    return pl.pallas_call(
        body, name=name, grid=(t // tm,),
        in_specs=[pl.BlockSpec((tm, d), row), pl.BlockSpec((tm, d), row)],
        out_specs=pl.BlockSpec((tm, d), row),
        out_shape=jax.ShapeDtypeStruct((t, d), F32),
        compiler_params=_params("parallel"),
    )(a0, a1)


def _split_w_in(w_in_l):
    wqkv = w_in_l[:, :3 * A_WIDTH]
    wfa = jnp.pad(w_in_l[:, 3 * A_WIDTH:3 * A_WIDTH + A_HEADS], ((0, 0), (0, 128 - A_HEADS)))
    whb = w_in_l[:, 3 * A_WIDTH + A_HEADS:3 * A_WIDTH + A_HEADS + 4 * B_WIDTH]
    wgt = w_in_l[:, 3 * A_WIDTH + A_HEADS + 4 * B_WIDTH:]
    return wqkv, jnp.concatenate([wgt, whb, wfa], axis=1)


def _merge_w_in_grad(dwall):
    o = 3 * A_WIDTH
    return jnp.concatenate([dwall[:, :o], dwall[:, o + 4096:o + 4096 + A_HEADS], dwall[:, o + 2048:o + 4096],
                            dwall[:, o:o + 2048]], axis=1)


def _layer_fwd(x, xb, w, sp, l):
    t = x.shape[0]
    n = f"l{l}_"
    wqkv, wrest = _split_w_in(w["w_in"])
    qkv = _matmul(xb, wqkv, "nn", BF16, MM_ROWS, 768, D_MODEL, n + "proj_qkv")
    rest = _matmul(xb, wrest, "nn", F32, MM_ROWS, 1408, D_MODEL, n + "proj_rest")
    bf = jnp.pad(sp["b_fgate"], (0, 128 - A_HEADS)).reshape(1, 128)
    fcum, fcol = _fox_gate_fwd(rest, bf, n + "fox_gate_fwd")
    qa, ka, va, qn, kn = _fox_prep_fwd(qkv, fcol, n + "fox_prep_fwd")
    kstart, qend = _fox_block_ranges(qn, kn, fcum)
    ya, lse = _fox_fwd(qa, ka, va, kstart, n + "fox_fwd")
    lb = sp["lb"].reshape(1, B_WIDTH)
    ng = sp["norm_g"].reshape(1, HD)
    yb, ob, states = _hgrn_fwd(rest, lb, ng, n + "hgrn_fwd")
    merged = _merge_fwd(ya, yb, w["w_pa"], w["w_pb"], rest, n + "merge_fwd")
    x1, x1b, xh1, rs1 = _mm_res_ln(merged, w["w_out"], x, sp["ln1_g"], sp["ln1_b"], n + "out_ln1")
    wu, wg = w["w_ff_in"][:, :FFN_HIDDEN], w["w_ff_in"][:, FFN_HIDDEN:]
    a, hu, hg = _ffn_in_swiglu(x1b, wu, wg, n + "ffn_in_swiglu")
    x2, x2b, xh2, rs2 = _mm_res_ln(a, w["w_ff_out"], x1, sp["ln2_g"], sp["ln2_b"], n + "ffn_out_ln2")
    saved = dict(xb=xb, wqkv=wqkv, wrest=wrest, qkv=qkv, rest=rest, bf=bf, fcol=fcol, ka=ka, va=va, ya=ya, lse=lse,
                 kstart=kstart, qend=qend,
                 lb=lb, ng=ng, yb=yb, ob=ob, states=states, merged=merged, x1b=x1b, xh1=xh1, rs1=rs1, a=a,
                 wu=wu, wg=wg, hu=hu, hg=hg,
                 xh2=xh2, rs2=rs2)
    return x2, x2b, saved


def _layer_bwd(dys, coefs, w, sp, s, l):
    n = f"l{l}_"
    dz2, dz2b, dg2, db2 = _ln_bwd(dys, coefs, s["xh2"], s["rs2"], sp["ln2_g"], n + "ln2_bwd")
    du, dg = _ffn_out_dx_swiglu(dz2b, w["w_ff_out"], s["hu"], s["hg"], n + "ffn_out_dx_swiglu")
    d_wffout = _matmul(s["a"], dz2b, "tn", F32, 1408, 1024, DW_ROWS, n + "ffn_out_dw")
    dx1u = _matmul(du, s["wu"], "nt", F32, MM_ROWS, 1024, FFN_HIDDEN, n + "ffn_in_dx_u")
    dx1g = _matmul(dg, s["wg"], "nt", F32, MM_ROWS, 1024, FFN_HIDDEN, n + "ffn_in_dx_g")
    d_wffin = jnp.concatenate([_matmul(s["x1b"], du, "tn", F32, 1024, 1408, DW_ROWS, n + "ffn_in_dw_u"),
                               _matmul(s["x1b"], dg, "tn", F32, 1024, 1408, DW_ROWS, n + "ffn_in_dw_g")], axis=1)
    dz1, dz1b, dg1, db1 = _ln_bwd([dz2, dx1u, dx1g], [ALPHA, 1.0, 1.0], s["xh1"], s["rs1"], sp["ln1_g"],
                                  n + "ln1_bwd")
    d_wout = _matmul(s["merged"], dz1b, "tn", F32, 1024, 1024, DW_ROWS, n + "out_dw")
    dgates, dpa, dpb, dya, dyb = _merge_bwd(dz1b, w["w_out"], s["ya"], s["yb"], w["w_pa"], w["w_pb"], s["rest"],
                                  n + "merge_bwd")
    d_wpa = _matmul(s["ya"], dpa, "tn", F32, 512, 1024, DW_ROWS, n + "pa_dw")
    d_wpb = _matmul(s["yb"], dpb, "tn", F32, 512, 1024, DW_ROWS, n + "pb_dw")
    qb, dob = _fox_prep_bwd(s["qkv"], s["fcol"], s["lse"], dya, s["ya"], n + "fox_prep_bwd")
    dq, rsum = _fox_bwd_dq(qb, s["ka"], s["va"], dob, s["kstart"], n + "fox_bwd_dq")
    dk, dv, csum = _fox_bwd_dkv(qb, s["ka"], s["va"], dob, s["qend"], n + "fox_bwd_dkv")
    dfa, dbf = _fox_gate_bwd(rsum, csum, s["rest"], s["bf"], n + "fox_gate_bwd")
    dhb, dlb, dng = _hgrn_bwd(dyb, s["rest"], s["ob"], s["states"], s["lb"], s["ng"], n + "hgrn_bwd")
    dproj = jnp.concatenate([dq, dk, dv, dgates, dhb, dfa], axis=1)
    wall = jnp.concatenate([s["wqkv"], s["wrest"]], axis=1)
    dxp = _matmul(dproj, wall, "nt", F32, MM_ROWS, 1024, 1920, n + "proj_dx")
    d_wall = _matmul(s["xb"], dproj, "tn", F32, 1024, 1152, DW_ROWS, n + "proj_dw")
    grads = dict(w_in=_merge_w_in_grad(d_wall), w_pa=d_wpa, w_pb=d_wpb, w_out=d_wout, w_ff_in=d_wffin,
                 w_ff_out=d_wffout, b_fgate=dbf[0, :A_HEADS], lb=dlb[0], norm_g=dng[0], ln1_g=dg1[0], ln1_b=db1[0],
                 ln2_g=dg2[0], ln2_b=db2[0])
    return [dz1, dxp], [ALPHA, 1.0], grads


def _lower_bounds(logits):
    sm = jax.nn.softmax(logits.astype(F32), axis=0)
    return jnp.cumsum(sm, axis=0) - sm[0:1]


def _local_step(x, target, wfull, small):
    lbs, lb_vjp = jax.vjp(_lower_bounds, small["hgrn_lb_logits"])
    h, hb = x, x.astype(BF16)
    saved, sps = [], []
    for l in range(DEPTH):
        sp = dict(b_fgate=small["b_fgate"][l], lb=lbs[l], norm_g=small["hgrn_norm_g"][l], ln1_g=small["ln1_g"][l],
                  ln1_b=small["ln1_b"][l], ln2_g=small["ln2_g"][l], ln2_b=small["ln2_b"][l])
        h, hb, s = _layer_fwd(h, hb, wfull[l], sp, l)
        saved.append(s)
        sps.append(sp)
    dy, lpart = _loss_head(h, target)
    dys, coefs = [dy], [1.0]
    grads = [None] * DEPTH
    for l in reversed(range(DEPTH)):
        dys, coefs, grads[l] = _layer_bwd(dys, coefs, wfull[l], sps[l], saved[l], l)
    grad_x = _axpy2(coefs[0], dys[0], coefs[1], dys[1], "grad_x")
    d_logits = lb_vjp(jnp.stack([grads[l]["lb"] for l in range(DEPTH)]))[0]
    return lpart[0, 0], grad_x, grads, d_logits


_BIG = [("w_in", "w_in", (D_MODEL, IN_TOTAL), 1), ("w_branch_a", "w_pa", (A_WIDTH, D_MODEL), 1),
        ("w_branch_b", "w_pb", (B_WIDTH, D_MODEL), 1), ("w_out", "w_out", (D_MODEL, D_MODEL), 0),
        ("w_ff_in", "w_ff_in", (D_MODEL, 2 * FFN_HIDDEN), 1), ("w_ff_out", "w_ff_out", (FFN_HIDDEN, D_MODEL), 0)]
_SMALL = [("b_fgate", A_HEADS), ("hgrn_lb_logits", B_WIDTH), ("hgrn_norm_g", HD), ("ln1_g", D_MODEL),
          ("ln1_b", D_MODEL), ("ln2_g", D_MODEL), ("ln2_b", D_MODEL)]
N_BIG = len(_BIG)
SMALL_ROWS = 80


def _by_chip(full, axis):
    if axis == 0:
        return full.reshape(N_CHIPS, full.shape[0] // N_CHIPS, full.shape[1])
    n = full.shape[1] // N_CHIPS
    return jnp.stack([full[:, q * n:(q + 1) * n] for q in range(N_CHIPS)])


def _from_chips(shards, axis):
    if axis == 0:
        return shards.reshape(N_CHIPS * shards.shape[1], shards.shape[2])
    return jnp.concatenate([shards[q] for q in range(N_CHIPS)], axis=1)


def _pack_small(per_name):
    flat = jnp.concatenate([per_name[name].reshape(-1) for name, _ in _SMALL])
    return jnp.pad(flat, (0, SMALL_ROWS * 128 - flat.shape[0])).reshape(SMALL_ROWS, 128)


def _unpack_small(slab):
    flat, out, r = slab.reshape(-1), {}, 0
    for name, n in _SMALL:
        out[name] = flat[r:r + DEPTH * n].reshape(DEPTH, n)
        r += DEPTH * n
    return out


_ANY = pl.BlockSpec(memory_space=pl.ANY)


def _place():
    return lax.axis_index("x"), lax.axis_index("y"), lax.axis_index("c")


def _other_chips(x, y):
    return [(1 - x, y), (x, 1 - y), (1 - x, 1 - y)]


def _chip_exchange(mine_of, out_refs, send_sems, recv_sems, local_sems):
    x, y, c = _place()
    q = 2 * x + y
    started = []
    for w, out_ref in enumerate(out_refs):
        local = pltpu.make_async_copy(mine_of(w, q), out_ref.at[q], local_sems.at[w])
        local.start()
        started.append(local)
    sends = []
    for k, (px, py) in enumerate(_other_chips(x, y)):
        for w, out_ref in enumerate(out_refs):
            cp = pltpu.make_async_remote_copy(src_ref=mine_of(w, 2 * px + py), dst_ref=out_ref.at[q],
                                              send_sem=send_sems.at[3 * w + k], recv_sem=recv_sems.at[3 * w + k],
                                              device_id=(px, py, c), device_id_type=MESH)
            cp.start()
            sends.append(cp)
    for k, (px, py) in enumerate(_other_chips(x, y)):
        for w, out_ref in enumerate(out_refs):
            pltpu.make_async_remote_copy(src_ref=mine_of(w, q), dst_ref=out_ref.at[2 * px + py],
                                         send_sem=send_sems.at[3 * w + k], recv_sem=recv_sems.at[3 * w + k],
                                         device_id=(px, py, c), device_id_type=MESH).wait_recv()
    for cp in sends:
        cp.wait_send()
    for local in started:
        local.wait()


def _sem_scratch(n):
    return [pltpu.SemaphoreType.DMA((3 * n,)), pltpu.SemaphoreType.DMA((3 * n,)), pltpu.SemaphoreType.DMA((n,))]


def _gather_weights(mine):
    n = len(mine)

    def body(*refs):
        in_refs, out_refs = refs[:n], refs[n:2 * n]
        send_sems, recv_sems, local_sems, pair_send, pair_recv = refs[2 * n:]
        x, y, c = _place()
        _chip_exchange(lambda w, q: in_refs[w].at[c], [o.at[c] for o in out_refs], send_sems, recv_sems, local_sems)
        sibling = (x, y, 1 - c)
        fwds = []
        for w, o in enumerate(out_refs):
            cp = pltpu.make_async_remote_copy(src_ref=o.at[c], dst_ref=o.at[c], send_sem=pair_send.at[w],
                                              recv_sem=pair_recv.at[w], device_id=sibling, device_id_type=MESH)
            cp.start()
            fwds.append(cp)
        for w, o in enumerate(out_refs):
            pltpu.make_async_remote_copy(src_ref=o.at[1 - c], dst_ref=o.at[1 - c], send_sem=pair_send.at[w],
                                         recv_sem=pair_recv.at[w], device_id=sibling, device_id_type=MESH).wait_recv()
        for cp in fwds:
            cp.wait_send()

    return pl.pallas_call(
        body, name="gather_weights", in_specs=[_ANY] * n, out_specs=[_ANY] * n,
        out_shape=[jax.ShapeDtypeStruct((DEPTH, N_CHIPS) + m.shape[1:], m.dtype) for m in mine],
        scratch_shapes=_sem_scratch(n) + [pltpu.SemaphoreType.DMA((n,)), pltpu.SemaphoreType.DMA((n,))],
    )(*mine)


def _pair_exchange(gs):
    n = len(gs)

    def body(*refs):
        g_refs, a_refs, send_sems, recv_sems = refs[:n], refs[n:2 * n], refs[2 * n], refs[2 * n + 1]
        x, y, c = _place()
        cps = []
        for w in range(n):
            cp = pltpu.make_async_remote_copy(src_ref=g_refs[w].at[1 - c], dst_ref=a_refs[w], send_sem=send_sems.at[w],
                                              recv_sem=recv_sems.at[w], device_id=(x, y, 1 - c), device_id_type=MESH)
            cp.start()
            cps.append(cp)
        for cp in cps:
            cp.wait()

    return pl.pallas_call(
        body, name="grad_pair_exchange", in_specs=[_ANY] * n, out_specs=[_ANY] * n,
        out_shape=[jax.ShapeDtypeStruct(g.shape[1:], g.dtype) for g in gs],
        scratch_shapes=[pltpu.SemaphoreType.DMA((n,)), pltpu.SemaphoreType.DMA((n,))],
    )(*gs)


def _row_block(rows):
    return math.gcd(rows, 256)


def _pair_sum(g, a, layer, name):
    _, nq, rows, cols = g.shape
    tb = _row_block(rows)

    def body(l_ref, g_ref, a_ref, o_ref):
        o_ref[...] = (g_ref[...] + a_ref[...]).astype(BF16)

    return pl.pallas_call(
        body, name=name,
        grid_spec=pltpu.PrefetchScalarGridSpec(
            num_scalar_prefetch=1, grid=(nq, rows // tb),
            in_specs=[pl.BlockSpec((None, None, tb, cols), lambda q, i, l_ref: (l_ref[0], q, i, 0)),
                      pl.BlockSpec((None, tb, cols), lambda q, i, l_ref: (q, i, 0))],
            out_specs=pl.BlockSpec((None, tb, cols), lambda q, i, l_ref: (q, i, 0))),
        out_shape=jax.ShapeDtypeStruct((nq, rows, cols), BF16),
        compiler_params=_params("parallel", "parallel"),
    )(layer.reshape(1).astype(jnp.int32), g, a)


def _shard_exchange(ps):
    n = len(ps)

    def body(*refs):
        p_refs, b_refs = refs[:n], refs[n:2 * n]
        send_sems, recv_sems, local_sems = refs[2 * n:]
        _chip_exchange(lambda w, q: p_refs[w].at[q], b_refs, send_sems, recv_sems, local_sems)

    return pl.pallas_call(
        body, name="grad_shard_exchange", in_specs=[_ANY] * n, out_specs=[_ANY] * n,
        out_shape=[jax.ShapeDtypeStruct(p.shape, p.dtype) for p in ps],
        scratch_shapes=_sem_scratch(n),
    )(*ps)


def _sum4(b, name):
    _, rows, cols = b.shape
    tb = _row_block(rows)

    def body(b_ref, o_ref):
        o_ref[...] = ((b_ref[0].astype(F32) + b_ref[1].astype(F32)) + b_ref[2].astype(F32)) + b_ref[3].astype(F32)

    return pl.pallas_call(
        body, name=name, grid=(rows // tb,),
        in_specs=[pl.BlockSpec((N_CHIPS, tb, cols), lambda i: (0, i, 0))],
        out_specs=pl.BlockSpec((tb, cols), lambda i: (i, 0)),
        out_shape=jax.ShapeDtypeStruct((rows, cols), F32),
        compiler_params=_params("parallel"),
    )(b)


def _result_exchange(gcs):
    n = len(gcs)

    def body(*refs):
        g_refs, o_refs, send_sems, recv_sems = refs[:n], refs[n:2 * n], refs[2 * n], refs[2 * n + 1]
        x, y, c = _place()
        cps = []
        for w in range(n):
            cp = pltpu.make_async_remote_copy(src_ref=g_refs[w], dst_ref=o_refs[w], send_sem=send_sems.at[w],
                                              recv_sem=recv_sems.at[w], device_id=(x, y, 1 - c), device_id_type=MESH)
            cp.start()
            cps.append(cp)
        for cp in cps:
            cp.wait()

    return pl.pallas_call(
        body, name="grad_result_exchange", in_specs=[_ANY] * n, out_specs=[_ANY] * n,
        out_shape=[jax.ShapeDtypeStruct(g.shape, g.dtype) for g in gcs],
        scratch_shapes=[pltpu.SemaphoreType.DMA((n,)), pltpu.SemaphoreType.DMA((n,))],
    )(*gcs)


def _allreduce_small(v):
    def body(v_ref, o_ref, buf, send_sems, recv_sems):
        x, y, c = _place()
        me = 4 * x + 2 * y + c
        buf[me] = v_ref[...]
        peers = []
        for k in range(1, N_DEV):
            px = 1 - x if k & 4 else x
            py = 1 - y if k & 2 else y
            pc = 1 - c if k & 1 else c
            peers.append((px, py, pc))
        sends = []
        for k, peer in enumerate(peers):
            cp = pltpu.make_async_remote_copy(src_ref=v_ref, dst_ref=buf.at[me], send_sem=send_sems.at[k],
                                              recv_sem=recv_sems.at[k], device_id=peer, device_id_type=MESH)
            cp.start()
            sends.append(cp)
        for k, (px, py, pc) in enumerate(peers):
            pltpu.make_async_remote_copy(src_ref=v_ref, dst_ref=buf.at[4 * px + 2 * py + pc], send_sem=send_sems.at[k],
                                         recv_sem=recv_sems.at[k], device_id=(px, py, pc),
                                         device_id_type=MESH).wait_recv()
        for cp in sends:
            cp.wait_send()
        acc = buf[0]
        for i in range(1, N_DEV):
            acc = acc + buf[i]
        o_ref[...] = acc

    vm = pl.BlockSpec(memory_space=pltpu.VMEM)
    return pl.pallas_call(
        body, name="small_allreduce", in_specs=[vm], out_specs=vm,
        out_shape=jax.ShapeDtypeStruct(v.shape, F32),
        scratch_shapes=[pltpu.VMEM((N_DEV,) + v.shape, F32), pltpu.SemaphoreType.DMA((N_DEV - 1,)),
                        pltpu.SemaphoreType.DMA((N_DEV - 1,))],
    )(v)


def _adam_update(w, g, m, v):
    nm = ADAM_B1 * m + (1.0 - ADAM_B1) * g
    nv = ADAM_B2 * v + (1.0 - ADAM_B2) * (g * g)
    m_hat = nm / (1.0 - ADAM_B1 ** ADAM_STEP)
    v_hat = nv / (1.0 - ADAM_B2 ** ADAM_STEP)
    return -ADAM_LR * (m_hat / (jnp.sqrt(v_hat) + ADAM_EPS) + ADAM_WD * w), nm, nv


def _adamw_small(w, g, m, v, name):
    def body(w_ref, g_ref, m_ref, v_ref, d_ref, nm_ref, nv_ref):
        d_ref[...], nm_ref[...], nv_ref[...] = _adam_update(w_ref[...], g_ref[...], m_ref[...], v_ref[...])

    vm = pl.BlockSpec(memory_space=pltpu.VMEM)
    return pl.pallas_call(
        body, name=name, in_specs=[vm] * 4, out_specs=[vm] * 3,
        out_shape=[jax.ShapeDtypeStruct(w.shape, F32)] * 3,
    )(w, g, m, v)


def _adamw_big(w, m, v, g_own, g_other, layer, name):
    _, rows, cols = w.shape
    tb = _row_block(rows)

    def body(l_ref, w_ref, m_ref, v_ref, go_ref, gx_ref, g_ref, d_ref, nm_ref, nv_ref):
        gv = jnp.where(pl.program_id(0) == l_ref[0], go_ref[...], gx_ref[...])
        g_ref[...] = gv
        d_ref[...], nm_ref[...], nv_ref[...] = _adam_update(w_ref[...], gv, m_ref[...], v_ref[...])

    per_layer = pl.BlockSpec((None, tb, cols), lambda l, i, l_ref: (l, i, 0))
    shared = pl.BlockSpec((tb, cols), lambda l, i, l_ref: (i, 0))
    return pl.pallas_call(
        body, name=name,
        grid_spec=pltpu.PrefetchScalarGridSpec(
            num_scalar_prefetch=1, grid=(DEPTH, rows // tb),
            in_specs=[per_layer, per_layer, per_layer, shared, shared], out_specs=[per_layer] * 4),
        out_shape=[jax.ShapeDtypeStruct(w.shape, F32)] * 4,
        compiler_params=_params("parallel", "parallel"),
    )(layer.reshape(1).astype(jnp.int32), w, m, v, g_own, g_other)


def kernel(x, w_in, b_fgate, hgrn_lb_logits, hgrn_norm_g, w_branch_a, w_branch_b, w_out, ln1_g, ln1_b, w_ff_in, w_ff_out, ln2_g, ln2_b, loss_target, m_w_in, m_b_fgate, m_hgrn_lb_logits, m_hgrn_norm_g, m_w_branch_a, m_w_branch_b, m_w_out, m_ln1_g, m_ln1_b, m_w_ff_in, m_w_ff_out, m_ln2_g, m_ln2_b, v_w_in, v_b_fgate, v_hgrn_lb_logits, v_hgrn_norm_g, v_w_branch_a, v_w_branch_b, v_w_out, v_ln1_g, v_ln1_b, v_w_ff_in, v_w_ff_out, v_ln2_g, v_ln2_b):
    weights = dict(w_in=w_in, b_fgate=b_fgate, hgrn_lb_logits=hgrn_lb_logits, hgrn_norm_g=hgrn_norm_g,
                   w_branch_a=w_branch_a, w_branch_b=w_branch_b, w_out=w_out, ln1_g=ln1_g, ln1_b=ln1_b,
                   w_ff_in=w_ff_in, w_ff_out=w_ff_out, ln2_g=ln2_g, ln2_b=ln2_b)
    mom1 = dict(w_in=m_w_in, b_fgate=m_b_fgate, hgrn_lb_logits=m_hgrn_lb_logits, hgrn_norm_g=m_hgrn_norm_g,
                w_branch_a=m_w_branch_a, w_branch_b=m_w_branch_b, w_out=m_w_out, ln1_g=m_ln1_g, ln1_b=m_ln1_b,
                w_ff_in=m_w_ff_in, w_ff_out=m_w_ff_out, ln2_g=m_ln2_g, ln2_b=m_ln2_b)
    mom2 = dict(w_in=v_w_in, b_fgate=v_b_fgate, hgrn_lb_logits=v_hgrn_lb_logits, hgrn_norm_g=v_hgrn_norm_g,
                w_branch_a=v_w_branch_a, w_branch_b=v_w_branch_b, w_out=v_w_out, ln1_g=v_ln1_g, ln1_b=v_ln1_b,
                w_ff_in=v_w_ff_in, w_ff_out=v_w_ff_out, ln2_g=v_ln2_g, ln2_b=v_ln2_b)
    core = lax.axis_index("c")

    gathered = _gather_weights([weights[name].astype(BF16) for name, _, _, _ in _BIG])
    wfull = [{key: _from_chips(gathered[w][l], axis) for w, (_, key, _, axis) in enumerate(_BIG)}
             for l in range(DEPTH)]
    small = {name: weights[name] for name, _ in _SMALL}

    loss_part, grad_x, grads, d_logits = _local_step(x[0], loss_target[0], wfull, small)

    g_all = [jnp.stack([_by_chip(grads[l][key], axis) for l in range(DEPTH)]) for _, key, _, axis in _BIG]
    received = _pair_exchange(g_all)
    pair = [_pair_sum(g_all[w], received[w], core, f"grad_pair_sum_{w}") for w in range(N_BIG)]
    by_chip = _shard_exchange(pair)
    g_layer = [_sum4(by_chip[w], f"grad_chip_sum_{w}") for w in range(N_BIG)]
    g_other = _result_exchange(g_layer)
    out_g, out_d, out_m, out_v = {}, {}, {}, {}
    for w, (name, _, _, _) in enumerate(_BIG):
        out_g[name], out_d[name], out_m[name], out_v[name] = _adamw_big(
            weights[name], mom1[name], mom2[name], g_layer[w], g_other[w], core, f"adamw_{name}")

    small_grads = {name: jnp.stack([grads[l][key] for l in range(DEPTH)])
                   for name, key in [("b_fgate", "b_fgate"), ("hgrn_norm_g", "norm_g"), ("ln1_g", "ln1_g"),
                                     ("ln1_b", "ln1_b"), ("ln2_g", "ln2_g"), ("ln2_b", "ln2_b")]}
    small_grads["hgrn_lb_logits"] = d_logits
    gs = _allreduce_small(_pack_small(small_grads))
    ds, ms, vs = _adamw_small(_pack_small(small), gs, _pack_small({n: mom1[n] for n, _ in _SMALL}),
                              _pack_small({n: mom2[n] for n, _ in _SMALL}), "adamw_small")
    for tree, slab in ((out_g, gs), (out_d, ds), (out_m, ms), (out_v, vs)):
        tree.update(_unpack_small(slab))

    loss = lax.psum(loss_part, ("x", "y", "c"))
    order = ["w_in", "b_fgate", "hgrn_lb_logits", "hgrn_norm_g", "w_branch_a", "w_branch_b", "w_out", "ln1_g", "ln1_b",
             "w_ff_in", "w_ff_out", "ln2_g", "ln2_b"]
    return (loss, grad_x[None], *[out_g[n] for n in order], *[out_d[n] for n in order],
            *[out_m[n] for n in order], *[out_v[n] for n in order])
```

```python
import functools
import math

import jax
import jax.numpy as jnp
import numpy as np
from jax import lax
from jax.experimental import pallas as pl
from jax.experimental.pallas import tpu as pltpu

F32 = jnp.float32
BF16 = jnp.bfloat16

D_MODEL = 1024
DEPTH = 2
A_HEADS = 8
A_WIDTH = 512
B_WIDTH = 512
B_HEADS = 4
HD = 128
CHUNK = 64
SUB = 16
FFN_HIDDEN = 2816
IN_TOTAL = 5640
ALPHA = (2 * DEPTH) ** 0.25
LN_EPS = 1e-5
RMS_EPS = 1e-6
ADAM_LR = 0.001
ADAM_B1 = 0.9
ADAM_B2 = 0.999
ADAM_EPS = 1e-08
ADAM_WD = 0.01
ADAM_STEP = 10
EXP_CLAMP = 60.0

VMEM_LIMIT_BYTES = 56 * 1024 * 1024
MM_ROWS = 1024
DW_ROWS = 2048
N_CHIPS = 4
N_DEV = 8
MESH = pl.DeviceIdType.MESH

_DN = {
    "nn": (((1,), (0,)), ((), ())),
    "nt": (((1,), (1,)), ((), ())),
    "tn": (((0,), (0,)), ((), ())),
}


def _dot(a, b, mode="nn"):
    return lax.dot_general(a.astype(BF16), b.astype(BF16), _DN[mode], preferred_element_type=F32)


def _pieces(x):
    h = x.astype(BF16)
    r = x - h.astype(F32)
    m = r.astype(BF16)
    return h, m, (r - m.astype(F32)).astype(BF16)


def _dot_hi(a, b, mode="nn", exact="a"):
    if exact == "a":
        h, m, l = _pieces(b)
        return (_dot(a, l, mode) + _dot(a, m, mode)) + _dot(a, h, mode)
    h, m, l = _pieces(a)
    return (_dot(l, b, mode) + _dot(m, b, mode)) + _dot(h, b, mode)


def _hdot(a, b, mode="nn"):
    bh, bl, _ = _pieces(b)
    return _dot(a, bl, mode) + _dot(a, bh, mode)


def _params(*sem):
    return pltpu.CompilerParams(dimension_semantics=sem, vmem_limit_bytes=VMEM_LIMIT_BYTES)


def _sigmoid(x):
    return 1.0 / (1.0 + jnp.exp(-x))


def _matmul(a, b, mode, out_dtype, tm, tn, tk, name):
    if mode == "nn":
        (m, k), (k2, n) = a.shape, b.shape
    elif mode == "nt":
        (m, k), (n, k2) = a.shape, b.shape
    else:
        (k, m), (k2, n) = a.shape, b.shape
    assert k == k2, (a.shape, b.shape, mode)
    tm, tn, tk = min(tm, m), min(tn, n), min(tk, k)
    assert m % tm == 0 and n % tn == 0 and k % tk == 0, (a.shape, b.shape, tm, tn, tk)
    nk = k // tk
    if mode == "tn":
        a_spec = pl.BlockSpec((tk, tm), lambda j, i, kk: (kk, i))
    else:
        a_spec = pl.BlockSpec((tm, tk), lambda j, i, kk: (i, kk))
    if mode == "nt":
        b_spec = pl.BlockSpec((tn, tk), lambda j, i, kk: (j, kk))
    else:
        b_spec = pl.BlockSpec((tk, tn), lambda j, i, kk: (kk, j))
    use_acc = nk > 1 and out_dtype != F32

    def body(a_ref, b_ref, o_ref, *scratch):
        p = _dot(a_ref[...], b_ref[...], mode)
        if nk == 1:
            o_ref[...] = p.astype(out_dtype)
            return
        acc_ref = scratch[0] if use_acc else o_ref
        kk = pl.program_id(2)

        @pl.when(kk == 0)
        def _():
            acc_ref[...] = p

        @pl.when(kk > 0)
        def _():
            acc_ref[...] += p

        if use_acc:
            @pl.when(kk == nk - 1)
            def _():
                o_ref[...] = acc_ref[...].astype(out_dtype)

    return pl.pallas_call(
        body,
        name=name,
        grid=(n // tn, m // tm, nk),
        in_specs=[a_spec, b_spec],
        out_specs=pl.BlockSpec((tm, tn), lambda j, i, kk: (i, j)),
        out_shape=jax.ShapeDtypeStruct((m, n), out_dtype),
        scratch_shapes=[pltpu.VMEM((tm, tn), F32)] if use_acc else [],
        compiler_params=_params("parallel", "parallel", "arbitrary"),
    )(a, b)


def _mm_res_ln(a, w, res, g, b, name, tm=512):
    t, k = a.shape
    d = w.shape[1]
    tm = min(tm, t)

    def body(a_ref, w_ref, r_ref, g_ref, b_ref, y_ref, yb_ref, xh_ref, rs_ref):
        z = ALPHA * r_ref[...] + _dot(a_ref[...], w_ref[...])
        mu = jnp.mean(z, axis=-1, keepdims=True)
        zc = z - mu
        var = jnp.mean(zc * zc, axis=-1, keepdims=True)
        rstd = lax.rsqrt(var + LN_EPS)
        xh = zc * rstd
        y = xh * g_ref[...] + b_ref[...]
        y_ref[...] = y
        yb_ref[...] = y.astype(BF16)
        xh_ref[...] = xh
        rs_ref[...] = rstd

    row = lambda i: (i, 0)
    fix = lambda i: (0, 0)
    return pl.pallas_call(
        body,
        name=name,
        grid=(t // tm,),
        in_specs=[pl.BlockSpec((tm, k), row), pl.BlockSpec((k, d), fix), pl.BlockSpec((tm, d), row),
                  pl.BlockSpec((1, d), fix), pl.BlockSpec((1, d), fix)],
        out_specs=[pl.BlockSpec((tm, d), row), pl.BlockSpec((tm, d), row), pl.BlockSpec((tm, d), row),
                   pl.BlockSpec((tm, 1), row)],
        out_shape=[jax.ShapeDtypeStruct((t, d), F32), jax.ShapeDtypeStruct((t, d), BF16),
                   jax.ShapeDtypeStruct((t, d), F32), jax.ShapeDtypeStruct((t, 1), F32)],
        compiler_params=_params("parallel"),
    )(a, w, res, g.reshape(1, d), b.reshape(1, d))


def _ln_bwd(dys, coefs, xhat, rstd, g, name, tm=512):
    t, d = xhat.shape
    tm = min(tm, t)
    n_in = len(dys)

    def body(*refs):
        dy_refs = refs[:n_in]
        xh_ref, rs_ref, g_ref, dz_ref, dzb_ref, dg_ref, db_ref = refs[n_in:]
        dy = coefs[0] * dy_refs[0][...].astype(F32)
        for c, r in zip(coefs[1:], dy_refs[1:]):
            dy = dy + c * r[...].astype(F32)
        xh = xh_ref[...]
        dxh = dy * g_ref[...]
        m1 = jnp.mean(dxh, axis=-1, keepdims=True)
        m2 = jnp.mean(dxh * xh, axis=-1, keepdims=True)
        dz = rs_ref[...] * (dxh - m1 - xh * m2)
        dz_ref[...] = dz
        dzb_ref[...] = dz.astype(BF16)
        pg = jnp.sum(dy * xh, axis=0, keepdims=True)
        pb = jnp.sum(dy, axis=0, keepdims=True)

        @pl.when(pl.program_id(0) == 0)
        def _():
            dg_ref[...] = pg
            db_ref[...] = pb

        @pl.when(pl.program_id(0) > 0)
        def _():
            dg_ref[...] += pg
            db_ref[...] += pb

    row = lambda i: (i, 0)
    fix = lambda i: (0, 0)
    return pl.pallas_call(
        body,
        name=name,
        grid=(t // tm,),
        in_specs=[pl.BlockSpec((tm, d), row)] * n_in
        + [pl.BlockSpec((tm, d), row), pl.BlockSpec((tm, 1), row), pl.BlockSpec((1, d), fix)],
        out_specs=[pl.BlockSpec((tm, d), row), pl.BlockSpec((tm, d), row), pl.BlockSpec((1, d), fix),
                   pl.BlockSpec((1, d), fix)],
        out_shape=[jax.ShapeDtypeStruct((t, d), F32), jax.ShapeDtypeStruct((t, d), BF16),
                   jax.ShapeDtypeStruct((1, d), F32), jax.ShapeDtypeStruct((1, d), F32)],
        compiler_params=_params("arbitrary"),
    )(*dys, xhat, rstd, g.reshape(1, d))


def _loss_head(y, target, name="loss_head", tm=512):
    t, d = y.shape
    tm = min(tm, t)

    def body(y_ref, t_ref, dy_ref, l_ref):
        e = y_ref[...] - t_ref[...]
        dy_ref[...] = e * (1.0 / d)
        part = jnp.full((8, 128), 0.5 / d, F32) * jnp.sum(e * e)

        @pl.when(pl.program_id(0) == 0)
        def _():
            l_ref[...] = part

        @pl.when(pl.program_id(0) > 0)
        def _():
            l_ref[...] += part

    row = lambda i: (i, 0)
    return pl.pallas_call(
        body,
        name=name,
        grid=(t // tm,),
        in_specs=[pl.BlockSpec((tm, d), row), pl.BlockSpec((tm, d), row)],
        out_specs=[pl.BlockSpec((tm, d), row), pl.BlockSpec((8, 128), lambda i: (0, 0))],
        out_shape=[jax.ShapeDtypeStruct((t, d), F32), jax.ShapeDtypeStruct((8, 128), F32)],
        compiler_params=_params("arbitrary"),
    )(y, target)


FFN_COLS = FFN_HIDDEN // 2


def _ffn_in_swiglu(xb, wu, wg, name, tm=512):
    t, d = xb.shape
    tm = min(tm, t)

    def body(x_ref, wu_ref, wg_ref, a_ref, u_ref, g_ref):
        x = x_ref[...]
        u = _dot(x, wu_ref[...])
        g = _dot(x, wg_ref[...])
        u_ref[...] = u
        g_ref[...] = g
        a_ref[...] = (g * _sigmoid(g) * u).astype(BF16)

    wspec = pl.BlockSpec((d, FFN_COLS), lambda j, i: (0, j))
    out = pl.BlockSpec((tm, FFN_COLS), lambda j, i: (i, j))
    return pl.pallas_call(
        body,
        name=name,
        grid=(FFN_HIDDEN // FFN_COLS, t // tm),
        in_specs=[pl.BlockSpec((tm, d), lambda j, i: (i, 0)), wspec, wspec],
        out_specs=[out, out, out],
        out_shape=[jax.ShapeDtypeStruct((t, FFN_HIDDEN), BF16), jax.ShapeDtypeStruct((t, FFN_HIDDEN), F32),
                   jax.ShapeDtypeStruct((t, FFN_HIDDEN), F32)],
        compiler_params=_params("parallel", "parallel"),
    )(xb, wu, wg)


def _ffn_out_dx_swiglu(dzb, w_ff_out, u, g, name, tm=512):
    t, d = dzb.shape
    tm = min(tm, t)

    def body(dz_ref, w_ref, u_ref, g_ref, du_ref, dg_ref):
        da = _dot(dz_ref[...], w_ref[...], "nt")
        gv = g_ref[...]
        sg = _sigmoid(gv)
        du_ref[...] = (da * gv * sg).astype(BF16)
        dg_ref[...] = (da * u_ref[...] * (sg * (1.0 + gv * (1.0 - sg)))).astype(BF16)

    blk = pl.BlockSpec((tm, FFN_COLS), lambda j, i: (i, j))
    return pl.pallas_call(
        body,
        name=name,
        grid=(FFN_HIDDEN // FFN_COLS, t // tm),
        in_specs=[pl.BlockSpec((tm, d), lambda j, i: (i, 0)), pl.BlockSpec((FFN_COLS, d), lambda j, i: (j, 0)), blk, blk],
        out_specs=[blk, blk],
        out_shape=[jax.ShapeDtypeStruct((t, FFN_HIDDEN), BF16)] * 2,
        compiler_params=_params("parallel", "parallel"),
    )(dzb, w_ff_out, u, g)


def _merge_fwd(ya, yb, wpa, wpb, rest, name, tm=512):
    t = ya.shape[0]
    tm = min(tm, t)

    def body(ya_ref, yb_ref, wa_ref, wb_ref, ga_ref, gb_ref, o_ref):
        pa = _dot(ya_ref[...], wa_ref[...])
        pb = _dot(yb_ref[...], wb_ref[...])
        o_ref[...] = (_sigmoid(ga_ref[...]) * pa + _sigmoid(gb_ref[...]) * pb).astype(BF16)

    row = lambda i: (i, 0)
    fix = lambda i: (0, 0)
    return pl.pallas_call(
        body,
        name=name,
        grid=(t // tm,),
        in_specs=[pl.BlockSpec((tm, A_WIDTH), row), pl.BlockSpec((tm, B_WIDTH), row),
                  pl.BlockSpec((A_WIDTH, D_MODEL), fix), pl.BlockSpec((B_WIDTH, D_MODEL), fix),
                  pl.BlockSpec((tm, D_MODEL), lambda i: (i, 0)), pl.BlockSpec((tm, D_MODEL), lambda i: (i, 1))],
        out_specs=pl.BlockSpec((tm, D_MODEL), row),
        out_shape=jax.ShapeDtypeStruct((t, D_MODEL), BF16),
        compiler_params=_params("parallel"),
    )(ya, yb, wpa, wpb, rest, rest)


def _merge_bwd(dzb, w_out, ya, yb, wpa, wpb, rest, name, tm=512):
    t = ya.shape[0]
    tm = min(tm, t)

    def body(dz_ref, wo_ref, ya_ref, yb_ref, wa_ref, wb_ref, ga_ref, gb_ref, dg_ref, dpa_ref, dpb_ref, dya_ref,
             dyb_ref):
        dm_v = _dot(dz_ref[...], wo_ref[...], "nt")
        pa = _dot(ya_ref[...], wa_ref[...])
        pb = _dot(yb_ref[...], wb_ref[...])
        sa = _sigmoid(ga_ref[...])
        sb = _sigmoid(gb_ref[...])
        dg_ref[:, :D_MODEL] = (dm_v * pa * sa * (1.0 - sa)).astype(BF16)
        dg_ref[:, D_MODEL:] = (dm_v * pb * sb * (1.0 - sb)).astype(BF16)
        dpa = (dm_v * sa).astype(BF16)
        dpb = (dm_v * sb).astype(BF16)
        dpa_ref[...] = dpa
        dpb_ref[...] = dpb
        dya_ref[...] = _dot(dpa, wa_ref[...], "nt").astype(BF16)
        dyb_ref[...] = _dot(dpb, wb_ref[...], "nt")

    row = lambda i: (i, 0)
    fix = lambda i: (0, 0)
    return pl.pallas_call(
        body,
        name=name,
        grid=(t // tm,),
        in_specs=[pl.BlockSpec((tm, D_MODEL), row), pl.BlockSpec((D_MODEL, D_MODEL), fix),
                  pl.BlockSpec((tm, A_WIDTH), row), pl.BlockSpec((tm, B_WIDTH), row),
                  pl.BlockSpec((A_WIDTH, D_MODEL), fix), pl.BlockSpec((B_WIDTH, D_MODEL), fix),
                  pl.BlockSpec((tm, D_MODEL), lambda i: (i, 0)), pl.BlockSpec((tm, D_MODEL), lambda i: (i, 1))],
        out_specs=[pl.BlockSpec((tm, 2 * D_MODEL), row), pl.BlockSpec((tm, D_MODEL), row),
                   pl.BlockSpec((tm, D_MODEL), row), pl.BlockSpec((tm, A_WIDTH), row), pl.BlockSpec((tm, B_WIDTH), row)],
        out_shape=[jax.ShapeDtypeStruct((t, 2 * D_MODEL), BF16), jax.ShapeDtypeStruct((t, D_MODEL), BF16),
                   jax.ShapeDtypeStruct((t, D_MODEL), BF16), jax.ShapeDtypeStruct((t, A_WIDTH), BF16),
                   jax.ShapeDtypeStruct((t, B_WIDTH), F32)],
        compiler_params=_params("parallel"),
    )(dzb, w_out, ya, yb, wpa, wpb, rest, rest)


FA_BLOCK = 4224 // 128 - 1


def _tri(n, lower):
    r = lax.broadcasted_iota(jnp.int32, (n, n), 0)
    c = lax.broadcasted_iota(jnp.int32, (n, n), 1)
    return jnp.where((r >= c) if lower else (r <= c), 1.0, 0.0).astype(F32)


def _head_spread(expand):
    shape = (128, A_WIDTH) if expand else (A_WIDTH, 128)
    r = lax.broadcasted_iota(jnp.int32, shape, 0)
    c = lax.broadcasted_iota(jnp.int32, shape, 1)
    hit = ((c >= 64 * r) & (c < 64 * r + 64)) if expand else (r == 64 * c)
    return jnp.where(hit, 1.0, 0.0).astype(F32)


def _fox_gate_fwd(rest, bf, name, tb=512):
    t = rest.shape[0]
    tb = min(tb, t)

    def body(fa_ref, bf_ref, f_ref, fc_ref, carry):
        @pl.when(pl.program_id(0) == 0)
        def _():
            carry[...] = jnp.zeros_like(carry)

        z = fa_ref[...] + bf_ref[...]
        logf = jnp.minimum(z, 0.0) - jnp.log(1.0 + jnp.exp(-jnp.abs(z)))
        f = _dot_hi(_tri(tb, True), logf) + carry[...]
        f_ref[...] = f
        fc_ref[...] = _dot_hi(f, _head_spread(True), exact="b")
        carry[...] = f[tb - 1:tb, :]

    return pl.pallas_call(
        body,
        name=name,
        grid=(t // tb,),
        in_specs=[pl.BlockSpec((tb, 128), lambda i: (i, FA_BLOCK)), pl.BlockSpec((1, 128), lambda i: (0, 0))],
        out_specs=[pl.BlockSpec((tb, 128), lambda i: (i, 0)), pl.BlockSpec((tb, A_WIDTH), lambda i: (i, 0))],
        out_shape=[jax.ShapeDtypeStruct((t, 128), F32), jax.ShapeDtypeStruct((t, A_WIDTH), F32)],
        scratch_shapes=[pltpu.VMEM((1, 128), F32)],
        compiler_params=_params("arbitrary"),
    )(rest, bf)


def _fox_gate_bwd(rsum, csum, rest, bf, name, tb=512):
    t = rest.shape[0]
    tb = min(tb, t)
    nb = t // tb

    def body(rs_ref, cs_ref, fa_ref, bf_ref, dfa_ref, dbf_ref, carry):
        @pl.when(pl.program_id(0) == 0)
        def _():
            carry[...] = jnp.zeros_like(carry)

        d_f = rs_ref[...] - _dot_hi(cs_ref[...], _head_spread(False), exact="b")
        dlogf = _dot_hi(_tri(tb, False), d_f) + carry[...]
        carry[...] = dlogf[0:1, :]
        z = fa_ref[...] + bf_ref[...]
        dz = dlogf * _sigmoid(-z)
        dfa_ref[...] = dz.astype(BF16)
        part = jnp.sum(dz, axis=0, keepdims=True)

        @pl.when(pl.program_id(0) == 0)
        def _():
            dbf_ref[...] = part

        @pl.when(pl.program_id(0) > 0)
        def _():
            dbf_ref[...] += part

    return pl.pallas_call(
        body,
        name=name,
        grid=(nb,),
        in_specs=[pl.BlockSpec((tb, 128), lambda i: (nb - 1 - i, 0)),
                  pl.BlockSpec((tb, A_WIDTH), lambda i: (nb - 1 - i, 0)),
                  pl.BlockSpec((tb, 128), lambda i: (nb - 1 - i, FA_BLOCK)),
                  pl.BlockSpec((1, 128), lambda i: (0, 0))],
        out_specs=[pl.BlockSpec((tb, 128), lambda i: (nb - 1 - i, 0)), pl.BlockSpec((1, 128), lambda i: (0, 0))],
        out_shape=[jax.ShapeDtypeStruct((t, 128), BF16), jax.ShapeDtypeStruct((1, 128), F32)],
        scratch_shapes=[pltpu.VMEM((1, 128), F32)],
        compiler_params=_params("arbitrary"),
    )(rsum, csum, rest, bf)


ATT_BLOCK = 512


def _head_mask(shape, j):
    lane = lax.broadcasted_iota(jnp.int32, shape, 1)
    return (lane < 64) if j == 0 else (lane >= 64)


def _aug_lanes(tb, j):
    lane = lax.broadcasted_iota(jnp.int32, (tb, 128), 1)
    own = (lane < 64) if j == 0 else (lane >= 64)
    return own, lane - 64 * (1 - j)


def _aug_query(own, li, q, pieces):
    h, m, l = pieces
    one, zero = jnp.ones_like(h), jnp.zeros_like(h)
    spare = jnp.where(li == 0, h, jnp.where(li == 1, m, jnp.where(li == 2, l, jnp.where(li < 6, one, zero))))
    return jnp.where(own, q, spare)


def _fox_prep_fwd(qkv, fcol, name, tb=512):
    t = qkv.shape[0]
    tb = min(tb, t)

    def body(q_ref, k_ref, v_ref, fc_ref, qa_ref, ka_ref, va_ref, qn_ref, kn_ref):
        pieces = _pieces(pltpu.roll(fc_ref[...], 64, 1))
        h, m, l = pieces
        q, k, v = q_ref[...], k_ref[...], v_ref[...]
        first = _head_mask((tb, 128), 0)
        for nrm_ref, x in ((qn_ref, q.astype(F32)), (kn_ref, k.astype(F32))):
            n0 = jnp.max(jnp.sum(jnp.where(first, x * x, 0.0), axis=1, keepdims=True))
            n1 = jnp.max(jnp.sum(jnp.where(first, 0.0, x * x), axis=1, keepdims=True))
            nrm_ref[...] = jnp.where(_head_mask((8, 128), 0), n0, n1)
        one, zero = jnp.ones_like(h), jnp.zeros_like(h)
        for j in (0, 1):
            own, li = _aug_lanes(tb, j)
            cols = slice(128 * j, 128 * (j + 1))
            qa_ref[:, cols] = _aug_query(own, li, q * 0.125, pieces)
            ks = jnp.where(li < 3, one, jnp.where(li == 3, -h, jnp.where(li == 4, -m, jnp.where(li == 5, -l, zero))))
            ka_ref[:, cols] = jnp.where(own, k, ks)
            va_ref[:, cols] = jnp.where(own, v, one)

    blk = pl.BlockSpec((tb, 256), lambda i, h: (i, h))
    nrm = pl.BlockSpec((None, None, 8, 128), lambda i, h: (i, h, 0, 0))
    return pl.pallas_call(
        body, name=name, grid=(t // tb, 4),
        in_specs=[pl.BlockSpec((tb, 128), lambda i, h: (i, h)), pl.BlockSpec((tb, 128), lambda i, h: (i, 4 + h)),
                  pl.BlockSpec((tb, 128), lambda i, h: (i, 8 + h)), pl.BlockSpec((tb, 128), lambda i, h: (i, h))],
        out_specs=[blk, blk, blk, nrm, nrm],
        out_shape=[jax.ShapeDtypeStruct((t, 2 * A_WIDTH), BF16)] * 3
        + [jax.ShapeDtypeStruct((t // tb, 4, 8, 128), F32)] * 2,
        compiler_params=_params("parallel", "parallel"),
    )(qkv, qkv, qkv, fcol)


def _fox_prep_bwd(qkv, fcol, lse, do, o, name, tb=512):
    t = qkv.shape[0]
    tb = min(tb, t)

    def body(q_ref, fc_ref, lse_ref, do_ref, o_ref, qb_ref, dob_ref):
        pieces = _pieces(pltpu.roll(fc_ref[...] - lse_ref[...], 64, 1))
        q = q_ref[...] * 0.125
        do_v = do_ref[...]
        prod = do_v.astype(F32) * o_ref[...].astype(F32)
        for j in (0, 1):
            own, li = _aug_lanes(tb, j)
            cols = slice(128 * j, 128 * (j + 1))
            qb_ref[:, cols] = _aug_query(own, li, q, pieces)
            delta = jnp.sum(jnp.where(own, prod, 0.0), axis=1, keepdims=True)
            h, m, l = _pieces(jnp.broadcast_to(delta, (tb, 128)))
            ds = jnp.where(li == 0, -h, jnp.where(li == 1, -m, jnp.where(li == 2, -l, jnp.zeros_like(h))))
            dob_ref[:, cols] = jnp.where(own, do_v, ds)

    pair = pl.BlockSpec((tb, 128), lambda i, h: (i, h))
    blk = pl.BlockSpec((tb, 256), lambda i, h: (i, h))
    return pl.pallas_call(
        body, name=name, grid=(t // tb, 4),
        in_specs=[pair, pair, pair, pair, pair],
        out_specs=[blk, blk],
        out_shape=[jax.ShapeDtypeStruct((t, 2 * A_WIDTH), BF16)] * 2,
        compiler_params=_params("parallel", "parallel"),
    )(qkv, fcol, lse, do, o)


def _tile_mask(n, transposed):
    r = lax.broadcasted_iota(jnp.int32, (n, n), 0)
    c = lax.broadcasted_iota(jnp.int32, (n, n), 1)
    return (c >= r) if transposed else (r >= c)


UNDERFLOW = -110.0


def _fox_block_ranges(qn, kn, fcum):
    t = fcum.shape[0]
    blk = min(ATT_BLOCK, t)
    nb = t // blk
    q2 = jnp.max(qn[:, :, 0, ::64].reshape(-1, A_HEADS), axis=0)
    k2 = jnp.max(kn[:, :, 0, ::64].reshape(-1, A_HEADS), axis=0)
    bound = 2.0 * jnp.sqrt(q2 * k2) * 0.125
    f = fcum[:, :A_HEADS]
    first = f[0::blk].T
    last = f[blk - 1::blk].T
    dead = (bound[:, None, None] + first[:, :, None] - last[:, None, :]) < UNDERFLOW
    qi = jnp.arange(nb)[None, :, None]
    kj = jnp.arange(nb)[None, None, :]
    dead = dead & (kj < qi)
    kstart = jnp.sum(dead, axis=2).astype(jnp.int32)
    qend = (kj[0] + jnp.sum((~dead) & (qi > kj), axis=1)).astype(jnp.int32)
    return kstart.reshape(-1), qend.reshape(-1)


def _fox_fwd(qa, ka, va, kstart, name):
    t = qa.shape[0]
    bq = min(ATT_BLOCK, t)
    nq = t // bq

    def body(ks_ref, q_ref, k_ref, v_ref, o_ref, lse_ref):
        i = pl.program_id(1)
        hp = pl.program_id(0)
        k0 = [ks_ref[(2 * hp + j) * nq + i] for j in (0, 1)]
        both0 = jnp.maximum(k0[0], k0[1])

        def head(j, kb, m, acc, masked):
            rows = pl.ds(pl.multiple_of(kb * bq, bq), bq)
            cols = slice(128 * j, 128 * (j + 1))
            s = _dot(q_ref[:, cols], k_ref[rows, cols], "nt")
            if masked:
                s = jnp.where(_tile_mask(bq, False), s, -jnp.inf)
            m_new = jnp.maximum(m, jnp.max(s, axis=1, keepdims=True))
            return m_new, jnp.exp(m - m_new) * acc + _dot(jnp.exp(s - m_new), v_ref[rows, cols])

        def pair(kb, carry, masked):
            return head(0, kb, carry[0], carry[1], masked) + head(1, kb, carry[2], carry[3], masked)

        init = (jnp.full((bq, 1), -jnp.inf, F32), jnp.zeros((bq, 128), F32))
        alone = [lax.fori_loop(k0[j], both0, lambda kb, c, j=j: head(j, kb, c[0], c[1], False), init) for j in (0, 1)]
        carry = lax.fori_loop(both0, i, lambda kb, c: pair(kb, c, False), alone[0] + alone[1])
        carry = pair(i, carry, True)
        outs = []
        for j in (0, 1):
            m, acc = carry[2 * j], carry[2 * j + 1]
            spare = 64 * (1 - j)
            l = acc[:, spare:spare + 1]
            outs.append((acc / l, m + jnp.log(l)))
        msk = _head_mask((bq, 128), 0)
        o_ref[...] = jnp.where(msk, outs[0][0], outs[1][0]).astype(BF16)
        lse_ref[...] = jnp.where(msk, outs[0][1], outs[1][1])

    res = pl.BlockSpec((t, 256), lambda h, i, tbl: (0, h))
    out = pl.BlockSpec((bq, 128), lambda h, i, tbl: (i, h))
    return pl.pallas_call(
        body,
        name=name,
        grid_spec=pltpu.PrefetchScalarGridSpec(
            num_scalar_prefetch=1, grid=(4, nq),
            in_specs=[pl.BlockSpec((bq, 256), lambda h, i, tbl: (i, h)), res, res],
            out_specs=[out, out]),
        out_shape=[jax.ShapeDtypeStruct((t, A_WIDTH), BF16), jax.ShapeDtypeStruct((t, A_WIDTH), F32)],
        compiler_params=_params("parallel", "parallel"),
    )(kstart, qa, ka, va)


def _fox_bwd(qb, ka, va, dob, qend, name):
    t = qb.shape[0]
    bk = min(ATT_BLOCK, t)
    nk = t // bk

    def body(qe_ref, k_ref, v_ref, q_hbm, do_hbm, dk_ref, dv_ref, cs_ref, dq_hbm, q_scr, do_scr, dq_scr, sems):
        jb = pl.program_id(1)
        hp = pl.program_id(0)
        pair_cols = pl.ds(pl.multiple_of(hp * 256, 256), 256)

        @pl.when(jb == 0)
        def _():
            loads = [pltpu.make_async_copy(q_hbm.at[:, pair_cols], q_scr, sems.at[0]),
                     pltpu.make_async_copy(do_hbm.at[:, pair_cols], do_scr, sems.at[1])]
            for cp in loads:
                cp.start()
            dq_scr[...] = jnp.zeros_like(dq_scr)
            for cp in loads:
                cp.wait()

        i1 = [qe_ref[(2 * hp + j) * nk + jb] + 1 for j in (0, 1)]
        both1 = jnp.minimum(i1[0], i1[1])

        def head(j, ib, dk_acc, dv_acc, masked):
            rows = pl.ds(pl.multiple_of(ib * bk, bk), bk)
            cols = slice(128 * j, 128 * (j + 1))
            qs = q_scr[rows, cols]
            dos = do_scr[rows, cols]
            kj = k_ref[:, cols]
            st = _dot(kj, qs, "nt")
            if masked:
                st = jnp.where(_tile_mask(bk, True), st, -jnp.inf)
            pt = jnp.exp(st)
            dst = (pt * _dot(v_ref[:, cols], dos, "nt")).astype(BF16)
            dq_scr[rows, cols] += _dot(dst, kj, "tn")
            return dk_acc + _dot(dst, qs), dv_acc + _dot(pt, dos)

        def pair(ib, carry, masked):
            return head(0, ib, carry[0], carry[1], masked) + head(1, ib, carry[2], carry[3], masked)

        carry = pair(jb, (jnp.zeros((bk, 128), F32),) * 4, True)
        carry = lax.fori_loop(jb + 1, both1, lambda ib, c: pair(ib, c, False), carry)
        alone = [lax.fori_loop(jnp.maximum(both1, jb + 1), i1[j],
                               lambda ib, c, j=j: head(j, ib, c[0], c[1], False), carry[2 * j:2 * j + 2])
                 for j in (0, 1)]
        carry = alone[0] + alone[1]
        outs = []
        for j in (0, 1):
            spare = 64 * (1 - j)
            dk_acc, dv_acc = carry[2 * j], carry[2 * j + 1]
            outs.append((dk_acc, dv_acc, dk_acc[:, spare + 3:spare + 4]))
        msk = _head_mask((bk, 128), 0)
        dk_ref[...] = jnp.where(msk, outs[0][0], outs[1][0]).astype(BF16)
        dv_ref[...] = jnp.where(msk, outs[0][1], outs[1][1]).astype(BF16)
        cs_ref[...] = jnp.where(msk, outs[0][2], outs[1][2])

        @pl.when(jb == nk - 1)
        def _():
            store = pltpu.make_async_copy(dq_scr, dq_hbm.at[:, pair_cols], sems.at[2])
            store.start()
            store.wait()

    blk = pl.BlockSpec((bk, 256), lambda h, i, tbl: (i, h))
    out = pl.BlockSpec((bk, 128), lambda h, i, tbl: (i, h))
    return pl.pallas_call(
        body,
        name=name,
        grid_spec=pltpu.PrefetchScalarGridSpec(
            num_scalar_prefetch=1, grid=(4, nk), in_specs=[blk, blk, _ANY, _ANY], out_specs=[out, out, out, _ANY],
            scratch_shapes=[pltpu.VMEM((t, 256), BF16), pltpu.VMEM((t, 256), BF16), pltpu.VMEM((t, 256), F32),
                            pltpu.SemaphoreType.DMA((3,))]),
        out_shape=[jax.ShapeDtypeStruct((t, A_WIDTH), BF16), jax.ShapeDtypeStruct((t, A_WIDTH), BF16),
                   jax.ShapeDtypeStruct((t, A_WIDTH), F32), jax.ShapeDtypeStruct((t, 2 * A_WIDTH), F32)],
        compiler_params=_params("arbitrary", "arbitrary"),
    )(qend, ka, va, qb, dob)


HG_ROWS = 256


def _hg_gates(hb_ref, rows, lbv):
    qb = hb_ref[rows, 0:B_WIDTH]
    fb = hb_ref[rows, B_WIDTH:2 * B_WIDTH]
    v = hb_ref[rows, 2 * B_WIDTH:3 * B_WIDTH]
    gb = hb_ref[rows, 3 * B_WIDTH:4 * B_WIDTH]
    sg = _sigmoid(fb)
    f = lbv + (1.0 - lbv) * sg
    sq = _sigmoid(qb)
    return qb, sq, qb * sq, sg, f, 1.0 - f, jnp.log(f), v, gb


def _hg_intra_factors(q, k, b):
    fac = []
    for i in range(CHUNK // SUB):
        bi = b[SUB * i:SUB * i + 1, :]
        eq = jnp.exp(b[SUB * i:SUB * (i + 1), :] - bi)
        ek = jnp.exp(jnp.minimum(bi - b, EXP_CLAMP))
        fac.append((eq, ek, q[SUB * i:SUB * (i + 1), :] * eq, k * ek))
    return fac


def _causal(n):
    r = lax.broadcasted_iota(jnp.int32, (n, n), 0)
    c = lax.broadcasted_iota(jnp.int32, (n, n), 1)
    return r >= c


def _hgrn_fwd(rest, lb, ng, name):
    t = rest.shape[0]
    bt = min(HG_ROWS, t)
    ncb = bt // CHUNK

    def body(hb_ref, lb_ref, ng_ref, y_ref, o_ref, st_ref, s_scr):
        @pl.when(pl.program_id(0) == 0)
        def _():
            s_scr[...] = jnp.zeros_like(s_scr)

        tril = _tri(CHUNK, True)
        causal = _causal(CHUNK)
        ones = jnp.ones((CHUNK, HD), F32)

        def chunk(c, carry):
            rows = pl.ds(pl.multiple_of(c * CHUNK, CHUNK), CHUNK)
            _, _, q_all, _, _, k_all, g_all, v_all, gb_all = _hg_gates(hb_ref, rows, lb_ref[...])
            b_all = _dot_hi(tril, g_all)
            qd_all = q_all * jnp.exp(b_all)
            kd_all = k_all * jnp.exp(b_all[CHUNK - 1:CHUNK, :] - b_all)
            eb_all = jnp.exp(_dot_hi(g_all, ones, "tn", exact="b"))
            sgb_all = _sigmoid(gb_all)
            for h in range(B_HEADS):
                cols = slice(h * HD, (h + 1) * HD)
                v = v_all[:, cols]
                s0 = s_scr[h]
                st_ref[c, h] = s0
                o = _dot(qd_all[:, cols], s0)
                fac = _hg_intra_factors(q_all[:, cols], k_all[:, cols], b_all[:, cols])
                a = jnp.concatenate([_dot(qe, ke, "nt") for _, _, qe, ke in fac], axis=0)
                o = o + _dot(jnp.where(causal, a, 0.0), v)
                s_scr[h] = eb_all[h * HD:(h + 1) * HD, :] * s0 + _dot(kd_all[:, cols], v, "tn")
                r = lax.rsqrt(jnp.mean(o * o, axis=-1, keepdims=True) + RMS_EPS)
                o_ref[rows, cols] = o
                y_ref[rows, cols] = (o * r * ng_ref[...] * sgb_all[:, cols]).astype(BF16)
            return carry

        lax.fori_loop(0, ncb, chunk, 0)

    return pl.pallas_call(
        body,
        name=name,
        grid=(t // bt,),
        in_specs=[pl.BlockSpec((bt, 4 * B_WIDTH), lambda i: (i, 1)), pl.BlockSpec((1, B_WIDTH), lambda i: (0, 0)),
                  pl.BlockSpec((1, HD), lambda i: (0, 0))],
        out_specs=[pl.BlockSpec((bt, B_WIDTH), lambda i: (i, 0)), pl.BlockSpec((bt, B_WIDTH), lambda i: (i, 0)),
                   pl.BlockSpec((ncb, B_HEADS, HD, HD), lambda i: (i, 0, 0, 0))],
        out_shape=[jax.ShapeDtypeStruct((t, B_WIDTH), BF16), jax.ShapeDtypeStruct((t, B_WIDTH), F32),
                   jax.ShapeDtypeStruct((t // CHUNK, B_HEADS, HD, HD), F32)],
        scratch_shapes=[pltpu.VMEM((B_HEADS, HD, HD), F32)],
        compiler_params=_params("arbitrary"),
    )(rest, lb, ng)


def _hgrn_bwd(dy, rest, o_saved, states, lb, ng, name):
    t = rest.shape[0]
    bt = min(HG_ROWS, t)
    ncb = bt // CHUNK
    nb = t // bt

    def body(dy_ref, hb_ref, o_ref, st_ref, lb_ref, ng_ref, dh_ref, dlb_ref, dng_ref, ds_scr):
        @pl.when(pl.program_id(0) == 0)
        def _():
            ds_scr[...] = jnp.zeros_like(ds_scr)
            dlb_ref[...] = jnp.zeros_like(dlb_ref)
            dng_ref[...] = jnp.zeros_like(dng_ref)

        tril = _tri(CHUNK, True)
        triu = _tri(CHUNK, False)
        causal = _causal(CHUNK)
        ones = jnp.ones((CHUNK, HD), F32)
        ones8 = jnp.ones((8, HD), F32)
        last_row = lax.broadcasted_iota(jnp.int32, (CHUNK, B_WIDTH), 0) == CHUNK - 1

        def chunk(cc, carry):
            dng_acc, dlb_acc = carry
            c = ncb - 1 - cc
            rows = pl.ds(pl.multiple_of(c * CHUNK, CHUNK), CHUNK)
            lbv = lb_ref[...]
            qb, sq, q_all, sg, f, k_all, g_all, v_all, gb = _hg_gates(hb_ref, rows, lbv)
            b_all = _dot_hi(tril, g_all)
            ebt_all = jnp.exp(b_all)
            blast = b_all[CHUNK - 1:CHUNK, :]
            ekd_all = jnp.exp(blast - b_all)
            eb_all = jnp.exp(_dot_hi(g_all, ones, "tn", exact="b"))
            sgb = _sigmoid(gb)
            dy_all = dy_ref[rows, :].astype(F32)
            don_all = dy_all * sgb
            ngv = ng_ref[...]
            dq_l, dk_l, dks_l, dv_l, on_l, prod_l = [], [], [], [], [], []
            for h in range(B_HEADS):
                cols = slice(h * HD, (h + 1) * HD)
                q, k, v = q_all[:, cols], k_all[:, cols], v_all[:, cols]
                o = o_ref[rows, cols]
                don = don_all[:, cols]
                r = lax.rsqrt(jnp.mean(o * o, axis=-1, keepdims=True) + RMS_EPS)
                on_l.append(o * r * ngv)
                dng_acc = dng_acc + jnp.sum(don * o * r, axis=0, keepdims=True)
                doh = don * ngv
                do = r * (doh - o * (r * r) * jnp.mean(doh * o, axis=-1, keepdims=True))
                ebt, ekd = ebt_all[:, cols], ekd_all[:, cols]
                s0 = st_ref[c, h]
                ds1 = ds_scr[h]
                fac = _hg_intra_factors(q, k, b_all[:, cols])
                a = jnp.concatenate([_dot(qe, ke, "nt") for _, _, qe, ke in fac], axis=0)
                a = jnp.where(causal, a, 0.0)
                da = jnp.where(causal, _dot(do, v, "nt"), 0.0)
                dv_l.append(_dot(a, do, "tn") + _dot(k * ekd, ds1))
                dq = ebt * _dot(do, s0, "nt")
                dq_l.append(dq + jnp.concatenate(
                    [eq * _hdot(da[SUB * i:SUB * (i + 1), :], ke) for i, (eq, _, _, ke) in enumerate(fac)], axis=0))
                dk_state = ekd * _dot(v, ds1, "nt")
                dk = dk_state
                for i, (_, ek, qe, _) in enumerate(fac):
                    dk = dk + ek * _hdot(da[SUB * i:SUB * (i + 1), :], qe, "tn")
                dk_l.append(dk)
                dks_l.append(dk_state)
                prod_l.append(ds1 * s0)
                ds_scr[h] = _dot(q * ebt, do, "tn") + eb_all[h * HD:(h + 1) * HD, :] * ds1
            dq_all, dk_all = jnp.concatenate(dq_l, axis=1), jnp.concatenate(dk_l, axis=1)
            extra = jnp.exp(blast) * _dot_hi(ones8, jnp.concatenate(prod_l, axis=0), "nt")[0:1, :] \
                + jnp.sum(k_all * jnp.concatenate(dks_l, axis=1), axis=0, keepdims=True)
            db = q_all * dq_all - k_all * dk_all + jnp.where(last_row, extra, 0.0)
            df = _dot_hi(triu, db) / f - dk_all
            dlb_acc = dlb_acc + jnp.sum(df * (1.0 - sg), axis=0, keepdims=True)
            dh_ref[rows, 0:B_WIDTH] = (dq_all * (sq * (1.0 + qb * (1.0 - sq)))).astype(BF16)
            dh_ref[rows, B_WIDTH:2 * B_WIDTH] = (df * (1.0 - lbv) * sg * (1.0 - sg)).astype(BF16)
            dh_ref[rows, 2 * B_WIDTH:3 * B_WIDTH] = jnp.concatenate(dv_l, axis=1).astype(BF16)
            dh_ref[rows, 3 * B_WIDTH:4 * B_WIDTH] = (dy_all * jnp.concatenate(on_l, axis=1)
                                                     * sgb * (1.0 - sgb)).astype(BF16)
            return dng_acc, dlb_acc

        dng_sum, dlb_sum = lax.fori_loop(0, ncb, chunk, (jnp.zeros((1, HD), F32), jnp.zeros((1, B_WIDTH), F32)))
        dng_ref[...] += dng_sum
        dlb_ref[...] += dlb_sum

    rev = lambda i: (nb - 1 - i, 0)
    return pl.pallas_call(
        body,
        name=name,
        grid=(nb,),
        in_specs=[pl.BlockSpec((bt, B_WIDTH), rev), pl.BlockSpec((bt, 4 * B_WIDTH), lambda i: (nb - 1 - i, 1)),
                  pl.BlockSpec((bt, B_WIDTH), rev),
                  pl.BlockSpec((ncb, B_HEADS, HD, HD), lambda i: (nb - 1 - i, 0, 0, 0)),
                  pl.BlockSpec((1, B_WIDTH), lambda i: (0, 0)), pl.BlockSpec((1, HD), lambda i: (0, 0))],
        out_specs=[pl.BlockSpec((bt, 4 * B_WIDTH), rev), pl.BlockSpec((1, B_WIDTH), lambda i: (0, 0)),
                   pl.BlockSpec((1, HD), lambda i: (0, 0))],
        out_shape=[jax.ShapeDtypeStruct((t, 4 * B_WIDTH), BF16), jax.ShapeDtypeStruct((1, B_WIDTH), F32),
                   jax.ShapeDtypeStruct((1, HD), F32)],
        scratch_shapes=[pltpu.VMEM((B_HEADS, HD, HD), F32)],
        compiler_params=_params("arbitrary"),
    )(dy, rest, o_saved, states, lb, ng)


def _axpy2(c0, a0, c1, a1, name, tm=512):
    t, d = a0.shape
    tm = min(tm, t)

    def body(a_ref, b_ref, o_ref):
        o_ref[...] = c0 * a_ref[...] + c1 * b_ref[...]

    row = lambda i: (i, 0)
    return pl.pallas_call(
        body, name=name, grid=(t // tm,),
        in_specs=[pl.BlockSpec((tm, d), row), pl.BlockSpec((tm, d), row)],
        out_specs=pl.BlockSpec((tm, d), row),
        out_shape=jax.ShapeDtypeStruct((t, d), F32),
        compiler_params=_params("parallel"),
    )(a0, a1)


def _split_w_in(w_in_l):
    wqkv = w_in_l[:, :3 * A_WIDTH]
    wfa = jnp.pad(w_in_l[:, 3 * A_WIDTH:3 * A_WIDTH + A_HEADS], ((0, 0), (0, 128 - A_HEADS)))
    whb = w_in_l[:, 3 * A_WIDTH + A_HEADS:3 * A_WIDTH + A_HEADS + 4 * B_WIDTH]
    wgt = w_in_l[:, 3 * A_WIDTH + A_HEADS + 4 * B_WIDTH:]
    return wqkv, jnp.concatenate([wgt, whb, wfa], axis=1)


def _merge_w_in_grad(dwall):
    o = 3 * A_WIDTH
    return jnp.concatenate([dwall[:, :o], dwall[:, o + 4096:o + 4096 + A_HEADS], dwall[:, o + 2048:o + 4096],
                            dwall[:, o:o + 2048]], axis=1)


def _layer_fwd(x, xb, w, sp, l):
    t = x.shape[0]
    n = f"l{l}_"
    wqkv, wrest = _split_w_in(w["w_in"])
    qkv = _matmul(xb, wqkv, "nn", BF16, MM_ROWS, 768, D_MODEL, n + "proj_qkv")
    rest = _matmul(xb, wrest, "nn", F32, MM_ROWS, 1408, D_MODEL, n + "proj_rest")
    bf = jnp.pad(sp["b_fgate"], (0, 128 - A_HEADS)).reshape(1, 128)
    fcum, fcol = _fox_gate_fwd(rest, bf, n + "fox_gate_fwd")
    qa, ka, va, qn, kn = _fox_prep_fwd(qkv, fcol, n + "fox_prep_fwd")
    kstart, qend = _fox_block_ranges(qn, kn, fcum)
    ya, lse = _fox_fwd(qa, ka, va, kstart, n + "fox_fwd")
    lb = sp["lb"].reshape(1, B_WIDTH)
    ng = sp["norm_g"].reshape(1, HD)
    yb, ob, states = _hgrn_fwd(rest, lb, ng, n + "hgrn_fwd")
    merged = _merge_fwd(ya, yb, w["w_pa"], w["w_pb"], rest, n + "merge_fwd")
    x1, x1b, xh1, rs1 = _mm_res_ln(merged, w["w_out"], x, sp["ln1_g"], sp["ln1_b"], n + "out_ln1")
    wu, wg = w["w_ff_in"][:, :FFN_HIDDEN], w["w_ff_in"][:, FFN_HIDDEN:]
    a, hu, hg = _ffn_in_swiglu(x1b, wu, wg, n + "ffn_in_swiglu")
    x2, x2b, xh2, rs2 = _mm_res_ln(a, w["w_ff_out"], x1, sp["ln2_g"], sp["ln2_b"], n + "ffn_out_ln2")
    saved = dict(xb=xb, wqkv=wqkv, wrest=wrest, qkv=qkv, rest=rest, bf=bf, fcol=fcol, ka=ka, va=va, ya=ya, lse=lse,
                 kstart=kstart, qend=qend,
                 lb=lb, ng=ng, yb=yb, ob=ob, states=states, merged=merged, x1b=x1b, xh1=xh1, rs1=rs1, a=a,
                 wu=wu, wg=wg, hu=hu, hg=hg,
                 xh2=xh2, rs2=rs2)
    return x2, x2b, saved


def _layer_bwd(dys, coefs, w, sp, s, l):
    n = f"l{l}_"
    dz2, dz2b, dg2, db2 = _ln_bwd(dys, coefs, s["xh2"], s["rs2"], sp["ln2_g"], n + "ln2_bwd")
    du, dg = _ffn_out_dx_swiglu(dz2b, w["w_ff_out"], s["hu"], s["hg"], n + "ffn_out_dx_swiglu")
    d_wffout = _matmul(s["a"], dz2b, "tn", F32, 1408, 1024, DW_ROWS, n + "ffn_out_dw")
    dx1u = _matmul(du, s["wu"], "nt", F32, MM_ROWS, 1024, FFN_HIDDEN, n + "ffn_in_dx_u")
    dx1g = _matmul(dg, s["wg"], "nt", F32, MM_ROWS, 1024, FFN_HIDDEN, n + "ffn_in_dx_g")
    d_wffin = jnp.concatenate([_matmul(s["x1b"], du, "tn", F32, 1024, 1408, DW_ROWS, n + "ffn_in_dw_u"),
                               _matmul(s["x1b"], dg, "tn", F32, 1024, 1408, DW_ROWS, n + "ffn_in_dw_g")], axis=1)
    dz1, dz1b, dg1, db1 = _ln_bwd([dz2, dx1u, dx1g], [ALPHA, 1.0, 1.0], s["xh1"], s["rs1"], sp["ln1_g"],
                                  n + "ln1_bwd")
    d_wout = _matmul(s["merged"], dz1b, "tn", F32, 1024, 1024, DW_ROWS, n + "out_dw")
    dgates, dpa, dpb, dya, dyb = _merge_bwd(dz1b, w["w_out"], s["ya"], s["yb"], w["w_pa"], w["w_pb"], s["rest"],
                                  n + "merge_bwd")
    d_wpa = _matmul(s["ya"], dpa, "tn", F32, 512, 1024, DW_ROWS, n + "pa_dw")
    d_wpb = _matmul(s["yb"], dpb, "tn", F32, 512, 1024, DW_ROWS, n + "pb_dw")
    qb, dob = _fox_prep_bwd(s["qkv"], s["fcol"], s["lse"], dya, s["ya"], n + "fox_prep_bwd")
    dk, dv, csum, dq_raw = _fox_bwd(qb, s["ka"], s["va"], dob, s["qend"], n + "fox_bwd")
    t = dq_raw.shape[0]
    per_head = dq_raw.reshape(t, 4, 2, 2, 64)
    dq = (jnp.stack([per_head[:, :, 0, 0, :], per_head[:, :, 1, 1, :]], axis=2).reshape(t, A_WIDTH) * 0.125).astype(BF16)
    rsum = jnp.stack([per_head[:, :, 0, 1, 0], per_head[:, :, 1, 0, 0]], axis=2).reshape(t, A_HEADS)
    dfa, dbf = _fox_gate_bwd(jnp.pad(rsum, ((0, 0), (0, 128 - A_HEADS))), csum, s["rest"], s["bf"], n + "fox_gate_bwd")
    dhb, dlb, dng = _hgrn_bwd(dyb, s["rest"], s["ob"], s["states"], s["lb"], s["ng"], n + "hgrn_bwd")
    dproj = jnp.concatenate([dq, dk, dv, dgates, dhb, dfa], axis=1)
    wall = jnp.concatenate([s["wqkv"], s["wrest"]], axis=1)
    dxp = _matmul(dproj, wall, "nt", F32, MM_ROWS, 1024, 1920, n + "proj_dx")
    d_wall = _matmul(s["xb"], dproj, "tn", F32, 1024, 1152, DW_ROWS, n + "proj_dw")
    grads = dict(w_in=_merge_w_in_grad(d_wall), w_pa=d_wpa, w_pb=d_wpb, w_out=d_wout, w_ff_in=d_wffin,
                 w_ff_out=d_wffout, b_fgate=dbf[0, :A_HEADS], lb=dlb[0], norm_g=dng[0], ln1_g=dg1[0], ln1_b=db1[0],
                 ln2_g=dg2[0], ln2_b=db2[0])
    return [dz1, dxp], [ALPHA, 1.0], grads


def _lower_bounds(logits):
    sm = jax.nn.softmax(logits.astype(F32), axis=0)
    return jnp.cumsum(sm, axis=0) - sm[0:1]


def _local_step(x, target, wfull, small):
    lbs, lb_vjp = jax.vjp(_lower_bounds, small["hgrn_lb_logits"])
    h, hb = x, x.astype(BF16)
    saved, sps = [], []
    for l in range(DEPTH):
        sp = dict(b_fgate=small["b_fgate"][l], lb=lbs[l], norm_g=small["hgrn_norm_g"][l], ln1_g=small["ln1_g"][l],
                  ln1_b=small["ln1_b"][l], ln2_g=small["ln2_g"][l], ln2_b=small["ln2_b"][l])
        h, hb, s = _layer_fwd(h, hb, wfull[l], sp, l)
        saved.append(s)
        sps.append(sp)
    dy, lpart = _loss_head(h, target)
    dys, coefs = [dy], [1.0]
    grads = [None] * DEPTH
    for l in reversed(range(DEPTH)):
        dys, coefs, grads[l] = _layer_bwd(dys, coefs, wfull[l], sps[l], saved[l], l)
    grad_x = _axpy2(coefs[0], dys[0], coefs[1], dys[1], "grad_x")
    d_logits = lb_vjp(jnp.stack([grads[l]["lb"] for l in range(DEPTH)]))[0]
    return lpart[0, 0], grad_x, grads, d_logits


_BIG = [("w_in", "w_in", (D_MODEL, IN_TOTAL), 1), ("w_branch_a", "w_pa", (A_WIDTH, D_MODEL), 1),
        ("w_branch_b", "w_pb", (B_WIDTH, D_MODEL), 1), ("w_out", "w_out", (D_MODEL, D_MODEL), 0),
        ("w_ff_in", "w_ff_in", (D_MODEL, 2 * FFN_HIDDEN), 1), ("w_ff_out", "w_ff_out", (FFN_HIDDEN, D_MODEL), 0)]
_SMALL = [("b_fgate", A_HEADS), ("hgrn_lb_logits", B_WIDTH), ("hgrn_norm_g", HD), ("ln1_g", D_MODEL),
          ("ln1_b", D_MODEL), ("ln2_g", D_MODEL), ("ln2_b", D_MODEL)]
N_BIG = len(_BIG)
SMALL_ROWS = 80


def _by_chip(full, axis):
    if axis == 0:
        return full.reshape(N_CHIPS, full.shape[0] // N_CHIPS, full.shape[1])
    n = full.shape[1] // N_CHIPS
    return jnp.stack([full[:, q * n:(q + 1) * n] for q in range(N_CHIPS)])


def _from_chips(shards, axis):
    if axis == 0:
        return shards.reshape(N_CHIPS * shards.shape[1], shards.shape[2])
    return jnp.concatenate([shards[q] for q in range(N_CHIPS)], axis=1)


def _pack_small(per_name):
    flat = jnp.concatenate([per_name[name].reshape(-1) for name, _ in _SMALL])
    return jnp.pad(flat, (0, SMALL_ROWS * 128 - flat.shape[0])).reshape(SMALL_ROWS, 128)


def _unpack_small(slab):
    flat, out, r = slab.reshape(-1), {}, 0
    for name, n in _SMALL:
        out[name] = flat[r:r + DEPTH * n].reshape(DEPTH, n)
        r += DEPTH * n
    return out


_ANY = pl.BlockSpec(memory_space=pl.ANY)


def _place():
    return lax.axis_index("x"), lax.axis_index("y"), lax.axis_index("c")


def _other_chips(x, y):
    return [(1 - x, y), (x, 1 - y), (1 - x, 1 - y)]


def _chip_exchange(mine_of, out_refs, send_sems, recv_sems, local_sems):
    x, y, c = _place()
    q = 2 * x + y
    started = []
    for w, out_ref in enumerate(out_refs):
        local = pltpu.make_async_copy(mine_of(w, q), out_ref.at[q], local_sems.at[w])
        local.start()
        started.append(local)
    sends = []
    for k, (px, py) in enumerate(_other_chips(x, y)):
        for w, out_ref in enumerate(out_refs):
            cp = pltpu.make_async_remote_copy(src_ref=mine_of(w, 2 * px + py), dst_ref=out_ref.at[q],
                                              send_sem=send_sems.at[3 * w + k], recv_sem=recv_sems.at[3 * w + k],
                                              device_id=(px, py, c), device_id_type=MESH)
            cp.start()
            sends.append(cp)
    for k, (px, py) in enumerate(_other_chips(x, y)):
        for w, out_ref in enumerate(out_refs):
            pltpu.make_async_remote_copy(src_ref=mine_of(w, q), dst_ref=out_ref.at[2 * px + py],
                                         send_sem=send_sems.at[3 * w + k], recv_sem=recv_sems.at[3 * w + k],
                                         device_id=(px, py, c), device_id_type=MESH).wait_recv()
    for cp in sends:
        cp.wait_send()
    for local in started:
        local.wait()


def _sem_scratch(n):
    return [pltpu.SemaphoreType.DMA((3 * n,)), pltpu.SemaphoreType.DMA((3 * n,)), pltpu.SemaphoreType.DMA((n,))]


def _gather_weights(mine):
    n = len(mine)

    def body(*refs):
        in_refs, out_refs = refs[:n], refs[n:2 * n]
        send_sems, recv_sems, local_sems, pair_send, pair_recv = refs[2 * n:]
        x, y, c = _place()
        _chip_exchange(lambda w, q: in_refs[w].at[c], [o.at[c] for o in out_refs], send_sems, recv_sems, local_sems)
        sibling = (x, y, 1 - c)
        fwds = []
        for w, o in enumerate(out_refs):
            cp = pltpu.make_async_remote_copy(src_ref=o.at[c], dst_ref=o.at[c], send_sem=pair_send.at[w],
                                              recv_sem=pair_recv.at[w], device_id=sibling, device_id_type=MESH)
            cp.start()
            fwds.append(cp)
        for w, o in enumerate(out_refs):
            pltpu.make_async_remote_copy(src_ref=o.at[1 - c], dst_ref=o.at[1 - c], send_sem=pair_send.at[w],
                                         recv_sem=pair_recv.at[w], device_id=sibling, device_id_type=MESH).wait_recv()
        for cp in fwds:
            cp.wait_send()

    return pl.pallas_call(
        body, name="gather_weights", in_specs=[_ANY] * n, out_specs=[_ANY] * n,
        out_shape=[jax.ShapeDtypeStruct((DEPTH, N_CHIPS) + m.shape[1:], m.dtype) for m in mine],
        scratch_shapes=_sem_scratch(n) + [pltpu.SemaphoreType.DMA((n,)), pltpu.SemaphoreType.DMA((n,))],
    )(*mine)


def _pair_exchange(gs):
    n = len(gs)

    def body(*refs):
        g_refs, a_refs, send_sems, recv_sems = refs[:n], refs[n:2 * n], refs[2 * n], refs[2 * n + 1]
        x, y, c = _place()
        cps = []
        for w in range(n):
            cp = pltpu.make_async_remote_copy(src_ref=g_refs[w].at[1 - c], dst_ref=a_refs[w], send_sem=send_sems.at[w],
                                              recv_sem=recv_sems.at[w], device_id=(x, y, 1 - c), device_id_type=MESH)
            cp.start()
            cps.append(cp)
        for cp in cps:
            cp.wait()

    return pl.pallas_call(
        body, name="grad_pair_exchange", in_specs=[_ANY] * n, out_specs=[_ANY] * n,
        out_shape=[jax.ShapeDtypeStruct(g.shape[1:], g.dtype) for g in gs],
        scratch_shapes=[pltpu.SemaphoreType.DMA((n,)), pltpu.SemaphoreType.DMA((n,))],
    )(*gs)


def _row_block(rows):
    return math.gcd(rows, 256)


def _pair_sum(g, a, layer, name):
    _, nq, rows, cols = g.shape
    tb = _row_block(rows)

    def body(l_ref, g_ref, a_ref, o_ref):
        o_ref[...] = (g_ref[...] + a_ref[...]).astype(BF16)

    return pl.pallas_call(
        body, name=name,
        grid_spec=pltpu.PrefetchScalarGridSpec(
            num_scalar_prefetch=1, grid=(nq, rows // tb),
            in_specs=[pl.BlockSpec((None, None, tb, cols), lambda q, i, l_ref: (l_ref[0], q, i, 0)),
                      pl.BlockSpec((None, tb, cols), lambda q, i, l_ref: (q, i, 0))],
            out_specs=pl.BlockSpec((None, tb, cols), lambda q, i, l_ref: (q, i, 0))),
        out_shape=jax.ShapeDtypeStruct((nq, rows, cols), BF16),
        compiler_params=_params("parallel", "parallel"),
    )(layer.reshape(1).astype(jnp.int32), g, a)


def _shard_exchange(ps):
    n = len(ps)

    def body(*refs):
        p_refs, b_refs = refs[:n], refs[n:2 * n]
        send_sems, recv_sems, local_sems = refs[2 * n:]
        _chip_exchange(lambda w, q: p_refs[w].at[q], b_refs, send_sems, recv_sems, local_sems)

    return pl.pallas_call(
        body, name="grad_shard_exchange", in_specs=[_ANY] * n, out_specs=[_ANY] * n,
        out_shape=[jax.ShapeDtypeStruct(p.shape, p.dtype) for p in ps],
        scratch_shapes=_sem_scratch(n),
    )(*ps)


def _sum4(b, name):
    _, rows, cols = b.shape
    tb = _row_block(rows)

    def body(b_ref, o_ref):
        o_ref[...] = ((b_ref[0].astype(F32) + b_ref[1].astype(F32)) + b_ref[2].astype(F32)) + b_ref[3].astype(F32)

    return pl.pallas_call(
        body, name=name, grid=(rows // tb,),
        in_specs=[pl.BlockSpec((N_CHIPS, tb, cols), lambda i: (0, i, 0))],
        out_specs=pl.BlockSpec((tb, cols), lambda i: (i, 0)),
        out_shape=jax.ShapeDtypeStruct((rows, cols), F32),
        compiler_params=_params("parallel"),
    )(b)


def _result_exchange(gcs):
    n = len(gcs)

    def body(*refs):
        g_refs, o_refs, send_sems, recv_sems = refs[:n], refs[n:2 * n], refs[2 * n], refs[2 * n + 1]
        x, y, c = _place()
        cps = []
        for w in range(n):
            cp = pltpu.make_async_remote_copy(src_ref=g_refs[w], dst_ref=o_refs[w], send_sem=send_sems.at[w],
                                              recv_sem=recv_sems.at[w], device_id=(x, y, 1 - c), device_id_type=MESH)
            cp.start()
            cps.append(cp)
        for cp in cps:
            cp.wait()

    return pl.pallas_call(
        body, name="grad_result_exchange", in_specs=[_ANY] * n, out_specs=[_ANY] * n,
        out_shape=[jax.ShapeDtypeStruct(g.shape, g.dtype) for g in gcs],
        scratch_shapes=[pltpu.SemaphoreType.DMA((n,)), pltpu.SemaphoreType.DMA((n,))],
    )(*gcs)


def _allreduce_small(v):
    def body(v_ref, o_ref, buf, send_sems, recv_sems):
        x, y, c = _place()
        me = 4 * x + 2 * y + c
        buf[me] = v_ref[...]
        peers = []
        for k in range(1, N_DEV):
            px = 1 - x if k & 4 else x
            py = 1 - y if k & 2 else y
            pc = 1 - c if k & 1 else c
            peers.append((px, py, pc))
        sends = []
        for k, peer in enumerate(peers):
            cp = pltpu.make_async_remote_copy(src_ref=v_ref, dst_ref=buf.at[me], send_sem=send_sems.at[k],
                                              recv_sem=recv_sems.at[k], device_id=peer, device_id_type=MESH)
            cp.start()
            sends.append(cp)
        for k, (px, py, pc) in enumerate(peers):
            pltpu.make_async_remote_copy(src_ref=v_ref, dst_ref=buf.at[4 * px + 2 * py + pc], send_sem=send_sems.at[k],
                                         recv_sem=recv_sems.at[k], device_id=(px, py, pc),
                                         device_id_type=MESH).wait_recv()
        for cp in sends:
            cp.wait_send()
        acc = buf[0]
        for i in range(1, N_DEV):
            acc = acc + buf[i]
        o_ref[...] = acc

    vm = pl.BlockSpec(memory_space=pltpu.VMEM)
    return pl.pallas_call(
        body, name="small_allreduce", in_specs=[vm], out_specs=vm,
        out_shape=jax.ShapeDtypeStruct(v.shape, F32),
        scratch_shapes=[pltpu.VMEM((N_DEV,) + v.shape, F32), pltpu.SemaphoreType.DMA((N_DEV - 1,)),
                        pltpu.SemaphoreType.DMA((N_DEV - 1,))],
    )(v)


def _adam_update(w, g, m, v):
    nm = ADAM_B1 * m + (1.0 - ADAM_B1) * g
    nv = ADAM_B2 * v + (1.0 - ADAM_B2) * (g * g)
    m_hat = nm / (1.0 - ADAM_B1 ** ADAM_STEP)
    v_hat = nv / (1.0 - ADAM_B2 ** ADAM_STEP)
    return -ADAM_LR * (m_hat / (jnp.sqrt(v_hat) + ADAM_EPS) + ADAM_WD * w), nm, nv


def _adamw_small(w, g, m, v, name):
    def body(w_ref, g_ref, m_ref, v_ref, d_ref, nm_ref, nv_ref):
        d_ref[...], nm_ref[...], nv_ref[...] = _adam_update(w_ref[...], g_ref[...], m_ref[...], v_ref[...])

    vm = pl.BlockSpec(memory_space=pltpu.VMEM)
    return pl.pallas_call(
        body, name=name, in_specs=[vm] * 4, out_specs=[vm] * 3,
        out_shape=[jax.ShapeDtypeStruct(w.shape, F32)] * 3,
    )(w, g, m, v)


def _adamw_big(w, m, v, g_own, g_other, layer, name):
    _, rows, cols = w.shape
    tb = _row_block(rows)

    def body(l_ref, w_ref, m_ref, v_ref, go_ref, gx_ref, g_ref, d_ref, nm_ref, nv_ref):
        gv = jnp.where(pl.program_id(0) == l_ref[0], go_ref[...], gx_ref[...])
        g_ref[...] = gv
        d_ref[...], nm_ref[...], nv_ref[...] = _adam_update(w_ref[...], gv, m_ref[...], v_ref[...])

    per_layer = pl.BlockSpec((None, tb, cols), lambda l, i, l_ref: (l, i, 0))
    shared = pl.BlockSpec((tb, cols), lambda l, i, l_ref: (i, 0))
    return pl.pallas_call(
        body, name=name,
        grid_spec=pltpu.PrefetchScalarGridSpec(
            num_scalar_prefetch=1, grid=(DEPTH, rows // tb),
            in_specs=[per_layer, per_layer, per_layer, shared, shared], out_specs=[per_layer] * 4),
        out_shape=[jax.ShapeDtypeStruct(w.shape, F32)] * 4,
        compiler_params=_params("parallel", "parallel"),
    )(layer.reshape(1).astype(jnp.int32), w, m, v, g_own, g_other)


def kernel(x, w_in, b_fgate, hgrn_lb_logits, hgrn_norm_g, w_branch_a, w_branch_b, w_out, ln1_g, ln1_b, w_ff_in, w_ff_out, ln2_g, ln2_b, loss_target, m_w_in, m_b_fgate, m_hgrn_lb_logits, m_hgrn_norm_g, m_w_branch_a, m_w_branch_b, m_w_out, m_ln1_g, m_ln1_b, m_w_ff_in, m_w_ff_out, m_ln2_g, m_ln2_b, v_w_in, v_b_fgate, v_hgrn_lb_logits, v_hgrn_norm_g, v_w_branch_a, v_w_branch_b, v_w_out, v_ln1_g, v_ln1_b, v_w_ff_in, v_w_ff_out, v_ln2_g, v_ln2_b):
    weights = dict(w_in=w_in, b_fgate=b_fgate, hgrn_lb_logits=hgrn_lb_logits, hgrn_norm_g=hgrn_norm_g,
                   w_branch_a=w_branch_a, w_branch_b=w_branch_b, w_out=w_out, ln1_g=ln1_g, ln1_b=ln1_b,
                   w_ff_in=w_ff_in, w_ff_out=w_ff_out, ln2_g=ln2_g, ln2_b=ln2_b)
    mom1 = dict(w_in=m_w_in, b_fgate=m_b_fgate, hgrn_lb_logits=m_hgrn_lb_logits, hgrn_norm_g=m_hgrn_norm_g,
                w_branch_a=m_w_branch_a, w_branch_b=m_w_branch_b, w_out=m_w_out, ln1_g=m_ln1_g, ln1_b=m_ln1_b,
                w_ff_in=m_w_ff_in, w_ff_out=m_w_ff_out, ln2_g=m_ln2_g, ln2_b=m_ln2_b)
    mom2 = dict(w_in=v_w_in, b_fgate=v_b_fgate, hgrn_lb_logits=v_hgrn_lb_logits, hgrn_norm_g=v_hgrn_norm_g,
                w_branch_a=v_w_branch_a, w_branch_b=v_w_branch_b, w_out=v_w_out, ln1_g=v_ln1_g, ln1_b=v_ln1_b,
                w_ff_in=v_w_ff_in, w_ff_out=v_w_ff_out, ln2_g=v_ln2_g, ln2_b=v_ln2_b)
    core = lax.axis_index("c")

    gathered = _gather_weights([weights[name].astype(BF16) for name, _, _, _ in _BIG])
    wfull = [{key: _from_chips(gathered[w][l], axis) for w, (_, key, _, axis) in enumerate(_BIG)}
             for l in range(DEPTH)]
    small = {name: weights[name] for name, _ in _SMALL}

    loss_part, grad_x, grads, d_logits = _local_step(x[0], loss_target[0], wfull, small)

    g_all = [jnp.stack([_by_chip(grads[l][key], axis) for l in range(DEPTH)]) for _, key, _, axis in _BIG]
    received = _pair_exchange(g_all)
    pair = [_pair_sum(g_all[w], received[w], core, f"grad_pair_sum_{w}") for w in range(N_BIG)]
    by_chip = _shard_exchange(pair)
    g_layer = [_sum4(by_chip[w], f"grad_chip_sum_{w}") for w in range(N_BIG)]
    g_other = _result_exchange(g_layer)
    out_g, out_d, out_m, out_v = {}, {}, {}, {}
    for w, (name, _, _, _) in enumerate(_BIG):
        out_g[name], out_d[name], out_m[name], out_v[name] = _adamw_big(
            weights[name], mom1[name], mom2[name], g_layer[w], g_other[w], core, f"adamw_{name}")

    small_grads = {name: jnp.stack([grads[l][key] for l in range(DEPTH)])
                   for name, key in [("b_fgate", "b_fgate"), ("hgrn_norm_g", "norm_g"), ("ln1_g", "ln1_g"),
                                     ("ln1_b", "ln1_b"), ("ln2_g", "ln2_g"), ("ln2_b", "ln2_b")]}
    small_grads["hgrn_lb_logits"] = d_logits
    gs = _allreduce_small(_pack_small(small_grads))
    ds, ms, vs = _adamw_small(_pack_small(small), gs, _pack_small({n: mom1[n] for n, _ in _SMALL}),
                              _pack_small({n: mom2[n] for n, _ in _SMALL}), "adamw_small")
    for tree, slab in ((out_g, gs), (out_d, ds), (out_m, ms), (out_v, vs)):
        tree.update(_unpack_small(slab))

    loss = lax.psum(loss_part, ("x", "y", "c"))
    order = ["w_in", "b_fgate", "hgrn_lb_logits", "hgrn_norm_g", "w_branch_a", "w_branch_b", "w_out", "ln1_g", "ln1_b",
             "w_ff_in", "w_ff_out", "ln2_g", "ln2_b"]
    return (loss, grad_x[None], *[out_g[n] for n in order], *[out_d[n] for n in order],
            *[out_m[n] for n in order], *[out_v[n] for n in order])
```

```python
import functools
import math

import jax
import jax.numpy as jnp
import numpy as np
from jax import lax
from jax.experimental import pallas as pl
from jax.experimental.pallas import tpu as pltpu

F32 = jnp.float32
BF16 = jnp.bfloat16

D_MODEL = 1024
DEPTH = 2
A_HEADS = 8
A_WIDTH = 512
B_WIDTH = 512
B_HEADS = 4
HD = 128
CHUNK = 64
SUB = 16
FFN_HIDDEN = 2816
IN_TOTAL = 5640
ALPHA = (2 * DEPTH) ** 0.25
LN_EPS = 1e-5
RMS_EPS = 1e-6
ADAM_LR = 0.001
ADAM_B1 = 0.9
ADAM_B2 = 0.999
ADAM_EPS = 1e-08
ADAM_WD = 0.01
ADAM_STEP = 10
EXP_CLAMP = 60.0

VMEM_LIMIT_BYTES = 56 * 1024 * 1024
MM_ROWS = 1024
DW_ROWS = 2048
N_CHIPS = 4
N_DEV = 8
MESH = pl.DeviceIdType.MESH

_DN = {
    "nn": (((1,), (0,)), ((), ())),
    "nt": (((1,), (1,)), ((), ())),
    "tn": (((0,), (0,)), ((), ())),
}


def _dot(a, b, mode="nn"):
    return lax.dot_general(a.astype(BF16), b.astype(BF16), _DN[mode], preferred_element_type=F32)


def _pieces(x):
    h = x.astype(BF16)
    r = x - h.astype(F32)
    m = r.astype(BF16)
    return h, m, (r - m.astype(F32)).astype(BF16)


def _dot_hi(a, b, mode="nn", exact="a"):
    if exact == "a":
        h, m, l = _pieces(b)
        return (_dot(a, l, mode) + _dot(a, m, mode)) + _dot(a, h, mode)
    h, m, l = _pieces(a)
    return (_dot(l, b, mode) + _dot(m, b, mode)) + _dot(h, b, mode)


def _hdot(a, b, mode="nn"):
    bh, bl, _ = _pieces(b)
    return _dot(a, bl, mode) + _dot(a, bh, mode)


def _params(*sem):
    return pltpu.CompilerParams(dimension_semantics=sem, vmem_limit_bytes=VMEM_LIMIT_BYTES)


def _sigmoid(x):
    return 1.0 / (1.0 + jnp.exp(-x))


def _matmul(a, b, mode, out_dtype, tm, tn, tk, name):
    if mode == "nn":
        (m, k), (k2, n) = a.shape, b.shape
    elif mode == "nt":
        (m, k), (n, k2) = a.shape, b.shape
    else:
        (k, m), (k2, n) = a.shape, b.shape
    assert k == k2, (a.shape, b.shape, mode)
    tm, tn, tk = min(tm, m), min(tn, n), min(tk, k)
    assert m % tm == 0 and n % tn == 0 and k % tk == 0, (a.shape, b.shape, tm, tn, tk)
    nk = k // tk
    if mode == "tn":
        a_spec = pl.BlockSpec((tk, tm), lambda j, i, kk: (kk, i))
    else:
        a_spec = pl.BlockSpec((tm, tk), lambda j, i, kk: (i, kk))
    if mode == "nt":
        b_spec = pl.BlockSpec((tn, tk), lambda j, i, kk: (j, kk))
    else:
        b_spec = pl.BlockSpec((tk, tn), lambda j, i, kk: (kk, j))
    use_acc = nk > 1 and out_dtype != F32

    def body(a_ref, b_ref, o_ref, *scratch):
        p = _dot(a_ref[...], b_ref[...], mode)
        if nk == 1:
            o_ref[...] = p.astype(out_dtype)
            return
        acc_ref = scratch[0] if use_acc else o_ref
        kk = pl.program_id(2)

        @pl.when(kk == 0)
        def _():
            acc_ref[...] = p

        @pl.when(kk > 0)
        def _():
            acc_ref[...] += p

        if use_acc:
            @pl.when(kk == nk - 1)
            def _():
                o_ref[...] = acc_ref[...].astype(out_dtype)

    return pl.pallas_call(
        body,
        name=name,
        grid=(n // tn, m // tm, nk),
        in_specs=[a_spec, b_spec],
        out_specs=pl.BlockSpec((tm, tn), lambda j, i, kk: (i, j)),
        out_shape=jax.ShapeDtypeStruct((m, n), out_dtype),
        scratch_shapes=[pltpu.VMEM((tm, tn), F32)] if use_acc else [],
        compiler_params=_params("parallel", "parallel", "arbitrary"),
    )(a, b)


def _mm_res_ln(a, w, res, g, b, name, tm=512):
    t, k = a.shape
    d = w.shape[1]
    tm = min(tm, t)

    def body(a_ref, w_ref, r_ref, g_ref, b_ref, y_ref, yb_ref, xh_ref, rs_ref):
        z = ALPHA * r_ref[...] + _dot(a_ref[...], w_ref[...])
        mu = jnp.mean(z, axis=-1, keepdims=True)
        zc = z - mu
        var = jnp.mean(zc * zc, axis=-1, keepdims=True)
        rstd = lax.rsqrt(var + LN_EPS)
        xh = zc * rstd
        y = xh * g_ref[...] + b_ref[...]
        y_ref[...] = y
        yb_ref[...] = y.astype(BF16)
        xh_ref[...] = xh
        rs_ref[...] = rstd

    row = lambda i: (i, 0)
    fix = lambda i: (0, 0)
    return pl.pallas_call(
        body,
        name=name,
        grid=(t // tm,),
        in_specs=[pl.BlockSpec((tm, k), row), pl.BlockSpec((k, d), fix), pl.BlockSpec((tm, d), row),
                  pl.BlockSpec((1, d), fix), pl.BlockSpec((1, d), fix)],
        out_specs=[pl.BlockSpec((tm, d), row), pl.BlockSpec((tm, d), row), pl.BlockSpec((tm, d), row),
                   pl.BlockSpec((tm, 1), row)],
        out_shape=[jax.ShapeDtypeStruct((t, d), F32), jax.ShapeDtypeStruct((t, d), BF16),
                   jax.ShapeDtypeStruct((t, d), F32), jax.ShapeDtypeStruct((t, 1), F32)],
        compiler_params=_params("parallel"),
    )(a, w, res, g.reshape(1, d), b.reshape(1, d))


def _ln_bwd(dys, coefs, xhat, rstd, g, name, tm=512):
    t, d = xhat.shape
    tm = min(tm, t)
    n_in = len(dys)

    def body(*refs):
        dy_refs = refs[:n_in]
        xh_ref, rs_ref, g_ref, dz_ref, dzb_ref, dg_ref, db_ref = refs[n_in:]
        dy = coefs[0] * dy_refs[0][...].astype(F32)
        for c, r in zip(coefs[1:], dy_refs[1:]):
            dy = dy + c * r[...].astype(F32)
        xh = xh_ref[...]
        dxh = dy * g_ref[...]
        m1 = jnp.mean(dxh, axis=-1, keepdims=True)
        m2 = jnp.mean(dxh * xh, axis=-1, keepdims=True)
        dz = rs_ref[...] * (dxh - m1 - xh * m2)
        dz_ref[...] = dz
        dzb_ref[...] = dz.astype(BF16)
        pg = jnp.sum(dy * xh, axis=0, keepdims=True)
        pb = jnp.sum(dy, axis=0, keepdims=True)

        @pl.when(pl.program_id(0) == 0)
        def _():
            dg_ref[...] = pg
            db_ref[...] = pb

        @pl.when(pl.program_id(0) > 0)
        def _():
            dg_ref[...] += pg
            db_ref[...] += pb

    row = lambda i: (i, 0)
    fix = lambda i: (0, 0)
    return pl.pallas_call(
        body,
        name=name,
        grid=(t // tm,),
        in_specs=[pl.BlockSpec((tm, d), row)] * n_in
        + [pl.BlockSpec((tm, d), row), pl.BlockSpec((tm, 1), row), pl.BlockSpec((1, d), fix)],
        out_specs=[pl.BlockSpec((tm, d), row), pl.BlockSpec((tm, d), row), pl.BlockSpec((1, d), fix),
                   pl.BlockSpec((1, d), fix)],
        out_shape=[jax.ShapeDtypeStruct((t, d), F32), jax.ShapeDtypeStruct((t, d), BF16),
                   jax.ShapeDtypeStruct((1, d), F32), jax.ShapeDtypeStruct((1, d), F32)],
        compiler_params=_params("arbitrary"),
    )(*dys, xhat, rstd, g.reshape(1, d))


def _loss_head(y, target, name="loss_head", tm=512):
    t, d = y.shape
    tm = min(tm, t)

    def body(y_ref, t_ref, dy_ref, l_ref):
        e = y_ref[...] - t_ref[...]
        dy_ref[...] = e * (1.0 / d)
        part = jnp.full((8, 128), 0.5 / d, F32) * jnp.sum(e * e)

        @pl.when(pl.program_id(0) == 0)
        def _():
            l_ref[...] = part

        @pl.when(pl.program_id(0) > 0)
        def _():
            l_ref[...] += part

    row = lambda i: (i, 0)
    return pl.pallas_call(
        body,
        name=name,
        grid=(t // tm,),
        in_specs=[pl.BlockSpec((tm, d), row), pl.BlockSpec((tm, d), row)],
        out_specs=[pl.BlockSpec((tm, d), row), pl.BlockSpec((8, 128), lambda i: (0, 0))],
        out_shape=[jax.ShapeDtypeStruct((t, d), F32), jax.ShapeDtypeStruct((8, 128), F32)],
        compiler_params=_params("arbitrary"),
    )(y, target)


FFN_COLS = FFN_HIDDEN // 2


def _ffn_in_swiglu(xb, wu, wg, name, tm=512):
    t, d = xb.shape
    tm = min(tm, t)

    def body(x_ref, wu_ref, wg_ref, a_ref, u_ref, g_ref):
        x = x_ref[...]
        u = _dot(x, wu_ref[...])
        g = _dot(x, wg_ref[...])
        u_ref[...] = u
        g_ref[...] = g
        a_ref[...] = (g * _sigmoid(g) * u).astype(BF16)

    wspec = pl.BlockSpec((d, FFN_COLS), lambda j, i: (0, j))
    out = pl.BlockSpec((tm, FFN_COLS), lambda j, i: (i, j))
    return pl.pallas_call(
        body,
        name=name,
        grid=(FFN_HIDDEN // FFN_COLS, t // tm),
        in_specs=[pl.BlockSpec((tm, d), lambda j, i: (i, 0)), wspec, wspec],
        out_specs=[out, out, out],
        out_shape=[jax.ShapeDtypeStruct((t, FFN_HIDDEN), BF16), jax.ShapeDtypeStruct((t, FFN_HIDDEN), F32),
                   jax.ShapeDtypeStruct((t, FFN_HIDDEN), F32)],
        compiler_params=_params("parallel", "parallel"),
    )(xb, wu, wg)


def _ffn_out_dx_swiglu(dzb, w_ff_out, u, g, name, tm=512):
    t, d = dzb.shape
    tm = min(tm, t)

    def body(dz_ref, w_ref, u_ref, g_ref, du_ref, dg_ref):
        da = _dot(dz_ref[...], w_ref[...], "nt")
        gv = g_ref[...]
        sg = _sigmoid(gv)
        du_ref[...] = (da * gv * sg).astype(BF16)
        dg_ref[...] = (da * u_ref[...] * (sg * (1.0 + gv * (1.0 - sg)))).astype(BF16)

    blk = pl.BlockSpec((tm, FFN_COLS), lambda j, i: (i, j))
    return pl.pallas_call(
        body,
        name=name,
        grid=(FFN_HIDDEN // FFN_COLS, t // tm),
        in_specs=[pl.BlockSpec((tm, d), lambda j, i: (i, 0)), pl.BlockSpec((FFN_COLS, d), lambda j, i: (j, 0)), blk, blk],
        out_specs=[blk, blk],
        out_shape=[jax.ShapeDtypeStruct((t, FFN_HIDDEN), BF16)] * 2,
        compiler_params=_params("parallel", "parallel"),
    )(dzb, w_ff_out, u, g)


def _merge_fwd(ya, yb, wpa, wpb, rest, name, tm=512):
    t = ya.shape[0]
    tm = min(tm, t)

    def body(ya_ref, yb_ref, wa_ref, wb_ref, ga_ref, gb_ref, o_ref):
        pa = _dot(ya_ref[...], wa_ref[...])
        pb = _dot(yb_ref[...], wb_ref[...])
        o_ref[...] = (_sigmoid(ga_ref[...]) * pa + _sigmoid(gb_ref[...]) * pb).astype(BF16)

    row = lambda i: (i, 0)
    fix = lambda i: (0, 0)
    return pl.pallas_call(
        body,
        name=name,
        grid=(t // tm,),
        in_specs=[pl.BlockSpec((tm, A_WIDTH), row), pl.BlockSpec((tm, B_WIDTH), row),
                  pl.BlockSpec((A_WIDTH, D_MODEL), fix), pl.BlockSpec((B_WIDTH, D_MODEL), fix),
                  pl.BlockSpec((tm, D_MODEL), lambda i: (i, 0)), pl.BlockSpec((tm, D_MODEL), lambda i: (i, 1))],
        out_specs=pl.BlockSpec((tm, D_MODEL), row),
        out_shape=jax.ShapeDtypeStruct((t, D_MODEL), BF16),
        compiler_params=_params("parallel"),
    )(ya, yb, wpa, wpb, rest, rest)


def _merge_bwd(dzb, w_out, ya, yb, wpa, wpb, rest, name, tm=512):
    t = ya.shape[0]
    tm = min(tm, t)

    def body(dz_ref, wo_ref, ya_ref, yb_ref, wa_ref, wb_ref, ga_ref, gb_ref, dg_ref, dpa_ref, dpb_ref, dya_ref,
             dyb_ref):
        dm_v = _dot(dz_ref[...], wo_ref[...], "nt")
        pa = _dot(ya_ref[...], wa_ref[...])
        pb = _dot(yb_ref[...], wb_ref[...])
        sa = _sigmoid(ga_ref[...])
        sb = _sigmoid(gb_ref[...])
        dg_ref[:, :D_MODEL] = (dm_v * pa * sa * (1.0 - sa)).astype(BF16)
        dg_ref[:, D_MODEL:] = (dm_v * pb * sb * (1.0 - sb)).astype(BF16)
        dpa = (dm_v * sa).astype(BF16)
        dpb = (dm_v * sb).astype(BF16)
        dpa_ref[...] = dpa
        dpb_ref[...] = dpb
        dya_ref[...] = _dot(dpa, wa_ref[...], "nt").astype(BF16)
        dyb_ref[...] = _dot(dpb, wb_ref[...], "nt")

    row = lambda i: (i, 0)
    fix = lambda i: (0, 0)
    return pl.pallas_call(
        body,
        name=name,
        grid=(t // tm,),
        in_specs=[pl.BlockSpec((tm, D_MODEL), row), pl.BlockSpec((D_MODEL, D_MODEL), fix),
                  pl.BlockSpec((tm, A_WIDTH), row), pl.BlockSpec((tm, B_WIDTH), row),
                  pl.BlockSpec((A_WIDTH, D_MODEL), fix), pl.BlockSpec((B_WIDTH, D_MODEL), fix),
                  pl.BlockSpec((tm, D_MODEL), lambda i: (i, 0)), pl.BlockSpec((tm, D_MODEL), lambda i: (i, 1))],
        out_specs=[pl.BlockSpec((tm, 2 * D_MODEL), row), pl.BlockSpec((tm, D_MODEL), row),
                   pl.BlockSpec((tm, D_MODEL), row), pl.BlockSpec((tm, A_WIDTH), row), pl.BlockSpec((tm, B_WIDTH), row)],
        out_shape=[jax.ShapeDtypeStruct((t, 2 * D_MODEL), BF16), jax.ShapeDtypeStruct((t, D_MODEL), BF16),
                   jax.ShapeDtypeStruct((t, D_MODEL), BF16), jax.ShapeDtypeStruct((t, A_WIDTH), BF16),
                   jax.ShapeDtypeStruct((t, B_WIDTH), F32)],
        compiler_params=_params("parallel"),
    )(dzb, w_out, ya, yb, wpa, wpb, rest, rest)


FA_BLOCK = 4224 // 128 - 1


def _tri(n, lower):
    r = lax.broadcasted_iota(jnp.int32, (n, n), 0)
    c = lax.broadcasted_iota(jnp.int32, (n, n), 1)
    return jnp.where((r >= c) if lower else (r <= c), 1.0, 0.0).astype(F32)


def _head_spread(expand):
    shape = (128, A_WIDTH) if expand else (A_WIDTH, 128)
    r = lax.broadcasted_iota(jnp.int32, shape, 0)
    c = lax.broadcasted_iota(jnp.int32, shape, 1)
    hit = ((c >= 64 * r) & (c < 64 * r + 64)) if expand else (r == 64 * c)
    return jnp.where(hit, 1.0, 0.0).astype(F32)


def _fox_gate_fwd(rest, bf, name, tb=512):
    t = rest.shape[0]
    tb = min(tb, t)

    def body(fa_ref, bf_ref, f_ref, fc_ref, carry):
        @pl.when(pl.program_id(0) == 0)
        def _():
            carry[...] = jnp.zeros_like(carry)

        z = fa_ref[...] + bf_ref[...]
        logf = jnp.minimum(z, 0.0) - jnp.log(1.0 + jnp.exp(-jnp.abs(z)))
        f = _dot_hi(_tri(tb, True), logf) + carry[...]
        f_ref[...] = f
        fc_ref[...] = _dot_hi(f, _head_spread(True), exact="b")
        carry[...] = f[tb - 1:tb, :]

    return pl.pallas_call(
        body,
        name=name,
        grid=(t // tb,),
        in_specs=[pl.BlockSpec((tb, 128), lambda i: (i, FA_BLOCK)), pl.BlockSpec((1, 128), lambda i: (0, 0))],
        out_specs=[pl.BlockSpec((tb, 128), lambda i: (i, 0)), pl.BlockSpec((tb, A_WIDTH), lambda i: (i, 0))],
        out_shape=[jax.ShapeDtypeStruct((t, 128), F32), jax.ShapeDtypeStruct((t, A_WIDTH), F32)],
        scratch_shapes=[pltpu.VMEM((1, 128), F32)],
        compiler_params=_params("arbitrary"),
    )(rest, bf)


def _fox_gate_bwd(rsum, csum, rest, bf, name, tb=512):
    t = rest.shape[0]
    tb = min(tb, t)
    nb = t // tb

    def body(rs_ref, cs_ref, fa_ref, bf_ref, dfa_ref, dbf_ref, carry):
        @pl.when(pl.program_id(0) == 0)
        def _():
            carry[...] = jnp.zeros_like(carry)

        d_f = _dot_hi(rs_ref[...].astype(F32) - cs_ref[...], _head_spread(False), exact="b")
        dlogf = _dot_hi(_tri(tb, False), d_f) + carry[...]
        carry[...] = dlogf[0:1, :]
        z = fa_ref[...] + bf_ref[...]
        dz = dlogf * _sigmoid(-z)
        dfa_ref[...] = dz.astype(BF16)
        part = jnp.sum(dz, axis=0, keepdims=True)

        @pl.when(pl.program_id(0) == 0)
        def _():
            dbf_ref[...] = part

        @pl.when(pl.program_id(0) > 0)
        def _():
            dbf_ref[...] += part

    return pl.pallas_call(
        body,
        name=name,
        grid=(nb,),
        in_specs=[pl.BlockSpec((tb, A_WIDTH), lambda i: (nb - 1 - i, 0)),
                  pl.BlockSpec((tb, A_WIDTH), lambda i: (nb - 1 - i, 0)),
                  pl.BlockSpec((tb, 128), lambda i: (nb - 1 - i, FA_BLOCK)),
                  pl.BlockSpec((1, 128), lambda i: (0, 0))],
        out_specs=[pl.BlockSpec((tb, 128), lambda i: (nb - 1 - i, 0)), pl.BlockSpec((1, 128), lambda i: (0, 0))],
        out_shape=[jax.ShapeDtypeStruct((t, 128), BF16), jax.ShapeDtypeStruct((1, 128), F32)],
        scratch_shapes=[pltpu.VMEM((1, 128), F32)],
        compiler_params=_params("arbitrary"),
    )(rsum, csum, rest, bf)


ATT_BLOCK = 512


def _head_mask(shape, j):
    lane = lax.broadcasted_iota(jnp.int32, shape, 1)
    return (lane < 64) if j == 0 else (lane >= 64)


def _aug_lanes(tb, j):
    lane = lax.broadcasted_iota(jnp.int32, (tb, 128), 1)
    own = (lane < 64) if j == 0 else (lane >= 64)
    return own, lane - 64 * (1 - j)


def _aug_query(own, li, q, pieces):
    h, m, l = pieces
    one, zero = jnp.ones_like(h), jnp.zeros_like(h)
    spare = jnp.where(li == 0, h, jnp.where(li == 1, m, jnp.where(li == 2, l, jnp.where(li < 6, one, zero))))
    return jnp.where(own, q, spare)


def _fox_prep_fwd(qkv, fcol, name, tb=2048):
    t = qkv.shape[0]
    tb = min(tb, t)

    def body(q_ref, k_ref, v_ref, fc_ref, qa_ref, ka_ref, va_ref, qn_ref, kn_ref):
        pieces = _pieces(pltpu.roll(fc_ref[...], 64, 1))
        h, m, l = pieces
        q, k, v = q_ref[...], k_ref[...], v_ref[...]
        first = _head_mask((tb, 128), 0)
        for nrm_ref, x in ((qn_ref, q.astype(F32)), (kn_ref, k.astype(F32))):
            n0 = jnp.max(jnp.sum(jnp.where(first, x * x, 0.0), axis=1, keepdims=True))
            n1 = jnp.max(jnp.sum(jnp.where(first, 0.0, x * x), axis=1, keepdims=True))
            nrm_ref[...] = jnp.where(_head_mask((8, 128), 0), n0, n1)
        one, zero = jnp.ones_like(h), jnp.zeros_like(h)
        for j in (0, 1):
            own, li = _aug_lanes(tb, j)
            cols = slice(128 * j, 128 * (j + 1))
            qa_ref[:, cols] = _aug_query(own, li, q * 0.125, pieces)
            ks = jnp.where(li < 3, one, jnp.where(li == 3, -h, jnp.where(li == 4, -m, jnp.where(li == 5, -l, zero))))
            ka_ref[:, cols] = jnp.where(own, k, ks)
            va_ref[:, cols] = jnp.where(own, v, one)

    blk = pl.BlockSpec((tb, 256), lambda i, h: (i, h))
    nrm = pl.BlockSpec((None, None, 8, 128), lambda i, h: (i, h, 0, 0))
    return pl.pallas_call(
        body, name=name, grid=(t // tb, 4),
        in_specs=[pl.BlockSpec((tb, 128), lambda i, h: (i, h)), pl.BlockSpec((tb, 128), lambda i, h: (i, 4 + h)),
                  pl.BlockSpec((tb, 128), lambda i, h: (i, 8 + h)), pl.BlockSpec((tb, 128), lambda i, h: (i, h))],
        out_specs=[blk, blk, blk, nrm, nrm],
        out_shape=[jax.ShapeDtypeStruct((t, 2 * A_WIDTH), BF16)] * 3
        + [jax.ShapeDtypeStruct((t // tb, 4, 8, 128), F32)] * 2,
        compiler_params=_params("parallel", "parallel"),
    )(qkv, qkv, qkv, fcol)


def _fox_prep_bwd(qkv, fcol, lse, do, o, name, tb=2048):
    t = qkv.shape[0]
    tb = min(tb, t)

    def body(q_ref, fc_ref, lse_ref, do_ref, o_ref, qb_ref, dob_ref):
        pieces = _pieces(pltpu.roll(fc_ref[...] - lse_ref[...], 64, 1))
        q = q_ref[...] * 0.125
        do_v = do_ref[...]
        prod = do_v.astype(F32) * o_ref[...].astype(F32)
        for j in (0, 1):
            own, li = _aug_lanes(tb, j)
            cols = slice(128 * j, 128 * (j + 1))
            qb_ref[:, cols] = _aug_query(own, li, q, pieces)
            delta = jnp.sum(jnp.where(own, prod, 0.0), axis=1, keepdims=True)
            h, m, l = _pieces(jnp.broadcast_to(delta, (tb, 128)))
            ds = jnp.where(li == 0, -h, jnp.where(li == 1, -m, jnp.where(li == 2, -l, jnp.zeros_like(h))))
            dob_ref[:, cols] = jnp.where(own, do_v, ds)

    pair = pl.BlockSpec((tb, 128), lambda i, h: (i, h))
    blk = pl.BlockSpec((tb, 256), lambda i, h: (i, h))
    return pl.pallas_call(
        body, name=name, grid=(t // tb, 4),
        in_specs=[pair, pair, pair, pair, pair],
        out_specs=[blk, blk],
        out_shape=[jax.ShapeDtypeStruct((t, 2 * A_WIDTH), BF16)] * 2,
        compiler_params=_params("parallel", "parallel"),
    )(qkv, fcol, lse, do, o)


def _tile_mask(n, transposed):
    r = lax.broadcasted_iota(jnp.int32, (n, n), 0)
    c = lax.broadcasted_iota(jnp.int32, (n, n), 1)
    return (c >= r) if transposed else (r >= c)


UNDERFLOW = -110.0


def _fox_block_ranges(qn, kn, fcum):
    t = fcum.shape[0]
    blk = min(ATT_BLOCK, t)
    nb = t // blk
    q2 = jnp.max(qn[:, :, 0, ::64].reshape(-1, A_HEADS), axis=0)
    k2 = jnp.max(kn[:, :, 0, ::64].reshape(-1, A_HEADS), axis=0)
    bound = 2.0 * jnp.sqrt(q2 * k2) * 0.125
    f = fcum[:, :A_HEADS]
    first = f[0::blk].T
    last = f[blk - 1::blk].T
    dead = (bound[:, None, None] + first[:, :, None] - last[:, None, :]) < UNDERFLOW
    qi = jnp.arange(nb)[None, :, None]
    kj = jnp.arange(nb)[None, None, :]
    dead = dead & (kj < qi)
    kstart = jnp.sum(dead, axis=2).astype(jnp.int32)
    qend = (kj[0] + jnp.sum((~dead) & (qi > kj), axis=1)).astype(jnp.int32)
    return kstart.reshape(-1), qend.reshape(-1)


def _fox_fwd(qa, ka, va, kstart, name):
    t = qa.shape[0]
    bq = min(ATT_BLOCK, t)
    nq = t // bq

    def body(ks_ref, q_ref, k_ref, v_ref, o_ref, lse_ref):
        i = pl.program_id(1)
        hp = pl.program_id(0)
        k0 = [ks_ref[(2 * hp + j) * nq + i] for j in (0, 1)]
        both0 = jnp.maximum(k0[0], k0[1])

        def head(j, kb, m, acc, masked):
            rows = pl.ds(pl.multiple_of(kb * bq, bq), bq)
            cols = slice(128 * j, 128 * (j + 1))
            s = _dot(q_ref[:, cols], k_ref[rows, cols], "nt")
            if masked:
                s = jnp.where(_tile_mask(bq, False), s, -jnp.inf)
            m_new = jnp.maximum(m, jnp.max(s, axis=1, keepdims=True))
            return m_new, jnp.exp(m - m_new) * acc + _dot(jnp.exp(s - m_new), v_ref[rows, cols])

        def pair(kb, carry, masked):
            return head(0, kb, carry[0], carry[1], masked) + head(1, kb, carry[2], carry[3], masked)

        init = (jnp.full((bq, 1), -jnp.inf, F32), jnp.zeros((bq, 128), F32))
        alone = [lax.fori_loop(k0[j], both0, lambda kb, c, j=j: head(j, kb, c[0], c[1], False), init) for j in (0, 1)]
        carry = lax.fori_loop(both0, i, lambda kb, c: pair(kb, c, False), alone[0] + alone[1])
        carry = pair(i, carry, True)
        outs = []
        for j in (0, 1):
            m, acc = carry[2 * j], carry[2 * j + 1]
            spare = 64 * (1 - j)
            l = acc[:, spare:spare + 1]
            outs.append((acc / l, m + jnp.log(l)))
        msk = _head_mask((bq, 128), 0)
        o_ref[...] = jnp.where(msk, outs[0][0], outs[1][0]).astype(BF16)
        lse_ref[...] = jnp.where(msk, outs[0][1], outs[1][1])

    res = pl.BlockSpec((t, 256), lambda h, i, tbl: (0, h))
    out = pl.BlockSpec((bq, 128), lambda h, i, tbl: (i, h))
    return pl.pallas_call(
        body,
        name=name,
        grid_spec=pltpu.PrefetchScalarGridSpec(
            num_scalar_prefetch=1, grid=(4, nq),
            in_specs=[pl.BlockSpec((bq, 256), lambda h, i, tbl: (i, h)), res, res],
            out_specs=[out, out]),
        out_shape=[jax.ShapeDtypeStruct((t, A_WIDTH), BF16), jax.ShapeDtypeStruct((t, A_WIDTH), F32)],
        compiler_params=_params("parallel", "parallel"),
    )(kstart, qa, ka, va)


def _fox_bwd(qb, ka, va, dob, qend, name):
    t = qb.shape[0]
    bk = min(ATT_BLOCK, t)
    nk = t // bk

    def body(qe_ref, k_ref, v_ref, q_hbm, do_hbm, dk_ref, dv_ref, cs_ref, dq_hbm, rs_hbm, q_scr, do_scr, dq_scr,
             sems):
        jb = pl.program_id(1)
        hp = pl.program_id(0)
        pair_cols = pl.ds(pl.multiple_of(hp * 256, 256), 256)

        @pl.when(jb == 0)
        def _():
            loads = [pltpu.make_async_copy(q_hbm.at[:, pair_cols], q_scr, sems.at[0]),
                     pltpu.make_async_copy(do_hbm.at[:, pair_cols], do_scr, sems.at[1])]
            for cp in loads:
                cp.start()
            dq_scr[...] = jnp.zeros_like(dq_scr)
            for cp in loads:
                cp.wait()

        i1 = [qe_ref[(2 * hp + j) * nk + jb] + 1 for j in (0, 1)]
        both1 = jnp.minimum(i1[0], i1[1])

        def head(j, ib, dk_acc, dv_acc, masked):
            rows = pl.ds(pl.multiple_of(ib * bk, bk), bk)
            cols = slice(128 * j, 128 * (j + 1))
            qs = q_scr[rows, cols]
            dos = do_scr[rows, cols]
            kj = k_ref[:, cols]
            st = _dot(kj, qs, "nt")
            if masked:
                st = jnp.where(_tile_mask(bk, True), st, -jnp.inf)
            pt = jnp.exp(st)
            dst = (pt * _dot(v_ref[:, cols], dos, "nt")).astype(BF16)
            dq_scr[rows, cols] += _dot(dst, kj, "tn")
            return dk_acc + _dot(dst, qs), dv_acc + _dot(pt, dos)

        def pair(ib, carry, masked):
            return head(0, ib, carry[0], carry[1], masked) + head(1, ib, carry[2], carry[3], masked)

        carry = pair(jb, (jnp.zeros((bk, 128), F32),) * 4, True)
        carry = lax.fori_loop(jb + 1, both1, lambda ib, c: pair(ib, c, False), carry)
        alone = [lax.fori_loop(jnp.maximum(both1, jb + 1), i1[j],
                               lambda ib, c, j=j: head(j, ib, c[0], c[1], False), carry[2 * j:2 * j + 2])
                 for j in (0, 1)]
        carry = alone[0] + alone[1]
        outs = []
        for j in (0, 1):
            spare = 64 * (1 - j)
            dk_acc, dv_acc = carry[2 * j], carry[2 * j + 1]
            outs.append((dk_acc, dv_acc, dk_acc[:, spare + 3:spare + 4]))
        msk = _head_mask((bk, 128), 0)
        dk_ref[...] = jnp.where(msk, outs[0][0], outs[1][0]).astype(BF16)
        dv_ref[...] = jnp.where(msk, outs[0][1], outs[1][1]).astype(BF16)
        cs_ref[...] = jnp.where(msk, outs[0][2], outs[1][2])

        @pl.when(jb == nk - 1)
        def _():
            def finish(r, carry):
                rows = pl.ds(pl.multiple_of(r * bk, bk), bk)
                x0, x1 = dq_scr[rows, 0:128], dq_scr[rows, 128:256]
                q_scr[rows, 0:128] = (jnp.where(msk, x0, x1) * 0.125).astype(BF16)
                do_scr[rows, 0:128] = jnp.where(msk, x0[:, 64:65], x1[:, 0:1]).astype(BF16)
                return carry

            lax.fori_loop(0, nk, finish, 0)
            head_cols = pl.ds(pl.multiple_of(hp * 128, 128), 128)
            stores = [pltpu.make_async_copy(q_scr.at[:, 0:128], dq_hbm.at[:, head_cols], sems.at[0]),
                      pltpu.make_async_copy(do_scr.at[:, 0:128], rs_hbm.at[:, head_cols], sems.at[1])]
            for cp in stores:
                cp.start()
            for cp in stores:
                cp.wait()

    blk = pl.BlockSpec((bk, 256), lambda h, i, tbl: (i, h))
    out = pl.BlockSpec((bk, 128), lambda h, i, tbl: (i, h))
    return pl.pallas_call(
        body,
        name=name,
        grid_spec=pltpu.PrefetchScalarGridSpec(
            num_scalar_prefetch=1, grid=(4, nk), in_specs=[blk, blk, _ANY, _ANY],
            out_specs=[out, out, out, _ANY, _ANY],
            scratch_shapes=[pltpu.VMEM((t, 256), BF16), pltpu.VMEM((t, 256), BF16), pltpu.VMEM((t, 256), F32),
                            pltpu.SemaphoreType.DMA((2,))]),
        out_shape=[jax.ShapeDtypeStruct((t, A_WIDTH), BF16), jax.ShapeDtypeStruct((t, A_WIDTH), BF16),
                   jax.ShapeDtypeStruct((t, A_WIDTH), F32), jax.ShapeDtypeStruct((t, A_WIDTH), BF16),
                   jax.ShapeDtypeStruct((t, A_WIDTH), BF16)],
        compiler_params=_params("arbitrary", "arbitrary"),
    )(qend, ka, va, qb, dob)


HG_ROWS = 256


def _hg_gates(hb_ref, rows, lbv):
    qb = hb_ref[rows, 0:B_WIDTH]
    fb = hb_ref[rows, B_WIDTH:2 * B_WIDTH]
    v = hb_ref[rows, 2 * B_WIDTH:3 * B_WIDTH]
    gb = hb_ref[rows, 3 * B_WIDTH:4 * B_WIDTH]
    sg = _sigmoid(fb)
    f = lbv + (1.0 - lbv) * sg
    sq = _sigmoid(qb)
    return qb, sq, qb * sq, sg, f, 1.0 - f, jnp.log(f), v, gb


def _hg_intra_factors(q, k, b):
    fac = []
    for i in range(CHUNK // SUB):
        bi = b[SUB * i:SUB * i + 1, :]
        eq = jnp.exp(b[SUB * i:SUB * (i + 1), :] - bi)
        ek = jnp.exp(jnp.minimum(bi - b, EXP_CLAMP))
        fac.append((eq, ek, q[SUB * i:SUB * (i + 1), :] * eq, k * ek))
    return fac


def _causal(n):
    r = lax.broadcasted_iota(jnp.int32, (n, n), 0)
    c = lax.broadcasted_iota(jnp.int32, (n, n), 1)
    return r >= c


def _hgrn_fwd(rest, lb, ng, name):
    t = rest.shape[0]
    bt = min(HG_ROWS, t)
    ncb = bt // CHUNK

    def body(hb_ref, lb_ref, ng_ref, y_ref, o_ref, st_ref, s_scr):
        @pl.when(pl.program_id(0) == 0)
        def _():
            s_scr[...] = jnp.zeros_like(s_scr)

        tril = _tri(CHUNK, True)
        causal = _causal(CHUNK)
        ones = jnp.ones((CHUNK, HD), F32)

        def chunk(c, carry):
            rows = pl.ds(pl.multiple_of(c * CHUNK, CHUNK), CHUNK)
            _, _, q_all, _, _, k_all, g_all, v_all, gb_all = _hg_gates(hb_ref, rows, lb_ref[...])
            b_all = _dot_hi(tril, g_all)
            qd_all = q_all * jnp.exp(b_all)
            kd_all = k_all * jnp.exp(b_all[CHUNK - 1:CHUNK, :] - b_all)
            eb_all = jnp.exp(_dot_hi(g_all, ones, "tn", exact="b"))
            sgb_all = _sigmoid(gb_all)
            for h in range(B_HEADS):
                cols = slice(h * HD, (h + 1) * HD)
                v = v_all[:, cols]
                s0 = s_scr[h]
                st_ref[c, h] = s0
                o = _dot(qd_all[:, cols], s0)
                fac = _hg_intra_factors(q_all[:, cols], k_all[:, cols], b_all[:, cols])
                a = jnp.concatenate([_dot(qe, ke, "nt") for _, _, qe, ke in fac], axis=0)
                o = o + _dot(jnp.where(causal, a, 0.0), v)
                s_scr[h] = eb_all[h * HD:(h + 1) * HD, :] * s0 + _dot(kd_all[:, cols], v, "tn")
                r = lax.rsqrt(jnp.mean(o * o, axis=-1, keepdims=True) + RMS_EPS)
                o_ref[rows, cols] = o
                y_ref[rows, cols] = (o * r * ng_ref[...] * sgb_all[:, cols]).astype(BF16)
            return carry

        lax.fori_loop(0, ncb, chunk, 0)

    return pl.pallas_call(
        body,
        name=name,
        grid=(t // bt,),
        in_specs=[pl.BlockSpec((bt, 4 * B_WIDTH), lambda i: (i, 1)), pl.BlockSpec((1, B_WIDTH), lambda i: (0, 0)),
                  pl.BlockSpec((1, HD), lambda i: (0, 0))],
        out_specs=[pl.BlockSpec((bt, B_WIDTH), lambda i: (i, 0)), pl.BlockSpec((bt, B_WIDTH), lambda i: (i, 0)),
                   pl.BlockSpec((ncb, B_HEADS, HD, HD), lambda i: (i, 0, 0, 0))],
        out_shape=[jax.ShapeDtypeStruct((t, B_WIDTH), BF16), jax.ShapeDtypeStruct((t, B_WIDTH), F32),
                   jax.ShapeDtypeStruct((t // CHUNK, B_HEADS, HD, HD), F32)],
        scratch_shapes=[pltpu.VMEM((B_HEADS, HD, HD), F32)],
        compiler_params=_params("arbitrary"),
    )(rest, lb, ng)


def _hgrn_bwd(dy, rest, o_saved, states, lb, ng, name):
    t = rest.shape[0]
    bt = min(HG_ROWS, t)
    ncb = bt // CHUNK
    nb = t // bt

    def body(dy_ref, hb_ref, o_ref, st_ref, lb_ref, ng_ref, dh_ref, dlb_ref, dng_ref, ds_scr):
        @pl.when(pl.program_id(0) == 0)
        def _():
            ds_scr[...] = jnp.zeros_like(ds_scr)
            dlb_ref[...] = jnp.zeros_like(dlb_ref)
            dng_ref[...] = jnp.zeros_like(dng_ref)

        tril = _tri(CHUNK, True)
        triu = _tri(CHUNK, False)
        causal = _causal(CHUNK)
        ones = jnp.ones((CHUNK, HD), F32)
        ones8 = jnp.ones((8, HD), F32)
        last_row = lax.broadcasted_iota(jnp.int32, (CHUNK, B_WIDTH), 0) == CHUNK - 1

        def chunk(cc, carry):
            dng_acc, dlb_acc = carry
            c = ncb - 1 - cc
            rows = pl.ds(pl.multiple_of(c * CHUNK, CHUNK), CHUNK)
            lbv = lb_ref[...]
            qb, sq, q_all, sg, f, k_all, g_all, v_all, gb = _hg_gates(hb_ref, rows, lbv)
            b_all = _dot_hi(tril, g_all)
            ebt_all = jnp.exp(b_all)
            blast = b_all[CHUNK - 1:CHUNK, :]
            ekd_all = jnp.exp(blast - b_all)
            eb_all = jnp.exp(_dot_hi(g_all, ones, "tn", exact="b"))
            sgb = _sigmoid(gb)
            dy_all = dy_ref[rows, :].astype(F32)
            don_all = dy_all * sgb
            ngv = ng_ref[...]
            dq_l, dk_l, dks_l, dv_l, on_l, prod_l = [], [], [], [], [], []
            for h in range(B_HEADS):
                cols = slice(h * HD, (h + 1) * HD)
                q, k, v = q_all[:, cols], k_all[:, cols], v_all[:, cols]
                o = o_ref[rows, cols]
                don = don_all[:, cols]
                r = lax.rsqrt(jnp.mean(o * o, axis=-1, keepdims=True) + RMS_EPS)
                on_l.append(o * r * ngv)
                dng_acc = dng_acc + jnp.sum(don * o * r, axis=0, keepdims=True)
                doh = don * ngv
                do = r * (doh - o * (r * r) * jnp.mean(doh * o, axis=-1, keepdims=True))
                ebt, ekd = ebt_all[:, cols], ekd_all[:, cols]
                s0 = st_ref[c, h]
                ds1 = ds_scr[h]
                fac = _hg_intra_factors(q, k, b_all[:, cols])
                a = jnp.concatenate([_dot(qe, ke, "nt") for _, _, qe, ke in fac], axis=0)
                a = jnp.where(causal, a, 0.0)
                da = jnp.where(causal, _dot(do, v, "nt"), 0.0)
                dv_l.append(_dot(a, do, "tn") + _dot(k * ekd, ds1))
                dq = ebt * _dot(do, s0, "nt")
                dq_l.append(dq + jnp.concatenate(
                    [eq * _hdot(da[SUB * i:SUB * (i + 1), :], ke) for i, (eq, _, _, ke) in enumerate(fac)], axis=0))
                dk_state = ekd * _dot(v, ds1, "nt")
                dk = dk_state
                for i, (_, ek, qe, _) in enumerate(fac):
                    dk = dk + ek * _hdot(da[SUB * i:SUB * (i + 1), :], qe, "tn")
                dk_l.append(dk)
                dks_l.append(dk_state)
                prod_l.append(ds1 * s0)
                ds_scr[h] = _dot(q * ebt, do, "tn") + eb_all[h * HD:(h + 1) * HD, :] * ds1
            dq_all, dk_all = jnp.concatenate(dq_l, axis=1), jnp.concatenate(dk_l, axis=1)
            extra = jnp.exp(blast) * _dot_hi(ones8, jnp.concatenate(prod_l, axis=0), "nt")[0:1, :] \
                + jnp.sum(k_all * jnp.concatenate(dks_l, axis=1), axis=0, keepdims=True)
            db = q_all * dq_all - k_all * dk_all + jnp.where(last_row, extra, 0.0)
            df = _dot_hi(triu, db) / f - dk_all
            dlb_acc = dlb_acc + jnp.sum(df * (1.0 - sg), axis=0, keepdims=True)
            dh_ref[rows, 0:B_WIDTH] = (dq_all * (sq * (1.0 + qb * (1.0 - sq)))).astype(BF16)
            dh_ref[rows, B_WIDTH:2 * B_WIDTH] = (df * (1.0 - lbv) * sg * (1.0 - sg)).astype(BF16)
            dh_ref[rows, 2 * B_WIDTH:3 * B_WIDTH] = jnp.concatenate(dv_l, axis=1).astype(BF16)
            dh_ref[rows, 3 * B_WIDTH:4 * B_WIDTH] = (dy_all * jnp.concatenate(on_l, axis=1)
                                                     * sgb * (1.0 - sgb)).astype(BF16)
            return dng_acc, dlb_acc

        dng_sum, dlb_sum = lax.fori_loop(0, ncb, chunk, (jnp.zeros((1, HD), F32), jnp.zeros((1, B_WIDTH), F32)))
        dng_ref[...] += dng_sum
        dlb_ref[...] += dlb_sum

    rev = lambda i: (nb - 1 - i, 0)
    return pl.pallas_call(
        body,
        name=name,
        grid=(nb,),
        in_specs=[pl.BlockSpec((bt, B_WIDTH), rev), pl.BlockSpec((bt, 4 * B_WIDTH), lambda i: (nb - 1 - i, 1)),
                  pl.BlockSpec((bt, B_WIDTH), rev),
                  pl.BlockSpec((ncb, B_HEADS, HD, HD), lambda i: (nb - 1 - i, 0, 0, 0)),
                  pl.BlockSpec((1, B_WIDTH), lambda i: (0, 0)), pl.BlockSpec((1, HD), lambda i: (0, 0))],
        out_specs=[pl.BlockSpec((bt, 4 * B_WIDTH), rev), pl.BlockSpec((1, B_WIDTH), lambda i: (0, 0)),
                   pl.BlockSpec((1, HD), lambda i: (0, 0))],
        out_shape=[jax.ShapeDtypeStruct((t, 4 * B_WIDTH), BF16), jax.ShapeDtypeStruct((1, B_WIDTH), F32),
                   jax.ShapeDtypeStruct((1, HD), F32)],
        scratch_shapes=[pltpu.VMEM((B_HEADS, HD, HD), F32)],
        compiler_params=_params("arbitrary"),
    )(dy, rest, o_saved, states, lb, ng)


def _axpy2(c0, a0, c1, a1, name, tm=512):
    t, d = a0.shape
    tm = min(tm, t)

    def body(a_ref, b_ref, o_ref):
        o_ref[...] = c0 * a_ref[...] + c1 * b_ref[...]

    row = lambda i: (i, 0)
    return pl.pallas_call(
        body, name=name, grid=(t // tm,),
        in_specs=[pl.BlockSpec((tm, d), row), pl.BlockSpec((tm, d), row)],
        out_specs=pl.BlockSpec((tm, d), row),
        out_shape=jax.ShapeDtypeStruct((t, d), F32),
        compiler_params=_params("parallel"),
    )(a0, a1)


def _split_w_in(w_in_l):
    wqkv = w_in_l[:, :3 * A_WIDTH]
    wfa = jnp.pad(w_in_l[:, 3 * A_WIDTH:3 * A_WIDTH + A_HEADS], ((0, 0), (0, 128 - A_HEADS)))
    whb = w_in_l[:, 3 * A_WIDTH + A_HEADS:3 * A_WIDTH + A_HEADS + 4 * B_WIDTH]
    wgt = w_in_l[:, 3 * A_WIDTH + A_HEADS + 4 * B_WIDTH:]
    return wqkv, jnp.concatenate([wgt, whb, wfa], axis=1)


def _merge_w_in_grad(dwall):
    o = 3 * A_WIDTH
    return jnp.concatenate([dwall[:, :o], dwall[:, o + 4096:o + 4096 + A_HEADS], dwall[:, o + 2048:o + 4096],
                            dwall[:, o:o + 2048]], axis=1)


def _layer_fwd(x, xb, w, sp, l):
    t = x.shape[0]
    n = f"l{l}_"
    wqkv, wrest = _split_w_in(w["w_in"])
    qkv = _matmul(xb, wqkv, "nn", BF16, MM_ROWS, 768, D_MODEL, n + "proj_qkv")
    rest = _matmul(xb, wrest, "nn", F32, MM_ROWS, 1408, D_MODEL, n + "proj_rest")
    bf = jnp.pad(sp["b_fgate"], (0, 128 - A_HEADS)).reshape(1, 128)
    fcum, fcol = _fox_gate_fwd(rest, bf, n + "fox_gate_fwd")
    qa, ka, va, qn, kn = _fox_prep_fwd(qkv, fcol, n + "fox_prep_fwd")
    kstart, qend = _fox_block_ranges(qn, kn, fcum)
    ya, lse = _fox_fwd(qa, ka, va, kstart, n + "fox_fwd")
    lb = sp["lb"].reshape(1, B_WIDTH)
    ng = sp["norm_g"].reshape(1, HD)
    yb, ob, states = _hgrn_fwd(rest, lb, ng, n + "hgrn_fwd")
    merged = _merge_fwd(ya, yb, w["w_pa"], w["w_pb"], rest, n + "merge_fwd")
    x1, x1b, xh1, rs1 = _mm_res_ln(merged, w["w_out"], x, sp["ln1_g"], sp["ln1_b"], n + "out_ln1")
    wu, wg = w["w_ff_in"][:, :FFN_HIDDEN], w["w_ff_in"][:, FFN_HIDDEN:]
    a, hu, hg = _ffn_in_swiglu(x1b, wu, wg, n + "ffn_in_swiglu")
    x2, x2b, xh2, rs2 = _mm_res_ln(a, w["w_ff_out"], x1, sp["ln2_g"], sp["ln2_b"], n + "ffn_out_ln2")
    saved = dict(xb=xb, wqkv=wqkv, wrest=wrest, qkv=qkv, rest=rest, bf=bf, fcol=fcol, ka=ka, va=va, ya=ya, lse=lse,
                 kstart=kstart, qend=qend,
                 lb=lb, ng=ng, yb=yb, ob=ob, states=states, merged=merged, x1b=x1b, xh1=xh1, rs1=rs1, a=a,
                 wu=wu, wg=wg, hu=hu, hg=hg,
                 xh2=xh2, rs2=rs2)
    return x2, x2b, saved


def _layer_bwd(dys, coefs, w, sp, s, l):
    n = f"l{l}_"
    dz2, dz2b, dg2, db2 = _ln_bwd(dys, coefs, s["xh2"], s["rs2"], sp["ln2_g"], n + "ln2_bwd")
    du, dg = _ffn_out_dx_swiglu(dz2b, w["w_ff_out"], s["hu"], s["hg"], n + "ffn_out_dx_swiglu")
    d_wffout = _matmul(s["a"], dz2b, "tn", F32, 1408, 1024, DW_ROWS, n + "ffn_out_dw")
    dx1u = _matmul(du, s["wu"], "nt", F32, MM_ROWS, 1024, FFN_HIDDEN, n + "ffn_in_dx_u")
    dx1g = _matmul(dg, s["wg"], "nt", F32, MM_ROWS, 1024, FFN_HIDDEN, n + "ffn_in_dx_g")
    d_wffin = jnp.concatenate([_matmul(s["x1b"], du, "tn", F32, 1024, 1408, DW_ROWS, n + "ffn_in_dw_u"),
                               _matmul(s["x1b"], dg, "tn", F32, 1024, 1408, DW_ROWS, n + "ffn_in_dw_g")], axis=1)
    dz1, dz1b, dg1, db1 = _ln_bwd([dz2, dx1u, dx1g], [ALPHA, 1.0, 1.0], s["xh1"], s["rs1"], sp["ln1_g"],
                                  n + "ln1_bwd")
    d_wout = _matmul(s["merged"], dz1b, "tn", F32, 1024, 1024, DW_ROWS, n + "out_dw")
    dgates, dpa, dpb, dya, dyb = _merge_bwd(dz1b, w["w_out"], s["ya"], s["yb"], w["w_pa"], w["w_pb"], s["rest"],
                                  n + "merge_bwd")
    d_wpa = _matmul(s["ya"], dpa, "tn", F32, 512, 1024, DW_ROWS, n + "pa_dw")
    d_wpb = _matmul(s["yb"], dpb, "tn", F32, 512, 1024, DW_ROWS, n + "pb_dw")
    qb, dob = _fox_prep_bwd(s["qkv"], s["fcol"], s["lse"], dya, s["ya"], n + "fox_prep_bwd")
    dk, dv, csum, dq, rsum = _fox_bwd(qb, s["ka"], s["va"], dob, s["qend"], n + "fox_bwd")
    dfa, dbf = _fox_gate_bwd(rsum, csum, s["rest"], s["bf"], n + "fox_gate_bwd")
    dhb, dlb, dng = _hgrn_bwd(dyb, s["rest"], s["ob"], s["states"], s["lb"], s["ng"], n + "hgrn_bwd")
    dproj = jnp.concatenate([dq, dk, dv, dgates, dhb, dfa], axis=1)
    wall = jnp.concatenate([s["wqkv"], s["wrest"]], axis=1)
    dxp = _matmul(dproj, wall, "nt", F32, MM_ROWS, 1024, 1920, n + "proj_dx")
    d_wall = _matmul(s["xb"], dproj, "tn", F32, 1024, 1152, DW_ROWS, n + "proj_dw")
    grads = dict(w_in=_merge_w_in_grad(d_wall), w_pa=d_wpa, w_pb=d_wpb, w_out=d_wout, w_ff_in=d_wffin,
                 w_ff_out=d_wffout, b_fgate=dbf[0, :A_HEADS], lb=dlb[0], norm_g=dng[0], ln1_g=dg1[0], ln1_b=db1[0],
                 ln2_g=dg2[0], ln2_b=db2[0])
    return [dz1, dxp], [ALPHA, 1.0], grads


def _lower_bounds(logits):
    sm = jax.nn.softmax(logits.astype(F32), axis=0)
    return jnp.cumsum(sm, axis=0) - sm[0:1]


def _local_step(x, target, wfull, small):
    lbs, lb_vjp = jax.vjp(_lower_bounds, small["hgrn_lb_logits"])
    h, hb = x, x.astype(BF16)
    saved, sps = [], []
    for l in range(DEPTH):
        sp = dict(b_fgate=small["b_fgate"][l], lb=lbs[l], norm_g=small["hgrn_norm_g"][l], ln1_g=small["ln1_g"][l],
                  ln1_b=small["ln1_b"][l], ln2_g=small["ln2_g"][l], ln2_b=small["ln2_b"][l])
        h, hb, s = _layer_fwd(h, hb, wfull[l], sp, l)
        saved.append(s)
        sps.append(sp)
    dy, lpart = _loss_head(h, target)
    dys, coefs = [dy], [1.0]
    grads = [None] * DEPTH
    for l in reversed(range(DEPTH)):
        dys, coefs, grads[l] = _layer_bwd(dys, coefs, wfull[l], sps[l], saved[l], l)
    grad_x = _axpy2(coefs[0], dys[0], coefs[1], dys[1], "grad_x")
    d_logits = lb_vjp(jnp.stack([grads[l]["lb"] for l in range(DEPTH)]))[0]
    return lpart[0, 0], grad_x, grads, d_logits


_BIG = [("w_in", "w_in", (D_MODEL, IN_TOTAL), 1), ("w_branch_a", "w_pa", (A_WIDTH, D_MODEL), 1),
        ("w_branch_b", "w_pb", (B_WIDTH, D_MODEL), 1), ("w_out", "w_out", (D_MODEL, D_MODEL), 0),
        ("w_ff_in", "w_ff_in", (D_MODEL, 2 * FFN_HIDDEN), 1), ("w_ff_out", "w_ff_out", (FFN_HIDDEN, D_MODEL), 0)]
_SMALL = [("b_fgate", A_HEADS), ("hgrn_lb_logits", B_WIDTH), ("hgrn_norm_g", HD), ("ln1_g", D_MODEL),
          ("ln1_b", D_MODEL), ("ln2_g", D_MODEL), ("ln2_b", D_MODEL)]
N_BIG = len(_BIG)
SMALL_ROWS = 80


def _by_chip(full, axis):
    if axis == 0:
        return full.reshape(N_CHIPS, full.shape[0] // N_CHIPS, full.shape[1])
    n = full.shape[1] // N_CHIPS
    return jnp.stack([full[:, q * n:(q + 1) * n] for q in range(N_CHIPS)])


def _from_chips(shards, axis):
    if axis == 0:
        return shards.reshape(N_CHIPS * shards.shape[1], shards.shape[2])
    return jnp.concatenate([shards[q] for q in range(N_CHIPS)], axis=1)


def _pack_small(per_name):
    flat = jnp.concatenate([per_name[name].reshape(-1) for name, _ in _SMALL])
    return jnp.pad(flat, (0, SMALL_ROWS * 128 - flat.shape[0])).reshape(SMALL_ROWS, 128)


def _unpack_small(slab):
    flat, out, r = slab.reshape(-1), {}, 0
    for name, n in _SMALL:
        out[name] = flat[r:r + DEPTH * n].reshape(DEPTH, n)
        r += DEPTH * n
    return out


_ANY = pl.BlockSpec(memory_space=pl.ANY)


def _place():
    return lax.axis_index("x"), lax.axis_index("y"), lax.axis_index("c")


def _other_chips(x, y):
    return [(1 - x, y), (x, 1 - y), (1 - x, 1 - y)]


def _chip_exchange(mine_of, out_refs, send_sems, recv_sems, local_sems):
    x, y, c = _place()
    q = 2 * x + y
    started = []
    for w, out_ref in enumerate(out_refs):
        local = pltpu.make_async_copy(mine_of(w, q), out_ref.at[q], local_sems.at[w])
        local.start()
        started.append(local)
    sends = []
    for k, (px, py) in enumerate(_other_chips(x, y)):
        for w, out_ref in enumerate(out_refs):
            cp = pltpu.make_async_remote_copy(src_ref=mine_of(w, 2 * px + py), dst_ref=out_ref.at[q],
                                              send_sem=send_sems.at[3 * w + k], recv_sem=recv_sems.at[3 * w + k],
                                              device_id=(px, py, c), device_id_type=MESH)
            cp.start()
            sends.append(cp)
    for k, (px, py) in enumerate(_other_chips(x, y)):
        for w, out_ref in enumerate(out_refs):
            pltpu.make_async_remote_copy(src_ref=mine_of(w, q), dst_ref=out_ref.at[2 * px + py],
                                         send_sem=send_sems.at[3 * w + k], recv_sem=recv_sems.at[3 * w + k],
                                         device_id=(px, py, c), device_id_type=MESH).wait_recv()
    for cp in sends:
        cp.wait_send()
    for local in started:
        local.wait()


def _sem_scratch(n):
    return [pltpu.SemaphoreType.DMA((3 * n,)), pltpu.SemaphoreType.DMA((3 * n,)), pltpu.SemaphoreType.DMA((n,))]


def _gather_weights(mine):
    n = len(mine)

    def body(*refs):
        in_refs, out_refs = refs[:n], refs[n:2 * n]
        send_sems, recv_sems, local_sems, pair_send, pair_recv = refs[2 * n:]
        x, y, c = _place()
        _chip_exchange(lambda w, q: in_refs[w].at[c], [o.at[c] for o in out_refs], send_sems, recv_sems, local_sems)
        sibling = (x, y, 1 - c)
        fwds = []
        for w, o in enumerate(out_refs):
            cp = pltpu.make_async_remote_copy(src_ref=o.at[c], dst_ref=o.at[c], send_sem=pair_send.at[w],
                                              recv_sem=pair_recv.at[w], device_id=sibling, device_id_type=MESH)
            cp.start()
            fwds.append(cp)
        for w, o in enumerate(out_refs):
            pltpu.make_async_remote_copy(src_ref=o.at[1 - c], dst_ref=o.at[1 - c], send_sem=pair_send.at[w],
                                         recv_sem=pair_recv.at[w], device_id=sibling, device_id_type=MESH).wait_recv()
        for cp in fwds:
            cp.wait_send()

    return pl.pallas_call(
        body, name="gather_weights", in_specs=[_ANY] * n, out_specs=[_ANY] * n,
        out_shape=[jax.ShapeDtypeStruct((DEPTH, N_CHIPS) + m.shape[1:], m.dtype) for m in mine],
        scratch_shapes=_sem_scratch(n) + [pltpu.SemaphoreType.DMA((n,)), pltpu.SemaphoreType.DMA((n,))],
    )(*mine)


def _pair_exchange(gs):
    n = len(gs)

    def body(*refs):
        g_refs, a_refs, send_sems, recv_sems = refs[:n], refs[n:2 * n], refs[2 * n], refs[2 * n + 1]
        x, y, c = _place()
        cps = []
        for w in range(n):
            cp = pltpu.make_async_remote_copy(src_ref=g_refs[w].at[1 - c], dst_ref=a_refs[w], send_sem=send_sems.at[w],
                                              recv_sem=recv_sems.at[w], device_id=(x, y, 1 - c), device_id_type=MESH)
            cp.start()
            cps.append(cp)
        for cp in cps:
            cp.wait()

    return pl.pallas_call(
        body, name="grad_pair_exchange", in_specs=[_ANY] * n, out_specs=[_ANY] * n,
        out_shape=[jax.ShapeDtypeStruct(g.shape[1:], g.dtype) for g in gs],
        scratch_shapes=[pltpu.SemaphoreType.DMA((n,)), pltpu.SemaphoreType.DMA((n,))],
    )(*gs)


def _row_block(rows):
    return math.gcd(rows, 256)


def _pair_sum(g, a, layer, name):
    _, nq, rows, cols = g.shape
    tb = _row_block(rows)

    def body(l_ref, g_ref, a_ref, o_ref):
        o_ref[...] = (g_ref[...] + a_ref[...]).astype(BF16)

    return pl.pallas_call(
        body, name=name,
        grid_spec=pltpu.PrefetchScalarGridSpec(
            num_scalar_prefetch=1, grid=(nq, rows // tb),
            in_specs=[pl.BlockSpec((None, None, tb, cols), lambda q, i, l_ref: (l_ref[0], q, i, 0)),
                      pl.BlockSpec((None, tb, cols), lambda q, i, l_ref: (q, i, 0))],
            out_specs=pl.BlockSpec((None, tb, cols), lambda q, i, l_ref: (q, i, 0))),
        out_shape=jax.ShapeDtypeStruct((nq, rows, cols), BF16),
        compiler_params=_params("parallel", "parallel"),
    )(layer.reshape(1).astype(jnp.int32), g, a)


def _shard_exchange(ps):
    n = len(ps)

    def body(*refs):
        p_refs, b_refs = refs[:n], refs[n:2 * n]
        send_sems, recv_sems, local_sems = refs[2 * n:]
        _chip_exchange(lambda w, q: p_refs[w].at[q], b_refs, send_sems, recv_sems, local_sems)

    return pl.pallas_call(
        body, name="grad_shard_exchange", in_specs=[_ANY] * n, out_specs=[_ANY] * n,
        out_shape=[jax.ShapeDtypeStruct(p.shape, p.dtype) for p in ps],
        scratch_shapes=_sem_scratch(n),
    )(*ps)


def _sum4(b, name):
    _, rows, cols = b.shape
    tb = _row_block(rows)

    def body(b_ref, o_ref):
        o_ref[...] = ((b_ref[0].astype(F32) + b_ref[1].astype(F32)) + b_ref[2].astype(F32)) + b_ref[3].astype(F32)

    return pl.pallas_call(
        body, name=name, grid=(rows // tb,),
        in_specs=[pl.BlockSpec((N_CHIPS, tb, cols), lambda i: (0, i, 0))],
        out_specs=pl.BlockSpec((tb, cols), lambda i: (i, 0)),
        out_shape=jax.ShapeDtypeStruct((rows, cols), F32),
        compiler_params=_params("parallel"),
    )(b)


def _result_exchange(gcs):
    n = len(gcs)

    def body(*refs):
        g_refs, o_refs, send_sems, recv_sems = refs[:n], refs[n:2 * n], refs[2 * n], refs[2 * n + 1]
        x, y, c = _place()
        cps = []
        for w in range(n):
            cp = pltpu.make_async_remote_copy(src_ref=g_refs[w], dst_ref=o_refs[w], send_sem=send_sems.at[w],
                                              recv_sem=recv_sems.at[w], device_id=(x, y, 1 - c), device_id_type=MESH)
            cp.start()
            cps.append(cp)
        for cp in cps:
            cp.wait()

    return pl.pallas_call(
        body, name="grad_result_exchange", in_specs=[_ANY] * n, out_specs=[_ANY] * n,
        out_shape=[jax.ShapeDtypeStruct(g.shape, g.dtype) for g in gcs],
        scratch_shapes=[pltpu.SemaphoreType.DMA((n,)), pltpu.SemaphoreType.DMA((n,))],
    )(*gcs)


def _allreduce_small(v):
    def body(v_ref, o_ref, buf, send_sems, recv_sems):
        x, y, c = _place()
        me = 4 * x + 2 * y + c
        buf[me] = v_ref[...]
        peers = []
        for k in range(1, N_DEV):
            px = 1 - x if k & 4 else x
            py = 1 - y if k & 2 else y
            pc = 1 - c if k & 1 else c
            peers.append((px, py, pc))
        sends = []
        for k, peer in enumerate(peers):
            cp = pltpu.make_async_remote_copy(src_ref=v_ref, dst_ref=buf.at[me], send_sem=send_sems.at[k],
                                              recv_sem=recv_sems.at[k], device_id=peer, device_id_type=MESH)
            cp.start()
            sends.append(cp)
        for k, (px, py, pc) in enumerate(peers):
            pltpu.make_async_remote_copy(src_ref=v_ref, dst_ref=buf.at[4 * px + 2 * py + pc], send_sem=send_sems.at[k],
                                         recv_sem=recv_sems.at[k], device_id=(px, py, pc),
                                         device_id_type=MESH).wait_recv()
        for cp in sends:
            cp.wait_send()
        acc = buf[0]
        for i in range(1, N_DEV):
            acc = acc + buf[i]
        o_ref[...] = acc

    vm = pl.BlockSpec(memory_space=pltpu.VMEM)
    return pl.pallas_call(
        body, name="small_allreduce", in_specs=[vm], out_specs=vm,
        out_shape=jax.ShapeDtypeStruct(v.shape, F32),
        scratch_shapes=[pltpu.VMEM((N_DEV,) + v.shape, F32), pltpu.SemaphoreType.DMA((N_DEV - 1,)),
                        pltpu.SemaphoreType.DMA((N_DEV - 1,))],
    )(v)


def _adam_update(w, g, m, v):
    nm = ADAM_B1 * m + (1.0 - ADAM_B1) * g
    nv = ADAM_B2 * v + (1.0 - ADAM_B2) * (g * g)
    m_hat = nm / (1.0 - ADAM_B1 ** ADAM_STEP)
    v_hat = nv / (1.0 - ADAM_B2 ** ADAM_STEP)
    return -ADAM_LR * (m_hat / (jnp.sqrt(v_hat) + ADAM_EPS) + ADAM_WD * w), nm, nv


def _adamw_small(w, g, m, v, name):
    def body(w_ref, g_ref, m_ref, v_ref, d_ref, nm_ref, nv_ref):
        d_ref[...], nm_ref[...], nv_ref[...] = _adam_update(w_ref[...], g_ref[...], m_ref[...], v_ref[...])

    vm = pl.BlockSpec(memory_space=pltpu.VMEM)
    return pl.pallas_call(
        body, name=name, in_specs=[vm] * 4, out_specs=[vm] * 3,
        out_shape=[jax.ShapeDtypeStruct(w.shape, F32)] * 3,
    )(w, g, m, v)


def _adamw_big(w, m, v, g_own, g_other, layer, name):
    _, rows, cols = w.shape
    tb = _row_block(rows)

    def body(l_ref, w_ref, m_ref, v_ref, go_ref, gx_ref, g_ref, d_ref, nm_ref, nv_ref):
        gv = jnp.where(pl.program_id(0) == l_ref[0], go_ref[...], gx_ref[...])
        g_ref[...] = gv
        d_ref[...], nm_ref[...], nv_ref[...] = _adam_update(w_ref[...], gv, m_ref[...], v_ref[...])

    per_layer = pl.BlockSpec((None, tb, cols), lambda l, i, l_ref: (l, i, 0))
    shared = pl.BlockSpec((tb, cols), lambda l, i, l_ref: (i, 0))
    return pl.pallas_call(
        body, name=name,
        grid_spec=pltpu.PrefetchScalarGridSpec(
            num_scalar_prefetch=1, grid=(DEPTH, rows // tb),
            in_specs=[per_layer, per_layer, per_layer, shared, shared], out_specs=[per_layer] * 4),
        out_shape=[jax.ShapeDtypeStruct(w.shape, F32)] * 4,
        compiler_params=_params("parallel", "parallel"),
    )(layer.reshape(1).astype(jnp.int32), w, m, v, g_own, g_other)


def kernel(x, w_in, b_fgate, hgrn_lb_logits, hgrn_norm_g, w_branch_a, w_branch_b, w_out, ln1_g, ln1_b, w_ff_in, w_ff_out, ln2_g, ln2_b, loss_target, m_w_in, m_b_fgate, m_hgrn_lb_logits, m_hgrn_norm_g, m_w_branch_a, m_w_branch_b, m_w_out, m_ln1_g, m_ln1_b, m_w_ff_in, m_w_ff_out, m_ln2_g, m_ln2_b, v_w_in, v_b_fgate, v_hgrn_lb_logits, v_hgrn_norm_g, v_w_branch_a, v_w_branch_b, v_w_out, v_ln1_g, v_ln1_b, v_w_ff_in, v_w_ff_out, v_ln2_g, v_ln2_b):
    weights = dict(w_in=w_in, b_fgate=b_fgate, hgrn_lb_logits=hgrn_lb_logits, hgrn_norm_g=hgrn_norm_g,
                   w_branch_a=w_branch_a, w_branch_b=w_branch_b, w_out=w_out, ln1_g=ln1_g, ln1_b=ln1_b,
                   w_ff_in=w_ff_in, w_ff_out=w_ff_out, ln2_g=ln2_g, ln2_b=ln2_b)
    mom1 = dict(w_in=m_w_in, b_fgate=m_b_fgate, hgrn_lb_logits=m_hgrn_lb_logits, hgrn_norm_g=m_hgrn_norm_g,
                w_branch_a=m_w_branch_a, w_branch_b=m_w_branch_b, w_out=m_w_out, ln1_g=m_ln1_g, ln1_b=m_ln1_b,
                w_ff_in=m_w_ff_in, w_ff_out=m_w_ff_out, ln2_g=m_ln2_g, ln2_b=m_ln2_b)
    mom2 = dict(w_in=v_w_in, b_fgate=v_b_fgate, hgrn_lb_logits=v_hgrn_lb_logits, hgrn_norm_g=v_hgrn_norm_g,
                w_branch_a=v_w_branch_a, w_branch_b=v_w_branch_b, w_out=v_w_out, ln1_g=v_ln1_g, ln1_b=v_ln1_b,
                w_ff_in=v_w_ff_in, w_ff_out=v_w_ff_out, ln2_g=v_ln2_g, ln2_b=v_ln2_b)
    core = lax.axis_index("c")

    gathered = _gather_weights([weights[name].astype(BF16) for name, _, _, _ in _BIG])
    wfull = [{key: _from_chips(gathered[w][l], axis) for w, (_, key, _, axis) in enumerate(_BIG)}
             for l in range(DEPTH)]
    small = {name: weights[name] for name, _ in _SMALL}

    loss_part, grad_x, grads, d_logits = _local_step(x[0], loss_target[0], wfull, small)

    g_all = [jnp.stack([_by_chip(grads[l][key], axis) for l in range(DEPTH)]) for _, key, _, axis in _BIG]
    received = _pair_exchange(g_all)
    pair = [_pair_sum(g_all[w], received[w], core, f"grad_pair_sum_{w}") for w in range(N_BIG)]
    by_chip = _shard_exchange(pair)
    g_layer = [_sum4(by_chip[w], f"grad_chip_sum_{w}") for w in range(N_BIG)]
    g_other = _result_exchange(g_layer)
    out_g, out_d, out_m, out_v = {}, {}, {}, {}
    for w, (name, _, _, _) in enumerate(_BIG):
        out_g[name], out_d[name], out_m[name], out_v[name] = _adamw_big(
            weights[name], mom1[name], mom2[name], g_layer[w], g_other[w], core, f"adamw_{name}")

    small_grads = {name: jnp.stack([grads[l][key] for l in range(DEPTH)])
                   for name, key in [("b_fgate", "b_fgate"), ("hgrn_norm_g", "norm_g"), ("ln1_g", "ln1_g"),
                                     ("ln1_b", "ln1_b"), ("ln2_g", "ln2_g"), ("ln2_b", "ln2_b")]}
    small_grads["hgrn_lb_logits"] = d_logits
    gs = _allreduce_small(_pack_small(small_grads))
    ds, ms, vs = _adamw_small(_pack_small(small), gs, _pack_small({n: mom1[n] for n, _ in _SMALL}),
                              _pack_small({n: mom2[n] for n, _ in _SMALL}), "adamw_small")
    for tree, slab in ((out_g, gs), (out_d, ds), (out_m, ms), (out_v, vs)):
        tree.update(_unpack_small(slab))

    loss = lax.psum(loss_part, ("x", "y", "c"))
    order = ["w_in", "b_fgate", "hgrn_lb_logits", "hgrn_norm_g", "w_branch_a", "w_branch_b", "w_out", "ln1_g", "ln1_b",
             "w_ff_in", "w_ff_out", "ln2_g", "ln2_b"]
    return (loss, grad_x[None], *[out_g[n] for n in order], *[out_d[n] for n in order],
            *[out_m[n] for n in order], *[out_v[n] for n in order])
```

```python
import functools
import math

import jax
import jax.numpy as jnp
import numpy as np
from jax import lax
from jax.experimental import pallas as pl
from jax.experimental.pallas import tpu as pltpu

F32 = jnp.float32
BF16 = jnp.bfloat16

D_MODEL = 1024
DEPTH = 2
A_HEADS = 8
A_WIDTH = 512
B_WIDTH = 512
B_HEADS = 4
HD = 128
CHUNK = 64
SUB = 16
FFN_HIDDEN = 2816
IN_TOTAL = 5640
ALPHA = (2 * DEPTH) ** 0.25
LN_EPS = 1e-5
RMS_EPS = 1e-6
ADAM_LR = 0.001
ADAM_B1 = 0.9
ADAM_B2 = 0.999
ADAM_EPS = 1e-08
ADAM_WD = 0.01
ADAM_STEP = 10
EXP_CLAMP = 60.0

VMEM_LIMIT_BYTES = 56 * 1024 * 1024
MM_ROWS = 1024
DW_ROWS = 2048
N_CHIPS = 4
N_DEV = 8
MESH = pl.DeviceIdType.MESH

_DN = {
    "nn": (((1,), (0,)), ((), ())),
    "nt": (((1,), (1,)), ((), ())),
    "tn": (((0,), (0,)), ((), ())),
}


def _dot(a, b, mode="nn"):
    return lax.dot_general(a.astype(BF16), b.astype(BF16), _DN[mode], preferred_element_type=F32)


def _pieces(x):
    h = x.astype(BF16)
    r = x - h.astype(F32)
    m = r.astype(BF16)
    return h, m, (r - m.astype(F32)).astype(BF16)


def _dot_hi(a, b, mode="nn", exact="a"):
    if exact == "a":
        h, m, l = _pieces(b)
        return (_dot(a, l, mode) + _dot(a, m, mode)) + _dot(a, h, mode)
    h, m, l = _pieces(a)
    return (_dot(l, b, mode) + _dot(m, b, mode)) + _dot(h, b, mode)


def _hdot(a, b, mode="nn"):
    bh, bl, _ = _pieces(b)
    return _dot(a, bl, mode) + _dot(a, bh, mode)


def _params(*sem):
    return pltpu.CompilerParams(dimension_semantics=sem, vmem_limit_bytes=VMEM_LIMIT_BYTES)


def _sigmoid(x):
    return 1.0 / (1.0 + jnp.exp(-x))


def _matmul(a, b, mode, out_dtype, tm, tn, tk, name):
    if mode == "nn":
        (m, k), (k2, n) = a.shape, b.shape
    elif mode == "nt":
        (m, k), (n, k2) = a.shape, b.shape
    else:
        (k, m), (k2, n) = a.shape, b.shape
    assert k == k2, (a.shape, b.shape, mode)
    tm, tn, tk = min(tm, m), min(tn, n), min(tk, k)
    assert m % tm == 0 and n % tn == 0 and k % tk == 0, (a.shape, b.shape, tm, tn, tk)
    nk = k // tk
    if mode == "tn":
        a_spec = pl.BlockSpec((tk, tm), lambda j, i, kk: (kk, i))
    else:
        a_spec = pl.BlockSpec((tm, tk), lambda j, i, kk: (i, kk))
    if mode == "nt":
        b_spec = pl.BlockSpec((tn, tk), lambda j, i, kk: (j, kk))
    else:
        b_spec = pl.BlockSpec((tk, tn), lambda j, i, kk: (kk, j))
    use_acc = nk > 1 and out_dtype != F32

    def body(a_ref, b_ref, o_ref, *scratch):
        p = _dot(a_ref[...], b_ref[...], mode)
        if nk == 1:
            o_ref[...] = p.astype(out_dtype)
            return
        acc_ref = scratch[0] if use_acc else o_ref
        kk = pl.program_id(2)

        @pl.when(kk == 0)
        def _():
            acc_ref[...] = p

        @pl.when(kk > 0)
        def _():
            acc_ref[...] += p

        if use_acc:
            @pl.when(kk == nk - 1)
            def _():
                o_ref[...] = acc_ref[...].astype(out_dtype)

    return pl.pallas_call(
        body,
        name=name,
        grid=(n // tn, m // tm, nk),
        in_specs=[a_spec, b_spec],
        out_specs=pl.BlockSpec((tm, tn), lambda j, i, kk: (i, j)),
        out_shape=jax.ShapeDtypeStruct((m, n), out_dtype),
        scratch_shapes=[pltpu.VMEM((tm, tn), F32)] if use_acc else [],
        compiler_params=_params("parallel", "parallel", "arbitrary"),
    )(a, b)


def _mm_res_ln(a, w, res, g, b, name, tm=512):
    t, k = a.shape
    d = w.shape[1]
    tm = min(tm, t)

    def body(a_ref, w_ref, r_ref, g_ref, b_ref, y_ref, yb_ref, xh_ref, rs_ref):
        z = ALPHA * r_ref[...] + _dot(a_ref[...], w_ref[...])
        mu = jnp.mean(z, axis=-1, keepdims=True)
        zc = z - mu
        var = jnp.mean(zc * zc, axis=-1, keepdims=True)
        rstd = lax.rsqrt(var + LN_EPS)
        xh = zc * rstd
        y = xh * g_ref[...] + b_ref[...]
        y_ref[...] = y
        yb_ref[...] = y.astype(BF16)
        xh_ref[...] = xh
        rs_ref[...] = rstd

    row = lambda i: (i, 0)
    fix = lambda i: (0, 0)
    return pl.pallas_call(
        body,
        name=name,
        grid=(t // tm,),
        in_specs=[pl.BlockSpec((tm, k), row), pl.BlockSpec((k, d), fix), pl.BlockSpec((tm, d), row),
                  pl.BlockSpec((1, d), fix), pl.BlockSpec((1, d), fix)],
        out_specs=[pl.BlockSpec((tm, d), row), pl.BlockSpec((tm, d), row), pl.BlockSpec((tm, d), row),
                   pl.BlockSpec((tm, 1), row)],
        out_shape=[jax.ShapeDtypeStruct((t, d), F32), jax.ShapeDtypeStruct((t, d), BF16),
                   jax.ShapeDtypeStruct((t, d), F32), jax.ShapeDtypeStruct((t, 1), F32)],
        compiler_params=_params("parallel"),
    )(a, w, res, g.reshape(1, d), b.reshape(1, d))


def _ln_bwd(dys, coefs, xhat, rstd, g, name, tm=512):
    t, d = xhat.shape
    tm = min(tm, t)
    n_in = len(dys)

    def body(*refs):
        dy_refs = refs[:n_in]
        xh_ref, rs_ref, g_ref, dz_ref, dzb_ref, dg_ref, db_ref = refs[n_in:]
        dy = coefs[0] * dy_refs[0][...].astype(F32)
        for c, r in zip(coefs[1:], dy_refs[1:]):
            dy = dy + c * r[...].astype(F32)
        xh = xh_ref[...]
        dxh = dy * g_ref[...]
        m1 = jnp.mean(dxh, axis=-1, keepdims=True)
        m2 = jnp.mean(dxh * xh, axis=-1, keepdims=True)
        dz = rs_ref[...] * (dxh - m1 - xh * m2)
        dz_ref[...] = dz
        dzb_ref[...] = dz.astype(BF16)
        pg = jnp.sum(dy * xh, axis=0, keepdims=True)
        pb = jnp.sum(dy, axis=0, keepdims=True)

        @pl.when(pl.program_id(0) == 0)
        def _():
            dg_ref[...] = pg
            db_ref[...] = pb

        @pl.when(pl.program_id(0) > 0)
        def _():
            dg_ref[...] += pg
            db_ref[...] += pb

    row = lambda i: (i, 0)
    fix = lambda i: (0, 0)
    return pl.pallas_call(
        body,
        name=name,
        grid=(t // tm,),
        in_specs=[pl.BlockSpec((tm, d), row)] * n_in
        + [pl.BlockSpec((tm, d), row), pl.BlockSpec((tm, 1), row), pl.BlockSpec((1, d), fix)],
        out_specs=[pl.BlockSpec((tm, d), row), pl.BlockSpec((tm, d), row), pl.BlockSpec((1, d), fix),
                   pl.BlockSpec((1, d), fix)],
        out_shape=[jax.ShapeDtypeStruct((t, d), F32), jax.ShapeDtypeStruct((t, d), BF16),
                   jax.ShapeDtypeStruct((1, d), F32), jax.ShapeDtypeStruct((1, d), F32)],
        compiler_params=_params("arbitrary"),
    )(*dys, xhat, rstd, g.reshape(1, d))


def _loss_head(y, target, name="loss_head", tm=512):
    t, d = y.shape
    tm = min(tm, t)

    def body(y_ref, t_ref, dy_ref, l_ref):
        e = y_ref[...] - t_ref[...]
        dy_ref[...] = e * (1.0 / d)
        part = jnp.full((8, 128), 0.5 / d, F32) * jnp.sum(e * e)

        @pl.when(pl.program_id(0) == 0)
        def _():
            l_ref[...] = part

        @pl.when(pl.program_id(0) > 0)
        def _():
            l_ref[...] += part

    row = lambda i: (i, 0)
    return pl.pallas_call(
        body,
        name=name,
        grid=(t // tm,),
        in_specs=[pl.BlockSpec((tm, d), row), pl.BlockSpec((tm, d), row)],
        out_specs=[pl.BlockSpec((tm, d), row), pl.BlockSpec((8, 128), lambda i: (0, 0))],
        out_shape=[jax.ShapeDtypeStruct((t, d), F32), jax.ShapeDtypeStruct((8, 128), F32)],
        compiler_params=_params("arbitrary"),
    )(y, target)


FFN_COLS = FFN_HIDDEN // 2


def _ffn_in_swiglu(xb, wu, wg, name, tm=512):
    t, d = xb.shape
    tm = min(tm, t)

    def body(x_ref, wu_ref, wg_ref, a_ref, u_ref, g_ref):
        x = x_ref[...]
        u = _dot(x, wu_ref[...])
        g = _dot(x, wg_ref[...])
        u_ref[...] = u
        g_ref[...] = g
        a_ref[...] = (g * _sigmoid(g) * u).astype(BF16)

    wspec = pl.BlockSpec((d, FFN_COLS), lambda j, i: (0, j))
    out = pl.BlockSpec((tm, FFN_COLS), lambda j, i: (i, j))
    return pl.pallas_call(
        body,
        name=name,
        grid=(FFN_HIDDEN // FFN_COLS, t // tm),
        in_specs=[pl.BlockSpec((tm, d), lambda j, i: (i, 0)), wspec, wspec],
        out_specs=[out, out, out],
        out_shape=[jax.ShapeDtypeStruct((t, FFN_HIDDEN), BF16), jax.ShapeDtypeStruct((t, FFN_HIDDEN), F32),
                   jax.ShapeDtypeStruct((t, FFN_HIDDEN), F32)],
        compiler_params=_params("parallel", "parallel"),
    )(xb, wu, wg)


def _ffn_out_dx_swiglu(dzb, w_ff_out, u, g, name, tm=512):
    t, d = dzb.shape
    tm = min(tm, t)

    def body(dz_ref, w_ref, u_ref, g_ref, du_ref, dg_ref):
        da = _dot(dz_ref[...], w_ref[...], "nt")
        gv = g_ref[...]
        sg = _sigmoid(gv)
        du_ref[...] = (da * gv * sg).astype(BF16)
        dg_ref[...] = (da * u_ref[...] * (sg * (1.0 + gv * (1.0 - sg)))).astype(BF16)

    blk = pl.BlockSpec((tm, FFN_COLS), lambda j, i: (i, j))
    return pl.pallas_call(
        body,
        name=name,
        grid=(FFN_HIDDEN // FFN_COLS, t // tm),
        in_specs=[pl.BlockSpec((tm, d), lambda j, i: (i, 0)), pl.BlockSpec((FFN_COLS, d), lambda j, i: (j, 0)), blk, blk],
        out_specs=[blk, blk],
        out_shape=[jax.ShapeDtypeStruct((t, FFN_HIDDEN), BF16)] * 2,
        compiler_params=_params("parallel", "parallel"),
    )(dzb, w_ff_out, u, g)


def _merge_fwd(ya, yb, wpa, wpb, rest, name, tm=512):
    t = ya.shape[0]
    tm = min(tm, t)

    def body(ya_ref, yb_ref, wa_ref, wb_ref, ga_ref, gb_ref, o_ref):
        pa = _dot(ya_ref[...], wa_ref[...])
        pb = _dot(yb_ref[...], wb_ref[...])
        o_ref[...] = (_sigmoid(ga_ref[...]) * pa + _sigmoid(gb_ref[...]) * pb).astype(BF16)

    row = lambda i: (i, 0)
    fix = lambda i: (0, 0)
    return pl.pallas_call(
        body,
        name=name,
        grid=(t // tm,),
        in_specs=[pl.BlockSpec((tm, A_WIDTH), row), pl.BlockSpec((tm, B_WIDTH), row),
                  pl.BlockSpec((A_WIDTH, D_MODEL), fix), pl.BlockSpec((B_WIDTH, D_MODEL), fix),
                  pl.BlockSpec((tm, D_MODEL), lambda i: (i, 0)), pl.BlockSpec((tm, D_MODEL), lambda i: (i, 1))],
        out_specs=pl.BlockSpec((tm, D_MODEL), row),
        out_shape=jax.ShapeDtypeStruct((t, D_MODEL), BF16),
        compiler_params=_params("parallel"),
    )(ya, yb, wpa, wpb, rest, rest)


def _merge_bwd(dzb, w_out, ya, yb, wpa, wpb, rest, name, tm=512):
    t = ya.shape[0]
    tm = min(tm, t)

    def body(dz_ref, wo_ref, ya_ref, yb_ref, wa_ref, wb_ref, ga_ref, gb_ref, dg_ref, dpa_ref, dpb_ref, dya_ref,
             dyb_ref):
        dm_v = _dot(dz_ref[...], wo_ref[...], "nt")
        pa = _dot(ya_ref[...], wa_ref[...])
        pb = _dot(yb_ref[...], wb_ref[...])
        sa = _sigmoid(ga_ref[...])
        sb = _sigmoid(gb_ref[...])
        dg_ref[:, :D_MODEL] = (dm_v * pa * sa * (1.0 - sa)).astype(BF16)
        dg_ref[:, D_MODEL:] = (dm_v * pb * sb * (1.0 - sb)).astype(BF16)
        dpa = (dm_v * sa).astype(BF16)
        dpb = (dm_v * sb).astype(BF16)
        dpa_ref[...] = dpa
        dpb_ref[...] = dpb
        dya_ref[...] = _dot(dpa, wa_ref[...], "nt").astype(BF16)
        dyb_ref[...] = _dot(dpb, wb_ref[...], "nt")

    row = lambda i: (i, 0)
    fix = lambda i: (0, 0)
    return pl.pallas_call(
        body,
        name=name,
        grid=(t // tm,),
        in_specs=[pl.BlockSpec((tm, D_MODEL), row), pl.BlockSpec((D_MODEL, D_MODEL), fix),
                  pl.BlockSpec((tm, A_WIDTH), row), pl.BlockSpec((tm, B_WIDTH), row),
                  pl.BlockSpec((A_WIDTH, D_MODEL), fix), pl.BlockSpec((B_WIDTH, D_MODEL), fix),
                  pl.BlockSpec((tm, D_MODEL), lambda i: (i, 0)), pl.BlockSpec((tm, D_MODEL), lambda i: (i, 1))],
        out_specs=[pl.BlockSpec((tm, 2 * D_MODEL), row), pl.BlockSpec((tm, D_MODEL), row),
                   pl.BlockSpec((tm, D_MODEL), row), pl.BlockSpec((tm, A_WIDTH), row), pl.BlockSpec((tm, B_WIDTH), row)],
        out_shape=[jax.ShapeDtypeStruct((t, 2 * D_MODEL), BF16), jax.ShapeDtypeStruct((t, D_MODEL), BF16),
                   jax.ShapeDtypeStruct((t, D_MODEL), BF16), jax.ShapeDtypeStruct((t, A_WIDTH), BF16),
                   jax.ShapeDtypeStruct((t, B_WIDTH), F32)],
        compiler_params=_params("parallel"),
    )(dzb, w_out, ya, yb, wpa, wpb, rest, rest)


FA_BLOCK = 4224 // 128 - 1


def _tri(n, lower):
    r = lax.broadcasted_iota(jnp.int32, (n, n), 0)
    c = lax.broadcasted_iota(jnp.int32, (n, n), 1)
    return jnp.where((r >= c) if lower else (r <= c), 1.0, 0.0).astype(F32)


def _head_spread(expand):
    shape = (128, A_WIDTH) if expand else (A_WIDTH, 128)
    r = lax.broadcasted_iota(jnp.int32, shape, 0)
    c = lax.broadcasted_iota(jnp.int32, shape, 1)
    hit = ((c >= 64 * r) & (c < 64 * r + 64)) if expand else (r == 64 * c)
    return jnp.where(hit, 1.0, 0.0).astype(F32)


def _fox_gate_fwd(rest, bf, name, tb=512):
    t = rest.shape[0]
    tb = min(tb, t)

    def body(fa_ref, bf_ref, f_ref, fc_ref, carry):
        @pl.when(pl.program_id(0) == 0)
        def _():
            carry[...] = jnp.zeros_like(carry)

        z = fa_ref[...] + bf_ref[...]
        logf = jnp.minimum(z, 0.0) - jnp.log(1.0 + jnp.exp(-jnp.abs(z)))
        f = _dot_hi(_tri(tb, True), logf) + carry[...]
        f_ref[...] = f
        fc_ref[...] = _dot_hi(f, _head_spread(True), exact="b")
        carry[...] = f[tb - 1:tb, :]

    return pl.pallas_call(
        body,
        name=name,
        grid=(t // tb,),
        in_specs=[pl.BlockSpec((tb, 128), lambda i: (i, FA_BLOCK)), pl.BlockSpec((1, 128), lambda i: (0, 0))],
        out_specs=[pl.BlockSpec((tb, 128), lambda i: (i, 0)), pl.BlockSpec((tb, A_WIDTH), lambda i: (i, 0))],
        out_shape=[jax.ShapeDtypeStruct((t, 128), F32), jax.ShapeDtypeStruct((t, A_WIDTH), F32)],
        scratch_shapes=[pltpu.VMEM((1, 128), F32)],
        compiler_params=_params("arbitrary"),
    )(rest, bf)


def _fox_gate_bwd(rsum, csum, rest, bf, name, tb=512):
    t = rest.shape[0]
    tb = min(tb, t)
    nb = t // tb

    def body(rs_ref, cs_ref, fa_ref, bf_ref, dfa_ref, dbf_ref, carry):
        @pl.when(pl.program_id(0) == 0)
        def _():
            carry[...] = jnp.zeros_like(carry)

        d_f = _dot_hi(rs_ref[...] - cs_ref[...], _head_spread(False), exact="b")
        dlogf = _dot_hi(_tri(tb, False), d_f) + carry[...]
        carry[...] = dlogf[0:1, :]
        z = fa_ref[...] + bf_ref[...]
        dz = dlogf * _sigmoid(-z)
        dfa_ref[...] = dz.astype(BF16)
        part = jnp.sum(dz, axis=0, keepdims=True)

        @pl.when(pl.program_id(0) == 0)
        def _():
            dbf_ref[...] = part

        @pl.when(pl.program_id(0) > 0)
        def _():
            dbf_ref[...] += part

    return pl.pallas_call(
        body,
        name=name,
        grid=(nb,),
        in_specs=[pl.BlockSpec((tb, A_WIDTH), lambda i: (nb - 1 - i, 0)),
                  pl.BlockSpec((tb, A_WIDTH), lambda i: (nb - 1 - i, 0)),
                  pl.BlockSpec((tb, 128), lambda i: (nb - 1 - i, FA_BLOCK)),
                  pl.BlockSpec((1, 128), lambda i: (0, 0))],
        out_specs=[pl.BlockSpec((tb, 128), lambda i: (nb - 1 - i, 0)), pl.BlockSpec((1, 128), lambda i: (0, 0))],
        out_shape=[jax.ShapeDtypeStruct((t, 128), BF16), jax.ShapeDtypeStruct((1, 128), F32)],
        scratch_shapes=[pltpu.VMEM((1, 128), F32)],
        compiler_params=_params("arbitrary"),
    )(rsum, csum, rest, bf)


ATT_BLOCK = 512


def _head_mask(shape, j):
    lane = lax.broadcasted_iota(jnp.int32, shape, 1)
    return (lane < 64) if j == 0 else (lane >= 64)


def _aug_lanes(tb, j):
    lane = lax.broadcasted_iota(jnp.int32, (tb, 128), 1)
    own = (lane < 64) if j == 0 else (lane >= 64)
    return own, lane - 64 * (1 - j)


def _aug_query(own, li, q, pieces):
    h, m, l = pieces
    one, zero = jnp.ones_like(h), jnp.zeros_like(h)
    spare = jnp.where(li == 0, h, jnp.where(li == 1, m, jnp.where(li == 2, l, jnp.where(li < 6, one, zero))))
    return jnp.where(own, q, spare)


def _fox_prep_fwd(qkv, fcol, name, tb=2048):
    t = qkv.shape[0]
    tb = min(tb, t)

    def body(q_ref, k_ref, v_ref, fc_ref, qa_ref, ka_ref, va_ref, qn_ref, kn_ref):
        pieces = _pieces(pltpu.roll(fc_ref[...], 64, 1))
        h, m, l = pieces
        q, k, v = q_ref[...], k_ref[...], v_ref[...]
        first = _head_mask((tb, 128), 0)
        for nrm_ref, x in ((qn_ref, q.astype(F32)), (kn_ref, k.astype(F32))):
            n0 = jnp.max(jnp.sum(jnp.where(first, x * x, 0.0), axis=1, keepdims=True))
            n1 = jnp.max(jnp.sum(jnp.where(first, 0.0, x * x), axis=1, keepdims=True))
            nrm_ref[...] = jnp.where(_head_mask((8, 128), 0), n0, n1)
        one, zero = jnp.ones_like(h), jnp.zeros_like(h)
        for j in (0, 1):
            own, li = _aug_lanes(tb, j)
            cols = slice(128 * j, 128 * (j + 1))
            qa_ref[:, cols] = _aug_query(own, li, q * 0.125, pieces)
            ks = jnp.where(li < 3, one, jnp.where(li == 3, -h, jnp.where(li == 4, -m, jnp.where(li == 5, -l, zero))))
            ka_ref[:, cols] = jnp.where(own, k, ks)
            va_ref[:, cols] = jnp.where(own, v, one)

    blk = pl.BlockSpec((tb, 256), lambda i, h: (i, h))
    nrm = pl.BlockSpec((None, None, 8, 128), lambda i, h: (i, h, 0, 0))
    return pl.pallas_call(
        body, name=name, grid=(t // tb, 4),
        in_specs=[pl.BlockSpec((tb, 128), lambda i, h: (i, h)), pl.BlockSpec((tb, 128), lambda i, h: (i, 4 + h)),
                  pl.BlockSpec((tb, 128), lambda i, h: (i, 8 + h)), pl.BlockSpec((tb, 128), lambda i, h: (i, h))],
        out_specs=[blk, blk, blk, nrm, nrm],
        out_shape=[jax.ShapeDtypeStruct((t, 2 * A_WIDTH), BF16)] * 3
        + [jax.ShapeDtypeStruct((t // tb, 4, 8, 128), F32)] * 2,
        compiler_params=_params("parallel", "parallel"),
    )(qkv, qkv, qkv, fcol)


def _fox_prep_bwd(qkv, fcol, lse, do, o, name, tb=2048):
    t = qkv.shape[0]
    tb = min(tb, t)

    def body(q_ref, fc_ref, lse_ref, do_ref, o_ref, qb_ref, dob_ref):
        pieces = _pieces(pltpu.roll(fc_ref[...] - lse_ref[...], 64, 1))
        q = q_ref[...] * 0.125
        do_v = do_ref[...]
        prod = do_v.astype(F32) * o_ref[...].astype(F32)
        for j in (0, 1):
            own, li = _aug_lanes(tb, j)
            cols = slice(128 * j, 128 * (j + 1))
            qb_ref[:, cols] = _aug_query(own, li, q, pieces)
            delta = jnp.sum(jnp.where(own, prod, 0.0), axis=1, keepdims=True)
            h, m, l = _pieces(jnp.broadcast_to(delta, (tb, 128)))
            ds = jnp.where(li == 0, -h, jnp.where(li == 1, -m, jnp.where(li == 2, -l, jnp.zeros_like(h))))
            dob_ref[:, cols] = jnp.where(own, do_v, ds)

    pair = pl.BlockSpec((tb, 128), lambda i, h: (i, h))
    blk = pl.BlockSpec((tb, 256), lambda i, h: (i, h))
    return pl.pallas_call(
        body, name=name, grid=(t // tb, 4),
        in_specs=[pair, pair, pair, pair, pair],
        out_specs=[blk, blk],
        out_shape=[jax.ShapeDtypeStruct((t, 2 * A_WIDTH), BF16)] * 2,
        compiler_params=_params("parallel", "parallel"),
    )(qkv, fcol, lse, do, o)


def _tile_mask(n, transposed):
    r = lax.broadcasted_iota(jnp.int32, (n, n), 0)
    c = lax.broadcasted_iota(jnp.int32, (n, n), 1)
    return (c >= r) if transposed else (r >= c)


UNDERFLOW = -110.0


def _fox_block_ranges(qn, kn, fcum):
    t = fcum.shape[0]
    blk = min(ATT_BLOCK, t)
    nb = t // blk
    q2 = jnp.max(qn[:, :, 0, ::64].reshape(-1, A_HEADS), axis=0)
    k2 = jnp.max(kn[:, :, 0, ::64].reshape(-1, A_HEADS), axis=0)
    bound = 2.0 * jnp.sqrt(q2 * k2) * 0.125
    f = fcum[:, :A_HEADS]
    first = f[0::blk].T
    last = f[blk - 1::blk].T
    dead = (bound[:, None, None] + first[:, :, None] - last[:, None, :]) < UNDERFLOW
    qi = jnp.arange(nb)[None, :, None]
    kj = jnp.arange(nb)[None, None, :]
    dead = dead & (kj < qi)
    kstart = jnp.sum(dead, axis=2).astype(jnp.int32)
    qend = (kj[0] + jnp.sum((~dead) & (qi > kj), axis=1)).astype(jnp.int32)
    return kstart.reshape(-1), qend.reshape(-1)


def _fox_fwd(qa, ka, va, kstart, name):
    t = qa.shape[0]
    bq = min(ATT_BLOCK, t)
    nq = t // bq

    def body(ks_ref, q_ref, k_ref, v_ref, o_ref, lse_ref):
        i = pl.program_id(1)
        hp = pl.program_id(0)
        k0 = [ks_ref[(2 * hp + j) * nq + i] for j in (0, 1)]
        both0 = jnp.maximum(k0[0], k0[1])

        def head(j, kb, m, acc, masked):
            rows = pl.ds(pl.multiple_of(kb * bq, bq), bq)
            cols = slice(128 * j, 128 * (j + 1))
            s = _dot(q_ref[:, cols], k_ref[rows, cols], "nt")
            if masked:
                s = jnp.where(_tile_mask(bq, False), s, -jnp.inf)
            m_new = jnp.maximum(m, jnp.max(s, axis=1, keepdims=True))
            return m_new, jnp.exp(m - m_new) * acc + _dot(jnp.exp(s - m_new), v_ref[rows, cols])

        def pair(kb, carry, masked):
            return head(0, kb, carry[0], carry[1], masked) + head(1, kb, carry[2], carry[3], masked)

        init = (jnp.full((bq, 1), -jnp.inf, F32), jnp.zeros((bq, 128), F32))
        alone = [lax.fori_loop(k0[j], both0, lambda kb, c, j=j: head(j, kb, c[0], c[1], False), init) for j in (0, 1)]
        carry = lax.fori_loop(both0, i, lambda kb, c: pair(kb, c, False), alone[0] + alone[1])
        carry = pair(i, carry, True)
        outs = []
        for j in (0, 1):
            m, acc = carry[2 * j], carry[2 * j + 1]
            spare = 64 * (1 - j)
            l = acc[:, spare:spare + 1]
            outs.append((acc / l, m + jnp.log(l)))
        msk = _head_mask((bq, 128), 0)
        o_ref[...] = jnp.where(msk, outs[0][0], outs[1][0]).astype(BF16)
        lse_ref[...] = jnp.where(msk, outs[0][1], outs[1][1])

    res = pl.BlockSpec((t, 256), lambda h, i, tbl: (0, h))
    out = pl.BlockSpec((bq, 128), lambda h, i, tbl: (i, h))
    return pl.pallas_call(
        body,
        name=name,
        grid_spec=pltpu.PrefetchScalarGridSpec(
            num_scalar_prefetch=1, grid=(4, nq),
            in_specs=[pl.BlockSpec((bq, 256), lambda h, i, tbl: (i, h)), res, res],
            out_specs=[out, out]),
        out_shape=[jax.ShapeDtypeStruct((t, A_WIDTH), BF16), jax.ShapeDtypeStruct((t, A_WIDTH), F32)],
        compiler_params=_params("parallel", "parallel"),
    )(kstart, qa, ka, va)


def _fox_bwd(qb, ka, va, dob, qend, name):
    t = qb.shape[0]
    bk = min(ATT_BLOCK, t)
    nk = t // bk

    def body(qe_ref, k_ref, v_ref, q_hbm, do_hbm, dk_ref, dv_ref, cs_ref, dq_hbm, rs_hbm, q_scr, do_scr, dq_scr,
             sems):
        jb = pl.program_id(1)
        hp = pl.program_id(0)
        pair_cols = pl.ds(pl.multiple_of(hp * 256, 256), 256)

        @pl.when(jb == 0)
        def _():
            loads = [pltpu.make_async_copy(q_hbm.at[:, pair_cols], q_scr, sems.at[0]),
                     pltpu.make_async_copy(do_hbm.at[:, pair_cols], do_scr, sems.at[1])]
            for cp in loads:
                cp.start()
            dq_scr[...] = jnp.zeros_like(dq_scr)
            for cp in loads:
                cp.wait()

        i1 = [qe_ref[(2 * hp + j) * nk + jb] + 1 for j in (0, 1)]
        both1 = jnp.minimum(i1[0], i1[1])

        def head(j, ib, dk_acc, dv_acc, masked):
            rows = pl.ds(pl.multiple_of(ib * bk, bk), bk)
            cols = slice(128 * j, 128 * (j + 1))
            qs = q_scr[rows, cols]
            dos = do_scr[rows, cols]
            kj = k_ref[:, cols]
            st = _dot(kj, qs, "nt")
            if masked:
                st = jnp.where(_tile_mask(bk, True), st, -jnp.inf)
            pt = jnp.exp(st)
            dst = (pt * _dot(v_ref[:, cols], dos, "nt")).astype(BF16)
            dq_scr[rows, cols] += _dot(dst, kj, "tn")
            return dk_acc + _dot(dst, qs), dv_acc + _dot(pt, dos)

        def pair(ib, carry, masked):
            return head(0, ib, carry[0], carry[1], masked) + head(1, ib, carry[2], carry[3], masked)

        carry = pair(jb, (jnp.zeros((bk, 128), F32),) * 4, True)
        carry = lax.fori_loop(jb + 1, both1, lambda ib, c: pair(ib, c, False), carry)
        alone = [lax.fori_loop(jnp.maximum(both1, jb + 1), i1[j],
                               lambda ib, c, j=j: head(j, ib, c[0], c[1], False), carry[2 * j:2 * j + 2])
                 for j in (0, 1)]
        carry = alone[0] + alone[1]
        outs = []
        for j in (0, 1):
            spare = 64 * (1 - j)
            dk_acc, dv_acc = carry[2 * j], carry[2 * j + 1]
            outs.append((dk_acc, dv_acc, dk_acc[:, spare + 3:spare + 4]))
        msk = _head_mask((bk, 128), 0)
        dk_ref[...] = jnp.where(msk, outs[0][0], outs[1][0]).astype(BF16)
        dv_ref[...] = jnp.where(msk, outs[0][1], outs[1][1]).astype(BF16)
        cs_ref[...] = jnp.where(msk, outs[0][2], outs[1][2])

        @pl.when(jb == nk - 1)
        def _():
            def finish(r, carry):
                rows = pl.ds(pl.multiple_of(r * bk, bk), bk)
                x0, x1 = dq_scr[rows, 0:128], dq_scr[rows, 128:256]
                q_scr[rows, 0:128] = (jnp.where(msk, x0, x1) * 0.125).astype(BF16)
                dq_scr[rows, 0:128] = jnp.where(msk, x0[:, 64:65], x1[:, 0:1])
                return carry

            lax.fori_loop(0, nk, finish, 0)
            head_cols = pl.ds(pl.multiple_of(hp * 128, 128), 128)
            stores = [pltpu.make_async_copy(q_scr.at[:, 0:128], dq_hbm.at[:, head_cols], sems.at[0]),
                      pltpu.make_async_copy(dq_scr.at[:, 0:128], rs_hbm.at[:, head_cols], sems.at[1])]
            for cp in stores:
                cp.start()
            for cp in stores:
                cp.wait()

    blk = pl.BlockSpec((bk, 256), lambda h, i, tbl: (i, h))
    out = pl.BlockSpec((bk, 128), lambda h, i, tbl: (i, h))
    return pl.pallas_call(
        body,
        name=name,
        grid_spec=pltpu.PrefetchScalarGridSpec(
            num_scalar_prefetch=1, grid=(4, nk), in_specs=[blk, blk, _ANY, _ANY],
            out_specs=[out, out, out, _ANY, _ANY],
            scratch_shapes=[pltpu.VMEM((t, 256), BF16), pltpu.VMEM((t, 256), BF16), pltpu.VMEM((t, 256), F32),
                            pltpu.SemaphoreType.DMA((2,))]),
        out_shape=[jax.ShapeDtypeStruct((t, A_WIDTH), BF16), jax.ShapeDtypeStruct((t, A_WIDTH), BF16),
                   jax.ShapeDtypeStruct((t, A_WIDTH), F32), jax.ShapeDtypeStruct((t, A_WIDTH), BF16),
                   jax.ShapeDtypeStruct((t, A_WIDTH), F32)],
        compiler_params=_params("arbitrary", "arbitrary"),
    )(qend, ka, va, qb, dob)


HG_ROWS = 256


def _hg_gates(hb_ref, rows, lbv):
    qb = hb_ref[rows, 0:B_WIDTH]
    fb = hb_ref[rows, B_WIDTH:2 * B_WIDTH]
    v = hb_ref[rows, 2 * B_WIDTH:3 * B_WIDTH]
    gb = hb_ref[rows, 3 * B_WIDTH:4 * B_WIDTH]
    sg = _sigmoid(fb)
    f = lbv + (1.0 - lbv) * sg
    sq = _sigmoid(qb)
    return qb, sq, qb * sq, sg, f, 1.0 - f, jnp.log(f), v, gb


def _hg_intra_factors(q, k, b):
    fac = []
    for i in range(CHUNK // SUB):
        bi = b[SUB * i:SUB * i + 1, :]
        eq = jnp.exp(b[SUB * i:SUB * (i + 1), :] - bi)
        ek = jnp.exp(jnp.minimum(bi - b, EXP_CLAMP))
        fac.append((eq, ek, q[SUB * i:SUB * (i + 1), :] * eq, k * ek))
    return fac


def _causal(n):
    r = lax.broadcasted_iota(jnp.int32, (n, n), 0)
    c = lax.broadcasted_iota(jnp.int32, (n, n), 1)
    return r >= c


def _hgrn_fwd(rest, lb, ng, name):
    t = rest.shape[0]
    bt = min(HG_ROWS, t)
    ncb = bt // CHUNK

    def body(hb_ref, lb_ref, ng_ref, y_ref, o_ref, st_ref, s_scr):
        @pl.when(pl.program_id(0) == 0)
        def _():
            s_scr[...] = jnp.zeros_like(s_scr)

        tril = _tri(CHUNK, True)
        causal = _causal(CHUNK)
        ones = jnp.ones((CHUNK, HD), F32)

        def chunk(c, carry):
            rows = pl.ds(pl.multiple_of(c * CHUNK, CHUNK), CHUNK)
            _, _, q_all, _, _, k_all, g_all, v_all, gb_all = _hg_gates(hb_ref, rows, lb_ref[...])
            b_all = _dot_hi(tril, g_all)
            qd_all = q_all * jnp.exp(b_all)
            kd_all = k_all * jnp.exp(b_all[CHUNK - 1:CHUNK, :] - b_all)
            eb_all = jnp.exp(_dot_hi(g_all, ones, "tn", exact="b"))
            sgb_all = _sigmoid(gb_all)
            for h in range(B_HEADS):
                cols = slice(h * HD, (h + 1) * HD)
                v = v_all[:, cols]
                s0 = s_scr[h]
                st_ref[c, h] = s0
                o = _dot(qd_all[:, cols], s0)
                fac = _hg_intra_factors(q_all[:, cols], k_all[:, cols], b_all[:, cols])
                a = jnp.concatenate([_dot(qe, ke, "nt") for _, _, qe, ke in fac], axis=0)
                o = o + _dot(jnp.where(causal, a, 0.0), v)
                s_scr[h] = eb_all[h * HD:(h + 1) * HD, :] * s0 + _dot(kd_all[:, cols], v, "tn")
                r = lax.rsqrt(jnp.mean(o * o, axis=-1, keepdims=True) + RMS_EPS)
                o_ref[rows, cols] = o
                y_ref[rows, cols] = (o * r * ng_ref[...] * sgb_all[:, cols]).astype(BF16)
            return carry

        lax.fori_loop(0, ncb, chunk, 0)

    return pl.pallas_call(
        body,
        name=name,
        grid=(t // bt,),
        in_specs=[pl.BlockSpec((bt, 4 * B_WIDTH), lambda i: (i, 1)), pl.BlockSpec((1, B_WIDTH), lambda i: (0, 0)),
                  pl.BlockSpec((1, HD), lambda i: (0, 0))],
        out_specs=[pl.BlockSpec((bt, B_WIDTH), lambda i: (i, 0)), pl.BlockSpec((bt, B_WIDTH), lambda i: (i, 0)),
                   pl.BlockSpec((ncb, B_HEADS, HD, HD), lambda i: (i, 0, 0, 0))],
        out_shape=[jax.ShapeDtypeStruct((t, B_WIDTH), BF16), jax.ShapeDtypeStruct((t, B_WIDTH), F32),
                   jax.ShapeDtypeStruct((t // CHUNK, B_HEADS, HD, HD), F32)],
        scratch_shapes=[pltpu.VMEM((B_HEADS, HD, HD), F32)],
        compiler_params=_params("arbitrary"),
    )(rest, lb, ng)


def _hgrn_bwd(dy, rest, o_saved, states, lb, ng, name):
    t = rest.shape[0]
    bt = min(HG_ROWS, t)
    ncb = bt // CHUNK
    nb = t // bt

    def body(dy_ref, hb_ref, o_ref, st_ref, lb_ref, ng_ref, dh_ref, dlb_ref, dng_ref, ds_scr):
        @pl.when(pl.program_id(0) == 0)
        def _():
            ds_scr[...] = jnp.zeros_like(ds_scr)
            dlb_ref[...] = jnp.zeros_like(dlb_ref)
            dng_ref[...] = jnp.zeros_like(dng_ref)

        tril = _tri(CHUNK, True)
        triu = _tri(CHUNK, False)
        causal = _causal(CHUNK)
        ones = jnp.ones((CHUNK, HD), F32)
        ones8 = jnp.ones((8, HD), F32)
        last_row = lax.broadcasted_iota(jnp.int32, (CHUNK, B_WIDTH), 0) == CHUNK - 1

        def chunk(cc, carry):
            dng_acc, dlb_acc = carry
            c = ncb - 1 - cc
            rows = pl.ds(pl.multiple_of(c * CHUNK, CHUNK), CHUNK)
            lbv = lb_ref[...]
            qb, sq, q_all, sg, f, k_all, g_all, v_all, gb = _hg_gates(hb_ref, rows, lbv)
            b_all = _dot_hi(tril, g_all)
            ebt_all = jnp.exp(b_all)
            blast = b_all[CHUNK - 1:CHUNK, :]
            ekd_all = jnp.exp(blast - b_all)
            eb_all = jnp.exp(_dot_hi(g_all, ones, "tn", exact="b"))
            sgb = _sigmoid(gb)
            dy_all = dy_ref[rows, :].astype(F32)
            don_all = dy_all * sgb
            ngv = ng_ref[...]
            dq_l, dk_l, dks_l, dv_l, on_l, prod_l = [], [], [], [], [], []
            for h in range(B_HEADS):
                cols = slice(h * HD, (h + 1) * HD)
                q, k, v = q_all[:, cols], k_all[:, cols], v_all[:, cols]
                o = o_ref[rows, cols]
                don = don_all[:, cols]
                r = lax.rsqrt(jnp.mean(o * o, axis=-1, keepdims=True) + RMS_EPS)
                on_l.append(o * r * ngv)
                dng_acc = dng_acc + jnp.sum(don * o * r, axis=0, keepdims=True)
                doh = don * ngv
                do = r * (doh - o * (r * r) * jnp.mean(doh * o, axis=-1, keepdims=True))
                ebt, ekd = ebt_all[:, cols], ekd_all[:, cols]
                s0 = st_ref[c, h]
                ds1 = ds_scr[h]
                fac = _hg_intra_factors(q, k, b_all[:, cols])
                a = jnp.concatenate([_dot(qe, ke, "nt") for _, _, qe, ke in fac], axis=0)
                a = jnp.where(causal, a, 0.0)
                da = jnp.where(causal, _dot(do, v, "nt"), 0.0)
                dv_l.append(_dot(a, do, "tn") + _dot(k * ekd, ds1))
                dq = ebt * _dot(do, s0, "nt")
                dq_l.append(dq + jnp.concatenate(
                    [eq * _hdot(da[SUB * i:SUB * (i + 1), :], ke) for i, (eq, _, _, ke) in enumerate(fac)], axis=0))
                dk_state = ekd * _dot(v, ds1, "nt")
                dk = dk_state
                for i, (_, ek, qe, _) in enumerate(fac):
                    dk = dk + ek * _hdot(da[SUB * i:SUB * (i + 1), :], qe, "tn")
                dk_l.append(dk)
                dks_l.append(dk_state)
                prod_l.append(ds1 * s0)
                ds_scr[h] = _dot(q * ebt, do, "tn") + eb_all[h * HD:(h + 1) * HD, :] * ds1
            dq_all, dk_all = jnp.concatenate(dq_l, axis=1), jnp.concatenate(dk_l, axis=1)
            extra = jnp.exp(blast) * _dot_hi(ones8, jnp.concatenate(prod_l, axis=0), "nt")[0:1, :] \
                + jnp.sum(k_all * jnp.concatenate(dks_l, axis=1), axis=0, keepdims=True)
            db = q_all * dq_all - k_all * dk_all + jnp.where(last_row, extra, 0.0)
            df = _dot_hi(triu, db) / f - dk_all
            dlb_acc = dlb_acc + jnp.sum(df * (1.0 - sg), axis=0, keepdims=True)
            dh_ref[rows, 0:B_WIDTH] = (dq_all * (sq * (1.0 + qb * (1.0 - sq)))).astype(BF16)
            dh_ref[rows, B_WIDTH:2 * B_WIDTH] = (df * (1.0 - lbv) * sg * (1.0 - sg)).astype(BF16)
            dh_ref[rows, 2 * B_WIDTH:3 * B_WIDTH] = jnp.concatenate(dv_l, axis=1).astype(BF16)
            dh_ref[rows, 3 * B_WIDTH:4 * B_WIDTH] = (dy_all * jnp.concatenate(on_l, axis=1)
                                                     * sgb * (1.0 - sgb)).astype(BF16)
            return dng_acc, dlb_acc

        dng_sum, dlb_sum = lax.fori_loop(0, ncb, chunk, (jnp.zeros((1, HD), F32), jnp.zeros((1, B_WIDTH), F32)))
        dng_ref[...] += dng_sum
        dlb_ref[...] += dlb_sum

    rev = lambda i: (nb - 1 - i, 0)
    return pl.pallas_call(
        body,
        name=name,
        grid=(nb,),
        in_specs=[pl.BlockSpec((bt, B_WIDTH), rev), pl.BlockSpec((bt, 4 * B_WIDTH), lambda i: (nb - 1 - i, 1)),
                  pl.BlockSpec((bt, B_WIDTH), rev),
                  pl.BlockSpec((ncb, B_HEADS, HD, HD), lambda i: (nb - 1 - i, 0, 0, 0)),
                  pl.BlockSpec((1, B_WIDTH), lambda i: (0, 0)), pl.BlockSpec((1, HD), lambda i: (0, 0))],
        out_specs=[pl.BlockSpec((bt, 4 * B_WIDTH), rev), pl.BlockSpec((1, B_WIDTH), lambda i: (0, 0)),
                   pl.BlockSpec((1, HD), lambda i: (0, 0))],
        out_shape=[jax.ShapeDtypeStruct((t, 4 * B_WIDTH), BF16), jax.ShapeDtypeStruct((1, B_WIDTH), F32),
                   jax.ShapeDtypeStruct((1, HD), F32)],
        scratch_shapes=[pltpu.VMEM((B_HEADS, HD, HD), F32)],
        compiler_params=_params("arbitrary"),
    )(dy, rest, o_saved, states, lb, ng)


def _axpy2(c0, a0, c1, a1, name, tm=512):
    t, d = a0.shape
    tm = min(tm, t)

    def body(a_ref, b_ref, o_ref):
        o_ref[...] = c0 * a_ref[...] + c1 * b_ref[...]

    row = lambda i: (i, 0)
    return pl.pallas_call(
        body, name=name, grid=(t // tm,),
        in_specs=[pl.BlockSpec((tm, d), row), pl.BlockSpec((tm, d), row)],
        out_specs=pl.BlockSpec((tm, d), row),
        out_shape=jax.ShapeDtypeStruct((t, d), F32),
        compiler_params=_params("parallel"),
    )(a0, a1)


def _split_w_in(w_in_l):
    wqkv = w_in_l[:, :3 * A_WIDTH]
    wfa = jnp.pad(w_in_l[:, 3 * A_WIDTH:3 * A_WIDTH + A_HEADS], ((0, 0), (0, 128 - A_HEADS)))
    whb = w_in_l[:, 3 * A_WIDTH + A_HEADS:3 * A_WIDTH + A_HEADS + 4 * B_WIDTH]
    wgt = w_in_l[:, 3 * A_WIDTH + A_HEADS + 4 * B_WIDTH:]
    return wqkv, jnp.concatenate([wgt, whb, wfa], axis=1)


def _merge_w_in_grad(dwall):
    o = 3 * A_WIDTH
    return jnp.concatenate([dwall[:, :o], dwall[:, o + 4096:o + 4096 + A_HEADS], dwall[:, o + 2048:o + 4096],
                            dwall[:, o:o + 2048]], axis=1)


def _layer_fwd(x, xb, w, sp, l):
    t = x.shape[0]
    n = f"l{l}_"
    wqkv, wrest = _split_w_in(w["w_in"])
    qkv = _matmul(xb, wqkv, "nn", BF16, MM_ROWS, 768, D_MODEL, n + "proj_qkv")
    rest = _matmul(xb, wrest, "nn", F32, MM_ROWS, 1408, D_MODEL, n + "proj_rest")
    bf = jnp.pad(sp["b_fgate"], (0, 128 - A_HEADS)).reshape(1, 128)
    fcum, fcol = _fox_gate_fwd(rest, bf, n + "fox_gate_fwd")
    qa, ka, va, qn, kn = _fox_prep_fwd(qkv, fcol, n + "fox_prep_fwd")
    kstart, qend = _fox_block_ranges(qn, kn, fcum)
    ya, lse = _fox_fwd(qa, ka, va, kstart, n + "fox_fwd")
    lb = sp["lb"].reshape(1, B_WIDTH)
    ng = sp["norm_g"].reshape(1, HD)
    yb, ob, states = _hgrn_fwd(rest, lb, ng, n + "hgrn_fwd")
    merged = _merge_fwd(ya, yb, w["w_pa"], w["w_pb"], rest, n + "merge_fwd")
    x1, x1b, xh1, rs1 = _mm_res_ln(merged, w["w_out"], x, sp["ln1_g"], sp["ln1_b"], n + "out_ln1")
    wu, wg = w["w_ff_in"][:, :FFN_HIDDEN], w["w_ff_in"][:, FFN_HIDDEN:]
    a, hu, hg = _ffn_in_swiglu(x1b, wu, wg, n + "ffn_in_swiglu")
    x2, x2b, xh2, rs2 = _mm_res_ln(a, w["w_ff_out"], x1, sp["ln2_g"], sp["ln2_b"], n + "ffn_out_ln2")
    saved = dict(xb=xb, wqkv=wqkv, wrest=wrest, qkv=qkv, rest=rest, bf=bf, fcol=fcol, ka=ka, va=va, ya=ya, lse=lse,
                 kstart=kstart, qend=qend,
                 lb=lb, ng=ng, yb=yb, ob=ob, states=states, merged=merged, x1b=x1b, xh1=xh1, rs1=rs1, a=a,
                 wu=wu, wg=wg, hu=hu, hg=hg,
                 xh2=xh2, rs2=rs2)
    return x2, x2b, saved


def _layer_bwd(dys, coefs, w, sp, s, l):
    n = f"l{l}_"
    dz2, dz2b, dg2, db2 = _ln_bwd(dys, coefs, s["xh2"], s["rs2"], sp["ln2_g"], n + "ln2_bwd")
    du, dg = _ffn_out_dx_swiglu(dz2b, w["w_ff_out"], s["hu"], s["hg"], n + "ffn_out_dx_swiglu")
    d_wffout = _matmul(s["a"], dz2b, "tn", F32, 1408, 1024, DW_ROWS, n + "ffn_out_dw")
    dx1u = _matmul(du, s["wu"], "nt", F32, MM_ROWS, 1024, FFN_HIDDEN, n + "ffn_in_dx_u")
    dx1g = _matmul(dg, s["wg"], "nt", F32, MM_ROWS, 1024, FFN_HIDDEN, n + "ffn_in_dx_g")
    d_wffin = jnp.concatenate([_matmul(s["x1b"], du, "tn", F32, 1024, 1408, DW_ROWS, n + "ffn_in_dw_u"),
                               _matmul(s["x1b"], dg, "tn", F32, 1024, 1408, DW_ROWS, n + "ffn_in_dw_g")], axis=1)
    dz1, dz1b, dg1, db1 = _ln_bwd([dz2, dx1u, dx1g], [ALPHA, 1.0, 1.0], s["xh1"], s["rs1"], sp["ln1_g"],
                                  n + "ln1_bwd")
    d_wout = _matmul(s["merged"], dz1b, "tn", F32, 1024, 1024, DW_ROWS, n + "out_dw")
    dgates, dpa, dpb, dya, dyb = _merge_bwd(dz1b, w["w_out"], s["ya"], s["yb"], w["w_pa"], w["w_pb"], s["rest"],
                                  n + "merge_bwd")
    d_wpa = _matmul(s["ya"], dpa, "tn", F32, 512, 1024, DW_ROWS, n + "pa_dw")
    d_wpb = _matmul(s["yb"], dpb, "tn", F32, 512, 1024, DW_ROWS, n + "pb_dw")
    qb, dob = _fox_prep_bwd(s["qkv"], s["fcol"], s["lse"], dya, s["ya"], n + "fox_prep_bwd")
    dk, dv, csum, dq, rsum = _fox_bwd(qb, s["ka"], s["va"], dob, s["qend"], n + "fox_bwd")
    dfa, dbf = _fox_gate_bwd(rsum, csum, s["rest"], s["bf"], n + "fox_gate_bwd")
    dhb, dlb, dng = _hgrn_bwd(dyb, s["rest"], s["ob"], s["states"], s["lb"], s["ng"], n + "hgrn_bwd")
    dproj = jnp.concatenate([dq, dk, dv, dgates, dhb, dfa], axis=1)
    wall = jnp.concatenate([s["wqkv"], s["wrest"]], axis=1)
    dxp = _matmul(dproj, wall, "nt", F32, MM_ROWS, 1024, 1920, n + "proj_dx")
    d_wall = _matmul(s["xb"], dproj, "tn", F32, 1024, 1152, DW_ROWS, n + "proj_dw")
    grads = dict(w_in=_merge_w_in_grad(d_wall), w_pa=d_wpa, w_pb=d_wpb, w_out=d_wout, w_ff_in=d_wffin,
                 w_ff_out=d_wffout, b_fgate=dbf[0, :A_HEADS], lb=dlb[0], norm_g=dng[0], ln1_g=dg1[0], ln1_b=db1[0],
                 ln2_g=dg2[0], ln2_b=db2[0])
    return [dz1, dxp], [ALPHA, 1.0], grads


def _lower_bounds(logits):
    sm = jax.nn.softmax(logits.astype(F32), axis=0)
    return jnp.cumsum(sm, axis=0) - sm[0:1]


def _local_step(x, target, wfull, small):
    lbs, lb_vjp = jax.vjp(_lower_bounds, small["hgrn_lb_logits"])
    h, hb = x, x.astype(BF16)
    saved, sps = [], []
    for l in range(DEPTH):
        sp = dict(b_fgate=small["b_fgate"][l], lb=lbs[l], norm_g=small["hgrn_norm_g"][l], ln1_g=small["ln1_g"][l],
                  ln1_b=small["ln1_b"][l], ln2_g=small["ln2_g"][l], ln2_b=small["ln2_b"][l])
        h, hb, s = _layer_fwd(h, hb, wfull[l], sp, l)
        saved.append(s)
        sps.append(sp)
    dy, lpart = _loss_head(h, target)
    dys, coefs = [dy], [1.0]
    grads = [None] * DEPTH
    for l in reversed(range(DEPTH)):
        dys, coefs, grads[l] = _layer_bwd(dys, coefs, wfull[l], sps[l], saved[l], l)
    grad_x = _axpy2(coefs[0], dys[0], coefs[1], dys[1], "grad_x")
    d_logits = lb_vjp(jnp.stack([grads[l]["lb"] for l in range(DEPTH)]))[0]
    return lpart[0, 0], grad_x, grads, d_logits


_BIG = [("w_in", "w_in", (D_MODEL, IN_TOTAL), 1), ("w_branch_a", "w_pa", (A_WIDTH, D_MODEL), 1),
        ("w_branch_b", "w_pb", (B_WIDTH, D_MODEL), 1), ("w_out", "w_out", (D_MODEL, D_MODEL), 0),
        ("w_ff_in", "w_ff_in", (D_MODEL, 2 * FFN_HIDDEN), 1), ("w_ff_out", "w_ff_out", (FFN_HIDDEN, D_MODEL), 0)]
_SMALL = [("b_fgate", A_HEADS), ("hgrn_lb_logits", B_WIDTH), ("hgrn_norm_g", HD), ("ln1_g", D_MODEL),
          ("ln1_b", D_MODEL), ("ln2_g", D_MODEL), ("ln2_b", D_MODEL)]
N_BIG = len(_BIG)
SMALL_ROWS = 80


def _by_chip(full, axis):
    if axis == 0:
        return full.reshape(N_CHIPS, full.shape[0] // N_CHIPS, full.shape[1])
    n = full.shape[1] // N_CHIPS
    return jnp.stack([full[:, q * n:(q + 1) * n] for q in range(N_CHIPS)])


def _from_chips(shards, axis):
    if axis == 0:
        return shards.reshape(N_CHIPS * shards.shape[1], shards.shape[2])
    return jnp.concatenate([shards[q] for q in range(N_CHIPS)], axis=1)


def _pack_small(per_name):
    flat = jnp.concatenate([per_name[name].reshape(-1) for name, _ in _SMALL])
    return jnp.pad(flat, (0, SMALL_ROWS * 128 - flat.shape[0])).reshape(SMALL_ROWS, 128)


def _unpack_small(slab):
    flat, out, r = slab.reshape(-1), {}, 0
    for name, n in _SMALL:
        out[name] = flat[r:r + DEPTH * n].reshape(DEPTH, n)
        r += DEPTH * n
    return out


_ANY = pl.BlockSpec(memory_space=pl.ANY)


def _place():
    return lax.axis_index("x"), lax.axis_index("y"), lax.axis_index("c")


def _other_chips(x, y):
    return [(1 - x, y), (x, 1 - y), (1 - x, 1 - y)]


def _chip_exchange(mine_of, out_refs, send_sems, recv_sems, local_sems):
    x, y, c = _place()
    q = 2 * x + y
    started = []
    for w, out_ref in enumerate(out_refs):
        local = pltpu.make_async_copy(mine_of(w, q), out_ref.at[q], local_sems.at[w])
        local.start()
        started.append(local)
    sends = []
    for k, (px, py) in enumerate(_other_chips(x, y)):
        for w, out_ref in enumerate(out_refs):
            cp = pltpu.make_async_remote_copy(src_ref=mine_of(w, 2 * px + py), dst_ref=out_ref.at[q],
                                              send_sem=send_sems.at[3 * w + k], recv_sem=recv_sems.at[3 * w + k],
                                              device_id=(px, py, c), device_id_type=MESH)
            cp.start()
            sends.append(cp)
    for k, (px, py) in enumerate(_other_chips(x, y)):
        for w, out_ref in enumerate(out_refs):
            pltpu.make_async_remote_copy(src_ref=mine_of(w, q), dst_ref=out_ref.at[2 * px + py],
                                         send_sem=send_sems.at[3 * w + k], recv_sem=recv_sems.at[3 * w + k],
                                         device_id=(px, py, c), device_id_type=MESH).wait_recv()
    for cp in sends:
        cp.wait_send()
    for local in started:
        local.wait()


def _sem_scratch(n):
    return [pltpu.SemaphoreType.DMA((3 * n,)), pltpu.SemaphoreType.DMA((3 * n,)), pltpu.SemaphoreType.DMA((n,))]


def _gather_weights(mine):
    n = len(mine)

    def body(*refs):
        in_refs, out_refs = refs[:n], refs[n:2 * n]
        send_sems, recv_sems, local_sems, pair_send, pair_recv = refs[2 * n:]
        x, y, c = _place()
        _chip_exchange(lambda w, q: in_refs[w].at[c], [o.at[c] for o in out_refs], send_sems, recv_sems, local_sems)
        sibling = (x, y, 1 - c)
        fwds = []
        for w, o in enumerate(out_refs):
            cp = pltpu.make_async_remote_copy(src_ref=o.at[c], dst_ref=o.at[c], send_sem=pair_send.at[w],
                                              recv_sem=pair_recv.at[w], device_id=sibling, device_id_type=MESH)
            cp.start()
            fwds.append(cp)
        for w, o in enumerate(out_refs):
            pltpu.make_async_remote_copy(src_ref=o.at[1 - c], dst_ref=o.at[1 - c], send_sem=pair_send.at[w],
                                         recv_sem=pair_recv.at[w], device_id=sibling, device_id_type=MESH).wait_recv()
        for cp in fwds:
            cp.wait_send()

    return pl.pallas_call(
        body, name="gather_weights", in_specs=[_ANY] * n, out_specs=[_ANY] * n,
        out_shape=[jax.ShapeDtypeStruct((DEPTH, N_CHIPS) + m.shape[1:], m.dtype) for m in mine],
        scratch_shapes=_sem_scratch(n) + [pltpu.SemaphoreType.DMA((n,)), pltpu.SemaphoreType.DMA((n,))],
    )(*mine)


def _pair_exchange(gs):
    n = len(gs)

    def body(*refs):
        g_refs, a_refs, send_sems, recv_sems = refs[:n], refs[n:2 * n], refs[2 * n], refs[2 * n + 1]
        x, y, c = _place()
        cps = []
        for w in range(n):
            cp = pltpu.make_async_remote_copy(src_ref=g_refs[w].at[1 - c], dst_ref=a_refs[w], send_sem=send_sems.at[w],
                                              recv_sem=recv_sems.at[w], device_id=(x, y, 1 - c), device_id_type=MESH)
            cp.start()
            cps.append(cp)
        for cp in cps:
            cp.wait()

    return pl.pallas_call(
        body, name="grad_pair_exchange", in_specs=[_ANY] * n, out_specs=[_ANY] * n,
        out_shape=[jax.ShapeDtypeStruct(g.shape[1:], g.dtype) for g in gs],
        scratch_shapes=[pltpu.SemaphoreType.DMA((n,)), pltpu.SemaphoreType.DMA((n,))],
    )(*gs)


def _row_block(rows):
    return math.gcd(rows, 256)


def _pair_sum(g, a, layer, name):
    _, nq, rows, cols = g.shape
    tb = _row_block(rows)

    def body(l_ref, g_ref, a_ref, o_ref):
        o_ref[...] = (g_ref[...] + a_ref[...]).astype(BF16)

    return pl.pallas_call(
        body, name=name,
        grid_spec=pltpu.PrefetchScalarGridSpec(
            num_scalar_prefetch=1, grid=(nq, rows // tb),
            in_specs=[pl.BlockSpec((None, None, tb, cols), lambda q, i, l_ref: (l_ref[0], q, i, 0)),
                      pl.BlockSpec((None, tb, cols), lambda q, i, l_ref: (q, i, 0))],
            out_specs=pl.BlockSpec((None, tb, cols), lambda q, i, l_ref: (q, i, 0))),
        out_shape=jax.ShapeDtypeStruct((nq, rows, cols), BF16),
        compiler_params=_params("parallel", "parallel"),
    )(layer.reshape(1).astype(jnp.int32), g, a)


def _shard_exchange(ps):
    n = len(ps)

    def body(*refs):
        p_refs, b_refs = refs[:n], refs[n:2 * n]
        send_sems, recv_sems, local_sems = refs[2 * n:]
        _chip_exchange(lambda w, q: p_refs[w].at[q], b_refs, send_sems, recv_sems, local_sems)

    return pl.pallas_call(
        body, name="grad_shard_exchange", in_specs=[_ANY] * n, out_specs=[_ANY] * n,
        out_shape=[jax.ShapeDtypeStruct(p.shape, p.dtype) for p in ps],
        scratch_shapes=_sem_scratch(n),
    )(*ps)


def _sum4(b, name):
    _, rows, cols = b.shape
    tb = _row_block(rows)

    def body(b_ref, o_ref):
        o_ref[...] = ((b_ref[0].astype(F32) + b_ref[1].astype(F32)) + b_ref[2].astype(F32)) + b_ref[3].astype(F32)

    return pl.pallas_call(
        body, name=name, grid=(rows // tb,),
        in_specs=[pl.BlockSpec((N_CHIPS, tb, cols), lambda i: (0, i, 0))],
        out_specs=pl.BlockSpec((tb, cols), lambda i: (i, 0)),
        out_shape=jax.ShapeDtypeStruct((rows, cols), F32),
        compiler_params=_params("parallel"),
    )(b)


def _result_exchange(gcs):
    n = len(gcs)

    def body(*refs):
        g_refs, o_refs, send_sems, recv_sems = refs[:n], refs[n:2 * n], refs[2 * n], refs[2 * n + 1]
        x, y, c = _place()
        cps = []
        for w in range(n):
            cp = pltpu.make_async_remote_copy(src_ref=g_refs[w], dst_ref=o_refs[w], send_sem=send_sems.at[w],
                                              recv_sem=recv_sems.at[w], device_id=(x, y, 1 - c), device_id_type=MESH)
            cp.start()
            cps.append(cp)
        for cp in cps:
            cp.wait()

    return pl.pallas_call(
        body, name="grad_result_exchange", in_specs=[_ANY] * n, out_specs=[_ANY] * n,
        out_shape=[jax.ShapeDtypeStruct(g.shape, g.dtype) for g in gcs],
        scratch_shapes=[pltpu.SemaphoreType.DMA((n,)), pltpu.SemaphoreType.DMA((n,))],
    )(*gcs)


def _allreduce_small(v):
    def body(v_ref, o_ref, buf, send_sems, recv_sems):
        x, y, c = _place()
        me = 4 * x + 2 * y + c
        buf[me] = v_ref[...]
        peers = []
        for k in range(1, N_DEV):
            px = 1 - x if k & 4 else x
            py = 1 - y if k & 2 else y
            pc = 1 - c if k & 1 else c
            peers.append((px, py, pc))
        sends = []
        for k, peer in enumerate(peers):
            cp = pltpu.make_async_remote_copy(src_ref=v_ref, dst_ref=buf.at[me], send_sem=send_sems.at[k],
                                              recv_sem=recv_sems.at[k], device_id=peer, device_id_type=MESH)
            cp.start()
            sends.append(cp)
        for k, (px, py, pc) in enumerate(peers):
            pltpu.make_async_remote_copy(src_ref=v_ref, dst_ref=buf.at[4 * px + 2 * py + pc], send_sem=send_sems.at[k],
                                         recv_sem=recv_sems.at[k], device_id=(px, py, pc),
                                         device_id_type=MESH).wait_recv()
        for cp in sends:
            cp.wait_send()
        acc = buf[0]
        for i in range(1, N_DEV):
            acc = acc + buf[i]
        o_ref[...] = acc

    vm = pl.BlockSpec(memory_space=pltpu.VMEM)
    return pl.pallas_call(
        body, name="small_allreduce", in_specs=[vm], out_specs=vm,
        out_shape=jax.ShapeDtypeStruct(v.shape, F32),
        scratch_shapes=[pltpu.VMEM((N_DEV,) + v.shape, F32), pltpu.SemaphoreType.DMA((N_DEV - 1,)),
                        pltpu.SemaphoreType.DMA((N_DEV - 1,))],
    )(v)


def _adam_update(w, g, m, v):
    nm = ADAM_B1 * m + (1.0 - ADAM_B1) * g
    nv = ADAM_B2 * v + (1.0 - ADAM_B2) * (g * g)
    m_hat = nm / (1.0 - ADAM_B1 ** ADAM_STEP)
    v_hat = nv / (1.0 - ADAM_B2 ** ADAM_STEP)
    return -ADAM_LR * (m_hat / (jnp.sqrt(v_hat) + ADAM_EPS) + ADAM_WD * w), nm, nv


def _adamw_small(w, g, m, v, name):
    def body(w_ref, g_ref, m_ref, v_ref, d_ref, nm_ref, nv_ref):
        d_ref[...], nm_ref[...], nv_ref[...] = _adam_update(w_ref[...], g_ref[...], m_ref[...], v_ref[...])

    vm = pl.BlockSpec(memory_space=pltpu.VMEM)
    return pl.pallas_call(
        body, name=name, in_specs=[vm] * 4, out_specs=[vm] * 3,
        out_shape=[jax.ShapeDtypeStruct(w.shape, F32)] * 3,
    )(w, g, m, v)


def _adamw_big(w, m, v, g_own, g_other, layer, name):
    _, rows, cols = w.shape
    tb = _row_block(rows)

    def body(l_ref, w_ref, m_ref, v_ref, go_ref, gx_ref, g_ref, d_ref, nm_ref, nv_ref):
        gv = jnp.where(pl.program_id(0) == l_ref[0], go_ref[...], gx_ref[...])
        g_ref[...] = gv
        d_ref[...], nm_ref[...], nv_ref[...] = _adam_update(w_ref[...], gv, m_ref[...], v_ref[...])

    per_layer = pl.BlockSpec((None, tb, cols), lambda l, i, l_ref: (l, i, 0))
    shared = pl.BlockSpec((tb, cols), lambda l, i, l_ref: (i, 0))
    return pl.pallas_call(
        body, name=name,
        grid_spec=pltpu.PrefetchScalarGridSpec(
            num_scalar_prefetch=1, grid=(DEPTH, rows // tb),
            in_specs=[per_layer, per_layer, per_layer, shared, shared], out_specs=[per_layer] * 4),
        out_shape=[jax.ShapeDtypeStruct(w.shape, F32)] * 4,
        compiler_params=_params("parallel", "parallel"),
    )(layer.reshape(1).astype(jnp.int32), w, m, v, g_own, g_other)


def kernel(x, w_in, b_fgate, hgrn_lb_logits, hgrn_norm_g, w_branch_a, w_branch_b, w_out, ln1_g, ln1_b, w_ff_in, w_ff_out, ln2_g, ln2_b, loss_target, m_w_in, m_b_fgate, m_hgrn_lb_logits, m_hgrn_norm_g, m_w_branch_a, m_w_branch_b, m_w_out, m_ln1_g, m_ln1_b, m_w_ff_in, m_w_ff_out, m_ln2_g, m_ln2_b, v_w_in, v_b_fgate, v_hgrn_lb_logits, v_hgrn_norm_g, v_w_branch_a, v_w_branch_b, v_w_out, v_ln1_g, v_ln1_b, v_w_ff_in, v_w_ff_out, v_ln2_g, v_ln2_b):
    weights = dict(w_in=w_in, b_fgate=b_fgate, hgrn_lb_logits=hgrn_lb_logits, hgrn_norm_g=hgrn_norm_g,
                   w_branch_a=w_branch_a, w_branch_b=w_branch_b, w_out=w_out, ln1_g=ln1_g, ln1_b=ln1_b,
                   w_ff_in=w_ff_in, w_ff_out=w_ff_out, ln2_g=ln2_g, ln2_b=ln2_b)
    mom1 = dict(w_in=m_w_in, b_fgate=m_b_fgate, hgrn_lb_logits=m_hgrn_lb_logits, hgrn_norm_g=m_hgrn_norm_g,
                w_branch_a=m_w_branch_a, w_branch_b=m_w_branch_b, w_out=m_w_out, ln1_g=m_ln1_g, ln1_b=m_ln1_b,
                w_ff_in=m_w_ff_in, w_ff_out=m_w_ff_out, ln2_g=m_ln2_g, ln2_b=m_ln2_b)
    mom2 = dict(w_in=v_w_in, b_fgate=v_b_fgate, hgrn_lb_logits=v_hgrn_lb_logits, hgrn_norm_g=v_hgrn_norm_g,
                w_branch_a=v_w_branch_a, w_branch_b=v_w_branch_b, w_out=v_w_out, ln1_g=v_ln1_g, ln1_b=v_ln1_b,
                w_ff_in=v_w_ff_in, w_ff_out=v_w_ff_out, ln2_g=v_ln2_g, ln2_b=v_ln2_b)
    core = lax.axis_index("c")

    gathered = _gather_weights([weights[name].astype(BF16) for name, _, _, _ in _BIG])
    wfull = [{key: _from_chips(gathered[w][l], axis) for w, (_, key, _, axis) in enumerate(_BIG)}
             for l in range(DEPTH)]
    small = {name: weights[name] for name, _ in _SMALL}

    loss_part, grad_x, grads, d_logits = _local_step(x[0], loss_target[0], wfull, small)

    g_all = [jnp.stack([_by_chip(grads[l][key], axis) for l in range(DEPTH)]) for _, key, _, axis in _BIG]
    received = _pair_exchange(g_all)
    pair = [_pair_sum(g_all[w], received[w], core, f"grad_pair_sum_{w}") for w in range(N_BIG)]
    by_chip = _shard_exchange(pair)
    g_layer = [_sum4(by_chip[w], f"grad_chip_sum_{w}") for w in range(N_BIG)]
    g_other = _result_exchange(g_layer)
    out_g, out_d, out_m, out_v = {}, {}, {}, {}
    for w, (name, _, _, _) in enumerate(_BIG):
        out_g[name], out_d[name], out_m[name], out_v[name] = _adamw_big(
            weights[name], mom1[name], mom2[name], g_layer[w], g_other[w], core, f"adamw_{name}")

    small_grads = {name: jnp.stack([grads[l][key] for l in range(DEPTH)])
                   for name, key in [("b_fgate", "b_fgate"), ("hgrn_norm_g", "norm_g"), ("ln1_g", "ln1_g"),
                                     ("ln1_b", "ln1_b"), ("ln2_g", "ln2_g"), ("ln2_b", "ln2_b")]}
    small_grads["hgrn_lb_logits"] = d_logits
    gs = _allreduce_small(_pack_small(small_grads))
    ds, ms, vs = _adamw_small(_pack_small(small), gs, _pack_small({n: mom1[n] for n, _ in _SMALL}),
                              _pack_small({n: mom2[n] for n, _ in _SMALL}), "adamw_small")
    for tree, slab in ((out_g, gs), (out_d, ds), (out_m, ms), (out_v, vs)):
        tree.update(_unpack_small(slab))

    loss = lax.psum(loss_part, ("x", "y", "c"))
    order = ["w_in", "b_fgate", "hgrn_lb_logits", "hgrn_norm_g", "w_branch_a", "w_branch_b", "w_out", "ln1_g", "ln1_b",
             "w_ff_in", "w_ff_out", "ln2_g", "ln2_b"]
    return (loss, grad_x[None], *[out_g[n] for n in order], *[out_d[n] for n in order],
            *[out_m[n] for n in order], *[out_v[n] for n in order])
```

```python
import functools
import math

import jax
import jax.numpy as jnp
import numpy as np
from jax import lax
from jax.experimental import pallas as pl
from jax.experimental.pallas import tpu as pltpu

F32 = jnp.float32
BF16 = jnp.bfloat16

D_MODEL = 1024
DEPTH = 2
A_HEADS = 8
A_WIDTH = 512
B_WIDTH = 512
B_HEADS = 4
HD = 128
CHUNK = 64
SUB = 16
FFN_HIDDEN = 2816
IN_TOTAL = 5640
ALPHA = (2 * DEPTH) ** 0.25
LN_EPS = 1e-5
RMS_EPS = 1e-6
ADAM_LR = 0.001
ADAM_B1 = 0.9
ADAM_B2 = 0.999
ADAM_EPS = 1e-08
ADAM_WD = 0.01
ADAM_STEP = 10
EXP_CLAMP = 60.0

VMEM_LIMIT_BYTES = 56 * 1024 * 1024
MM_ROWS = 1024
DW_ROWS = 2048
N_CHIPS = 4
N_DEV = 8
MESH = pl.DeviceIdType.MESH

_DN = {
    "nn": (((1,), (0,)), ((), ())),
    "nt": (((1,), (1,)), ((), ())),
    "tn": (((0,), (0,)), ((), ())),
}


def _dot(a, b, mode="nn"):
    return lax.dot_general(a.astype(BF16), b.astype(BF16), _DN[mode], preferred_element_type=F32)


def _pieces(x):
    h = x.astype(BF16)
    r = x - h.astype(F32)
    m = r.astype(BF16)
    return h, m, (r - m.astype(F32)).astype(BF16)


def _dot_hi(a, b, mode="nn", exact="a"):
    if exact == "a":
        h, m, l = _pieces(b)
        return (_dot(a, l, mode) + _dot(a, m, mode)) + _dot(a, h, mode)
    h, m, l = _pieces(a)
    return (_dot(l, b, mode) + _dot(m, b, mode)) + _dot(h, b, mode)


def _hdot(a, b, mode="nn"):
    bh, bl, _ = _pieces(b)
    return _dot(a, bl, mode) + _dot(a, bh, mode)


def _params(*sem):
    return pltpu.CompilerParams(dimension_semantics=sem, vmem_limit_bytes=VMEM_LIMIT_BYTES)


def _sigmoid(x):
    return 1.0 / (1.0 + jnp.exp(-x))


def _matmul(a, b, mode, out_dtype, tm, tn, tk, name):
    if mode == "nn":
        (m, k), (k2, n) = a.shape, b.shape
    elif mode == "nt":
        (m, k), (n, k2) = a.shape, b.shape
    else:
        (k, m), (k2, n) = a.shape, b.shape
    assert k == k2, (a.shape, b.shape, mode)
    tm, tn, tk = min(tm, m), min(tn, n), min(tk, k)
    assert m % tm == 0 and n % tn == 0 and k % tk == 0, (a.shape, b.shape, tm, tn, tk)
    nk = k // tk
    if mode == "tn":
        a_spec = pl.BlockSpec((tk, tm), lambda j, i, kk: (kk, i))
    else:
        a_spec = pl.BlockSpec((tm, tk), lambda j, i, kk: (i, kk))
    if mode == "nt":
        b_spec = pl.BlockSpec((tn, tk), lambda j, i, kk: (j, kk))
    else:
        b_spec = pl.BlockSpec((tk, tn), lambda j, i, kk: (kk, j))
    use_acc = nk > 1 and out_dtype != F32

    def body(a_ref, b_ref, o_ref, *scratch):
        p = _dot(a_ref[...], b_ref[...], mode)
        if nk == 1:
            o_ref[...] = p.astype(out_dtype)
            return
        acc_ref = scratch[0] if use_acc else o_ref
        kk = pl.program_id(2)

        @pl.when(kk == 0)
        def _():
            acc_ref[...] = p

        @pl.when(kk > 0)
        def _():
            acc_ref[...] += p

        if use_acc:
            @pl.when(kk == nk - 1)
            def _():
                o_ref[...] = acc_ref[...].astype(out_dtype)

    return pl.pallas_call(
        body,
        name=name,
        grid=(n // tn, m // tm, nk),
        in_specs=[a_spec, b_spec],
        out_specs=pl.BlockSpec((tm, tn), lambda j, i, kk: (i, j)),
        out_shape=jax.ShapeDtypeStruct((m, n), out_dtype),
        scratch_shapes=[pltpu.VMEM((tm, tn), F32)] if use_acc else [],
        compiler_params=_params("parallel", "parallel", "arbitrary"),
    )(a, b)


def _mm_res_ln(a, w, res, g, b, name, tm=512):
    t, k = a.shape
    d = w.shape[1]
    tm = min(tm, t)

    def body(a_ref, w_ref, r_ref, g_ref, b_ref, y_ref, yb_ref, xh_ref, rs_ref):
        z = ALPHA * r_ref[...] + _dot(a_ref[...], w_ref[...])
        mu = jnp.mean(z, axis=-1, keepdims=True)
        zc = z - mu
        var = jnp.mean(zc * zc, axis=-1, keepdims=True)
        rstd = lax.rsqrt(var + LN_EPS)
        xh = zc * rstd
        y = xh * g_ref[...] + b_ref[...]
        y_ref[...] = y
        yb_ref[...] = y.astype(BF16)
        xh_ref[...] = xh
        rs_ref[...] = rstd

    row = lambda i: (i, 0)
    fix = lambda i: (0, 0)
    return pl.pallas_call(
        body,
        name=name,
        grid=(t // tm,),
        in_specs=[pl.BlockSpec((tm, k), row), pl.BlockSpec((k, d), fix), pl.BlockSpec((tm, d), row),
                  pl.BlockSpec((1, d), fix), pl.BlockSpec((1, d), fix)],
        out_specs=[pl.BlockSpec((tm, d), row), pl.BlockSpec((tm, d), row), pl.BlockSpec((tm, d), row),
                   pl.BlockSpec((tm, 1), row)],
        out_shape=[jax.ShapeDtypeStruct((t, d), F32), jax.ShapeDtypeStruct((t, d), BF16),
                   jax.ShapeDtypeStruct((t, d), F32), jax.ShapeDtypeStruct((t, 1), F32)],
        compiler_params=_params("parallel"),
    )(a, w, res, g.reshape(1, d), b.reshape(1, d))


def _ln_bwd(dys, coefs, xhat, rstd, g, name, tm=512):
    t, d = xhat.shape
    tm = min(tm, t)
    n_in = len(dys)

    def body(*refs):
        dy_refs = refs[:n_in]
        xh_ref, rs_ref, g_ref, dz_ref, dzb_ref, dg_ref, db_ref = refs[n_in:]
        dy = coefs[0] * dy_refs[0][...].astype(F32)
        for c, r in zip(coefs[1:], dy_refs[1:]):
            dy = dy + c * r[...].astype(F32)
        xh = xh_ref[...]
        dxh = dy * g_ref[...]
        m1 = jnp.mean(dxh, axis=-1, keepdims=True)
        m2 = jnp.mean(dxh * xh, axis=-1, keepdims=True)
        dz = rs_ref[...] * (dxh - m1 - xh * m2)
        dz_ref[...] = dz
        dzb_ref[...] = dz.astype(BF16)
        pg = jnp.sum(dy * xh, axis=0, keepdims=True)
        pb = jnp.sum(dy, axis=0, keepdims=True)

        @pl.when(pl.program_id(0) == 0)
        def _():
            dg_ref[...] = pg
            db_ref[...] = pb

        @pl.when(pl.program_id(0) > 0)
        def _():
            dg_ref[...] += pg
            db_ref[...] += pb

    row = lambda i: (i, 0)
    fix = lambda i: (0, 0)
    return pl.pallas_call(
        body,
        name=name,
        grid=(t // tm,),
        in_specs=[pl.BlockSpec((tm, d), row)] * n_in
        + [pl.BlockSpec((tm, d), row), pl.BlockSpec((tm, 1), row), pl.BlockSpec((1, d), fix)],
        out_specs=[pl.BlockSpec((tm, d), row), pl.BlockSpec((tm, d), row), pl.BlockSpec((1, d), fix),
                   pl.BlockSpec((1, d), fix)],
        out_shape=[jax.ShapeDtypeStruct((t, d), F32), jax.ShapeDtypeStruct((t, d), BF16),
                   jax.ShapeDtypeStruct((1, d), F32), jax.ShapeDtypeStruct((1, d), F32)],
        compiler_params=_params("arbitrary"),
    )(*dys, xhat, rstd, g.reshape(1, d))


def _loss_head(y, target, name="loss_head", tm=512):
    t, d = y.shape
    tm = min(tm, t)

    def body(y_ref, t_ref, dy_ref, l_ref):
        e = y_ref[...] - t_ref[...]
        dy_ref[...] = e * (1.0 / d)
        part = jnp.full((8, 128), 0.5 / d, F32) * jnp.sum(e * e)

        @pl.when(pl.program_id(0) == 0)
        def _():
            l_ref[...] = part

        @pl.when(pl.program_id(0) > 0)
        def _():
            l_ref[...] += part

    row = lambda i: (i, 0)
    return pl.pallas_call(
        body,
        name=name,
        grid=(t // tm,),
        in_specs=[pl.BlockSpec((tm, d), row), pl.BlockSpec((tm, d), row)],
        out_specs=[pl.BlockSpec((tm, d), row), pl.BlockSpec((8, 128), lambda i: (0, 0))],
        out_shape=[jax.ShapeDtypeStruct((t, d), F32), jax.ShapeDtypeStruct((8, 128), F32)],
        compiler_params=_params("arbitrary"),
    )(y, target)


FFN_COLS = FFN_HIDDEN // 2


def _ffn_in_swiglu(xb, wu, wg, name, tm=512):
    t, d = xb.shape
    tm = min(tm, t)

    def body(x_ref, wu_ref, wg_ref, a_ref, u_ref, g_ref):
        x = x_ref[...]
        u = _dot(x, wu_ref[...])
        g = _dot(x, wg_ref[...])
        u_ref[...] = u
        g_ref[...] = g
        a_ref[...] = (g * _sigmoid(g) * u).astype(BF16)

    wspec = pl.BlockSpec((d, FFN_COLS), lambda j, i: (0, j))
    out = pl.BlockSpec((tm, FFN_COLS), lambda j, i: (i, j))
    return pl.pallas_call(
        body,
        name=name,
        grid=(FFN_HIDDEN // FFN_COLS, t // tm),
        in_specs=[pl.BlockSpec((tm, d), lambda j, i: (i, 0)), wspec, wspec],
        out_specs=[out, out, out],
        out_shape=[jax.ShapeDtypeStruct((t, FFN_HIDDEN), BF16), jax.ShapeDtypeStruct((t, FFN_HIDDEN), F32),
                   jax.ShapeDtypeStruct((t, FFN_HIDDEN), F32)],
        compiler_params=_params("parallel", "parallel"),
    )(xb, wu, wg)


def _ffn_out_dx_swiglu(dzb, w_ff_out, u, g, name, tm=512):
    t, d = dzb.shape
    tm = min(tm, t)

    def body(dz_ref, w_ref, u_ref, g_ref, du_ref, dg_ref):
        da = _dot(dz_ref[...], w_ref[...], "nt")
        gv = g_ref[...]
        sg = _sigmoid(gv)
        du_ref[...] = (da * gv * sg).astype(BF16)
        dg_ref[...] = (da * u_ref[...] * (sg * (1.0 + gv * (1.0 - sg)))).astype(BF16)

    blk = pl.BlockSpec((tm, FFN_COLS), lambda j, i: (i, j))
    return pl.pallas_call(
        body,
        name=name,
        grid=(FFN_HIDDEN // FFN_COLS, t // tm),
        in_specs=[pl.BlockSpec((tm, d), lambda j, i: (i, 0)), pl.BlockSpec((FFN_COLS, d), lambda j, i: (j, 0)), blk, blk],
        out_specs=[blk, blk],
        out_shape=[jax.ShapeDtypeStruct((t, FFN_HIDDEN), BF16)] * 2,
        compiler_params=_params("parallel", "parallel"),
    )(dzb, w_ff_out, u, g)


def _merge_fwd(ya, yb, wpa, wpb, rest, name, tm=512):
    t = ya.shape[0]
    tm = min(tm, t)

    def body(ya_ref, yb_ref, wa_ref, wb_ref, ga_ref, gb_ref, o_ref):
        pa = _dot(ya_ref[...], wa_ref[...])
        pb = _dot(yb_ref[...], wb_ref[...])
        o_ref[...] = (_sigmoid(ga_ref[...]) * pa + _sigmoid(gb_ref[...]) * pb).astype(BF16)

    row = lambda i: (i, 0)
    fix = lambda i: (0, 0)
    return pl.pallas_call(
        body,
        name=name,
        grid=(t // tm,),
        in_specs=[pl.BlockSpec((tm, A_WIDTH), row), pl.BlockSpec((tm, B_WIDTH), row),
                  pl.BlockSpec((A_WIDTH, D_MODEL), fix), pl.BlockSpec((B_WIDTH, D_MODEL), fix),
                  pl.BlockSpec((tm, D_MODEL), lambda i: (i, 0)), pl.BlockSpec((tm, D_MODEL), lambda i: (i, 1))],
        out_specs=pl.BlockSpec((tm, D_MODEL), row),
        out_shape=jax.ShapeDtypeStruct((t, D_MODEL), BF16),
        compiler_params=_params("parallel"),
    )(ya, yb, wpa, wpb, rest, rest)


def _merge_bwd(dzb, w_out, ya, yb, wpa, wpb, rest, name, tm=512):
    t = ya.shape[0]
    tm = min(tm, t)

    def body(dz_ref, wo_ref, ya_ref, yb_ref, wa_ref, wb_ref, ga_ref, gb_ref, dg_ref, dpa_ref, dpb_ref, dya_ref,
             dyb_ref):
        dm_v = _dot(dz_ref[...], wo_ref[...], "nt")
        pa = _dot(ya_ref[...], wa_ref[...])
        pb = _dot(yb_ref[...], wb_ref[...])
        sa = _sigmoid(ga_ref[...])
        sb = _sigmoid(gb_ref[...])
        dg_ref[:, :D_MODEL] = (dm_v * pa * sa * (1.0 - sa)).astype(BF16)
        dg_ref[:, D_MODEL:] = (dm_v * pb * sb * (1.0 - sb)).astype(BF16)
        dpa = (dm_v * sa).astype(BF16)
        dpb = (dm_v * sb).astype(BF16)
        dpa_ref[...] = dpa
        dpb_ref[...] = dpb
        dya_ref[...] = _dot(dpa, wa_ref[...], "nt").astype(BF16)
        dyb_ref[...] = _dot(dpb, wb_ref[...], "nt")

    row = lambda i: (i, 0)
    fix = lambda i: (0, 0)
    return pl.pallas_call(
        body,
        name=name,
        grid=(t // tm,),
        in_specs=[pl.BlockSpec((tm, D_MODEL), row), pl.BlockSpec((D_MODEL, D_MODEL), fix),
                  pl.BlockSpec((tm, A_WIDTH), row), pl.BlockSpec((tm, B_WIDTH), row),
                  pl.BlockSpec((A_WIDTH, D_MODEL), fix), pl.BlockSpec((B_WIDTH, D_MODEL), fix),
                  pl.BlockSpec((tm, D_MODEL), lambda i: (i, 0)), pl.BlockSpec((tm, D_MODEL), lambda i: (i, 1))],
        out_specs=[pl.BlockSpec((tm, 2 * D_MODEL), row), pl.BlockSpec((tm, D_MODEL), row),
                   pl.BlockSpec((tm, D_MODEL), row), pl.BlockSpec((tm, A_WIDTH), row), pl.BlockSpec((tm, B_WIDTH), row)],
        out_shape=[jax.ShapeDtypeStruct((t, 2 * D_MODEL), BF16), jax.ShapeDtypeStruct((t, D_MODEL), BF16),
                   jax.ShapeDtypeStruct((t, D_MODEL), BF16), jax.ShapeDtypeStruct((t, A_WIDTH), BF16),
                   jax.ShapeDtypeStruct((t, B_WIDTH), F32)],
        compiler_params=_params("parallel"),
    )(dzb, w_out, ya, yb, wpa, wpb, rest, rest)


FA_BLOCK = 4224 // 128 - 1


def _tri(n, lower):
    r = lax.broadcasted_iota(jnp.int32, (n, n), 0)
    c = lax.broadcasted_iota(jnp.int32, (n, n), 1)
    return jnp.where((r >= c) if lower else (r <= c), 1.0, 0.0).astype(F32)


def _head_spread(expand):
    shape = (128, A_WIDTH) if expand else (A_WIDTH, 128)
    r = lax.broadcasted_iota(jnp.int32, shape, 0)
    c = lax.broadcasted_iota(jnp.int32, shape, 1)
    hit = ((c >= 64 * r) & (c < 64 * r + 64)) if expand else (r == 64 * c)
    return jnp.where(hit, 1.0, 0.0).astype(F32)


def _fox_gate_fwd(rest, bf, name, tb=512):
    t = rest.shape[0]
    tb = min(tb, t)

    def body(fa_ref, bf_ref, f_ref, fc_ref, carry):
        @pl.when(pl.program_id(0) == 0)
        def _():
            carry[...] = jnp.zeros_like(carry)

        z = fa_ref[...] + bf_ref[...]
        logf = jnp.minimum(z, 0.0) - jnp.log(1.0 + jnp.exp(-jnp.abs(z)))
        f = _dot_hi(_tri(tb, True), logf) + carry[...]
        f_ref[...] = f
        fc_ref[...] = _dot_hi(f, _head_spread(True), exact="b")
        carry[...] = f[tb - 1:tb, :]

    return pl.pallas_call(
        body,
        name=name,
        grid=(t // tb,),
        in_specs=[pl.BlockSpec((tb, 128), lambda i: (i, FA_BLOCK)), pl.BlockSpec((1, 128), lambda i: (0, 0))],
        out_specs=[pl.BlockSpec((tb, 128), lambda i: (i, 0)), pl.BlockSpec((tb, A_WIDTH), lambda i: (i, 0))],
        out_shape=[jax.ShapeDtypeStruct((t, 128), F32), jax.ShapeDtypeStruct((t, A_WIDTH), F32)],
        scratch_shapes=[pltpu.VMEM((1, 128), F32)],
        compiler_params=_params("arbitrary"),
    )(rest, bf)


def _fox_gate_bwd(rsum, csum, rest, bf, name, tb=512):
    t = rest.shape[0]
    tb = min(tb, t)
    nb = t // tb

    def body(rs_ref, cs_ref, fa_ref, bf_ref, dfa_ref, dbf_ref, carry):
        @pl.when(pl.program_id(0) == 0)
        def _():
            carry[...] = jnp.zeros_like(carry)

        d_f = _dot_hi(rs_ref[...] - cs_ref[...], _head_spread(False), exact="b")
        dlogf = _dot_hi(_tri(tb, False), d_f) + carry[...]
        carry[...] = dlogf[0:1, :]
        z = fa_ref[...] + bf_ref[...]
        dz = dlogf * _sigmoid(-z)
        dfa_ref[...] = dz.astype(BF16)
        part = jnp.sum(dz, axis=0, keepdims=True)

        @pl.when(pl.program_id(0) == 0)
        def _():
            dbf_ref[...] = part

        @pl.when(pl.program_id(0) > 0)
        def _():
            dbf_ref[...] += part

    return pl.pallas_call(
        body,
        name=name,
        grid=(nb,),
        in_specs=[pl.BlockSpec((tb, A_WIDTH), lambda i: (nb - 1 - i, 0)),
                  pl.BlockSpec((tb, A_WIDTH), lambda i: (nb - 1 - i, 0)),
                  pl.BlockSpec((tb, 128), lambda i: (nb - 1 - i, FA_BLOCK)),
                  pl.BlockSpec((1, 128), lambda i: (0, 0))],
        out_specs=[pl.BlockSpec((tb, 128), lambda i: (nb - 1 - i, 0)), pl.BlockSpec((1, 128), lambda i: (0, 0))],
        out_shape=[jax.ShapeDtypeStruct((t, 128), BF16), jax.ShapeDtypeStruct((1, 128), F32)],
        scratch_shapes=[pltpu.VMEM((1, 128), F32)],
        compiler_params=_params("arbitrary"),
    )(rsum, csum, rest, bf)


ATT_BLOCK = 512


def _head_mask(shape, j):
    lane = lax.broadcasted_iota(jnp.int32, shape, 1)
    return (lane < 64) if j == 0 else (lane >= 64)


def _aug_lanes(tb, j):
    lane = lax.broadcasted_iota(jnp.int32, (tb, 128), 1)
    own = (lane < 64) if j == 0 else (lane >= 64)
    return own, lane - 64 * (1 - j)


def _aug_query(own, li, q, pieces):
    h, m, l = pieces
    one, zero = jnp.ones_like(h), jnp.zeros_like(h)
    spare = jnp.where(li == 0, h, jnp.where(li == 1, m, jnp.where(li == 2, l, jnp.where(li < 6, one, zero))))
    return jnp.where(own, q, spare)


def _fox_prep_fwd(qkv, fcol, name, tb=2048):
    t = qkv.shape[0]
    tb = min(tb, t)

    def body(q_ref, k_ref, v_ref, fc_ref, qa_ref, ka_ref, va_ref, qn_ref, kn_ref):
        pieces = _pieces(pltpu.roll(fc_ref[...], 64, 1))
        h, m, l = pieces
        q, k, v = q_ref[...], k_ref[...], v_ref[...]
        first = _head_mask((tb, 128), 0)
        for nrm_ref, x in ((qn_ref, q.astype(F32)), (kn_ref, k.astype(F32))):
            n0 = jnp.max(jnp.sum(jnp.where(first, x * x, 0.0), axis=1, keepdims=True))
            n1 = jnp.max(jnp.sum(jnp.where(first, 0.0, x * x), axis=1, keepdims=True))
            nrm_ref[...] = jnp.where(_head_mask((8, 128), 0), n0, n1)
        one, zero = jnp.ones_like(h), jnp.zeros_like(h)
        for j in (0, 1):
            own, li = _aug_lanes(tb, j)
            cols = slice(128 * j, 128 * (j + 1))
            qa_ref[:, cols] = _aug_query(own, li, q * 0.125, pieces)
            ks = jnp.where(li < 3, one, jnp.where(li == 3, -h, jnp.where(li == 4, -m, jnp.where(li == 5, -l, zero))))
            ka_ref[:, cols] = jnp.where(own, k, ks)
            va_ref[:, cols] = jnp.where(own, v, one)

    blk = pl.BlockSpec((tb, 256), lambda i, h: (i, h))
    nrm = pl.BlockSpec((None, None, 8, 128), lambda i, h: (i, h, 0, 0))
    return pl.pallas_call(
        body, name=name, grid=(t // tb, 4),
        in_specs=[pl.BlockSpec((tb, 128), lambda i, h: (i, h)), pl.BlockSpec((tb, 128), lambda i, h: (i, 4 + h)),
                  pl.BlockSpec((tb, 128), lambda i, h: (i, 8 + h)), pl.BlockSpec((tb, 128), lambda i, h: (i, h))],
        out_specs=[blk, blk, blk, nrm, nrm],
        out_shape=[jax.ShapeDtypeStruct((t, 2 * A_WIDTH), BF16)] * 3
        + [jax.ShapeDtypeStruct((t // tb, 4, 8, 128), F32)] * 2,
        compiler_params=_params("parallel", "parallel"),
    )(qkv, qkv, qkv, fcol)


def _fox_prep_bwd(qkv, fcol, lse, do, o, name, tb=2048):
    t = qkv.shape[0]
    tb = min(tb, t)

    def body(q_ref, fc_ref, lse_ref, do_ref, o_ref, qb_ref, dob_ref):
        pieces = _pieces(pltpu.roll(fc_ref[...] - lse_ref[...], 64, 1))
        q = q_ref[...] * 0.125
        do_v = do_ref[...]
        prod = do_v.astype(F32) * o_ref[...].astype(F32)
        for j in (0, 1):
            own, li = _aug_lanes(tb, j)
            cols = slice(128 * j, 128 * (j + 1))
            qb_ref[:, cols] = _aug_query(own, li, q, pieces)
            delta = jnp.sum(jnp.where(own, prod, 0.0), axis=1, keepdims=True)
            h, m, l = _pieces(jnp.broadcast_to(delta, (tb, 128)))
            ds = jnp.where(li == 0, -h, jnp.where(li == 1, -m, jnp.where(li == 2, -l, jnp.zeros_like(h))))
            dob_ref[:, cols] = jnp.where(own, do_v, ds)

    pair = pl.BlockSpec((tb, 128), lambda i, h: (i, h))
    blk = pl.BlockSpec((tb, 256), lambda i, h: (i, h))
    return pl.pallas_call(
        body, name=name, grid=(t // tb, 4),
        in_specs=[pair, pair, pair, pair, pair],
        out_specs=[blk, blk],
        out_shape=[jax.ShapeDtypeStruct((t, 2 * A_WIDTH), BF16)] * 2,
        compiler_params=_params("parallel", "parallel"),
    )(qkv, fcol, lse, do, o)


def _tile_mask(n, transposed):
    r = lax.broadcasted_iota(jnp.int32, (n, n), 0)
    c = lax.broadcasted_iota(jnp.int32, (n, n), 1)
    return (c >= r) if transposed else (r >= c)


UNDERFLOW = -110.0


def _fox_block_ranges(qn, kn, fcum):
    t = fcum.shape[0]
    blk = min(ATT_BLOCK, t)
    nb = t // blk
    q2 = jnp.max(qn[:, :, 0, ::64].reshape(-1, A_HEADS), axis=0)
    k2 = jnp.max(kn[:, :, 0, ::64].reshape(-1, A_HEADS), axis=0)
    bound = 2.0 * jnp.sqrt(q2 * k2) * 0.125
    f = fcum[:, :A_HEADS]
    first = f[0::blk].T
    last = f[blk - 1::blk].T
    dead = (bound[:, None, None] + first[:, :, None] - last[:, None, :]) < UNDERFLOW
    qi = jnp.arange(nb)[None, :, None]
    kj = jnp.arange(nb)[None, None, :]
    dead = dead & (kj < qi)
    kstart = jnp.sum(dead, axis=2).astype(jnp.int32)
    qend = (kj[0] + jnp.sum((~dead) & (qi > kj), axis=1)).astype(jnp.int32)
    return kstart.reshape(-1), qend.reshape(-1)


def _fox_fwd(qa, ka, va, kstart, name):
    t = qa.shape[0]
    bq = min(ATT_BLOCK, t)
    nq = t // bq

    def body(ks_ref, q_ref, k_ref, v_ref, o_ref, lse_ref):
        i = pl.program_id(1)
        hp = pl.program_id(0)
        k0 = [ks_ref[(2 * hp + j) * nq + i] for j in (0, 1)]
        both0 = jnp.maximum(k0[0], k0[1])

        def head(j, kb, m, acc, masked):
            rows = pl.ds(pl.multiple_of(kb * bq, bq), bq)
            cols = slice(128 * j, 128 * (j + 1))
            s = _dot(q_ref[:, cols], k_ref[rows, cols], "nt")
            if masked:
                s = jnp.where(_tile_mask(bq, False), s, -jnp.inf)
            m_new = jnp.maximum(m, jnp.max(s, axis=1, keepdims=True))
            return m_new, jnp.exp(m - m_new) * acc + _dot(jnp.exp(s - m_new), v_ref[rows, cols])

        def pair(kb, carry, masked):
            return head(0, kb, carry[0], carry[1], masked) + head(1, kb, carry[2], carry[3], masked)

        init = (jnp.full((bq, 1), -jnp.inf, F32), jnp.zeros((bq, 128), F32))
        alone = [lax.fori_loop(k0[j], both0, lambda kb, c, j=j: head(j, kb, c[0], c[1], False), init) for j in (0, 1)]
        carry = lax.fori_loop(both0, i, lambda kb, c: pair(kb, c, False), alone[0] + alone[1])
        carry = pair(i, carry, True)
        outs = []
        for j in (0, 1):
            m, acc = carry[2 * j], carry[2 * j + 1]
            spare = 64 * (1 - j)
            l = acc[:, spare:spare + 1]
            outs.append((acc / l, m + jnp.log(l)))
        msk = _head_mask((bq, 128), 0)
        o_ref[...] = jnp.where(msk, outs[0][0], outs[1][0]).astype(BF16)
        lse_ref[...] = jnp.where(msk, outs[0][1], outs[1][1])

    res = pl.BlockSpec((t, 256), lambda h, i, tbl: (0, h))
    out = pl.BlockSpec((bq, 128), lambda h, i, tbl: (i, h))
    return pl.pallas_call(
        body,
        name=name,
        grid_spec=pltpu.PrefetchScalarGridSpec(
            num_scalar_prefetch=1, grid=(4, nq),
            in_specs=[pl.BlockSpec((bq, 256), lambda h, i, tbl: (i, h)), res, res],
            out_specs=[out, out]),
        out_shape=[jax.ShapeDtypeStruct((t, A_WIDTH), BF16), jax.ShapeDtypeStruct((t, A_WIDTH), F32)],
        compiler_params=_params("parallel", "parallel"),
    )(kstart, qa, ka, va)


def _fox_bwd(qb, ka, va, dob, qend, name):
    t = qb.shape[0]
    bk = min(ATT_BLOCK, t)
    nk = t // bk

    def body(qe_ref, k_ref, v_ref, q_hbm, do_hbm, dk_ref, dv_ref, cs_ref, dq_hbm, rs_hbm, q_scr, do_scr, dq_scr,
             sems):
        jb = pl.program_id(1)
        hp = pl.program_id(0)
        pair_cols = pl.ds(pl.multiple_of(hp * 256, 256), 256)

        @pl.when(jb == 0)
        def _():
            loads = [pltpu.make_async_copy(q_hbm.at[:, pair_cols], q_scr, sems.at[0]),
                     pltpu.make_async_copy(do_hbm.at[:, pair_cols], do_scr, sems.at[1])]
            for cp in loads:
                cp.start()
            dq_scr[...] = jnp.zeros_like(dq_scr)
            for cp in loads:
                cp.wait()

        i1 = [qe_ref[(2 * hp + j) * nk + jb] + 1 for j in (0, 1)]
        both1 = jnp.minimum(i1[0], i1[1])

        def head(j, ib, dk_acc, dv_acc, masked):
            rows = pl.ds(pl.multiple_of(ib * bk, bk), bk)
            cols = slice(128 * j, 128 * (j + 1))
            qs = q_scr[rows, cols]
            dos = do_scr[rows, cols]
            kj = k_ref[:, cols]
            st = _dot(kj, qs, "nt")
            if masked:
                st = jnp.where(_tile_mask(bk, True), st, -jnp.inf)
            pt = jnp.exp(st)
            dst = (pt * _dot(v_ref[:, cols], dos, "nt")).astype(BF16)
            dq_scr[rows, cols] += _dot(dst, kj, "tn")
            return dk_acc + _dot(dst, qs), dv_acc + _dot(pt, dos)

        def pair(ib, carry, masked):
            return head(0, ib, carry[0], carry[1], masked) + head(1, ib, carry[2], carry[3], masked)

        carry = pair(jb, (jnp.zeros((bk, 128), F32),) * 4, True)
        carry = lax.fori_loop(jb + 1, both1, lambda ib, c: pair(ib, c, False), carry)
        alone = [lax.fori_loop(jnp.maximum(both1, jb + 1), i1[j],
                               lambda ib, c, j=j: head(j, ib, c[0], c[1], False), carry[2 * j:2 * j + 2])
                 for j in (0, 1)]
        carry = alone[0] + alone[1]
        outs = []
        for j in (0, 1):
            spare = 64 * (1 - j)
            dk_acc, dv_acc = carry[2 * j], carry[2 * j + 1]
            outs.append((dk_acc, dv_acc, dk_acc[:, spare + 3:spare + 4]))
        msk = _head_mask((bk, 128), 0)
        dk_ref[...] = jnp.where(msk, outs[0][0], outs[1][0]).astype(BF16)
        dv_ref[...] = jnp.where(msk, outs[0][1], outs[1][1]).astype(BF16)
        cs_ref[...] = jnp.where(msk, outs[0][2], outs[1][2])

        @pl.when(jb == nk - 1)
        def _():
            def finish(r, carry):
                rows = pl.ds(pl.multiple_of(r * bk, bk), bk)
                x0, x1 = dq_scr[rows, 0:128], dq_scr[rows, 128:256]
                q_scr[rows, 0:128] = (jnp.where(msk, x0, x1) * 0.125).astype(BF16)
                dq_scr[rows, 0:128] = jnp.where(msk, x0[:, 64:65], x1[:, 0:1])
                return carry

            lax.fori_loop(0, nk, finish, 0)
            head_cols = pl.ds(pl.multiple_of(hp * 128, 128), 128)
            stores = [pltpu.make_async_copy(q_scr.at[:, 0:128], dq_hbm.at[:, head_cols], sems.at[0]),
                      pltpu.make_async_copy(dq_scr.at[:, 0:128], rs_hbm.at[:, head_cols], sems.at[1])]
            for cp in stores:
                cp.start()
            for cp in stores:
                cp.wait()

    blk = pl.BlockSpec((bk, 256), lambda h, i, tbl: (i, h))
    out = pl.BlockSpec((bk, 128), lambda h, i, tbl: (i, h))
    return pl.pallas_call(
        body,
        name=name,
        grid_spec=pltpu.PrefetchScalarGridSpec(
            num_scalar_prefetch=1, grid=(4, nk), in_specs=[blk, blk, _ANY, _ANY],
            out_specs=[out, out, out, _ANY, _ANY],
            scratch_shapes=[pltpu.VMEM((t, 256), BF16), pltpu.VMEM((t, 256), BF16), pltpu.VMEM((t, 256), F32),
                            pltpu.SemaphoreType.DMA((2,))]),
        out_shape=[jax.ShapeDtypeStruct((t, A_WIDTH), BF16), jax.ShapeDtypeStruct((t, A_WIDTH), BF16),
                   jax.ShapeDtypeStruct((t, A_WIDTH), F32), jax.ShapeDtypeStruct((t, A_WIDTH), BF16),
                   jax.ShapeDtypeStruct((t, A_WIDTH), F32)],
        compiler_params=_params("arbitrary", "arbitrary"),
    )(qend, ka, va, qb, dob)


HG_ROWS = 256


def _hg_gates(hb_ref, rows, lbv):
    qb = hb_ref[rows, 0:B_WIDTH]
    fb = hb_ref[rows, B_WIDTH:2 * B_WIDTH]
    v = hb_ref[rows, 2 * B_WIDTH:3 * B_WIDTH]
    gb = hb_ref[rows, 3 * B_WIDTH:4 * B_WIDTH]
    sg = _sigmoid(fb)
    f = lbv + (1.0 - lbv) * sg
    sq = _sigmoid(qb)
    return qb, sq, qb * sq, sg, f, 1.0 - f, jnp.log(f), v, gb


def _hg_intra_factors(q, k, b):
    fac = []
    for i in range(CHUNK // SUB):
        bi = b[SUB * i:SUB * i + 1, :]
        eq = jnp.exp(b[SUB * i:SUB * (i + 1), :] - bi)
        ek = jnp.exp(jnp.minimum(bi - b, EXP_CLAMP))
        fac.append((eq, ek, q[SUB * i:SUB * (i + 1), :] * eq, k * ek))
    return fac


def _causal(n):
    r = lax.broadcasted_iota(jnp.int32, (n, n), 0)
    c = lax.broadcasted_iota(jnp.int32, (n, n), 1)
    return r >= c


def _hgrn_fwd(rest, lb, ng, name, ride=()):
    t = rest.shape[0]
    bt = min(HG_ROWS, t)
    ncb = bt // CHUNK
    n = len(ride)
    nsteps = t // bt

    def body(hb_ref, lb_ref, ng_ref, *refs):
        ride_in, (y_ref, o_ref, st_ref), ride_out = refs[:n], refs[n:n + 3], refs[n + 3:2 * n + 3]
        s_scr, sems = refs[2 * n + 3], refs[2 * n + 4:]

        @pl.when(pl.program_id(0) == 0)
        def _():
            s_scr[...] = jnp.zeros_like(s_scr)
            if n:
                _gather_start(ride_in, ride_out, sems)

        tril = _tri(CHUNK, True)
        causal = _causal(CHUNK)
        ones = jnp.ones((CHUNK, HD), F32)

        def chunk(c, carry):
            rows = pl.ds(pl.multiple_of(c * CHUNK, CHUNK), CHUNK)
            _, _, q_all, _, _, k_all, g_all, v_all, gb_all = _hg_gates(hb_ref, rows, lb_ref[...])
            b_all = _dot_hi(tril, g_all)
            qd_all = q_all * jnp.exp(b_all)
            kd_all = k_all * jnp.exp(b_all[CHUNK - 1:CHUNK, :] - b_all)
            eb_all = jnp.exp(_dot_hi(g_all, ones, "tn", exact="b"))
            sgb_all = _sigmoid(gb_all)
            for h in range(B_HEADS):
                cols = slice(h * HD, (h + 1) * HD)
                v = v_all[:, cols]
                s0 = s_scr[h]
                st_ref[c, h] = s0
                o = _dot(qd_all[:, cols], s0)
                fac = _hg_intra_factors(q_all[:, cols], k_all[:, cols], b_all[:, cols])
                a = jnp.concatenate([_dot(qe, ke, "nt") for _, _, qe, ke in fac], axis=0)
                o = o + _dot(jnp.where(causal, a, 0.0), v)
                s_scr[h] = eb_all[h * HD:(h + 1) * HD, :] * s0 + _dot(kd_all[:, cols], v, "tn")
                r = lax.rsqrt(jnp.mean(o * o, axis=-1, keepdims=True) + RMS_EPS)
                o_ref[rows, cols] = o
                y_ref[rows, cols] = (o * r * ng_ref[...] * sgb_all[:, cols]).astype(BF16)
            return carry

        lax.fori_loop(0, ncb, chunk, 0, unroll=2)

        if n:
            @pl.when(pl.program_id(0) == nsteps - 1)
            def _():
                _gather_finish(ride_in, ride_out, sems)

    res = pl.pallas_call(
        body,
        name=name,
        grid=(nsteps,),
        in_specs=[pl.BlockSpec((bt, 4 * B_WIDTH), lambda i: (i, 1)), pl.BlockSpec((1, B_WIDTH), lambda i: (0, 0)),
                  pl.BlockSpec((1, HD), lambda i: (0, 0))] + [_ANY] * n,
        out_specs=[pl.BlockSpec((bt, B_WIDTH), lambda i: (i, 0)), pl.BlockSpec((bt, B_WIDTH), lambda i: (i, 0)),
                   pl.BlockSpec((ncb, B_HEADS, HD, HD), lambda i: (i, 0, 0, 0))] + [_ANY] * n,
        out_shape=[jax.ShapeDtypeStruct((t, B_WIDTH), BF16), jax.ShapeDtypeStruct((t, B_WIDTH), F32),
                   jax.ShapeDtypeStruct((t // CHUNK, B_HEADS, HD, HD), F32)] + _gather_out_shapes(ride),
        scratch_shapes=[pltpu.VMEM((B_HEADS, HD, HD), F32)] + (_gather_scratch(n) if n else []),
        compiler_params=_params("arbitrary"),
    )(rest, lb, ng, *ride)
    return res[:3], res[3:]


def _hgrn_bwd(dy, rest, o_saved, states, lb, ng, name):
    t = rest.shape[0]
    bt = min(HG_ROWS, t)
    ncb = bt // CHUNK
    nb = t // bt

    def body(dy_ref, hb_ref, o_ref, st_ref, lb_ref, ng_ref, dh_ref, dlb_ref, dng_ref, ds_scr):
        @pl.when(pl.program_id(0) == 0)
        def _():
            ds_scr[...] = jnp.zeros_like(ds_scr)
            dlb_ref[...] = jnp.zeros_like(dlb_ref)
            dng_ref[...] = jnp.zeros_like(dng_ref)

        tril = _tri(CHUNK, True)
        triu = _tri(CHUNK, False)
        causal = _causal(CHUNK)
        ones = jnp.ones((CHUNK, HD), F32)
        ones8 = jnp.ones((8, HD), F32)
        last_row = lax.broadcasted_iota(jnp.int32, (CHUNK, B_WIDTH), 0) == CHUNK - 1

        def chunk(cc, carry):
            dng_acc, dlb_acc = carry
            c = ncb - 1 - cc
            rows = pl.ds(pl.multiple_of(c * CHUNK, CHUNK), CHUNK)
            lbv = lb_ref[...]
            qb, sq, q_all, sg, f, k_all, g_all, v_all, gb = _hg_gates(hb_ref, rows, lbv)
            b_all = _dot_hi(tril, g_all)
            ebt_all = jnp.exp(b_all)
            blast = b_all[CHUNK - 1:CHUNK, :]
            ekd_all = jnp.exp(blast - b_all)
            eb_all = jnp.exp(_dot_hi(g_all, ones, "tn", exact="b"))
            sgb = _sigmoid(gb)
            dy_all = dy_ref[rows, :].astype(F32)
            don_all = dy_all * sgb
            ngv = ng_ref[...]
            dq_l, dk_l, dks_l, dv_l, on_l, prod_l = [], [], [], [], [], []
            for h in range(B_HEADS):
                cols = slice(h * HD, (h + 1) * HD)
                q, k, v = q_all[:, cols], k_all[:, cols], v_all[:, cols]
                o = o_ref[rows, cols]
                don = don_all[:, cols]
                r = lax.rsqrt(jnp.mean(o * o, axis=-1, keepdims=True) + RMS_EPS)
                on_l.append(o * r * ngv)
                dng_acc = dng_acc + jnp.sum(don * o * r, axis=0, keepdims=True)
                doh = don * ngv
                do = r * (doh - o * (r * r) * jnp.mean(doh * o, axis=-1, keepdims=True))
                ebt, ekd = ebt_all[:, cols], ekd_all[:, cols]
                s0 = st_ref[c, h]
                ds1 = ds_scr[h]
                fac = _hg_intra_factors(q, k, b_all[:, cols])
                a = jnp.concatenate([_dot(qe, ke, "nt") for _, _, qe, ke in fac], axis=0)
                a = jnp.where(causal, a, 0.0)
                da = jnp.where(causal, _dot(do, v, "nt"), 0.0)
                dv_l.append(_dot(a, do, "tn") + _dot(k * ekd, ds1))
                dq = ebt * _dot(do, s0, "nt")
                dq_l.append(dq + jnp.concatenate(
                    [eq * _hdot(da[SUB * i:SUB * (i + 1), :], ke) for i, (eq, _, _, ke) in enumerate(fac)], axis=0))
                dk_state = ekd * _dot(v, ds1, "nt")
                dk = dk_state
                for i, (_, ek, qe, _) in enumerate(fac):
                    dk = dk + ek * _hdot(da[SUB * i:SUB * (i + 1), :], qe, "tn")
                dk_l.append(dk)
                dks_l.append(dk_state)
                prod_l.append(ds1 * s0)
                ds_scr[h] = _dot(q * ebt, do, "tn") + eb_all[h * HD:(h + 1) * HD, :] * ds1
            dq_all, dk_all = jnp.concatenate(dq_l, axis=1), jnp.concatenate(dk_l, axis=1)
            extra = jnp.exp(blast) * _dot_hi(ones8, jnp.concatenate(prod_l, axis=0), "nt")[0:1, :] \
                + jnp.sum(k_all * jnp.concatenate(dks_l, axis=1), axis=0, keepdims=True)
            db = q_all * dq_all - k_all * dk_all + jnp.where(last_row, extra, 0.0)
            df = _dot_hi(triu, db) / f - dk_all
            dlb_acc = dlb_acc + jnp.sum(df * (1.0 - sg), axis=0, keepdims=True)
            dh_ref[rows, 0:B_WIDTH] = (dq_all * (sq * (1.0 + qb * (1.0 - sq)))).astype(BF16)
            dh_ref[rows, B_WIDTH:2 * B_WIDTH] = (df * (1.0 - lbv) * sg * (1.0 - sg)).astype(BF16)
            dh_ref[rows, 2 * B_WIDTH:3 * B_WIDTH] = jnp.concatenate(dv_l, axis=1).astype(BF16)
            dh_ref[rows, 3 * B_WIDTH:4 * B_WIDTH] = (dy_all * jnp.concatenate(on_l, axis=1)
                                                     * sgb * (1.0 - sgb)).astype(BF16)
            return dng_acc, dlb_acc

        dng_sum, dlb_sum = lax.fori_loop(0, ncb, chunk, (jnp.zeros((1, HD), F32), jnp.zeros((1, B_WIDTH), F32)))
        dng_ref[...] += dng_sum
        dlb_ref[...] += dlb_sum

    rev = lambda i: (nb - 1 - i, 0)
    return pl.pallas_call(
        body,
        name=name,
        grid=(nb,),
        in_specs=[pl.BlockSpec((bt, B_WIDTH), rev), pl.BlockSpec((bt, 4 * B_WIDTH), lambda i: (nb - 1 - i, 1)),
                  pl.BlockSpec((bt, B_WIDTH), rev),
                  pl.BlockSpec((ncb, B_HEADS, HD, HD), lambda i: (nb - 1 - i, 0, 0, 0)),
                  pl.BlockSpec((1, B_WIDTH), lambda i: (0, 0)), pl.BlockSpec((1, HD), lambda i: (0, 0))],
        out_specs=[pl.BlockSpec((bt, 4 * B_WIDTH), rev), pl.BlockSpec((1, B_WIDTH), lambda i: (0, 0)),
                   pl.BlockSpec((1, HD), lambda i: (0, 0))],
        out_shape=[jax.ShapeDtypeStruct((t, 4 * B_WIDTH), BF16), jax.ShapeDtypeStruct((1, B_WIDTH), F32),
                   jax.ShapeDtypeStruct((1, HD), F32)],
        scratch_shapes=[pltpu.VMEM((B_HEADS, HD, HD), F32)],
        compiler_params=_params("arbitrary"),
    )(dy, rest, o_saved, states, lb, ng)


def _axpy2(c0, a0, c1, a1, name, tm=512):
    t, d = a0.shape
    tm = min(tm, t)

    def body(a_ref, b_ref, o_ref):
        o_ref[...] = c0 * a_ref[...] + c1 * b_ref[...]

    row = lambda i: (i, 0)
    return pl.pallas_call(
        body, name=name, grid=(t // tm,),
        in_specs=[pl.BlockSpec((tm, d), row), pl.BlockSpec((tm, d), row)],
        out_specs=pl.BlockSpec((tm, d), row),
        out_shape=jax.ShapeDtypeStruct((t, d), F32),
        compiler_params=_params("parallel"),
    )(a0, a1)


def _split_w_in(w_in_l):
    wqkv = w_in_l[:, :3 * A_WIDTH]
    wfa = jnp.pad(w_in_l[:, 3 * A_WIDTH:3 * A_WIDTH + A_HEADS], ((0, 0), (0, 128 - A_HEADS)))
    whb = w_in_l[:, 3 * A_WIDTH + A_HEADS:3 * A_WIDTH + A_HEADS + 4 * B_WIDTH]
    wgt = w_in_l[:, 3 * A_WIDTH + A_HEADS + 4 * B_WIDTH:]
    return wqkv, jnp.concatenate([wgt, whb, wfa], axis=1)


def _merge_w_in_grad(dwall):
    o = 3 * A_WIDTH
    return jnp.concatenate([dwall[:, :o], dwall[:, o + 4096:o + 4096 + A_HEADS], dwall[:, o + 2048:o + 4096],
                            dwall[:, o:o + 2048]], axis=1)


def _layer_fwd(x, xb, w, sp, l, ride=(), late_weights=None):
    t = x.shape[0]
    n = f"l{l}_"
    wqkv, wrest = _split_w_in(w["w_in"])
    qkv = _matmul(xb, wqkv, "nn", BF16, MM_ROWS, 768, D_MODEL, n + "proj_qkv")
    rest = _matmul(xb, wrest, "nn", F32, MM_ROWS, 1408, D_MODEL, n + "proj_rest")
    bf = jnp.pad(sp["b_fgate"], (0, 128 - A_HEADS)).reshape(1, 128)
    fcum, fcol = _fox_gate_fwd(rest, bf, n + "fox_gate_fwd")
    qa, ka, va, qn, kn = _fox_prep_fwd(qkv, fcol, n + "fox_prep_fwd")
    kstart, qend = _fox_block_ranges(qn, kn, fcum)
    ya, lse = _fox_fwd(qa, ka, va, kstart, n + "fox_fwd")
    lb = sp["lb"].reshape(1, B_WIDTH)
    ng = sp["norm_g"].reshape(1, HD)
    (yb, ob, states), gathered = _hgrn_fwd(rest, lb, ng, n + "hgrn_fwd", ride)
    if ride:
        late = late_weights(gathered)
        w = {**w, **late[l]}
    merged = _merge_fwd(ya, yb, w["w_pa"], w["w_pb"], rest, n + "merge_fwd")
    x1, x1b, xh1, rs1 = _mm_res_ln(merged, w["w_out"], x, sp["ln1_g"], sp["ln1_b"], n + "out_ln1")
    wu, wg = w["w_ff_in"][:, :FFN_HIDDEN], w["w_ff_in"][:, FFN_HIDDEN:]
    a, hu, hg = _ffn_in_swiglu(x1b, wu, wg, n + "ffn_in_swiglu")
    x2, x2b, xh2, rs2 = _mm_res_ln(a, w["w_ff_out"], x1, sp["ln2_g"], sp["ln2_b"], n + "ffn_out_ln2")
    saved = dict(xb=xb, wqkv=wqkv, wrest=wrest, qkv=qkv, rest=rest, bf=bf, fcol=fcol, ka=ka, va=va, ya=ya, lse=lse,
                 kstart=kstart, qend=qend,
                 lb=lb, ng=ng, yb=yb, ob=ob, states=states, merged=merged, x1b=x1b, xh1=xh1, rs1=rs1, a=a,
                 wu=wu, wg=wg, hu=hu, hg=hg,
                 xh2=xh2, rs2=rs2)
    return x2, x2b, saved, (late if ride else None)


def _layer_bwd(dys, coefs, w, sp, s, l):
    n = f"l{l}_"
    dz2, dz2b, dg2, db2 = _ln_bwd(dys, coefs, s["xh2"], s["rs2"], sp["ln2_g"], n + "ln2_bwd")
    du, dg = _ffn_out_dx_swiglu(dz2b, w["w_ff_out"], s["hu"], s["hg"], n + "ffn_out_dx_swiglu")
    d_wffout = _matmul(s["a"], dz2b, "tn", F32, 1408, 1024, DW_ROWS, n + "ffn_out_dw")
    dx1u = _matmul(du, s["wu"], "nt", F32, MM_ROWS, 1024, FFN_HIDDEN, n + "ffn_in_dx_u")
    dx1g = _matmul(dg, s["wg"], "nt", F32, MM_ROWS, 1024, FFN_HIDDEN, n + "ffn_in_dx_g")
    d_wffin = jnp.concatenate([_matmul(s["x1b"], du, "tn", F32, 1024, 1408, DW_ROWS, n + "ffn_in_dw_u"),
                               _matmul(s["x1b"], dg, "tn", F32, 1024, 1408, DW_ROWS, n + "ffn_in_dw_g")], axis=1)
    dz1, dz1b, dg1, db1 = _ln_bwd([dz2, dx1u, dx1g], [ALPHA, 1.0, 1.0], s["xh1"], s["rs1"], sp["ln1_g"],
                                  n + "ln1_bwd")
    d_wout = _matmul(s["merged"], dz1b, "tn", F32, 1024, 1024, DW_ROWS, n + "out_dw")
    dgates, dpa, dpb, dya, dyb = _merge_bwd(dz1b, w["w_out"], s["ya"], s["yb"], w["w_pa"], w["w_pb"], s["rest"],
                                  n + "merge_bwd")
    d_wpa = _matmul(s["ya"], dpa, "tn", F32, 512, 1024, DW_ROWS, n + "pa_dw")
    d_wpb = _matmul(s["yb"], dpb, "tn", F32, 512, 1024, DW_ROWS, n + "pb_dw")
    qb, dob = _fox_prep_bwd(s["qkv"], s["fcol"], s["lse"], dya, s["ya"], n + "fox_prep_bwd")
    dk, dv, csum, dq, rsum = _fox_bwd(qb, s["ka"], s["va"], dob, s["qend"], n + "fox_bwd")
    dfa, dbf = _fox_gate_bwd(rsum, csum, s["rest"], s["bf"], n + "fox_gate_bwd")
    dhb, dlb, dng = _hgrn_bwd(dyb, s["rest"], s["ob"], s["states"], s["lb"], s["ng"], n + "hgrn_bwd")
    dproj = jnp.concatenate([dq, dk, dv, dgates, dhb, dfa], axis=1)
    wall = jnp.concatenate([s["wqkv"], s["wrest"]], axis=1)
    dxp = _matmul(dproj, wall, "nt", F32, MM_ROWS, 1024, 1920, n + "proj_dx")
    d_wall = _matmul(s["xb"], dproj, "tn", F32, 1024, 1152, DW_ROWS, n + "proj_dw")
    grads = dict(w_in=_merge_w_in_grad(d_wall), w_pa=d_wpa, w_pb=d_wpb, w_out=d_wout, w_ff_in=d_wffin,
                 w_ff_out=d_wffout, b_fgate=dbf[0, :A_HEADS], lb=dlb[0], norm_g=dng[0], ln1_g=dg1[0], ln1_b=db1[0],
                 ln2_g=dg2[0], ln2_b=db2[0])
    return [dz1, dxp], [ALPHA, 1.0], grads


def _lower_bounds(logits):
    sm = jax.nn.softmax(logits.astype(F32), axis=0)
    return jnp.cumsum(sm, axis=0) - sm[0:1]


def _local_step(x, target, wfull, small, ride=(), late_weights=None):
    lbs, lb_vjp = jax.vjp(_lower_bounds, small["hgrn_lb_logits"])
    h, hb = x, x.astype(BF16)
    wfull = list(wfull)
    saved, sps = [], []
    for l in range(DEPTH):
        sp = dict(b_fgate=small["b_fgate"][l], lb=lbs[l], norm_g=small["hgrn_norm_g"][l], ln1_g=small["ln1_g"][l],
                  ln1_b=small["ln1_b"][l], ln2_g=small["ln2_g"][l], ln2_b=small["ln2_b"][l])
        h, hb, s, late = _layer_fwd(h, hb, wfull[l], sp, l, ride if l == 0 else (), late_weights)
        if late is not None:
            wfull = [{**wfull[k], **late[k]} for k in range(DEPTH)]
        saved.append(s)
        sps.append(sp)
    dy, lpart = _loss_head(h, target)
    dys, coefs = [dy], [1.0]
    grads = [None] * DEPTH
    for l in reversed(range(DEPTH)):
        dys, coefs, grads[l] = _layer_bwd(dys, coefs, wfull[l], sps[l], saved[l], l)
    grad_x = _axpy2(coefs[0], dys[0], coefs[1], dys[1], "grad_x")
    d_logits = lb_vjp(jnp.stack([grads[l]["lb"] for l in range(DEPTH)]))[0]
    return lpart[0, 0], grad_x, grads, d_logits


_BIG = [("w_in", "w_in", (D_MODEL, IN_TOTAL), 1), ("w_branch_a", "w_pa", (A_WIDTH, D_MODEL), 1),
        ("w_branch_b", "w_pb", (B_WIDTH, D_MODEL), 1), ("w_out", "w_out", (D_MODEL, D_MODEL), 0),
        ("w_ff_in", "w_ff_in", (D_MODEL, 2 * FFN_HIDDEN), 1), ("w_ff_out", "w_ff_out", (FFN_HIDDEN, D_MODEL), 0)]
_SMALL = [("b_fgate", A_HEADS), ("hgrn_lb_logits", B_WIDTH), ("hgrn_norm_g", HD), ("ln1_g", D_MODEL),
          ("ln1_b", D_MODEL), ("ln2_g", D_MODEL), ("ln2_b", D_MODEL)]
N_BIG = len(_BIG)
SMALL_ROWS = 80


def _by_chip(full, axis):
    if axis == 0:
        return full.reshape(N_CHIPS, full.shape[0] // N_CHIPS, full.shape[1])
    n = full.shape[1] // N_CHIPS
    return jnp.stack([full[:, q * n:(q + 1) * n] for q in range(N_CHIPS)])


def _from_chips(shards, axis):
    if axis == 0:
        return shards.reshape(N_CHIPS * shards.shape[1], shards.shape[2])
    return jnp.concatenate([shards[q] for q in range(N_CHIPS)], axis=1)


def _pack_small(per_name):
    flat = jnp.concatenate([per_name[name].reshape(-1) for name, _ in _SMALL])
    return jnp.pad(flat, (0, SMALL_ROWS * 128 - flat.shape[0])).reshape(SMALL_ROWS, 128)


def _unpack_small(slab):
    flat, out, r = slab.reshape(-1), {}, 0
    for name, n in _SMALL:
        out[name] = flat[r:r + DEPTH * n].reshape(DEPTH, n)
        r += DEPTH * n
    return out


_ANY = pl.BlockSpec(memory_space=pl.ANY)


def _place():
    return lax.axis_index("x"), lax.axis_index("y"), lax.axis_index("c")


def _other_chips(x, y):
    return [(1 - x, y), (x, 1 - y), (1 - x, 1 - y)]


def _chip_exchange(mine_of, out_refs, send_sems, recv_sems, local_sems):
    _chip_exchange_start(mine_of, out_refs, send_sems, recv_sems, local_sems)
    _chip_exchange_wait(mine_of, out_refs, send_sems, recv_sems, local_sems)


def _chip_exchange_copies(mine_of, out_refs, send_sems, recv_sems, local_sems):
    x, y, c = _place()
    q = 2 * x + y
    local = [pltpu.make_async_copy(mine_of(w, q), out_ref.at[q], local_sems.at[w]) for w, out_ref in enumerate(out_refs)]
    sends, recvs = [], []
    for k, (px, py) in enumerate(_other_chips(x, y)):
        for w, out_ref in enumerate(out_refs):
            sems = dict(send_sem=send_sems.at[3 * w + k], recv_sem=recv_sems.at[3 * w + k], device_id=(px, py, c),
                        device_id_type=MESH)
            sends.append(pltpu.make_async_remote_copy(src_ref=mine_of(w, 2 * px + py), dst_ref=out_ref.at[q], **sems))
            recvs.append(pltpu.make_async_remote_copy(src_ref=mine_of(w, q), dst_ref=out_ref.at[2 * px + py], **sems))
    return local, sends, recvs


def _chip_exchange_start(*args):
    local, sends, _ = _chip_exchange_copies(*args)
    for cp in local + sends:
        cp.start()


def _chip_exchange_wait(*args):
    local, sends, recvs = _chip_exchange_copies(*args)
    for cp in recvs:
        cp.wait_recv()
    for cp in sends:
        cp.wait_send()
    for cp in local:
        cp.wait()


def _sem_scratch(n):
    return [pltpu.SemaphoreType.DMA((3 * n,)), pltpu.SemaphoreType.DMA((3 * n,)), pltpu.SemaphoreType.DMA((n,))]


def _gather_scratch(n):
    return _sem_scratch(n) + [pltpu.SemaphoreType.DMA((n,)), pltpu.SemaphoreType.DMA((n,))]


def _gather_out_shapes(mine):
    return [jax.ShapeDtypeStruct((DEPTH, N_CHIPS) + m.shape[1:], m.dtype) for m in mine]


def _gather_start(in_refs, out_refs, sems):
    c = lax.axis_index("c")
    _chip_exchange_start(lambda w, q: in_refs[w].at[c], [o.at[c] for o in out_refs], *sems[:3])


def _gather_finish(in_refs, out_refs, sems):
    x, y, c = _place()
    _chip_exchange_wait(lambda w, q: in_refs[w].at[c], [o.at[c] for o in out_refs], *sems[:3])
    pair_send, pair_recv = sems[3:]
    sibling = (x, y, 1 - c)
    fwds = []
    for w, o in enumerate(out_refs):
        cp = pltpu.make_async_remote_copy(src_ref=o.at[c], dst_ref=o.at[c], send_sem=pair_send.at[w],
                                          recv_sem=pair_recv.at[w], device_id=sibling, device_id_type=MESH)
        cp.start()
        fwds.append(cp)
    for w, o in enumerate(out_refs):
        pltpu.make_async_remote_copy(src_ref=o.at[1 - c], dst_ref=o.at[1 - c], send_sem=pair_send.at[w],
                                     recv_sem=pair_recv.at[w], device_id=sibling, device_id_type=MESH).wait_recv()
    for cp in fwds:
        cp.wait_send()


def _gather_weights(mine):
    n = len(mine)

    def body(*refs):
        in_refs, out_refs, sems = refs[:n], refs[n:2 * n], refs[2 * n:]
        _gather_start(in_refs, out_refs, sems)
        _gather_finish(in_refs, out_refs, sems)

    return pl.pallas_call(
        body, name="gather_weights", in_specs=[_ANY] * n, out_specs=[_ANY] * n,
        out_shape=_gather_out_shapes(mine), scratch_shapes=_gather_scratch(n),
    )(*mine)


def _pair_exchange(gs):
    n = len(gs)

    def body(*refs):
        g_refs, a_refs, send_sems, recv_sems = refs[:n], refs[n:2 * n], refs[2 * n], refs[2 * n + 1]
        x, y, c = _place()
        cps = []
        for w in range(n):
            cp = pltpu.make_async_remote_copy(src_ref=g_refs[w].at[1 - c], dst_ref=a_refs[w], send_sem=send_sems.at[w],
                                              recv_sem=recv_sems.at[w], device_id=(x, y, 1 - c), device_id_type=MESH)
            cp.start()
            cps.append(cp)
        for cp in cps:
            cp.wait()

    return pl.pallas_call(
        body, name="grad_pair_exchange", in_specs=[_ANY] * n, out_specs=[_ANY] * n,
        out_shape=[jax.ShapeDtypeStruct(g.shape[1:], g.dtype) for g in gs],
        scratch_shapes=[pltpu.SemaphoreType.DMA((n,)), pltpu.SemaphoreType.DMA((n,))],
    )(*gs)


def _row_block(rows):
    return math.gcd(rows, 256)


def _pair_sum(g, a, layer, name):
    _, nq, rows, cols = g.shape
    tb = _row_block(rows)

    def body(l_ref, g_ref, a_ref, o_ref):
        o_ref[...] = (g_ref[...] + a_ref[...]).astype(BF16)

    return pl.pallas_call(
        body, name=name,
        grid_spec=pltpu.PrefetchScalarGridSpec(
            num_scalar_prefetch=1, grid=(nq, rows // tb),
            in_specs=[pl.BlockSpec((None, None, tb, cols), lambda q, i, l_ref: (l_ref[0], q, i, 0)),
                      pl.BlockSpec((None, tb, cols), lambda q, i, l_ref: (q, i, 0))],
            out_specs=pl.BlockSpec((None, tb, cols), lambda q, i, l_ref: (q, i, 0))),
        out_shape=jax.ShapeDtypeStruct((nq, rows, cols), BF16),
        compiler_params=_params("parallel", "parallel"),
    )(layer.reshape(1).astype(jnp.int32), g, a)


def _shard_exchange(ps):
    n = len(ps)

    def body(*refs):
        p_refs, b_refs = refs[:n], refs[n:2 * n]
        send_sems, recv_sems, local_sems = refs[2 * n:]
        _chip_exchange(lambda w, q: p_refs[w].at[q], b_refs, send_sems, recv_sems, local_sems)

    return pl.pallas_call(
        body, name="grad_shard_exchange", in_specs=[_ANY] * n, out_specs=[_ANY] * n,
        out_shape=[jax.ShapeDtypeStruct(p.shape, p.dtype) for p in ps],
        scratch_shapes=_sem_scratch(n),
    )(*ps)


def _sum4(b, name):
    _, rows, cols = b.shape
    tb = _row_block(rows)

    def body(b_ref, o_ref):
        o_ref[...] = ((b_ref[0].astype(F32) + b_ref[1].astype(F32)) + b_ref[2].astype(F32)) + b_ref[3].astype(F32)

    return pl.pallas_call(
        body, name=name, grid=(rows // tb,),
        in_specs=[pl.BlockSpec((N_CHIPS, tb, cols), lambda i: (0, i, 0))],
        out_specs=pl.BlockSpec((tb, cols), lambda i: (i, 0)),
        out_shape=jax.ShapeDtypeStruct((rows, cols), F32),
        compiler_params=_params("parallel"),
    )(b)


def _result_exchange(gcs):
    n = len(gcs)

    def body(*refs):
        g_refs, o_refs, send_sems, recv_sems = refs[:n], refs[n:2 * n], refs[2 * n], refs[2 * n + 1]
        x, y, c = _place()
        cps = []
        for w in range(n):
            cp = pltpu.make_async_remote_copy(src_ref=g_refs[w], dst_ref=o_refs[w], send_sem=send_sems.at[w],
                                              recv_sem=recv_sems.at[w], device_id=(x, y, 1 - c), device_id_type=MESH)
            cp.start()
            cps.append(cp)
        for cp in cps:
            cp.wait()

    return pl.pallas_call(
        body, name="grad_result_exchange", in_specs=[_ANY] * n, out_specs=[_ANY] * n,
        out_shape=[jax.ShapeDtypeStruct(g.shape, g.dtype) for g in gcs],
        scratch_shapes=[pltpu.SemaphoreType.DMA((n,)), pltpu.SemaphoreType.DMA((n,))],
    )(*gcs)


def _allreduce_small(v):
    def body(v_ref, o_ref, buf, send_sems, recv_sems):
        x, y, c = _place()
        me = 4 * x + 2 * y + c
        buf[me] = v_ref[...]
        peers = []
        for k in range(1, N_DEV):
            px = 1 - x if k & 4 else x
            py = 1 - y if k & 2 else y
            pc = 1 - c if k & 1 else c
            peers.append((px, py, pc))
        sends = []
        for k, peer in enumerate(peers):
            cp = pltpu.make_async_remote_copy(src_ref=v_ref, dst_ref=buf.at[me], send_sem=send_sems.at[k],
                                              recv_sem=recv_sems.at[k], device_id=peer, device_id_type=MESH)
            cp.start()
            sends.append(cp)
        for k, (px, py, pc) in enumerate(peers):
            pltpu.make_async_remote_copy(src_ref=v_ref, dst_ref=buf.at[4 * px + 2 * py + pc], send_sem=send_sems.at[k],
                                         recv_sem=recv_sems.at[k], device_id=(px, py, pc),
                                         device_id_type=MESH).wait_recv()
        for cp in sends:
            cp.wait_send()
        acc = buf[0]
        for i in range(1, N_DEV):
            acc = acc + buf[i]
        o_ref[...] = acc

    vm = pl.BlockSpec(memory_space=pltpu.VMEM)
    return pl.pallas_call(
        body, name="small_allreduce", in_specs=[vm], out_specs=vm,
        out_shape=jax.ShapeDtypeStruct(v.shape, F32),
        scratch_shapes=[pltpu.VMEM((N_DEV,) + v.shape, F32), pltpu.SemaphoreType.DMA((N_DEV - 1,)),
                        pltpu.SemaphoreType.DMA((N_DEV - 1,))],
    )(v)


def _adam_update(w, g, m, v):
    nm = ADAM_B1 * m + (1.0 - ADAM_B1) * g
    nv = ADAM_B2 * v + (1.0 - ADAM_B2) * (g * g)
    m_hat = nm / (1.0 - ADAM_B1 ** ADAM_STEP)
    v_hat = nv / (1.0 - ADAM_B2 ** ADAM_STEP)
    return -ADAM_LR * (m_hat / (jnp.sqrt(v_hat) + ADAM_EPS) + ADAM_WD * w), nm, nv


def _adamw_small(w, g, m, v, name):
    def body(w_ref, g_ref, m_ref, v_ref, d_ref, nm_ref, nv_ref):
        d_ref[...], nm_ref[...], nv_ref[...] = _adam_update(w_ref[...], g_ref[...], m_ref[...], v_ref[...])

    vm = pl.BlockSpec(memory_space=pltpu.VMEM)
    return pl.pallas_call(
        body, name=name, in_specs=[vm] * 4, out_specs=[vm] * 3,
        out_shape=[jax.ShapeDtypeStruct(w.shape, F32)] * 3,
    )(w, g, m, v)


def _adamw_big(w, m, v, g_own, g_other, layer, name):
    _, rows, cols = w.shape
    tb = _row_block(rows)

    def body(l_ref, w_ref, m_ref, v_ref, go_ref, gx_ref, g_ref, d_ref, nm_ref, nv_ref):
        gv = jnp.where(pl.program_id(0) == l_ref[0], go_ref[...], gx_ref[...])
        g_ref[...] = gv
        d_ref[...], nm_ref[...], nv_ref[...] = _adam_update(w_ref[...], gv, m_ref[...], v_ref[...])

    per_layer = pl.BlockSpec((None, tb, cols), lambda l, i, l_ref: (l, i, 0))
    shared = pl.BlockSpec((tb, cols), lambda l, i, l_ref: (i, 0))
    return pl.pallas_call(
        body, name=name,
        grid_spec=pltpu.PrefetchScalarGridSpec(
            num_scalar_prefetch=1, grid=(DEPTH, rows // tb),
            in_specs=[per_layer, per_layer, per_layer, shared, shared], out_specs=[per_layer] * 4),
        out_shape=[jax.ShapeDtypeStruct(w.shape, F32)] * 4,
        compiler_params=_params("parallel", "parallel"),
    )(layer.reshape(1).astype(jnp.int32), w, m, v, g_own, g_other)


def kernel(x, w_in, b_fgate, hgrn_lb_logits, hgrn_norm_g, w_branch_a, w_branch_b, w_out, ln1_g, ln1_b, w_ff_in, w_ff_out, ln2_g, ln2_b, loss_target, m_w_in, m_b_fgate, m_hgrn_lb_logits, m_hgrn_norm_g, m_w_branch_a, m_w_branch_b, m_w_out, m_ln1_g, m_ln1_b, m_w_ff_in, m_w_ff_out, m_ln2_g, m_ln2_b, v_w_in, v_b_fgate, v_hgrn_lb_logits, v_hgrn_norm_g, v_w_branch_a, v_w_branch_b, v_w_out, v_ln1_g, v_ln1_b, v_w_ff_in, v_w_ff_out, v_ln2_g, v_ln2_b):
    weights = dict(w_in=w_in, b_fgate=b_fgate, hgrn_lb_logits=hgrn_lb_logits, hgrn_norm_g=hgrn_norm_g,
                   w_branch_a=w_branch_a, w_branch_b=w_branch_b, w_out=w_out, ln1_g=ln1_g, ln1_b=ln1_b,
                   w_ff_in=w_ff_in, w_ff_out=w_ff_out, ln2_g=ln2_g, ln2_b=ln2_b)
    mom1 = dict(w_in=m_w_in, b_fgate=m_b_fgate, hgrn_lb_logits=m_hgrn_lb_logits, hgrn_norm_g=m_hgrn_norm_g,
                w_branch_a=m_w_branch_a, w_branch_b=m_w_branch_b, w_out=m_w_out, ln1_g=m_ln1_g, ln1_b=m_ln1_b,
                w_ff_in=m_w_ff_in, w_ff_out=m_w_ff_out, ln2_g=m_ln2_g, ln2_b=m_ln2_b)
    mom2 = dict(w_in=v_w_in, b_fgate=v_b_fgate, hgrn_lb_logits=v_hgrn_lb_logits, hgrn_norm_g=v_hgrn_norm_g,
                w_branch_a=v_w_branch_a, w_branch_b=v_w_branch_b, w_out=v_w_out, ln1_g=v_ln1_g, ln1_b=v_ln1_b,
                w_ff_in=v_w_ff_in, w_ff_out=v_w_ff_out, ln2_g=v_ln2_g, ln2_b=v_ln2_b)
    core = lax.axis_index("c")

    def full_weights(entries, gathered):
        return [{key: _from_chips(gathered[w][l], axis) for w, (_, key, _, axis) in enumerate(entries)}
                for l in range(DEPTH)]

    wfull = full_weights(_BIG[:1], _gather_weights([weights["w_in"].astype(BF16)]))
    ride = [weights[name].astype(BF16) for name, _, _, _ in _BIG[1:]]
    small = {name: weights[name] for name, _ in _SMALL}

    loss_part, grad_x, grads, d_logits = _local_step(x[0], loss_target[0], wfull, small, ride,
                                                     functools.partial(full_weights, _BIG[1:]))

    g_all = [jnp.stack([_by_chip(grads[l][key], axis) for l in range(DEPTH)]) for _, key, _, axis in _BIG]
    received = _pair_exchange(g_all)
    pair = [_pair_sum(g_all[w], received[w], core, f"grad_pair_sum_{w}") for w in range(N_BIG)]
    by_chip = _shard_exchange(pair)
    g_layer = [_sum4(by_chip[w], f"grad_chip_sum_{w}") for w in range(N_BIG)]
    g_other = _result_exchange(g_layer)
    out_g, out_d, out_m, out_v = {}, {}, {}, {}
    for w, (name, _, _, _) in enumerate(_BIG):
        out_g[name], out_d[name], out_m[name], out_v[name] = _adamw_big(
            weights[name], mom1[name], mom2[name], g_layer[w], g_other[w], core, f"adamw_{name}")

    small_grads = {name: jnp.stack([grads[l][key] for l in range(DEPTH)])
                   for name, key in [("b_fgate", "b_fgate"), ("hgrn_norm_g", "norm_g"), ("ln1_g", "ln1_g"),
                                     ("ln1_b", "ln1_b"), ("ln2_g", "ln2_g"), ("ln2_b", "ln2_b")]}
    small_grads["hgrn_lb_logits"] = d_logits
    gs = _allreduce_small(_pack_small(small_grads))
    ds, ms, vs = _adamw_small(_pack_small(small), gs, _pack_small({n: mom1[n] for n, _ in _SMALL}),
                              _pack_small({n: mom2[n] for n, _ in _SMALL}), "adamw_small")
    for tree, slab in ((out_g, gs), (out_d, ds), (out_m, ms), (out_v, vs)):
        tree.update(_unpack_small(slab))

    loss = lax.psum(loss_part, ("x", "y", "c"))
    order = ["w_in", "b_fgate", "hgrn_lb_logits", "hgrn_norm_g", "w_branch_a", "w_branch_b", "w_out", "ln1_g", "ln1_b",
             "w_ff_in", "w_ff_out", "ln2_g", "ln2_b"]
    return (loss, grad_x[None], *[out_g[n] for n in order], *[out_d[n] for n in order],
            *[out_m[n] for n in order], *[out_v[n] for n in order])
```

```python
import math

import jax
import jax.numpy as jnp
from jax import lax
from jax.experimental import pallas as pl
from jax.experimental.pallas import tpu as pltpu

F32 = jnp.float32
BF16 = jnp.bfloat16

D_MODEL = 1024
DEPTH = 2
A_HEADS = 8
A_WIDTH = 512
B_WIDTH = 512
B_HEADS = 4
HD = 128
CHUNK = 64
SUB = 16
FFN_HIDDEN = 2816
IN_TOTAL = 5640
ALPHA = (2 * DEPTH) ** 0.25
LN_EPS = 1e-5
RMS_EPS = 1e-6
ADAM_LR = 0.001
ADAM_B1 = 0.9
ADAM_B2 = 0.999
ADAM_EPS = 1e-08
ADAM_WD = 0.01
ADAM_STEP = 10
EXP_CLAMP = 60.0

VMEM_LIMIT_BYTES = 56 * 1024 * 1024
MM_ROWS = 1024
DW_ROWS = 2048
N_CHIPS = 4
N_DEV = 8
MESH = pl.DeviceIdType.MESH

_DN = {
    "nn": (((1,), (0,)), ((), ())),
    "nt": (((1,), (1,)), ((), ())),
    "tn": (((0,), (0,)), ((), ())),
}


def _dot(a, b, mode="nn"):
    return lax.dot_general(a.astype(BF16), b.astype(BF16), _DN[mode], preferred_element_type=F32)


def _pieces(x):
    h = x.astype(BF16)
    r = x - h.astype(F32)
    m = r.astype(BF16)
    return h, m, (r - m.astype(F32)).astype(BF16)


def _dot_hi(a, b, mode="nn", exact="a"):
    if exact == "a":
        h, m, l = _pieces(b)
        return (_dot(a, l, mode) + _dot(a, m, mode)) + _dot(a, h, mode)
    h, m, l = _pieces(a)
    return (_dot(l, b, mode) + _dot(m, b, mode)) + _dot(h, b, mode)


def _hdot(a, b, mode="nn"):
    bh, bl, _ = _pieces(b)
    return _dot(a, bl, mode) + _dot(a, bh, mode)


def _params(*sem):
    return pltpu.CompilerParams(dimension_semantics=sem, vmem_limit_bytes=VMEM_LIMIT_BYTES)


def _sigmoid(x):
    return 1.0 / (1.0 + jnp.exp(-x))


def _matmul(a, b, mode, out_dtype, tm, tn, tk, name):
    if mode == "nn":
        (m, k), (k2, n) = a.shape, b.shape
    elif mode == "nt":
        (m, k), (n, k2) = a.shape, b.shape
    else:
        (k, m), (k2, n) = a.shape, b.shape
    assert k == k2, (a.shape, b.shape, mode)
    tm, tn, tk = min(tm, m), min(tn, n), min(tk, k)
    assert m % tm == 0 and n % tn == 0 and k % tk == 0, (a.shape, b.shape, tm, tn, tk)
    nk = k // tk
    if mode == "tn":
        a_spec = pl.BlockSpec((tk, tm), lambda j, i, kk: (kk, i))
    else:
        a_spec = pl.BlockSpec((tm, tk), lambda j, i, kk: (i, kk))
    if mode == "nt":
        b_spec = pl.BlockSpec((tn, tk), lambda j, i, kk: (j, kk))
    else:
        b_spec = pl.BlockSpec((tk, tn), lambda j, i, kk: (kk, j))
    use_acc = nk > 1 and out_dtype != F32

    def body(a_ref, b_ref, o_ref, *scratch):
        p = _dot(a_ref[...], b_ref[...], mode)
        if nk == 1:
            o_ref[...] = p.astype(out_dtype)
            return
        acc_ref = scratch[0] if use_acc else o_ref
        kk = pl.program_id(2)

        @pl.when(kk == 0)
        def _():
            acc_ref[...] = p

        @pl.when(kk > 0)
        def _():
            acc_ref[...] += p

        if use_acc:
            @pl.when(kk == nk - 1)
            def _():
                o_ref[...] = acc_ref[...].astype(out_dtype)

    return pl.pallas_call(
        body,
        name=name,
        grid=(n // tn, m // tm, nk),
        in_specs=[a_spec, b_spec],
        out_specs=pl.BlockSpec((tm, tn), lambda j, i, kk: (i, j)),
        out_shape=jax.ShapeDtypeStruct((m, n), out_dtype),
        scratch_shapes=[pltpu.VMEM((tm, tn), F32)] if use_acc else [],
        compiler_params=_params("parallel", "parallel", "arbitrary"),
    )(a, b)


def _mm_res_ln(a, w, res, g, b, name, tm=512):
    t, k = a.shape
    d = w.shape[1]
    tm = min(tm, t)

    def body(a_ref, w_ref, r_ref, g_ref, b_ref, y_ref, yb_ref, xh_ref, rs_ref):
        z = ALPHA * r_ref[...] + _dot(a_ref[...], w_ref[...])
        mu = jnp.mean(z, axis=-1, keepdims=True)
        zc = z - mu
        var = jnp.mean(zc * zc, axis=-1, keepdims=True)
        rstd = lax.rsqrt(var + LN_EPS)
        xh = zc * rstd
        y = xh * g_ref[...] + b_ref[...]
        y_ref[...] = y
        yb_ref[...] = y.astype(BF16)
        xh_ref[...] = xh
        rs_ref[...] = rstd

    row = lambda i: (i, 0)
    fix = lambda i: (0, 0)
    return pl.pallas_call(
        body,
        name=name,
        grid=(t // tm,),
        in_specs=[pl.BlockSpec((tm, k), row), pl.BlockSpec((k, d), fix), pl.BlockSpec((tm, d), row),
                  pl.BlockSpec((1, d), fix), pl.BlockSpec((1, d), fix)],
        out_specs=[pl.BlockSpec((tm, d), row), pl.BlockSpec((tm, d), row), pl.BlockSpec((tm, d), row),
                   pl.BlockSpec((tm, 1), row)],
        out_shape=[jax.ShapeDtypeStruct((t, d), F32), jax.ShapeDtypeStruct((t, d), BF16),
                   jax.ShapeDtypeStruct((t, d), F32), jax.ShapeDtypeStruct((t, 1), F32)],
        compiler_params=_params("parallel"),
    )(a, w, res, g.reshape(1, d), b.reshape(1, d))


def _ln_bwd(dys, coefs, xhat, rstd, g, name, tm=512):
    t, d = xhat.shape
    tm = min(tm, t)
    n_in = len(dys)

    def body(*refs):
        dy_refs = refs[:n_in]
        xh_ref, rs_ref, g_ref, dz_ref, dzb_ref, dg_ref, db_ref = refs[n_in:]
        dy = coefs[0] * dy_refs[0][...].astype(F32)
        for c, r in zip(coefs[1:], dy_refs[1:]):
            dy = dy + c * r[...].astype(F32)
        xh = xh_ref[...]
        dxh = dy * g_ref[...]
        m1 = jnp.mean(dxh, axis=-1, keepdims=True)
        m2 = jnp.mean(dxh * xh, axis=-1, keepdims=True)
        dz = rs_ref[...] * (dxh - m1 - xh * m2)
        dz_ref[...] = dz
        dzb_ref[...] = dz.astype(BF16)
        pg = jnp.sum(dy * xh, axis=0, keepdims=True)
        pb = jnp.sum(dy, axis=0, keepdims=True)

        @pl.when(pl.program_id(0) == 0)
        def _():
            dg_ref[...] = pg
            db_ref[...] = pb

        @pl.when(pl.program_id(0) > 0)
        def _():
            dg_ref[...] += pg
            db_ref[...] += pb

    row = lambda i: (i, 0)
    fix = lambda i: (0, 0)
    return pl.pallas_call(
        body,
        name=name,
        grid=(t // tm,),
        in_specs=[pl.BlockSpec((tm, d), row)] * n_in
        + [pl.BlockSpec((tm, d), row), pl.BlockSpec((tm, 1), row), pl.BlockSpec((1, d), fix)],
        out_specs=[pl.BlockSpec((tm, d), row), pl.BlockSpec((tm, d), row), pl.BlockSpec((1, d), fix),
                   pl.BlockSpec((1, d), fix)],
        out_shape=[jax.ShapeDtypeStruct((t, d), F32), jax.ShapeDtypeStruct((t, d), BF16),
                   jax.ShapeDtypeStruct((1, d), F32), jax.ShapeDtypeStruct((1, d), F32)],
        compiler_params=_params("arbitrary"),
    )(*dys, xhat, rstd, g.reshape(1, d))


def _loss_head(y, target, name="loss_head", tm=512):
    t, d = y.shape
    tm = min(tm, t)

    def body(y_ref, t_ref, dy_ref, l_ref):
        e = y_ref[...] - t_ref[...]
        dy_ref[...] = e * (1.0 / d)
        part = jnp.full((8, 128), 0.5 / d, F32) * jnp.sum(e * e)

        @pl.when(pl.program_id(0) == 0)
        def _():
            l_ref[...] = part

        @pl.when(pl.program_id(0) > 0)
        def _():
            l_ref[...] += part

    row = lambda i: (i, 0)
    return pl.pallas_call(
        body,
        name=name,
        grid=(t // tm,),
        in_specs=[pl.BlockSpec((tm, d), row), pl.BlockSpec((tm, d), row)],
        out_specs=[pl.BlockSpec((tm, d), row), pl.BlockSpec((8, 128), lambda i: (0, 0))],
        out_shape=[jax.ShapeDtypeStruct((t, d), F32), jax.ShapeDtypeStruct((8, 128), F32)],
        compiler_params=_params("arbitrary"),
    )(y, target)


FFN_COLS = FFN_HIDDEN // 2


def _ffn_in_swiglu(xb, wu, wg, name, tm=512):
    t, d = xb.shape
    tm = min(tm, t)

    def body(x_ref, wu_ref, wg_ref, a_ref, u_ref, g_ref):
        x = x_ref[...]
        u = _dot(x, wu_ref[...])
        g = _dot(x, wg_ref[...])
        u_ref[...] = u
        g_ref[...] = g
        a_ref[...] = (g * _sigmoid(g) * u).astype(BF16)

    wspec = pl.BlockSpec((d, FFN_COLS), lambda j, i: (0, j))
    out = pl.BlockSpec((tm, FFN_COLS), lambda j, i: (i, j))
    return pl.pallas_call(
        body,
        name=name,
        grid=(FFN_HIDDEN // FFN_COLS, t // tm),
        in_specs=[pl.BlockSpec((tm, d), lambda j, i: (i, 0)), wspec, wspec],
        out_specs=[out, out, out],
        out_shape=[jax.ShapeDtypeStruct((t, FFN_HIDDEN), BF16), jax.ShapeDtypeStruct((t, FFN_HIDDEN), F32),
                   jax.ShapeDtypeStruct((t, FFN_HIDDEN), F32)],
        compiler_params=_params("parallel", "parallel"),
    )(xb, wu, wg)


def _ffn_out_dx_swiglu(dzb, w_ff_out, u, g, name, tm=512):
    t, d = dzb.shape
    tm = min(tm, t)

    def body(dz_ref, w_ref, u_ref, g_ref, du_ref, dg_ref):
        da = _dot(dz_ref[...], w_ref[...], "nt")
        gv = g_ref[...]
        sg = _sigmoid(gv)
        du_ref[...] = (da * gv * sg).astype(BF16)
        dg_ref[...] = (da * u_ref[...] * (sg * (1.0 + gv * (1.0 - sg)))).astype(BF16)

    blk = pl.BlockSpec((tm, FFN_COLS), lambda j, i: (i, j))
    return pl.pallas_call(
        body,
        name=name,
        grid=(FFN_HIDDEN // FFN_COLS, t // tm),
        in_specs=[pl.BlockSpec((tm, d), lambda j, i: (i, 0)), pl.BlockSpec((FFN_COLS, d), lambda j, i: (j, 0)), blk, blk],
        out_specs=[blk, blk],
        out_shape=[jax.ShapeDtypeStruct((t, FFN_HIDDEN), BF16)] * 2,
        compiler_params=_params("parallel", "parallel"),
    )(dzb, w_ff_out, u, g)


def _merge_fwd(ya, yb, wpa, wpb, rest, name, tm=512):
    t = ya.shape[0]
    tm = min(tm, t)

    def body(ya_ref, yb_ref, wa_ref, wb_ref, ga_ref, gb_ref, o_ref):
        pa = _dot(ya_ref[...], wa_ref[...])
        pb = _dot(yb_ref[...], wb_ref[...])
        o_ref[...] = (_sigmoid(ga_ref[...]) * pa + _sigmoid(gb_ref[...]) * pb).astype(BF16)

    row = lambda i: (i, 0)
    fix = lambda i: (0, 0)
    return pl.pallas_call(
        body,
        name=name,
        grid=(t // tm,),
        in_specs=[pl.BlockSpec((tm, A_WIDTH), row), pl.BlockSpec((tm, B_WIDTH), row),
                  pl.BlockSpec((A_WIDTH, D_MODEL), fix), pl.BlockSpec((B_WIDTH, D_MODEL), fix),
                  pl.BlockSpec((tm, D_MODEL), lambda i: (i, 0)), pl.BlockSpec((tm, D_MODEL), lambda i: (i, 1))],
        out_specs=pl.BlockSpec((tm, D_MODEL), row),
        out_shape=jax.ShapeDtypeStruct((t, D_MODEL), BF16),
        compiler_params=_params("parallel"),
    )(ya, yb, wpa, wpb, rest, rest)


def _merge_bwd(dzb, w_out, ya, yb, wpa, wpb, rest, name, tm=512):
    t = ya.shape[0]
    tm = min(tm, t)

    def body(dz_ref, wo_ref, ya_ref, yb_ref, wa_ref, wb_ref, ga_ref, gb_ref, dg_ref, dpa_ref, dpb_ref, dya_ref,
             dyb_ref):
        dm_v = _dot(dz_ref[...], wo_ref[...], "nt")
        pa = _dot(ya_ref[...], wa_ref[...])
        pb = _dot(yb_ref[...], wb_ref[...])
        sa = _sigmoid(ga_ref[...])
        sb = _sigmoid(gb_ref[...])
        dg_ref[:, :D_MODEL] = (dm_v * pa * sa * (1.0 - sa)).astype(BF16)
        dg_ref[:, D_MODEL:] = (dm_v * pb * sb * (1.0 - sb)).astype(BF16)
        dpa = (dm_v * sa).astype(BF16)
        dpb = (dm_v * sb).astype(BF16)
        dpa_ref[...] = dpa
        dpb_ref[...] = dpb
        dya_ref[...] = _dot(dpa, wa_ref[...], "nt").astype(BF16)
        dyb_ref[...] = _dot(dpb, wb_ref[...], "nt")

    row = lambda i: (i, 0)
    fix = lambda i: (0, 0)
    return pl.pallas_call(
        body,
        name=name,
        grid=(t // tm,),
        in_specs=[pl.BlockSpec((tm, D_MODEL), row), pl.BlockSpec((D_MODEL, D_MODEL), fix),
                  pl.BlockSpec((tm, A_WIDTH), row), pl.BlockSpec((tm, B_WIDTH), row),
                  pl.BlockSpec((A_WIDTH, D_MODEL), fix), pl.BlockSpec((B_WIDTH, D_MODEL), fix),
                  pl.BlockSpec((tm, D_MODEL), lambda i: (i, 0)), pl.BlockSpec((tm, D_MODEL), lambda i: (i, 1))],
        out_specs=[pl.BlockSpec((tm, 2 * D_MODEL), row), pl.BlockSpec((tm, D_MODEL), row),
                   pl.BlockSpec((tm, D_MODEL), row), pl.BlockSpec((tm, A_WIDTH), row), pl.BlockSpec((tm, B_WIDTH), row)],
        out_shape=[jax.ShapeDtypeStruct((t, 2 * D_MODEL), BF16), jax.ShapeDtypeStruct((t, D_MODEL), BF16),
                   jax.ShapeDtypeStruct((t, D_MODEL), BF16), jax.ShapeDtypeStruct((t, A_WIDTH), BF16),
                   jax.ShapeDtypeStruct((t, B_WIDTH), F32)],
        compiler_params=_params("parallel"),
    )(dzb, w_out, ya, yb, wpa, wpb, rest, rest)


FA_BLOCK = 4224 // 128 - 1


def _tri(n, lower):
    r = lax.broadcasted_iota(jnp.int32, (n, n), 0)
    c = lax.broadcasted_iota(jnp.int32, (n, n), 1)
    return jnp.where((r >= c) if lower else (r <= c), 1.0, 0.0).astype(F32)


def _head_spread(expand):
    shape = (128, A_WIDTH) if expand else (A_WIDTH, 128)
    r = lax.broadcasted_iota(jnp.int32, shape, 0)
    c = lax.broadcasted_iota(jnp.int32, shape, 1)
    hit = ((c >= 64 * r) & (c < 64 * r + 64)) if expand else (r == 64 * c)
    return jnp.where(hit, 1.0, 0.0).astype(F32)


def _fox_gate_fwd(rest, bf, name, tb=512):
    t = rest.shape[0]
    tb = min(tb, t)

    def body(fa_ref, bf_ref, f_ref, fc_ref, carry):
        @pl.when(pl.program_id(0) == 0)
        def _():
            carry[...] = jnp.zeros_like(carry)

        z = fa_ref[...] + bf_ref[...]
        logf = jnp.minimum(z, 0.0) - jnp.log(1.0 + jnp.exp(-jnp.abs(z)))
        f = _dot_hi(_tri(tb, True), logf) + carry[...]
        f_ref[...] = f
        fc_ref[...] = _dot_hi(f, _head_spread(True), exact="b")
        carry[...] = f[tb - 1:tb, :]

    return pl.pallas_call(
        body,
        name=name,
        grid=(t // tb,),
        in_specs=[pl.BlockSpec((tb, 128), lambda i: (i, FA_BLOCK)), pl.BlockSpec((1, 128), lambda i: (0, 0))],
        out_specs=[pl.BlockSpec((tb, 128), lambda i: (i, 0)), pl.BlockSpec((tb, A_WIDTH), lambda i: (i, 0))],
        out_shape=[jax.ShapeDtypeStruct((t, 128), F32), jax.ShapeDtypeStruct((t, A_WIDTH), F32)],
        scratch_shapes=[pltpu.VMEM((1, 128), F32)],
        compiler_params=_params("arbitrary"),
    )(rest, bf)


def _fox_gate_bwd(rsum, csum, rest, bf, name, tb=512):
    t = rest.shape[0]
    tb = min(tb, t)
    nb = t // tb

    def body(rs_ref, cs_ref, fa_ref, bf_ref, dfa_ref, dbf_ref, carry):
        @pl.when(pl.program_id(0) == 0)
        def _():
            carry[...] = jnp.zeros_like(carry)

        d_f = _dot_hi(rs_ref[...] - cs_ref[...], _head_spread(False), exact="b")
        dlogf = _dot_hi(_tri(tb, False), d_f) + carry[...]
        carry[...] = dlogf[0:1, :]
        z = fa_ref[...] + bf_ref[...]
        dz = dlogf * _sigmoid(-z)
        dfa_ref[...] = dz.astype(BF16)
        part = jnp.sum(dz, axis=0, keepdims=True)

        @pl.when(pl.program_id(0) == 0)
        def _():
            dbf_ref[...] = part

        @pl.when(pl.program_id(0) > 0)
        def _():
            dbf_ref[...] += part

    return pl.pallas_call(
        body,
        name=name,
        grid=(nb,),
        in_specs=[pl.BlockSpec((tb, A_WIDTH), lambda i: (nb - 1 - i, 0)),
                  pl.BlockSpec((tb, A_WIDTH), lambda i: (nb - 1 - i, 0)),
                  pl.BlockSpec((tb, 128), lambda i: (nb - 1 - i, FA_BLOCK)),
                  pl.BlockSpec((1, 128), lambda i: (0, 0))],
        out_specs=[pl.BlockSpec((tb, 128), lambda i: (nb - 1 - i, 0)), pl.BlockSpec((1, 128), lambda i: (0, 0))],
        out_shape=[jax.ShapeDtypeStruct((t, 128), BF16), jax.ShapeDtypeStruct((1, 128), F32)],
        scratch_shapes=[pltpu.VMEM((1, 128), F32)],
        compiler_params=_params("arbitrary"),
    )(rsum, csum, rest, bf)


ATT_BLOCK = 512


def _head_mask(shape, j):
    lane = lax.broadcasted_iota(jnp.int32, shape, 1)
    return (lane < 64) if j == 0 else (lane >= 64)


def _aug_lanes(tb, j):
    lane = lax.broadcasted_iota(jnp.int32, (tb, 128), 1)
    own = (lane < 64) if j == 0 else (lane >= 64)
    return own, lane - 64 * (1 - j)


def _aug_query(own, li, q, pieces):
    h, m, l = pieces
    one, zero = jnp.ones_like(h), jnp.zeros_like(h)
    spare = jnp.where(li == 0, h, jnp.where(li == 1, m, jnp.where(li == 2, l, jnp.where(li < 6, one, zero))))
    return jnp.where(own, q, spare)


def _fox_prep_fwd(qkv, fcol, name, tb=2048):
    t = qkv.shape[0]
    tb = min(tb, t)

    def body(q_ref, k_ref, v_ref, fc_ref, qa_ref, ka_ref, va_ref, qn_ref, kn_ref):
        pieces = _pieces(pltpu.roll(fc_ref[...], 64, 1))
        h, m, l = pieces
        q, k, v = q_ref[...], k_ref[...], v_ref[...]
        first = _head_mask((tb, 128), 0)
        for nrm_ref, x in ((qn_ref, q.astype(F32)), (kn_ref, k.astype(F32))):
            n0 = jnp.max(jnp.sum(jnp.where(first, x * x, 0.0), axis=1, keepdims=True))
            n1 = jnp.max(jnp.sum(jnp.where(first, 0.0, x * x), axis=1, keepdims=True))
            nrm_ref[...] = jnp.where(_head_mask((8, 128), 0), n0, n1)
        one, zero = jnp.ones_like(h), jnp.zeros_like(h)
        for j in (0, 1):
            own, li = _aug_lanes(tb, j)
            cols = slice(128 * j, 128 * (j + 1))
            qa_ref[:, cols] = _aug_query(own, li, q * 0.125, pieces)
            ks = jnp.where(li < 3, one, jnp.where(li == 3, -h, jnp.where(li == 4, -m, jnp.where(li == 5, -l, zero))))
            ka_ref[:, cols] = jnp.where(own, k, ks)
            va_ref[:, cols] = jnp.where(own, v, one)

    blk = pl.BlockSpec((tb, 256), lambda i, h: (i, h))
    nrm = pl.BlockSpec((None, None, 8, 128), lambda i, h: (i, h, 0, 0))
    return pl.pallas_call(
        body, name=name, grid=(t // tb, 4),
        in_specs=[pl.BlockSpec((tb, 128), lambda i, h: (i, h)), pl.BlockSpec((tb, 128), lambda i, h: (i, 4 + h)),
                  pl.BlockSpec((tb, 128), lambda i, h: (i, 8 + h)), pl.BlockSpec((tb, 128), lambda i, h: (i, h))],
        out_specs=[blk, blk, blk, nrm, nrm],
        out_shape=[jax.ShapeDtypeStruct((t, 2 * A_WIDTH), BF16)] * 3
        + [jax.ShapeDtypeStruct((t // tb, 4, 8, 128), F32)] * 2,
        compiler_params=_params("parallel", "parallel"),
    )(qkv, qkv, qkv, fcol)


def _fox_prep_bwd(qkv, fcol, lse, do, o, name, tb=2048):
    t = qkv.shape[0]
    tb = min(tb, t)

    def body(q_ref, fc_ref, lse_ref, do_ref, o_ref, qb_ref, dob_ref):
        pieces = _pieces(pltpu.roll(fc_ref[...] - lse_ref[...], 64, 1))
        q = q_ref[...] * 0.125
        do_v = do_ref[...]
        prod = do_v.astype(F32) * o_ref[...].astype(F32)
        for j in (0, 1):
            own, li = _aug_lanes(tb, j)
            cols = slice(128 * j, 128 * (j + 1))
            qb_ref[:, cols] = _aug_query(own, li, q, pieces)
            delta = jnp.sum(jnp.where(own, prod, 0.0), axis=1, keepdims=True)
            h, m, l = _pieces(jnp.broadcast_to(delta, (tb, 128)))
            ds = jnp.where(li == 0, -h, jnp.where(li == 1, -m, jnp.where(li == 2, -l, jnp.zeros_like(h))))
            dob_ref[:, cols] = jnp.where(own, do_v, ds)

    pair = pl.BlockSpec((tb, 128), lambda i, h: (i, h))
    blk = pl.BlockSpec((tb, 256), lambda i, h: (i, h))
    return pl.pallas_call(
        body, name=name, grid=(t // tb, 4),
        in_specs=[pair, pair, pair, pair, pair],
        out_specs=[blk, blk],
        out_shape=[jax.ShapeDtypeStruct((t, 2 * A_WIDTH), BF16)] * 2,
        compiler_params=_params("parallel", "parallel"),
    )(qkv, fcol, lse, do, o)


def _tile_mask(n, transposed):
    r = lax.broadcasted_iota(jnp.int32, (n, n), 0)
    c = lax.broadcasted_iota(jnp.int32, (n, n), 1)
    return (c >= r) if transposed else (r >= c)


UNDERFLOW = -110.0


def _fox_block_ranges(qn, kn, fcum):
    t = fcum.shape[0]
    blk = min(ATT_BLOCK, t)
    nb = t // blk
    q2 = jnp.max(qn[:, :, 0, ::64].reshape(-1, A_HEADS), axis=0)
    k2 = jnp.max(kn[:, :, 0, ::64].reshape(-1, A_HEADS), axis=0)
    bound = 2.0 * jnp.sqrt(q2 * k2) * 0.125
    f = fcum[:, :A_HEADS]
    first = f[0::blk].T
    last = f[blk - 1::blk].T
    dead = (bound[:, None, None] + first[:, :, None] - last[:, None, :]) < UNDERFLOW
    qi = jnp.arange(nb)[None, :, None]
    kj = jnp.arange(nb)[None, None, :]
    dead = dead & (kj < qi)
    kstart = jnp.sum(dead, axis=2).astype(jnp.int32)
    qend = (kj[0] + jnp.sum((~dead) & (qi > kj), axis=1)).astype(jnp.int32)
    return kstart.reshape(-1), qend.reshape(-1)


def _fox_fwd(qa, ka, va, kstart, name):
    t = qa.shape[0]
    bq = min(ATT_BLOCK, t)
    nq = t // bq

    def body(ks_ref, q_ref, k_ref, v_ref, o_ref, lse_ref):
        i = pl.program_id(1)
        hp = pl.program_id(0)
        k0 = [ks_ref[(2 * hp + j) * nq + i] for j in (0, 1)]
        both0 = jnp.maximum(k0[0], k0[1])

        def head(j, kb, m, acc, masked):
            rows = pl.ds(pl.multiple_of(kb * bq, bq), bq)
            cols = slice(128 * j, 128 * (j + 1))
            s = _dot(q_ref[:, cols], k_ref[rows, cols], "nt")
            if masked:
                s = jnp.where(_tile_mask(bq, False), s, -jnp.inf)
            m_new = jnp.maximum(m, jnp.max(s, axis=1, keepdims=True))
            return m_new, jnp.exp(m - m_new) * acc + _dot(jnp.exp(s - m_new), v_ref[rows, cols])

        def pair(kb, carry, masked):
            return head(0, kb, carry[0], carry[1], masked) + head(1, kb, carry[2], carry[3], masked)

        init = (jnp.full((bq, 1), -jnp.inf, F32), jnp.zeros((bq, 128), F32))
        alone = [lax.fori_loop(k0[j], both0, lambda kb, c, j=j: head(j, kb, c[0], c[1], False), init) for j in (0, 1)]
        carry = lax.fori_loop(both0, i, lambda kb, c: pair(kb, c, False), alone[0] + alone[1])
        carry = pair(i, carry, True)
        outs = []
        for j in (0, 1):
            m, acc = carry[2 * j], carry[2 * j + 1]
            spare = 64 * (1 - j)
            l = acc[:, spare:spare + 1]
            outs.append((acc / l, m + jnp.log(l)))
        msk = _head_mask((bq, 128), 0)
        o_ref[...] = jnp.where(msk, outs[0][0], outs[1][0]).astype(BF16)
        lse_ref[...] = jnp.where(msk, outs[0][1], outs[1][1])

    res = pl.BlockSpec((t, 256), lambda h, i, tbl: (0, h))
    out = pl.BlockSpec((bq, 128), lambda h, i, tbl: (i, h))
    return pl.pallas_call(
        body,
        name=name,
        grid_spec=pltpu.PrefetchScalarGridSpec(
            num_scalar_prefetch=1, grid=(4, nq),
            in_specs=[pl.BlockSpec((bq, 256), lambda h, i, tbl: (i, h)), res, res],
            out_specs=[out, out]),
        out_shape=[jax.ShapeDtypeStruct((t, A_WIDTH), BF16), jax.ShapeDtypeStruct((t, A_WIDTH), F32)],
        compiler_params=_params("parallel", "parallel"),
    )(kstart, qa, ka, va)


def _fox_bwd(qb, ka, va, dob, qend, name):
    t = qb.shape[0]
    bk = min(ATT_BLOCK, t)
    nk = t // bk

    def body(qe_ref, k_ref, v_ref, q_hbm, do_hbm, dk_ref, dv_ref, cs_ref, dq_hbm, rs_hbm, q_scr, do_scr, dq_scr,
             sems):
        jb = pl.program_id(1)
        hp = pl.program_id(0)
        pair_cols = pl.ds(pl.multiple_of(hp * 256, 256), 256)

        @pl.when(jb == 0)
        def _():
            loads = [pltpu.make_async_copy(q_hbm.at[:, pair_cols], q_scr, sems.at[0]),
                     pltpu.make_async_copy(do_hbm.at[:, pair_cols], do_scr, sems.at[1])]
            for cp in loads:
                cp.start()
            dq_scr[...] = jnp.zeros_like(dq_scr)
            for cp in loads:
                cp.wait()

        i1 = [qe_ref[(2 * hp + j) * nk + jb] + 1 for j in (0, 1)]
        both1 = jnp.minimum(i1[0], i1[1])

        def head(j, ib, dk_acc, dv_acc, masked):
            rows = pl.ds(pl.multiple_of(ib * bk, bk), bk)
            cols = slice(128 * j, 128 * (j + 1))
            qs = q_scr[rows, cols]
            dos = do_scr[rows, cols]
            kj = k_ref[:, cols]
            st = _dot(kj, qs, "nt")
            if masked:
                st = jnp.where(_tile_mask(bk, True), st, -jnp.inf)
            pt = jnp.exp(st)
            dst = (pt * _dot(v_ref[:, cols], dos, "nt")).astype(BF16)
            dq_scr[rows, cols] += _dot(dst, kj, "tn")
            return dk_acc + _dot(dst, qs), dv_acc + _dot(pt, dos)

        def pair(ib, carry, masked):
            return head(0, ib, carry[0], carry[1], masked) + head(1, ib, carry[2], carry[3], masked)

        carry = pair(jb, (jnp.zeros((bk, 128), F32),) * 4, True)
        carry = lax.fori_loop(jb + 1, both1, lambda ib, c: pair(ib, c, False), carry)
        alone = [lax.fori_loop(jnp.maximum(both1, jb + 1), i1[j],
                               lambda ib, c, j=j: head(j, ib, c[0], c[1], False), carry[2 * j:2 * j + 2])
                 for j in (0, 1)]
        carry = alone[0] + alone[1]
        outs = []
        for j in (0, 1):
            spare = 64 * (1 - j)
            dk_acc, dv_acc = carry[2 * j], carry[2 * j + 1]
            outs.append((dk_acc, dv_acc, dk_acc[:, spare + 3:spare + 4]))
        msk = _head_mask((bk, 128), 0)
        dk_ref[...] = jnp.where(msk, outs[0][0], outs[1][0]).astype(BF16)
        dv_ref[...] = jnp.where(msk, outs[0][1], outs[1][1]).astype(BF16)
        cs_ref[...] = jnp.where(msk, outs[0][2], outs[1][2])

        @pl.when(jb == nk - 1)
        def _():
            def finish(r, carry):
                rows = pl.ds(pl.multiple_of(r * bk, bk), bk)
                x0, x1 = dq_scr[rows, 0:128], dq_scr[rows, 128:256]
                q_scr[rows, 0:128] = (jnp.where(msk, x0, x1) * 0.125).astype(BF16)
                dq_scr[rows, 0:128] = jnp.where(msk, x0[:, 64:65], x1[:, 0:1])
                return carry

            lax.fori_loop(0, nk, finish, 0)
            head_cols = pl.ds(pl.multiple_of(hp * 128, 128), 128)
            stores = [pltpu.make_async_copy(q_scr.at[:, 0:128], dq_hbm.at[:, head_cols], sems.at[0]),
                      pltpu.make_async_copy(dq_scr.at[:, 0:128], rs_hbm.at[:, head_cols], sems.at[1])]
            for cp in stores:
                cp.start()
            for cp in stores:
                cp.wait()

    blk = pl.BlockSpec((bk, 256), lambda h, i, tbl: (i, h))
    out = pl.BlockSpec((bk, 128), lambda h, i, tbl: (i, h))
    return pl.pallas_call(
        body,
        name=name,
        grid_spec=pltpu.PrefetchScalarGridSpec(
            num_scalar_prefetch=1, grid=(4, nk), in_specs=[blk, blk, _ANY, _ANY],
            out_specs=[out, out, out, _ANY, _ANY],
            scratch_shapes=[pltpu.VMEM((t, 256), BF16), pltpu.VMEM((t, 256), BF16), pltpu.VMEM((t, 256), F32),
                            pltpu.SemaphoreType.DMA((2,))]),
        out_shape=[jax.ShapeDtypeStruct((t, A_WIDTH), BF16), jax.ShapeDtypeStruct((t, A_WIDTH), BF16),
                   jax.ShapeDtypeStruct((t, A_WIDTH), F32), jax.ShapeDtypeStruct((t, A_WIDTH), BF16),
                   jax.ShapeDtypeStruct((t, A_WIDTH), F32)],
        compiler_params=_params("arbitrary", "arbitrary"),
    )(qend, ka, va, qb, dob)


HG_ROWS = 256


def _hg_gates(hb_ref, rows, lbv):
    qb = hb_ref[rows, 0:B_WIDTH]
    fb = hb_ref[rows, B_WIDTH:2 * B_WIDTH]
    v = hb_ref[rows, 2 * B_WIDTH:3 * B_WIDTH]
    gb = hb_ref[rows, 3 * B_WIDTH:4 * B_WIDTH]
    sg = _sigmoid(fb)
    f = lbv + (1.0 - lbv) * sg
    sq = _sigmoid(qb)
    return qb, sq, qb * sq, sg, f, 1.0 - f, jnp.log(f), v, gb


def _hg_intra_factors(q, k, b):
    fac = []
    for i in range(CHUNK // SUB):
        bi = b[SUB * i:SUB * i + 1, :]
        eq = jnp.exp(b[SUB * i:SUB * (i + 1), :] - bi)
        ek = jnp.exp(jnp.minimum(bi - b, EXP_CLAMP))
        fac.append((eq, ek, q[SUB * i:SUB * (i + 1), :] * eq, k * ek))
    return fac


def _causal(n):
    r = lax.broadcasted_iota(jnp.int32, (n, n), 0)
    c = lax.broadcasted_iota(jnp.int32, (n, n), 1)
    return r >= c


def _hgrn_fwd(rest, lb, ng, name, ride=()):
    t = rest.shape[0]
    bt = min(HG_ROWS, t)
    ncb = bt // CHUNK
    n = len(ride)
    nsteps = t // bt

    def body(hb_ref, lb_ref, ng_ref, *refs):
        ride_in, (y_ref, o_ref, st_ref), ride_out = refs[:n], refs[n:n + 3], refs[n + 3:2 * n + 3]
        s_scr, sems = refs[2 * n + 3], refs[2 * n + 4:]

        @pl.when(pl.program_id(0) == 0)
        def _():
            s_scr[...] = jnp.zeros_like(s_scr)
            if n:
                _gather_start(ride_in, ride_out, sems)

        tril = _tri(CHUNK, True)
        causal = _causal(CHUNK)
        ones = jnp.ones((CHUNK, HD), F32)

        def chunk(c, carry):
            rows = pl.ds(pl.multiple_of(c * CHUNK, CHUNK), CHUNK)
            _, _, q_all, _, _, k_all, g_all, v_all, gb_all = _hg_gates(hb_ref, rows, lb_ref[...])
            b_all = _dot_hi(tril, g_all)
            qd_all = q_all * jnp.exp(b_all)
            kd_all = k_all * jnp.exp(b_all[CHUNK - 1:CHUNK, :] - b_all)
            eb_all = jnp.exp(_dot_hi(g_all, ones, "tn", exact="b"))
            sgb_all = _sigmoid(gb_all)
            for h in range(B_HEADS):
                cols = slice(h * HD, (h + 1) * HD)
                v = v_all[:, cols]
                s0 = s_scr[h]
                st_ref[c, h] = s0
                o = _dot(qd_all[:, cols], s0)
                fac = _hg_intra_factors(q_all[:, cols], k_all[:, cols], b_all[:, cols])
                a = jnp.concatenate([_dot(qe, ke, "nt") for _, _, qe, ke in fac], axis=0)
                o = o + _dot(jnp.where(causal, a, 0.0), v)
                s_scr[h] = eb_all[h * HD:(h + 1) * HD, :] * s0 + _dot(kd_all[:, cols], v, "tn")
                r = lax.rsqrt(jnp.mean(o * o, axis=-1, keepdims=True) + RMS_EPS)
                o_ref[rows, cols] = o
                y_ref[rows, cols] = (o * r * ng_ref[...] * sgb_all[:, cols]).astype(BF16)
            return carry

        lax.fori_loop(0, ncb, chunk, 0, unroll=2)

        if n:
            @pl.when(pl.program_id(0) == nsteps - 1)
            def _():
                _gather_finish(ride_in, ride_out, sems)

    res = pl.pallas_call(
        body,
        name=name,
        grid=(nsteps,),
        in_specs=[pl.BlockSpec((bt, 4 * B_WIDTH), lambda i: (i, 1)), pl.BlockSpec((1, B_WIDTH), lambda i: (0, 0)),
                  pl.BlockSpec((1, HD), lambda i: (0, 0))] + [_ANY] * n,
        out_specs=[pl.BlockSpec((bt, B_WIDTH), lambda i: (i, 0)), pl.BlockSpec((bt, B_WIDTH), lambda i: (i, 0)),
                   pl.BlockSpec((ncb, B_HEADS, HD, HD), lambda i: (i, 0, 0, 0))] + [_ANY] * n,
        out_shape=[jax.ShapeDtypeStruct((t, B_WIDTH), BF16), jax.ShapeDtypeStruct((t, B_WIDTH), F32),
                   jax.ShapeDtypeStruct((t // CHUNK, B_HEADS, HD, HD), F32)] + _gather_out_shapes(ride),
        scratch_shapes=[pltpu.VMEM((B_HEADS, HD, HD), F32)] + (_gather_scratch(n) if n else []),
        compiler_params=_params("arbitrary"),
    )(rest, lb, ng, *ride)
    return res[:3], res[3:]


def _hgrn_bwd(dy, rest, o_saved, states, lb, ng, name):
    t = rest.shape[0]
    bt = min(HG_ROWS, t)
    ncb = bt // CHUNK
    nb = t // bt

    def body(dy_ref, hb_ref, o_ref, st_ref, lb_ref, ng_ref, dh_ref, dlb_ref, dng_ref, ds_scr):
        @pl.when(pl.program_id(0) == 0)
        def _():
            ds_scr[...] = jnp.zeros_like(ds_scr)
            dlb_ref[...] = jnp.zeros_like(dlb_ref)
            dng_ref[...] = jnp.zeros_like(dng_ref)

        tril = _tri(CHUNK, True)
        triu = _tri(CHUNK, False)
        causal = _causal(CHUNK)
        ones = jnp.ones((CHUNK, HD), F32)
        ones8 = jnp.ones((8, HD), F32)
        last_row = lax.broadcasted_iota(jnp.int32, (CHUNK, B_WIDTH), 0) == CHUNK - 1

        def chunk(cc, carry):
            dng_acc, dlb_acc = carry
            c = ncb - 1 - cc
            rows = pl.ds(pl.multiple_of(c * CHUNK, CHUNK), CHUNK)
            lbv = lb_ref[...]
            qb, sq, q_all, sg, f, k_all, g_all, v_all, gb = _hg_gates(hb_ref, rows, lbv)
            b_all = _dot_hi(tril, g_all)
            ebt_all = jnp.exp(b_all)
            blast = b_all[CHUNK - 1:CHUNK, :]
            ekd_all = jnp.exp(blast - b_all)
            eb_all = jnp.exp(_dot_hi(g_all, ones, "tn", exact="b"))
            sgb = _sigmoid(gb)
            dy_all = dy_ref[rows, :].astype(F32)
            don_all = dy_all * sgb
            ngv = ng_ref[...]
            dq_l, dk_l, dks_l, dv_l, on_l, prod_l = [], [], [], [], [], []
            for h in range(B_HEADS):
                cols = slice(h * HD, (h + 1) * HD)
                q, k, v = q_all[:, cols], k_all[:, cols], v_all[:, cols]
                o = o_ref[rows, cols]
                don = don_all[:, cols]
                r = lax.rsqrt(jnp.mean(o * o, axis=-1, keepdims=True) + RMS_EPS)
                on_l.append(o * r * ngv)
                dng_acc = dng_acc + jnp.sum(don * o * r, axis=0, keepdims=True)
                doh = don * ngv
                do = r * (doh - o * (r * r) * jnp.mean(doh * o, axis=-1, keepdims=True))
                ebt, ekd = ebt_all[:, cols], ekd_all[:, cols]
                s0 = st_ref[c, h]
                ds1 = ds_scr[h]
                fac = _hg_intra_factors(q, k, b_all[:, cols])
                a = jnp.concatenate([_dot(qe, ke, "nt") for _, _, qe, ke in fac], axis=0)
                a = jnp.where(causal, a, 0.0)
                da = jnp.where(causal, _dot(do, v, "nt"), 0.0)
                dv_l.append(_dot(a, do, "tn") + _dot(k * ekd, ds1))
                dq = ebt * _dot(do, s0, "nt")
                dq_l.append(dq + jnp.concatenate(
                    [eq * _hdot(da[SUB * i:SUB * (i + 1), :], ke) for i, (eq, _, _, ke) in enumerate(fac)], axis=0))
                dk_state = ekd * _dot(v, ds1, "nt")
                dk = dk_state
                for i, (_, ek, qe, _) in enumerate(fac):
                    dk = dk + ek * _hdot(da[SUB * i:SUB * (i + 1), :], qe, "tn")
                dk_l.append(dk)
                dks_l.append(dk_state)
                prod_l.append(ds1 * s0)
                ds_scr[h] = _dot(q * ebt, do, "tn") + eb_all[h * HD:(h + 1) * HD, :] * ds1
            dq_all, dk_all = jnp.concatenate(dq_l, axis=1), jnp.concatenate(dk_l, axis=1)
            extra = jnp.exp(blast) * _dot_hi(ones8, jnp.concatenate(prod_l, axis=0), "nt")[0:1, :] \
                + jnp.sum(k_all * jnp.concatenate(dks_l, axis=1), axis=0, keepdims=True)
            db = q_all * dq_all - k_all * dk_all + jnp.where(last_row, extra, 0.0)
            df = _dot_hi(triu, db) / f - dk_all
            dlb_acc = dlb_acc + jnp.sum(df * (1.0 - sg), axis=0, keepdims=True)
            dh_ref[rows, 0:B_WIDTH] = (dq_all * (sq * (1.0 + qb * (1.0 - sq)))).astype(BF16)
            dh_ref[rows, B_WIDTH:2 * B_WIDTH] = (df * (1.0 - lbv) * sg * (1.0 - sg)).astype(BF16)
            dh_ref[rows, 2 * B_WIDTH:3 * B_WIDTH] = jnp.concatenate(dv_l, axis=1).astype(BF16)
            dh_ref[rows, 3 * B_WIDTH:4 * B_WIDTH] = (dy_all * jnp.concatenate(on_l, axis=1)
                                                     * sgb * (1.0 - sgb)).astype(BF16)
            return dng_acc, dlb_acc

        dng_sum, dlb_sum = lax.fori_loop(0, ncb, chunk, (jnp.zeros((1, HD), F32), jnp.zeros((1, B_WIDTH), F32)))
        dng_ref[...] += dng_sum
        dlb_ref[...] += dlb_sum

    rev = lambda i: (nb - 1 - i, 0)
    return pl.pallas_call(
        body,
        name=name,
        grid=(nb,),
        in_specs=[pl.BlockSpec((bt, B_WIDTH), rev), pl.BlockSpec((bt, 4 * B_WIDTH), lambda i: (nb - 1 - i, 1)),
                  pl.BlockSpec((bt, B_WIDTH), rev),
                  pl.BlockSpec((ncb, B_HEADS, HD, HD), lambda i: (nb - 1 - i, 0, 0, 0)),
                  pl.BlockSpec((1, B_WIDTH), lambda i: (0, 0)), pl.BlockSpec((1, HD), lambda i: (0, 0))],
        out_specs=[pl.BlockSpec((bt, 4 * B_WIDTH), rev), pl.BlockSpec((1, B_WIDTH), lambda i: (0, 0)),
                   pl.BlockSpec((1, HD), lambda i: (0, 0))],
        out_shape=[jax.ShapeDtypeStruct((t, 4 * B_WIDTH), BF16), jax.ShapeDtypeStruct((1, B_WIDTH), F32),
                   jax.ShapeDtypeStruct((1, HD), F32)],
        scratch_shapes=[pltpu.VMEM((B_HEADS, HD, HD), F32)],
        compiler_params=_params("arbitrary"),
    )(dy, rest, o_saved, states, lb, ng)


def _axpy2(c0, a0, c1, a1, name, tm=512):
    t, d = a0.shape
    tm = min(tm, t)

    def body(a_ref, b_ref, o_ref):
        o_ref[...] = c0 * a_ref[...] + c1 * b_ref[...]

    row = lambda i: (i, 0)
    return pl.pallas_call(
        body, name=name, grid=(t // tm,),
        in_specs=[pl.BlockSpec((tm, d), row), pl.BlockSpec((tm, d), row)],
        out_specs=pl.BlockSpec((tm, d), row),
        out_shape=jax.ShapeDtypeStruct((t, d), F32),
        compiler_params=_params("parallel"),
    )(a0, a1)


def _split_w_in(w_in_l):
    wqkv = w_in_l[:, :3 * A_WIDTH]
    wfa = jnp.pad(w_in_l[:, 3 * A_WIDTH:3 * A_WIDTH + A_HEADS], ((0, 0), (0, 128 - A_HEADS)))
    whb = w_in_l[:, 3 * A_WIDTH + A_HEADS:3 * A_WIDTH + A_HEADS + 4 * B_WIDTH]
    wgt = w_in_l[:, 3 * A_WIDTH + A_HEADS + 4 * B_WIDTH:]
    return wqkv, jnp.concatenate([wgt, whb, wfa], axis=1)


def _merge_w_in_grad(dwall):
    o = 3 * A_WIDTH
    return jnp.concatenate([dwall[:, :o], dwall[:, o + 4096:o + 4096 + A_HEADS], dwall[:, o + 2048:o + 4096],
                            dwall[:, o:o + 2048]], axis=1)


def _layer_fwd(x, xb, w, sp, l, ride=(), late_weights=None):
    t = x.shape[0]
    n = f"l{l}_"
    wqkv, wrest = _split_w_in(w["w_in"])
    qkv = _matmul(xb, wqkv, "nn", BF16, MM_ROWS, 768, D_MODEL, n + "proj_qkv")
    rest = _matmul(xb, wrest, "nn", F32, MM_ROWS, 1408, D_MODEL, n + "proj_rest")
    bf = jnp.pad(sp["b_fgate"], (0, 128 - A_HEADS)).reshape(1, 128)
    fcum, fcol = _fox_gate_fwd(rest, bf, n + "fox_gate_fwd")
    qa, ka, va, qn, kn = _fox_prep_fwd(qkv, fcol, n + "fox_prep_fwd")
    kstart, qend = _fox_block_ranges(qn, kn, fcum)
    ya, lse = _fox_fwd(qa, ka, va, kstart, n + "fox_fwd")
    lb = sp["lb"].reshape(1, B_WIDTH)
    ng = sp["norm_g"].reshape(1, HD)
    (yb, ob, states), gathered = _hgrn_fwd(rest, lb, ng, n + "hgrn_fwd", ride)
    if ride:
        late = late_weights(gathered)
        w = {**w, **late[l]}
    merged = _merge_fwd(ya, yb, w["w_pa"], w["w_pb"], rest, n + "merge_fwd")
    x1, x1b, xh1, rs1 = _mm_res_ln(merged, w["w_out"], x, sp["ln1_g"], sp["ln1_b"], n + "out_ln1")
    wu, wg = w["w_ff_in"][:, :FFN_HIDDEN], w["w_ff_in"][:, FFN_HIDDEN:]
    a, hu, hg = _ffn_in_swiglu(x1b, wu, wg, n + "ffn_in_swiglu")
    x2, x2b, xh2, rs2 = _mm_res_ln(a, w["w_ff_out"], x1, sp["ln2_g"], sp["ln2_b"], n + "ffn_out_ln2")
    saved = dict(xb=xb, wqkv=wqkv, wrest=wrest, qkv=qkv, rest=rest, bf=bf, fcol=fcol, ka=ka, va=va, ya=ya, lse=lse,
                 qend=qend,
                 lb=lb, ng=ng, yb=yb, ob=ob, states=states, merged=merged, x1b=x1b, xh1=xh1, rs1=rs1, a=a,
                 wu=wu, wg=wg, hu=hu, hg=hg,
                 xh2=xh2, rs2=rs2)
    return x2, x2b, saved, (late if ride else None)


def _layer_bwd(dys, coefs, w, sp, s, l):
    n = f"l{l}_"
    dz2, dz2b, dg2, db2 = _ln_bwd(dys, coefs, s["xh2"], s["rs2"], sp["ln2_g"], n + "ln2_bwd")
    du, dg = _ffn_out_dx_swiglu(dz2b, w["w_ff_out"], s["hu"], s["hg"], n + "ffn_out_dx_swiglu")
    d_wffout = _matmul(s["a"], dz2b, "tn", F32, 1408, 1024, DW_ROWS, n + "ffn_out_dw")
    dx1u = _matmul(du, s["wu"], "nt", F32, MM_ROWS, 1024, FFN_HIDDEN, n + "ffn_in_dx_u")
    dx1g = _matmul(dg, s["wg"], "nt", F32, MM_ROWS, 1024, FFN_HIDDEN, n + "ffn_in_dx_g")
    d_wffin = jnp.concatenate([_matmul(s["x1b"], du, "tn", F32, 1024, 1408, DW_ROWS, n + "ffn_in_dw_u"),
                               _matmul(s["x1b"], dg, "tn", F32, 1024, 1408, DW_ROWS, n + "ffn_in_dw_g")], axis=1)
    dz1, dz1b, dg1, db1 = _ln_bwd([dz2, dx1u, dx1g], [ALPHA, 1.0, 1.0], s["xh1"], s["rs1"], sp["ln1_g"],
                                  n + "ln1_bwd")
    d_wout = _matmul(s["merged"], dz1b, "tn", F32, 1024, 1024, DW_ROWS, n + "out_dw")
    dgates, dpa, dpb, dya, dyb = _merge_bwd(dz1b, w["w_out"], s["ya"], s["yb"], w["w_pa"], w["w_pb"], s["rest"],
                                  n + "merge_bwd")
    d_wpa = _matmul(s["ya"], dpa, "tn", F32, 512, 1024, DW_ROWS, n + "pa_dw")
    d_wpb = _matmul(s["yb"], dpb, "tn", F32, 512, 1024, DW_ROWS, n + "pb_dw")
    qb, dob = _fox_prep_bwd(s["qkv"], s["fcol"], s["lse"], dya, s["ya"], n + "fox_prep_bwd")
    dk, dv, csum, dq, rsum = _fox_bwd(qb, s["ka"], s["va"], dob, s["qend"], n + "fox_bwd")
    dfa, dbf = _fox_gate_bwd(rsum, csum, s["rest"], s["bf"], n + "fox_gate_bwd")
    dhb, dlb, dng = _hgrn_bwd(dyb, s["rest"], s["ob"], s["states"], s["lb"], s["ng"], n + "hgrn_bwd")
    dproj = jnp.concatenate([dq, dk, dv, dgates, dhb, dfa], axis=1)
    wall = jnp.concatenate([s["wqkv"], s["wrest"]], axis=1)
    dxp = _matmul(dproj, wall, "nt", F32, MM_ROWS, 1024, 1920, n + "proj_dx")
    d_wall = _matmul(s["xb"], dproj, "tn", F32, 1024, 1152, DW_ROWS, n + "proj_dw")
    grads = dict(w_in=_merge_w_in_grad(d_wall), w_pa=d_wpa, w_pb=d_wpb, w_out=d_wout, w_ff_in=d_wffin,
                 w_ff_out=d_wffout, b_fgate=dbf[0, :A_HEADS], lb=dlb[0], norm_g=dng[0], ln1_g=dg1[0], ln1_b=db1[0],
                 ln2_g=dg2[0], ln2_b=db2[0])
    return [dz1, dxp], [ALPHA, 1.0], grads


def _lower_bounds(logits):
    sm = jax.nn.softmax(logits.astype(F32), axis=0)
    return jnp.cumsum(sm, axis=0) - sm[0:1]


def _local_step(x, target, wfull, small, ride=(), late_weights=None):
    lbs, lb_vjp = jax.vjp(_lower_bounds, small["hgrn_lb_logits"])
    h, hb = x, x.astype(BF16)
    wfull = list(wfull)
    saved, sps = [], []
    for l in range(DEPTH):
        sp = dict(b_fgate=small["b_fgate"][l], lb=lbs[l], norm_g=small["hgrn_norm_g"][l], ln1_g=small["ln1_g"][l],
                  ln1_b=small["ln1_b"][l], ln2_g=small["ln2_g"][l], ln2_b=small["ln2_b"][l])
        h, hb, s, late = _layer_fwd(h, hb, wfull[l], sp, l, ride if l == 0 else (), late_weights)
        if late is not None:
            wfull = [{**wfull[k], **late[k]} for k in range(DEPTH)]
        saved.append(s)
        sps.append(sp)
    dy, lpart = _loss_head(h, target)
    dys, coefs = [dy], [1.0]
    grads = [None] * DEPTH
    for l in reversed(range(DEPTH)):
        dys, coefs, grads[l] = _layer_bwd(dys, coefs, wfull[l], sps[l], saved[l], l)
    grad_x = _axpy2(coefs[0], dys[0], coefs[1], dys[1], "grad_x")
    d_logits = lb_vjp(jnp.stack([grads[l]["lb"] for l in range(DEPTH)]))[0]
    return lpart[0, 0], grad_x, grads, d_logits


_BIG = [("w_in", "w_in", (D_MODEL, IN_TOTAL), 1), ("w_branch_a", "w_pa", (A_WIDTH, D_MODEL), 1),
        ("w_branch_b", "w_pb", (B_WIDTH, D_MODEL), 1), ("w_out", "w_out", (D_MODEL, D_MODEL), 0),
        ("w_ff_in", "w_ff_in", (D_MODEL, 2 * FFN_HIDDEN), 1), ("w_ff_out", "w_ff_out", (FFN_HIDDEN, D_MODEL), 0)]
_SMALL = [("b_fgate", A_HEADS), ("hgrn_lb_logits", B_WIDTH), ("hgrn_norm_g", HD), ("ln1_g", D_MODEL),
          ("ln1_b", D_MODEL), ("ln2_g", D_MODEL), ("ln2_b", D_MODEL)]
N_BIG = len(_BIG)
SMALL_ROWS = 80


def _by_chip(full, axis):
    if axis == 0:
        return full.reshape(N_CHIPS, full.shape[0] // N_CHIPS, full.shape[1])
    n = full.shape[1] // N_CHIPS
    return jnp.stack([full[:, q * n:(q + 1) * n] for q in range(N_CHIPS)])


def _from_chips(shards, axis):
    if axis == 0:
        return shards.reshape(N_CHIPS * shards.shape[1], shards.shape[2])
    return jnp.concatenate([shards[q] for q in range(N_CHIPS)], axis=1)


def _pack_small(per_name):
    flat = jnp.concatenate([per_name[name].reshape(-1) for name, _ in _SMALL])
    return jnp.pad(flat, (0, SMALL_ROWS * 128 - flat.shape[0])).reshape(SMALL_ROWS, 128)


def _unpack_small(slab):
    flat, out, r = slab.reshape(-1), {}, 0
    for name, n in _SMALL:
        out[name] = flat[r:r + DEPTH * n].reshape(DEPTH, n)
        r += DEPTH * n
    return out


_ANY = pl.BlockSpec(memory_space=pl.ANY)


def _place():
    return lax.axis_index("x"), lax.axis_index("y"), lax.axis_index("c")


def _other_chips(x, y):
    return [(1 - x, y), (x, 1 - y), (1 - x, 1 - y)]


def _chip_exchange(mine_of, out_refs, send_sems, recv_sems, local_sems):
    _chip_exchange_start(mine_of, out_refs, send_sems, recv_sems, local_sems)
    _chip_exchange_wait(mine_of, out_refs, send_sems, recv_sems, local_sems)


def _chip_exchange_copies(mine_of, out_refs, send_sems, recv_sems, local_sems):
    x, y, c = _place()
    q = 2 * x + y
    local = [pltpu.make_async_copy(mine_of(w, q), out_ref.at[q], local_sems.at[w]) for w, out_ref in enumerate(out_refs)]
    sends, recvs = [], []
    for k, (px, py) in enumerate(_other_chips(x, y)):
        for w, out_ref in enumerate(out_refs):
            sems = dict(send_sem=send_sems.at[3 * w + k], recv_sem=recv_sems.at[3 * w + k], device_id=(px, py, c),
                        device_id_type=MESH)
            sends.append(pltpu.make_async_remote_copy(src_ref=mine_of(w, 2 * px + py), dst_ref=out_ref.at[q], **sems))
            recvs.append(pltpu.make_async_remote_copy(src_ref=mine_of(w, q), dst_ref=out_ref.at[2 * px + py], **sems))
    return local, sends, recvs


def _chip_exchange_start(*args):
    local, sends, _ = _chip_exchange_copies(*args)
    for cp in local + sends:
        cp.start()


def _chip_exchange_wait(*args):
    local, sends, recvs = _chip_exchange_copies(*args)
    for cp in recvs:
        cp.wait_recv()
    for cp in sends:
        cp.wait_send()
    for cp in local:
        cp.wait()


def _sem_scratch(n):
    return [pltpu.SemaphoreType.DMA((3 * n,)), pltpu.SemaphoreType.DMA((3 * n,)), pltpu.SemaphoreType.DMA((n,))]


def _gather_scratch(n):
    return _sem_scratch(n) + [pltpu.SemaphoreType.DMA((n,)), pltpu.SemaphoreType.DMA((n,))]


def _gather_out_shapes(mine):
    return [jax.ShapeDtypeStruct((DEPTH, N_CHIPS) + m.shape[1:], m.dtype) for m in mine]


def _gather_start(in_refs, out_refs, sems):
    c = lax.axis_index("c")
    _chip_exchange_start(lambda w, q: in_refs[w].at[c], [o.at[c] for o in out_refs], *sems[:3])


def _gather_finish(in_refs, out_refs, sems):
    x, y, c = _place()
    _chip_exchange_wait(lambda w, q: in_refs[w].at[c], [o.at[c] for o in out_refs], *sems[:3])
    pair_send, pair_recv = sems[3:]
    sibling = (x, y, 1 - c)
    fwds = []
    for w, o in enumerate(out_refs):
        cp = pltpu.make_async_remote_copy(src_ref=o.at[c], dst_ref=o.at[c], send_sem=pair_send.at[w],
                                          recv_sem=pair_recv.at[w], device_id=sibling, device_id_type=MESH)
        cp.start()
        fwds.append(cp)
    for w, o in enumerate(out_refs):
        pltpu.make_async_remote_copy(src_ref=o.at[1 - c], dst_ref=o.at[1 - c], send_sem=pair_send.at[w],
                                     recv_sem=pair_recv.at[w], device_id=sibling, device_id_type=MESH).wait_recv()
    for cp in fwds:
        cp.wait_send()


def _gather_weights(mine):
    n = len(mine)

    def body(*refs):
        in_refs, out_refs, sems = refs[:n], refs[n:2 * n], refs[2 * n:]
        _gather_start(in_refs, out_refs, sems)
        _gather_finish(in_refs, out_refs, sems)

    return pl.pallas_call(
        body, name="gather_weights", in_specs=[_ANY] * n, out_specs=[_ANY] * n,
        out_shape=_gather_out_shapes(mine), scratch_shapes=_gather_scratch(n),
    )(*mine)


def _pair_exchange(gs):
    n = len(gs)

    def body(*refs):
        g_refs, a_refs, send_sems, recv_sems = refs[:n], refs[n:2 * n], refs[2 * n], refs[2 * n + 1]
        x, y, c = _place()
        cps = []
        for w in range(n):
            cp = pltpu.make_async_remote_copy(src_ref=g_refs[w].at[1 - c], dst_ref=a_refs[w], send_sem=send_sems.at[w],
                                              recv_sem=recv_sems.at[w], device_id=(x, y, 1 - c), device_id_type=MESH)
            cp.start()
            cps.append(cp)
        for cp in cps:
            cp.wait()

    return pl.pallas_call(
        body, name="grad_pair_exchange", in_specs=[_ANY] * n, out_specs=[_ANY] * n,
        out_shape=[jax.ShapeDtypeStruct(g.shape[1:], g.dtype) for g in gs],
        scratch_shapes=[pltpu.SemaphoreType.DMA((n,)), pltpu.SemaphoreType.DMA((n,))],
    )(*gs)


def _row_block(rows):
    return math.gcd(rows, 256)


def _pair_sum(g, a, layer, name):
    _, nq, rows, cols = g.shape
    tb = _row_block(rows)

    def body(l_ref, g_ref, a_ref, o_ref):
        o_ref[...] = (g_ref[...] + a_ref[...]).astype(BF16)

    return pl.pallas_call(
        body, name=name,
        grid_spec=pltpu.PrefetchScalarGridSpec(
            num_scalar_prefetch=1, grid=(nq, rows // tb),
            in_specs=[pl.BlockSpec((None, None, tb, cols), lambda q, i, l_ref: (l_ref[0], q, i, 0)),
                      pl.BlockSpec((None, tb, cols), lambda q, i, l_ref: (q, i, 0))],
            out_specs=pl.BlockSpec((None, tb, cols), lambda q, i, l_ref: (q, i, 0))),
        out_shape=jax.ShapeDtypeStruct((nq, rows, cols), BF16),
        compiler_params=_params("parallel", "parallel"),
    )(layer.reshape(1).astype(jnp.int32), g, a)


def _shard_exchange(ps):
    n = len(ps)

    def body(*refs):
        p_refs, b_refs = refs[:n], refs[n:2 * n]
        send_sems, recv_sems, local_sems = refs[2 * n:]
        _chip_exchange(lambda w, q: p_refs[w].at[q], b_refs, send_sems, recv_sems, local_sems)

    return pl.pallas_call(
        body, name="grad_shard_exchange", in_specs=[_ANY] * n, out_specs=[_ANY] * n,
        out_shape=[jax.ShapeDtypeStruct(p.shape, p.dtype) for p in ps],
        scratch_shapes=_sem_scratch(n),
    )(*ps)


def _sum4(b, name):
    _, rows, cols = b.shape
    tb = _row_block(rows)

    def body(b_ref, o_ref):
        o_ref[...] = ((b_ref[0].astype(F32) + b_ref[1].astype(F32)) + b_ref[2].astype(F32)) + b_ref[3].astype(F32)

    return pl.pallas_call(
        body, name=name, grid=(rows // tb,),
        in_specs=[pl.BlockSpec((N_CHIPS, tb, cols), lambda i: (0, i, 0))],
        out_specs=pl.BlockSpec((tb, cols), lambda i: (i, 0)),
        out_shape=jax.ShapeDtypeStruct((rows, cols), F32),
        compiler_params=_params("parallel"),
    )(b)


def _result_exchange(gcs):
    n = len(gcs)

    def body(*refs):
        g_refs, o_refs, send_sems, recv_sems = refs[:n], refs[n:2 * n], refs[2 * n], refs[2 * n + 1]
        x, y, c = _place()
        cps = []
        for w in range(n):
            cp = pltpu.make_async_remote_copy(src_ref=g_refs[w], dst_ref=o_refs[w], send_sem=send_sems.at[w],
                                              recv_sem=recv_sems.at[w], device_id=(x, y, 1 - c), device_id_type=MESH)
            cp.start()
            cps.append(cp)
        for cp in cps:
            cp.wait()

    return pl.pallas_call(
        body, name="grad_result_exchange", in_specs=[_ANY] * n, out_specs=[_ANY] * n,
        out_shape=[jax.ShapeDtypeStruct(g.shape, g.dtype) for g in gcs],
        scratch_shapes=[pltpu.SemaphoreType.DMA((n,)), pltpu.SemaphoreType.DMA((n,))],
    )(*gcs)


def _allreduce_small(v):
    def body(v_ref, o_ref, buf, send_sems, recv_sems):
        x, y, c = _place()
        me = 4 * x + 2 * y + c
        buf[me] = v_ref[...]
        peers = []
        for k in range(1, N_DEV):
            px = 1 - x if k & 4 else x
            py = 1 - y if k & 2 else y
            pc = 1 - c if k & 1 else c
            peers.append((px, py, pc))
        sends = []
        for k, peer in enumerate(peers):
            cp = pltpu.make_async_remote_copy(src_ref=v_ref, dst_ref=buf.at[me], send_sem=send_sems.at[k],
                                              recv_sem=recv_sems.at[k], device_id=peer, device_id_type=MESH)
            cp.start()
            sends.append(cp)
        for k, (px, py, pc) in enumerate(peers):
            pltpu.make_async_remote_copy(src_ref=v_ref, dst_ref=buf.at[4 * px + 2 * py + pc], send_sem=send_sems.at[k],
                                         recv_sem=recv_sems.at[k], device_id=(px, py, pc),
                                         device_id_type=MESH).wait_recv()
        for cp in sends:
            cp.wait_send()
        acc = buf[0]
        for i in range(1, N_DEV):
            acc = acc + buf[i]
        o_ref[...] = acc

    vm = pl.BlockSpec(memory_space=pltpu.VMEM)
    return pl.pallas_call(
        body, name="small_allreduce", in_specs=[vm], out_specs=vm,
        out_shape=jax.ShapeDtypeStruct(v.shape, F32),
        scratch_shapes=[pltpu.VMEM((N_DEV,) + v.shape, F32), pltpu.SemaphoreType.DMA((N_DEV - 1,)),
                        pltpu.SemaphoreType.DMA((N_DEV - 1,))],
    )(v)


def _adam_update(w, g, m, v):
    nm = ADAM_B1 * m + (1.0 - ADAM_B1) * g
    nv = ADAM_B2 * v + (1.0 - ADAM_B2) * (g * g)
    m_hat = nm / (1.0 - ADAM_B1 ** ADAM_STEP)
    v_hat = nv / (1.0 - ADAM_B2 ** ADAM_STEP)
    return -ADAM_LR * (m_hat / (jnp.sqrt(v_hat) + ADAM_EPS) + ADAM_WD * w), nm, nv


def _adamw_small(w, g, m, v, name):
    def body(w_ref, g_ref, m_ref, v_ref, d_ref, nm_ref, nv_ref):
        d_ref[...], nm_ref[...], nv_ref[...] = _adam_update(w_ref[...], g_ref[...], m_ref[...], v_ref[...])

    vm = pl.BlockSpec(memory_space=pltpu.VMEM)
    return pl.pallas_call(
        body, name=name, in_specs=[vm] * 4, out_specs=[vm] * 3,
        out_shape=[jax.ShapeDtypeStruct(w.shape, F32)] * 3,
    )(w, g, m, v)


def _adamw_big(w, m, v, g_own, g_other, layer, name):
    _, rows, cols = w.shape
    tb = _row_block(rows)

    def body(l_ref, w_ref, m_ref, v_ref, go_ref, gx_ref, g_ref, d_ref, nm_ref, nv_ref):
        gv = jnp.where(pl.program_id(0) == l_ref[0], go_ref[...], gx_ref[...])
        g_ref[...] = gv
        d_ref[...], nm_ref[...], nv_ref[...] = _adam_update(w_ref[...], gv, m_ref[...], v_ref[...])

    per_layer = pl.BlockSpec((None, tb, cols), lambda l, i, l_ref: (l, i, 0))
    shared = pl.BlockSpec((tb, cols), lambda l, i, l_ref: (i, 0))
    return pl.pallas_call(
        body, name=name,
        grid_spec=pltpu.PrefetchScalarGridSpec(
            num_scalar_prefetch=1, grid=(DEPTH, rows // tb),
            in_specs=[per_layer, per_layer, per_layer, shared, shared], out_specs=[per_layer] * 4),
        out_shape=[jax.ShapeDtypeStruct(w.shape, F32)] * 4,
        compiler_params=_params("parallel", "parallel"),
    )(layer.reshape(1).astype(jnp.int32), w, m, v, g_own, g_other)


def kernel(x, w_in, b_fgate, hgrn_lb_logits, hgrn_norm_g, w_branch_a, w_branch_b, w_out, ln1_g, ln1_b, w_ff_in, w_ff_out, ln2_g, ln2_b, loss_target, m_w_in, m_b_fgate, m_hgrn_lb_logits, m_hgrn_norm_g, m_w_branch_a, m_w_branch_b, m_w_out, m_ln1_g, m_ln1_b, m_w_ff_in, m_w_ff_out, m_ln2_g, m_ln2_b, v_w_in, v_b_fgate, v_hgrn_lb_logits, v_hgrn_norm_g, v_w_branch_a, v_w_branch_b, v_w_out, v_ln1_g, v_ln1_b, v_w_ff_in, v_w_ff_out, v_ln2_g, v_ln2_b):
    weights = dict(w_in=w_in, b_fgate=b_fgate, hgrn_lb_logits=hgrn_lb_logits, hgrn_norm_g=hgrn_norm_g,
                   w_branch_a=w_branch_a, w_branch_b=w_branch_b, w_out=w_out, ln1_g=ln1_g, ln1_b=ln1_b,
                   w_ff_in=w_ff_in, w_ff_out=w_ff_out, ln2_g=ln2_g, ln2_b=ln2_b)
    mom1 = dict(w_in=m_w_in, b_fgate=m_b_fgate, hgrn_lb_logits=m_hgrn_lb_logits, hgrn_norm_g=m_hgrn_norm_g,
                w_branch_a=m_w_branch_a, w_branch_b=m_w_branch_b, w_out=m_w_out, ln1_g=m_ln1_g, ln1_b=m_ln1_b,
                w_ff_in=m_w_ff_in, w_ff_out=m_w_ff_out, ln2_g=m_ln2_g, ln2_b=m_ln2_b)
    mom2 = dict(w_in=v_w_in, b_fgate=v_b_fgate, hgrn_lb_logits=v_hgrn_lb_logits, hgrn_norm_g=v_hgrn_norm_g,
                w_branch_a=v_w_branch_a, w_branch_b=v_w_branch_b, w_out=v_w_out, ln1_g=v_ln1_g, ln1_b=v_ln1_b,
                w_ff_in=v_w_ff_in, w_ff_out=v_w_ff_out, ln2_g=v_ln2_g, ln2_b=v_ln2_b)
    core = lax.axis_index("c")

    def full_w_in(gathered):
        return _from_chips(jnp.concatenate([gathered[0], gathered[1]], axis=1), 1)

    def late_weights(gathered):
        per_layer = [{key: _from_chips(gathered[1 + w][l], axis) for w, (_, key, _, axis) in enumerate(_BIG[1:])}
                     for l in range(DEPTH)]
        per_layer[1]["w_in"] = full_w_in(gathered[0])
        return per_layer

    w_in_halves = weights["w_in"].astype(BF16).reshape(DEPTH, 2, D_MODEL // 2, IN_TOTAL // N_CHIPS)
    wfull = [{"w_in": full_w_in(_gather_weights([w_in_halves[0]])[0])}, {}]
    ride = [w_in_halves[1]] + [weights[name].astype(BF16) for name, _, _, _ in _BIG[1:]]
    small = {name: weights[name] for name, _ in _SMALL}

    loss_part, grad_x, grads, d_logits = _local_step(x[0], loss_target[0], wfull, small, ride, late_weights)

    g_all = [jnp.stack([_by_chip(grads[l][key], axis) for l in range(DEPTH)]) for _, key, _, axis in _BIG]
    received = _pair_exchange(g_all)
    pair = [_pair_sum(g_all[w], received[w], core, f"grad_pair_sum_{w}") for w in range(N_BIG)]
    by_chip = _shard_exchange(pair)
    g_layer = [_sum4(by_chip[w], f"grad_chip_sum_{w}") for w in range(N_BIG)]
    g_other = _result_exchange(g_layer)
    out_g, out_d, out_m, out_v = {}, {}, {}, {}
    for w, (name, _, _, _) in enumerate(_BIG):
        out_g[name], out_d[name], out_m[name], out_v[name] = _adamw_big(
            weights[name], mom1[name], mom2[name], g_layer[w], g_other[w], core, f"adamw_{name}")

    small_grads = {name: jnp.stack([grads[l][key] for l in range(DEPTH)])
                   for name, key in [("b_fgate", "b_fgate"), ("hgrn_norm_g", "norm_g"), ("ln1_g", "ln1_g"),
                                     ("ln1_b", "ln1_b"), ("ln2_g", "ln2_g"), ("ln2_b", "ln2_b")]}
    small_grads["hgrn_lb_logits"] = d_logits
    gs = _allreduce_small(_pack_small(small_grads))
    ds, ms, vs = _adamw_small(_pack_small(small), gs, _pack_small({n: mom1[n] for n, _ in _SMALL}),
                              _pack_small({n: mom2[n] for n, _ in _SMALL}), "adamw_small")
    for tree, slab in ((out_g, gs), (out_d, ds), (out_m, ms), (out_v, vs)):
        tree.update(_unpack_small(slab))

    loss = lax.psum(loss_part, ("x", "y", "c"))
    order = ["w_in", "b_fgate", "hgrn_lb_logits", "hgrn_norm_g", "w_branch_a", "w_branch_b", "w_out", "ln1_g", "ln1_b",
             "w_ff_in", "w_ff_out", "ln2_g", "ln2_b"]
    return (loss, grad_x[None], *[out_g[n] for n in order], *[out_d[n] for n in order],
            *[out_m[n] for n in order], *[out_v[n] for n in order])
```

```python
import math

import jax
import jax.numpy as jnp
from jax import lax
from jax.experimental import pallas as pl
from jax.experimental.pallas import tpu as pltpu

F32 = jnp.float32
BF16 = jnp.bfloat16

D_MODEL = 1024
DEPTH = 2
A_HEADS = 8
A_WIDTH = 512
B_WIDTH = 512
B_HEADS = 4
HD = 128
CHUNK = 64
SUB = 16
FFN_HIDDEN = 2816
IN_TOTAL = 5640
ALPHA = (2 * DEPTH) ** 0.25
LN_EPS = 1e-5
RMS_EPS = 1e-6
ADAM_LR = 0.001
ADAM_B1 = 0.9
ADAM_B2 = 0.999
ADAM_EPS = 1e-08
ADAM_WD = 0.01
ADAM_STEP = 10
EXP_CLAMP = 60.0

VMEM_LIMIT_BYTES = 56 * 1024 * 1024
MM_ROWS = 1024
DW_ROWS = 2048
N_CHIPS = 4
N_DEV = 8
MESH = pl.DeviceIdType.MESH

_DN = {
    "nn": (((1,), (0,)), ((), ())),
    "nt": (((1,), (1,)), ((), ())),
    "tn": (((0,), (0,)), ((), ())),
}


def _dot(a, b, mode="nn"):
    return lax.dot_general(a.astype(BF16), b.astype(BF16), _DN[mode], preferred_element_type=F32)


def _pieces(x):
    h = x.astype(BF16)
    r = x - h.astype(F32)
    m = r.astype(BF16)
    return h, m, (r - m.astype(F32)).astype(BF16)


def _dot_hi(a, b, mode="nn", exact="a"):
    if exact == "a":
        h, m, l = _pieces(b)
        return (_dot(a, l, mode) + _dot(a, m, mode)) + _dot(a, h, mode)
    h, m, l = _pieces(a)
    return (_dot(l, b, mode) + _dot(m, b, mode)) + _dot(h, b, mode)


def _hdot(a, b, mode="nn"):
    bh, bl, _ = _pieces(b)
    return _dot(a, bl, mode) + _dot(a, bh, mode)


def _params(*sem):
    return pltpu.CompilerParams(dimension_semantics=sem, vmem_limit_bytes=VMEM_LIMIT_BYTES)


def _sigmoid(x):
    return 1.0 / (1.0 + jnp.exp(-x))


def _matmul(a, b, mode, out_dtype, tm, tn, tk, name):
    if mode == "nn":
        (m, k), (k2, n) = a.shape, b.shape
    elif mode == "nt":
        (m, k), (n, k2) = a.shape, b.shape
    else:
        (k, m), (k2, n) = a.shape, b.shape
    assert k == k2, (a.shape, b.shape, mode)
    tm, tn, tk = min(tm, m), min(tn, n), min(tk, k)
    assert m % tm == 0 and n % tn == 0 and k % tk == 0, (a.shape, b.shape, tm, tn, tk)
    nk = k // tk
    if mode == "tn":
        a_spec = pl.BlockSpec((tk, tm), lambda j, i, kk: (kk, i))
    else:
        a_spec = pl.BlockSpec((tm, tk), lambda j, i, kk: (i, kk))
    if mode == "nt":
        b_spec = pl.BlockSpec((tn, tk), lambda j, i, kk: (j, kk))
    else:
        b_spec = pl.BlockSpec((tk, tn), lambda j, i, kk: (kk, j))
    use_acc = nk > 1 and out_dtype != F32

    def body(a_ref, b_ref, o_ref, *scratch):
        p = _dot(a_ref[...], b_ref[...], mode)
        if nk == 1:
            o_ref[...] = p.astype(out_dtype)
            return
        acc_ref = scratch[0] if use_acc else o_ref
        kk = pl.program_id(2)

        @pl.when(kk == 0)
        def _():
            acc_ref[...] = p

        @pl.when(kk > 0)
        def _():
            acc_ref[...] += p

        if use_acc:
            @pl.when(kk == nk - 1)
            def _():
                o_ref[...] = acc_ref[...].astype(out_dtype)

    return pl.pallas_call(
        body,
        name=name,
        grid=(n // tn, m // tm, nk),
        in_specs=[a_spec, b_spec],
        out_specs=pl.BlockSpec((tm, tn), lambda j, i, kk: (i, j)),
        out_shape=jax.ShapeDtypeStruct((m, n), out_dtype),
        scratch_shapes=[pltpu.VMEM((tm, tn), F32)] if use_acc else [],
        compiler_params=_params("parallel", "parallel", "arbitrary"),
    )(a, b)


def _mm_res_ln(a, w, res, g, b, name, tm=512):
    t, k = a.shape
    d = w.shape[1]
    tm = min(tm, t)

    def body(a_ref, w_ref, r_ref, g_ref, b_ref, y_ref, yb_ref, xh_ref, rs_ref):
        z = ALPHA * r_ref[...] + _dot(a_ref[...], w_ref[...])
        mu = jnp.mean(z, axis=-1, keepdims=True)
        zc = z - mu
        var = jnp.mean(zc * zc, axis=-1, keepdims=True)
        rstd = lax.rsqrt(var + LN_EPS)
        xh = zc * rstd
        y = xh * g_ref[...] + b_ref[...]
        y_ref[...] = y
        yb_ref[...] = y.astype(BF16)
        xh_ref[...] = xh
        rs_ref[...] = rstd

    row = lambda i: (i, 0)
    fix = lambda i: (0, 0)
    return pl.pallas_call(
        body,
        name=name,
        grid=(t // tm,),
        in_specs=[pl.BlockSpec((tm, k), row), pl.BlockSpec((k, d), fix), pl.BlockSpec((tm, d), row),
                  pl.BlockSpec((1, d), fix), pl.BlockSpec((1, d), fix)],
        out_specs=[pl.BlockSpec((tm, d), row), pl.BlockSpec((tm, d), row), pl.BlockSpec((tm, d), row),
                   pl.BlockSpec((tm, 1), row)],
        out_shape=[jax.ShapeDtypeStruct((t, d), F32), jax.ShapeDtypeStruct((t, d), BF16),
                   jax.ShapeDtypeStruct((t, d), F32), jax.ShapeDtypeStruct((t, 1), F32)],
        compiler_params=_params("parallel"),
    )(a, w, res, g.reshape(1, d), b.reshape(1, d))


def _ln_bwd(dys, coefs, xhat, rstd, g, name, tm=512):
    t, d = xhat.shape
    tm = min(tm, t)
    n_in = len(dys)

    def body(*refs):
        dy_refs = refs[:n_in]
        xh_ref, rs_ref, g_ref, dz_ref, dzb_ref, dg_ref, db_ref = refs[n_in:]
        dy = coefs[0] * dy_refs[0][...].astype(F32)
        for c, r in zip(coefs[1:], dy_refs[1:]):
            dy = dy + c * r[...].astype(F32)
        xh = xh_ref[...]
        dxh = dy * g_ref[...]
        m1 = jnp.mean(dxh, axis=-1, keepdims=True)
        m2 = jnp.mean(dxh * xh, axis=-1, keepdims=True)
        dz = rs_ref[...] * (dxh - m1 - xh * m2)
        dz_ref[...] = dz
        dzb_ref[...] = dz.astype(BF16)
        pg = jnp.sum(dy * xh, axis=0, keepdims=True)
        pb = jnp.sum(dy, axis=0, keepdims=True)

        @pl.when(pl.program_id(0) == 0)
        def _():
            dg_ref[...] = pg
            db_ref[...] = pb

        @pl.when(pl.program_id(0) > 0)
        def _():
            dg_ref[...] += pg
            db_ref[...] += pb

    row = lambda i: (i, 0)
    fix = lambda i: (0, 0)
    return pl.pallas_call(
        body,
        name=name,
        grid=(t // tm,),
        in_specs=[pl.BlockSpec((tm, d), row)] * n_in
        + [pl.BlockSpec((tm, d), row), pl.BlockSpec((tm, 1), row), pl.BlockSpec((1, d), fix)],
        out_specs=[pl.BlockSpec((tm, d), row), pl.BlockSpec((tm, d), row), pl.BlockSpec((1, d), fix),
                   pl.BlockSpec((1, d), fix)],
        out_shape=[jax.ShapeDtypeStruct((t, d), F32), jax.ShapeDtypeStruct((t, d), BF16),
                   jax.ShapeDtypeStruct((1, d), F32), jax.ShapeDtypeStruct((1, d), F32)],
        compiler_params=_params("arbitrary"),
    )(*dys, xhat, rstd, g.reshape(1, d))


def _loss_head(y, target, name="loss_head", tm=512):
    t, d = y.shape
    tm = min(tm, t)

    def body(y_ref, t_ref, dy_ref, l_ref):
        e = y_ref[...] - t_ref[...]
        dy_ref[...] = e * (1.0 / d)
        part = jnp.full((8, 128), 0.5 / d, F32) * jnp.sum(e * e)

        @pl.when(pl.program_id(0) == 0)
        def _():
            l_ref[...] = part

        @pl.when(pl.program_id(0) > 0)
        def _():
            l_ref[...] += part

    row = lambda i: (i, 0)
    return pl.pallas_call(
        body,
        name=name,
        grid=(t // tm,),
        in_specs=[pl.BlockSpec((tm, d), row), pl.BlockSpec((tm, d), row)],
        out_specs=[pl.BlockSpec((tm, d), row), pl.BlockSpec((8, 128), lambda i: (0, 0))],
        out_shape=[jax.ShapeDtypeStruct((t, d), F32), jax.ShapeDtypeStruct((8, 128), F32)],
        compiler_params=_params("arbitrary"),
    )(y, target)


FFN_COLS = FFN_HIDDEN // 2


def _ffn_in_swiglu(xb, wu, wg, name, tm=MM_ROWS):
    t, d = xb.shape
    tm = min(tm, t)

    def body(x_ref, wu_ref, wg_ref, a_ref, u_ref, g_ref):
        x = x_ref[...]
        u = _dot(x, wu_ref[...])
        g = _dot(x, wg_ref[...])
        u_ref[...] = u.astype(BF16)
        g_ref[...] = g.astype(BF16)
        a_ref[...] = (g * _sigmoid(g) * u).astype(BF16)

    wspec = pl.BlockSpec((d, FFN_COLS), lambda j, i: (0, j))
    out = pl.BlockSpec((tm, FFN_COLS), lambda j, i: (i, j))
    return pl.pallas_call(
        body,
        name=name,
        grid=(FFN_HIDDEN // FFN_COLS, t // tm),
        in_specs=[pl.BlockSpec((tm, d), lambda j, i: (i, 0)), wspec, wspec],
        out_specs=[out, out, out],
        out_shape=[jax.ShapeDtypeStruct((t, FFN_HIDDEN), BF16)] * 3,
        compiler_params=_params("parallel", "parallel"),
    )(xb, wu, wg)


def _ffn_out_dx_swiglu(dzb, w_ff_out, u, g, name, tm=MM_ROWS):
    t, d = dzb.shape
    tm = min(tm, t)

    def body(dz_ref, w_ref, u_ref, g_ref, du_ref, dg_ref):
        da = _dot(dz_ref[...], w_ref[...], "nt")
        gv = g_ref[...].astype(F32)
        sg = _sigmoid(gv)
        du_ref[...] = (da * gv * sg).astype(BF16)
        dg_ref[...] = (da * u_ref[...].astype(F32) * (sg * (1.0 + gv * (1.0 - sg)))).astype(BF16)

    blk = pl.BlockSpec((tm, FFN_COLS), lambda j, i: (i, j))
    return pl.pallas_call(
        body,
        name=name,
        grid=(FFN_HIDDEN // FFN_COLS, t // tm),
        in_specs=[pl.BlockSpec((tm, d), lambda j, i: (i, 0)), pl.BlockSpec((FFN_COLS, d), lambda j, i: (j, 0)), blk, blk],
        out_specs=[blk, blk],
        out_shape=[jax.ShapeDtypeStruct((t, FFN_HIDDEN), BF16)] * 2,
        compiler_params=_params("parallel", "parallel"),
    )(dzb, w_ff_out, u, g)


def _merge_fwd(ya, yb, wpa, wpb, rest, name, tm=512):
    t = ya.shape[0]
    tm = min(tm, t)

    def body(ya_ref, yb_ref, wa_ref, wb_ref, ga_ref, gb_ref, o_ref):
        pa = _dot(ya_ref[...], wa_ref[...])
        pb = _dot(yb_ref[...], wb_ref[...])
        o_ref[...] = (_sigmoid(ga_ref[...]) * pa + _sigmoid(gb_ref[...]) * pb).astype(BF16)

    row = lambda i: (i, 0)
    fix = lambda i: (0, 0)
    return pl.pallas_call(
        body,
        name=name,
        grid=(t // tm,),
        in_specs=[pl.BlockSpec((tm, A_WIDTH), row), pl.BlockSpec((tm, B_WIDTH), row),
                  pl.BlockSpec((A_WIDTH, D_MODEL), fix), pl.BlockSpec((B_WIDTH, D_MODEL), fix),
                  pl.BlockSpec((tm, D_MODEL), lambda i: (i, 0)), pl.BlockSpec((tm, D_MODEL), lambda i: (i, 1))],
        out_specs=pl.BlockSpec((tm, D_MODEL), row),
        out_shape=jax.ShapeDtypeStruct((t, D_MODEL), BF16),
        compiler_params=_params("parallel"),
    )(ya, yb, wpa, wpb, rest, rest)


def _merge_bwd(dzb, w_out, ya, yb, wpa, wpb, rest, name, tm=512):
    t = ya.shape[0]
    tm = min(tm, t)

    def body(dz_ref, wo_ref, ya_ref, yb_ref, wa_ref, wb_ref, ga_ref, gb_ref, dg_ref, dpa_ref, dpb_ref, dya_ref,
             dyb_ref):
        dm_v = _dot(dz_ref[...], wo_ref[...], "nt")
        pa = _dot(ya_ref[...], wa_ref[...])
        pb = _dot(yb_ref[...], wb_ref[...])
        sa = _sigmoid(ga_ref[...])
        sb = _sigmoid(gb_ref[...])
        dg_ref[:, :D_MODEL] = (dm_v * pa * sa * (1.0 - sa)).astype(BF16)
        dg_ref[:, D_MODEL:] = (dm_v * pb * sb * (1.0 - sb)).astype(BF16)
        dpa = (dm_v * sa).astype(BF16)
        dpb = (dm_v * sb).astype(BF16)
        dpa_ref[...] = dpa
        dpb_ref[...] = dpb
        dya_ref[...] = _dot(dpa, wa_ref[...], "nt").astype(BF16)
        dyb_ref[...] = _dot(dpb, wb_ref[...], "nt")

    row = lambda i: (i, 0)
    fix = lambda i: (0, 0)
    return pl.pallas_call(
        body,
        name=name,
        grid=(t // tm,),
        in_specs=[pl.BlockSpec((tm, D_MODEL), row), pl.BlockSpec((D_MODEL, D_MODEL), fix),
                  pl.BlockSpec((tm, A_WIDTH), row), pl.BlockSpec((tm, B_WIDTH), row),
                  pl.BlockSpec((A_WIDTH, D_MODEL), fix), pl.BlockSpec((B_WIDTH, D_MODEL), fix),
                  pl.BlockSpec((tm, D_MODEL), lambda i: (i, 0)), pl.BlockSpec((tm, D_MODEL), lambda i: (i, 1))],
        out_specs=[pl.BlockSpec((tm, 2 * D_MODEL), row), pl.BlockSpec((tm, D_MODEL), row),
                   pl.BlockSpec((tm, D_MODEL), row), pl.BlockSpec((tm, A_WIDTH), row), pl.BlockSpec((tm, B_WIDTH), row)],
        out_shape=[jax.ShapeDtypeStruct((t, 2 * D_MODEL), BF16), jax.ShapeDtypeStruct((t, D_MODEL), BF16),
                   jax.ShapeDtypeStruct((t, D_MODEL), BF16), jax.ShapeDtypeStruct((t, A_WIDTH), BF16),
                   jax.ShapeDtypeStruct((t, B_WIDTH), F32)],
        compiler_params=_params("parallel"),
    )(dzb, w_out, ya, yb, wpa, wpb, rest, rest)


FA_BLOCK = 4224 // 128 - 1


def _tri(n, lower):
    r = lax.broadcasted_iota(jnp.int32, (n, n), 0)
    c = lax.broadcasted_iota(jnp.int32, (n, n), 1)
    return jnp.where((r >= c) if lower else (r <= c), 1.0, 0.0).astype(F32)


def _head_spread(expand):
    shape = (128, A_WIDTH) if expand else (A_WIDTH, 128)
    r = lax.broadcasted_iota(jnp.int32, shape, 0)
    c = lax.broadcasted_iota(jnp.int32, shape, 1)
    hit = ((c >= 64 * r) & (c < 64 * r + 64)) if expand else (r == 64 * c)
    return jnp.where(hit, 1.0, 0.0).astype(F32)


def _fox_gate_fwd(rest, bf, name, tb=512):
    t = rest.shape[0]
    tb = min(tb, t)

    def body(fa_ref, bf_ref, f_ref, fc_ref, carry):
        @pl.when(pl.program_id(0) == 0)
        def _():
            carry[...] = jnp.zeros_like(carry)

        z = fa_ref[...] + bf_ref[...]
        logf = jnp.minimum(z, 0.0) - jnp.log(1.0 + jnp.exp(-jnp.abs(z)))
        f = _dot_hi(_tri(tb, True), logf) + carry[...]
        f_ref[...] = f
        fc_ref[...] = _dot_hi(f, _head_spread(True), exact="b")
        carry[...] = f[tb - 1:tb, :]

    return pl.pallas_call(
        body,
        name=name,
        grid=(t // tb,),
        in_specs=[pl.BlockSpec((tb, 128), lambda i: (i, FA_BLOCK)), pl.BlockSpec((1, 128), lambda i: (0, 0))],
        out_specs=[pl.BlockSpec((tb, 128), lambda i: (i, 0)), pl.BlockSpec((tb, A_WIDTH), lambda i: (i, 0))],
        out_shape=[jax.ShapeDtypeStruct((t, 128), F32), jax.ShapeDtypeStruct((t, A_WIDTH), F32)],
        scratch_shapes=[pltpu.VMEM((1, 128), F32)],
        compiler_params=_params("arbitrary"),
    )(rest, bf)


def _fox_gate_bwd(rsum, csum, rest, bf, name, tb=512):
    t = rest.shape[0]
    tb = min(tb, t)
    nb = t // tb

    def body(rs_ref, cs_ref, fa_ref, bf_ref, dfa_ref, dbf_ref, carry):
        @pl.when(pl.program_id(0) == 0)
        def _():
            carry[...] = jnp.zeros_like(carry)

        d_f = _dot_hi(rs_ref[...] - cs_ref[...], _head_spread(False), exact="b")
        dlogf = _dot_hi(_tri(tb, False), d_f) + carry[...]
        carry[...] = dlogf[0:1, :]
        z = fa_ref[...] + bf_ref[...]
        dz = dlogf * _sigmoid(-z)
        dfa_ref[...] = dz.astype(BF16)
        part = jnp.sum(dz, axis=0, keepdims=True)

        @pl.when(pl.program_id(0) == 0)
        def _():
            dbf_ref[...] = part

        @pl.when(pl.program_id(0) > 0)
        def _():
            dbf_ref[...] += part

    return pl.pallas_call(
        body,
        name=name,
        grid=(nb,),
        in_specs=[pl.BlockSpec((tb, A_WIDTH), lambda i: (nb - 1 - i, 0)),
                  pl.BlockSpec((tb, A_WIDTH), lambda i: (nb - 1 - i, 0)),
                  pl.BlockSpec((tb, 128), lambda i: (nb - 1 - i, FA_BLOCK)),
                  pl.BlockSpec((1, 128), lambda i: (0, 0))],
        out_specs=[pl.BlockSpec((tb, 128), lambda i: (nb - 1 - i, 0)), pl.BlockSpec((1, 128), lambda i: (0, 0))],
        out_shape=[jax.ShapeDtypeStruct((t, 128), BF16), jax.ShapeDtypeStruct((1, 128), F32)],
        scratch_shapes=[pltpu.VMEM((1, 128), F32)],
        compiler_params=_params("arbitrary"),
    )(rsum, csum, rest, bf)


ATT_BLOCK = 512


def _head_mask(shape, j):
    lane = lax.broadcasted_iota(jnp.int32, shape, 1)
    return (lane < 64) if j == 0 else (lane >= 64)


def _aug_lanes(tb, j):
    lane = lax.broadcasted_iota(jnp.int32, (tb, 128), 1)
    own = (lane < 64) if j == 0 else (lane >= 64)
    return own, lane - 64 * (1 - j)


def _aug_query(own, li, q, pieces):
    h, m, l = pieces
    one, zero = jnp.ones_like(h), jnp.zeros_like(h)
    spare = jnp.where(li == 0, h, jnp.where(li == 1, m, jnp.where(li == 2, l, jnp.where(li < 6, one, zero))))
    return jnp.where(own, q, spare)


def _fox_prep_fwd(qkv, fcol, name, tb=2048):
    t = qkv.shape[0]
    tb = min(tb, t)

    def body(q_ref, k_ref, v_ref, fc_ref, qa_ref, ka_ref, va_ref, qn_ref, kn_ref):
        pieces = _pieces(pltpu.roll(fc_ref[...], 64, 1))
        h, m, l = pieces
        q, k, v = q_ref[...], k_ref[...], v_ref[...]
        first = _head_mask((tb, 128), 0)
        for nrm_ref, x in ((qn_ref, q.astype(F32)), (kn_ref, k.astype(F32))):
            n0 = jnp.max(jnp.sum(jnp.where(first, x * x, 0.0), axis=1, keepdims=True))
            n1 = jnp.max(jnp.sum(jnp.where(first, 0.0, x * x), axis=1, keepdims=True))
            nrm_ref[...] = jnp.where(_head_mask((8, 128), 0), n0, n1)
        one, zero = jnp.ones_like(h), jnp.zeros_like(h)
        for j in (0, 1):
            own, li = _aug_lanes(tb, j)
            cols = slice(128 * j, 128 * (j + 1))
            qa_ref[:, cols] = _aug_query(own, li, q * 0.125, pieces)
            ks = jnp.where(li < 3, one, jnp.where(li == 3, -h, jnp.where(li == 4, -m, jnp.where(li == 5, -l, zero))))
            ka_ref[:, cols] = jnp.where(own, k, ks)
            va_ref[:, cols] = jnp.where(own, v, one)

    blk = pl.BlockSpec((tb, 256), lambda i, h: (i, h))
    nrm = pl.BlockSpec((None, None, 8, 128), lambda i, h: (i, h, 0, 0))
    return pl.pallas_call(
        body, name=name, grid=(t // tb, 4),
        in_specs=[pl.BlockSpec((tb, 128), lambda i, h: (i, h)), pl.BlockSpec((tb, 128), lambda i, h: (i, 4 + h)),
                  pl.BlockSpec((tb, 128), lambda i, h: (i, 8 + h)), pl.BlockSpec((tb, 128), lambda i, h: (i, h))],
        out_specs=[blk, blk, blk, nrm, nrm],
        out_shape=[jax.ShapeDtypeStruct((t, 2 * A_WIDTH), BF16)] * 3
        + [jax.ShapeDtypeStruct((t // tb, 4, 8, 128), F32)] * 2,
        compiler_params=_params("parallel", "parallel"),
    )(qkv, qkv, qkv, fcol)


def _fox_prep_bwd(qkv, fcol, lse, do, o, name, tb=2048):
    t = qkv.shape[0]
    tb = min(tb, t)

    def body(q_ref, fc_ref, lse_ref, do_ref, o_ref, qb_ref, dob_ref):
        pieces = _pieces(pltpu.roll(fc_ref[...] - lse_ref[...], 64, 1))
        q = q_ref[...] * 0.125
        do_v = do_ref[...]
        prod = do_v.astype(F32) * o_ref[...].astype(F32)
        for j in (0, 1):
            own, li = _aug_lanes(tb, j)
            cols = slice(128 * j, 128 * (j + 1))
            qb_ref[:, cols] = _aug_query(own, li, q, pieces)
            delta = jnp.sum(jnp.where(own, prod, 0.0), axis=1, keepdims=True)
            h, m, l = _pieces(jnp.broadcast_to(delta, (tb, 128)))
            ds = jnp.where(li == 0, -h, jnp.where(li == 1, -m, jnp.where(li == 2, -l, jnp.zeros_like(h))))
            dob_ref[:, cols] = jnp.where(own, do_v, ds)

    pair = pl.BlockSpec((tb, 128), lambda i, h: (i, h))
    blk = pl.BlockSpec((tb, 256), lambda i, h: (i, h))
    return pl.pallas_call(
        body, name=name, grid=(t // tb, 4),
        in_specs=[pair, pair, pair, pair, pair],
        out_specs=[blk, blk],
        out_shape=[jax.ShapeDtypeStruct((t, 2 * A_WIDTH), BF16)] * 2,
        compiler_params=_params("parallel", "parallel"),
    )(qkv, fcol, lse, do, o)


def _tile_mask(n, transposed):
    r = lax.broadcasted_iota(jnp.int32, (n, n), 0)
    c = lax.broadcasted_iota(jnp.int32, (n, n), 1)
    return (c >= r) if transposed else (r >= c)


UNDERFLOW = -110.0


def _fox_block_ranges(qn, kn, fcum):
    t = fcum.shape[0]
    blk = min(ATT_BLOCK, t)
    nb = t // blk
    q2 = jnp.max(qn[:, :, 0, ::64].reshape(-1, A_HEADS), axis=0)
    k2 = jnp.max(kn[:, :, 0, ::64].reshape(-1, A_HEADS), axis=0)
    bound = 2.0 * jnp.sqrt(q2 * k2) * 0.125
    f = fcum[:, :A_HEADS]
    first = f[0::blk].T
    last = f[blk - 1::blk].T
    dead = (bound[:, None, None] + first[:, :, None] - last[:, None, :]) < UNDERFLOW
    qi = jnp.arange(nb)[None, :, None]
    kj = jnp.arange(nb)[None, None, :]
    dead = dead & (kj < qi)
    kstart = jnp.sum(dead, axis=2).astype(jnp.int32)
    qend = (kj[0] + jnp.sum((~dead) & (qi > kj), axis=1)).astype(jnp.int32)
    return kstart.reshape(-1), qend.reshape(-1)


def _fox_fwd(qa, ka, va, kstart, name):
    t = qa.shape[0]
    bq = min(ATT_BLOCK, t)
    nq = t // bq

    def body(ks_ref, q_ref, k_ref, v_ref, o_ref, lse_ref):
        i = pl.program_id(1)
        hp = pl.program_id(0)
        k0 = [ks_ref[(2 * hp + j) * nq + i] for j in (0, 1)]
        both0 = jnp.maximum(k0[0], k0[1])

        def head(j, kb, m, acc, masked):
            rows = pl.ds(pl.multiple_of(kb * bq, bq), bq)
            cols = slice(128 * j, 128 * (j + 1))
            s = _dot(q_ref[:, cols], k_ref[rows, cols], "nt")
            if masked:
                s = jnp.where(_tile_mask(bq, False), s, -jnp.inf)
            m_new = jnp.maximum(m, jnp.max(s, axis=1, keepdims=True))
            return m_new, jnp.exp(m - m_new) * acc + _dot(jnp.exp(s - m_new), v_ref[rows, cols])

        def pair(kb, carry, masked):
            return head(0, kb, carry[0], carry[1], masked) + head(1, kb, carry[2], carry[3], masked)

        init = (jnp.full((bq, 1), -jnp.inf, F32), jnp.zeros((bq, 128), F32))
        alone = [lax.fori_loop(k0[j], both0, lambda kb, c, j=j: head(j, kb, c[0], c[1], False), init) for j in (0, 1)]
        carry = lax.fori_loop(both0, i, lambda kb, c: pair(kb, c, False), alone[0] + alone[1])
        carry = pair(i, carry, True)
        outs = []
        for j in (0, 1):
            m, acc = carry[2 * j], carry[2 * j + 1]
            spare = 64 * (1 - j)
            l = acc[:, spare:spare + 1]
            outs.append((acc / l, m + jnp.log(l)))
        msk = _head_mask((bq, 128), 0)
        o_ref[...] = jnp.where(msk, outs[0][0], outs[1][0]).astype(BF16)
        lse_ref[...] = jnp.where(msk, outs[0][1], outs[1][1])

    res = pl.BlockSpec((t, 256), lambda h, i, tbl: (0, h))
    out = pl.BlockSpec((bq, 128), lambda h, i, tbl: (i, h))
    return pl.pallas_call(
        body,
        name=name,
        grid_spec=pltpu.PrefetchScalarGridSpec(
            num_scalar_prefetch=1, grid=(4, nq),
            in_specs=[pl.BlockSpec((bq, 256), lambda h, i, tbl: (i, h)), res, res],
            out_specs=[out, out]),
        out_shape=[jax.ShapeDtypeStruct((t, A_WIDTH), BF16), jax.ShapeDtypeStruct((t, A_WIDTH), F32)],
        compiler_params=_params("parallel", "parallel"),
    )(kstart, qa, ka, va)


def _fox_bwd(qb, ka, va, dob, qend, name):
    t = qb.shape[0]
    bk = min(ATT_BLOCK, t)
    nk = t // bk

    def body(qe_ref, k_ref, v_ref, q_hbm, do_hbm, dk_ref, dv_ref, cs_ref, dq_hbm, rs_hbm, q_scr, do_scr, dq_scr,
             sems):
        jb = pl.program_id(1)
        hp = pl.program_id(0)
        pair_cols = pl.ds(pl.multiple_of(hp * 256, 256), 256)

        @pl.when(jb == 0)
        def _():
            loads = [pltpu.make_async_copy(q_hbm.at[:, pair_cols], q_scr, sems.at[0]),
                     pltpu.make_async_copy(do_hbm.at[:, pair_cols], do_scr, sems.at[1])]
            for cp in loads:
                cp.start()
            dq_scr[...] = jnp.zeros_like(dq_scr)
            for cp in loads:
                cp.wait()

        i1 = [qe_ref[(2 * hp + j) * nk + jb] + 1 for j in (0, 1)]
        both1 = jnp.minimum(i1[0], i1[1])

        def head(j, ib, dk_acc, dv_acc, masked):
            rows = pl.ds(pl.multiple_of(ib * bk, bk), bk)
            cols = slice(128 * j, 128 * (j + 1))
            qs = q_scr[rows, cols]
            dos = do_scr[rows, cols]
            kj = k_ref[:, cols]
            st = _dot(kj, qs, "nt")
            if masked:
                st = jnp.where(_tile_mask(bk, True), st, -jnp.inf)
            pt = jnp.exp(st)
            dst = (pt * _dot(v_ref[:, cols], dos, "nt")).astype(BF16)
            dq_scr[rows, cols] += _dot(dst, kj, "tn")
            return dk_acc + _dot(dst, qs), dv_acc + _dot(pt, dos)

        def pair(ib, carry, masked):
            return head(0, ib, carry[0], carry[1], masked) + head(1, ib, carry[2], carry[3], masked)

        carry = pair(jb, (jnp.zeros((bk, 128), F32),) * 4, True)
        carry = lax.fori_loop(jb + 1, both1, lambda ib, c: pair(ib, c, False), carry)
        alone = [lax.fori_loop(jnp.maximum(both1, jb + 1), i1[j],
                               lambda ib, c, j=j: head(j, ib, c[0], c[1], False), carry[2 * j:2 * j + 2])
                 for j in (0, 1)]
        carry = alone[0] + alone[1]
        outs = []
        for j in (0, 1):
            spare = 64 * (1 - j)
            dk_acc, dv_acc = carry[2 * j], carry[2 * j + 1]
            outs.append((dk_acc, dv_acc, dk_acc[:, spare + 3:spare + 4]))
        msk = _head_mask((bk, 128), 0)
        dk_ref[...] = jnp.where(msk, outs[0][0], outs[1][0]).astype(BF16)
        dv_ref[...] = jnp.where(msk, outs[0][1], outs[1][1]).astype(BF16)
        cs_ref[...] = jnp.where(msk, outs[0][2], outs[1][2])

        @pl.when(jb == nk - 1)
        def _():
            def finish(r, carry):
                rows = pl.ds(pl.multiple_of(r * bk, bk), bk)
                x0, x1 = dq_scr[rows, 0:128], dq_scr[rows, 128:256]
                q_scr[rows, 0:128] = (jnp.where(msk, x0, x1) * 0.125).astype(BF16)
                dq_scr[rows, 0:128] = jnp.where(msk, x0[:, 64:65], x1[:, 0:1])
                return carry

            lax.fori_loop(0, nk, finish, 0)
            head_cols = pl.ds(pl.multiple_of(hp * 128, 128), 128)
            stores = [pltpu.make_async_copy(q_scr.at[:, 0:128], dq_hbm.at[:, head_cols], sems.at[0]),
                      pltpu.make_async_copy(dq_scr.at[:, 0:128], rs_hbm.at[:, head_cols], sems.at[1])]
            for cp in stores:
                cp.start()
            for cp in stores:
                cp.wait()

    blk = pl.BlockSpec((bk, 256), lambda h, i, tbl: (i, h))
    out = pl.BlockSpec((bk, 128), lambda h, i, tbl: (i, h))
    return pl.pallas_call(
        body,
        name=name,
        grid_spec=pltpu.PrefetchScalarGridSpec(
            num_scalar_prefetch=1, grid=(4, nk), in_specs=[blk, blk, _ANY, _ANY],
            out_specs=[out, out, out, _ANY, _ANY],
            scratch_shapes=[pltpu.VMEM((t, 256), BF16), pltpu.VMEM((t, 256), BF16), pltpu.VMEM((t, 256), F32),
                            pltpu.SemaphoreType.DMA((2,))]),
        out_shape=[jax.ShapeDtypeStruct((t, A_WIDTH), BF16), jax.ShapeDtypeStruct((t, A_WIDTH), BF16),
                   jax.ShapeDtypeStruct((t, A_WIDTH), F32), jax.ShapeDtypeStruct((t, A_WIDTH), BF16),
                   jax.ShapeDtypeStruct((t, A_WIDTH), F32)],
        compiler_params=_params("arbitrary", "arbitrary"),
    )(qend, ka, va, qb, dob)


HG_ROWS = 256


def _hg_gates(hb_ref, rows, lbv):
    qb = hb_ref[rows, 0:B_WIDTH]
    fb = hb_ref[rows, B_WIDTH:2 * B_WIDTH]
    v = hb_ref[rows, 2 * B_WIDTH:3 * B_WIDTH]
    gb = hb_ref[rows, 3 * B_WIDTH:4 * B_WIDTH]
    sg = _sigmoid(fb)
    f = lbv + (1.0 - lbv) * sg
    sq = _sigmoid(qb)
    return qb, sq, qb * sq, sg, f, 1.0 - f, jnp.log(f), v, gb


def _hg_intra_factors(q, k, b):
    fac = []
    for i in range(CHUNK // SUB):
        bi = b[SUB * i:SUB * i + 1, :]
        eq = jnp.exp(b[SUB * i:SUB * (i + 1), :] - bi)
        ek = jnp.exp(jnp.minimum(bi - b, EXP_CLAMP))
        fac.append((eq, ek, q[SUB * i:SUB * (i + 1), :] * eq, k * ek))
    return fac


def _causal(n):
    r = lax.broadcasted_iota(jnp.int32, (n, n), 0)
    c = lax.broadcasted_iota(jnp.int32, (n, n), 1)
    return r >= c


def _hgrn_fwd(rest, lb, ng, name, ride=()):
    t = rest.shape[0]
    bt = min(HG_ROWS, t)
    ncb = bt // CHUNK
    n = len(ride)
    nsteps = t // bt

    def body(hb_ref, lb_ref, ng_ref, *refs):
        ride_in, (y_ref, o_ref, st_ref), ride_out = refs[:n], refs[n:n + 3], refs[n + 3:2 * n + 3]
        s_scr, sems = refs[2 * n + 3], refs[2 * n + 4:]

        @pl.when(pl.program_id(0) == 0)
        def _():
            s_scr[...] = jnp.zeros_like(s_scr)
            if n:
                _gather_start(ride_in, ride_out, sems)

        tril = _tri(CHUNK, True)
        causal = _causal(CHUNK)
        ones = jnp.ones((CHUNK, HD), F32)

        def chunk(c, carry):
            rows = pl.ds(pl.multiple_of(c * CHUNK, CHUNK), CHUNK)
            _, _, q_all, _, _, k_all, g_all, v_all, gb_all = _hg_gates(hb_ref, rows, lb_ref[...])
            b_all = _dot_hi(tril, g_all)
            qd_all = q_all * jnp.exp(b_all)
            kd_all = k_all * jnp.exp(b_all[CHUNK - 1:CHUNK, :] - b_all)
            eb_all = jnp.exp(_dot_hi(g_all, ones, "tn", exact="b"))
            sgb_all = _sigmoid(gb_all)
            for h in range(B_HEADS):
                cols = slice(h * HD, (h + 1) * HD)
                v = v_all[:, cols]
                s0 = s_scr[h]
                st_ref[c, h] = s0
                o = _dot(qd_all[:, cols], s0)
                fac = _hg_intra_factors(q_all[:, cols], k_all[:, cols], b_all[:, cols])
                a = jnp.concatenate([_dot(qe, ke, "nt") for _, _, qe, ke in fac], axis=0)
                o = o + _dot(jnp.where(causal, a, 0.0), v)
                s_scr[h] = eb_all[h * HD:(h + 1) * HD, :] * s0 + _dot(kd_all[:, cols], v, "tn")
                r = lax.rsqrt(jnp.mean(o * o, axis=-1, keepdims=True) + RMS_EPS)
                o_ref[rows, cols] = o
                y_ref[rows, cols] = (o * r * ng_ref[...] * sgb_all[:, cols]).astype(BF16)
            return carry

        lax.fori_loop(0, ncb, chunk, 0, unroll=2)

        if n:
            @pl.when(pl.program_id(0) == nsteps - 1)
            def _():
                _gather_finish(ride_in, ride_out, sems)

    res = pl.pallas_call(
        body,
        name=name,
        grid=(nsteps,),
        in_specs=[pl.BlockSpec((bt, 4 * B_WIDTH), lambda i: (i, 1)), pl.BlockSpec((1, B_WIDTH), lambda i: (0, 0)),
                  pl.BlockSpec((1, HD), lambda i: (0, 0))] + [_ANY] * n,
        out_specs=[pl.BlockSpec((bt, B_WIDTH), lambda i: (i, 0)), pl.BlockSpec((bt, B_WIDTH), lambda i: (i, 0)),
                   pl.BlockSpec((ncb, B_HEADS, HD, HD), lambda i: (i, 0, 0, 0))] + [_ANY] * n,
        out_shape=[jax.ShapeDtypeStruct((t, B_WIDTH), BF16), jax.ShapeDtypeStruct((t, B_WIDTH), F32),
                   jax.ShapeDtypeStruct((t // CHUNK, B_HEADS, HD, HD), F32)] + _gather_out_shapes(ride),
        scratch_shapes=[pltpu.VMEM((B_HEADS, HD, HD), F32)] + (_gather_scratch(n) if n else []),
        compiler_params=_params("arbitrary"),
    )(rest, lb, ng, *ride)
    return res[:3], res[3:]


def _hgrn_bwd(dy, rest, o_saved, states, lb, ng, name):
    t = rest.shape[0]
    bt = min(HG_ROWS, t)
    ncb = bt // CHUNK
    nb = t // bt

    def body(dy_ref, hb_ref, o_ref, st_ref, lb_ref, ng_ref, dh_ref, dlb_ref, dng_ref, ds_scr):
        @pl.when(pl.program_id(0) == 0)
        def _():
            ds_scr[...] = jnp.zeros_like(ds_scr)
            dlb_ref[...] = jnp.zeros_like(dlb_ref)
            dng_ref[...] = jnp.zeros_like(dng_ref)

        tril = _tri(CHUNK, True)
        triu = _tri(CHUNK, False)
        causal = _causal(CHUNK)
        ones = jnp.ones((CHUNK, HD), F32)
        ones8 = jnp.ones((8, HD), F32)
        last_row = lax.broadcasted_iota(jnp.int32, (CHUNK, B_WIDTH), 0) == CHUNK - 1

        def chunk(cc, carry):
            dng_acc, dlb_acc = carry
            c = ncb - 1 - cc
            rows = pl.ds(pl.multiple_of(c * CHUNK, CHUNK), CHUNK)
            lbv = lb_ref[...]
            qb, sq, q_all, sg, f, k_all, g_all, v_all, gb = _hg_gates(hb_ref, rows, lbv)
            b_all = _dot_hi(tril, g_all)
            ebt_all = jnp.exp(b_all)
            blast = b_all[CHUNK - 1:CHUNK, :]
            ekd_all = jnp.exp(blast - b_all)
            eb_all = jnp.exp(_dot_hi(g_all, ones, "tn", exact="b"))
            sgb = _sigmoid(gb)
            dy_all = dy_ref[rows, :].astype(F32)
            don_all = dy_all * sgb
            ngv = ng_ref[...]
            dq_l, dk_l, dks_l, dv_l, on_l, prod_l = [], [], [], [], [], []
            for h in range(B_HEADS):
                cols = slice(h * HD, (h + 1) * HD)
                q, k, v = q_all[:, cols], k_all[:, cols], v_all[:, cols]
                o = o_ref[rows, cols]
                don = don_all[:, cols]
                r = lax.rsqrt(jnp.mean(o * o, axis=-1, keepdims=True) + RMS_EPS)
                on_l.append(o * r * ngv)
                dng_acc = dng_acc + jnp.sum(don * o * r, axis=0, keepdims=True)
                doh = don * ngv
                do = r * (doh - o * (r * r) * jnp.mean(doh * o, axis=-1, keepdims=True))
                ebt, ekd = ebt_all[:, cols], ekd_all[:, cols]
                s0 = st_ref[c, h]
                ds1 = ds_scr[h]
                fac = _hg_intra_factors(q, k, b_all[:, cols])
                a = jnp.concatenate([_dot(qe, ke, "nt") for _, _, qe, ke in fac], axis=0)
                a = jnp.where(causal, a, 0.0)
                da = jnp.where(causal, _dot(do, v, "nt"), 0.0)
                dv_l.append(_dot(a, do, "tn") + _dot(k * ekd, ds1))
                dq = ebt * _dot(do, s0, "nt")
                dq_l.append(dq + jnp.concatenate(
                    [eq * _hdot(da[SUB * i:SUB * (i + 1), :], ke) for i, (eq, _, _, ke) in enumerate(fac)], axis=0))
                dk_state = ekd * _dot(v, ds1, "nt")
                dk = dk_state
                for i, (_, ek, qe, _) in enumerate(fac):
                    dk = dk + ek * _hdot(da[SUB * i:SUB * (i + 1), :], qe, "tn")
                dk_l.append(dk)
                dks_l.append(dk_state)
                prod_l.append(ds1 * s0)
                ds_scr[h] = _dot(q * ebt, do, "tn") + eb_all[h * HD:(h + 1) * HD, :] * ds1
            dq_all, dk_all = jnp.concatenate(dq_l, axis=1), jnp.concatenate(dk_l, axis=1)
            extra = jnp.exp(blast) * _dot_hi(ones8, jnp.concatenate(prod_l, axis=0), "nt")[0:1, :] \
                + jnp.sum(k_all * jnp.concatenate(dks_l, axis=1), axis=0, keepdims=True)
            db = q_all * dq_all - k_all * dk_all + jnp.where(last_row, extra, 0.0)
            df = _dot_hi(triu, db) / f - dk_all
            dlb_acc = dlb_acc + jnp.sum(df * (1.0 - sg), axis=0, keepdims=True)
            dh_ref[rows, 0:B_WIDTH] = (dq_all * (sq * (1.0 + qb * (1.0 - sq)))).astype(BF16)
            dh_ref[rows, B_WIDTH:2 * B_WIDTH] = (df * (1.0 - lbv) * sg * (1.0 - sg)).astype(BF16)
            dh_ref[rows, 2 * B_WIDTH:3 * B_WIDTH] = jnp.concatenate(dv_l, axis=1).astype(BF16)
            dh_ref[rows, 3 * B_WIDTH:4 * B_WIDTH] = (dy_all * jnp.concatenate(on_l, axis=1)
                                                     * sgb * (1.0 - sgb)).astype(BF16)
            return dng_acc, dlb_acc

        dng_sum, dlb_sum = lax.fori_loop(0, ncb, chunk, (jnp.zeros((1, HD), F32), jnp.zeros((1, B_WIDTH), F32)))
        dng_ref[...] += dng_sum
        dlb_ref[...] += dlb_sum

    rev = lambda i: (nb - 1 - i, 0)
    return pl.pallas_call(
        body,
        name=name,
        grid=(nb,),
        in_specs=[pl.BlockSpec((bt, B_WIDTH), rev), pl.BlockSpec((bt, 4 * B_WIDTH), lambda i: (nb - 1 - i, 1)),
                  pl.BlockSpec((bt, B_WIDTH), rev),
                  pl.BlockSpec((ncb, B_HEADS, HD, HD), lambda i: (nb - 1 - i, 0, 0, 0)),
                  pl.BlockSpec((1, B_WIDTH), lambda i: (0, 0)), pl.BlockSpec((1, HD), lambda i: (0, 0))],
        out_specs=[pl.BlockSpec((bt, 4 * B_WIDTH), rev), pl.BlockSpec((1, B_WIDTH), lambda i: (0, 0)),
                   pl.BlockSpec((1, HD), lambda i: (0, 0))],
        out_shape=[jax.ShapeDtypeStruct((t, 4 * B_WIDTH), BF16), jax.ShapeDtypeStruct((1, B_WIDTH), F32),
                   jax.ShapeDtypeStruct((1, HD), F32)],
        scratch_shapes=[pltpu.VMEM((B_HEADS, HD, HD), F32)],
        compiler_params=_params("arbitrary"),
    )(dy, rest, o_saved, states, lb, ng)


def _axpy2(c0, a0, c1, a1, name, tm=512):
    t, d = a0.shape
    tm = min(tm, t)

    def body(a_ref, b_ref, o_ref):
        o_ref[...] = c0 * a_ref[...] + c1 * b_ref[...]

    row = lambda i: (i, 0)
    return pl.pallas_call(
        body, name=name, grid=(t // tm,),
        in_specs=[pl.BlockSpec((tm, d), row), pl.BlockSpec((tm, d), row)],
        out_specs=pl.BlockSpec((tm, d), row),
        out_shape=jax.ShapeDtypeStruct((t, d), F32),
        compiler_params=_params("parallel"),
    )(a0, a1)


def _split_w_in(w_in_l):
    wqkv = w_in_l[:, :3 * A_WIDTH]
    wfa = jnp.pad(w_in_l[:, 3 * A_WIDTH:3 * A_WIDTH + A_HEADS], ((0, 0), (0, 128 - A_HEADS)))
    whb = w_in_l[:, 3 * A_WIDTH + A_HEADS:3 * A_WIDTH + A_HEADS + 4 * B_WIDTH]
    wgt = w_in_l[:, 3 * A_WIDTH + A_HEADS + 4 * B_WIDTH:]
    return wqkv, jnp.concatenate([wgt, whb, wfa], axis=1)


def _merge_w_in_grad(dwall):
    o = 3 * A_WIDTH
    return jnp.concatenate([dwall[:, :o], dwall[:, o + 4096:o + 4096 + A_HEADS], dwall[:, o + 2048:o + 4096],
                            dwall[:, o:o + 2048]], axis=1)


def _layer_fwd(x, xb, w, sp, l, ride=(), late_weights=None):
    t = x.shape[0]
    n = f"l{l}_"
    wqkv, wrest = _split_w_in(w["w_in"])
    qkv = _matmul(xb, wqkv, "nn", BF16, MM_ROWS, 768, D_MODEL, n + "proj_qkv")
    rest = _matmul(xb, wrest, "nn", F32, MM_ROWS, 1408, D_MODEL, n + "proj_rest")
    bf = jnp.pad(sp["b_fgate"], (0, 128 - A_HEADS)).reshape(1, 128)
    fcum, fcol = _fox_gate_fwd(rest, bf, n + "fox_gate_fwd")
    qa, ka, va, qn, kn = _fox_prep_fwd(qkv, fcol, n + "fox_prep_fwd")
    kstart, qend = _fox_block_ranges(qn, kn, fcum)
    ya, lse = _fox_fwd(qa, ka, va, kstart, n + "fox_fwd")
    lb = sp["lb"].reshape(1, B_WIDTH)
    ng = sp["norm_g"].reshape(1, HD)
    (yb, ob, states), gathered = _hgrn_fwd(rest, lb, ng, n + "hgrn_fwd", ride)
    if ride:
        late = late_weights(gathered)
        w = {**w, **late[l]}
    merged = _merge_fwd(ya, yb, w["w_pa"], w["w_pb"], rest, n + "merge_fwd")
    x1, x1b, xh1, rs1 = _mm_res_ln(merged, w["w_out"], x, sp["ln1_g"], sp["ln1_b"], n + "out_ln1")
    wu, wg = w["w_ff_in"][:, :FFN_HIDDEN], w["w_ff_in"][:, FFN_HIDDEN:]
    a, hu, hg = _ffn_in_swiglu(x1b, wu, wg, n + "ffn_in_swiglu")
    x2, x2b, xh2, rs2 = _mm_res_ln(a, w["w_ff_out"], x1, sp["ln2_g"], sp["ln2_b"], n + "ffn_out_ln2")
    saved = dict(xb=xb, wqkv=wqkv, wrest=wrest, qkv=qkv, rest=rest, bf=bf, fcol=fcol, ka=ka, va=va, ya=ya, lse=lse,
                 qend=qend,
                 lb=lb, ng=ng, yb=yb, ob=ob, states=states, merged=merged, x1b=x1b, xh1=xh1, rs1=rs1, a=a,
                 wu=wu, wg=wg, hu=hu, hg=hg,
                 xh2=xh2, rs2=rs2)
    return x2, x2b, saved, (late if ride else None)


def _layer_bwd(dys, coefs, w, sp, s, l):
    n = f"l{l}_"
    dz2, dz2b, dg2, db2 = _ln_bwd(dys, coefs, s["xh2"], s["rs2"], sp["ln2_g"], n + "ln2_bwd")
    du, dg = _ffn_out_dx_swiglu(dz2b, w["w_ff_out"], s["hu"], s["hg"], n + "ffn_out_dx_swiglu")
    d_wffout = _matmul(s["a"], dz2b, "tn", F32, 1408, 1024, DW_ROWS, n + "ffn_out_dw")
    dx1u = _matmul(du, s["wu"], "nt", F32, MM_ROWS, 1024, FFN_HIDDEN, n + "ffn_in_dx_u")
    dx1g = _matmul(dg, s["wg"], "nt", F32, MM_ROWS, 1024, FFN_HIDDEN, n + "ffn_in_dx_g")
    d_wffin = jnp.concatenate([_matmul(s["x1b"], du, "tn", F32, 1024, 1408, DW_ROWS, n + "ffn_in_dw_u"),
                               _matmul(s["x1b"], dg, "tn", F32, 1024, 1408, DW_ROWS, n + "ffn_in_dw_g")], axis=1)
    dz1, dz1b, dg1, db1 = _ln_bwd([dz2, dx1u, dx1g], [ALPHA, 1.0, 1.0], s["xh1"], s["rs1"], sp["ln1_g"],
                                  n + "ln1_bwd")
    d_wout = _matmul(s["merged"], dz1b, "tn", F32, 1024, 1024, DW_ROWS, n + "out_dw")
    dgates, dpa, dpb, dya, dyb = _merge_bwd(dz1b, w["w_out"], s["ya"], s["yb"], w["w_pa"], w["w_pb"], s["rest"],
                                  n + "merge_bwd")
    d_wpa = _matmul(s["ya"], dpa, "tn", F32, 512, 1024, DW_ROWS, n + "pa_dw")
    d_wpb = _matmul(s["yb"], dpb, "tn", F32, 512, 1024, DW_ROWS, n + "pb_dw")
    qb, dob = _fox_prep_bwd(s["qkv"], s["fcol"], s["lse"], dya, s["ya"], n + "fox_prep_bwd")
    dk, dv, csum, dq, rsum = _fox_bwd(qb, s["ka"], s["va"], dob, s["qend"], n + "fox_bwd")
    dfa, dbf = _fox_gate_bwd(rsum, csum, s["rest"], s["bf"], n + "fox_gate_bwd")
    dhb, dlb, dng = _hgrn_bwd(dyb, s["rest"], s["ob"], s["states"], s["lb"], s["ng"], n + "hgrn_bwd")
    dproj = jnp.concatenate([dq, dk, dv, dgates, dhb, dfa], axis=1)
    wall = jnp.concatenate([s["wqkv"], s["wrest"]], axis=1)
    dxp = _matmul(dproj, wall, "nt", F32, MM_ROWS, 1024, 1920, n + "proj_dx")
    d_wall = _matmul(s["xb"], dproj, "tn", F32, 1024, 1152, DW_ROWS, n + "proj_dw")
    grads = dict(w_in=_merge_w_in_grad(d_wall), w_pa=d_wpa, w_pb=d_wpb, w_out=d_wout, w_ff_in=d_wffin,
                 w_ff_out=d_wffout, b_fgate=dbf[0, :A_HEADS], lb=dlb[0], norm_g=dng[0], ln1_g=dg1[0], ln1_b=db1[0],
                 ln2_g=dg2[0], ln2_b=db2[0])
    return [dz1, dxp], [ALPHA, 1.0], grads


def _lower_bounds(logits):
    sm = jax.nn.softmax(logits.astype(F32), axis=0)
    return jnp.cumsum(sm, axis=0) - sm[0:1]


def _local_step(x, target, wfull, small, ride=(), late_weights=None):
    lbs, lb_vjp = jax.vjp(_lower_bounds, small["hgrn_lb_logits"])
    h, hb = x, x.astype(BF16)
    wfull = list(wfull)
    saved, sps = [], []
    for l in range(DEPTH):
        sp = dict(b_fgate=small["b_fgate"][l], lb=lbs[l], norm_g=small["hgrn_norm_g"][l], ln1_g=small["ln1_g"][l],
                  ln1_b=small["ln1_b"][l], ln2_g=small["ln2_g"][l], ln2_b=small["ln2_b"][l])
        h, hb, s, late = _layer_fwd(h, hb, wfull[l], sp, l, ride if l == 0 else (), late_weights)
        if late is not None:
            wfull = [{**wfull[k], **late[k]} for k in range(DEPTH)]
        saved.append(s)
        sps.append(sp)
    dy, lpart = _loss_head(h, target)
    dys, coefs = [dy], [1.0]
    grads = [None] * DEPTH
    for l in reversed(range(DEPTH)):
        dys, coefs, grads[l] = _layer_bwd(dys, coefs, wfull[l], sps[l], saved[l], l)
    grad_x = _axpy2(coefs[0], dys[0], coefs[1], dys[1], "grad_x")
    d_logits = lb_vjp(jnp.stack([grads[l]["lb"] for l in range(DEPTH)]))[0]
    return lpart[0, 0], grad_x, grads, d_logits


_BIG = [("w_in", "w_in", (D_MODEL, IN_TOTAL), 1), ("w_branch_a", "w_pa", (A_WIDTH, D_MODEL), 1),
        ("w_branch_b", "w_pb", (B_WIDTH, D_MODEL), 1), ("w_out", "w_out", (D_MODEL, D_MODEL), 0),
        ("w_ff_in", "w_ff_in", (D_MODEL, 2 * FFN_HIDDEN), 1), ("w_ff_out", "w_ff_out", (FFN_HIDDEN, D_MODEL), 0)]
_SMALL = [("b_fgate", A_HEADS), ("hgrn_lb_logits", B_WIDTH), ("hgrn_norm_g", HD), ("ln1_g", D_MODEL),
          ("ln1_b", D_MODEL), ("ln2_g", D_MODEL), ("ln2_b", D_MODEL)]
N_BIG = len(_BIG)
SMALL_ROWS = 80


def _by_chip(full, axis):
    if axis == 0:
        return full.reshape(N_CHIPS, full.shape[0] // N_CHIPS, full.shape[1])
    n = full.shape[1] // N_CHIPS
    return jnp.stack([full[:, q * n:(q + 1) * n] for q in range(N_CHIPS)])


def _from_chips(shards, axis):
    if axis == 0:
        return shards.reshape(N_CHIPS * shards.shape[1], shards.shape[2])
    return jnp.concatenate([shards[q] for q in range(N_CHIPS)], axis=1)


def _pack_small(per_name):
    flat = jnp.concatenate([per_name[name].reshape(-1) for name, _ in _SMALL])
    return jnp.pad(flat, (0, SMALL_ROWS * 128 - flat.shape[0])).reshape(SMALL_ROWS, 128)


def _unpack_small(slab):
    flat, out, r = slab.reshape(-1), {}, 0
    for name, n in _SMALL:
        out[name] = flat[r:r + DEPTH * n].reshape(DEPTH, n)
        r += DEPTH * n
    return out


_ANY = pl.BlockSpec(memory_space=pl.ANY)


def _place():
    return lax.axis_index("x"), lax.axis_index("y"), lax.axis_index("c")


def _other_chips(x, y):
    return [(1 - x, y), (x, 1 - y), (1 - x, 1 - y)]


def _chip_exchange(mine_of, out_refs, send_sems, recv_sems, local_sems):
    _chip_exchange_start(mine_of, out_refs, send_sems, recv_sems, local_sems)
    _chip_exchange_wait(mine_of, out_refs, send_sems, recv_sems, local_sems)


def _chip_exchange_copies(mine_of, out_refs, send_sems, recv_sems, local_sems):
    x, y, c = _place()
    q = 2 * x + y
    local = [pltpu.make_async_copy(mine_of(w, q), out_ref.at[q], local_sems.at[w]) for w, out_ref in enumerate(out_refs)]
    sends, recvs = [], []
    for k, (px, py) in enumerate(_other_chips(x, y)):
        for w, out_ref in enumerate(out_refs):
            sems = dict(send_sem=send_sems.at[3 * w + k], recv_sem=recv_sems.at[3 * w + k], device_id=(px, py, c),
                        device_id_type=MESH)
            sends.append(pltpu.make_async_remote_copy(src_ref=mine_of(w, 2 * px + py), dst_ref=out_ref.at[q], **sems))
            recvs.append(pltpu.make_async_remote_copy(src_ref=mine_of(w, q), dst_ref=out_ref.at[2 * px + py], **sems))
    return local, sends, recvs


def _chip_exchange_start(*args):
    local, sends, _ = _chip_exchange_copies(*args)
    for cp in local + sends:
        cp.start()


def _chip_exchange_wait(*args):
    local, sends, recvs = _chip_exchange_copies(*args)
    for cp in recvs:
        cp.wait_recv()
    for cp in sends:
        cp.wait_send()
    for cp in local:
        cp.wait()


def _sem_scratch(n):
    return [pltpu.SemaphoreType.DMA((3 * n,)), pltpu.SemaphoreType.DMA((3 * n,)), pltpu.SemaphoreType.DMA((n,))]


def _gather_scratch(n):
    return _sem_scratch(n) + [pltpu.SemaphoreType.DMA((n,)), pltpu.SemaphoreType.DMA((n,))]


def _gather_out_shapes(mine):
    return [jax.ShapeDtypeStruct((DEPTH, N_CHIPS) + m.shape[1:], m.dtype) for m in mine]


def _gather_start(in_refs, out_refs, sems):
    c = lax.axis_index("c")
    _chip_exchange_start(lambda w, q: in_refs[w].at[c], [o.at[c] for o in out_refs], *sems[:3])


def _gather_finish(in_refs, out_refs, sems):
    x, y, c = _place()
    _chip_exchange_wait(lambda w, q: in_refs[w].at[c], [o.at[c] for o in out_refs], *sems[:3])
    pair_send, pair_recv = sems[3:]
    sibling = (x, y, 1 - c)
    fwds = []
    for w, o in enumerate(out_refs):
        cp = pltpu.make_async_remote_copy(src_ref=o.at[c], dst_ref=o.at[c], send_sem=pair_send.at[w],
                                          recv_sem=pair_recv.at[w], device_id=sibling, device_id_type=MESH)
        cp.start()
        fwds.append(cp)
    for w, o in enumerate(out_refs):
        pltpu.make_async_remote_copy(src_ref=o.at[1 - c], dst_ref=o.at[1 - c], send_sem=pair_send.at[w],
                                     recv_sem=pair_recv.at[w], device_id=sibling, device_id_type=MESH).wait_recv()
    for cp in fwds:
        cp.wait_send()


def _gather_weights(mine):
    n = len(mine)

    def body(*refs):
        in_refs, out_refs, sems = refs[:n], refs[n:2 * n], refs[2 * n:]
        _gather_start(in_refs, out_refs, sems)
        _gather_finish(in_refs, out_refs, sems)

    return pl.pallas_call(
        body, name="gather_weights", in_specs=[_ANY] * n, out_specs=[_ANY] * n,
        out_shape=_gather_out_shapes(mine), scratch_shapes=_gather_scratch(n),
    )(*mine)


def _pair_exchange(gs):
    n = len(gs)

    def body(*refs):
        g_refs, a_refs, send_sems, recv_sems = refs[:n], refs[n:2 * n], refs[2 * n], refs[2 * n + 1]
        x, y, c = _place()
        cps = []
        for w in range(n):
            cp = pltpu.make_async_remote_copy(src_ref=g_refs[w].at[1 - c], dst_ref=a_refs[w], send_sem=send_sems.at[w],
                                              recv_sem=recv_sems.at[w], device_id=(x, y, 1 - c), device_id_type=MESH)
            cp.start()
            cps.append(cp)
        for cp in cps:
            cp.wait()

    return pl.pallas_call(
        body, name="grad_pair_exchange", in_specs=[_ANY] * n, out_specs=[_ANY] * n,
        out_shape=[jax.ShapeDtypeStruct(g.shape[1:], g.dtype) for g in gs],
        scratch_shapes=[pltpu.SemaphoreType.DMA((n,)), pltpu.SemaphoreType.DMA((n,))],
    )(*gs)


def _row_block(rows):
    return math.gcd(rows, 256)


def _pair_sum(g, a, layer, name):
    _, nq, rows, cols = g.shape
    tb = _row_block(rows)

    def body(l_ref, g_ref, a_ref, o_ref):
        o_ref[...] = (g_ref[...] + a_ref[...]).astype(BF16)

    return pl.pallas_call(
        body, name=name,
        grid_spec=pltpu.PrefetchScalarGridSpec(
            num_scalar_prefetch=1, grid=(nq, rows // tb),
            in_specs=[pl.BlockSpec((None, None, tb, cols), lambda q, i, l_ref: (l_ref[0], q, i, 0)),
                      pl.BlockSpec((None, tb, cols), lambda q, i, l_ref: (q, i, 0))],
            out_specs=pl.BlockSpec((None, tb, cols), lambda q, i, l_ref: (q, i, 0))),
        out_shape=jax.ShapeDtypeStruct((nq, rows, cols), BF16),
        compiler_params=_params("parallel", "parallel"),
    )(layer.reshape(1).astype(jnp.int32), g, a)


def _shard_exchange(ps):
    n = len(ps)

    def body(*refs):
        p_refs, b_refs = refs[:n], refs[n:2 * n]
        send_sems, recv_sems, local_sems = refs[2 * n:]
        _chip_exchange(lambda w, q: p_refs[w].at[q], b_refs, send_sems, recv_sems, local_sems)

    return pl.pallas_call(
        body, name="grad_shard_exchange", in_specs=[_ANY] * n, out_specs=[_ANY] * n,
        out_shape=[jax.ShapeDtypeStruct(p.shape, p.dtype) for p in ps],
        scratch_shapes=_sem_scratch(n),
    )(*ps)


def _sum4(b, name):
    _, rows, cols = b.shape
    tb = _row_block(rows)

    def body(b_ref, o_ref):
        o_ref[...] = ((b_ref[0].astype(F32) + b_ref[1].astype(F32)) + b_ref[2].astype(F32)) + b_ref[3].astype(F32)

    return pl.pallas_call(
        body, name=name, grid=(rows // tb,),
        in_specs=[pl.BlockSpec((N_CHIPS, tb, cols), lambda i: (0, i, 0))],
        out_specs=pl.BlockSpec((tb, cols), lambda i: (i, 0)),
        out_shape=jax.ShapeDtypeStruct((rows, cols), F32),
        compiler_params=_params("parallel"),
    )(b)


def _result_exchange(gcs):
    n = len(gcs)

    def body(*refs):
        g_refs, o_refs, send_sems, recv_sems = refs[:n], refs[n:2 * n], refs[2 * n], refs[2 * n + 1]
        x, y, c = _place()
        cps = []
        for w in range(n):
            cp = pltpu.make_async_remote_copy(src_ref=g_refs[w], dst_ref=o_refs[w], send_sem=send_sems.at[w],
                                              recv_sem=recv_sems.at[w], device_id=(x, y, 1 - c), device_id_type=MESH)
            cp.start()
            cps.append(cp)
        for cp in cps:
            cp.wait()

    return pl.pallas_call(
        body, name="grad_result_exchange", in_specs=[_ANY] * n, out_specs=[_ANY] * n,
        out_shape=[jax.ShapeDtypeStruct(g.shape, g.dtype) for g in gcs],
        scratch_shapes=[pltpu.SemaphoreType.DMA((n,)), pltpu.SemaphoreType.DMA((n,))],
    )(*gcs)


def _allreduce_small(v):
    def body(v_ref, o_ref, buf, send_sems, recv_sems):
        x, y, c = _place()
        me = 4 * x + 2 * y + c
        buf[me] = v_ref[...]
        peers = []
        for k in range(1, N_DEV):
            px = 1 - x if k & 4 else x
            py = 1 - y if k & 2 else y
            pc = 1 - c if k & 1 else c
            peers.append((px, py, pc))
        sends = []
        for k, peer in enumerate(peers):
            cp = pltpu.make_async_remote_copy(src_ref=v_ref, dst_ref=buf.at[me], send_sem=send_sems.at[k],
                                              recv_sem=recv_sems.at[k], device_id=peer, device_id_type=MESH)
            cp.start()
            sends.append(cp)
        for k, (px, py, pc) in enumerate(peers):
            pltpu.make_async_remote_copy(src_ref=v_ref, dst_ref=buf.at[4 * px + 2 * py + pc], send_sem=send_sems.at[k],
                                         recv_sem=recv_sems.at[k], device_id=(px, py, pc),
                                         device_id_type=MESH).wait_recv()
        for cp in sends:
            cp.wait_send()
        acc = buf[0]
        for i in range(1, N_DEV):
            acc = acc + buf[i]
        o_ref[...] = acc

    vm = pl.BlockSpec(memory_space=pltpu.VMEM)
    return pl.pallas_call(
        body, name="small_allreduce", in_specs=[vm], out_specs=vm,
        out_shape=jax.ShapeDtypeStruct(v.shape, F32),
        scratch_shapes=[pltpu.VMEM((N_DEV,) + v.shape, F32), pltpu.SemaphoreType.DMA((N_DEV - 1,)),
                        pltpu.SemaphoreType.DMA((N_DEV - 1,))],
    )(v)


def _adam_update(w, g, m, v):
    nm = ADAM_B1 * m + (1.0 - ADAM_B1) * g
    nv = ADAM_B2 * v + (1.0 - ADAM_B2) * (g * g)
    m_hat = nm / (1.0 - ADAM_B1 ** ADAM_STEP)
    v_hat = nv / (1.0 - ADAM_B2 ** ADAM_STEP)
    return -ADAM_LR * (m_hat / (jnp.sqrt(v_hat) + ADAM_EPS) + ADAM_WD * w), nm, nv


def _adamw_small(w, g, m, v, name):
    def body(w_ref, g_ref, m_ref, v_ref, d_ref, nm_ref, nv_ref):
        d_ref[...], nm_ref[...], nv_ref[...] = _adam_update(w_ref[...], g_ref[...], m_ref[...], v_ref[...])

    vm = pl.BlockSpec(memory_space=pltpu.VMEM)
    return pl.pallas_call(
        body, name=name, in_specs=[vm] * 4, out_specs=[vm] * 3,
        out_shape=[jax.ShapeDtypeStruct(w.shape, F32)] * 3,
    )(w, g, m, v)


def _adamw_big(w, m, v, g_own, g_other, layer, name):
    _, rows, cols = w.shape
    tb = _row_block(rows)

    def body(l_ref, w_ref, m_ref, v_ref, go_ref, gx_ref, g_ref, d_ref, nm_ref, nv_ref):
        gv = jnp.where(pl.program_id(0) == l_ref[0], go_ref[...], gx_ref[...])
        g_ref[...] = gv
        d_ref[...], nm_ref[...], nv_ref[...] = _adam_update(w_ref[...], gv, m_ref[...], v_ref[...])

    per_layer = pl.BlockSpec((None, tb, cols), lambda l, i, l_ref: (l, i, 0))
    shared = pl.BlockSpec((tb, cols), lambda l, i, l_ref: (i, 0))
    return pl.pallas_call(
        body, name=name,
        grid_spec=pltpu.PrefetchScalarGridSpec(
            num_scalar_prefetch=1, grid=(DEPTH, rows // tb),
            in_specs=[per_layer, per_layer, per_layer, shared, shared], out_specs=[per_layer] * 4),
        out_shape=[jax.ShapeDtypeStruct(w.shape, F32)] * 4,
        compiler_params=_params("parallel", "parallel"),
    )(layer.reshape(1).astype(jnp.int32), w, m, v, g_own, g_other)


def kernel(x, w_in, b_fgate, hgrn_lb_logits, hgrn_norm_g, w_branch_a, w_branch_b, w_out, ln1_g, ln1_b, w_ff_in, w_ff_out, ln2_g, ln2_b, loss_target, m_w_in, m_b_fgate, m_hgrn_lb_logits, m_hgrn_norm_g, m_w_branch_a, m_w_branch_b, m_w_out, m_ln1_g, m_ln1_b, m_w_ff_in, m_w_ff_out, m_ln2_g, m_ln2_b, v_w_in, v_b_fgate, v_hgrn_lb_logits, v_hgrn_norm_g, v_w_branch_a, v_w_branch_b, v_w_out, v_ln1_g, v_ln1_b, v_w_ff_in, v_w_ff_out, v_ln2_g, v_ln2_b):
    weights = dict(w_in=w_in, b_fgate=b_fgate, hgrn_lb_logits=hgrn_lb_logits, hgrn_norm_g=hgrn_norm_g,
                   w_branch_a=w_branch_a, w_branch_b=w_branch_b, w_out=w_out, ln1_g=ln1_g, ln1_b=ln1_b,
                   w_ff_in=w_ff_in, w_ff_out=w_ff_out, ln2_g=ln2_g, ln2_b=ln2_b)
    mom1 = dict(w_in=m_w_in, b_fgate=m_b_fgate, hgrn_lb_logits=m_hgrn_lb_logits, hgrn_norm_g=m_hgrn_norm_g,
                w_branch_a=m_w_branch_a, w_branch_b=m_w_branch_b, w_out=m_w_out, ln1_g=m_ln1_g, ln1_b=m_ln1_b,
                w_ff_in=m_w_ff_in, w_ff_out=m_w_ff_out, ln2_g=m_ln2_g, ln2_b=m_ln2_b)
    mom2 = dict(w_in=v_w_in, b_fgate=v_b_fgate, hgrn_lb_logits=v_hgrn_lb_logits, hgrn_norm_g=v_hgrn_norm_g,
                w_branch_a=v_w_branch_a, w_branch_b=v_w_branch_b, w_out=v_w_out, ln1_g=v_ln1_g, ln1_b=v_ln1_b,
                w_ff_in=v_w_ff_in, w_ff_out=v_w_ff_out, ln2_g=v_ln2_g, ln2_b=v_ln2_b)
    core = lax.axis_index("c")

    def full_w_in(gathered):
        return _from_chips(jnp.concatenate([gathered[0], gathered[1]], axis=1), 1)

    def late_weights(gathered):
        per_layer = [{key: _from_chips(gathered[1 + w][l], axis) for w, (_, key, _, axis) in enumerate(_BIG[1:])}
                     for l in range(DEPTH)]
        per_layer[1]["w_in"] = full_w_in(gathered[0])
        return per_layer

    w_in_halves = weights["w_in"].astype(BF16).reshape(DEPTH, 2, D_MODEL // 2, IN_TOTAL // N_CHIPS)
    wfull = [{"w_in": full_w_in(_gather_weights([w_in_halves[0]])[0])}, {}]
    ride = [w_in_halves[1]] + [weights[name].astype(BF16) for name, _, _, _ in _BIG[1:]]
    small = {name: weights[name] for name, _ in _SMALL}

    loss_part, grad_x, grads, d_logits = _local_step(x[0], loss_target[0], wfull, small, ride, late_weights)

    g_all = [jnp.stack([_by_chip(grads[l][key], axis) for l in range(DEPTH)]) for _, key, _, axis in _BIG]
    received = _pair_exchange(g_all)
    pair = [_pair_sum(g_all[w], received[w], core, f"grad_pair_sum_{w}") for w in range(N_BIG)]
    by_chip = _shard_exchange(pair)
    g_layer = [_sum4(by_chip[w], f"grad_chip_sum_{w}") for w in range(N_BIG)]
    g_other = _result_exchange(g_layer)
    out_g, out_d, out_m, out_v = {}, {}, {}, {}
    for w, (name, _, _, _) in enumerate(_BIG):
        out_g[name], out_d[name], out_m[name], out_v[name] = _adamw_big(
            weights[name], mom1[name], mom2[name], g_layer[w], g_other[w], core, f"adamw_{name}")

    small_grads = {name: jnp.stack([grads[l][key] for l in range(DEPTH)])
                   for name, key in [("b_fgate", "b_fgate"), ("hgrn_norm_g", "norm_g"), ("ln1_g", "ln1_g"),
                                     ("ln1_b", "ln1_b"), ("ln2_g", "ln2_g"), ("ln2_b", "ln2_b")]}
    small_grads["hgrn_lb_logits"] = d_logits
    gs = _allreduce_small(_pack_small(small_grads))
    ds, ms, vs = _adamw_small(_pack_small(small), gs, _pack_small({n: mom1[n] for n, _ in _SMALL}),
                              _pack_small({n: mom2[n] for n, _ in _SMALL}), "adamw_small")
    for tree, slab in ((out_g, gs), (out_d, ds), (out_m, ms), (out_v, vs)):
        tree.update(_unpack_small(slab))

    loss = lax.psum(loss_part, ("x", "y", "c"))
    order = ["w_in", "b_fgate", "hgrn_lb_logits", "hgrn_norm_g", "w_branch_a", "w_branch_b", "w_out", "ln1_g", "ln1_b",
             "w_ff_in", "w_ff_out", "ln2_g", "ln2_b"]
    return (loss, grad_x[None], *[out_g[n] for n in order], *[out_d[n] for n in order],
            *[out_m[n] for n in order], *[out_v[n] for n in order])
```

```python
import math

import jax
import jax.numpy as jnp
from jax import lax
from jax.experimental import pallas as pl
from jax.experimental.pallas import tpu as pltpu

F32 = jnp.float32
BF16 = jnp.bfloat16

D_MODEL = 1024
DEPTH = 2
A_HEADS = 8
A_WIDTH = 512
B_WIDTH = 512
B_HEADS = 4
HD = 128
CHUNK = 64
SUB = 16
FFN_HIDDEN = 2816
IN_TOTAL = 5640
ALPHA = (2 * DEPTH) ** 0.25
LN_EPS = 1e-5
RMS_EPS = 1e-6
ADAM_LR = 0.001
ADAM_B1 = 0.9
ADAM_B2 = 0.999
ADAM_EPS = 1e-08
ADAM_WD = 0.01
ADAM_STEP = 10
EXP_CLAMP = 60.0

VMEM_LIMIT_BYTES = 56 * 1024 * 1024
MM_ROWS = 1024
DW_ROWS = 2048
N_CHIPS = 4
N_DEV = 8
MESH = pl.DeviceIdType.MESH

_DN = {
    "nn": (((1,), (0,)), ((), ())),
    "nt": (((1,), (1,)), ((), ())),
    "tn": (((0,), (0,)), ((), ())),
}


def _dot(a, b, mode="nn"):
    return lax.dot_general(a.astype(BF16), b.astype(BF16), _DN[mode], preferred_element_type=F32)


def _pieces(x):
    h = x.astype(BF16)
    r = x - h.astype(F32)
    m = r.astype(BF16)
    return h, m, (r - m.astype(F32)).astype(BF16)


def _dot_hi(a, b, mode="nn", exact="a"):
    if exact == "a":
        h, m, l = _pieces(b)
        return (_dot(a, l, mode) + _dot(a, m, mode)) + _dot(a, h, mode)
    h, m, l = _pieces(a)
    return (_dot(l, b, mode) + _dot(m, b, mode)) + _dot(h, b, mode)


def _hdot(a, b, mode="nn"):
    bh, bl, _ = _pieces(b)
    return _dot(a, bl, mode) + _dot(a, bh, mode)


def _params(*sem):
    return pltpu.CompilerParams(dimension_semantics=sem, vmem_limit_bytes=VMEM_LIMIT_BYTES)


def _sigmoid(x):
    return 1.0 / (1.0 + jnp.exp(-x))


def _matmul(a, b, mode, out_dtype, tm, tn, tk, name):
    if mode == "nn":
        (m, k), (k2, n) = a.shape, b.shape
    elif mode == "nt":
        (m, k), (n, k2) = a.shape, b.shape
    else:
        (k, m), (k2, n) = a.shape, b.shape
    assert k == k2, (a.shape, b.shape, mode)
    tm, tn, tk = min(tm, m), min(tn, n), min(tk, k)
    assert m % tm == 0 and n % tn == 0 and k % tk == 0, (a.shape, b.shape, tm, tn, tk)
    nk = k // tk
    if mode == "tn":
        a_spec = pl.BlockSpec((tk, tm), lambda j, i, kk: (kk, i))
    else:
        a_spec = pl.BlockSpec((tm, tk), lambda j, i, kk: (i, kk))
    if mode == "nt":
        b_spec = pl.BlockSpec((tn, tk), lambda j, i, kk: (j, kk))
    else:
        b_spec = pl.BlockSpec((tk, tn), lambda j, i, kk: (kk, j))
    use_acc = nk > 1 and out_dtype != F32

    def body(a_ref, b_ref, o_ref, *scratch):
        p = _dot(a_ref[...], b_ref[...], mode)
        if nk == 1:
            o_ref[...] = p.astype(out_dtype)
            return
        acc_ref = scratch[0] if use_acc else o_ref
        kk = pl.program_id(2)

        @pl.when(kk == 0)
        def _():
            acc_ref[...] = p

        @pl.when(kk > 0)
        def _():
            acc_ref[...] += p

        if use_acc:
            @pl.when(kk == nk - 1)
            def _():
                o_ref[...] = acc_ref[...].astype(out_dtype)

    return pl.pallas_call(
        body,
        name=name,
        grid=(n // tn, m // tm, nk),
        in_specs=[a_spec, b_spec],
        out_specs=pl.BlockSpec((tm, tn), lambda j, i, kk: (i, j)),
        out_shape=jax.ShapeDtypeStruct((m, n), out_dtype),
        scratch_shapes=[pltpu.VMEM((tm, tn), F32)] if use_acc else [],
        compiler_params=_params("parallel", "parallel", "arbitrary"),
    )(a, b)


def _mm_res_ln(a, w, res, g, b, name, tm=512):
    t, k = a.shape
    d = w.shape[1]
    tm = min(tm, t)

    def body(a_ref, w_ref, r_ref, g_ref, b_ref, y_ref, yb_ref, xh_ref, rs_ref):
        z = ALPHA * r_ref[...] + _dot(a_ref[...], w_ref[...])
        mu = jnp.mean(z, axis=-1, keepdims=True)
        zc = z - mu
        var = jnp.mean(zc * zc, axis=-1, keepdims=True)
        rstd = lax.rsqrt(var + LN_EPS)
        xh = zc * rstd
        y = xh * g_ref[...] + b_ref[...]
        y_ref[...] = y
        yb_ref[...] = y.astype(BF16)
        xh_ref[...] = xh
        rs_ref[...] = rstd

    row = lambda i: (i, 0)
    fix = lambda i: (0, 0)
    return pl.pallas_call(
        body,
        name=name,
        grid=(t // tm,),
        in_specs=[pl.BlockSpec((tm, k), row), pl.BlockSpec((k, d), fix), pl.BlockSpec((tm, d), row),
                  pl.BlockSpec((1, d), fix), pl.BlockSpec((1, d), fix)],
        out_specs=[pl.BlockSpec((tm, d), row), pl.BlockSpec((tm, d), row), pl.BlockSpec((tm, d), row),
                   pl.BlockSpec((tm, 1), row)],
        out_shape=[jax.ShapeDtypeStruct((t, d), F32), jax.ShapeDtypeStruct((t, d), BF16),
                   jax.ShapeDtypeStruct((t, d), F32), jax.ShapeDtypeStruct((t, 1), F32)],
        compiler_params=_params("parallel"),
    )(a, w, res, g.reshape(1, d), b.reshape(1, d))


def _ln_bwd(dys, coefs, xhat, rstd, g, name, tm=512):
    t, d = xhat.shape
    tm = min(tm, t)
    n_in = len(dys)

    def body(*refs):
        dy_refs = refs[:n_in]
        xh_ref, rs_ref, g_ref, dz_ref, dzb_ref, dg_ref, db_ref = refs[n_in:]
        dy = coefs[0] * dy_refs[0][...].astype(F32)
        for c, r in zip(coefs[1:], dy_refs[1:]):
            dy = dy + c * r[...].astype(F32)
        xh = xh_ref[...]
        dxh = dy * g_ref[...]
        m1 = jnp.mean(dxh, axis=-1, keepdims=True)
        m2 = jnp.mean(dxh * xh, axis=-1, keepdims=True)
        dz = rs_ref[...] * (dxh - m1 - xh * m2)
        dz_ref[...] = dz
        dzb_ref[...] = dz.astype(BF16)
        pg = jnp.sum(dy * xh, axis=0, keepdims=True)
        pb = jnp.sum(dy, axis=0, keepdims=True)

        @pl.when(pl.program_id(0) == 0)
        def _():
            dg_ref[...] = pg
            db_ref[...] = pb

        @pl.when(pl.program_id(0) > 0)
        def _():
            dg_ref[...] += pg
            db_ref[...] += pb

    row = lambda i: (i, 0)
    fix = lambda i: (0, 0)
    return pl.pallas_call(
        body,
        name=name,
        grid=(t // tm,),
        in_specs=[pl.BlockSpec((tm, d), row)] * n_in
        + [pl.BlockSpec((tm, d), row), pl.BlockSpec((tm, 1), row), pl.BlockSpec((1, d), fix)],
        out_specs=[pl.BlockSpec((tm, d), row), pl.BlockSpec((tm, d), row), pl.BlockSpec((1, d), fix),
                   pl.BlockSpec((1, d), fix)],
        out_shape=[jax.ShapeDtypeStruct((t, d), F32), jax.ShapeDtypeStruct((t, d), BF16),
                   jax.ShapeDtypeStruct((1, d), F32), jax.ShapeDtypeStruct((1, d), F32)],
        compiler_params=_params("arbitrary"),
    )(*dys, xhat, rstd, g.reshape(1, d))


def _loss_head(y, target, name="loss_head", tm=512):
    t, d = y.shape
    tm = min(tm, t)

    def body(y_ref, t_ref, dy_ref, l_ref):
        e = y_ref[...] - t_ref[...]
        dy_ref[...] = e * (1.0 / d)
        part = jnp.full((8, 128), 0.5 / d, F32) * jnp.sum(e * e)

        @pl.when(pl.program_id(0) == 0)
        def _():
            l_ref[...] = part

        @pl.when(pl.program_id(0) > 0)
        def _():
            l_ref[...] += part

    row = lambda i: (i, 0)
    return pl.pallas_call(
        body,
        name=name,
        grid=(t // tm,),
        in_specs=[pl.BlockSpec((tm, d), row), pl.BlockSpec((tm, d), row)],
        out_specs=[pl.BlockSpec((tm, d), row), pl.BlockSpec((8, 128), lambda i: (0, 0))],
        out_shape=[jax.ShapeDtypeStruct((t, d), F32), jax.ShapeDtypeStruct((8, 128), F32)],
        compiler_params=_params("arbitrary"),
    )(y, target)


FFN_COLS = FFN_HIDDEN // 2


def _ffn_in_swiglu(xb, wu, wg, name, tm=MM_ROWS):
    t, d = xb.shape
    tm = min(tm, t)

    def body(x_ref, wu_ref, wg_ref, a_ref, u_ref, g_ref):
        x = x_ref[...]
        u = _dot(x, wu_ref[...])
        g = _dot(x, wg_ref[...])
        u_ref[...] = u.astype(BF16)
        g_ref[...] = g.astype(BF16)
        a_ref[...] = (g * _sigmoid(g) * u).astype(BF16)

    wspec = pl.BlockSpec((d, FFN_COLS), lambda j, i: (0, j))
    out = pl.BlockSpec((tm, FFN_COLS), lambda j, i: (i, j))
    return pl.pallas_call(
        body,
        name=name,
        grid=(FFN_HIDDEN // FFN_COLS, t // tm),
        in_specs=[pl.BlockSpec((tm, d), lambda j, i: (i, 0)), wspec, wspec],
        out_specs=[out, out, out],
        out_shape=[jax.ShapeDtypeStruct((t, FFN_HIDDEN), BF16)] * 3,
        compiler_params=_params("parallel", "parallel"),
    )(xb, wu, wg)


def _ffn_out_dx_swiglu(dzb, w_ff_out, u, g, name, tm=MM_ROWS):
    t, d = dzb.shape
    tm = min(tm, t)

    def body(dz_ref, w_ref, u_ref, g_ref, du_ref, dg_ref):
        da = _dot(dz_ref[...], w_ref[...], "nt")
        gv = g_ref[...].astype(F32)
        sg = _sigmoid(gv)
        du_ref[...] = (da * gv * sg).astype(BF16)
        dg_ref[...] = (da * u_ref[...].astype(F32) * (sg * (1.0 + gv * (1.0 - sg)))).astype(BF16)

    blk = pl.BlockSpec((tm, FFN_COLS), lambda j, i: (i, j))
    return pl.pallas_call(
        body,
        name=name,
        grid=(FFN_HIDDEN // FFN_COLS, t // tm),
        in_specs=[pl.BlockSpec((tm, d), lambda j, i: (i, 0)), pl.BlockSpec((FFN_COLS, d), lambda j, i: (j, 0)), blk, blk],
        out_specs=[blk, blk],
        out_shape=[jax.ShapeDtypeStruct((t, FFN_HIDDEN), BF16)] * 2,
        compiler_params=_params("parallel", "parallel"),
    )(dzb, w_ff_out, u, g)


def _merge_fwd(ya, yb, wpa, wpb, rest, name, tm=512):
    t = ya.shape[0]
    tm = min(tm, t)

    def body(ya_ref, yb_ref, wa_ref, wb_ref, ga_ref, gb_ref, o_ref):
        pa = _dot(ya_ref[...], wa_ref[...])
        pb = _dot(yb_ref[...], wb_ref[...])
        o_ref[...] = (_sigmoid(ga_ref[...]) * pa + _sigmoid(gb_ref[...]) * pb).astype(BF16)

    row = lambda i: (i, 0)
    fix = lambda i: (0, 0)
    return pl.pallas_call(
        body,
        name=name,
        grid=(t // tm,),
        in_specs=[pl.BlockSpec((tm, A_WIDTH), row), pl.BlockSpec((tm, B_WIDTH), row),
                  pl.BlockSpec((A_WIDTH, D_MODEL), fix), pl.BlockSpec((B_WIDTH, D_MODEL), fix),
                  pl.BlockSpec((tm, D_MODEL), lambda i: (i, 0)), pl.BlockSpec((tm, D_MODEL), lambda i: (i, 1))],
        out_specs=pl.BlockSpec((tm, D_MODEL), row),
        out_shape=jax.ShapeDtypeStruct((t, D_MODEL), BF16),
        compiler_params=_params("parallel"),
    )(ya, yb, wpa, wpb, rest, rest)


def _merge_bwd(dzb, w_out, ya, yb, wpa, wpb, rest, name, tm=512):
    t = ya.shape[0]
    tm = min(tm, t)

    def body(dz_ref, wo_ref, ya_ref, yb_ref, wa_ref, wb_ref, ga_ref, gb_ref, dg_ref, dpa_ref, dpb_ref, dya_ref,
             dyb_ref):
        dm_v = _dot(dz_ref[...], wo_ref[...], "nt")
        pa = _dot(ya_ref[...], wa_ref[...])
        pb = _dot(yb_ref[...], wb_ref[...])
        sa = _sigmoid(ga_ref[...])
        sb = _sigmoid(gb_ref[...])
        dg_ref[:, :D_MODEL] = (dm_v * pa * sa * (1.0 - sa)).astype(BF16)
        dg_ref[:, D_MODEL:] = (dm_v * pb * sb * (1.0 - sb)).astype(BF16)
        dpa = (dm_v * sa).astype(BF16)
        dpb = (dm_v * sb).astype(BF16)
        dpa_ref[...] = dpa
        dpb_ref[...] = dpb
        dya_ref[...] = _dot(dpa, wa_ref[...], "nt").astype(BF16)
        dyb_ref[...] = _dot(dpb, wb_ref[...], "nt")

    row = lambda i: (i, 0)
    fix = lambda i: (0, 0)
    return pl.pallas_call(
        body,
        name=name,
        grid=(t // tm,),
        in_specs=[pl.BlockSpec((tm, D_MODEL), row), pl.BlockSpec((D_MODEL, D_MODEL), fix),
                  pl.BlockSpec((tm, A_WIDTH), row), pl.BlockSpec((tm, B_WIDTH), row),
                  pl.BlockSpec((A_WIDTH, D_MODEL), fix), pl.BlockSpec((B_WIDTH, D_MODEL), fix),
                  pl.BlockSpec((tm, D_MODEL), lambda i: (i, 0)), pl.BlockSpec((tm, D_MODEL), lambda i: (i, 1))],
        out_specs=[pl.BlockSpec((tm, 2 * D_MODEL), row), pl.BlockSpec((tm, D_MODEL), row),
                   pl.BlockSpec((tm, D_MODEL), row), pl.BlockSpec((tm, A_WIDTH), row), pl.BlockSpec((tm, B_WIDTH), row)],
        out_shape=[jax.ShapeDtypeStruct((t, 2 * D_MODEL), BF16), jax.ShapeDtypeStruct((t, D_MODEL), BF16),
                   jax.ShapeDtypeStruct((t, D_MODEL), BF16), jax.ShapeDtypeStruct((t, A_WIDTH), BF16),
                   jax.ShapeDtypeStruct((t, B_WIDTH), F32)],
        compiler_params=_params("parallel"),
    )(dzb, w_out, ya, yb, wpa, wpb, rest, rest)


FA_BLOCK = 4224 // 128 - 1


def _tri(n, lower):
    r = lax.broadcasted_iota(jnp.int32, (n, n), 0)
    c = lax.broadcasted_iota(jnp.int32, (n, n), 1)
    return jnp.where((r >= c) if lower else (r <= c), 1.0, 0.0).astype(F32)


def _head_spread(expand):
    shape = (128, A_WIDTH) if expand else (A_WIDTH, 128)
    r = lax.broadcasted_iota(jnp.int32, shape, 0)
    c = lax.broadcasted_iota(jnp.int32, shape, 1)
    hit = ((c >= 64 * r) & (c < 64 * r + 64)) if expand else (r == 64 * c)
    return jnp.where(hit, 1.0, 0.0).astype(F32)


def _fox_gate_fwd(rest, bf, name, tb=512):
    t = rest.shape[0]
    tb = min(tb, t)

    def body(fa_ref, bf_ref, f_ref, fc_ref, carry):
        @pl.when(pl.program_id(0) == 0)
        def _():
            carry[...] = jnp.zeros_like(carry)

        z = fa_ref[...] + bf_ref[...]
        logf = jnp.minimum(z, 0.0) - jnp.log(1.0 + jnp.exp(-jnp.abs(z)))
        f = _dot_hi(_tri(tb, True), logf) + carry[...]
        f_ref[...] = f
        fc_ref[...] = _dot_hi(f, _head_spread(True), exact="b")
        carry[...] = f[tb - 1:tb, :]

    return pl.pallas_call(
        body,
        name=name,
        grid=(t // tb,),
        in_specs=[pl.BlockSpec((tb, 128), lambda i: (i, FA_BLOCK)), pl.BlockSpec((1, 128), lambda i: (0, 0))],
        out_specs=[pl.BlockSpec((tb, 128), lambda i: (i, 0)), pl.BlockSpec((tb, A_WIDTH), lambda i: (i, 0))],
        out_shape=[jax.ShapeDtypeStruct((t, 128), F32), jax.ShapeDtypeStruct((t, A_WIDTH), F32)],
        scratch_shapes=[pltpu.VMEM((1, 128), F32)],
        compiler_params=_params("arbitrary"),
    )(rest, bf)


def _fox_gate_bwd(rsum, csum, rest, bf, name, tb=512):
    t = rest.shape[0]
    tb = min(tb, t)
    nb = t // tb

    def body(rs_ref, cs_ref, fa_ref, bf_ref, dfa_ref, dbf_ref, carry):
        @pl.when(pl.program_id(0) == 0)
        def _():
            carry[...] = jnp.zeros_like(carry)

        d_f = _dot_hi(rs_ref[...] - cs_ref[...], _head_spread(False), exact="b")
        dlogf = _dot_hi(_tri(tb, False), d_f) + carry[...]
        carry[...] = dlogf[0:1, :]
        z = fa_ref[...] + bf_ref[...]
        dz = dlogf * _sigmoid(-z)
        dfa_ref[...] = dz.astype(BF16)
        part = jnp.sum(dz, axis=0, keepdims=True)

        @pl.when(pl.program_id(0) == 0)
        def _():
            dbf_ref[...] = part

        @pl.when(pl.program_id(0) > 0)
        def _():
            dbf_ref[...] += part

    return pl.pallas_call(
        body,
        name=name,
        grid=(nb,),
        in_specs=[pl.BlockSpec((tb, A_WIDTH), lambda i: (nb - 1 - i, 0)),
                  pl.BlockSpec((tb, A_WIDTH), lambda i: (nb - 1 - i, 0)),
                  pl.BlockSpec((tb, 128), lambda i: (nb - 1 - i, FA_BLOCK)),
                  pl.BlockSpec((1, 128), lambda i: (0, 0))],
        out_specs=[pl.BlockSpec((tb, 128), lambda i: (nb - 1 - i, 0)), pl.BlockSpec((1, 128), lambda i: (0, 0))],
        out_shape=[jax.ShapeDtypeStruct((t, 128), BF16), jax.ShapeDtypeStruct((1, 128), F32)],
        scratch_shapes=[pltpu.VMEM((1, 128), F32)],
        compiler_params=_params("arbitrary"),
    )(rsum, csum, rest, bf)


ATT_BLOCK = 512
FWD_Q_BLOCKS = 2
BWD_K_BLOCKS = 2


def _head_mask(shape, j):
    lane = lax.broadcasted_iota(jnp.int32, shape, 1)
    return (lane < 64) if j == 0 else (lane >= 64)


def _aug_lanes(tb, j):
    lane = lax.broadcasted_iota(jnp.int32, (tb, 128), 1)
    own = (lane < 64) if j == 0 else (lane >= 64)
    return own, lane - 64 * (1 - j)


def _aug_query(own, li, q, pieces):
    h, m, l = pieces
    one, zero = jnp.ones_like(h), jnp.zeros_like(h)
    spare = jnp.where(li == 0, h, jnp.where(li == 1, m, jnp.where(li == 2, l, jnp.where(li < 6, one, zero))))
    return jnp.where(own, q, spare)


def _fox_prep_fwd(qkv, fcol, name, tb=2048):
    t = qkv.shape[0]
    tb = min(tb, t)

    def body(q_ref, k_ref, v_ref, fc_ref, qa_ref, ka_ref, va_ref, qn_ref, kn_ref):
        pieces = _pieces(pltpu.roll(fc_ref[...], 64, 1))
        h, m, l = pieces
        q, k, v = q_ref[...], k_ref[...], v_ref[...]
        first = _head_mask((tb, 128), 0)
        for nrm_ref, x in ((qn_ref, q.astype(F32)), (kn_ref, k.astype(F32))):
            n0 = jnp.max(jnp.sum(jnp.where(first, x * x, 0.0), axis=1, keepdims=True))
            n1 = jnp.max(jnp.sum(jnp.where(first, 0.0, x * x), axis=1, keepdims=True))
            nrm_ref[...] = jnp.where(_head_mask((8, 128), 0), n0, n1)
        one, zero = jnp.ones_like(h), jnp.zeros_like(h)
        for j in (0, 1):
            own, li = _aug_lanes(tb, j)
            cols = slice(128 * j, 128 * (j + 1))
            qa_ref[:, cols] = _aug_query(own, li, q * 0.125, pieces)
            ks = jnp.where(li < 3, one, jnp.where(li == 3, -h, jnp.where(li == 4, -m, jnp.where(li == 5, -l, zero))))
            ka_ref[:, cols] = jnp.where(own, k, ks)
            va_ref[:, cols] = jnp.where(own, v, one)

    blk = pl.BlockSpec((tb, 256), lambda i, h: (i, h))
    nrm = pl.BlockSpec((None, None, 8, 128), lambda i, h: (i, h, 0, 0))
    return pl.pallas_call(
        body, name=name, grid=(t // tb, 4),
        in_specs=[pl.BlockSpec((tb, 128), lambda i, h: (i, h)), pl.BlockSpec((tb, 128), lambda i, h: (i, 4 + h)),
                  pl.BlockSpec((tb, 128), lambda i, h: (i, 8 + h)), pl.BlockSpec((tb, 128), lambda i, h: (i, h))],
        out_specs=[blk, blk, blk, nrm, nrm],
        out_shape=[jax.ShapeDtypeStruct((t, 2 * A_WIDTH), BF16)] * 3
        + [jax.ShapeDtypeStruct((t // tb, 4, 8, 128), F32)] * 2,
        compiler_params=_params("parallel", "parallel"),
    )(qkv, qkv, qkv, fcol)


def _fox_prep_bwd(qkv, fcol, lse, do, o, name, tb=2048):
    t = qkv.shape[0]
    tb = min(tb, t)

    def body(q_ref, fc_ref, lse_ref, do_ref, o_ref, qb_ref, dob_ref):
        pieces = _pieces(pltpu.roll(fc_ref[...] - lse_ref[...], 64, 1))
        q = q_ref[...] * 0.125
        do_v = do_ref[...]
        prod = do_v.astype(F32) * o_ref[...].astype(F32)
        for j in (0, 1):
            own, li = _aug_lanes(tb, j)
            cols = slice(128 * j, 128 * (j + 1))
            qb_ref[:, cols] = _aug_query(own, li, q, pieces)
            delta = jnp.sum(jnp.where(own, prod, 0.0), axis=1, keepdims=True)
            h, m, l = _pieces(jnp.broadcast_to(delta, (tb, 128)))
            ds = jnp.where(li == 0, -h, jnp.where(li == 1, -m, jnp.where(li == 2, -l, jnp.zeros_like(h))))
            dob_ref[:, cols] = jnp.where(own, do_v, ds)

    pair = pl.BlockSpec((tb, 128), lambda i, h: (i, h))
    blk = pl.BlockSpec((tb, 256), lambda i, h: (i, h))
    return pl.pallas_call(
        body, name=name, grid=(t // tb, 4),
        in_specs=[pair, pair, pair, pair, pair],
        out_specs=[blk, blk],
        out_shape=[jax.ShapeDtypeStruct((t, 2 * A_WIDTH), BF16)] * 2,
        compiler_params=_params("parallel", "parallel"),
    )(qkv, fcol, lse, do, o)


def _tile_mask(n, transposed):
    r = lax.broadcasted_iota(jnp.int32, (n, n), 0)
    c = lax.broadcasted_iota(jnp.int32, (n, n), 1)
    return (c >= r) if transposed else (r >= c)


UNDERFLOW = -110.0


def _fox_block_ranges(qn, kn, fcum):
    t = fcum.shape[0]
    blk = min(ATT_BLOCK, t)
    nb = t // blk
    q2 = jnp.max(qn[:, :, 0, ::64].reshape(-1, A_HEADS), axis=0)
    k2 = jnp.max(kn[:, :, 0, ::64].reshape(-1, A_HEADS), axis=0)
    bound = 2.0 * jnp.sqrt(q2 * k2) * 0.125
    f = fcum[:, :A_HEADS]
    first = f[0::blk].T
    last = f[blk - 1::blk].T
    dead = (bound[:, None, None] + first[:, :, None] - last[:, None, :]) < UNDERFLOW
    qi = jnp.arange(nb)[None, :, None]
    kj = jnp.arange(nb)[None, None, :]
    dead = dead & (kj < qi)
    kstart = jnp.sum(dead, axis=2).astype(jnp.int32)
    qend = (kj[0] + jnp.sum((~dead) & (qi > kj), axis=1)).astype(jnp.int32)
    return kstart.reshape(-1), qend.reshape(-1)


def _fox_fwd(qa, ka, va, kstart, name):
    t = qa.shape[0]
    bk = min(ATT_BLOCK, t)
    nk = t // bk
    qf = FWD_Q_BLOCKS if t % (FWD_Q_BLOCKS * bk) == 0 else 1
    bq = qf * bk
    nq = t // bq

    def body(ks_ref, q_ref, k_ref, v_ref, o_ref, lse_ref):
        i = pl.program_id(1)
        hp = pl.program_id(0)
        k0 = [ks_ref[(2 * hp + j) * nk + qf * i] for j in (0, 1)]
        both0 = jnp.maximum(k0[0], k0[1])

        def head(j, kb, m, acc, diag):
            rows = pl.ds(pl.multiple_of(kb * bk, bk), bk)
            cols = slice(128 * j, 128 * (j + 1))
            s = _dot(q_ref[:, cols], k_ref[rows, cols], "nt")
            if diag is not None:
                r = lax.broadcasted_iota(jnp.int32, (bq, bk), 0)
                c = lax.broadcasted_iota(jnp.int32, (bq, bk), 1)
                s = jnp.where(r - c >= diag, s, -jnp.inf)
            m_new = jnp.maximum(m, jnp.max(s, axis=1, keepdims=True))
            return m_new, jnp.exp(m - m_new) * acc + _dot(jnp.exp(s - m_new), v_ref[rows, cols])

        def pair(kb, carry, diag):
            return head(0, kb, carry[0], carry[1], diag) + head(1, kb, carry[2], carry[3], diag)

        init = (jnp.full((bq, 1), -jnp.inf, F32), jnp.zeros((bq, 128), F32))
        alone = [lax.fori_loop(k0[j], both0, lambda kb, c, j=j: head(j, kb, c[0], c[1], None), init) for j in (0, 1)]
        carry = lax.fori_loop(both0, qf * i, lambda kb, c: pair(kb, c, None), alone[0] + alone[1])
        for d in range(qf):
            carry = pair(qf * i + d, carry, d * bk)
        outs = []
        for j in (0, 1):
            m, acc = carry[2 * j], carry[2 * j + 1]
            spare = 64 * (1 - j)
            l = acc[:, spare:spare + 1]
            outs.append((acc / l, m + jnp.log(l)))
        msk = _head_mask((bq, 128), 0)
        o_ref[...] = jnp.where(msk, outs[0][0], outs[1][0]).astype(BF16)
        lse_ref[...] = jnp.where(msk, outs[0][1], outs[1][1])

    res = pl.BlockSpec((t, 256), lambda h, i, tbl: (0, h))
    out = pl.BlockSpec((bq, 128), lambda h, i, tbl: (i, h))
    return pl.pallas_call(
        body,
        name=name,
        grid_spec=pltpu.PrefetchScalarGridSpec(
            num_scalar_prefetch=1, grid=(4, nq),
            in_specs=[pl.BlockSpec((bq, 256), lambda h, i, tbl: (i, h)), res, res],
            out_specs=[out, out]),
        out_shape=[jax.ShapeDtypeStruct((t, A_WIDTH), BF16), jax.ShapeDtypeStruct((t, A_WIDTH), F32)],
        compiler_params=_params("parallel", "parallel"),
    )(kstart, qa, ka, va)


def _fox_bwd(qb, ka, va, dob, qend, name):
    t = qb.shape[0]
    bq = min(ATT_BLOCK, t)
    nq = t // bq
    kf = BWD_K_BLOCKS if t % (BWD_K_BLOCKS * bq) == 0 else 1
    bk = kf * bq
    nk = t // bk

    def body(qe_ref, k_ref, v_ref, q_hbm, do_hbm, dk_ref, dv_ref, cs_ref, dq_hbm, rs_hbm, q_scr, do_scr, dq_scr,
             sems):
        jb = pl.program_id(1)
        hp = pl.program_id(0)
        pair_cols = pl.ds(pl.multiple_of(hp * 256, 256), 256)

        @pl.when(jb == 0)
        def _():
            loads = [pltpu.make_async_copy(q_hbm.at[:, pair_cols], q_scr, sems.at[0]),
                     pltpu.make_async_copy(do_hbm.at[:, pair_cols], do_scr, sems.at[1])]
            for cp in loads:
                cp.start()
            dq_scr[...] = jnp.zeros_like(dq_scr)
            for cp in loads:
                cp.wait()

        i1 = [qe_ref[(2 * hp + j) * nq + kf * jb + kf - 1] + 1 for j in (0, 1)]
        both1 = jnp.minimum(i1[0], i1[1])

        def head(j, ib, dk_acc, dv_acc, diag):
            rows = pl.ds(pl.multiple_of(ib * bq, bq), bq)
            cols = slice(128 * j, 128 * (j + 1))
            qs = q_scr[rows, cols]
            dos = do_scr[rows, cols]
            kj = k_ref[:, cols]
            st = _dot(kj, qs, "nt")
            if diag is not None:
                r = lax.broadcasted_iota(jnp.int32, (bk, bq), 0)
                c = lax.broadcasted_iota(jnp.int32, (bk, bq), 1)
                st = jnp.where(c - r >= -diag, st, -jnp.inf)
            pt = jnp.exp(st)
            dst = (pt * _dot(v_ref[:, cols], dos, "nt")).astype(BF16)
            dq_scr[rows, cols] += _dot(dst, kj, "tn")
            return dk_acc + _dot(dst, qs), dv_acc + _dot(pt, dos)

        def pair(ib, carry, diag):
            return head(0, ib, carry[0], carry[1], diag) + head(1, ib, carry[2], carry[3], diag)

        carry = (jnp.zeros((bk, 128), F32),) * 4
        for d in range(kf):
            carry = pair(kf * jb + d, carry, d * bq)
        first = kf * jb + kf
        carry = lax.fori_loop(first, both1, lambda ib, c: pair(ib, c, None), carry)
        alone = [lax.fori_loop(jnp.maximum(both1, first), i1[j],
                               lambda ib, c, j=j: head(j, ib, c[0], c[1], None), carry[2 * j:2 * j + 2])
                 for j in (0, 1)]
        carry = alone[0] + alone[1]
        outs = []
        for j in (0, 1):
            spare = 64 * (1 - j)
            dk_acc, dv_acc = carry[2 * j], carry[2 * j + 1]
            outs.append((dk_acc, dv_acc, dk_acc[:, spare + 3:spare + 4]))
        msk = _head_mask((bk, 128), 0)
        dk_ref[...] = jnp.where(msk, outs[0][0], outs[1][0]).astype(BF16)
        dv_ref[...] = jnp.where(msk, outs[0][1], outs[1][1]).astype(BF16)
        cs_ref[...] = jnp.where(msk, outs[0][2], outs[1][2])

        @pl.when(jb == nk - 1)
        def _():
            first_head = _head_mask((bq, 128), 0)

            def finish(r, carry):
                rows = pl.ds(pl.multiple_of(r * bq, bq), bq)
                x0, x1 = dq_scr[rows, 0:128], dq_scr[rows, 128:256]
                q_scr[rows, 0:128] = (jnp.where(first_head, x0, x1) * 0.125).astype(BF16)
                dq_scr[rows, 0:128] = jnp.where(first_head, x0[:, 64:65], x1[:, 0:1])
                return carry

            lax.fori_loop(0, nq, finish, 0)
            head_cols = pl.ds(pl.multiple_of(hp * 128, 128), 128)
            stores = [pltpu.make_async_copy(q_scr.at[:, 0:128], dq_hbm.at[:, head_cols], sems.at[0]),
                      pltpu.make_async_copy(dq_scr.at[:, 0:128], rs_hbm.at[:, head_cols], sems.at[1])]
            for cp in stores:
                cp.start()
            for cp in stores:
                cp.wait()

    blk = pl.BlockSpec((bk, 256), lambda h, i, tbl: (i, h))
    out = pl.BlockSpec((bk, 128), lambda h, i, tbl: (i, h))
    return pl.pallas_call(
        body,
        name=name,
        grid_spec=pltpu.PrefetchScalarGridSpec(
            num_scalar_prefetch=1, grid=(4, nk), in_specs=[blk, blk, _ANY, _ANY],
            out_specs=[out, out, out, _ANY, _ANY],
            scratch_shapes=[pltpu.VMEM((t, 256), BF16), pltpu.VMEM((t, 256), BF16), pltpu.VMEM((t, 256), F32),
                            pltpu.SemaphoreType.DMA((2,))]),
        out_shape=[jax.ShapeDtypeStruct((t, A_WIDTH), BF16), jax.ShapeDtypeStruct((t, A_WIDTH), BF16),
                   jax.ShapeDtypeStruct((t, A_WIDTH), F32), jax.ShapeDtypeStruct((t, A_WIDTH), BF16),
                   jax.ShapeDtypeStruct((t, A_WIDTH), F32)],
        compiler_params=_params("arbitrary", "arbitrary"),
    )(qend, ka, va, qb, dob)


HG_ROWS = 256


def _hg_gates(hb_ref, rows, lbv):
    qb = hb_ref[rows, 0:B_WIDTH]
    fb = hb_ref[rows, B_WIDTH:2 * B_WIDTH]
    v = hb_ref[rows, 2 * B_WIDTH:3 * B_WIDTH]
    gb = hb_ref[rows, 3 * B_WIDTH:4 * B_WIDTH]
    sg = _sigmoid(fb)
    f = lbv + (1.0 - lbv) * sg
    sq = _sigmoid(qb)
    return qb, sq, qb * sq, sg, f, 1.0 - f, jnp.log(f), v, gb


def _hg_intra_factors(q, k, b):
    fac = []
    for i in range(CHUNK // SUB):
        bi = b[SUB * i:SUB * i + 1, :]
        eq = jnp.exp(b[SUB * i:SUB * (i + 1), :] - bi)
        ek = jnp.exp(jnp.minimum(bi - b, EXP_CLAMP))
        fac.append((eq, ek, q[SUB * i:SUB * (i + 1), :] * eq, k * ek))
    return fac


def _causal(n):
    r = lax.broadcasted_iota(jnp.int32, (n, n), 0)
    c = lax.broadcasted_iota(jnp.int32, (n, n), 1)
    return r >= c


def _hgrn_fwd(rest, lb, ng, name, ride=()):
    t = rest.shape[0]
    bt = min(HG_ROWS, t)
    ncb = bt // CHUNK
    n = len(ride)
    nsteps = t // bt

    def body(hb_ref, lb_ref, ng_ref, *refs):
        ride_in, (y_ref, o_ref, st_ref), ride_out = refs[:n], refs[n:n + 3], refs[n + 3:2 * n + 3]
        s_scr, sems = refs[2 * n + 3], refs[2 * n + 4:]

        @pl.when(pl.program_id(0) == 0)
        def _():
            s_scr[...] = jnp.zeros_like(s_scr)
            if n:
                _gather_start(ride_in, ride_out, sems)

        tril = _tri(CHUNK, True)
        causal = _causal(CHUNK)
        ones = jnp.ones((CHUNK, HD), F32)

        def chunk(c, carry):
            rows = pl.ds(pl.multiple_of(c * CHUNK, CHUNK), CHUNK)
            _, _, q_all, _, _, k_all, g_all, v_all, gb_all = _hg_gates(hb_ref, rows, lb_ref[...])
            b_all = _dot_hi(tril, g_all)
            qd_all = q_all * jnp.exp(b_all)
            kd_all = k_all * jnp.exp(b_all[CHUNK - 1:CHUNK, :] - b_all)
            eb_all = jnp.exp(_dot_hi(g_all, ones, "tn", exact="b"))
            sgb_all = _sigmoid(gb_all)
            for h in range(B_HEADS):
                cols = slice(h * HD, (h + 1) * HD)
                v = v_all[:, cols]
                s0 = s_scr[h]
                st_ref[c, h] = s0
                o = _dot(qd_all[:, cols], s0)
                fac = _hg_intra_factors(q_all[:, cols], k_all[:, cols], b_all[:, cols])
                a = jnp.concatenate([_dot(qe, ke, "nt") for _, _, qe, ke in fac], axis=0)
                o = o + _dot(jnp.where(causal, a, 0.0), v)
                s_scr[h] = eb_all[h * HD:(h + 1) * HD, :] * s0 + _dot(kd_all[:, cols], v, "tn")
                r = lax.rsqrt(jnp.mean(o * o, axis=-1, keepdims=True) + RMS_EPS)
                o_ref[rows, cols] = o
                y_ref[rows, cols] = (o * r * ng_ref[...] * sgb_all[:, cols]).astype(BF16)
            return carry

        lax.fori_loop(0, ncb, chunk, 0, unroll=2)

        if n:
            @pl.when(pl.program_id(0) == nsteps - 1)
            def _():
                _gather_finish(ride_in, ride_out, sems)

    res = pl.pallas_call(
        body,
        name=name,
        grid=(nsteps,),
        in_specs=[pl.BlockSpec((bt, 4 * B_WIDTH), lambda i: (i, 1)), pl.BlockSpec((1, B_WIDTH), lambda i: (0, 0)),
                  pl.BlockSpec((1, HD), lambda i: (0, 0))] + [_ANY] * n,
        out_specs=[pl.BlockSpec((bt, B_WIDTH), lambda i: (i, 0)), pl.BlockSpec((bt, B_WIDTH), lambda i: (i, 0)),
                   pl.BlockSpec((ncb, B_HEADS, HD, HD), lambda i: (i, 0, 0, 0))] + [_ANY] * n,
        out_shape=[jax.ShapeDtypeStruct((t, B_WIDTH), BF16), jax.ShapeDtypeStruct((t, B_WIDTH), F32),
                   jax.ShapeDtypeStruct((t // CHUNK, B_HEADS, HD, HD), F32)] + _gather_out_shapes(ride),
        scratch_shapes=[pltpu.VMEM((B_HEADS, HD, HD), F32)] + (_gather_scratch(n) if n else []),
        compiler_params=_params("arbitrary"),
    )(rest, lb, ng, *ride)
    return res[:3], res[3:]


def _hgrn_bwd(dy, rest, o_saved, states, lb, ng, name):
    t = rest.shape[0]
    bt = min(HG_ROWS, t)
    ncb = bt // CHUNK
    nb = t // bt

    def body(dy_ref, hb_ref, o_ref, st_ref, lb_ref, ng_ref, dh_ref, dlb_ref, dng_ref, ds_scr):
        @pl.when(pl.program_id(0) == 0)
        def _():
            ds_scr[...] = jnp.zeros_like(ds_scr)
            dlb_ref[...] = jnp.zeros_like(dlb_ref)
            dng_ref[...] = jnp.zeros_like(dng_ref)

        tril = _tri(CHUNK, True)
        triu = _tri(CHUNK, False)
        causal = _causal(CHUNK)
        ones = jnp.ones((CHUNK, HD), F32)
        ones8 = jnp.ones((8, HD), F32)
        last_row = lax.broadcasted_iota(jnp.int32, (CHUNK, B_WIDTH), 0) == CHUNK - 1

        def chunk(cc, carry):
            dng_acc, dlb_acc = carry
            c = ncb - 1 - cc
            rows = pl.ds(pl.multiple_of(c * CHUNK, CHUNK), CHUNK)
            lbv = lb_ref[...]
            qb, sq, q_all, sg, f, k_all, g_all, v_all, gb = _hg_gates(hb_ref, rows, lbv)
            b_all = _dot_hi(tril, g_all)
            ebt_all = jnp.exp(b_all)
            blast = b_all[CHUNK - 1:CHUNK, :]
            ekd_all = jnp.exp(blast - b_all)
            eb_all = jnp.exp(_dot_hi(g_all, ones, "tn", exact="b"))
            sgb = _sigmoid(gb)
            dy_all = dy_ref[rows, :].astype(F32)
            don_all = dy_all * sgb
            ngv = ng_ref[...]
            dq_l, dk_l, dks_l, dv_l, on_l, prod_l = [], [], [], [], [], []
            for h in range(B_HEADS):
                cols = slice(h * HD, (h + 1) * HD)
                q, k, v = q_all[:, cols], k_all[:, cols], v_all[:, cols]
                o = o_ref[rows, cols]
                don = don_all[:, cols]
                r = lax.rsqrt(jnp.mean(o * o, axis=-1, keepdims=True) + RMS_EPS)
                on_l.append(o * r * ngv)
                dng_acc = dng_acc + jnp.sum(don * o * r, axis=0, keepdims=True)
                doh = don * ngv
                do = r * (doh - o * (r * r) * jnp.mean(doh * o, axis=-1, keepdims=True))
                ebt, ekd = ebt_all[:, cols], ekd_all[:, cols]
                s0 = st_ref[c, h]
                ds1 = ds_scr[h]
                fac = _hg_intra_factors(q, k, b_all[:, cols])
                a = jnp.concatenate([_dot(qe, ke, "nt") for _, _, qe, ke in fac], axis=0)
                a = jnp.where(causal, a, 0.0)
                da = jnp.where(causal, _dot(do, v, "nt"), 0.0)
                dv_l.append(_dot(a, do, "tn") + _dot(k * ekd, ds1))
                dq = ebt * _dot(do, s0, "nt")
                dq_l.append(dq + jnp.concatenate(
                    [eq * _hdot(da[SUB * i:SUB * (i + 1), :], ke) for i, (eq, _, _, ke) in enumerate(fac)], axis=0))
                dk_state = ekd * _dot(v, ds1, "nt")
                dk = dk_state
                for i, (_, ek, qe, _) in enumerate(fac):
                    dk = dk + ek * _hdot(da[SUB * i:SUB * (i + 1), :], qe, "tn")
                dk_l.append(dk)
                dks_l.append(dk_state)
                prod_l.append(ds1 * s0)
                ds_scr[h] = _dot(q * ebt, do, "tn") + eb_all[h * HD:(h + 1) * HD, :] * ds1
            dq_all, dk_all = jnp.concatenate(dq_l, axis=1), jnp.concatenate(dk_l, axis=1)
            extra = jnp.exp(blast) * _dot_hi(ones8, jnp.concatenate(prod_l, axis=0), "nt")[0:1, :] \
                + jnp.sum(k_all * jnp.concatenate(dks_l, axis=1), axis=0, keepdims=True)
            db = q_all * dq_all - k_all * dk_all + jnp.where(last_row, extra, 0.0)
            df = _dot_hi(triu, db) / f - dk_all
            dlb_acc = dlb_acc + jnp.sum(df * (1.0 - sg), axis=0, keepdims=True)
            dh_ref[rows, 0:B_WIDTH] = (dq_all * (sq * (1.0 + qb * (1.0 - sq)))).astype(BF16)
            dh_ref[rows, B_WIDTH:2 * B_WIDTH] = (df * (1.0 - lbv) * sg * (1.0 - sg)).astype(BF16)
            dh_ref[rows, 2 * B_WIDTH:3 * B_WIDTH] = jnp.concatenate(dv_l, axis=1).astype(BF16)
            dh_ref[rows, 3 * B_WIDTH:4 * B_WIDTH] = (dy_all * jnp.concatenate(on_l, axis=1)
                                                     * sgb * (1.0 - sgb)).astype(BF16)
            return dng_acc, dlb_acc

        dng_sum, dlb_sum = lax.fori_loop(0, ncb, chunk, (jnp.zeros((1, HD), F32), jnp.zeros((1, B_WIDTH), F32)))
        dng_ref[...] += dng_sum
        dlb_ref[...] += dlb_sum

    rev = lambda i: (nb - 1 - i, 0)
    return pl.pallas_call(
        body,
        name=name,
        grid=(nb,),
        in_specs=[pl.BlockSpec((bt, B_WIDTH), rev), pl.BlockSpec((bt, 4 * B_WIDTH), lambda i: (nb - 1 - i, 1)),
                  pl.BlockSpec((bt, B_WIDTH), rev),
                  pl.BlockSpec((ncb, B_HEADS, HD, HD), lambda i: (nb - 1 - i, 0, 0, 0)),
                  pl.BlockSpec((1, B_WIDTH), lambda i: (0, 0)), pl.BlockSpec((1, HD), lambda i: (0, 0))],
        out_specs=[pl.BlockSpec((bt, 4 * B_WIDTH), rev), pl.BlockSpec((1, B_WIDTH), lambda i: (0, 0)),
                   pl.BlockSpec((1, HD), lambda i: (0, 0))],
        out_shape=[jax.ShapeDtypeStruct((t, 4 * B_WIDTH), BF16), jax.ShapeDtypeStruct((1, B_WIDTH), F32),
                   jax.ShapeDtypeStruct((1, HD), F32)],
        scratch_shapes=[pltpu.VMEM((B_HEADS, HD, HD), F32)],
        compiler_params=_params("arbitrary"),
    )(dy, rest, o_saved, states, lb, ng)


def _axpy2(c0, a0, c1, a1, name, tm=512):
    t, d = a0.shape
    tm = min(tm, t)

    def body(a_ref, b_ref, o_ref):
        o_ref[...] = c0 * a_ref[...] + c1 * b_ref[...]

    row = lambda i: (i, 0)
    return pl.pallas_call(
        body, name=name, grid=(t // tm,),
        in_specs=[pl.BlockSpec((tm, d), row), pl.BlockSpec((tm, d), row)],
        out_specs=pl.BlockSpec((tm, d), row),
        out_shape=jax.ShapeDtypeStruct((t, d), F32),
        compiler_params=_params("parallel"),
    )(a0, a1)


def _split_w_in(w_in_l):
    wqkv = w_in_l[:, :3 * A_WIDTH]
    wfa = jnp.pad(w_in_l[:, 3 * A_WIDTH:3 * A_WIDTH + A_HEADS], ((0, 0), (0, 128 - A_HEADS)))
    whb = w_in_l[:, 3 * A_WIDTH + A_HEADS:3 * A_WIDTH + A_HEADS + 4 * B_WIDTH]
    wgt = w_in_l[:, 3 * A_WIDTH + A_HEADS + 4 * B_WIDTH:]
    return wqkv, jnp.concatenate([wgt, whb, wfa], axis=1)


def _merge_w_in_grad(dwall):
    o = 3 * A_WIDTH
    return jnp.concatenate([dwall[:, :o], dwall[:, o + 4096:o + 4096 + A_HEADS], dwall[:, o + 2048:o + 4096],
                            dwall[:, o:o + 2048]], axis=1)


def _layer_fwd(x, xb, w, sp, l, ride=(), late_weights=None):
    t = x.shape[0]
    n = f"l{l}_"
    wqkv, wrest = _split_w_in(w["w_in"])
    qkv = _matmul(xb, wqkv, "nn", BF16, MM_ROWS, 768, D_MODEL, n + "proj_qkv")
    rest = _matmul(xb, wrest, "nn", F32, MM_ROWS, 1408, D_MODEL, n + "proj_rest")
    bf = jnp.pad(sp["b_fgate"], (0, 128 - A_HEADS)).reshape(1, 128)
    fcum, fcol = _fox_gate_fwd(rest, bf, n + "fox_gate_fwd")
    qa, ka, va, qn, kn = _fox_prep_fwd(qkv, fcol, n + "fox_prep_fwd")
    kstart, qend = _fox_block_ranges(qn, kn, fcum)
    ya, lse = _fox_fwd(qa, ka, va, kstart, n + "fox_fwd")
    lb = sp["lb"].reshape(1, B_WIDTH)
    ng = sp["norm_g"].reshape(1, HD)
    (yb, ob, states), gathered = _hgrn_fwd(rest, lb, ng, n + "hgrn_fwd", ride)
    if ride:
        late = late_weights(gathered)
        w = {**w, **late[l]}
    merged = _merge_fwd(ya, yb, w["w_pa"], w["w_pb"], rest, n + "merge_fwd")
    x1, x1b, xh1, rs1 = _mm_res_ln(merged, w["w_out"], x, sp["ln1_g"], sp["ln1_b"], n + "out_ln1")
    wu, wg = w["w_ff_in"][:, :FFN_HIDDEN], w["w_ff_in"][:, FFN_HIDDEN:]
    a, hu, hg = _ffn_in_swiglu(x1b, wu, wg, n + "ffn_in_swiglu")
    x2, x2b, xh2, rs2 = _mm_res_ln(a, w["w_ff_out"], x1, sp["ln2_g"], sp["ln2_b"], n + "ffn_out_ln2")
    saved = dict(xb=xb, wqkv=wqkv, wrest=wrest, qkv=qkv, rest=rest, bf=bf, fcol=fcol, ka=ka, va=va, ya=ya, lse=lse,
                 qend=qend,
                 lb=lb, ng=ng, yb=yb, ob=ob, states=states, merged=merged, x1b=x1b, xh1=xh1, rs1=rs1, a=a,
                 wu=wu, wg=wg, hu=hu, hg=hg,
                 xh2=xh2, rs2=rs2)
    return x2, x2b, saved, (late if ride else None)


def _layer_bwd(dys, coefs, w, sp, s, l):
    n = f"l{l}_"
    dz2, dz2b, dg2, db2 = _ln_bwd(dys, coefs, s["xh2"], s["rs2"], sp["ln2_g"], n + "ln2_bwd")
    du, dg = _ffn_out_dx_swiglu(dz2b, w["w_ff_out"], s["hu"], s["hg"], n + "ffn_out_dx_swiglu")
    d_wffout = _matmul(s["a"], dz2b, "tn", F32, 1408, 1024, DW_ROWS, n + "ffn_out_dw")
    dx1u = _matmul(du, s["wu"], "nt", F32, MM_ROWS, 1024, FFN_HIDDEN, n + "ffn_in_dx_u")
    dx1g = _matmul(dg, s["wg"], "nt", F32, MM_ROWS, 1024, FFN_HIDDEN, n + "ffn_in_dx_g")
    d_wffin = jnp.concatenate([_matmul(s["x1b"], du, "tn", F32, 1024, 1408, DW_ROWS, n + "ffn_in_dw_u"),
                               _matmul(s["x1b"], dg, "tn", F32, 1024, 1408, DW_ROWS, n + "ffn_in_dw_g")], axis=1)
    dz1, dz1b, dg1, db1 = _ln_bwd([dz2, dx1u, dx1g], [ALPHA, 1.0, 1.0], s["xh1"], s["rs1"], sp["ln1_g"],
                                  n + "ln1_bwd")
    d_wout = _matmul(s["merged"], dz1b, "tn", F32, 1024, 1024, DW_ROWS, n + "out_dw")
    dgates, dpa, dpb, dya, dyb = _merge_bwd(dz1b, w["w_out"], s["ya"], s["yb"], w["w_pa"], w["w_pb"], s["rest"],
                                  n + "merge_bwd")
    d_wpa = _matmul(s["ya"], dpa, "tn", F32, 512, 1024, DW_ROWS, n + "pa_dw")
    d_wpb = _matmul(s["yb"], dpb, "tn", F32, 512, 1024, DW_ROWS, n + "pb_dw")
    qb, dob = _fox_prep_bwd(s["qkv"], s["fcol"], s["lse"], dya, s["ya"], n + "fox_prep_bwd")
    dk, dv, csum, dq, rsum = _fox_bwd(qb, s["ka"], s["va"], dob, s["qend"], n + "fox_bwd")
    dfa, dbf = _fox_gate_bwd(rsum, csum, s["rest"], s["bf"], n + "fox_gate_bwd")
    dhb, dlb, dng = _hgrn_bwd(dyb, s["rest"], s["ob"], s["states"], s["lb"], s["ng"], n + "hgrn_bwd")
    dproj = jnp.concatenate([dq, dk, dv, dgates, dhb, dfa], axis=1)
    wall = jnp.concatenate([s["wqkv"], s["wrest"]], axis=1)
    dxp = _matmul(dproj, wall, "nt", F32, MM_ROWS, 1024, 1920, n + "proj_dx")
    d_wall = _matmul(s["xb"], dproj, "tn", F32, 1024, 1152, DW_ROWS, n + "proj_dw")
    grads = dict(w_in=_merge_w_in_grad(d_wall), w_pa=d_wpa, w_pb=d_wpb, w_out=d_wout, w_ff_in=d_wffin,
                 w_ff_out=d_wffout, b_fgate=dbf[0, :A_HEADS], lb=dlb[0], norm_g=dng[0], ln1_g=dg1[0], ln1_b=db1[0],
                 ln2_g=dg2[0], ln2_b=db2[0])
    return [dz1, dxp], [ALPHA, 1.0], grads


def _lower_bounds(logits):
    sm = jax.nn.softmax(logits.astype(F32), axis=0)
    return jnp.cumsum(sm, axis=0) - sm[0:1]


def _local_step(x, target, wfull, small, ride=(), late_weights=None):
    lbs, lb_vjp = jax.vjp(_lower_bounds, small["hgrn_lb_logits"])
    h, hb = x, x.astype(BF16)
    wfull = list(wfull)
    saved, sps = [], []
    for l in range(DEPTH):
        sp = dict(b_fgate=small["b_fgate"][l], lb=lbs[l], norm_g=small["hgrn_norm_g"][l], ln1_g=small["ln1_g"][l],
                  ln1_b=small["ln1_b"][l], ln2_g=small["ln2_g"][l], ln2_b=small["ln2_b"][l])
        h, hb, s, late = _layer_fwd(h, hb, wfull[l], sp, l, ride if l == 0 else (), late_weights)
        if late is not None:
            wfull = [{**wfull[k], **late[k]} for k in range(DEPTH)]
        saved.append(s)
        sps.append(sp)
    dy, lpart = _loss_head(h, target)
    dys, coefs = [dy], [1.0]
    grads = [None] * DEPTH
    for l in reversed(range(DEPTH)):
        dys, coefs, grads[l] = _layer_bwd(dys, coefs, wfull[l], sps[l], saved[l], l)
    grad_x = _axpy2(coefs[0], dys[0], coefs[1], dys[1], "grad_x")
    d_logits = lb_vjp(jnp.stack([grads[l]["lb"] for l in range(DEPTH)]))[0]
    return lpart[0, 0], grad_x, grads, d_logits


_BIG = [("w_in", "w_in", (D_MODEL, IN_TOTAL), 1), ("w_branch_a", "w_pa", (A_WIDTH, D_MODEL), 1),
        ("w_branch_b", "w_pb", (B_WIDTH, D_MODEL), 1), ("w_out", "w_out", (D_MODEL, D_MODEL), 0),
        ("w_ff_in", "w_ff_in", (D_MODEL, 2 * FFN_HIDDEN), 1), ("w_ff_out", "w_ff_out", (FFN_HIDDEN, D_MODEL), 0)]
_SMALL = [("b_fgate", A_HEADS), ("hgrn_lb_logits", B_WIDTH), ("hgrn_norm_g", HD), ("ln1_g", D_MODEL),
          ("ln1_b", D_MODEL), ("ln2_g", D_MODEL), ("ln2_b", D_MODEL)]
N_BIG = len(_BIG)
SMALL_ROWS = 80


def _by_chip(full, axis):
    if axis == 0:
        return full.reshape(N_CHIPS, full.shape[0] // N_CHIPS, full.shape[1])
    n = full.shape[1] // N_CHIPS
    return jnp.stack([full[:, q * n:(q + 1) * n] for q in range(N_CHIPS)])


def _from_chips(shards, axis):
    if axis == 0:
        return shards.reshape(N_CHIPS * shards.shape[1], shards.shape[2])
    return jnp.concatenate([shards[q] for q in range(N_CHIPS)], axis=1)


def _pack_small(per_name):
    flat = jnp.concatenate([per_name[name].reshape(-1) for name, _ in _SMALL])
    return jnp.pad(flat, (0, SMALL_ROWS * 128 - flat.shape[0])).reshape(SMALL_ROWS, 128)


def _unpack_small(slab):
    flat, out, r = slab.reshape(-1), {}, 0
    for name, n in _SMALL:
        out[name] = flat[r:r + DEPTH * n].reshape(DEPTH, n)
        r += DEPTH * n
    return out


_ANY = pl.BlockSpec(memory_space=pl.ANY)


def _place():
    return lax.axis_index("x"), lax.axis_index("y"), lax.axis_index("c")


def _other_chips(x, y):
    return [(1 - x, y), (x, 1 - y), (1 - x, 1 - y)]


def _chip_exchange(mine_of, out_refs, send_sems, recv_sems, local_sems):
    _chip_exchange_start(mine_of, out_refs, send_sems, recv_sems, local_sems)
    _chip_exchange_wait(mine_of, out_refs, send_sems, recv_sems, local_sems)


def _chip_exchange_copies(mine_of, out_refs, send_sems, recv_sems, local_sems):
    x, y, c = _place()
    q = 2 * x + y
    local = [pltpu.make_async_copy(mine_of(w, q), out_ref.at[q], local_sems.at[w]) for w, out_ref in enumerate(out_refs)]
    sends, recvs = [], []
    for k, (px, py) in enumerate(_other_chips(x, y)):
        for w, out_ref in enumerate(out_refs):
            sems = dict(send_sem=send_sems.at[3 * w + k], recv_sem=recv_sems.at[3 * w + k], device_id=(px, py, c),
                        device_id_type=MESH)
            sends.append(pltpu.make_async_remote_copy(src_ref=mine_of(w, 2 * px + py), dst_ref=out_ref.at[q], **sems))
            recvs.append(pltpu.make_async_remote_copy(src_ref=mine_of(w, q), dst_ref=out_ref.at[2 * px + py], **sems))
    return local, sends, recvs


def _chip_exchange_start(*args):
    local, sends, _ = _chip_exchange_copies(*args)
    for cp in local + sends:
        cp.start()


def _chip_exchange_wait(*args):
    local, sends, recvs = _chip_exchange_copies(*args)
    for cp in recvs:
        cp.wait_recv()
    for cp in sends:
        cp.wait_send()
    for cp in local:
        cp.wait()


def _sem_scratch(n):
    return [pltpu.SemaphoreType.DMA((3 * n,)), pltpu.SemaphoreType.DMA((3 * n,)), pltpu.SemaphoreType.DMA((n,))]


def _gather_scratch(n):
    return _sem_scratch(n) + [pltpu.SemaphoreType.DMA((n,)), pltpu.SemaphoreType.DMA((n,))]


def _gather_out_shapes(mine):
    return [jax.ShapeDtypeStruct((DEPTH, N_CHIPS) + m.shape[1:], m.dtype) for m in mine]


def _gather_start(in_refs, out_refs, sems):
    c = lax.axis_index("c")
    _chip_exchange_start(lambda w, q: in_refs[w].at[c], [o.at[c] for o in out_refs], *sems[:3])


def _gather_finish(in_refs, out_refs, sems):
    x, y, c = _place()
    _chip_exchange_wait(lambda w, q: in_refs[w].at[c], [o.at[c] for o in out_refs], *sems[:3])
    pair_send, pair_recv = sems[3:]
    sibling = (x, y, 1 - c)
    fwds = []
    for w, o in enumerate(out_refs):
        cp = pltpu.make_async_remote_copy(src_ref=o.at[c], dst_ref=o.at[c], send_sem=pair_send.at[w],
                                          recv_sem=pair_recv.at[w], device_id=sibling, device_id_type=MESH)
        cp.start()
        fwds.append(cp)
    for w, o in enumerate(out_refs):
        pltpu.make_async_remote_copy(src_ref=o.at[1 - c], dst_ref=o.at[1 - c], send_sem=pair_send.at[w],
                                     recv_sem=pair_recv.at[w], device_id=sibling, device_id_type=MESH).wait_recv()
    for cp in fwds:
        cp.wait_send()


def _gather_weights(mine):
    n = len(mine)

    def body(*refs):
        in_refs, out_refs, sems = refs[:n], refs[n:2 * n], refs[2 * n:]
        _gather_start(in_refs, out_refs, sems)
        _gather_finish(in_refs, out_refs, sems)

    return pl.pallas_call(
        body, name="gather_weights", in_specs=[_ANY] * n, out_specs=[_ANY] * n,
        out_shape=_gather_out_shapes(mine), scratch_shapes=_gather_scratch(n),
    )(*mine)


def _pair_exchange(gs):
    n = len(gs)

    def body(*refs):
        g_refs, a_refs, send_sems, recv_sems = refs[:n], refs[n:2 * n], refs[2 * n], refs[2 * n + 1]
        x, y, c = _place()
        cps = []
        for w in range(n):
            cp = pltpu.make_async_remote_copy(src_ref=g_refs[w].at[1 - c], dst_ref=a_refs[w], send_sem=send_sems.at[w],
                                              recv_sem=recv_sems.at[w], device_id=(x, y, 1 - c), device_id_type=MESH)
            cp.start()
            cps.append(cp)
        for cp in cps:
            cp.wait()

    return pl.pallas_call(
        body, name="grad_pair_exchange", in_specs=[_ANY] * n, out_specs=[_ANY] * n,
        out_shape=[jax.ShapeDtypeStruct(g.shape[1:], g.dtype) for g in gs],
        scratch_shapes=[pltpu.SemaphoreType.DMA((n,)), pltpu.SemaphoreType.DMA((n,))],
    )(*gs)


def _row_block(rows):
    return math.gcd(rows, 256)


def _pair_sum(g, a, layer, name):
    _, nq, rows, cols = g.shape
    tb = _row_block(rows)

    def body(l_ref, g_ref, a_ref, o_ref):
        o_ref[...] = (g_ref[...] + a_ref[...]).astype(BF16)

    return pl.pallas_call(
        body, name=name,
        grid_spec=pltpu.PrefetchScalarGridSpec(
            num_scalar_prefetch=1, grid=(nq, rows // tb),
            in_specs=[pl.BlockSpec((None, None, tb, cols), lambda q, i, l_ref: (l_ref[0], q, i, 0)),
                      pl.BlockSpec((None, tb, cols), lambda q, i, l_ref: (q, i, 0))],
            out_specs=pl.BlockSpec((None, tb, cols), lambda q, i, l_ref: (q, i, 0))),
        out_shape=jax.ShapeDtypeStruct((nq, rows, cols), BF16),
        compiler_params=_params("parallel", "parallel"),
    )(layer.reshape(1).astype(jnp.int32), g, a)


def _shard_exchange(ps):
    n = len(ps)

    def body(*refs):
        p_refs, b_refs = refs[:n], refs[n:2 * n]
        send_sems, recv_sems, local_sems = refs[2 * n:]
        _chip_exchange(lambda w, q: p_refs[w].at[q], b_refs, send_sems, recv_sems, local_sems)

    return pl.pallas_call(
        body, name="grad_shard_exchange", in_specs=[_ANY] * n, out_specs=[_ANY] * n,
        out_shape=[jax.ShapeDtypeStruct(p.shape, p.dtype) for p in ps],
        scratch_shapes=_sem_scratch(n),
    )(*ps)


def _sum4(b, name):
    _, rows, cols = b.shape
    tb = _row_block(rows)

    def body(b_ref, o_ref):
        o_ref[...] = ((b_ref[0].astype(F32) + b_ref[1].astype(F32)) + b_ref[2].astype(F32)) + b_ref[3].astype(F32)

    return pl.pallas_call(
        body, name=name, grid=(rows // tb,),
        in_specs=[pl.BlockSpec((N_CHIPS, tb, cols), lambda i: (0, i, 0))],
        out_specs=pl.BlockSpec((tb, cols), lambda i: (i, 0)),
        out_shape=jax.ShapeDtypeStruct((rows, cols), F32),
        compiler_params=_params("parallel"),
    )(b)


def _result_exchange(gcs):
    n = len(gcs)

    def body(*refs):
        g_refs, o_refs, send_sems, recv_sems = refs[:n], refs[n:2 * n], refs[2 * n], refs[2 * n + 1]
        x, y, c = _place()
        cps = []
        for w in range(n):
            cp = pltpu.make_async_remote_copy(src_ref=g_refs[w], dst_ref=o_refs[w], send_sem=send_sems.at[w],
                                              recv_sem=recv_sems.at[w], device_id=(x, y, 1 - c), device_id_type=MESH)
            cp.start()
            cps.append(cp)
        for cp in cps:
            cp.wait()

    return pl.pallas_call(
        body, name="grad_result_exchange", in_specs=[_ANY] * n, out_specs=[_ANY] * n,
        out_shape=[jax.ShapeDtypeStruct(g.shape, g.dtype) for g in gcs],
        scratch_shapes=[pltpu.SemaphoreType.DMA((n,)), pltpu.SemaphoreType.DMA((n,))],
    )(*gcs)


def _allreduce_small(v):
    def body(v_ref, o_ref, buf, send_sems, recv_sems):
        x, y, c = _place()
        me = 4 * x + 2 * y + c
        buf[me] = v_ref[...]
        peers = []
        for k in range(1, N_DEV):
            px = 1 - x if k & 4 else x
            py = 1 - y if k & 2 else y
            pc = 1 - c if k & 1 else c
            peers.append((px, py, pc))
        sends = []
        for k, peer in enumerate(peers):
            cp = pltpu.make_async_remote_copy(src_ref=v_ref, dst_ref=buf.at[me], send_sem=send_sems.at[k],
                                              recv_sem=recv_sems.at[k], device_id=peer, device_id_type=MESH)
            cp.start()
            sends.append(cp)
        for k, (px, py, pc) in enumerate(peers):
            pltpu.make_async_remote_copy(src_ref=v_ref, dst_ref=buf.at[4 * px + 2 * py + pc], send_sem=send_sems.at[k],
                                         recv_sem=recv_sems.at[k], device_id=(px, py, pc),
                                         device_id_type=MESH).wait_recv()
        for cp in sends:
            cp.wait_send()
        acc = buf[0]
        for i in range(1, N_DEV):
            acc = acc + buf[i]
        o_ref[...] = acc

    vm = pl.BlockSpec(memory_space=pltpu.VMEM)
    return pl.pallas_call(
        body, name="small_allreduce", in_specs=[vm], out_specs=vm,
        out_shape=jax.ShapeDtypeStruct(v.shape, F32),
        scratch_shapes=[pltpu.VMEM((N_DEV,) + v.shape, F32), pltpu.SemaphoreType.DMA((N_DEV - 1,)),
                        pltpu.SemaphoreType.DMA((N_DEV - 1,))],
    )(v)


def _adam_update(w, g, m, v):
    nm = ADAM_B1 * m + (1.0 - ADAM_B1) * g
    nv = ADAM_B2 * v + (1.0 - ADAM_B2) * (g * g)
    m_hat = nm / (1.0 - ADAM_B1 ** ADAM_STEP)
    v_hat = nv / (1.0 - ADAM_B2 ** ADAM_STEP)
    return -ADAM_LR * (m_hat / (jnp.sqrt(v_hat) + ADAM_EPS) + ADAM_WD * w), nm, nv


def _adamw_small(w, g, m, v, name):
    def body(w_ref, g_ref, m_ref, v_ref, d_ref, nm_ref, nv_ref):
        d_ref[...], nm_ref[...], nv_ref[...] = _adam_update(w_ref[...], g_ref[...], m_ref[...], v_ref[...])

    vm = pl.BlockSpec(memory_space=pltpu.VMEM)
    return pl.pallas_call(
        body, name=name, in_specs=[vm] * 4, out_specs=[vm] * 3,
        out_shape=[jax.ShapeDtypeStruct(w.shape, F32)] * 3,
    )(w, g, m, v)


def _adamw_big(w, m, v, g_own, g_other, layer, name):
    _, rows, cols = w.shape
    tb = _row_block(rows)

    def body(l_ref, w_ref, m_ref, v_ref, go_ref, gx_ref, g_ref, d_ref, nm_ref, nv_ref):
        gv = jnp.where(pl.program_id(0) == l_ref[0], go_ref[...], gx_ref[...])
        g_ref[...] = gv
        d_ref[...], nm_ref[...], nv_ref[...] = _adam_update(w_ref[...], gv, m_ref[...], v_ref[...])

    per_layer = pl.BlockSpec((None, tb, cols), lambda l, i, l_ref: (l, i, 0))
    shared = pl.BlockSpec((tb, cols), lambda l, i, l_ref: (i, 0))
    return pl.pallas_call(
        body, name=name,
        grid_spec=pltpu.PrefetchScalarGridSpec(
            num_scalar_prefetch=1, grid=(DEPTH, rows // tb),
            in_specs=[per_layer, per_layer, per_layer, shared, shared], out_specs=[per_layer] * 4),
        out_shape=[jax.ShapeDtypeStruct(w.shape, F32)] * 4,
        compiler_params=_params("parallel", "parallel"),
    )(layer.reshape(1).astype(jnp.int32), w, m, v, g_own, g_other)


def kernel(x, w_in, b_fgate, hgrn_lb_logits, hgrn_norm_g, w_branch_a, w_branch_b, w_out, ln1_g, ln1_b, w_ff_in, w_ff_out, ln2_g, ln2_b, loss_target, m_w_in, m_b_fgate, m_hgrn_lb_logits, m_hgrn_norm_g, m_w_branch_a, m_w_branch_b, m_w_out, m_ln1_g, m_ln1_b, m_w_ff_in, m_w_ff_out, m_ln2_g, m_ln2_b, v_w_in, v_b_fgate, v_hgrn_lb_logits, v_hgrn_norm_g, v_w_branch_a, v_w_branch_b, v_w_out, v_ln1_g, v_ln1_b, v_w_ff_in, v_w_ff_out, v_ln2_g, v_ln2_b):
    weights = dict(w_in=w_in, b_fgate=b_fgate, hgrn_lb_logits=hgrn_lb_logits, hgrn_norm_g=hgrn_norm_g,
                   w_branch_a=w_branch_a, w_branch_b=w_branch_b, w_out=w_out, ln1_g=ln1_g, ln1_b=ln1_b,
                   w_ff_in=w_ff_in, w_ff_out=w_ff_out, ln2_g=ln2_g, ln2_b=ln2_b)
    mom1 = dict(w_in=m_w_in, b_fgate=m_b_fgate, hgrn_lb_logits=m_hgrn_lb_logits, hgrn_norm_g=m_hgrn_norm_g,
                w_branch_a=m_w_branch_a, w_branch_b=m_w_branch_b, w_out=m_w_out, ln1_g=m_ln1_g, ln1_b=m_ln1_b,
                w_ff_in=m_w_ff_in, w_ff_out=m_w_ff_out, ln2_g=m_ln2_g, ln2_b=m_ln2_b)
    mom2 = dict(w_in=v_w_in, b_fgate=v_b_fgate, hgrn_lb_logits=v_hgrn_lb_logits, hgrn_norm_g=v_hgrn_norm_g,
                w_branch_a=v_w_branch_a, w_branch_b=v_w_branch_b, w_out=v_w_out, ln1_g=v_ln1_g, ln1_b=v_ln1_b,
                w_ff_in=v_w_ff_in, w_ff_out=v_w_ff_out, ln2_g=v_ln2_g, ln2_b=v_ln2_b)
    core = lax.axis_index("c")

    def full_w_in(gathered):
        return _from_chips(jnp.concatenate([gathered[0], gathered[1]], axis=1), 1)

    def late_weights(gathered):
        per_layer = [{key: _from_chips(gathered[1 + w][l], axis) for w, (_, key, _, axis) in enumerate(_BIG[1:])}
                     for l in range(DEPTH)]
        per_layer[1]["w_in"] = full_w_in(gathered[0])
        return per_layer

    w_in_halves = weights["w_in"].astype(BF16).reshape(DEPTH, 2, D_MODEL // 2, IN_TOTAL // N_CHIPS)
    wfull = [{"w_in": full_w_in(_gather_weights([w_in_halves[0]])[0])}, {}]
    ride = [w_in_halves[1]] + [weights[name].astype(BF16) for name, _, _, _ in _BIG[1:]]
    small = {name: weights[name] for name, _ in _SMALL}

    loss_part, grad_x, grads, d_logits = _local_step(x[0], loss_target[0], wfull, small, ride, late_weights)

    g_all = [jnp.stack([_by_chip(grads[l][key], axis) for l in range(DEPTH)]) for _, key, _, axis in _BIG]
    received = _pair_exchange(g_all)
    pair = [_pair_sum(g_all[w], received[w], core, f"grad_pair_sum_{w}") for w in range(N_BIG)]
    by_chip = _shard_exchange(pair)
    g_layer = [_sum4(by_chip[w], f"grad_chip_sum_{w}") for w in range(N_BIG)]
    g_other = _result_exchange(g_layer)
    out_g, out_d, out_m, out_v = {}, {}, {}, {}
    for w, (name, _, _, _) in enumerate(_BIG):
        out_g[name], out_d[name], out_m[name], out_v[name] = _adamw_big(
            weights[name], mom1[name], mom2[name], g_layer[w], g_other[w], core, f"adamw_{name}")

    small_grads = {name: jnp.stack([grads[l][key] for l in range(DEPTH)])
                   for name, key in [("b_fgate", "b_fgate"), ("hgrn_norm_g", "norm_g"), ("ln1_g", "ln1_g"),
                                     ("ln1_b", "ln1_b"), ("ln2_g", "ln2_g"), ("ln2_b", "ln2_b")]}
    small_grads["hgrn_lb_logits"] = d_logits
    gs = _allreduce_small(_pack_small(small_grads))
    ds, ms, vs = _adamw_small(_pack_small(small), gs, _pack_small({n: mom1[n] for n, _ in _SMALL}),
                              _pack_small({n: mom2[n] for n, _ in _SMALL}), "adamw_small")
    for tree, slab in ((out_g, gs), (out_d, ds), (out_m, ms), (out_v, vs)):
        tree.update(_unpack_small(slab))

    loss = lax.psum(loss_part, ("x", "y", "c"))
    order = ["w_in", "b_fgate", "hgrn_lb_logits", "hgrn_norm_g", "w_branch_a", "w_branch_b", "w_out", "ln1_g", "ln1_b",
             "w_ff_in", "w_ff_out", "ln2_g", "ln2_b"]
    return (loss, grad_x[None], *[out_g[n] for n in order], *[out_d[n] for n in order],
            *[out_m[n] for n in order], *[out_v[n] for n in order])
```

```python
import math

import jax
import jax.numpy as jnp
from jax import lax
from jax.experimental import pallas as pl
from jax.experimental.pallas import tpu as pltpu

F32 = jnp.float32
BF16 = jnp.bfloat16

D_MODEL = 1024
DEPTH = 2
A_HEADS = 8
A_WIDTH = 512
B_WIDTH = 512
B_HEADS = 4
HD = 128
CHUNK = 64
SUB = 16
FFN_HIDDEN = 2816
IN_TOTAL = 5640
ALPHA = (2 * DEPTH) ** 0.25
LN_EPS = 1e-5
RMS_EPS = 1e-6
ADAM_LR = 0.001
ADAM_B1 = 0.9
ADAM_B2 = 0.999
ADAM_EPS = 1e-08
ADAM_WD = 0.01
ADAM_STEP = 10
EXP_CLAMP = 60.0

VMEM_LIMIT_BYTES = 56 * 1024 * 1024
MM_ROWS = 1024
DW_ROWS = 2048
N_CHIPS = 4
N_DEV = 8
MESH = pl.DeviceIdType.MESH

_DN = {
    "nn": (((1,), (0,)), ((), ())),
    "nt": (((1,), (1,)), ((), ())),
    "tn": (((0,), (0,)), ((), ())),
}


def _dot(a, b, mode="nn"):
    return lax.dot_general(a.astype(BF16), b.astype(BF16), _DN[mode], preferred_element_type=F32)


def _pieces(x):
    h = x.astype(BF16)
    r = x - h.astype(F32)
    m = r.astype(BF16)
    return h, m, (r - m.astype(F32)).astype(BF16)


def _dot_hi(a, b, mode="nn", exact="a"):
    if exact == "a":
        h, m, l = _pieces(b)
        return (_dot(a, l, mode) + _dot(a, m, mode)) + _dot(a, h, mode)
    h, m, l = _pieces(a)
    return (_dot(l, b, mode) + _dot(m, b, mode)) + _dot(h, b, mode)


def _hdot(a, b, mode="nn"):
    bh, bl, _ = _pieces(b)
    return _dot(a, bl, mode) + _dot(a, bh, mode)


def _params(*sem):
    return pltpu.CompilerParams(dimension_semantics=sem, vmem_limit_bytes=VMEM_LIMIT_BYTES)


def _sigmoid(x):
    return 1.0 / (1.0 + jnp.exp(-x))


def _matmul(a, b, mode, out_dtype, tm, tn, tk, name):
    if mode == "nn":
        (m, k), (k2, n) = a.shape, b.shape
    elif mode == "nt":
        (m, k), (n, k2) = a.shape, b.shape
    else:
        (k, m), (k2, n) = a.shape, b.shape
    assert k == k2, (a.shape, b.shape, mode)
    tm, tn, tk = min(tm, m), min(tn, n), min(tk, k)
    assert m % tm == 0 and n % tn == 0 and k % tk == 0, (a.shape, b.shape, tm, tn, tk)
    nk = k // tk
    if mode == "tn":
        a_spec = pl.BlockSpec((tk, tm), lambda j, i, kk: (kk, i))
    else:
        a_spec = pl.BlockSpec((tm, tk), lambda j, i, kk: (i, kk))
    if mode == "nt":
        b_spec = pl.BlockSpec((tn, tk), lambda j, i, kk: (j, kk))
    else:
        b_spec = pl.BlockSpec((tk, tn), lambda j, i, kk: (kk, j))
    use_acc = nk > 1 and out_dtype != F32

    def body(a_ref, b_ref, o_ref, *scratch):
        p = _dot(a_ref[...], b_ref[...], mode)
        if nk == 1:
            o_ref[...] = p.astype(out_dtype)
            return
        acc_ref = scratch[0] if use_acc else o_ref
        kk = pl.program_id(2)

        @pl.when(kk == 0)
        def _():
            acc_ref[...] = p

        @pl.when(kk > 0)
        def _():
            acc_ref[...] += p

        if use_acc:
            @pl.when(kk == nk - 1)
            def _():
                o_ref[...] = acc_ref[...].astype(out_dtype)

    return pl.pallas_call(
        body,
        name=name,
        grid=(n // tn, m // tm, nk),
        in_specs=[a_spec, b_spec],
        out_specs=pl.BlockSpec((tm, tn), lambda j, i, kk: (i, j)),
        out_shape=jax.ShapeDtypeStruct((m, n), out_dtype),
        scratch_shapes=[pltpu.VMEM((tm, tn), F32)] if use_acc else [],
        compiler_params=_params("parallel", "parallel", "arbitrary"),
    )(a, b)


def _matmul_nt_sum(a_list, b_list, name, tm=512):
    m = a_list[0].shape[0]
    n = b_list[0].shape[0]
    tm = min(tm, m)
    np_ = len(a_list)

    def body(*refs):
        o_ref = refs[2 * np_]
        acc = _dot(refs[0][...], refs[np_][...], "nt")
        for p in range(1, np_):
            acc = acc + _dot(refs[p][...], refs[np_ + p][...], "nt")
        o_ref[...] = acc

    return pl.pallas_call(
        body,
        name=name,
        grid=(m // tm,),
        in_specs=[pl.BlockSpec((tm, a.shape[1]), lambda i: (i, 0)) for a in a_list]
        + [pl.BlockSpec(b.shape, lambda i: (0, 0)) for b in b_list],
        out_specs=pl.BlockSpec((tm, n), lambda i: (i, 0)),
        out_shape=jax.ShapeDtypeStruct((m, n), F32),
        compiler_params=_params("parallel"),
    )(*a_list, *b_list)


def _mm_res_ln(a, w, res, g, b, name, tm=512):
    t, k = a.shape
    d = w.shape[1]
    tm = min(tm, t)

    def body(a_ref, w_ref, r_ref, g_ref, b_ref, y_ref, yb_ref, xh_ref, rs_ref):
        z = ALPHA * r_ref[...] + _dot(a_ref[...], w_ref[...])
        mu = jnp.mean(z, axis=-1, keepdims=True)
        zc = z - mu
        var = jnp.mean(zc * zc, axis=-1, keepdims=True)
        rstd = lax.rsqrt(var + LN_EPS)
        xh = zc * rstd
        y = xh * g_ref[...] + b_ref[...]
        y_ref[...] = y
        yb_ref[...] = y.astype(BF16)
        xh_ref[...] = xh
        rs_ref[...] = rstd

    row = lambda i: (i, 0)
    fix = lambda i: (0, 0)
    return pl.pallas_call(
        body,
        name=name,
        grid=(t // tm,),
        in_specs=[pl.BlockSpec((tm, k), row), pl.BlockSpec((k, d), fix), pl.BlockSpec((tm, d), row),
                  pl.BlockSpec((1, d), fix), pl.BlockSpec((1, d), fix)],
        out_specs=[pl.BlockSpec((tm, d), row), pl.BlockSpec((tm, d), row), pl.BlockSpec((tm, d), row),
                   pl.BlockSpec((tm, 1), row)],
        out_shape=[jax.ShapeDtypeStruct((t, d), F32), jax.ShapeDtypeStruct((t, d), BF16),
                   jax.ShapeDtypeStruct((t, d), F32), jax.ShapeDtypeStruct((t, 1), F32)],
        compiler_params=_params("parallel"),
    )(a, w, res, g.reshape(1, d), b.reshape(1, d))


def _ln_bwd(dys, coefs, xhat, rstd, g, name, tm=512):
    t, d = xhat.shape
    tm = min(tm, t)
    n_in = len(dys)

    def body(*refs):
        dy_refs = refs[:n_in]
        xh_ref, rs_ref, g_ref, dz_ref, dzb_ref, dg_ref, db_ref = refs[n_in:]
        dy = coefs[0] * dy_refs[0][...].astype(F32)
        for c, r in zip(coefs[1:], dy_refs[1:]):
            dy = dy + c * r[...].astype(F32)
        xh = xh_ref[...]
        dxh = dy * g_ref[...]
        m1 = jnp.mean(dxh, axis=-1, keepdims=True)
        m2 = jnp.mean(dxh * xh, axis=-1, keepdims=True)
        dz = rs_ref[...] * (dxh - m1 - xh * m2)
        dz_ref[...] = dz
        dzb_ref[...] = dz.astype(BF16)
        pg = jnp.sum(dy * xh, axis=0, keepdims=True)
        pb = jnp.sum(dy, axis=0, keepdims=True)

        @pl.when(pl.program_id(0) == 0)
        def _():
            dg_ref[...] = pg
            db_ref[...] = pb

        @pl.when(pl.program_id(0) > 0)
        def _():
            dg_ref[...] += pg
            db_ref[...] += pb

    row = lambda i: (i, 0)
    fix = lambda i: (0, 0)
    return pl.pallas_call(
        body,
        name=name,
        grid=(t // tm,),
        in_specs=[pl.BlockSpec((tm, d), row)] * n_in
        + [pl.BlockSpec((tm, d), row), pl.BlockSpec((tm, 1), row), pl.BlockSpec((1, d), fix)],
        out_specs=[pl.BlockSpec((tm, d), row), pl.BlockSpec((tm, d), row), pl.BlockSpec((1, d), fix),
                   pl.BlockSpec((1, d), fix)],
        out_shape=[jax.ShapeDtypeStruct((t, d), F32), jax.ShapeDtypeStruct((t, d), BF16),
                   jax.ShapeDtypeStruct((1, d), F32), jax.ShapeDtypeStruct((1, d), F32)],
        compiler_params=_params("arbitrary"),
    )(*dys, xhat, rstd, g.reshape(1, d))


def _loss_head(y, target, name="loss_head", tm=512):
    t, d = y.shape
    tm = min(tm, t)

    def body(y_ref, t_ref, dy_ref, l_ref):
        e = y_ref[...] - t_ref[...]
        dy_ref[...] = e * (1.0 / d)
        part = jnp.full((8, 128), 0.5 / d, F32) * jnp.sum(e * e)

        @pl.when(pl.program_id(0) == 0)
        def _():
            l_ref[...] = part

        @pl.when(pl.program_id(0) > 0)
        def _():
            l_ref[...] += part

    row = lambda i: (i, 0)
    return pl.pallas_call(
        body,
        name=name,
        grid=(t // tm,),
        in_specs=[pl.BlockSpec((tm, d), row), pl.BlockSpec((tm, d), row)],
        out_specs=[pl.BlockSpec((tm, d), row), pl.BlockSpec((8, 128), lambda i: (0, 0))],
        out_shape=[jax.ShapeDtypeStruct((t, d), F32), jax.ShapeDtypeStruct((8, 128), F32)],
        compiler_params=_params("arbitrary"),
    )(y, target)


FFN_COLS = FFN_HIDDEN // 2


def _ffn_in_swiglu(xb, wu, wg, name, tm=MM_ROWS):
    t, d = xb.shape
    tm = min(tm, t)

    def body(x_ref, wu_ref, wg_ref, a_ref, u_ref, g_ref):
        x = x_ref[...]
        u = _dot(x, wu_ref[...])
        g = _dot(x, wg_ref[...])
        u_ref[...] = u.astype(BF16)
        g_ref[...] = g.astype(BF16)
        a_ref[...] = (g * _sigmoid(g) * u).astype(BF16)

    wspec = pl.BlockSpec((d, FFN_COLS), lambda j, i: (0, j))
    out = pl.BlockSpec((tm, FFN_COLS), lambda j, i: (i, j))
    return pl.pallas_call(
        body,
        name=name,
        grid=(FFN_HIDDEN // FFN_COLS, t // tm),
        in_specs=[pl.BlockSpec((tm, d), lambda j, i: (i, 0)), wspec, wspec],
        out_specs=[out, out, out],
        out_shape=[jax.ShapeDtypeStruct((t, FFN_HIDDEN), BF16)] * 3,
        compiler_params=_params("parallel", "parallel"),
    )(xb, wu, wg)


def _ffn_out_dx_swiglu(dzb, w_ff_out, u, g, name, tm=MM_ROWS):
    t, d = dzb.shape
    tm = min(tm, t)

    def body(dz_ref, w_ref, u_ref, g_ref, du_ref, dg_ref):
        da = _dot(dz_ref[...], w_ref[...], "nt")
        gv = g_ref[...].astype(F32)
        sg = _sigmoid(gv)
        du_ref[...] = (da * gv * sg).astype(BF16)
        dg_ref[...] = (da * u_ref[...].astype(F32) * (sg * (1.0 + gv * (1.0 - sg)))).astype(BF16)

    blk = pl.BlockSpec((tm, FFN_COLS), lambda j, i: (i, j))
    return pl.pallas_call(
        body,
        name=name,
        grid=(FFN_HIDDEN // FFN_COLS, t // tm),
        in_specs=[pl.BlockSpec((tm, d), lambda j, i: (i, 0)), pl.BlockSpec((FFN_COLS, d), lambda j, i: (j, 0)), blk, blk],
        out_specs=[blk, blk],
        out_shape=[jax.ShapeDtypeStruct((t, FFN_HIDDEN), BF16)] * 2,
        compiler_params=_params("parallel", "parallel"),
    )(dzb, w_ff_out, u, g)


def _merge_fwd(ya, yb, wpa, wpb, rest, name, tm=512):
    t = ya.shape[0]
    tm = min(tm, t)

    def body(ya_ref, yb_ref, wa_ref, wb_ref, ga_ref, gb_ref, o_ref):
        pa = _dot(ya_ref[...], wa_ref[...])
        pb = _dot(yb_ref[...], wb_ref[...])
        o_ref[...] = (_sigmoid(ga_ref[...]) * pa + _sigmoid(gb_ref[...]) * pb).astype(BF16)

    row = lambda i: (i, 0)
    fix = lambda i: (0, 0)
    return pl.pallas_call(
        body,
        name=name,
        grid=(t // tm,),
        in_specs=[pl.BlockSpec((tm, A_WIDTH), row), pl.BlockSpec((tm, B_WIDTH), row),
                  pl.BlockSpec((A_WIDTH, D_MODEL), fix), pl.BlockSpec((B_WIDTH, D_MODEL), fix),
                  pl.BlockSpec((tm, D_MODEL), lambda i: (i, 0)), pl.BlockSpec((tm, D_MODEL), lambda i: (i, 1))],
        out_specs=pl.BlockSpec((tm, D_MODEL), row),
        out_shape=jax.ShapeDtypeStruct((t, D_MODEL), BF16),
        compiler_params=_params("parallel"),
    )(ya, yb, wpa, wpb, rest, rest)


def _merge_bwd(dzb, w_out, ya, yb, wpa, wpb, rest, name, tm=512):
    t = ya.shape[0]
    tm = min(tm, t)

    def body(dz_ref, wo_ref, ya_ref, yb_ref, wa_ref, wb_ref, ga_ref, gb_ref, dg_ref, dpa_ref, dpb_ref, dya_ref,
             dyb_ref):
        dm_v = _dot(dz_ref[...], wo_ref[...], "nt")
        pa = _dot(ya_ref[...], wa_ref[...])
        pb = _dot(yb_ref[...], wb_ref[...])
        sa = _sigmoid(ga_ref[...])
        sb = _sigmoid(gb_ref[...])
        dg_ref[:, :D_MODEL] = (dm_v * pa * sa * (1.0 - sa)).astype(BF16)
        dg_ref[:, D_MODEL:] = (dm_v * pb * sb * (1.0 - sb)).astype(BF16)
        dpa = (dm_v * sa).astype(BF16)
        dpb = (dm_v * sb).astype(BF16)
        dpa_ref[...] = dpa
        dpb_ref[...] = dpb
        dya_ref[...] = _dot(dpa, wa_ref[...], "nt").astype(BF16)
        dyb_ref[...] = _dot(dpb, wb_ref[...], "nt")

    row = lambda i: (i, 0)
    fix = lambda i: (0, 0)
    return pl.pallas_call(
        body,
        name=name,
        grid=(t // tm,),
        in_specs=[pl.BlockSpec((tm, D_MODEL), row), pl.BlockSpec((D_MODEL, D_MODEL), fix),
                  pl.BlockSpec((tm, A_WIDTH), row), pl.BlockSpec((tm, B_WIDTH), row),
                  pl.BlockSpec((A_WIDTH, D_MODEL), fix), pl.BlockSpec((B_WIDTH, D_MODEL), fix),
                  pl.BlockSpec((tm, D_MODEL), lambda i: (i, 0)), pl.BlockSpec((tm, D_MODEL), lambda i: (i, 1))],
        out_specs=[pl.BlockSpec((tm, 2 * D_MODEL), row), pl.BlockSpec((tm, D_MODEL), row),
                   pl.BlockSpec((tm, D_MODEL), row), pl.BlockSpec((tm, A_WIDTH), row), pl.BlockSpec((tm, B_WIDTH), row)],
        out_shape=[jax.ShapeDtypeStruct((t, 2 * D_MODEL), BF16), jax.ShapeDtypeStruct((t, D_MODEL), BF16),
                   jax.ShapeDtypeStruct((t, D_MODEL), BF16), jax.ShapeDtypeStruct((t, A_WIDTH), BF16),
                   jax.ShapeDtypeStruct((t, B_WIDTH), F32)],
        compiler_params=_params("parallel"),
    )(dzb, w_out, ya, yb, wpa, wpb, rest, rest)


FA_BLOCK = 4224 // 128 - 1


def _tri(n, lower):
    r = lax.broadcasted_iota(jnp.int32, (n, n), 0)
    c = lax.broadcasted_iota(jnp.int32, (n, n), 1)
    return jnp.where((r >= c) if lower else (r <= c), 1.0, 0.0).astype(F32)


def _head_spread(expand):
    shape = (128, A_WIDTH) if expand else (A_WIDTH, 128)
    r = lax.broadcasted_iota(jnp.int32, shape, 0)
    c = lax.broadcasted_iota(jnp.int32, shape, 1)
    hit = ((c >= 64 * r) & (c < 64 * r + 64)) if expand else (r == 64 * c)
    return jnp.where(hit, 1.0, 0.0).astype(F32)


def _fox_gate_fwd(rest, bf, name, tb=512):
    t = rest.shape[0]
    tb = min(tb, t)

    def body(fa_ref, bf_ref, f_ref, fc_ref, carry):
        @pl.when(pl.program_id(0) == 0)
        def _():
            carry[...] = jnp.zeros_like(carry)

        z = fa_ref[...] + bf_ref[...]
        logf = jnp.minimum(z, 0.0) - jnp.log(1.0 + jnp.exp(-jnp.abs(z)))
        f = _dot_hi(_tri(tb, True), logf) + carry[...]
        f_ref[...] = f
        fc_ref[...] = _dot_hi(f, _head_spread(True), exact="b")
        carry[...] = f[tb - 1:tb, :]

    return pl.pallas_call(
        body,
        name=name,
        grid=(t // tb,),
        in_specs=[pl.BlockSpec((tb, 128), lambda i: (i, FA_BLOCK)), pl.BlockSpec((1, 128), lambda i: (0, 0))],
        out_specs=[pl.BlockSpec((tb, 128), lambda i: (i, 0)), pl.BlockSpec((tb, A_WIDTH), lambda i: (i, 0))],
        out_shape=[jax.ShapeDtypeStruct((t, 128), F32), jax.ShapeDtypeStruct((t, A_WIDTH), F32)],
        scratch_shapes=[pltpu.VMEM((1, 128), F32)],
        compiler_params=_params("arbitrary"),
    )(rest, bf)


def _fox_gate_bwd(rsum, csum, rest, bf, name, tb=512):
    t = rest.shape[0]
    tb = min(tb, t)
    nb = t // tb

    def body(rs_ref, cs_ref, fa_ref, bf_ref, dfa_ref, dbf_ref, carry):
        @pl.when(pl.program_id(0) == 0)
        def _():
            carry[...] = jnp.zeros_like(carry)

        d_f = _dot_hi(rs_ref[...] - cs_ref[...], _head_spread(False), exact="b")
        dlogf = _dot_hi(_tri(tb, False), d_f) + carry[...]
        carry[...] = dlogf[0:1, :]
        z = fa_ref[...] + bf_ref[...]
        dz = dlogf * _sigmoid(-z)
        dfa_ref[...] = dz.astype(BF16)
        part = jnp.sum(dz, axis=0, keepdims=True)

        @pl.when(pl.program_id(0) == 0)
        def _():
            dbf_ref[...] = part

        @pl.when(pl.program_id(0) > 0)
        def _():
            dbf_ref[...] += part

    return pl.pallas_call(
        body,
        name=name,
        grid=(nb,),
        in_specs=[pl.BlockSpec((tb, A_WIDTH), lambda i: (nb - 1 - i, 0)),
                  pl.BlockSpec((tb, A_WIDTH), lambda i: (nb - 1 - i, 0)),
                  pl.BlockSpec((tb, 128), lambda i: (nb - 1 - i, FA_BLOCK)),
                  pl.BlockSpec((1, 128), lambda i: (0, 0))],
        out_specs=[pl.BlockSpec((tb, 128), lambda i: (nb - 1 - i, 0)), pl.BlockSpec((1, 128), lambda i: (0, 0))],
        out_shape=[jax.ShapeDtypeStruct((t, 128), BF16), jax.ShapeDtypeStruct((1, 128), F32)],
        scratch_shapes=[pltpu.VMEM((1, 128), F32)],
        compiler_params=_params("arbitrary"),
    )(rsum, csum, rest, bf)


ATT_BLOCK = 512
FWD_Q_BLOCKS = 2
BWD_K_BLOCKS = 2


def _head_mask(shape, j):
    lane = lax.broadcasted_iota(jnp.int32, shape, 1)
    return (lane < 64) if j == 0 else (lane >= 64)


def _aug_lanes(tb, j):
    lane = lax.broadcasted_iota(jnp.int32, (tb, 128), 1)
    own = (lane < 64) if j == 0 else (lane >= 64)
    return own, lane - 64 * (1 - j)


def _aug_query(own, li, q, pieces):
    h, m, l = pieces
    one, zero = jnp.ones_like(h), jnp.zeros_like(h)
    spare = jnp.where(li == 0, h, jnp.where(li == 1, m, jnp.where(li == 2, l, jnp.where(li < 6, one, zero))))
    return jnp.where(own, q, spare)


def _fox_prep_fwd(qkv, fcol, name, tb=2048):
    t = qkv.shape[0]
    tb = min(tb, t)

    def body(q_ref, k_ref, v_ref, fc_ref, qa_ref, ka_ref, va_ref, qn_ref, kn_ref):
        pieces = _pieces(pltpu.roll(fc_ref[...], 64, 1))
        h, m, l = pieces
        q, k, v = q_ref[...], k_ref[...], v_ref[...]
        first = _head_mask((tb, 128), 0)
        for nrm_ref, x in ((qn_ref, q.astype(F32)), (kn_ref, k.astype(F32))):
            n0 = jnp.max(jnp.sum(jnp.where(first, x * x, 0.0), axis=1, keepdims=True))
            n1 = jnp.max(jnp.sum(jnp.where(first, 0.0, x * x), axis=1, keepdims=True))
            nrm_ref[...] = jnp.where(_head_mask((8, 128), 0), n0, n1)
        one, zero = jnp.ones_like(h), jnp.zeros_like(h)
        for j in (0, 1):
            own, li = _aug_lanes(tb, j)
            cols = slice(128 * j, 128 * (j + 1))
            qa_ref[:, cols] = _aug_query(own, li, q * 0.125, pieces)
            ks = jnp.where(li < 3, one, jnp.where(li == 3, -h, jnp.where(li == 4, -m, jnp.where(li == 5, -l, zero))))
            ka_ref[:, cols] = jnp.where(own, k, ks)
            va_ref[:, cols] = jnp.where(own, v, one)

    blk = pl.BlockSpec((tb, 256), lambda i, h: (i, h))
    nrm = pl.BlockSpec((None, None, 8, 128), lambda i, h: (i, h, 0, 0))
    return pl.pallas_call(
        body, name=name, grid=(t // tb, 4),
        in_specs=[pl.BlockSpec((tb, 128), lambda i, h: (i, h)), pl.BlockSpec((tb, 128), lambda i, h: (i, 4 + h)),
                  pl.BlockSpec((tb, 128), lambda i, h: (i, 8 + h)), pl.BlockSpec((tb, 128), lambda i, h: (i, h))],
        out_specs=[blk, blk, blk, nrm, nrm],
        out_shape=[jax.ShapeDtypeStruct((t, 2 * A_WIDTH), BF16)] * 3
        + [jax.ShapeDtypeStruct((t // tb, 4, 8, 128), F32)] * 2,
        compiler_params=_params("parallel", "parallel"),
    )(qkv, qkv, qkv, fcol)


def _fox_prep_bwd(qkv, fcol, lse, do, o, name, tb=2048):
    t = qkv.shape[0]
    tb = min(tb, t)

    def body(q_ref, fc_ref, lse_ref, do_ref, o_ref, qb_ref, dob_ref):
        pieces = _pieces(pltpu.roll(fc_ref[...] - lse_ref[...], 64, 1))
        q = q_ref[...] * 0.125
        do_v = do_ref[...]
        prod = do_v.astype(F32) * o_ref[...].astype(F32)
        for j in (0, 1):
            own, li = _aug_lanes(tb, j)
            cols = slice(128 * j, 128 * (j + 1))
            qb_ref[:, cols] = _aug_query(own, li, q, pieces)
            delta = jnp.sum(jnp.where(own, prod, 0.0), axis=1, keepdims=True)
            h, m, l = _pieces(jnp.broadcast_to(delta, (tb, 128)))
            ds = jnp.where(li == 0, -h, jnp.where(li == 1, -m, jnp.where(li == 2, -l, jnp.zeros_like(h))))
            dob_ref[:, cols] = jnp.where(own, do_v, ds)

    pair = pl.BlockSpec((tb, 128), lambda i, h: (i, h))
    blk = pl.BlockSpec((tb, 256), lambda i, h: (i, h))
    return pl.pallas_call(
        body, name=name, grid=(t // tb, 4),
        in_specs=[pair, pair, pair, pair, pair],
        out_specs=[blk, blk],
        out_shape=[jax.ShapeDtypeStruct((t, 2 * A_WIDTH), BF16)] * 2,
        compiler_params=_params("parallel", "parallel"),
    )(qkv, fcol, lse, do, o)


def _tile_mask(n, transposed):
    r = lax.broadcasted_iota(jnp.int32, (n, n), 0)
    c = lax.broadcasted_iota(jnp.int32, (n, n), 1)
    return (c >= r) if transposed else (r >= c)


UNDERFLOW = -110.0


def _fox_block_ranges(qn, kn, fcum):
    t = fcum.shape[0]
    blk = min(ATT_BLOCK, t)
    nb = t // blk
    q2 = jnp.max(qn[:, :, 0, ::64].reshape(-1, A_HEADS), axis=0)
    k2 = jnp.max(kn[:, :, 0, ::64].reshape(-1, A_HEADS), axis=0)
    bound = 2.0 * jnp.sqrt(q2 * k2) * 0.125
    f = fcum[:, :A_HEADS]
    first = f[0::blk].T
    last = f[blk - 1::blk].T
    dead = (bound[:, None, None] + first[:, :, None] - last[:, None, :]) < UNDERFLOW
    qi = jnp.arange(nb)[None, :, None]
    kj = jnp.arange(nb)[None, None, :]
    dead = dead & (kj < qi)
    kstart = jnp.sum(dead, axis=2).astype(jnp.int32)
    qend = (kj[0] + jnp.sum((~dead) & (qi > kj), axis=1)).astype(jnp.int32)
    return kstart.reshape(-1), qend.reshape(-1)


def _fox_fwd(qa, ka, va, kstart, name):
    t = qa.shape[0]
    bk = min(ATT_BLOCK, t)
    nk = t // bk
    qf = FWD_Q_BLOCKS if t % (FWD_Q_BLOCKS * bk) == 0 else 1
    bq = qf * bk
    nq = t // bq

    def body(ks_ref, q_ref, k_ref, v_ref, o_ref, lse_ref):
        i = pl.program_id(1)
        hp = pl.program_id(0)
        k0 = [ks_ref[(2 * hp + j) * nk + qf * i] for j in (0, 1)]
        both0 = jnp.maximum(k0[0], k0[1])

        def head(j, kb, m, acc, diag):
            rows = pl.ds(pl.multiple_of(kb * bk, bk), bk)
            cols = slice(128 * j, 128 * (j + 1))
            s = _dot(q_ref[:, cols], k_ref[rows, cols], "nt")
            if diag is not None:
                r = lax.broadcasted_iota(jnp.int32, (bq, bk), 0)
                c = lax.broadcasted_iota(jnp.int32, (bq, bk), 1)
                s = jnp.where(r - c >= diag, s, -jnp.inf)
            m_new = jnp.maximum(m, jnp.max(s, axis=1, keepdims=True))
            return m_new, jnp.exp(m - m_new) * acc + _dot(jnp.exp(s - m_new), v_ref[rows, cols])

        def pair(kb, carry, diag):
            return head(0, kb, carry[0], carry[1], diag) + head(1, kb, carry[2], carry[3], diag)

        init = (jnp.full((bq, 1), -jnp.inf, F32), jnp.zeros((bq, 128), F32))
        alone = [lax.fori_loop(k0[j], both0, lambda kb, c, j=j: head(j, kb, c[0], c[1], None), init) for j in (0, 1)]
        carry = lax.fori_loop(both0, qf * i, lambda kb, c: pair(kb, c, None), alone[0] + alone[1])
        for d in range(qf):
            carry = pair(qf * i + d, carry, d * bk)
        outs = []
        for j in (0, 1):
            m, acc = carry[2 * j], carry[2 * j + 1]
            spare = 64 * (1 - j)
            l = acc[:, spare:spare + 1]
            outs.append((acc / l, m + jnp.log(l)))
        msk = _head_mask((bq, 128), 0)
        o_ref[...] = jnp.where(msk, outs[0][0], outs[1][0]).astype(BF16)
        lse_ref[...] = jnp.where(msk, outs[0][1], outs[1][1])

    res = pl.BlockSpec((t, 256), lambda h, i, tbl: (0, h))
    out = pl.BlockSpec((bq, 128), lambda h, i, tbl: (i, h))
    return pl.pallas_call(
        body,
        name=name,
        grid_spec=pltpu.PrefetchScalarGridSpec(
            num_scalar_prefetch=1, grid=(4, nq),
            in_specs=[pl.BlockSpec((bq, 256), lambda h, i, tbl: (i, h)), res, res],
            out_specs=[out, out]),
        out_shape=[jax.ShapeDtypeStruct((t, A_WIDTH), BF16), jax.ShapeDtypeStruct((t, A_WIDTH), F32)],
        compiler_params=_params("parallel", "parallel"),
    )(kstart, qa, ka, va)


def _fox_bwd(qb, ka, va, dob, qend, name):
    t = qb.shape[0]
    bq = min(ATT_BLOCK, t)
    nq = t // bq
    kf = BWD_K_BLOCKS if t % (BWD_K_BLOCKS * bq) == 0 else 1
    bk = kf * bq
    nk = t // bk

    def body(qe_ref, k_ref, v_ref, q_hbm, do_hbm, dk_ref, dv_ref, cs_ref, dq_hbm, rs_hbm, q_scr, do_scr, dq_scr,
             sems):
        jb = pl.program_id(1)
        hp = pl.program_id(0)
        pair_cols = pl.ds(pl.multiple_of(hp * 256, 256), 256)

        @pl.when(jb == 0)
        def _():
            loads = [pltpu.make_async_copy(q_hbm.at[:, pair_cols], q_scr, sems.at[0]),
                     pltpu.make_async_copy(do_hbm.at[:, pair_cols], do_scr, sems.at[1])]
            for cp in loads:
                cp.start()
            dq_scr[...] = jnp.zeros_like(dq_scr)
            for cp in loads:
                cp.wait()

        i1 = [qe_ref[(2 * hp + j) * nq + kf * jb + kf - 1] + 1 for j in (0, 1)]
        both1 = jnp.minimum(i1[0], i1[1])

        def head(j, ib, dk_acc, dv_acc, diag):
            rows = pl.ds(pl.multiple_of(ib * bq, bq), bq)
            cols = slice(128 * j, 128 * (j + 1))
            qs = q_scr[rows, cols]
            dos = do_scr[rows, cols]
            kj = k_ref[:, cols]
            st = _dot(kj, qs, "nt")
            if diag is not None:
                r = lax.broadcasted_iota(jnp.int32, (bk, bq), 0)
                c = lax.broadcasted_iota(jnp.int32, (bk, bq), 1)
                st = jnp.where(c - r >= -diag, st, -jnp.inf)
            pt = jnp.exp(st)
            dst = (pt * _dot(v_ref[:, cols], dos, "nt")).astype(BF16)
            dq_scr[rows, cols] += _dot(dst, kj, "tn")
            return dk_acc + _dot(dst, qs), dv_acc + _dot(pt, dos)

        def pair(ib, carry, diag):
            return head(0, ib, carry[0], carry[1], diag) + head(1, ib, carry[2], carry[3], diag)

        carry = (jnp.zeros((bk, 128), F32),) * 4
        for d in range(kf):
            carry = pair(kf * jb + d, carry, d * bq)
        first = kf * jb + kf
        carry = lax.fori_loop(first, both1, lambda ib, c: pair(ib, c, None), carry)
        alone = [lax.fori_loop(jnp.maximum(both1, first), i1[j],
                               lambda ib, c, j=j: head(j, ib, c[0], c[1], None), carry[2 * j:2 * j + 2])
                 for j in (0, 1)]
        carry = alone[0] + alone[1]
        outs = []
        for j in (0, 1):
            spare = 64 * (1 - j)
            dk_acc, dv_acc = carry[2 * j], carry[2 * j + 1]
            outs.append((dk_acc, dv_acc, dk_acc[:, spare + 3:spare + 4]))
        msk = _head_mask((bk, 128), 0)
        dk_ref[...] = jnp.where(msk, outs[0][0], outs[1][0]).astype(BF16)
        dv_ref[...] = jnp.where(msk, outs[0][1], outs[1][1]).astype(BF16)
        cs_ref[...] = jnp.where(msk, outs[0][2], outs[1][2])

        @pl.when(jb == nk - 1)
        def _():
            first_head = _head_mask((bq, 128), 0)

            def finish(r, carry):
                rows = pl.ds(pl.multiple_of(r * bq, bq), bq)
                x0, x1 = dq_scr[rows, 0:128], dq_scr[rows, 128:256]
                q_scr[rows, 0:128] = (jnp.where(first_head, x0, x1) * 0.125).astype(BF16)
                dq_scr[rows, 0:128] = jnp.where(first_head, x0[:, 64:65], x1[:, 0:1])
                return carry

            lax.fori_loop(0, nq, finish, 0)
            head_cols = pl.ds(pl.multiple_of(hp * 128, 128), 128)
            stores = [pltpu.make_async_copy(q_scr.at[:, 0:128], dq_hbm.at[:, head_cols], sems.at[0]),
                      pltpu.make_async_copy(dq_scr.at[:, 0:128], rs_hbm.at[:, head_cols], sems.at[1])]
            for cp in stores:
                cp.start()
            for cp in stores:
                cp.wait()

    blk = pl.BlockSpec((bk, 256), lambda h, i, tbl: (i, h))
    out = pl.BlockSpec((bk, 128), lambda h, i, tbl: (i, h))
    return pl.pallas_call(
        body,
        name=name,
        grid_spec=pltpu.PrefetchScalarGridSpec(
            num_scalar_prefetch=1, grid=(4, nk), in_specs=[blk, blk, _ANY, _ANY],
            out_specs=[out, out, out, _ANY, _ANY],
            scratch_shapes=[pltpu.VMEM((t, 256), BF16), pltpu.VMEM((t, 256), BF16), pltpu.VMEM((t, 256), F32),
                            pltpu.SemaphoreType.DMA((2,))]),
        out_shape=[jax.ShapeDtypeStruct((t, A_WIDTH), BF16), jax.ShapeDtypeStruct((t, A_WIDTH), BF16),
                   jax.ShapeDtypeStruct((t, A_WIDTH), F32), jax.ShapeDtypeStruct((t, A_WIDTH), BF16),
                   jax.ShapeDtypeStruct((t, A_WIDTH), F32)],
        compiler_params=_params("arbitrary", "arbitrary"),
    )(qend, ka, va, qb, dob)


HG_ROWS = 256


def _hg_gates(hb_ref, rows, lbv):
    qb = hb_ref[rows, 0:B_WIDTH]
    fb = hb_ref[rows, B_WIDTH:2 * B_WIDTH]
    v = hb_ref[rows, 2 * B_WIDTH:3 * B_WIDTH]
    gb = hb_ref[rows, 3 * B_WIDTH:4 * B_WIDTH]
    sg = _sigmoid(fb)
    f = lbv + (1.0 - lbv) * sg
    sq = _sigmoid(qb)
    return qb, sq, qb * sq, sg, f, 1.0 - f, jnp.log(f), v, gb


def _hg_intra_factors(q, k, b):
    fac = []
    for i in range(CHUNK // SUB):
        bi = b[SUB * i:SUB * i + 1, :]
        eq = jnp.exp(b[SUB * i:SUB * (i + 1), :] - bi)
        ek = jnp.exp(jnp.minimum(bi - b, EXP_CLAMP))
        fac.append((eq, ek, q[SUB * i:SUB * (i + 1), :] * eq, k * ek))
    return fac


def _causal(n):
    r = lax.broadcasted_iota(jnp.int32, (n, n), 0)
    c = lax.broadcasted_iota(jnp.int32, (n, n), 1)
    return r >= c


def _hgrn_fwd(rest, lb, ng, name, ride=()):
    t = rest.shape[0]
    bt = min(HG_ROWS, t)
    ncb = bt // CHUNK
    n = len(ride)
    nsteps = t // bt

    def body(hb_ref, lb_ref, ng_ref, *refs):
        ride_in, (y_ref, o_ref, st_ref), ride_out = refs[:n], refs[n:n + 3], refs[n + 3:2 * n + 3]
        s_scr, sems = refs[2 * n + 3], refs[2 * n + 4:]

        @pl.when(pl.program_id(0) == 0)
        def _():
            s_scr[...] = jnp.zeros_like(s_scr)
            if n:
                _gather_start(ride_in, ride_out, sems)

        tril = _tri(CHUNK, True)
        causal = _causal(CHUNK)
        ones = jnp.ones((CHUNK, HD), F32)

        def chunk(c, carry):
            rows = pl.ds(pl.multiple_of(c * CHUNK, CHUNK), CHUNK)
            _, _, q_all, _, _, k_all, g_all, v_all, gb_all = _hg_gates(hb_ref, rows, lb_ref[...])
            b_all = _dot_hi(tril, g_all)
            qd_all = q_all * jnp.exp(b_all)
            kd_all = k_all * jnp.exp(b_all[CHUNK - 1:CHUNK, :] - b_all)
            eb_all = jnp.exp(_dot_hi(g_all, ones, "tn", exact="b"))
            sgb_all = _sigmoid(gb_all)
            for h in range(B_HEADS):
                cols = slice(h * HD, (h + 1) * HD)
                v = v_all[:, cols]
                s0 = s_scr[h]
                st_ref[c, h] = s0
                o = _dot(qd_all[:, cols], s0)
                fac = _hg_intra_factors(q_all[:, cols], k_all[:, cols], b_all[:, cols])
                a = jnp.concatenate([_dot(qe, ke, "nt") for _, _, qe, ke in fac], axis=0)
                o = o + _dot(jnp.where(causal, a, 0.0), v)
                s_scr[h] = eb_all[h * HD:(h + 1) * HD, :] * s0 + _dot(kd_all[:, cols], v, "tn")
                r = lax.rsqrt(jnp.mean(o * o, axis=-1, keepdims=True) + RMS_EPS)
                o_ref[rows, cols] = o
                y_ref[rows, cols] = (o * r * ng_ref[...] * sgb_all[:, cols]).astype(BF16)
            return carry

        lax.fori_loop(0, ncb, chunk, 0, unroll=2)

        if n:
            @pl.when(pl.program_id(0) == nsteps - 1)
            def _():
                _gather_finish(ride_in, ride_out, sems)

    res = pl.pallas_call(
        body,
        name=name,
        grid=(nsteps,),
        in_specs=[pl.BlockSpec((bt, 4 * B_WIDTH), lambda i: (i, 1)), pl.BlockSpec((1, B_WIDTH), lambda i: (0, 0)),
                  pl.BlockSpec((1, HD), lambda i: (0, 0))] + [_ANY] * n,
        out_specs=[pl.BlockSpec((bt, B_WIDTH), lambda i: (i, 0)), pl.BlockSpec((bt, B_WIDTH), lambda i: (i, 0)),
                   pl.BlockSpec((ncb, B_HEADS, HD, HD), lambda i: (i, 0, 0, 0))] + [_ANY] * n,
        out_shape=[jax.ShapeDtypeStruct((t, B_WIDTH), BF16), jax.ShapeDtypeStruct((t, B_WIDTH), F32),
                   jax.ShapeDtypeStruct((t // CHUNK, B_HEADS, HD, HD), F32)] + _gather_out_shapes(ride),
        scratch_shapes=[pltpu.VMEM((B_HEADS, HD, HD), F32)] + (_gather_scratch(n) if n else []),
        compiler_params=_params("arbitrary"),
    )(rest, lb, ng, *ride)
    return res[:3], res[3:]


def _hgrn_bwd(dy, rest, o_saved, states, lb, ng, name):
    t = rest.shape[0]
    bt = min(HG_ROWS, t)
    ncb = bt // CHUNK
    nb = t // bt

    def body(dy_ref, hb_ref, o_ref, st_ref, lb_ref, ng_ref, dh_ref, dlb_ref, dng_ref, ds_scr):
        @pl.when(pl.program_id(0) == 0)
        def _():
            ds_scr[...] = jnp.zeros_like(ds_scr)
            dlb_ref[...] = jnp.zeros_like(dlb_ref)
            dng_ref[...] = jnp.zeros_like(dng_ref)

        tril = _tri(CHUNK, True)
        triu = _tri(CHUNK, False)
        causal = _causal(CHUNK)
        ones = jnp.ones((CHUNK, HD), F32)
        ones8 = jnp.ones((8, HD), F32)
        last_row = lax.broadcasted_iota(jnp.int32, (CHUNK, B_WIDTH), 0) == CHUNK - 1

        def chunk(cc, carry):
            dng_acc, dlb_acc = carry
            c = ncb - 1 - cc
            rows = pl.ds(pl.multiple_of(c * CHUNK, CHUNK), CHUNK)
            lbv = lb_ref[...]
            qb, sq, q_all, sg, f, k_all, g_all, v_all, gb = _hg_gates(hb_ref, rows, lbv)
            b_all = _dot_hi(tril, g_all)
            ebt_all = jnp.exp(b_all)
            blast = b_all[CHUNK - 1:CHUNK, :]
            ekd_all = jnp.exp(blast - b_all)
            eb_all = jnp.exp(_dot_hi(g_all, ones, "tn", exact="b"))
            sgb = _sigmoid(gb)
            dy_all = dy_ref[rows, :].astype(F32)
            don_all = dy_all * sgb
            ngv = ng_ref[...]
            dq_l, dk_l, dks_l, dv_l, on_l, prod_l = [], [], [], [], [], []
            for h in range(B_HEADS):
                cols = slice(h * HD, (h + 1) * HD)
                q, k, v = q_all[:, cols], k_all[:, cols], v_all[:, cols]
                o = o_ref[rows, cols]
                don = don_all[:, cols]
                r = lax.rsqrt(jnp.mean(o * o, axis=-1, keepdims=True) + RMS_EPS)
                on_l.append(o * r * ngv)
                dng_acc = dng_acc + jnp.sum(don * o * r, axis=0, keepdims=True)
                doh = don * ngv
                do = r * (doh - o * (r * r) * jnp.mean(doh * o, axis=-1, keepdims=True))
                ebt, ekd = ebt_all[:, cols], ekd_all[:, cols]
                s0 = st_ref[c, h]
                ds1 = ds_scr[h]
                fac = _hg_intra_factors(q, k, b_all[:, cols])
                a = jnp.concatenate([_dot(qe, ke, "nt") for _, _, qe, ke in fac], axis=0)
                a = jnp.where(causal, a, 0.0)
                da = jnp.where(causal, _dot(do, v, "nt"), 0.0)
                dv_l.append(_dot(a, do, "tn") + _dot(k * ekd, ds1))
                dq = ebt * _dot(do, s0, "nt")
                dq_l.append(dq + jnp.concatenate(
                    [eq * _hdot(da[SUB * i:SUB * (i + 1), :], ke) for i, (eq, _, _, ke) in enumerate(fac)], axis=0))
                dk_state = ekd * _dot(v, ds1, "nt")
                dk = dk_state
                for i, (_, ek, qe, _) in enumerate(fac):
                    dk = dk + ek * _hdot(da[SUB * i:SUB * (i + 1), :], qe, "tn")
                dk_l.append(dk)
                dks_l.append(dk_state)
                prod_l.append(ds1 * s0)
                ds_scr[h] = _dot(q * ebt, do, "tn") + eb_all[h * HD:(h + 1) * HD, :] * ds1
            dq_all, dk_all = jnp.concatenate(dq_l, axis=1), jnp.concatenate(dk_l, axis=1)
            extra = jnp.exp(blast) * _dot_hi(ones8, jnp.concatenate(prod_l, axis=0), "nt")[0:1, :] \
                + jnp.sum(k_all * jnp.concatenate(dks_l, axis=1), axis=0, keepdims=True)
            db = q_all * dq_all - k_all * dk_all + jnp.where(last_row, extra, 0.0)
            df = _dot_hi(triu, db) / f - dk_all
            dlb_acc = dlb_acc + jnp.sum(df * (1.0 - sg), axis=0, keepdims=True)
            dh_ref[rows, 0:B_WIDTH] = (dq_all * (sq * (1.0 + qb * (1.0 - sq)))).astype(BF16)
            dh_ref[rows, B_WIDTH:2 * B_WIDTH] = (df * (1.0 - lbv) * sg * (1.0 - sg)).astype(BF16)
            dh_ref[rows, 2 * B_WIDTH:3 * B_WIDTH] = jnp.concatenate(dv_l, axis=1).astype(BF16)
            dh_ref[rows, 3 * B_WIDTH:4 * B_WIDTH] = (dy_all * jnp.concatenate(on_l, axis=1)
                                                     * sgb * (1.0 - sgb)).astype(BF16)
            return dng_acc, dlb_acc

        dng_sum, dlb_sum = lax.fori_loop(0, ncb, chunk, (jnp.zeros((1, HD), F32), jnp.zeros((1, B_WIDTH), F32)))
        dng_ref[...] += dng_sum
        dlb_ref[...] += dlb_sum

    rev = lambda i: (nb - 1 - i, 0)
    return pl.pallas_call(
        body,
        name=name,
        grid=(nb,),
        in_specs=[pl.BlockSpec((bt, B_WIDTH), rev), pl.BlockSpec((bt, 4 * B_WIDTH), lambda i: (nb - 1 - i, 1)),
                  pl.BlockSpec((bt, B_WIDTH), rev),
                  pl.BlockSpec((ncb, B_HEADS, HD, HD), lambda i: (nb - 1 - i, 0, 0, 0)),
                  pl.BlockSpec((1, B_WIDTH), lambda i: (0, 0)), pl.BlockSpec((1, HD), lambda i: (0, 0))],
        out_specs=[pl.BlockSpec((bt, 4 * B_WIDTH), rev), pl.BlockSpec((1, B_WIDTH), lambda i: (0, 0)),
                   pl.BlockSpec((1, HD), lambda i: (0, 0))],
        out_shape=[jax.ShapeDtypeStruct((t, 4 * B_WIDTH), BF16), jax.ShapeDtypeStruct((1, B_WIDTH), F32),
                   jax.ShapeDtypeStruct((1, HD), F32)],
        scratch_shapes=[pltpu.VMEM((B_HEADS, HD, HD), F32)],
        compiler_params=_params("arbitrary"),
    )(dy, rest, o_saved, states, lb, ng)


def _axpy2(c0, a0, c1, a1, name, tm=512):
    t, d = a0.shape
    tm = min(tm, t)

    def body(a_ref, b_ref, o_ref):
        o_ref[...] = c0 * a_ref[...] + c1 * b_ref[...]

    row = lambda i: (i, 0)
    return pl.pallas_call(
        body, name=name, grid=(t // tm,),
        in_specs=[pl.BlockSpec((tm, d), row), pl.BlockSpec((tm, d), row)],
        out_specs=pl.BlockSpec((tm, d), row),
        out_shape=jax.ShapeDtypeStruct((t, d), F32),
        compiler_params=_params("parallel"),
    )(a0, a1)


def _split_w_in(w_in_l):
    wqkv = w_in_l[:, :3 * A_WIDTH]
    wfa = jnp.pad(w_in_l[:, 3 * A_WIDTH:3 * A_WIDTH + A_HEADS], ((0, 0), (0, 128 - A_HEADS)))
    whb = w_in_l[:, 3 * A_WIDTH + A_HEADS:3 * A_WIDTH + A_HEADS + 4 * B_WIDTH]
    wgt = w_in_l[:, 3 * A_WIDTH + A_HEADS + 4 * B_WIDTH:]
    return wqkv, jnp.concatenate([wgt, whb, wfa], axis=1), (jnp.concatenate([wqkv, wfa], axis=1), wgt, whb)


def _merge_w_in_grad(d_att, d_gates, d_hb):
    o = 3 * A_WIDTH
    return jnp.concatenate([d_att[:, :o + A_HEADS], d_hb, d_gates], axis=1)


def _layer_fwd(x, xb, w, sp, l, ride=(), late_weights=None):
    t = x.shape[0]
    n = f"l{l}_"
    wqkv, wrest, wgroups = _split_w_in(w["w_in"])
    qkv = _matmul(xb, wqkv, "nn", BF16, MM_ROWS, 768, D_MODEL, n + "proj_qkv")
    rest = _matmul(xb, wrest, "nn", F32, MM_ROWS, 1408, D_MODEL, n + "proj_rest")
    bf = jnp.pad(sp["b_fgate"], (0, 128 - A_HEADS)).reshape(1, 128)
    fcum, fcol = _fox_gate_fwd(rest, bf, n + "fox_gate_fwd")
    qa, ka, va, qn, kn = _fox_prep_fwd(qkv, fcol, n + "fox_prep_fwd")
    kstart, qend = _fox_block_ranges(qn, kn, fcum)
    ya, lse = _fox_fwd(qa, ka, va, kstart, n + "fox_fwd")
    lb = sp["lb"].reshape(1, B_WIDTH)
    ng = sp["norm_g"].reshape(1, HD)
    (yb, ob, states), gathered = _hgrn_fwd(rest, lb, ng, n + "hgrn_fwd", ride)
    if ride:
        late = late_weights(gathered)
        w = {**w, **late[l]}
    merged = _merge_fwd(ya, yb, w["w_pa"], w["w_pb"], rest, n + "merge_fwd")
    x1, x1b, xh1, rs1 = _mm_res_ln(merged, w["w_out"], x, sp["ln1_g"], sp["ln1_b"], n + "out_ln1")
    wu, wg = w["w_ff_in"][:, :FFN_HIDDEN], w["w_ff_in"][:, FFN_HIDDEN:]
    a, hu, hg = _ffn_in_swiglu(x1b, wu, wg, n + "ffn_in_swiglu")
    x2, x2b, xh2, rs2 = _mm_res_ln(a, w["w_ff_out"], x1, sp["ln2_g"], sp["ln2_b"], n + "ffn_out_ln2")
    saved = dict(xb=xb, wgroups=wgroups, qkv=qkv, rest=rest, bf=bf, fcol=fcol, ka=ka, va=va, ya=ya, lse=lse,
                 qend=qend,
                 lb=lb, ng=ng, yb=yb, ob=ob, states=states, merged=merged, x1b=x1b, xh1=xh1, rs1=rs1, a=a,
                 wu=wu, wg=wg, hu=hu, hg=hg,
                 xh2=xh2, rs2=rs2)
    return x2, x2b, saved, (late if ride else None)


def _layer_bwd(dys, coefs, w, sp, s, l):
    n = f"l{l}_"
    dz2, dz2b, dg2, db2 = _ln_bwd(dys, coefs, s["xh2"], s["rs2"], sp["ln2_g"], n + "ln2_bwd")
    du, dg = _ffn_out_dx_swiglu(dz2b, w["w_ff_out"], s["hu"], s["hg"], n + "ffn_out_dx_swiglu")
    d_wffout = _matmul(s["a"], dz2b, "tn", F32, 1408, 1024, DW_ROWS, n + "ffn_out_dw")
    dx1u = _matmul(du, s["wu"], "nt", F32, MM_ROWS, 1024, FFN_HIDDEN, n + "ffn_in_dx_u")
    dx1g = _matmul(dg, s["wg"], "nt", F32, MM_ROWS, 1024, FFN_HIDDEN, n + "ffn_in_dx_g")
    d_wffin = jnp.concatenate([_matmul(s["x1b"], du, "tn", F32, 1024, 1408, DW_ROWS, n + "ffn_in_dw_u"),
                               _matmul(s["x1b"], dg, "tn", F32, 1024, 1408, DW_ROWS, n + "ffn_in_dw_g")], axis=1)
    dz1, dz1b, dg1, db1 = _ln_bwd([dz2, dx1u, dx1g], [ALPHA, 1.0, 1.0], s["xh1"], s["rs1"], sp["ln1_g"],
                                  n + "ln1_bwd")
    d_wout = _matmul(s["merged"], dz1b, "tn", F32, 1024, 1024, DW_ROWS, n + "out_dw")
    dgates, dpa, dpb, dya, dyb = _merge_bwd(dz1b, w["w_out"], s["ya"], s["yb"], w["w_pa"], w["w_pb"], s["rest"],
                                  n + "merge_bwd")
    d_wpa = _matmul(s["ya"], dpa, "tn", F32, 512, 1024, DW_ROWS, n + "pa_dw")
    d_wpb = _matmul(s["yb"], dpb, "tn", F32, 512, 1024, DW_ROWS, n + "pb_dw")
    qb, dob = _fox_prep_bwd(s["qkv"], s["fcol"], s["lse"], dya, s["ya"], n + "fox_prep_bwd")
    dk, dv, csum, dq, rsum = _fox_bwd(qb, s["ka"], s["va"], dob, s["qend"], n + "fox_bwd")
    dfa, dbf = _fox_gate_bwd(rsum, csum, s["rest"], s["bf"], n + "fox_gate_bwd")
    dhb, dlb, dng = _hgrn_bwd(dyb, s["rest"], s["ob"], s["states"], s["lb"], s["ng"], n + "hgrn_bwd")
    datt = jnp.concatenate([dq, dk, dv, dfa], axis=1)
    dxp = _matmul_nt_sum([datt, dgates, dhb], list(s["wgroups"]), n + "proj_dx")
    d_w_in = _merge_w_in_grad(_matmul(s["xb"], datt, "tn", F32, 1024, 1664, DW_ROWS, n + "proj_dw_att"),
                              _matmul(s["xb"], dgates, "tn", F32, 1024, 1024, DW_ROWS, n + "proj_dw_gates"),
                              _matmul(s["xb"], dhb, "tn", F32, 1024, 1024, DW_ROWS, n + "proj_dw_hgrn"))
    grads = dict(w_in=d_w_in, w_pa=d_wpa, w_pb=d_wpb, w_out=d_wout, w_ff_in=d_wffin,
                 w_ff_out=d_wffout, b_fgate=dbf[0, :A_HEADS], lb=dlb[0], norm_g=dng[0], ln1_g=dg1[0], ln1_b=db1[0],
                 ln2_g=dg2[0], ln2_b=db2[0])
    return [dz1, dxp], [ALPHA, 1.0], grads


def _lower_bounds(logits):
    sm = jax.nn.softmax(logits.astype(F32), axis=0)
    return jnp.cumsum(sm, axis=0) - sm[0:1]


def _local_step(x, target, wfull, small, ride=(), late_weights=None):
    lbs, lb_vjp = jax.vjp(_lower_bounds, small["hgrn_lb_logits"])
    h, hb = x, x.astype(BF16)
    wfull = list(wfull)
    saved, sps = [], []
    for l in range(DEPTH):
        sp = dict(b_fgate=small["b_fgate"][l], lb=lbs[l], norm_g=small["hgrn_norm_g"][l], ln1_g=small["ln1_g"][l],
                  ln1_b=small["ln1_b"][l], ln2_g=small["ln2_g"][l], ln2_b=small["ln2_b"][l])
        h, hb, s, late = _layer_fwd(h, hb, wfull[l], sp, l, ride if l == 0 else (), late_weights)
        if late is not None:
            wfull = [{**wfull[k], **late[k]} for k in range(DEPTH)]
        saved.append(s)
        sps.append(sp)
    dy, lpart = _loss_head(h, target)
    dys, coefs = [dy], [1.0]
    grads = [None] * DEPTH
    for l in reversed(range(DEPTH)):
        dys, coefs, grads[l] = _layer_bwd(dys, coefs, wfull[l], sps[l], saved[l], l)
    grad_x = _axpy2(coefs[0], dys[0], coefs[1], dys[1], "grad_x")
    d_logits = lb_vjp(jnp.stack([grads[l]["lb"] for l in range(DEPTH)]))[0]
    return lpart[0, 0], grad_x, grads, d_logits


_BIG = [("w_in", "w_in", (D_MODEL, IN_TOTAL), 1), ("w_branch_a", "w_pa", (A_WIDTH, D_MODEL), 1),
        ("w_branch_b", "w_pb", (B_WIDTH, D_MODEL), 1), ("w_out", "w_out", (D_MODEL, D_MODEL), 0),
        ("w_ff_in", "w_ff_in", (D_MODEL, 2 * FFN_HIDDEN), 1), ("w_ff_out", "w_ff_out", (FFN_HIDDEN, D_MODEL), 0)]
_SMALL = [("b_fgate", A_HEADS), ("hgrn_lb_logits", B_WIDTH), ("hgrn_norm_g", HD), ("ln1_g", D_MODEL),
          ("ln1_b", D_MODEL), ("ln2_g", D_MODEL), ("ln2_b", D_MODEL)]
N_BIG = len(_BIG)
SMALL_ROWS = 80


def _by_chip(full, axis):
    if axis == 0:
        return full.reshape(N_CHIPS, full.shape[0] // N_CHIPS, full.shape[1])
    n = full.shape[1] // N_CHIPS
    return jnp.stack([full[:, q * n:(q + 1) * n] for q in range(N_CHIPS)])


def _from_chips(shards, axis):
    if axis == 0:
        return shards.reshape(N_CHIPS * shards.shape[1], shards.shape[2])
    return jnp.concatenate([shards[q] for q in range(N_CHIPS)], axis=1)


def _pack_small(per_name):
    flat = jnp.concatenate([per_name[name].reshape(-1) for name, _ in _SMALL])
    return jnp.pad(flat, (0, SMALL_ROWS * 128 - flat.shape[0])).reshape(SMALL_ROWS, 128)


def _unpack_small(slab):
    flat, out, r = slab.reshape(-1), {}, 0
    for name, n in _SMALL:
        out[name] = flat[r:r + DEPTH * n].reshape(DEPTH, n)
        r += DEPTH * n
    return out


_ANY = pl.BlockSpec(memory_space=pl.ANY)


def _place():
    return lax.axis_index("x"), lax.axis_index("y"), lax.axis_index("c")


def _other_chips(x, y):
    return [(1 - x, y), (x, 1 - y), (1 - x, 1 - y)]


def _chip_exchange(mine_of, out_refs, send_sems, recv_sems, local_sems):
    _chip_exchange_start(mine_of, out_refs, send_sems, recv_sems, local_sems)
    _chip_exchange_wait(mine_of, out_refs, send_sems, recv_sems, local_sems)


def _chip_exchange_copies(mine_of, out_refs, send_sems, recv_sems, local_sems):
    x, y, c = _place()
    q = 2 * x + y
    local = [pltpu.make_async_copy(mine_of(w, q), out_ref.at[q], local_sems.at[w]) for w, out_ref in enumerate(out_refs)]
    sends, recvs = [], []
    for k, (px, py) in enumerate(_other_chips(x, y)):
        for w, out_ref in enumerate(out_refs):
            sems = dict(send_sem=send_sems.at[3 * w + k], recv_sem=recv_sems.at[3 * w + k], device_id=(px, py, c),
                        device_id_type=MESH)
            sends.append(pltpu.make_async_remote_copy(src_ref=mine_of(w, 2 * px + py), dst_ref=out_ref.at[q], **sems))
            recvs.append(pltpu.make_async_remote_copy(src_ref=mine_of(w, q), dst_ref=out_ref.at[2 * px + py], **sems))
    return local, sends, recvs


def _chip_exchange_start(*args):
    local, sends, _ = _chip_exchange_copies(*args)
    for cp in local + sends:
        cp.start()


def _chip_exchange_wait(*args):
    local, sends, recvs = _chip_exchange_copies(*args)
    for cp in recvs:
        cp.wait_recv()
    for cp in sends:
        cp.wait_send()
    for cp in local:
        cp.wait()


def _sem_scratch(n):
    return [pltpu.SemaphoreType.DMA((3 * n,)), pltpu.SemaphoreType.DMA((3 * n,)), pltpu.SemaphoreType.DMA((n,))]


def _gather_scratch(n):
    return _sem_scratch(n) + [pltpu.SemaphoreType.DMA((n,)), pltpu.SemaphoreType.DMA((n,))]


def _gather_out_shapes(mine):
    return [jax.ShapeDtypeStruct((DEPTH, N_CHIPS) + m.shape[1:], m.dtype) for m in mine]


def _gather_start(in_refs, out_refs, sems):
    c = lax.axis_index("c")
    _chip_exchange_start(lambda w, q: in_refs[w].at[c], [o.at[c] for o in out_refs], *sems[:3])


def _gather_finish(in_refs, out_refs, sems):
    x, y, c = _place()
    _chip_exchange_wait(lambda w, q: in_refs[w].at[c], [o.at[c] for o in out_refs], *sems[:3])
    pair_send, pair_recv = sems[3:]
    sibling = (x, y, 1 - c)
    fwds = []
    for w, o in enumerate(out_refs):
        cp = pltpu.make_async_remote_copy(src_ref=o.at[c], dst_ref=o.at[c], send_sem=pair_send.at[w],
                                          recv_sem=pair_recv.at[w], device_id=sibling, device_id_type=MESH)
        cp.start()
        fwds.append(cp)
    for w, o in enumerate(out_refs):
        pltpu.make_async_remote_copy(src_ref=o.at[1 - c], dst_ref=o.at[1 - c], send_sem=pair_send.at[w],
                                     recv_sem=pair_recv.at[w], device_id=sibling, device_id_type=MESH).wait_recv()
    for cp in fwds:
        cp.wait_send()


def _gather_weights(mine):
    n = len(mine)

    def body(*refs):
        in_refs, out_refs, sems = refs[:n], refs[n:2 * n], refs[2 * n:]
        _gather_start(in_refs, out_refs, sems)
        _gather_finish(in_refs, out_refs, sems)

    return pl.pallas_call(
        body, name="gather_weights", in_specs=[_ANY] * n, out_specs=[_ANY] * n,
        out_shape=_gather_out_shapes(mine), scratch_shapes=_gather_scratch(n),
    )(*mine)


def _pair_exchange(gs):
    n = len(gs)

    def body(*refs):
        g_refs, a_refs, send_sems, recv_sems = refs[:n], refs[n:2 * n], refs[2 * n], refs[2 * n + 1]
        x, y, c = _place()
        cps = []
        for w in range(n):
            cp = pltpu.make_async_remote_copy(src_ref=g_refs[w].at[1 - c], dst_ref=a_refs[w], send_sem=send_sems.at[w],
                                              recv_sem=recv_sems.at[w], device_id=(x, y, 1 - c), device_id_type=MESH)
            cp.start()
            cps.append(cp)
        for cp in cps:
            cp.wait()

    return pl.pallas_call(
        body, name="grad_pair_exchange", in_specs=[_ANY] * n, out_specs=[_ANY] * n,
        out_shape=[jax.ShapeDtypeStruct(g.shape[1:], g.dtype) for g in gs],
        scratch_shapes=[pltpu.SemaphoreType.DMA((n,)), pltpu.SemaphoreType.DMA((n,))],
    )(*gs)


def _row_block(rows):
    return math.gcd(rows, 256)


def _pair_sum(g, a, layer, name):
    _, nq, rows, cols = g.shape
    tb = _row_block(rows)

    def body(l_ref, g_ref, a_ref, o_ref):
        o_ref[...] = (g_ref[...].astype(F32) + a_ref[...].astype(F32)).astype(BF16)

    return pl.pallas_call(
        body, name=name,
        grid_spec=pltpu.PrefetchScalarGridSpec(
            num_scalar_prefetch=1, grid=(nq, rows // tb),
            in_specs=[pl.BlockSpec((None, None, tb, cols), lambda q, i, l_ref: (l_ref[0], q, i, 0)),
                      pl.BlockSpec((None, tb, cols), lambda q, i, l_ref: (q, i, 0))],
            out_specs=pl.BlockSpec((None, tb, cols), lambda q, i, l_ref: (q, i, 0))),
        out_shape=jax.ShapeDtypeStruct((nq, rows, cols), BF16),
        compiler_params=_params("parallel", "parallel"),
    )(layer.reshape(1).astype(jnp.int32), g, a)


def _shard_exchange(ps):
    n = len(ps)

    def body(*refs):
        p_refs, b_refs = refs[:n], refs[n:2 * n]
        send_sems, recv_sems, local_sems = refs[2 * n:]
        _chip_exchange(lambda w, q: p_refs[w].at[q], b_refs, send_sems, recv_sems, local_sems)

    return pl.pallas_call(
        body, name="grad_shard_exchange", in_specs=[_ANY] * n, out_specs=[_ANY] * n,
        out_shape=[jax.ShapeDtypeStruct(p.shape, p.dtype) for p in ps],
        scratch_shapes=_sem_scratch(n),
    )(*ps)


def _sum4(b, name):
    _, rows, cols = b.shape
    tb = _row_block(rows)

    def body(b_ref, o_ref):
        o_ref[...] = ((b_ref[0].astype(F32) + b_ref[1].astype(F32)) + b_ref[2].astype(F32)) + b_ref[3].astype(F32)

    return pl.pallas_call(
        body, name=name, grid=(rows // tb,),
        in_specs=[pl.BlockSpec((N_CHIPS, tb, cols), lambda i: (0, i, 0))],
        out_specs=pl.BlockSpec((tb, cols), lambda i: (i, 0)),
        out_shape=jax.ShapeDtypeStruct((rows, cols), F32),
        compiler_params=_params("parallel"),
    )(b)


def _result_exchange(gcs):
    n = len(gcs)

    def body(*refs):
        g_refs, o_refs, send_sems, recv_sems = refs[:n], refs[n:2 * n], refs[2 * n], refs[2 * n + 1]
        x, y, c = _place()
        cps = []
        for w in range(n):
            cp = pltpu.make_async_remote_copy(src_ref=g_refs[w], dst_ref=o_refs[w], send_sem=send_sems.at[w],
                                              recv_sem=recv_sems.at[w], device_id=(x, y, 1 - c), device_id_type=MESH)
            cp.start()
            cps.append(cp)
        for cp in cps:
            cp.wait()

    return pl.pallas_call(
        body, name="grad_result_exchange", in_specs=[_ANY] * n, out_specs=[_ANY] * n,
        out_shape=[jax.ShapeDtypeStruct(g.shape, g.dtype) for g in gcs],
        scratch_shapes=[pltpu.SemaphoreType.DMA((n,)), pltpu.SemaphoreType.DMA((n,))],
    )(*gcs)


def _allreduce_small(v):
    def body(v_ref, o_ref, buf, send_sems, recv_sems):
        x, y, c = _place()
        me = 4 * x + 2 * y + c
        buf[me] = v_ref[...]
        peers = []
        for k in range(1, N_DEV):
            px = 1 - x if k & 4 else x
            py = 1 - y if k & 2 else y
            pc = 1 - c if k & 1 else c
            peers.append((px, py, pc))
        sends = []
        for k, peer in enumerate(peers):
            cp = pltpu.make_async_remote_copy(src_ref=v_ref, dst_ref=buf.at[me], send_sem=send_sems.at[k],
                                              recv_sem=recv_sems.at[k], device_id=peer, device_id_type=MESH)
            cp.start()
            sends.append(cp)
        for k, (px, py, pc) in enumerate(peers):
            pltpu.make_async_remote_copy(src_ref=v_ref, dst_ref=buf.at[4 * px + 2 * py + pc], send_sem=send_sems.at[k],
                                         recv_sem=recv_sems.at[k], device_id=(px, py, pc),
                                         device_id_type=MESH).wait_recv()
        for cp in sends:
            cp.wait_send()
        acc = buf[0]
        for i in range(1, N_DEV):
            acc = acc + buf[i]
        o_ref[...] = acc

    vm = pl.BlockSpec(memory_space=pltpu.VMEM)
    return pl.pallas_call(
        body, name="small_allreduce", in_specs=[vm], out_specs=vm,
        out_shape=jax.ShapeDtypeStruct(v.shape, F32),
        scratch_shapes=[pltpu.VMEM((N_DEV,) + v.shape, F32), pltpu.SemaphoreType.DMA((N_DEV - 1,)),
                        pltpu.SemaphoreType.DMA((N_DEV - 1,))],
    )(v)


def _adam_update(w, g, m, v):
    nm = ADAM_B1 * m + (1.0 - ADAM_B1) * g
    nv = ADAM_B2 * v + (1.0 - ADAM_B2) * (g * g)
    m_hat = nm / (1.0 - ADAM_B1 ** ADAM_STEP)
    v_hat = nv / (1.0 - ADAM_B2 ** ADAM_STEP)
    return -ADAM_LR * (m_hat / (jnp.sqrt(v_hat) + ADAM_EPS) + ADAM_WD * w), nm, nv


def _adamw_small(w, g, m, v, name):
    def body(w_ref, g_ref, m_ref, v_ref, d_ref, nm_ref, nv_ref):
        d_ref[...], nm_ref[...], nv_ref[...] = _adam_update(w_ref[...], g_ref[...], m_ref[...], v_ref[...])

    vm = pl.BlockSpec(memory_space=pltpu.VMEM)
    return pl.pallas_call(
        body, name=name, in_specs=[vm] * 4, out_specs=[vm] * 3,
        out_shape=[jax.ShapeDtypeStruct(w.shape, F32)] * 3,
    )(w, g, m, v)


def _adamw_big(w, m, v, g_own, g_other, layer, name):
    _, rows, cols = w.shape
    tb = _row_block(rows)

    def body(l_ref, w_ref, m_ref, v_ref, go_ref, gx_ref, g_ref, d_ref, nm_ref, nv_ref):
        gv = jnp.where(pl.program_id(0) == l_ref[0], go_ref[...], gx_ref[...])
        g_ref[...] = gv
        d_ref[...], nm_ref[...], nv_ref[...] = _adam_update(w_ref[...], gv, m_ref[...], v_ref[...])

    per_layer = pl.BlockSpec((None, tb, cols), lambda l, i, l_ref: (l, i, 0))
    shared = pl.BlockSpec((tb, cols), lambda l, i, l_ref: (i, 0))
    return pl.pallas_call(
        body, name=name,
        grid_spec=pltpu.PrefetchScalarGridSpec(
            num_scalar_prefetch=1, grid=(DEPTH, rows // tb),
            in_specs=[per_layer, per_layer, per_layer, shared, shared], out_specs=[per_layer] * 4),
        out_shape=[jax.ShapeDtypeStruct(w.shape, F32)] * 4,
        compiler_params=_params("parallel", "parallel"),
    )(layer.reshape(1).astype(jnp.int32), w, m, v, g_own, g_other)


def kernel(x, w_in, b_fgate, hgrn_lb_logits, hgrn_norm_g, w_branch_a, w_branch_b, w_out, ln1_g, ln1_b, w_ff_in, w_ff_out, ln2_g, ln2_b, loss_target, m_w_in, m_b_fgate, m_hgrn_lb_logits, m_hgrn_norm_g, m_w_branch_a, m_w_branch_b, m_w_out, m_ln1_g, m_ln1_b, m_w_ff_in, m_w_ff_out, m_ln2_g, m_ln2_b, v_w_in, v_b_fgate, v_hgrn_lb_logits, v_hgrn_norm_g, v_w_branch_a, v_w_branch_b, v_w_out, v_ln1_g, v_ln1_b, v_w_ff_in, v_w_ff_out, v_ln2_g, v_ln2_b):
    weights = dict(w_in=w_in, b_fgate=b_fgate, hgrn_lb_logits=hgrn_lb_logits, hgrn_norm_g=hgrn_norm_g,
                   w_branch_a=w_branch_a, w_branch_b=w_branch_b, w_out=w_out, ln1_g=ln1_g, ln1_b=ln1_b,
                   w_ff_in=w_ff_in, w_ff_out=w_ff_out, ln2_g=ln2_g, ln2_b=ln2_b)
    mom1 = dict(w_in=m_w_in, b_fgate=m_b_fgate, hgrn_lb_logits=m_hgrn_lb_logits, hgrn_norm_g=m_hgrn_norm_g,
                w_branch_a=m_w_branch_a, w_branch_b=m_w_branch_b, w_out=m_w_out, ln1_g=m_ln1_g, ln1_b=m_ln1_b,
                w_ff_in=m_w_ff_in, w_ff_out=m_w_ff_out, ln2_g=m_ln2_g, ln2_b=m_ln2_b)
    mom2 = dict(w_in=v_w_in, b_fgate=v_b_fgate, hgrn_lb_logits=v_hgrn_lb_logits, hgrn_norm_g=v_hgrn_norm_g,
                w_branch_a=v_w_branch_a, w_branch_b=v_w_branch_b, w_out=v_w_out, ln1_g=v_ln1_g, ln1_b=v_ln1_b,
                w_ff_in=v_w_ff_in, w_ff_out=v_w_ff_out, ln2_g=v_ln2_g, ln2_b=v_ln2_b)
    core = lax.axis_index("c")

    def full_w_in(gathered):
        return _from_chips(jnp.concatenate([gathered[0], gathered[1]], axis=1), 1)

    def late_weights(gathered):
        per_layer = [{key: _from_chips(gathered[1 + w][l], axis) for w, (_, key, _, axis) in enumerate(_BIG[1:])}
                     for l in range(DEPTH)]
        per_layer[1]["w_in"] = full_w_in(gathered[0])
        return per_layer

    w_in_halves = weights["w_in"].astype(BF16).reshape(DEPTH, 2, D_MODEL // 2, IN_TOTAL // N_CHIPS)
    wfull = [{"w_in": full_w_in(_gather_weights([w_in_halves[0]])[0])}, {}]
    ride = [w_in_halves[1]] + [weights[name].astype(BF16) for name, _, _, _ in _BIG[1:]]
    small = {name: weights[name] for name, _ in _SMALL}

    loss_part, grad_x, grads, d_logits = _local_step(x[0], loss_target[0], wfull, small, ride, late_weights)

    g_all = [jnp.stack([_by_chip(grads[l][key], axis) for l in range(DEPTH)]).astype(BF16)
             for _, key, _, axis in _BIG]
    received = _pair_exchange(g_all)
    pair = [_pair_sum(g_all[w], received[w], core, f"grad_pair_sum_{w}") for w in range(N_BIG)]
    by_chip = _shard_exchange(pair)
    g_layer = [_sum4(by_chip[w], f"grad_chip_sum_{w}") for w in range(N_BIG)]
    g_other = _result_exchange(g_layer)
    out_g, out_d, out_m, out_v = {}, {}, {}, {}
    for w, (name, _, _, _) in enumerate(_BIG):
        out_g[name], out_d[name], out_m[name], out_v[name] = _adamw_big(
            weights[name], mom1[name], mom2[name], g_layer[w], g_other[w], core, f"adamw_{name}")

    small_grads = {name: jnp.stack([grads[l][key] for l in range(DEPTH)])
                   for name, key in [("b_fgate", "b_fgate"), ("hgrn_norm_g", "norm_g"), ("ln1_g", "ln1_g"),
                                     ("ln1_b", "ln1_b"), ("ln2_g", "ln2_g"), ("ln2_b", "ln2_b")]}
    small_grads["hgrn_lb_logits"] = d_logits
    gs = _allreduce_small(_pack_small(small_grads))
    ds, ms, vs = _adamw_small(_pack_small(small), gs, _pack_small({n: mom1[n] for n, _ in _SMALL}),
                              _pack_small({n: mom2[n] for n, _ in _SMALL}), "adamw_small")
    for tree, slab in ((out_g, gs), (out_d, ds), (out_m, ms), (out_v, vs)):
        tree.update(_unpack_small(slab))

    loss = lax.psum(loss_part, ("x", "y", "c"))
    order = ["w_in", "b_fgate", "hgrn_lb_logits", "hgrn_norm_g", "w_branch_a", "w_branch_b", "w_out", "ln1_g", "ln1_b",
             "w_ff_in", "w_ff_out", "ln2_g", "ln2_b"]
    return (loss, grad_x[None], *[out_g[n] for n in order], *[out_d[n] for n in order],
            *[out_m[n] for n in order], *[out_v[n] for n in order])
```

```python
import math

import jax
import jax.numpy as jnp
from jax import lax
from jax.experimental import pallas as pl
from jax.experimental.pallas import tpu as pltpu

F32 = jnp.float32
BF16 = jnp.bfloat16

D_MODEL = 1024
DEPTH = 2
A_HEADS = 8
A_WIDTH = 512
B_WIDTH = 512
B_HEADS = 4
HD = 128
CHUNK = 64
SUB = 16
FFN_HIDDEN = 2816
IN_TOTAL = 5640
ALPHA = (2 * DEPTH) ** 0.25
LN_EPS = 1e-5
RMS_EPS = 1e-6
ADAM_LR = 0.001
ADAM_B1 = 0.9
ADAM_B2 = 0.999
ADAM_EPS = 1e-08
ADAM_WD = 0.01
ADAM_STEP = 10
EXP_CLAMP = 60.0

VMEM_LIMIT_BYTES = 56 * 1024 * 1024
MM_ROWS = 1024
DW_ROWS = 2048
N_CHIPS = 4
N_DEV = 8
MESH = pl.DeviceIdType.MESH

_DN = {
    "nn": (((1,), (0,)), ((), ())),
    "nt": (((1,), (1,)), ((), ())),
    "tn": (((0,), (0,)), ((), ())),
}


def _dot(a, b, mode="nn"):
    return lax.dot_general(a.astype(BF16), b.astype(BF16), _DN[mode], preferred_element_type=F32)


def _pieces(x):
    h = x.astype(BF16)
    r = x - h.astype(F32)
    m = r.astype(BF16)
    return h, m, (r - m.astype(F32)).astype(BF16)


def _dot_hi(a, b, mode="nn", exact="a"):
    if exact == "a":
        h, m, l = _pieces(b)
        return (_dot(a, l, mode) + _dot(a, m, mode)) + _dot(a, h, mode)
    h, m, l = _pieces(a)
    return (_dot(l, b, mode) + _dot(m, b, mode)) + _dot(h, b, mode)


def _hdot(a, b, mode="nn"):
    bh, bl, _ = _pieces(b)
    return _dot(a, bl, mode) + _dot(a, bh, mode)


def _params(*sem):
    return pltpu.CompilerParams(dimension_semantics=sem, vmem_limit_bytes=VMEM_LIMIT_BYTES)


def _sigmoid(x):
    return 1.0 / (1.0 + jnp.exp(-x))


def _matmul(a, b, mode, out_dtype, tm, tn, tk, name):
    if mode == "nn":
        (m, k), (k2, n) = a.shape, b.shape
    elif mode == "nt":
        (m, k), (n, k2) = a.shape, b.shape
    else:
        (k, m), (k2, n) = a.shape, b.shape
    assert k == k2, (a.shape, b.shape, mode)
    tm, tn, tk = min(tm, m), min(tn, n), min(tk, k)
    assert m % tm == 0 and n % tn == 0 and k % tk == 0, (a.shape, b.shape, tm, tn, tk)
    nk = k // tk
    if mode == "tn":
        a_spec = pl.BlockSpec((tk, tm), lambda j, i, kk: (kk, i))
    else:
        a_spec = pl.BlockSpec((tm, tk), lambda j, i, kk: (i, kk))
    if mode == "nt":
        b_spec = pl.BlockSpec((tn, tk), lambda j, i, kk: (j, kk))
    else:
        b_spec = pl.BlockSpec((tk, tn), lambda j, i, kk: (kk, j))
    use_acc = nk > 1 and out_dtype != F32

    def body(a_ref, b_ref, o_ref, *scratch):
        p = _dot(a_ref[...], b_ref[...], mode)
        if nk == 1:
            o_ref[...] = p.astype(out_dtype)
            return
        acc_ref = scratch[0] if use_acc else o_ref
        kk = pl.program_id(2)

        @pl.when(kk == 0)
        def _():
            acc_ref[...] = p

        @pl.when(kk > 0)
        def _():
            acc_ref[...] += p

        if use_acc:
            @pl.when(kk == nk - 1)
            def _():
                o_ref[...] = acc_ref[...].astype(out_dtype)

    return pl.pallas_call(
        body,
        name=name,
        grid=(n // tn, m // tm, nk),
        in_specs=[a_spec, b_spec],
        out_specs=pl.BlockSpec((tm, tn), lambda j, i, kk: (i, j)),
        out_shape=jax.ShapeDtypeStruct((m, n), out_dtype),
        scratch_shapes=[pltpu.VMEM((tm, tn), F32)] if use_acc else [],
        compiler_params=_params("parallel", "parallel", "arbitrary"),
    )(a, b)


def _matmul_nt_sum(a_list, b_list, name, plus=None, tm=512):
    m = a_list[0].shape[0]
    n = b_list[0].shape[0]
    tm = min(tm, m)
    np_ = len(a_list)
    extra = [] if plus is None else [plus[1]]

    def body(*refs):
        o_ref = refs[-1]
        acc = _dot(refs[0][...], refs[np_][...], "nt")
        for p in range(1, np_):
            acc = acc + _dot(refs[p][...], refs[np_ + p][...], "nt")
        if plus is not None:
            acc = acc + plus[0] * refs[2 * np_][...]
        o_ref[...] = acc

    row = pl.BlockSpec((tm, n), lambda i: (i, 0))
    return pl.pallas_call(
        body,
        name=name,
        grid=(m // tm,),
        in_specs=[pl.BlockSpec((tm, a.shape[1]), lambda i: (i, 0)) for a in a_list]
        + [pl.BlockSpec(b.shape, lambda i: (0, 0)) for b in b_list] + [row] * len(extra),
        out_specs=row,
        out_shape=jax.ShapeDtypeStruct((m, n), F32),
        compiler_params=_params("parallel"),
    )(*a_list, *b_list, *extra)


def _mm_res_ln(a, w, res, g, b, name, tm=512):
    t, k = a.shape
    d = w.shape[1]
    tm = min(tm, t)

    def body(a_ref, w_ref, r_ref, g_ref, b_ref, y_ref, yb_ref, xh_ref, rs_ref):
        z = ALPHA * r_ref[...] + _dot(a_ref[...], w_ref[...])
        mu = jnp.mean(z, axis=-1, keepdims=True)
        zc = z - mu
        var = jnp.mean(zc * zc, axis=-1, keepdims=True)
        rstd = lax.rsqrt(var + LN_EPS)
        xh = zc * rstd
        y = xh * g_ref[...] + b_ref[...]
        y_ref[...] = y
        yb_ref[...] = y.astype(BF16)
        xh_ref[...] = xh
        rs_ref[...] = rstd

    row = lambda i: (i, 0)
    fix = lambda i: (0, 0)
    return pl.pallas_call(
        body,
        name=name,
        grid=(t // tm,),
        in_specs=[pl.BlockSpec((tm, k), row), pl.BlockSpec((k, d), fix), pl.BlockSpec((tm, d), row),
                  pl.BlockSpec((1, d), fix), pl.BlockSpec((1, d), fix)],
        out_specs=[pl.BlockSpec((tm, d), row), pl.BlockSpec((tm, d), row), pl.BlockSpec((tm, d), row),
                   pl.BlockSpec((tm, 1), row)],
        out_shape=[jax.ShapeDtypeStruct((t, d), F32), jax.ShapeDtypeStruct((t, d), BF16),
                   jax.ShapeDtypeStruct((t, d), F32), jax.ShapeDtypeStruct((t, 1), F32)],
        compiler_params=_params("parallel"),
    )(a, w, res, g.reshape(1, d), b.reshape(1, d))


def _ln_bwd(dys, coefs, xhat, rstd, g, name, tm=512):
    t, d = xhat.shape
    tm = min(tm, t)
    n_in = len(dys)

    def body(*refs):
        dy_refs = refs[:n_in]
        xh_ref, rs_ref, g_ref, dz_ref, dzb_ref, dg_ref, db_ref = refs[n_in:]
        dy = coefs[0] * dy_refs[0][...].astype(F32)
        for c, r in zip(coefs[1:], dy_refs[1:]):
            dy = dy + c * r[...].astype(F32)
        xh = xh_ref[...]
        dxh = dy * g_ref[...]
        m1 = jnp.mean(dxh, axis=-1, keepdims=True)
        m2 = jnp.mean(dxh * xh, axis=-1, keepdims=True)
        dz = rs_ref[...] * (dxh - m1 - xh * m2)
        dz_ref[...] = dz
        dzb_ref[...] = dz.astype(BF16)
        pg = jnp.sum(dy * xh, axis=0, keepdims=True)
        pb = jnp.sum(dy, axis=0, keepdims=True)

        @pl.when(pl.program_id(0) == 0)
        def _():
            dg_ref[...] = pg
            db_ref[...] = pb

        @pl.when(pl.program_id(0) > 0)
        def _():
            dg_ref[...] += pg
            db_ref[...] += pb

    row = lambda i: (i, 0)
    fix = lambda i: (0, 0)
    return pl.pallas_call(
        body,
        name=name,
        grid=(t // tm,),
        in_specs=[pl.BlockSpec((tm, d), row)] * n_in
        + [pl.BlockSpec((tm, d), row), pl.BlockSpec((tm, 1), row), pl.BlockSpec((1, d), fix)],
        out_specs=[pl.BlockSpec((tm, d), row), pl.BlockSpec((tm, d), row), pl.BlockSpec((1, d), fix),
                   pl.BlockSpec((1, d), fix)],
        out_shape=[jax.ShapeDtypeStruct((t, d), F32), jax.ShapeDtypeStruct((t, d), BF16),
                   jax.ShapeDtypeStruct((1, d), F32), jax.ShapeDtypeStruct((1, d), F32)],
        compiler_params=_params("arbitrary"),
    )(*dys, xhat, rstd, g.reshape(1, d))


def _loss_head(y, target, name="loss_head", tm=512):
    t, d = y.shape
    tm = min(tm, t)

    def body(y_ref, t_ref, dy_ref, l_ref):
        e = y_ref[...] - t_ref[...]
        dy_ref[...] = e * (1.0 / d)
        part = jnp.full((8, 128), 0.5 / d, F32) * jnp.sum(e * e)

        @pl.when(pl.program_id(0) == 0)
        def _():
            l_ref[...] = part

        @pl.when(pl.program_id(0) > 0)
        def _():
            l_ref[...] += part

    row = lambda i: (i, 0)
    return pl.pallas_call(
        body,
        name=name,
        grid=(t // tm,),
        in_specs=[pl.BlockSpec((tm, d), row), pl.BlockSpec((tm, d), row)],
        out_specs=[pl.BlockSpec((tm, d), row), pl.BlockSpec((8, 128), lambda i: (0, 0))],
        out_shape=[jax.ShapeDtypeStruct((t, d), F32), jax.ShapeDtypeStruct((8, 128), F32)],
        compiler_params=_params("arbitrary"),
    )(y, target)


FFN_COLS = FFN_HIDDEN // 2


def _ffn_in_swiglu(xb, wu, wg, name, tm=MM_ROWS):
    t, d = xb.shape
    tm = min(tm, t)

    def body(x_ref, wu_ref, wg_ref, a_ref, u_ref, g_ref):
        x = x_ref[...]
        u = _dot(x, wu_ref[...])
        g = _dot(x, wg_ref[...])
        u_ref[...] = u.astype(BF16)
        g_ref[...] = g.astype(BF16)
        a_ref[...] = (g * _sigmoid(g) * u).astype(BF16)

    wspec = pl.BlockSpec((d, FFN_COLS), lambda j, i: (0, j))
    out = pl.BlockSpec((tm, FFN_COLS), lambda j, i: (i, j))
    return pl.pallas_call(
        body,
        name=name,
        grid=(FFN_HIDDEN // FFN_COLS, t // tm),
        in_specs=[pl.BlockSpec((tm, d), lambda j, i: (i, 0)), wspec, wspec],
        out_specs=[out, out, out],
        out_shape=[jax.ShapeDtypeStruct((t, FFN_HIDDEN), BF16)] * 3,
        compiler_params=_params("parallel", "parallel"),
    )(xb, wu, wg)


def _ffn_out_dx_swiglu(dzb, w_ff_out, u, g, name, tm=MM_ROWS):
    t, d = dzb.shape
    tm = min(tm, t)

    def body(dz_ref, w_ref, u_ref, g_ref, du_ref, dg_ref):
        da = _dot(dz_ref[...], w_ref[...], "nt")
        gv = g_ref[...].astype(F32)
        sg = _sigmoid(gv)
        du_ref[...] = (da * gv * sg).astype(BF16)
        dg_ref[...] = (da * u_ref[...].astype(F32) * (sg * (1.0 + gv * (1.0 - sg)))).astype(BF16)

    blk = pl.BlockSpec((tm, FFN_COLS), lambda j, i: (i, j))
    return pl.pallas_call(
        body,
        name=name,
        grid=(FFN_HIDDEN // FFN_COLS, t // tm),
        in_specs=[pl.BlockSpec((tm, d), lambda j, i: (i, 0)), pl.BlockSpec((FFN_COLS, d), lambda j, i: (j, 0)), blk, blk],
        out_specs=[blk, blk],
        out_shape=[jax.ShapeDtypeStruct((t, FFN_HIDDEN), BF16)] * 2,
        compiler_params=_params("parallel", "parallel"),
    )(dzb, w_ff_out, u, g)


def _merge_fwd(ya, yb, wpa, wpb, rest, name, tm=512):
    t = ya.shape[0]
    tm = min(tm, t)

    def body(ya_ref, yb_ref, wa_ref, wb_ref, ga_ref, gb_ref, o_ref):
        pa = _dot(ya_ref[...], wa_ref[...])
        pb = _dot(yb_ref[...], wb_ref[...])
        o_ref[...] = (_sigmoid(ga_ref[...]) * pa + _sigmoid(gb_ref[...]) * pb).astype(BF16)

    row = lambda i: (i, 0)
    fix = lambda i: (0, 0)
    return pl.pallas_call(
        body,
        name=name,
        grid=(t // tm,),
        in_specs=[pl.BlockSpec((tm, A_WIDTH), row), pl.BlockSpec((tm, B_WIDTH), row),
                  pl.BlockSpec((A_WIDTH, D_MODEL), fix), pl.BlockSpec((B_WIDTH, D_MODEL), fix),
                  pl.BlockSpec((tm, D_MODEL), lambda i: (i, 0)), pl.BlockSpec((tm, D_MODEL), lambda i: (i, 1))],
        out_specs=pl.BlockSpec((tm, D_MODEL), row),
        out_shape=jax.ShapeDtypeStruct((t, D_MODEL), BF16),
        compiler_params=_params("parallel"),
    )(ya, yb, wpa, wpb, rest, rest)


def _merge_bwd(dzb, w_out, ya, yb, wpa, wpb, rest, name, tm=512):
    t = ya.shape[0]
    tm = min(tm, t)

    def body(dz_ref, wo_ref, ya_ref, yb_ref, wa_ref, wb_ref, ga_ref, gb_ref, dg_ref, dpa_ref, dpb_ref, dya_ref,
             dyb_ref):
        dm_v = _dot(dz_ref[...], wo_ref[...], "nt")
        pa = _dot(ya_ref[...], wa_ref[...])
        pb = _dot(yb_ref[...], wb_ref[...])
        sa = _sigmoid(ga_ref[...])
        sb = _sigmoid(gb_ref[...])
        dg_ref[:, :D_MODEL] = (dm_v * pa * sa * (1.0 - sa)).astype(BF16)
        dg_ref[:, D_MODEL:] = (dm_v * pb * sb * (1.0 - sb)).astype(BF16)
        dpa = (dm_v * sa).astype(BF16)
        dpb = (dm_v * sb).astype(BF16)
        dpa_ref[...] = dpa
        dpb_ref[...] = dpb
        dya_ref[...] = _dot(dpa, wa_ref[...], "nt").astype(BF16)
        dyb_ref[...] = _dot(dpb, wb_ref[...], "nt")

    row = lambda i: (i, 0)
    fix = lambda i: (0, 0)
    return pl.pallas_call(
        body,
        name=name,
        grid=(t // tm,),
        in_specs=[pl.BlockSpec((tm, D_MODEL), row), pl.BlockSpec((D_MODEL, D_MODEL), fix),
                  pl.BlockSpec((tm, A_WIDTH), row), pl.BlockSpec((tm, B_WIDTH), row),
                  pl.BlockSpec((A_WIDTH, D_MODEL), fix), pl.BlockSpec((B_WIDTH, D_MODEL), fix),
                  pl.BlockSpec((tm, D_MODEL), lambda i: (i, 0)), pl.BlockSpec((tm, D_MODEL), lambda i: (i, 1))],
        out_specs=[pl.BlockSpec((tm, 2 * D_MODEL), row), pl.BlockSpec((tm, D_MODEL), row),
                   pl.BlockSpec((tm, D_MODEL), row), pl.BlockSpec((tm, A_WIDTH), row), pl.BlockSpec((tm, B_WIDTH), row)],
        out_shape=[jax.ShapeDtypeStruct((t, 2 * D_MODEL), BF16), jax.ShapeDtypeStruct((t, D_MODEL), BF16),
                   jax.ShapeDtypeStruct((t, D_MODEL), BF16), jax.ShapeDtypeStruct((t, A_WIDTH), BF16),
                   jax.ShapeDtypeStruct((t, B_WIDTH), F32)],
        compiler_params=_params("parallel"),
    )(dzb, w_out, ya, yb, wpa, wpb, rest, rest)


FA_BLOCK = 4224 // 128 - 1


def _tri(n, lower):
    r = lax.broadcasted_iota(jnp.int32, (n, n), 0)
    c = lax.broadcasted_iota(jnp.int32, (n, n), 1)
    return jnp.where((r >= c) if lower else (r <= c), 1.0, 0.0).astype(F32)


def _head_spread(expand):
    shape = (128, A_WIDTH) if expand else (A_WIDTH, 128)
    r = lax.broadcasted_iota(jnp.int32, shape, 0)
    c = lax.broadcasted_iota(jnp.int32, shape, 1)
    hit = ((c >= 64 * r) & (c < 64 * r + 64)) if expand else (r == 64 * c)
    return jnp.where(hit, 1.0, 0.0).astype(F32)


def _fox_gate_fwd(rest, bf, name, tb=512):
    t = rest.shape[0]
    tb = min(tb, t)

    def body(fa_ref, bf_ref, f_ref, fc_ref, carry):
        @pl.when(pl.program_id(0) == 0)
        def _():
            carry[...] = jnp.zeros_like(carry)

        z = fa_ref[...] + bf_ref[...]
        logf = jnp.minimum(z, 0.0) - jnp.log(1.0 + jnp.exp(-jnp.abs(z)))
        f = _dot_hi(_tri(tb, True), logf) + carry[...]
        f_ref[...] = f
        fc_ref[...] = _dot_hi(f, _head_spread(True), exact="b")
        carry[...] = f[tb - 1:tb, :]

    return pl.pallas_call(
        body,
        name=name,
        grid=(t // tb,),
        in_specs=[pl.BlockSpec((tb, 128), lambda i: (i, FA_BLOCK)), pl.BlockSpec((1, 128), lambda i: (0, 0))],
        out_specs=[pl.BlockSpec((tb, 128), lambda i: (i, 0)), pl.BlockSpec((tb, A_WIDTH), lambda i: (i, 0))],
        out_shape=[jax.ShapeDtypeStruct((t, 128), F32), jax.ShapeDtypeStruct((t, A_WIDTH), F32)],
        scratch_shapes=[pltpu.VMEM((1, 128), F32)],
        compiler_params=_params("arbitrary"),
    )(rest, bf)


def _fox_gate_bwd(rsum, csum, rest, bf, name, tb=512):
    t = rest.shape[0]
    tb = min(tb, t)
    nb = t // tb

    def body(rs_ref, cs_ref, fa_ref, bf_ref, dfa_ref, dbf_ref, carry):
        @pl.when(pl.program_id(0) == 0)
        def _():
            carry[...] = jnp.zeros_like(carry)

        d_f = _dot_hi(rs_ref[...] - cs_ref[...], _head_spread(False), exact="b")
        dlogf = _dot_hi(_tri(tb, False), d_f) + carry[...]
        carry[...] = dlogf[0:1, :]
        z = fa_ref[...] + bf_ref[...]
        dz = dlogf * _sigmoid(-z)
        dfa_ref[...] = dz.astype(BF16)
        part = jnp.sum(dz, axis=0, keepdims=True)

        @pl.when(pl.program_id(0) == 0)
        def _():
            dbf_ref[...] = part

        @pl.when(pl.program_id(0) > 0)
        def _():
            dbf_ref[...] += part

    return pl.pallas_call(
        body,
        name=name,
        grid=(nb,),
        in_specs=[pl.BlockSpec((tb, A_WIDTH), lambda i: (nb - 1 - i, 0)),
                  pl.BlockSpec((tb, A_WIDTH), lambda i: (nb - 1 - i, 0)),
                  pl.BlockSpec((tb, 128), lambda i: (nb - 1 - i, FA_BLOCK)),
                  pl.BlockSpec((1, 128), lambda i: (0, 0))],
        out_specs=[pl.BlockSpec((tb, 128), lambda i: (nb - 1 - i, 0)), pl.BlockSpec((1, 128), lambda i: (0, 0))],
        out_shape=[jax.ShapeDtypeStruct((t, 128), BF16), jax.ShapeDtypeStruct((1, 128), F32)],
        scratch_shapes=[pltpu.VMEM((1, 128), F32)],
        compiler_params=_params("arbitrary"),
    )(rsum, csum, rest, bf)


ATT_BLOCK = 512
FWD_Q_BLOCKS = 2
BWD_K_BLOCKS = 2


def _head_mask(shape, j):
    lane = lax.broadcasted_iota(jnp.int32, shape, 1)
    return (lane < 64) if j == 0 else (lane >= 64)


def _aug_lanes(tb, j):
    lane = lax.broadcasted_iota(jnp.int32, (tb, 128), 1)
    own = (lane < 64) if j == 0 else (lane >= 64)
    return own, lane - 64 * (1 - j)


def _aug_query(own, li, q, pieces):
    h, m, l = pieces
    one, zero = jnp.ones_like(h), jnp.zeros_like(h)
    spare = jnp.where(li == 0, h, jnp.where(li == 1, m, jnp.where(li == 2, l, jnp.where(li < 6, one, zero))))
    return jnp.where(own, q, spare)


def _fox_prep_fwd(qkv, fcol, name, tb=2048):
    t = qkv.shape[0]
    tb = min(tb, t)

    def body(q_ref, k_ref, v_ref, fc_ref, qa_ref, ka_ref, va_ref, qn_ref, kn_ref):
        pieces = _pieces(pltpu.roll(fc_ref[...], 64, 1))
        h, m, l = pieces
        q, k, v = q_ref[...], k_ref[...], v_ref[...]
        first = _head_mask((tb, 128), 0)
        for nrm_ref, x in ((qn_ref, q.astype(F32)), (kn_ref, k.astype(F32))):
            n0 = jnp.max(jnp.sum(jnp.where(first, x * x, 0.0), axis=1, keepdims=True))
            n1 = jnp.max(jnp.sum(jnp.where(first, 0.0, x * x), axis=1, keepdims=True))
            nrm_ref[...] = jnp.where(_head_mask((8, 128), 0), n0, n1)
        one, zero = jnp.ones_like(h), jnp.zeros_like(h)
        for j in (0, 1):
            own, li = _aug_lanes(tb, j)
            cols = slice(128 * j, 128 * (j + 1))
            qa_ref[:, cols] = _aug_query(own, li, q * 0.125, pieces)
            ks = jnp.where(li < 3, one, jnp.where(li == 3, -h, jnp.where(li == 4, -m, jnp.where(li == 5, -l, zero))))
            ka_ref[:, cols] = jnp.where(own, k, ks)
            va_ref[:, cols] = jnp.where(own, v, one)

    blk = pl.BlockSpec((tb, 256), lambda i, h: (i, h))
    nrm = pl.BlockSpec((None, None, 8, 128), lambda i, h: (i, h, 0, 0))
    return pl.pallas_call(
        body, name=name, grid=(t // tb, 4),
        in_specs=[pl.BlockSpec((tb, 128), lambda i, h: (i, h)), pl.BlockSpec((tb, 128), lambda i, h: (i, 4 + h)),
                  pl.BlockSpec((tb, 128), lambda i, h: (i, 8 + h)), pl.BlockSpec((tb, 128), lambda i, h: (i, h))],
        out_specs=[blk, blk, blk, nrm, nrm],
        out_shape=[jax.ShapeDtypeStruct((t, 2 * A_WIDTH), BF16)] * 3
        + [jax.ShapeDtypeStruct((t // tb, 4, 8, 128), F32)] * 2,
        compiler_params=_params("parallel", "parallel"),
    )(qkv, qkv, qkv, fcol)


def _fox_prep_bwd(qkv, fcol, lse, do, o, name, tb=2048):
    t = qkv.shape[0]
    tb = min(tb, t)

    def body(q_ref, fc_ref, lse_ref, do_ref, o_ref, qb_ref, dob_ref):
        pieces = _pieces(pltpu.roll(fc_ref[...] - lse_ref[...], 64, 1))
        q = q_ref[...] * 0.125
        do_v = do_ref[...]
        prod = do_v.astype(F32) * o_ref[...].astype(F32)
        for j in (0, 1):
            own, li = _aug_lanes(tb, j)
            cols = slice(128 * j, 128 * (j + 1))
            qb_ref[:, cols] = _aug_query(own, li, q, pieces)
            delta = jnp.sum(jnp.where(own, prod, 0.0), axis=1, keepdims=True)
            h, m, l = _pieces(jnp.broadcast_to(delta, (tb, 128)))
            ds = jnp.where(li == 0, -h, jnp.where(li == 1, -m, jnp.where(li == 2, -l, jnp.zeros_like(h))))
            dob_ref[:, cols] = jnp.where(own, do_v, ds)

    pair = pl.BlockSpec((tb, 128), lambda i, h: (i, h))
    blk = pl.BlockSpec((tb, 256), lambda i, h: (i, h))
    return pl.pallas_call(
        body, name=name, grid=(t // tb, 4),
        in_specs=[pair, pair, pair, pair, pair],
        out_specs=[blk, blk],
        out_shape=[jax.ShapeDtypeStruct((t, 2 * A_WIDTH), BF16)] * 2,
        compiler_params=_params("parallel", "parallel"),
    )(qkv, fcol, lse, do, o)


def _tile_mask(n, transposed):
    r = lax.broadcasted_iota(jnp.int32, (n, n), 0)
    c = lax.broadcasted_iota(jnp.int32, (n, n), 1)
    return (c >= r) if transposed else (r >= c)


UNDERFLOW = -110.0


def _fox_block_ranges(qn, kn, fcum):
    t = fcum.shape[0]
    blk = min(ATT_BLOCK, t)
    nb = t // blk
    q2 = jnp.max(qn[:, :, 0, ::64].reshape(-1, A_HEADS), axis=0)
    k2 = jnp.max(kn[:, :, 0, ::64].reshape(-1, A_HEADS), axis=0)
    bound = 2.0 * jnp.sqrt(q2 * k2) * 0.125
    f = fcum[:, :A_HEADS]
    first = f[0::blk].T
    last = f[blk - 1::blk].T
    dead = (bound[:, None, None] + first[:, :, None] - last[:, None, :]) < UNDERFLOW
    qi = jnp.arange(nb)[None, :, None]
    kj = jnp.arange(nb)[None, None, :]
    dead = dead & (kj < qi)
    kstart = jnp.sum(dead, axis=2).astype(jnp.int32)
    qend = (kj[0] + jnp.sum((~dead) & (qi > kj), axis=1)).astype(jnp.int32)
    return kstart.reshape(-1), qend.reshape(-1)


def _fox_fwd(qa, ka, va, kstart, name):
    t = qa.shape[0]
    bk = min(ATT_BLOCK, t)
    nk = t // bk
    qf = FWD_Q_BLOCKS if t % (FWD_Q_BLOCKS * bk) == 0 else 1
    bq = qf * bk
    nq = t // bq

    def body(ks_ref, q_ref, k_ref, v_ref, o_ref, lse_ref):
        i = pl.program_id(1)
        hp = pl.program_id(0)
        k0 = [ks_ref[(2 * hp + j) * nk + qf * i] for j in (0, 1)]
        both0 = jnp.maximum(k0[0], k0[1])

        def head(j, kb, m, acc, diag):
            rows = pl.ds(pl.multiple_of(kb * bk, bk), bk)
            cols = slice(128 * j, 128 * (j + 1))
            s = _dot(q_ref[:, cols], k_ref[rows, cols], "nt")
            if diag is not None:
                r = lax.broadcasted_iota(jnp.int32, (bq, bk), 0)
                c = lax.broadcasted_iota(jnp.int32, (bq, bk), 1)
                s = jnp.where(r - c >= diag, s, -jnp.inf)
            m_new = jnp.maximum(m, jnp.max(s, axis=1, keepdims=True))
            return m_new, jnp.exp(m - m_new) * acc + _dot(jnp.exp(s - m_new), v_ref[rows, cols])

        def pair(kb, carry, diag):
            return head(0, kb, carry[0], carry[1], diag) + head(1, kb, carry[2], carry[3], diag)

        init = (jnp.full((bq, 1), -jnp.inf, F32), jnp.zeros((bq, 128), F32))
        alone = [lax.fori_loop(k0[j], both0, lambda kb, c, j=j: head(j, kb, c[0], c[1], None), init) for j in (0, 1)]
        carry = lax.fori_loop(both0, qf * i, lambda kb, c: pair(kb, c, None), alone[0] + alone[1])
        for d in range(qf):
            carry = pair(qf * i + d, carry, d * bk)
        outs = []
        for j in (0, 1):
            m, acc = carry[2 * j], carry[2 * j + 1]
            spare = 64 * (1 - j)
            l = acc[:, spare:spare + 1]
            outs.append((acc / l, m + jnp.log(l)))
        msk = _head_mask((bq, 128), 0)
        o_ref[...] = jnp.where(msk, outs[0][0], outs[1][0]).astype(BF16)
        lse_ref[...] = jnp.where(msk, outs[0][1], outs[1][1])

    res = pl.BlockSpec((t, 256), lambda h, i, tbl: (0, h))
    out = pl.BlockSpec((bq, 128), lambda h, i, tbl: (i, h))
    return pl.pallas_call(
        body,
        name=name,
        grid_spec=pltpu.PrefetchScalarGridSpec(
            num_scalar_prefetch=1, grid=(4, nq),
            in_specs=[pl.BlockSpec((bq, 256), lambda h, i, tbl: (i, h)), res, res],
            out_specs=[out, out]),
        out_shape=[jax.ShapeDtypeStruct((t, A_WIDTH), BF16), jax.ShapeDtypeStruct((t, A_WIDTH), F32)],
        compiler_params=_params("parallel", "parallel"),
    )(kstart, qa, ka, va)


def _fox_bwd(qb, ka, va, dob, qend, name):
    t = qb.shape[0]
    bq = min(ATT_BLOCK, t)
    nq = t // bq
    kf = BWD_K_BLOCKS if t % (BWD_K_BLOCKS * bq) == 0 else 1
    bk = kf * bq
    nk = t // bk

    def body(qe_ref, k_ref, v_ref, q_hbm, do_hbm, dk_ref, dv_ref, cs_ref, dq_hbm, rs_hbm, q_scr, do_scr, dq_scr,
             sems):
        jb = pl.program_id(1)
        hp = pl.program_id(0)
        pair_cols = pl.ds(pl.multiple_of(hp * 256, 256), 256)

        @pl.when(jb == 0)
        def _():
            loads = [pltpu.make_async_copy(q_hbm.at[:, pair_cols], q_scr, sems.at[0]),
                     pltpu.make_async_copy(do_hbm.at[:, pair_cols], do_scr, sems.at[1])]
            for cp in loads:
                cp.start()
            dq_scr[...] = jnp.zeros_like(dq_scr)
            for cp in loads:
                cp.wait()

        i1 = [qe_ref[(2 * hp + j) * nq + kf * jb + kf - 1] + 1 for j in (0, 1)]
        both1 = jnp.minimum(i1[0], i1[1])

        def head(j, ib, dk_acc, dv_acc, diag):
            rows = pl.ds(pl.multiple_of(ib * bq, bq), bq)
            cols = slice(128 * j, 128 * (j + 1))
            qs = q_scr[rows, cols]
            dos = do_scr[rows, cols]
            kj = k_ref[:, cols]
            st = _dot(kj, qs, "nt")
            if diag is not None:
                r = lax.broadcasted_iota(jnp.int32, (bk, bq), 0)
                c = lax.broadcasted_iota(jnp.int32, (bk, bq), 1)
                st = jnp.where(c - r >= -diag, st, -jnp.inf)
            pt = jnp.exp(st)
            dst = (pt * _dot(v_ref[:, cols], dos, "nt")).astype(BF16)
            dq_scr[rows, cols] += _dot(dst, kj, "tn")
            return dk_acc + _dot(dst, qs), dv_acc + _dot(pt, dos)

        def pair(ib, carry, diag):
            return head(0, ib, carry[0], carry[1], diag) + head(1, ib, carry[2], carry[3], diag)

        carry = (jnp.zeros((bk, 128), F32),) * 4
        for d in range(kf):
            carry = pair(kf * jb + d, carry, d * bq)
        first = kf * jb + kf
        carry = lax.fori_loop(first, both1, lambda ib, c: pair(ib, c, None), carry)
        alone = [lax.fori_loop(jnp.maximum(both1, first), i1[j],
                               lambda ib, c, j=j: head(j, ib, c[0], c[1], None), carry[2 * j:2 * j + 2])
                 for j in (0, 1)]
        carry = alone[0] + alone[1]
        outs = []
        for j in (0, 1):
            spare = 64 * (1 - j)
            dk_acc, dv_acc = carry[2 * j], carry[2 * j + 1]
            outs.append((dk_acc, dv_acc, dk_acc[:, spare + 3:spare + 4]))
        msk = _head_mask((bk, 128), 0)
        dk_ref[...] = jnp.where(msk, outs[0][0], outs[1][0]).astype(BF16)
        dv_ref[...] = jnp.where(msk, outs[0][1], outs[1][1]).astype(BF16)
        cs_ref[...] = jnp.where(msk, outs[0][2], outs[1][2])

        @pl.when(jb == nk - 1)
        def _():
            first_head = _head_mask((bq, 128), 0)

            def finish(r, carry):
                rows = pl.ds(pl.multiple_of(r * bq, bq), bq)
                x0, x1 = dq_scr[rows, 0:128], dq_scr[rows, 128:256]
                q_scr[rows, 0:128] = (jnp.where(first_head, x0, x1) * 0.125).astype(BF16)
                dq_scr[rows, 0:128] = jnp.where(first_head, x0[:, 64:65], x1[:, 0:1])
                return carry

            lax.fori_loop(0, nq, finish, 0)
            head_cols = pl.ds(pl.multiple_of(hp * 128, 128), 128)
            stores = [pltpu.make_async_copy(q_scr.at[:, 0:128], dq_hbm.at[:, head_cols], sems.at[0]),
                      pltpu.make_async_copy(dq_scr.at[:, 0:128], rs_hbm.at[:, head_cols], sems.at[1])]
            for cp in stores:
                cp.start()
            for cp in stores:
                cp.wait()

    blk = pl.BlockSpec((bk, 256), lambda h, i, tbl: (i, h))
    out = pl.BlockSpec((bk, 128), lambda h, i, tbl: (i, h))
    return pl.pallas_call(
        body,
        name=name,
        grid_spec=pltpu.PrefetchScalarGridSpec(
            num_scalar_prefetch=1, grid=(4, nk), in_specs=[blk, blk, _ANY, _ANY],
            out_specs=[out, out, out, _ANY, _ANY],
            scratch_shapes=[pltpu.VMEM((t, 256), BF16), pltpu.VMEM((t, 256), BF16), pltpu.VMEM((t, 256), F32),
                            pltpu.SemaphoreType.DMA((2,))]),
        out_shape=[jax.ShapeDtypeStruct((t, A_WIDTH), BF16), jax.ShapeDtypeStruct((t, A_WIDTH), BF16),
                   jax.ShapeDtypeStruct((t, A_WIDTH), F32), jax.ShapeDtypeStruct((t, A_WIDTH), BF16),
                   jax.ShapeDtypeStruct((t, A_WIDTH), F32)],
        compiler_params=_params("arbitrary", "arbitrary"),
    )(qend, ka, va, qb, dob)


HG_ROWS = 256


def _hg_gates(hb_ref, rows, lbv):
    qb = hb_ref[rows, 0:B_WIDTH]
    fb = hb_ref[rows, B_WIDTH:2 * B_WIDTH]
    v = hb_ref[rows, 2 * B_WIDTH:3 * B_WIDTH]
    gb = hb_ref[rows, 3 * B_WIDTH:4 * B_WIDTH]
    sg = _sigmoid(fb)
    f = lbv + (1.0 - lbv) * sg
    sq = _sigmoid(qb)
    return qb, sq, qb * sq, sg, f, 1.0 - f, jnp.log(f), v, gb


def _hg_intra_factors(q, k, b):
    fac = []
    for i in range(CHUNK // SUB):
        bi = b[SUB * i:SUB * i + 1, :]
        eq = jnp.exp(b[SUB * i:SUB * (i + 1), :] - bi)
        ek = jnp.exp(jnp.minimum(bi - b, EXP_CLAMP))
        fac.append((eq, ek, q[SUB * i:SUB * (i + 1), :] * eq, k * ek))
    return fac


def _causal(n):
    r = lax.broadcasted_iota(jnp.int32, (n, n), 0)
    c = lax.broadcasted_iota(jnp.int32, (n, n), 1)
    return r >= c


def _hgrn_fwd(rest, lb, ng, name, ride=()):
    t = rest.shape[0]
    bt = min(HG_ROWS, t)
    ncb = bt // CHUNK
    n = len(ride)
    nsteps = t // bt

    def body(hb_ref, lb_ref, ng_ref, *refs):
        ride_in, (y_ref, o_ref, st_ref), ride_out = refs[:n], refs[n:n + 3], refs[n + 3:2 * n + 3]
        s_scr, sems = refs[2 * n + 3], refs[2 * n + 4:]

        @pl.when(pl.program_id(0) == 0)
        def _():
            s_scr[...] = jnp.zeros_like(s_scr)
            if n:
                _gather_start(ride_in, ride_out, sems)

        tril = _tri(CHUNK, True)
        causal = _causal(CHUNK)
        ones = jnp.ones((CHUNK, HD), F32)

        def chunk(c, carry):
            rows = pl.ds(pl.multiple_of(c * CHUNK, CHUNK), CHUNK)
            _, _, q_all, _, _, k_all, g_all, v_all, gb_all = _hg_gates(hb_ref, rows, lb_ref[...])
            b_all = _dot_hi(tril, g_all)
            qd_all = q_all * jnp.exp(b_all)
            kd_all = k_all * jnp.exp(b_all[CHUNK - 1:CHUNK, :] - b_all)
            eb_all = jnp.exp(_dot_hi(g_all, ones, "tn", exact="b"))
            sgb_all = _sigmoid(gb_all)
            for h in range(B_HEADS):
                cols = slice(h * HD, (h + 1) * HD)
                v = v_all[:, cols]
                s0 = s_scr[h]
                st_ref[c, h] = s0
                o = _dot(qd_all[:, cols], s0)
                fac = _hg_intra_factors(q_all[:, cols], k_all[:, cols], b_all[:, cols])
                a = jnp.concatenate([_dot(qe, ke, "nt") for _, _, qe, ke in fac], axis=0)
                o = o + _dot(jnp.where(causal, a, 0.0), v)
                s_scr[h] = eb_all[h * HD:(h + 1) * HD, :] * s0 + _dot(kd_all[:, cols], v, "tn")
                r = lax.rsqrt(jnp.mean(o * o, axis=-1, keepdims=True) + RMS_EPS)
                o_ref[rows, cols] = o
                y_ref[rows, cols] = (o * r * ng_ref[...] * sgb_all[:, cols]).astype(BF16)
            return carry

        lax.fori_loop(0, ncb, chunk, 0, unroll=2)

        if n:
            @pl.when(pl.program_id(0) == nsteps - 1)
            def _():
                _gather_finish(ride_in, ride_out, sems)

    res = pl.pallas_call(
        body,
        name=name,
        grid=(nsteps,),
        in_specs=[pl.BlockSpec((bt, 4 * B_WIDTH), lambda i: (i, 1)), pl.BlockSpec((1, B_WIDTH), lambda i: (0, 0)),
                  pl.BlockSpec((1, HD), lambda i: (0, 0))] + [_ANY] * n,
        out_specs=[pl.BlockSpec((bt, B_WIDTH), lambda i: (i, 0)), pl.BlockSpec((bt, B_WIDTH), lambda i: (i, 0)),
                   pl.BlockSpec((ncb, B_HEADS, HD, HD), lambda i: (i, 0, 0, 0))] + [_ANY] * n,
        out_shape=[jax.ShapeDtypeStruct((t, B_WIDTH), BF16), jax.ShapeDtypeStruct((t, B_WIDTH), F32),
                   jax.ShapeDtypeStruct((t // CHUNK, B_HEADS, HD, HD), F32)] + _gather_out_shapes(ride),
        scratch_shapes=[pltpu.VMEM((B_HEADS, HD, HD), F32)] + (_gather_scratch(n) if n else []),
        compiler_params=_params("arbitrary"),
    )(rest, lb, ng, *ride)
    return res[:3], res[3:]


def _hgrn_bwd(dy, rest, o_saved, states, lb, ng, name):
    t = rest.shape[0]
    bt = min(HG_ROWS, t)
    ncb = bt // CHUNK
    nb = t // bt

    def body(dy_ref, hb_ref, o_ref, st_ref, lb_ref, ng_ref, dh_ref, dlb_ref, dng_ref, ds_scr):
        @pl.when(pl.program_id(0) == 0)
        def _():
            ds_scr[...] = jnp.zeros_like(ds_scr)
            dlb_ref[...] = jnp.zeros_like(dlb_ref)
            dng_ref[...] = jnp.zeros_like(dng_ref)

        tril = _tri(CHUNK, True)
        triu = _tri(CHUNK, False)
        causal = _causal(CHUNK)
        ones = jnp.ones((CHUNK, HD), F32)
        ones8 = jnp.ones((8, HD), F32)
        last_row = lax.broadcasted_iota(jnp.int32, (CHUNK, B_WIDTH), 0) == CHUNK - 1

        def chunk(cc, carry):
            dng_acc, dlb_acc = carry
            c = ncb - 1 - cc
            rows = pl.ds(pl.multiple_of(c * CHUNK, CHUNK), CHUNK)
            lbv = lb_ref[...]
            qb, sq, q_all, sg, f, k_all, g_all, v_all, gb = _hg_gates(hb_ref, rows, lbv)
            b_all = _dot_hi(tril, g_all)
            ebt_all = jnp.exp(b_all)
            blast = b_all[CHUNK - 1:CHUNK, :]
            ekd_all = jnp.exp(blast - b_all)
            eb_all = jnp.exp(_dot_hi(g_all, ones, "tn", exact="b"))
            sgb = _sigmoid(gb)
            dy_all = dy_ref[rows, :].astype(F32)
            don_all = dy_all * sgb
            ngv = ng_ref[...]
            dq_l, dk_l, dks_l, dv_l, on_l, prod_l = [], [], [], [], [], []
            for h in range(B_HEADS):
                cols = slice(h * HD, (h + 1) * HD)
                q, k, v = q_all[:, cols], k_all[:, cols], v_all[:, cols]
                o = o_ref[rows, cols]
                don = don_all[:, cols]
                r = lax.rsqrt(jnp.mean(o * o, axis=-1, keepdims=True) + RMS_EPS)
                on_l.append(o * r * ngv)
                dng_acc = dng_acc + jnp.sum(don * o * r, axis=0, keepdims=True)
                doh = don * ngv
                do = r * (doh - o * (r * r) * jnp.mean(doh * o, axis=-1, keepdims=True))
                ebt, ekd = ebt_all[:, cols], ekd_all[:, cols]
                s0 = st_ref[c, h]
                ds1 = ds_scr[h]
                fac = _hg_intra_factors(q, k, b_all[:, cols])
                a = jnp.concatenate([_dot(qe, ke, "nt") for _, _, qe, ke in fac], axis=0)
                a = jnp.where(causal, a, 0.0)
                da = jnp.where(causal, _dot(do, v, "nt"), 0.0)
                dv_l.append(_dot(a, do, "tn") + _dot(k * ekd, ds1))
                dq = ebt * _dot(do, s0, "nt")
                dq_l.append(dq + jnp.concatenate(
                    [eq * _hdot(da[SUB * i:SUB * (i + 1), :], ke) for i, (eq, _, _, ke) in enumerate(fac)], axis=0))
                dk_state = ekd * _dot(v, ds1, "nt")
                dk = dk_state
                for i, (_, ek, qe, _) in enumerate(fac):
                    dk = dk + ek * _hdot(da[SUB * i:SUB * (i + 1), :], qe, "tn")
                dk_l.append(dk)
                dks_l.append(dk_state)
                prod_l.append(ds1 * s0)
                ds_scr[h] = _dot(q * ebt, do, "tn") + eb_all[h * HD:(h + 1) * HD, :] * ds1
            dq_all, dk_all = jnp.concatenate(dq_l, axis=1), jnp.concatenate(dk_l, axis=1)
            extra = jnp.exp(blast) * _dot_hi(ones8, jnp.concatenate(prod_l, axis=0), "nt")[0:1, :] \
                + jnp.sum(k_all * jnp.concatenate(dks_l, axis=1), axis=0, keepdims=True)
            db = q_all * dq_all - k_all * dk_all + jnp.where(last_row, extra, 0.0)
            df = _dot_hi(triu, db) / f - dk_all
            dlb_acc = dlb_acc + jnp.sum(df * (1.0 - sg), axis=0, keepdims=True)
            dh_ref[rows, 0:B_WIDTH] = (dq_all * (sq * (1.0 + qb * (1.0 - sq)))).astype(BF16)
            dh_ref[rows, B_WIDTH:2 * B_WIDTH] = (df * (1.0 - lbv) * sg * (1.0 - sg)).astype(BF16)
            dh_ref[rows, 2 * B_WIDTH:3 * B_WIDTH] = jnp.concatenate(dv_l, axis=1).astype(BF16)
            dh_ref[rows, 3 * B_WIDTH:4 * B_WIDTH] = (dy_all * jnp.concatenate(on_l, axis=1)
                                                     * sgb * (1.0 - sgb)).astype(BF16)
            return dng_acc, dlb_acc

        dng_sum, dlb_sum = lax.fori_loop(0, ncb, chunk, (jnp.zeros((1, HD), F32), jnp.zeros((1, B_WIDTH), F32)))
        dng_ref[...] += dng_sum
        dlb_ref[...] += dlb_sum

    rev = lambda i: (nb - 1 - i, 0)
    return pl.pallas_call(
        body,
        name=name,
        grid=(nb,),
        in_specs=[pl.BlockSpec((bt, B_WIDTH), rev), pl.BlockSpec((bt, 4 * B_WIDTH), lambda i: (nb - 1 - i, 1)),
                  pl.BlockSpec((bt, B_WIDTH), rev),
                  pl.BlockSpec((ncb, B_HEADS, HD, HD), lambda i: (nb - 1 - i, 0, 0, 0)),
                  pl.BlockSpec((1, B_WIDTH), lambda i: (0, 0)), pl.BlockSpec((1, HD), lambda i: (0, 0))],
        out_specs=[pl.BlockSpec((bt, 4 * B_WIDTH), rev), pl.BlockSpec((1, B_WIDTH), lambda i: (0, 0)),
                   pl.BlockSpec((1, HD), lambda i: (0, 0))],
        out_shape=[jax.ShapeDtypeStruct((t, 4 * B_WIDTH), BF16), jax.ShapeDtypeStruct((1, B_WIDTH), F32),
                   jax.ShapeDtypeStruct((1, HD), F32)],
        scratch_shapes=[pltpu.VMEM((B_HEADS, HD, HD), F32)],
        compiler_params=_params("arbitrary"),
    )(dy, rest, o_saved, states, lb, ng)


def _split_w_in(w_in_l):
    wqkv = w_in_l[:, :3 * A_WIDTH]
    wfa = jnp.pad(w_in_l[:, 3 * A_WIDTH:3 * A_WIDTH + A_HEADS], ((0, 0), (0, 128 - A_HEADS)))
    whb = w_in_l[:, 3 * A_WIDTH + A_HEADS:3 * A_WIDTH + A_HEADS + 4 * B_WIDTH]
    wgt = w_in_l[:, 3 * A_WIDTH + A_HEADS + 4 * B_WIDTH:]
    return wqkv, jnp.concatenate([wgt, whb, wfa], axis=1), (jnp.concatenate([wqkv, wfa], axis=1), wgt, whb)


def _merge_w_in_grad(d_att, d_gates, d_hb):
    o = 3 * A_WIDTH
    return jnp.concatenate([d_att[:, :o + A_HEADS], d_hb, d_gates], axis=1)


def _layer_fwd(x, xb, w, sp, l, ride=(), late_weights=None):
    t = x.shape[0]
    n = f"l{l}_"
    wqkv, wrest, wgroups = _split_w_in(w["w_in"])
    qkv = _matmul(xb, wqkv, "nn", BF16, MM_ROWS, 768, D_MODEL, n + "proj_qkv")
    rest = _matmul(xb, wrest, "nn", F32, MM_ROWS, 1408, D_MODEL, n + "proj_rest")
    bf = jnp.pad(sp["b_fgate"], (0, 128 - A_HEADS)).reshape(1, 128)
    fcum, fcol = _fox_gate_fwd(rest, bf, n + "fox_gate_fwd")
    qa, ka, va, qn, kn = _fox_prep_fwd(qkv, fcol, n + "fox_prep_fwd")
    kstart, qend = _fox_block_ranges(qn, kn, fcum)
    ya, lse = _fox_fwd(qa, ka, va, kstart, n + "fox_fwd")
    lb = sp["lb"].reshape(1, B_WIDTH)
    ng = sp["norm_g"].reshape(1, HD)
    (yb, ob, states), gathered = _hgrn_fwd(rest, lb, ng, n + "hgrn_fwd", ride)
    if ride:
        late = late_weights(l, gathered)
        w = {**w, **late[l]}
    merged = _merge_fwd(ya, yb, w["w_pa"], w["w_pb"], rest, n + "merge_fwd")
    x1, x1b, xh1, rs1 = _mm_res_ln(merged, w["w_out"], x, sp["ln1_g"], sp["ln1_b"], n + "out_ln1")
    wu, wg = w["w_ff_in"][:, :FFN_HIDDEN], w["w_ff_in"][:, FFN_HIDDEN:]
    a, hu, hg = _ffn_in_swiglu(x1b, wu, wg, n + "ffn_in_swiglu")
    x2, x2b, xh2, rs2 = _mm_res_ln(a, w["w_ff_out"], x1, sp["ln2_g"], sp["ln2_b"], n + "ffn_out_ln2")
    saved = dict(xb=xb, wgroups=wgroups, qkv=qkv, rest=rest, bf=bf, fcol=fcol, ka=ka, va=va, ya=ya, lse=lse,
                 qend=qend,
                 lb=lb, ng=ng, yb=yb, ob=ob, states=states, merged=merged, x1b=x1b, xh1=xh1, rs1=rs1, a=a,
                 wu=wu, wg=wg, hu=hu, hg=hg,
                 xh2=xh2, rs2=rs2)
    return x2, x2b, saved, (late if ride else None)


def _layer_bwd(dys, coefs, w, sp, s, l):
    n = f"l{l}_"
    dz2, dz2b, dg2, db2 = _ln_bwd(dys, coefs, s["xh2"], s["rs2"], sp["ln2_g"], n + "ln2_bwd")
    du, dg = _ffn_out_dx_swiglu(dz2b, w["w_ff_out"], s["hu"], s["hg"], n + "ffn_out_dx_swiglu")
    d_wffout = _matmul(s["a"], dz2b, "tn", F32, 1408, 1024, DW_ROWS, n + "ffn_out_dw")
    dx1 = _matmul_nt_sum([du, dg], [s["wu"], s["wg"]], n + "ffn_in_dx")
    d_wffin = jnp.concatenate([_matmul(s["x1b"], du, "tn", F32, 1024, 1408, DW_ROWS, n + "ffn_in_dw_u"),
                               _matmul(s["x1b"], dg, "tn", F32, 1024, 1408, DW_ROWS, n + "ffn_in_dw_g")], axis=1)
    dz1, dz1b, dg1, db1 = _ln_bwd([dz2, dx1], [ALPHA, 1.0], s["xh1"], s["rs1"], sp["ln1_g"], n + "ln1_bwd")
    d_wout = _matmul(s["merged"], dz1b, "tn", F32, 1024, 1024, DW_ROWS, n + "out_dw")
    dgates, dpa, dpb, dya, dyb = _merge_bwd(dz1b, w["w_out"], s["ya"], s["yb"], w["w_pa"], w["w_pb"], s["rest"],
                                  n + "merge_bwd")
    d_wpa = _matmul(s["ya"], dpa, "tn", F32, 512, 1024, DW_ROWS, n + "pa_dw")
    d_wpb = _matmul(s["yb"], dpb, "tn", F32, 512, 1024, DW_ROWS, n + "pb_dw")
    qb, dob = _fox_prep_bwd(s["qkv"], s["fcol"], s["lse"], dya, s["ya"], n + "fox_prep_bwd")
    dk, dv, csum, dq, rsum = _fox_bwd(qb, s["ka"], s["va"], dob, s["qend"], n + "fox_bwd")
    dfa, dbf = _fox_gate_bwd(rsum, csum, s["rest"], s["bf"], n + "fox_gate_bwd")
    dhb, dlb, dng = _hgrn_bwd(dyb, s["rest"], s["ob"], s["states"], s["lb"], s["ng"], n + "hgrn_bwd")
    datt = jnp.concatenate([dq, dk, dv, dfa], axis=1)
    dxp = _matmul_nt_sum([datt, dgates, dhb], list(s["wgroups"]), n + "proj_dx", (ALPHA, dz1) if l == 0 else None)
    d_w_in = _merge_w_in_grad(_matmul(s["xb"], datt, "tn", F32, 1024, 1664, DW_ROWS, n + "proj_dw_att"),
                              _matmul(s["xb"], dgates, "tn", F32, 1024, 1024, DW_ROWS, n + "proj_dw_gates"),
                              _matmul(s["xb"], dhb, "tn", F32, 1024, 1024, DW_ROWS, n + "proj_dw_hgrn"))
    grads = dict(w_in=d_w_in, w_pa=d_wpa, w_pb=d_wpb, w_out=d_wout, w_ff_in=d_wffin,
                 w_ff_out=d_wffout, b_fgate=dbf[0, :A_HEADS], lb=dlb[0], norm_g=dng[0], ln1_g=dg1[0], ln1_b=db1[0],
                 ln2_g=dg2[0], ln2_b=db2[0])
    if l == 0:
        return [dxp], [1.0], grads
    return [dz1, dxp], [ALPHA, 1.0], grads


def _lower_bounds(logits):
    sm = jax.nn.softmax(logits.astype(F32), axis=0)
    return jnp.cumsum(sm, axis=0) - sm[0:1]


def _local_step(x, target, wfull, small, rides=None, late_weights=None):
    lbs, lb_vjp = jax.vjp(_lower_bounds, small["hgrn_lb_logits"])
    h, hb = x, x.astype(BF16)
    wfull = list(wfull)
    saved, sps = [], []
    for l in range(DEPTH):
        sp = dict(b_fgate=small["b_fgate"][l], lb=lbs[l], norm_g=small["hgrn_norm_g"][l], ln1_g=small["ln1_g"][l],
                  ln1_b=small["ln1_b"][l], ln2_g=small["ln2_g"][l], ln2_b=small["ln2_b"][l])
        h, hb, s, late = _layer_fwd(h, hb, wfull[l], sp, l, rides[l] if rides else (), late_weights)
        if late is not None:
            wfull = [{**wfull[k], **late[k]} for k in range(DEPTH)]
        saved.append(s)
        sps.append(sp)
    dy, lpart = _loss_head(h, target)
    dys, coefs = [dy], [1.0]
    grads = [None] * DEPTH
    for l in reversed(range(DEPTH)):
        dys, coefs, grads[l] = _layer_bwd(dys, coefs, wfull[l], sps[l], saved[l], l)
    grad_x = dys[0]
    d_logits = lb_vjp(jnp.stack([grads[l]["lb"] for l in range(DEPTH)]))[0]
    return lpart[0, 0], grad_x, grads, d_logits


_BIG = [("w_in", "w_in", (D_MODEL, IN_TOTAL), 1), ("w_branch_a", "w_pa", (A_WIDTH, D_MODEL), 1),
        ("w_branch_b", "w_pb", (B_WIDTH, D_MODEL), 1), ("w_out", "w_out", (D_MODEL, D_MODEL), 0),
        ("w_ff_in", "w_ff_in", (D_MODEL, 2 * FFN_HIDDEN), 1), ("w_ff_out", "w_ff_out", (FFN_HIDDEN, D_MODEL), 0)]
_SMALL = [("b_fgate", A_HEADS), ("hgrn_lb_logits", B_WIDTH), ("hgrn_norm_g", HD), ("ln1_g", D_MODEL),
          ("ln1_b", D_MODEL), ("ln2_g", D_MODEL), ("ln2_b", D_MODEL)]
N_BIG = len(_BIG)
SMALL_ROWS = 80


def _by_chip(full, axis):
    if axis == 0:
        return full.reshape(N_CHIPS, full.shape[0] // N_CHIPS, full.shape[1])
    n = full.shape[1] // N_CHIPS
    return jnp.stack([full[:, q * n:(q + 1) * n] for q in range(N_CHIPS)])


def _from_chips(shards, axis):
    if axis == 0:
        return shards.reshape(N_CHIPS * shards.shape[1], shards.shape[2])
    return jnp.concatenate([shards[q] for q in range(N_CHIPS)], axis=1)


def _pack_small(per_name):
    flat = jnp.concatenate([per_name[name].reshape(-1) for name, _ in _SMALL])
    return jnp.pad(flat, (0, SMALL_ROWS * 128 - flat.shape[0])).reshape(SMALL_ROWS, 128)


def _unpack_small(slab):
    flat, out, r = slab.reshape(-1), {}, 0
    for name, n in _SMALL:
        out[name] = flat[r:r + DEPTH * n].reshape(DEPTH, n)
        r += DEPTH * n
    return out


_ANY = pl.BlockSpec(memory_space=pl.ANY)


def _place():
    return lax.axis_index("x"), lax.axis_index("y"), lax.axis_index("c")


def _other_chips(x, y):
    return [(1 - x, y), (x, 1 - y), (1 - x, 1 - y)]


def _chip_exchange(mine_of, out_refs, send_sems, recv_sems, local_sems):
    _chip_exchange_start(mine_of, out_refs, send_sems, recv_sems, local_sems)
    _chip_exchange_wait(mine_of, out_refs, send_sems, recv_sems, local_sems)


def _chip_exchange_copies(mine_of, out_refs, send_sems, recv_sems, local_sems):
    x, y, c = _place()
    q = 2 * x + y
    local = [pltpu.make_async_copy(mine_of(w, q), out_ref.at[q], local_sems.at[w]) for w, out_ref in enumerate(out_refs)]
    sends, recvs = [], []
    for k, (px, py) in enumerate(_other_chips(x, y)):
        for w, out_ref in enumerate(out_refs):
            sems = dict(send_sem=send_sems.at[3 * w + k], recv_sem=recv_sems.at[3 * w + k], device_id=(px, py, c),
                        device_id_type=MESH)
            sends.append(pltpu.make_async_remote_copy(src_ref=mine_of(w, 2 * px + py), dst_ref=out_ref.at[q], **sems))
            recvs.append(pltpu.make_async_remote_copy(src_ref=mine_of(w, q), dst_ref=out_ref.at[2 * px + py], **sems))
    return local, sends, recvs


def _chip_exchange_start(*args):
    local, sends, _ = _chip_exchange_copies(*args)
    for cp in local + sends:
        cp.start()


def _chip_exchange_wait(*args):
    local, sends, recvs = _chip_exchange_copies(*args)
    for cp in recvs:
        cp.wait_recv()
    for cp in sends:
        cp.wait_send()
    for cp in local:
        cp.wait()


def _sem_scratch(n):
    return [pltpu.SemaphoreType.DMA((3 * n,)), pltpu.SemaphoreType.DMA((3 * n,)), pltpu.SemaphoreType.DMA((n,))]


def _gather_scratch(n):
    return _sem_scratch(n) + [pltpu.SemaphoreType.DMA((n,)), pltpu.SemaphoreType.DMA((n,))]


def _gather_out_shapes(mine):
    return [jax.ShapeDtypeStruct((DEPTH, N_CHIPS) + m.shape[1:], m.dtype) for m in mine]


def _gather_start(in_refs, out_refs, sems):
    c = lax.axis_index("c")
    _chip_exchange_start(lambda w, q: in_refs[w].at[c], [o.at[c] for o in out_refs], *sems[:3])


def _gather_finish(in_refs, out_refs, sems):
    x, y, c = _place()
    _chip_exchange_wait(lambda w, q: in_refs[w].at[c], [o.at[c] for o in out_refs], *sems[:3])
    pair_send, pair_recv = sems[3:]
    sibling = (x, y, 1 - c)
    fwds = []
    for w, o in enumerate(out_refs):
        cp = pltpu.make_async_remote_copy(src_ref=o.at[c], dst_ref=o.at[c], send_sem=pair_send.at[w],
                                          recv_sem=pair_recv.at[w], device_id=sibling, device_id_type=MESH)
        cp.start()
        fwds.append(cp)
    for w, o in enumerate(out_refs):
        pltpu.make_async_remote_copy(src_ref=o.at[1 - c], dst_ref=o.at[1 - c], send_sem=pair_send.at[w],
                                     recv_sem=pair_recv.at[w], device_id=sibling, device_id_type=MESH).wait_recv()
    for cp in fwds:
        cp.wait_send()


def _gather_weights(mine):
    n = len(mine)

    def body(*refs):
        in_refs, out_refs, sems = refs[:n], refs[n:2 * n], refs[2 * n:]
        _gather_start(in_refs, out_refs, sems)
        _gather_finish(in_refs, out_refs, sems)

    return pl.pallas_call(
        body, name="gather_weights", in_specs=[_ANY] * n, out_specs=[_ANY] * n,
        out_shape=_gather_out_shapes(mine), scratch_shapes=_gather_scratch(n),
    )(*mine)


def _pair_exchange(gs):
    n = len(gs)

    def body(*refs):
        g_refs, a_refs, send_sems, recv_sems = refs[:n], refs[n:2 * n], refs[2 * n], refs[2 * n + 1]
        x, y, c = _place()
        cps = []
        for w in range(n):
            cp = pltpu.make_async_remote_copy(src_ref=g_refs[w].at[1 - c], dst_ref=a_refs[w], send_sem=send_sems.at[w],
                                              recv_sem=recv_sems.at[w], device_id=(x, y, 1 - c), device_id_type=MESH)
            cp.start()
            cps.append(cp)
        for cp in cps:
            cp.wait()

    return pl.pallas_call(
        body, name="grad_pair_exchange", in_specs=[_ANY] * n, out_specs=[_ANY] * n,
        out_shape=[jax.ShapeDtypeStruct(g.shape[1:], g.dtype) for g in gs],
        scratch_shapes=[pltpu.SemaphoreType.DMA((n,)), pltpu.SemaphoreType.DMA((n,))],
    )(*gs)


def _row_block(rows):
    return math.gcd(rows, 256)


def _pair_sum(g, a, layer, name):
    _, nq, rows, cols = g.shape
    tb = _row_block(rows)

    def body(l_ref, g_ref, a_ref, o_ref):
        o_ref[...] = (g_ref[...].astype(F32) + a_ref[...].astype(F32)).astype(BF16)

    return pl.pallas_call(
        body, name=name,
        grid_spec=pltpu.PrefetchScalarGridSpec(
            num_scalar_prefetch=1, grid=(nq, rows // tb),
            in_specs=[pl.BlockSpec((None, None, tb, cols), lambda q, i, l_ref: (l_ref[0], q, i, 0)),
                      pl.BlockSpec((None, tb, cols), lambda q, i, l_ref: (q, i, 0))],
            out_specs=pl.BlockSpec((None, tb, cols), lambda q, i, l_ref: (q, i, 0))),
        out_shape=jax.ShapeDtypeStruct((nq, rows, cols), BF16),
        compiler_params=_params("parallel", "parallel"),
    )(layer.reshape(1).astype(jnp.int32), g, a)


def _shard_exchange(ps):
    n = len(ps)

    def body(*refs):
        p_refs, b_refs = refs[:n], refs[n:2 * n]
        send_sems, recv_sems, local_sems = refs[2 * n:]
        _chip_exchange(lambda w, q: p_refs[w].at[q], b_refs, send_sems, recv_sems, local_sems)

    return pl.pallas_call(
        body, name="grad_shard_exchange", in_specs=[_ANY] * n, out_specs=[_ANY] * n,
        out_shape=[jax.ShapeDtypeStruct(p.shape, p.dtype) for p in ps],
        scratch_shapes=_sem_scratch(n),
    )(*ps)


def _sum4(b, name):
    _, rows, cols = b.shape
    tb = _row_block(rows)

    def body(b_ref, o_ref):
        o_ref[...] = ((b_ref[0].astype(F32) + b_ref[1].astype(F32)) + b_ref[2].astype(F32)) + b_ref[3].astype(F32)

    return pl.pallas_call(
        body, name=name, grid=(rows // tb,),
        in_specs=[pl.BlockSpec((N_CHIPS, tb, cols), lambda i: (0, i, 0))],
        out_specs=pl.BlockSpec((tb, cols), lambda i: (i, 0)),
        out_shape=jax.ShapeDtypeStruct((rows, cols), F32),
        compiler_params=_params("parallel"),
    )(b)


def _result_exchange(gcs):
    n = len(gcs)

    def body(*refs):
        g_refs, o_refs, send_sems, recv_sems = refs[:n], refs[n:2 * n], refs[2 * n], refs[2 * n + 1]
        x, y, c = _place()
        cps = []
        for w in range(n):
            cp = pltpu.make_async_remote_copy(src_ref=g_refs[w], dst_ref=o_refs[w], send_sem=send_sems.at[w],
                                              recv_sem=recv_sems.at[w], device_id=(x, y, 1 - c), device_id_type=MESH)
            cp.start()
            cps.append(cp)
        for cp in cps:
            cp.wait()

    return pl.pallas_call(
        body, name="grad_result_exchange", in_specs=[_ANY] * n, out_specs=[_ANY] * n,
        out_shape=[jax.ShapeDtypeStruct(g.shape, g.dtype) for g in gcs],
        scratch_shapes=[pltpu.SemaphoreType.DMA((n,)), pltpu.SemaphoreType.DMA((n,))],
    )(*gcs)


def _allreduce_small(v):
    def body(v_ref, o_ref, buf, send_sems, recv_sems):
        x, y, c = _place()
        me = 4 * x + 2 * y + c
        buf[me] = v_ref[...]
        peers = []
        for k in range(1, N_DEV):
            px = 1 - x if k & 4 else x
            py = 1 - y if k & 2 else y
            pc = 1 - c if k & 1 else c
            peers.append((px, py, pc))
        sends = []
        for k, peer in enumerate(peers):
            cp = pltpu.make_async_remote_copy(src_ref=v_ref, dst_ref=buf.at[me], send_sem=send_sems.at[k],
                                              recv_sem=recv_sems.at[k], device_id=peer, device_id_type=MESH)
            cp.start()
            sends.append(cp)
        for k, (px, py, pc) in enumerate(peers):
            pltpu.make_async_remote_copy(src_ref=v_ref, dst_ref=buf.at[4 * px + 2 * py + pc], send_sem=send_sems.at[k],
                                         recv_sem=recv_sems.at[k], device_id=(px, py, pc),
                                         device_id_type=MESH).wait_recv()
        for cp in sends:
            cp.wait_send()
        acc = buf[0]
        for i in range(1, N_DEV):
            acc = acc + buf[i]
        o_ref[...] = acc

    vm = pl.BlockSpec(memory_space=pltpu.VMEM)
    return pl.pallas_call(
        body, name="small_allreduce", in_specs=[vm], out_specs=vm,
        out_shape=jax.ShapeDtypeStruct(v.shape, F32),
        scratch_shapes=[pltpu.VMEM((N_DEV,) + v.shape, F32), pltpu.SemaphoreType.DMA((N_DEV - 1,)),
                        pltpu.SemaphoreType.DMA((N_DEV - 1,))],
    )(v)


def _adam_update(w, g, m, v):
    nm = ADAM_B1 * m + (1.0 - ADAM_B1) * g
    nv = ADAM_B2 * v + (1.0 - ADAM_B2) * (g * g)
    m_hat = nm / (1.0 - ADAM_B1 ** ADAM_STEP)
    v_hat = nv / (1.0 - ADAM_B2 ** ADAM_STEP)
    return -ADAM_LR * (m_hat / (jnp.sqrt(v_hat) + ADAM_EPS) + ADAM_WD * w), nm, nv


def _adamw_small(w, g, m, v, name):
    def body(w_ref, g_ref, m_ref, v_ref, d_ref, nm_ref, nv_ref):
        d_ref[...], nm_ref[...], nv_ref[...] = _adam_update(w_ref[...], g_ref[...], m_ref[...], v_ref[...])

    vm = pl.BlockSpec(memory_space=pltpu.VMEM)
    return pl.pallas_call(
        body, name=name, in_specs=[vm] * 4, out_specs=[vm] * 3,
        out_shape=[jax.ShapeDtypeStruct(w.shape, F32)] * 3,
    )(w, g, m, v)


def _adamw_big(w, m, v, g_own, g_other, layer, name):
    _, rows, cols = w.shape
    tb = _row_block(rows)

    def body(l_ref, w_ref, m_ref, v_ref, go_ref, gx_ref, g_ref, d_ref, nm_ref, nv_ref):
        gv = jnp.where(pl.program_id(0) == l_ref[0], go_ref[...], gx_ref[...])
        g_ref[...] = gv
        d_ref[...], nm_ref[...], nv_ref[...] = _adam_update(w_ref[...], gv, m_ref[...], v_ref[...])

    per_layer = pl.BlockSpec((None, tb, cols), lambda l, i, l_ref: (l, i, 0))
    shared = pl.BlockSpec((tb, cols), lambda l, i, l_ref: (i, 0))
    return pl.pallas_call(
        body, name=name,
        grid_spec=pltpu.PrefetchScalarGridSpec(
            num_scalar_prefetch=1, grid=(DEPTH, rows // tb),
            in_specs=[per_layer, per_layer, per_layer, shared, shared], out_specs=[per_layer] * 4),
        out_shape=[jax.ShapeDtypeStruct(w.shape, F32)] * 4,
        compiler_params=_params("parallel", "parallel"),
    )(layer.reshape(1).astype(jnp.int32), w, m, v, g_own, g_other)


def kernel(x, w_in, b_fgate, hgrn_lb_logits, hgrn_norm_g, w_branch_a, w_branch_b, w_out, ln1_g, ln1_b, w_ff_in, w_ff_out, ln2_g, ln2_b, loss_target, m_w_in, m_b_fgate, m_hgrn_lb_logits, m_hgrn_norm_g, m_w_branch_a, m_w_branch_b, m_w_out, m_ln1_g, m_ln1_b, m_w_ff_in, m_w_ff_out, m_ln2_g, m_ln2_b, v_w_in, v_b_fgate, v_hgrn_lb_logits, v_hgrn_norm_g, v_w_branch_a, v_w_branch_b, v_w_out, v_ln1_g, v_ln1_b, v_w_ff_in, v_w_ff_out, v_ln2_g, v_ln2_b):
    weights = dict(w_in=w_in, b_fgate=b_fgate, hgrn_lb_logits=hgrn_lb_logits, hgrn_norm_g=hgrn_norm_g,
                   w_branch_a=w_branch_a, w_branch_b=w_branch_b, w_out=w_out, ln1_g=ln1_g, ln1_b=ln1_b,
                   w_ff_in=w_ff_in, w_ff_out=w_ff_out, ln2_g=ln2_g, ln2_b=ln2_b)
    mom1 = dict(w_in=m_w_in, b_fgate=m_b_fgate, hgrn_lb_logits=m_hgrn_lb_logits, hgrn_norm_g=m_hgrn_norm_g,
                w_branch_a=m_w_branch_a, w_branch_b=m_w_branch_b, w_out=m_w_out, ln1_g=m_ln1_g, ln1_b=m_ln1_b,
                w_ff_in=m_w_ff_in, w_ff_out=m_w_ff_out, ln2_g=m_ln2_g, ln2_b=m_ln2_b)
    mom2 = dict(w_in=v_w_in, b_fgate=v_b_fgate, hgrn_lb_logits=v_hgrn_lb_logits, hgrn_norm_g=v_hgrn_norm_g,
                w_branch_a=v_w_branch_a, w_branch_b=v_w_branch_b, w_out=v_w_out, ln1_g=v_ln1_g, ln1_b=v_ln1_b,
                w_ff_in=v_w_ff_in, w_ff_out=v_w_ff_out, ln2_g=v_ln2_g, ln2_b=v_ln2_b)
    core = lax.axis_index("c")

    def halves(shard):
        return shard.astype(BF16).reshape(2, shard.shape[0] // 2, shard.shape[1])

    def from_halves(gathered, axis):
        return _from_chips(jnp.concatenate([gathered[0], gathered[1]], axis=1), axis)

    mid, ffn = _BIG[1:4], _BIG[4:]

    def late_weights(l, gathered):
        if l == 1:
            return [{}, {key: from_halves(gathered[w], axis) for w, (_, key, _, axis) in enumerate(ffn)}]
        upd = [{key: _from_chips(gathered[1 + w][k], axis) for w, (_, key, _, axis) in enumerate(mid)}
               for k in range(DEPTH)]
        upd[1]["w_in"] = from_halves(gathered[0], 1)
        upd[0].update({key: from_halves(gathered[4 + w], axis) for w, (_, key, _, axis) in enumerate(ffn)})
        return upd

    wfull = [{"w_in": from_halves(_gather_weights([halves(weights["w_in"][0])])[0], 1)}, {}]
    rides = [[halves(weights["w_in"][1])] + [weights[name].astype(BF16) for name, _, _, _ in mid]
             + [halves(weights[name][0]) for name, _, _, _ in ffn],
             [halves(weights[name][1]) for name, _, _, _ in ffn]]
    small = {name: weights[name] for name, _ in _SMALL}

    loss_part, grad_x, grads, d_logits = _local_step(x[0], loss_target[0], wfull, small, rides, late_weights)

    g_all = [jnp.stack([_by_chip(grads[l][key], axis) for l in range(DEPTH)]).astype(BF16)
             for _, key, _, axis in _BIG]
    received = _pair_exchange(g_all)
    pair = [_pair_sum(g_all[w], received[w], core, f"grad_pair_sum_{w}") for w in range(N_BIG)]
    by_chip = _shard_exchange(pair)
    g_layer = [_sum4(by_chip[w], f"grad_chip_sum_{w}") for w in range(N_BIG)]
    g_other = _result_exchange(g_layer)
    out_g, out_d, out_m, out_v = {}, {}, {}, {}
    for w, (name, _, _, _) in enumerate(_BIG):
        out_g[name], out_d[name], out_m[name], out_v[name] = _adamw_big(
            weights[name], mom1[name], mom2[name], g_layer[w], g_other[w], core, f"adamw_{name}")

    small_grads = {name: jnp.stack([grads[l][key] for l in range(DEPTH)])
                   for name, key in [("b_fgate", "b_fgate"), ("hgrn_norm_g", "norm_g"), ("ln1_g", "ln1_g"),
                                     ("ln1_b", "ln1_b"), ("ln2_g", "ln2_g"), ("ln2_b", "ln2_b")]}
    small_grads["hgrn_lb_logits"] = d_logits
    gs = _allreduce_small(_pack_small(small_grads))
    ds, ms, vs = _adamw_small(_pack_small(small), gs, _pack_small({n: mom1[n] for n, _ in _SMALL}),
                              _pack_small({n: mom2[n] for n, _ in _SMALL}), "adamw_small")
    for tree, slab in ((out_g, gs), (out_d, ds), (out_m, ms), (out_v, vs)):
        tree.update(_unpack_small(slab))

    loss = lax.psum(loss_part, ("x", "y", "c"))
    order = ["w_in", "b_fgate", "hgrn_lb_logits", "hgrn_norm_g", "w_branch_a", "w_branch_b", "w_out", "ln1_g", "ln1_b",
             "w_ff_in", "w_ff_out", "ln2_g", "ln2_b"]
    return (loss, grad_x[None], *[out_g[n] for n in order], *[out_d[n] for n in order],
            *[out_m[n] for n in order], *[out_v[n] for n in order])
```

```python
import math

import jax
import jax.numpy as jnp
from jax import lax
from jax.experimental import pallas as pl
from jax.experimental.pallas import tpu as pltpu

F32 = jnp.float32
BF16 = jnp.bfloat16

D_MODEL = 1024
DEPTH = 2
A_HEADS = 8
A_WIDTH = 512
B_WIDTH = 512
B_HEADS = 4
HD = 128
CHUNK = 64
SUB = 16
FFN_HIDDEN = 2816
IN_TOTAL = 5640
ALPHA = (2 * DEPTH) ** 0.25
LN_EPS = 1e-5
RMS_EPS = 1e-6
ADAM_LR = 0.001
ADAM_B1 = 0.9
ADAM_B2 = 0.999
ADAM_EPS = 1e-08
ADAM_WD = 0.01
ADAM_STEP = 10
EXP_CLAMP = 60.0

VMEM_LIMIT_BYTES = 56 * 1024 * 1024
MM_ROWS = 1024
DW_ROWS = 2048
N_CHIPS = 4
N_DEV = 8
MESH = pl.DeviceIdType.MESH

_DN = {
    "nn": (((1,), (0,)), ((), ())),
    "nt": (((1,), (1,)), ((), ())),
    "tn": (((0,), (0,)), ((), ())),
}


def _dot(a, b, mode="nn"):
    return lax.dot_general(a.astype(BF16), b.astype(BF16), _DN[mode], preferred_element_type=F32)


def _pieces(x):
    h = x.astype(BF16)
    r = x - h.astype(F32)
    m = r.astype(BF16)
    return h, m, (r - m.astype(F32)).astype(BF16)


def _dot_hi(a, b, mode="nn", exact="a"):
    if exact == "a":
        h, m, l = _pieces(b)
        return (_dot(a, l, mode) + _dot(a, m, mode)) + _dot(a, h, mode)
    h, m, l = _pieces(a)
    return (_dot(l, b, mode) + _dot(m, b, mode)) + _dot(h, b, mode)


def _hdot(a, b, mode="nn"):
    bh, bl, _ = _pieces(b)
    return _dot(a, bl, mode) + _dot(a, bh, mode)


def _params(*sem):
    return pltpu.CompilerParams(dimension_semantics=sem, vmem_limit_bytes=VMEM_LIMIT_BYTES)


def _sigmoid(x):
    return 1.0 / (1.0 + jnp.exp(-x))


def _matmul(a, b, mode, out_dtype, tm, tn, tk, name):
    if mode == "nn":
        (m, k), (k2, n) = a.shape, b.shape
    elif mode == "nt":
        (m, k), (n, k2) = a.shape, b.shape
    else:
        (k, m), (k2, n) = a.shape, b.shape
    assert k == k2, (a.shape, b.shape, mode)
    tm, tn, tk = min(tm, m), min(tn, n), min(tk, k)
    assert m % tm == 0 and n % tn == 0 and k % tk == 0, (a.shape, b.shape, tm, tn, tk)
    nk = k // tk
    if mode == "tn":
        a_spec = pl.BlockSpec((tk, tm), lambda j, i, kk: (kk, i))
    else:
        a_spec = pl.BlockSpec((tm, tk), lambda j, i, kk: (i, kk))
    if mode == "nt":
        b_spec = pl.BlockSpec((tn, tk), lambda j, i, kk: (j, kk))
    else:
        b_spec = pl.BlockSpec((tk, tn), lambda j, i, kk: (kk, j))
    use_acc = nk > 1 and out_dtype != F32

    def body(a_ref, b_ref, o_ref, *scratch):
        p = _dot(a_ref[...], b_ref[...], mode)
        if nk == 1:
            o_ref[...] = p.astype(out_dtype)
            return
        acc_ref = scratch[0] if use_acc else o_ref
        kk = pl.program_id(2)

        @pl.when(kk == 0)
        def _():
            acc_ref[...] = p

        @pl.when(kk > 0)
        def _():
            acc_ref[...] += p

        if use_acc:
            @pl.when(kk == nk - 1)
            def _():
                o_ref[...] = acc_ref[...].astype(out_dtype)

    return pl.pallas_call(
        body,
        name=name,
        grid=(n // tn, m // tm, nk),
        in_specs=[a_spec, b_spec],
        out_specs=pl.BlockSpec((tm, tn), lambda j, i, kk: (i, j)),
        out_shape=jax.ShapeDtypeStruct((m, n), out_dtype),
        scratch_shapes=[pltpu.VMEM((tm, tn), F32)] if use_acc else [],
        compiler_params=_params("parallel", "parallel", "arbitrary"),
    )(a, b)


def _matmul_nt_sum(a_list, b_list, name, plus=None, tm=512):
    m = a_list[0].shape[0]
    n = b_list[0].shape[0]
    tm = min(tm, m)
    np_ = len(a_list)
    extra = [] if plus is None else [plus[1]]

    def body(*refs):
        o_ref = refs[-1]
        acc = _dot(refs[0][...], refs[np_][...], "nt")
        for p in range(1, np_):
            acc = acc + _dot(refs[p][...], refs[np_ + p][...], "nt")
        if plus is not None:
            acc = acc + plus[0] * refs[2 * np_][...]
        o_ref[...] = acc

    row = pl.BlockSpec((tm, n), lambda i: (i, 0))
    return pl.pallas_call(
        body,
        name=name,
        grid=(m // tm,),
        in_specs=[pl.BlockSpec((tm, a.shape[1]), lambda i: (i, 0)) for a in a_list]
        + [pl.BlockSpec(b.shape, lambda i: (0, 0)) for b in b_list] + [row] * len(extra),
        out_specs=row,
        out_shape=jax.ShapeDtypeStruct((m, n), F32),
        compiler_params=_params("parallel"),
    )(*a_list, *b_list, *extra)


def _mm_res_ln(a, w, res, g, b, name, tm=512):
    t, k = a.shape
    d = w.shape[1]
    tm = min(tm, t)

    def body(a_ref, w_ref, r_ref, g_ref, b_ref, y_ref, yb_ref, xh_ref, rs_ref):
        z = ALPHA * r_ref[...] + _dot(a_ref[...], w_ref[...])
        mu = jnp.mean(z, axis=-1, keepdims=True)
        zc = z - mu
        var = jnp.mean(zc * zc, axis=-1, keepdims=True)
        rstd = lax.rsqrt(var + LN_EPS)
        xh = zc * rstd
        y = xh * g_ref[...] + b_ref[...]
        y_ref[...] = y
        yb_ref[...] = y.astype(BF16)
        xh_ref[...] = xh
        rs_ref[...] = rstd

    row = lambda i: (i, 0)
    fix = lambda i: (0, 0)
    return pl.pallas_call(
        body,
        name=name,
        grid=(t // tm,),
        in_specs=[pl.BlockSpec((tm, k), row), pl.BlockSpec((k, d), fix), pl.BlockSpec((tm, d), row),
                  pl.BlockSpec((1, d), fix), pl.BlockSpec((1, d), fix)],
        out_specs=[pl.BlockSpec((tm, d), row), pl.BlockSpec((tm, d), row), pl.BlockSpec((tm, d), row),
                   pl.BlockSpec((tm, 1), row)],
        out_shape=[jax.ShapeDtypeStruct((t, d), F32), jax.ShapeDtypeStruct((t, d), BF16),
                   jax.ShapeDtypeStruct((t, d), F32), jax.ShapeDtypeStruct((t, 1), F32)],
        compiler_params=_params("parallel"),
    )(a, w, res, g.reshape(1, d), b.reshape(1, d))


def _ln_bwd(dys, coefs, xhat, rstd, g, name, tm=512):
    t, d = xhat.shape
    tm = min(tm, t)
    n_in = len(dys)

    def body(*refs):
        dy_refs = refs[:n_in]
        xh_ref, rs_ref, g_ref, dz_ref, dzb_ref, dg_ref, db_ref = refs[n_in:]
        dy = coefs[0] * dy_refs[0][...].astype(F32)
        for c, r in zip(coefs[1:], dy_refs[1:]):
            dy = dy + c * r[...].astype(F32)
        xh = xh_ref[...]
        dxh = dy * g_ref[...]
        m1 = jnp.mean(dxh, axis=-1, keepdims=True)
        m2 = jnp.mean(dxh * xh, axis=-1, keepdims=True)
        dz = rs_ref[...] * (dxh - m1 - xh * m2)
        dz_ref[...] = dz
        dzb_ref[...] = dz.astype(BF16)
        pg = jnp.sum(dy * xh, axis=0, keepdims=True)
        pb = jnp.sum(dy, axis=0, keepdims=True)

        @pl.when(pl.program_id(0) == 0)
        def _():
            dg_ref[...] = pg
            db_ref[...] = pb

        @pl.when(pl.program_id(0) > 0)
        def _():
            dg_ref[...] += pg
            db_ref[...] += pb

    row = lambda i: (i, 0)
    fix = lambda i: (0, 0)
    return pl.pallas_call(
        body,
        name=name,
        grid=(t // tm,),
        in_specs=[pl.BlockSpec((tm, d), row)] * n_in
        + [pl.BlockSpec((tm, d), row), pl.BlockSpec((tm, 1), row), pl.BlockSpec((1, d), fix)],
        out_specs=[pl.BlockSpec((tm, d), row), pl.BlockSpec((tm, d), row), pl.BlockSpec((1, d), fix),
                   pl.BlockSpec((1, d), fix)],
        out_shape=[jax.ShapeDtypeStruct((t, d), F32), jax.ShapeDtypeStruct((t, d), BF16),
                   jax.ShapeDtypeStruct((1, d), F32), jax.ShapeDtypeStruct((1, d), F32)],
        compiler_params=_params("arbitrary"),
    )(*dys, xhat, rstd, g.reshape(1, d))


def _loss_head(y, target, name="loss_head", tm=512):
    t, d = y.shape
    tm = min(tm, t)

    def body(y_ref, t_ref, dy_ref, l_ref):
        e = y_ref[...] - t_ref[...]
        dy_ref[...] = e * (1.0 / d)
        part = jnp.full((8, 128), 0.5 / d, F32) * jnp.sum(e * e)

        @pl.when(pl.program_id(0) == 0)
        def _():
            l_ref[...] = part

        @pl.when(pl.program_id(0) > 0)
        def _():
            l_ref[...] += part

    row = lambda i: (i, 0)
    return pl.pallas_call(
        body,
        name=name,
        grid=(t // tm,),
        in_specs=[pl.BlockSpec((tm, d), row), pl.BlockSpec((tm, d), row)],
        out_specs=[pl.BlockSpec((tm, d), row), pl.BlockSpec((8, 128), lambda i: (0, 0))],
        out_shape=[jax.ShapeDtypeStruct((t, d), F32), jax.ShapeDtypeStruct((8, 128), F32)],
        compiler_params=_params("arbitrary"),
    )(y, target)


FFN_COLS = FFN_HIDDEN // 2


def _ffn_in_swiglu(xb, wu, wg, name, tm=MM_ROWS):
    t, d = xb.shape
    tm = min(tm, t)

    def body(x_ref, wu_ref, wg_ref, a_ref, u_ref, g_ref):
        x = x_ref[...]
        u = _dot(x, wu_ref[...])
        g = _dot(x, wg_ref[...])
        u_ref[...] = u.astype(BF16)
        g_ref[...] = g.astype(BF16)
        a_ref[...] = (g * _sigmoid(g) * u).astype(BF16)

    wspec = pl.BlockSpec((d, FFN_COLS), lambda j, i: (0, j))
    out = pl.BlockSpec((tm, FFN_COLS), lambda j, i: (i, j))
    return pl.pallas_call(
        body,
        name=name,
        grid=(FFN_HIDDEN // FFN_COLS, t // tm),
        in_specs=[pl.BlockSpec((tm, d), lambda j, i: (i, 0)), wspec, wspec],
        out_specs=[out, out, out],
        out_shape=[jax.ShapeDtypeStruct((t, FFN_HIDDEN), BF16)] * 3,
        compiler_params=_params("parallel", "parallel"),
    )(xb, wu, wg)


def _ffn_out_dx_swiglu(dzb, w_ff_out, u, g, name, tm=MM_ROWS):
    t, d = dzb.shape
    tm = min(tm, t)

    def body(dz_ref, w_ref, u_ref, g_ref, du_ref, dg_ref):
        da = _dot(dz_ref[...], w_ref[...], "nt")
        gv = g_ref[...].astype(F32)
        sg = _sigmoid(gv)
        du_ref[...] = (da * gv * sg).astype(BF16)
        dg_ref[...] = (da * u_ref[...].astype(F32) * (sg * (1.0 + gv * (1.0 - sg)))).astype(BF16)

    blk = pl.BlockSpec((tm, FFN_COLS), lambda j, i: (i, j))
    return pl.pallas_call(
        body,
        name=name,
        grid=(FFN_HIDDEN // FFN_COLS, t // tm),
        in_specs=[pl.BlockSpec((tm, d), lambda j, i: (i, 0)), pl.BlockSpec((FFN_COLS, d), lambda j, i: (j, 0)), blk, blk],
        out_specs=[blk, blk],
        out_shape=[jax.ShapeDtypeStruct((t, FFN_HIDDEN), BF16)] * 2,
        compiler_params=_params("parallel", "parallel"),
    )(dzb, w_ff_out, u, g)


def _merge_fwd(ya, yb, wpa, wpb, gates, name, tm=512):
    t = ya.shape[0]
    tm = min(tm, t)

    def body(ya_ref, yb_ref, wa_ref, wb_ref, ga_ref, gb_ref, o_ref):
        pa = _dot(ya_ref[...], wa_ref[...])
        pb = _dot(yb_ref[...], wb_ref[...])
        o_ref[...] = (_sigmoid(ga_ref[...].astype(F32)) * pa + _sigmoid(gb_ref[...].astype(F32)) * pb).astype(BF16)

    row = lambda i: (i, 0)
    fix = lambda i: (0, 0)
    return pl.pallas_call(
        body,
        name=name,
        grid=(t // tm,),
        in_specs=[pl.BlockSpec((tm, A_WIDTH), row), pl.BlockSpec((tm, B_WIDTH), row),
                  pl.BlockSpec((A_WIDTH, D_MODEL), fix), pl.BlockSpec((B_WIDTH, D_MODEL), fix),
                  pl.BlockSpec((tm, D_MODEL), lambda i: (i, 0)), pl.BlockSpec((tm, D_MODEL), lambda i: (i, 1))],
        out_specs=pl.BlockSpec((tm, D_MODEL), row),
        out_shape=jax.ShapeDtypeStruct((t, D_MODEL), BF16),
        compiler_params=_params("parallel"),
    )(ya, yb, wpa, wpb, gates, gates)


def _merge_bwd(dzb, w_out, ya, yb, wpa, wpb, gates, name, tm=512):
    t = ya.shape[0]
    tm = min(tm, t)

    def body(dz_ref, wo_ref, ya_ref, yb_ref, wa_ref, wb_ref, ga_ref, gb_ref, dg_ref, dpa_ref, dpb_ref, dya_ref,
             dyb_ref):
        dm_v = _dot(dz_ref[...], wo_ref[...], "nt")
        pa = _dot(ya_ref[...], wa_ref[...])
        pb = _dot(yb_ref[...], wb_ref[...])
        sa = _sigmoid(ga_ref[...].astype(F32))
        sb = _sigmoid(gb_ref[...].astype(F32))
        dg_ref[:, :D_MODEL] = (dm_v * pa * sa * (1.0 - sa)).astype(BF16)
        dg_ref[:, D_MODEL:] = (dm_v * pb * sb * (1.0 - sb)).astype(BF16)
        dpa = (dm_v * sa).astype(BF16)
        dpb = (dm_v * sb).astype(BF16)
        dpa_ref[...] = dpa
        dpb_ref[...] = dpb
        dya_ref[...] = _dot(dpa, wa_ref[...], "nt").astype(BF16)
        dyb_ref[...] = _dot(dpb, wb_ref[...], "nt")

    row = lambda i: (i, 0)
    fix = lambda i: (0, 0)
    return pl.pallas_call(
        body,
        name=name,
        grid=(t // tm,),
        in_specs=[pl.BlockSpec((tm, D_MODEL), row), pl.BlockSpec((D_MODEL, D_MODEL), fix),
                  pl.BlockSpec((tm, A_WIDTH), row), pl.BlockSpec((tm, B_WIDTH), row),
                  pl.BlockSpec((A_WIDTH, D_MODEL), fix), pl.BlockSpec((B_WIDTH, D_MODEL), fix),
                  pl.BlockSpec((tm, D_MODEL), lambda i: (i, 0)), pl.BlockSpec((tm, D_MODEL), lambda i: (i, 1))],
        out_specs=[pl.BlockSpec((tm, 2 * D_MODEL), row), pl.BlockSpec((tm, D_MODEL), row),
                   pl.BlockSpec((tm, D_MODEL), row), pl.BlockSpec((tm, A_WIDTH), row), pl.BlockSpec((tm, B_WIDTH), row)],
        out_shape=[jax.ShapeDtypeStruct((t, 2 * D_MODEL), BF16), jax.ShapeDtypeStruct((t, D_MODEL), BF16),
                   jax.ShapeDtypeStruct((t, D_MODEL), BF16), jax.ShapeDtypeStruct((t, A_WIDTH), BF16),
                   jax.ShapeDtypeStruct((t, B_WIDTH), F32)],
        compiler_params=_params("parallel"),
    )(dzb, w_out, ya, yb, wpa, wpb, gates, gates)


FA_BLOCK = 4 * B_WIDTH // 128


def _tri(n, lower):
    r = lax.broadcasted_iota(jnp.int32, (n, n), 0)
    c = lax.broadcasted_iota(jnp.int32, (n, n), 1)
    return jnp.where((r >= c) if lower else (r <= c), 1.0, 0.0).astype(F32)


def _head_spread(expand):
    shape = (128, A_WIDTH) if expand else (A_WIDTH, 128)
    r = lax.broadcasted_iota(jnp.int32, shape, 0)
    c = lax.broadcasted_iota(jnp.int32, shape, 1)
    hit = ((c >= 64 * r) & (c < 64 * r + 64)) if expand else (r == 64 * c)
    return jnp.where(hit, 1.0, 0.0).astype(F32)


def _fox_gate_fwd(rest, bf, name, tb=512):
    t = rest.shape[0]
    tb = min(tb, t)

    def body(fa_ref, bf_ref, f_ref, fc_ref, carry):
        @pl.when(pl.program_id(0) == 0)
        def _():
            carry[...] = jnp.zeros_like(carry)

        z = fa_ref[...] + bf_ref[...]
        logf = jnp.minimum(z, 0.0) - jnp.log(1.0 + jnp.exp(-jnp.abs(z)))
        f = _dot_hi(_tri(tb, True), logf) + carry[...]
        f_ref[...] = f
        fc_ref[...] = _dot_hi(f, _head_spread(True), exact="b")
        carry[...] = f[tb - 1:tb, :]

    return pl.pallas_call(
        body,
        name=name,
        grid=(t // tb,),
        in_specs=[pl.BlockSpec((tb, 128), lambda i: (i, FA_BLOCK)), pl.BlockSpec((1, 128), lambda i: (0, 0))],
        out_specs=[pl.BlockSpec((tb, 128), lambda i: (i, 0)), pl.BlockSpec((tb, A_WIDTH), lambda i: (i, 0))],
        out_shape=[jax.ShapeDtypeStruct((t, 128), F32), jax.ShapeDtypeStruct((t, A_WIDTH), F32)],
        scratch_shapes=[pltpu.VMEM((1, 128), F32)],
        compiler_params=_params("arbitrary"),
    )(rest, bf)


def _fox_gate_bwd(rsum, csum, rest, bf, name, tb=512):
    t = rest.shape[0]
    tb = min(tb, t)
    nb = t // tb

    def body(rs_ref, cs_ref, fa_ref, bf_ref, dfa_ref, dbf_ref, carry):
        @pl.when(pl.program_id(0) == 0)
        def _():
            carry[...] = jnp.zeros_like(carry)

        d_f = _dot_hi(rs_ref[...] - cs_ref[...], _head_spread(False), exact="b")
        dlogf = _dot_hi(_tri(tb, False), d_f) + carry[...]
        carry[...] = dlogf[0:1, :]
        z = fa_ref[...] + bf_ref[...]
        dz = dlogf * _sigmoid(-z)
        dfa_ref[...] = dz.astype(BF16)
        part = jnp.sum(dz, axis=0, keepdims=True)

        @pl.when(pl.program_id(0) == 0)
        def _():
            dbf_ref[...] = part

        @pl.when(pl.program_id(0) > 0)
        def _():
            dbf_ref[...] += part

    return pl.pallas_call(
        body,
        name=name,
        grid=(nb,),
        in_specs=[pl.BlockSpec((tb, A_WIDTH), lambda i: (nb - 1 - i, 0)),
                  pl.BlockSpec((tb, A_WIDTH), lambda i: (nb - 1 - i, 0)),
                  pl.BlockSpec((tb, 128), lambda i: (nb - 1 - i, FA_BLOCK)),
                  pl.BlockSpec((1, 128), lambda i: (0, 0))],
        out_specs=[pl.BlockSpec((tb, 128), lambda i: (nb - 1 - i, 0)), pl.BlockSpec((1, 128), lambda i: (0, 0))],
        out_shape=[jax.ShapeDtypeStruct((t, 128), BF16), jax.ShapeDtypeStruct((1, 128), F32)],
        scratch_shapes=[pltpu.VMEM((1, 128), F32)],
        compiler_params=_params("arbitrary"),
    )(rsum, csum, rest, bf)


ATT_BLOCK = 512
FWD_Q_BLOCKS = 2
BWD_K_BLOCKS = 2


def _head_mask(shape, j):
    lane = lax.broadcasted_iota(jnp.int32, shape, 1)
    return (lane < 64) if j == 0 else (lane >= 64)


def _aug_lanes(tb, j):
    lane = lax.broadcasted_iota(jnp.int32, (tb, 128), 1)
    own = (lane < 64) if j == 0 else (lane >= 64)
    return own, lane - 64 * (1 - j)


def _aug_query(own, li, q, pieces):
    h, m, l = pieces
    one, zero = jnp.ones_like(h), jnp.zeros_like(h)
    spare = jnp.where(li == 0, h, jnp.where(li == 1, m, jnp.where(li == 2, l, jnp.where(li < 6, one, zero))))
    return jnp.where(own, q, spare)


def _fox_prep_fwd(qkv, fcol, name, tb=2048):
    t = qkv.shape[0]
    tb = min(tb, t)

    def body(q_ref, k_ref, v_ref, fc_ref, qa_ref, ka_ref, va_ref, qn_ref, kn_ref):
        pieces = _pieces(pltpu.roll(fc_ref[...], 64, 1))
        h, m, l = pieces
        q, k, v = q_ref[...], k_ref[...], v_ref[...]
        first = _head_mask((tb, 128), 0)
        for nrm_ref, x in ((qn_ref, q.astype(F32)), (kn_ref, k.astype(F32))):
            n0 = jnp.max(jnp.sum(jnp.where(first, x * x, 0.0), axis=1, keepdims=True))
            n1 = jnp.max(jnp.sum(jnp.where(first, 0.0, x * x), axis=1, keepdims=True))
            nrm_ref[...] = jnp.where(_head_mask((8, 128), 0), n0, n1)
        one, zero = jnp.ones_like(h), jnp.zeros_like(h)
        for j in (0, 1):
            own, li = _aug_lanes(tb, j)
            cols = slice(128 * j, 128 * (j + 1))
            qa_ref[:, cols] = _aug_query(own, li, q * 0.125, pieces)
            ks = jnp.where(li < 3, one, jnp.where(li == 3, -h, jnp.where(li == 4, -m, jnp.where(li == 5, -l, zero))))
            ka_ref[:, cols] = jnp.where(own, k, ks)
            va_ref[:, cols] = jnp.where(own, v, one)

    blk = pl.BlockSpec((tb, 256), lambda i, h: (i, h))
    nrm = pl.BlockSpec((None, None, 8, 128), lambda i, h: (i, h, 0, 0))
    return pl.pallas_call(
        body, name=name, grid=(t // tb, 4),
        in_specs=[pl.BlockSpec((tb, 128), lambda i, h: (i, h)), pl.BlockSpec((tb, 128), lambda i, h: (i, 4 + h)),
                  pl.BlockSpec((tb, 128), lambda i, h: (i, 8 + h)), pl.BlockSpec((tb, 128), lambda i, h: (i, h))],
        out_specs=[blk, blk, blk, nrm, nrm],
        out_shape=[jax.ShapeDtypeStruct((t, 2 * A_WIDTH), BF16)] * 3
        + [jax.ShapeDtypeStruct((t // tb, 4, 8, 128), F32)] * 2,
        compiler_params=_params("parallel", "parallel"),
    )(qkv, qkv, qkv, fcol)


def _fox_prep_bwd(qkv, fcol, lse, do, o, name, tb=2048):
    t = qkv.shape[0]
    tb = min(tb, t)

    def body(q_ref, fc_ref, lse_ref, do_ref, o_ref, qb_ref, dob_ref):
        pieces = _pieces(pltpu.roll(fc_ref[...] - lse_ref[...], 64, 1))
        q = q_ref[...] * 0.125
        do_v = do_ref[...]
        prod = do_v.astype(F32) * o_ref[...].astype(F32)
        for j in (0, 1):
            own, li = _aug_lanes(tb, j)
            cols = slice(128 * j, 128 * (j + 1))
            qb_ref[:, cols] = _aug_query(own, li, q, pieces)
            delta = jnp.sum(jnp.where(own, prod, 0.0), axis=1, keepdims=True)
            h, m, l = _pieces(jnp.broadcast_to(delta, (tb, 128)))
            ds = jnp.where(li == 0, -h, jnp.where(li == 1, -m, jnp.where(li == 2, -l, jnp.zeros_like(h))))
            dob_ref[:, cols] = jnp.where(own, do_v, ds)

    pair = pl.BlockSpec((tb, 128), lambda i, h: (i, h))
    blk = pl.BlockSpec((tb, 256), lambda i, h: (i, h))
    return pl.pallas_call(
        body, name=name, grid=(t // tb, 4),
        in_specs=[pair, pair, pair, pair, pair],
        out_specs=[blk, blk],
        out_shape=[jax.ShapeDtypeStruct((t, 2 * A_WIDTH), BF16)] * 2,
        compiler_params=_params("parallel", "parallel"),
    )(qkv, fcol, lse, do, o)


def _tile_mask(n, transposed):
    r = lax.broadcasted_iota(jnp.int32, (n, n), 0)
    c = lax.broadcasted_iota(jnp.int32, (n, n), 1)
    return (c >= r) if transposed else (r >= c)


UNDERFLOW = -110.0


def _fox_block_ranges(qn, kn, fcum):
    t = fcum.shape[0]
    blk = min(ATT_BLOCK, t)
    nb = t // blk
    q2 = jnp.max(qn[:, :, 0, ::64].reshape(-1, A_HEADS), axis=0)
    k2 = jnp.max(kn[:, :, 0, ::64].reshape(-1, A_HEADS), axis=0)
    bound = 2.0 * jnp.sqrt(q2 * k2) * 0.125
    f = fcum[:, :A_HEADS]
    first = f[0::blk].T
    last = f[blk - 1::blk].T
    dead = (bound[:, None, None] + first[:, :, None] - last[:, None, :]) < UNDERFLOW
    qi = jnp.arange(nb)[None, :, None]
    kj = jnp.arange(nb)[None, None, :]
    dead = dead & (kj < qi)
    kstart = jnp.sum(dead, axis=2).astype(jnp.int32)
    qend = (kj[0] + jnp.sum((~dead) & (qi > kj), axis=1)).astype(jnp.int32)
    return kstart.reshape(-1), qend.reshape(-1)


def _fox_fwd(qa, ka, va, kstart, name):
    t = qa.shape[0]
    bk = min(ATT_BLOCK, t)
    nk = t // bk
    qf = FWD_Q_BLOCKS if t % (FWD_Q_BLOCKS * bk) == 0 else 1
    bq = qf * bk
    nq = t // bq

    def body(ks_ref, q_ref, k_ref, v_ref, o_ref, lse_ref):
        i = pl.program_id(1)
        hp = pl.program_id(0)
        k0 = [ks_ref[(2 * hp + j) * nk + qf * i] for j in (0, 1)]
        both0 = jnp.maximum(k0[0], k0[1])

        def head(j, kb, m, acc, diag):
            rows = pl.ds(pl.multiple_of(kb * bk, bk), bk)
            cols = slice(128 * j, 128 * (j + 1))
            s = _dot(q_ref[:, cols], k_ref[rows, cols], "nt")
            if diag is not None:
                r = lax.broadcasted_iota(jnp.int32, (bq, bk), 0)
                c = lax.broadcasted_iota(jnp.int32, (bq, bk), 1)
                s = jnp.where(r - c >= diag, s, -jnp.inf)
            m_new = jnp.maximum(m, jnp.max(s, axis=1, keepdims=True))
            return m_new, jnp.exp(m - m_new) * acc + _dot(jnp.exp(s - m_new), v_ref[rows, cols])

        def pair(kb, carry, diag):
            return head(0, kb, carry[0], carry[1], diag) + head(1, kb, carry[2], carry[3], diag)

        init = (jnp.full((bq, 1), -jnp.inf, F32), jnp.zeros((bq, 128), F32))
        alone = [lax.fori_loop(k0[j], both0, lambda kb, c, j=j: head(j, kb, c[0], c[1], None), init) for j in (0, 1)]
        carry = lax.fori_loop(both0, qf * i, lambda kb, c: pair(kb, c, None), alone[0] + alone[1])
        for d in range(qf):
            carry = pair(qf * i + d, carry, d * bk)
        outs = []
        for j in (0, 1):
            m, acc = carry[2 * j], carry[2 * j + 1]
            spare = 64 * (1 - j)
            l = acc[:, spare:spare + 1]
            outs.append((acc / l, m + jnp.log(l)))
        msk = _head_mask((bq, 128), 0)
        o_ref[...] = jnp.where(msk, outs[0][0], outs[1][0]).astype(BF16)
        lse_ref[...] = jnp.where(msk, outs[0][1], outs[1][1])

    res = pl.BlockSpec((t, 256), lambda h, i, tbl: (0, h))
    out = pl.BlockSpec((bq, 128), lambda h, i, tbl: (i, h))
    return pl.pallas_call(
        body,
        name=name,
        grid_spec=pltpu.PrefetchScalarGridSpec(
            num_scalar_prefetch=1, grid=(4, nq),
            in_specs=[pl.BlockSpec((bq, 256), lambda h, i, tbl: (i, h)), res, res],
            out_specs=[out, out]),
        out_shape=[jax.ShapeDtypeStruct((t, A_WIDTH), BF16), jax.ShapeDtypeStruct((t, A_WIDTH), F32)],
        compiler_params=_params("parallel", "parallel"),
    )(kstart, qa, ka, va)


def _fox_bwd(qb, ka, va, dob, qend, name):
    t = qb.shape[0]
    bq = min(ATT_BLOCK, t)
    nq = t // bq
    kf = BWD_K_BLOCKS if t % (BWD_K_BLOCKS * bq) == 0 else 1
    bk = kf * bq
    nk = t // bk

    def body(qe_ref, k_ref, v_ref, q_hbm, do_hbm, dk_ref, dv_ref, cs_ref, dq_hbm, rs_hbm, q_scr, do_scr, dq_scr,
             sems):
        jb = pl.program_id(1)
        hp = pl.program_id(0)
        pair_cols = pl.ds(pl.multiple_of(hp * 256, 256), 256)

        @pl.when(jb == 0)
        def _():
            loads = [pltpu.make_async_copy(q_hbm.at[:, pair_cols], q_scr, sems.at[0]),
                     pltpu.make_async_copy(do_hbm.at[:, pair_cols], do_scr, sems.at[1])]
            for cp in loads:
                cp.start()
            dq_scr[...] = jnp.zeros_like(dq_scr)
            for cp in loads:
                cp.wait()

        i1 = [qe_ref[(2 * hp + j) * nq + kf * jb + kf - 1] + 1 for j in (0, 1)]
        both1 = jnp.minimum(i1[0], i1[1])

        def head(j, ib, dk_acc, dv_acc, diag):
            rows = pl.ds(pl.multiple_of(ib * bq, bq), bq)
            cols = slice(128 * j, 128 * (j + 1))
            qs = q_scr[rows, cols]
            dos = do_scr[rows, cols]
            kj = k_ref[:, cols]
            st = _dot(kj, qs, "nt")
            if diag is not None:
                r = lax.broadcasted_iota(jnp.int32, (bk, bq), 0)
                c = lax.broadcasted_iota(jnp.int32, (bk, bq), 1)
                st = jnp.where(c - r >= -diag, st, -jnp.inf)
            pt = jnp.exp(st)
            dst = (pt * _dot(v_ref[:, cols], dos, "nt")).astype(BF16)
            dq_scr[rows, cols] += _dot(dst, kj, "tn")
            return dk_acc + _dot(dst, qs), dv_acc + _dot(pt, dos)

        def pair(ib, carry, diag):
            return head(0, ib, carry[0], carry[1], diag) + head(1, ib, carry[2], carry[3], diag)

        carry = (jnp.zeros((bk, 128), F32),) * 4
        for d in range(kf):
            carry = pair(kf * jb + d, carry, d * bq)
        first = kf * jb + kf
        carry = lax.fori_loop(first, both1, lambda ib, c: pair(ib, c, None), carry)
        alone = [lax.fori_loop(jnp.maximum(both1, first), i1[j],
                               lambda ib, c, j=j: head(j, ib, c[0], c[1], None), carry[2 * j:2 * j + 2])
                 for j in (0, 1)]
        carry = alone[0] + alone[1]
        outs = []
        for j in (0, 1):
            spare = 64 * (1 - j)
            dk_acc, dv_acc = carry[2 * j], carry[2 * j + 1]
            outs.append((dk_acc, dv_acc, dk_acc[:, spare + 3:spare + 4]))
        msk = _head_mask((bk, 128), 0)
        dk_ref[...] = jnp.where(msk, outs[0][0], outs[1][0]).astype(BF16)
        dv_ref[...] = jnp.where(msk, outs[0][1], outs[1][1]).astype(BF16)
        cs_ref[...] = jnp.where(msk, outs[0][2], outs[1][2])

        @pl.when(jb == nk - 1)
        def _():
            first_head = _head_mask((bq, 128), 0)

            def finish(r, carry):
                rows = pl.ds(pl.multiple_of(r * bq, bq), bq)
                x0, x1 = dq_scr[rows, 0:128], dq_scr[rows, 128:256]
                q_scr[rows, 0:128] = (jnp.where(first_head, x0, x1) * 0.125).astype(BF16)
                dq_scr[rows, 0:128] = jnp.where(first_head, x0[:, 64:65], x1[:, 0:1])
                return carry

            lax.fori_loop(0, nq, finish, 0)
            head_cols = pl.ds(pl.multiple_of(hp * 128, 128), 128)
            stores = [pltpu.make_async_copy(q_scr.at[:, 0:128], dq_hbm.at[:, head_cols], sems.at[0]),
                      pltpu.make_async_copy(dq_scr.at[:, 0:128], rs_hbm.at[:, head_cols], sems.at[1])]
            for cp in stores:
                cp.start()
            for cp in stores:
                cp.wait()

    blk = pl.BlockSpec((bk, 256), lambda h, i, tbl: (i, h))
    out = pl.BlockSpec((bk, 128), lambda h, i, tbl: (i, h))
    return pl.pallas_call(
        body,
        name=name,
        grid_spec=pltpu.PrefetchScalarGridSpec(
            num_scalar_prefetch=1, grid=(4, nk), in_specs=[blk, blk, _ANY, _ANY],
            out_specs=[out, out, out, _ANY, _ANY],
            scratch_shapes=[pltpu.VMEM((t, 256), BF16), pltpu.VMEM((t, 256), BF16), pltpu.VMEM((t, 256), F32),
                            pltpu.SemaphoreType.DMA((2,))]),
        out_shape=[jax.ShapeDtypeStruct((t, A_WIDTH), BF16), jax.ShapeDtypeStruct((t, A_WIDTH), BF16),
                   jax.ShapeDtypeStruct((t, A_WIDTH), F32), jax.ShapeDtypeStruct((t, A_WIDTH), BF16),
                   jax.ShapeDtypeStruct((t, A_WIDTH), F32)],
        compiler_params=_params("arbitrary", "arbitrary"),
    )(qend, ka, va, qb, dob)


HG_ROWS = 256


def _hg_gates(hb_ref, rows, lbv):
    qb = hb_ref[rows, 0:B_WIDTH]
    fb = hb_ref[rows, B_WIDTH:2 * B_WIDTH]
    v = hb_ref[rows, 2 * B_WIDTH:3 * B_WIDTH]
    gb = hb_ref[rows, 3 * B_WIDTH:4 * B_WIDTH]
    sg = _sigmoid(fb)
    f = lbv + (1.0 - lbv) * sg
    sq = _sigmoid(qb)
    return qb, sq, qb * sq, sg, f, 1.0 - f, jnp.log(f), v, gb


def _hg_intra_factors(q, k, b):
    fac = []
    for i in range(CHUNK // SUB):
        bi = b[SUB * i:SUB * i + 1, :]
        eq = jnp.exp(b[SUB * i:SUB * (i + 1), :] - bi)
        ek = jnp.exp(jnp.minimum(bi - b, EXP_CLAMP))
        fac.append((eq, ek, q[SUB * i:SUB * (i + 1), :] * eq, k * ek))
    return fac


def _causal(n):
    r = lax.broadcasted_iota(jnp.int32, (n, n), 0)
    c = lax.broadcasted_iota(jnp.int32, (n, n), 1)
    return r >= c


def _hgrn_fwd(rest, lb, ng, name, ride=()):
    t = rest.shape[0]
    bt = min(HG_ROWS, t)
    ncb = bt // CHUNK
    n = len(ride)
    nsteps = t // bt

    def body(hb_ref, lb_ref, ng_ref, *refs):
        ride_in, (y_ref, o_ref, st_ref), ride_out = refs[:n], refs[n:n + 3], refs[n + 3:2 * n + 3]
        s_scr, sems = refs[2 * n + 3], refs[2 * n + 4:]

        @pl.when(pl.program_id(0) == 0)
        def _():
            s_scr[...] = jnp.zeros_like(s_scr)
            if n:
                _gather_start(ride_in, ride_out, sems)

        tril = _tri(CHUNK, True)
        causal = _causal(CHUNK)
        ones = jnp.ones((CHUNK, HD), F32)

        def chunk(c, carry):
            rows = pl.ds(pl.multiple_of(c * CHUNK, CHUNK), CHUNK)
            _, _, q_all, _, _, k_all, g_all, v_all, gb_all = _hg_gates(hb_ref, rows, lb_ref[...])
            b_all = _dot_hi(tril, g_all)
            qd_all = q_all * jnp.exp(b_all)
            kd_all = k_all * jnp.exp(b_all[CHUNK - 1:CHUNK, :] - b_all)
            eb_all = jnp.exp(_dot_hi(g_all, ones, "tn", exact="b"))
            sgb_all = _sigmoid(gb_all)
            for h in range(B_HEADS):
                cols = slice(h * HD, (h + 1) * HD)
                v = v_all[:, cols]
                s0 = s_scr[h]
                st_ref[c, h] = s0
                o = _dot(qd_all[:, cols], s0)
                fac = _hg_intra_factors(q_all[:, cols], k_all[:, cols], b_all[:, cols])
                a = jnp.concatenate([_dot(qe, ke, "nt") for _, _, qe, ke in fac], axis=0)
                o = o + _dot(jnp.where(causal, a, 0.0), v)
                s_scr[h] = eb_all[h * HD:(h + 1) * HD, :] * s0 + _dot(kd_all[:, cols], v, "tn")
                r = lax.rsqrt(jnp.mean(o * o, axis=-1, keepdims=True) + RMS_EPS)
                o_ref[rows, cols] = o
                y_ref[rows, cols] = (o * r * ng_ref[...] * sgb_all[:, cols]).astype(BF16)
            return carry

        lax.fori_loop(0, ncb, chunk, 0, unroll=2)

        if n:
            @pl.when(pl.program_id(0) == nsteps - 1)
            def _():
                _gather_finish(ride_in, ride_out, sems)

    res = pl.pallas_call(
        body,
        name=name,
        grid=(nsteps,),
        in_specs=[pl.BlockSpec((bt, 4 * B_WIDTH), lambda i: (i, 0)), pl.BlockSpec((1, B_WIDTH), lambda i: (0, 0)),
                  pl.BlockSpec((1, HD), lambda i: (0, 0))] + [_ANY] * n,
        out_specs=[pl.BlockSpec((bt, B_WIDTH), lambda i: (i, 0)), pl.BlockSpec((bt, B_WIDTH), lambda i: (i, 0)),
                   pl.BlockSpec((ncb, B_HEADS, HD, HD), lambda i: (i, 0, 0, 0))] + [_ANY] * n,
        out_shape=[jax.ShapeDtypeStruct((t, B_WIDTH), BF16), jax.ShapeDtypeStruct((t, B_WIDTH), F32),
                   jax.ShapeDtypeStruct((t // CHUNK, B_HEADS, HD, HD), F32)] + _gather_out_shapes(ride),
        scratch_shapes=[pltpu.VMEM((B_HEADS, HD, HD), F32)] + (_gather_scratch(n) if n else []),
        compiler_params=_params("arbitrary"),
    )(rest, lb, ng, *ride)
    return res[:3], res[3:]


def _hgrn_bwd(dy, rest, o_saved, states, lb, ng, name):
    t = rest.shape[0]
    bt = min(HG_ROWS, t)
    ncb = bt // CHUNK
    nb = t // bt

    def body(dy_ref, hb_ref, o_ref, st_ref, lb_ref, ng_ref, dh_ref, dlb_ref, dng_ref, ds_scr):
        @pl.when(pl.program_id(0) == 0)
        def _():
            ds_scr[...] = jnp.zeros_like(ds_scr)
            dlb_ref[...] = jnp.zeros_like(dlb_ref)
            dng_ref[...] = jnp.zeros_like(dng_ref)

        tril = _tri(CHUNK, True)
        triu = _tri(CHUNK, False)
        causal = _causal(CHUNK)
        ones = jnp.ones((CHUNK, HD), F32)
        ones8 = jnp.ones((8, HD), F32)
        last_row = lax.broadcasted_iota(jnp.int32, (CHUNK, B_WIDTH), 0) == CHUNK - 1

        def chunk(cc, carry):
            dng_acc, dlb_acc = carry
            c = ncb - 1 - cc
            rows = pl.ds(pl.multiple_of(c * CHUNK, CHUNK), CHUNK)
            lbv = lb_ref[...]
            qb, sq, q_all, sg, f, k_all, g_all, v_all, gb = _hg_gates(hb_ref, rows, lbv)
            b_all = _dot_hi(tril, g_all)
            ebt_all = jnp.exp(b_all)
            blast = b_all[CHUNK - 1:CHUNK, :]
            ekd_all = jnp.exp(blast - b_all)
            eb_all = jnp.exp(_dot_hi(g_all, ones, "tn", exact="b"))
            sgb = _sigmoid(gb)
            dy_all = dy_ref[rows, :].astype(F32)
            don_all = dy_all * sgb
            ngv = ng_ref[...]
            dq_l, dk_l, dks_l, dv_l, on_l, prod_l = [], [], [], [], [], []
            for h in range(B_HEADS):
                cols = slice(h * HD, (h + 1) * HD)
                q, k, v = q_all[:, cols], k_all[:, cols], v_all[:, cols]
                o = o_ref[rows, cols]
                don = don_all[:, cols]
                r = lax.rsqrt(jnp.mean(o * o, axis=-1, keepdims=True) + RMS_EPS)
                on_l.append(o * r * ngv)
                dng_acc = dng_acc + jnp.sum(don * o * r, axis=0, keepdims=True)
                doh = don * ngv
                do = r * (doh - o * (r * r) * jnp.mean(doh * o, axis=-1, keepdims=True))
                ebt, ekd = ebt_all[:, cols], ekd_all[:, cols]
                s0 = st_ref[c, h]
                ds1 = ds_scr[h]
                fac = _hg_intra_factors(q, k, b_all[:, cols])
                a = jnp.concatenate([_dot(qe, ke, "nt") for _, _, qe, ke in fac], axis=0)
                a = jnp.where(causal, a, 0.0)
                da = jnp.where(causal, _dot(do, v, "nt"), 0.0)
                dv_l.append(_dot(a, do, "tn") + _dot(k * ekd, ds1))
                dq = ebt * _dot(do, s0, "nt")
                dq_l.append(dq + jnp.concatenate(
                    [eq * _hdot(da[SUB * i:SUB * (i + 1), :], ke) for i, (eq, _, _, ke) in enumerate(fac)], axis=0))
                dk_state = ekd * _dot(v, ds1, "nt")
                dk = dk_state
                for i, (_, ek, qe, _) in enumerate(fac):
                    dk = dk + ek * _hdot(da[SUB * i:SUB * (i + 1), :], qe, "tn")
                dk_l.append(dk)
                dks_l.append(dk_state)
                prod_l.append(ds1 * s0)
                ds_scr[h] = _dot(q * ebt, do, "tn") + eb_all[h * HD:(h + 1) * HD, :] * ds1
            dq_all, dk_all = jnp.concatenate(dq_l, axis=1), jnp.concatenate(dk_l, axis=1)
            extra = jnp.exp(blast) * _dot_hi(ones8, jnp.concatenate(prod_l, axis=0), "nt")[0:1, :] \
                + jnp.sum(k_all * jnp.concatenate(dks_l, axis=1), axis=0, keepdims=True)
            db = q_all * dq_all - k_all * dk_all + jnp.where(last_row, extra, 0.0)
            df = _dot_hi(triu, db) / f - dk_all
            dlb_acc = dlb_acc + jnp.sum(df * (1.0 - sg), axis=0, keepdims=True)
            dh_ref[rows, 0:B_WIDTH] = (dq_all * (sq * (1.0 + qb * (1.0 - sq)))).astype(BF16)
            dh_ref[rows, B_WIDTH:2 * B_WIDTH] = (df * (1.0 - lbv) * sg * (1.0 - sg)).astype(BF16)
            dh_ref[rows, 2 * B_WIDTH:3 * B_WIDTH] = jnp.concatenate(dv_l, axis=1).astype(BF16)
            dh_ref[rows, 3 * B_WIDTH:4 * B_WIDTH] = (dy_all * jnp.concatenate(on_l, axis=1)
                                                     * sgb * (1.0 - sgb)).astype(BF16)
            return dng_acc, dlb_acc

        dng_sum, dlb_sum = lax.fori_loop(0, ncb, chunk, (jnp.zeros((1, HD), F32), jnp.zeros((1, B_WIDTH), F32)))
        dng_ref[...] += dng_sum
        dlb_ref[...] += dlb_sum

    rev = lambda i: (nb - 1 - i, 0)
    return pl.pallas_call(
        body,
        name=name,
        grid=(nb,),
        in_specs=[pl.BlockSpec((bt, B_WIDTH), rev), pl.BlockSpec((bt, 4 * B_WIDTH), rev),
                  pl.BlockSpec((bt, B_WIDTH), rev),
                  pl.BlockSpec((ncb, B_HEADS, HD, HD), lambda i: (nb - 1 - i, 0, 0, 0)),
                  pl.BlockSpec((1, B_WIDTH), lambda i: (0, 0)), pl.BlockSpec((1, HD), lambda i: (0, 0))],
        out_specs=[pl.BlockSpec((bt, 4 * B_WIDTH), rev), pl.BlockSpec((1, B_WIDTH), lambda i: (0, 0)),
                   pl.BlockSpec((1, HD), lambda i: (0, 0))],
        out_shape=[jax.ShapeDtypeStruct((t, 4 * B_WIDTH), BF16), jax.ShapeDtypeStruct((1, B_WIDTH), F32),
                   jax.ShapeDtypeStruct((1, HD), F32)],
        scratch_shapes=[pltpu.VMEM((B_HEADS, HD, HD), F32)],
        compiler_params=_params("arbitrary"),
    )(dy, rest, o_saved, states, lb, ng)


def _split_w_in(w_in_l):
    wqkv = w_in_l[:, :3 * A_WIDTH]
    wfa = jnp.pad(w_in_l[:, 3 * A_WIDTH:3 * A_WIDTH + A_HEADS], ((0, 0), (0, 128 - A_HEADS)))
    whb = w_in_l[:, 3 * A_WIDTH + A_HEADS:3 * A_WIDTH + A_HEADS + 4 * B_WIDTH]
    wgt = w_in_l[:, 3 * A_WIDTH + A_HEADS + 4 * B_WIDTH:]
    return wqkv, wgt, jnp.concatenate([whb, wfa], axis=1), (jnp.concatenate([wqkv, wfa], axis=1), wgt, whb)


def _merge_w_in_grad(d_att, d_gates, d_hb):
    o = 3 * A_WIDTH
    return jnp.concatenate([d_att[:, :o + A_HEADS], d_hb, d_gates], axis=1)


def _layer_fwd(x, xb, w, sp, l, ride=(), late_weights=None):
    t = x.shape[0]
    n = f"l{l}_"
    wqkv, wgates, wrest, wgroups = _split_w_in(w["w_in"])
    qkv = _matmul(xb, wqkv, "nn", BF16, MM_ROWS, 768, D_MODEL, n + "proj_qkv")
    gates = _matmul(xb, wgates, "nn", BF16, MM_ROWS, 1024, D_MODEL, n + "proj_gates")
    rest = _matmul(xb, wrest, "nn", F32, MM_ROWS, 4 * B_WIDTH + 128, D_MODEL, n + "proj_rest")
    bf = jnp.pad(sp["b_fgate"], (0, 128 - A_HEADS)).reshape(1, 128)
    fcum, fcol = _fox_gate_fwd(rest, bf, n + "fox_gate_fwd")
    qa, ka, va, qn, kn = _fox_prep_fwd(qkv, fcol, n + "fox_prep_fwd")
    kstart, qend = _fox_block_ranges(qn, kn, fcum)
    ya, lse = _fox_fwd(qa, ka, va, kstart, n + "fox_fwd")
    lb = sp["lb"].reshape(1, B_WIDTH)
    ng = sp["norm_g"].reshape(1, HD)
    (yb, ob, states), gathered = _hgrn_fwd(rest, lb, ng, n + "hgrn_fwd", ride)
    if ride:
        late = late_weights(l, gathered)
        w = {**w, **late[l]}
    merged = _merge_fwd(ya, yb, w["w_pa"], w["w_pb"], gates, n + "merge_fwd")
    x1, x1b, xh1, rs1 = _mm_res_ln(merged, w["w_out"], x, sp["ln1_g"], sp["ln1_b"], n + "out_ln1")
    wu, wg = w["w_ff_in"][:, :FFN_HIDDEN], w["w_ff_in"][:, FFN_HIDDEN:]
    a, hu, hg = _ffn_in_swiglu(x1b, wu, wg, n + "ffn_in_swiglu")
    x2, x2b, xh2, rs2 = _mm_res_ln(a, w["w_ff_out"], x1, sp["ln2_g"], sp["ln2_b"], n + "ffn_out_ln2")
    saved = dict(xb=xb, wgroups=wgroups, qkv=qkv, rest=rest, gates=gates, bf=bf, fcol=fcol, ka=ka, va=va, ya=ya, lse=lse,
                 qend=qend,
                 lb=lb, ng=ng, yb=yb, ob=ob, states=states, merged=merged, x1b=x1b, xh1=xh1, rs1=rs1, a=a,
                 wu=wu, wg=wg, hu=hu, hg=hg,
                 xh2=xh2, rs2=rs2)
    return x2, x2b, saved, (late if ride else None)


def _layer_bwd(dys, coefs, w, sp, s, l):
    n = f"l{l}_"
    dz2, dz2b, dg2, db2 = _ln_bwd(dys, coefs, s["xh2"], s["rs2"], sp["ln2_g"], n + "ln2_bwd")
    du, dg = _ffn_out_dx_swiglu(dz2b, w["w_ff_out"], s["hu"], s["hg"], n + "ffn_out_dx_swiglu")
    d_wffout = _matmul(s["a"], dz2b, "tn", F32, 1408, 1024, DW_ROWS, n + "ffn_out_dw")
    dx1 = _matmul_nt_sum([du, dg], [s["wu"], s["wg"]], n + "ffn_in_dx")
    d_wffin = jnp.concatenate([_matmul(s["x1b"], du, "tn", F32, 1024, 1408, DW_ROWS, n + "ffn_in_dw_u"),
                               _matmul(s["x1b"], dg, "tn", F32, 1024, 1408, DW_ROWS, n + "ffn_in_dw_g")], axis=1)
    dz1, dz1b, dg1, db1 = _ln_bwd([dz2, dx1], [ALPHA, 1.0], s["xh1"], s["rs1"], sp["ln1_g"], n + "ln1_bwd")
    d_wout = _matmul(s["merged"], dz1b, "tn", F32, 1024, 1024, DW_ROWS, n + "out_dw")
    dgates, dpa, dpb, dya, dyb = _merge_bwd(dz1b, w["w_out"], s["ya"], s["yb"], w["w_pa"], w["w_pb"], s["gates"],
                                  n + "merge_bwd")
    d_wpa = _matmul(s["ya"], dpa, "tn", F32, 512, 1024, DW_ROWS, n + "pa_dw")
    d_wpb = _matmul(s["yb"], dpb, "tn", F32, 512, 1024, DW_ROWS, n + "pb_dw")
    qb, dob = _fox_prep_bwd(s["qkv"], s["fcol"], s["lse"], dya, s["ya"], n + "fox_prep_bwd")
    dk, dv, csum, dq, rsum = _fox_bwd(qb, s["ka"], s["va"], dob, s["qend"], n + "fox_bwd")
    dfa, dbf = _fox_gate_bwd(rsum, csum, s["rest"], s["bf"], n + "fox_gate_bwd")
    dhb, dlb, dng = _hgrn_bwd(dyb, s["rest"], s["ob"], s["states"], s["lb"], s["ng"], n + "hgrn_bwd")
    datt = jnp.concatenate([dq, dk, dv, dfa], axis=1)
    dxp = _matmul_nt_sum([datt, dgates, dhb], list(s["wgroups"]), n + "proj_dx", (ALPHA, dz1) if l == 0 else None)
    d_w_in = _merge_w_in_grad(_matmul(s["xb"], datt, "tn", F32, 1024, 1664, DW_ROWS, n + "proj_dw_att"),
                              _matmul(s["xb"], dgates, "tn", F32, 1024, 1024, DW_ROWS, n + "proj_dw_gates"),
                              _matmul(s["xb"], dhb, "tn", F32, 1024, 1024, DW_ROWS, n + "proj_dw_hgrn"))
    grads = dict(w_in=d_w_in, w_pa=d_wpa, w_pb=d_wpb, w_out=d_wout, w_ff_in=d_wffin,
                 w_ff_out=d_wffout, b_fgate=dbf[0, :A_HEADS], lb=dlb[0], norm_g=dng[0], ln1_g=dg1[0], ln1_b=db1[0],
                 ln2_g=dg2[0], ln2_b=db2[0])
    if l == 0:
        return [dxp], [1.0], grads
    return [dz1, dxp], [ALPHA, 1.0], grads


def _lower_bounds(logits):
    sm = jax.nn.softmax(logits.astype(F32), axis=0)
    return jnp.cumsum(sm, axis=0) - sm[0:1]


def _local_step(x, target, wfull, small, rides=None, late_weights=None):
    lbs, lb_vjp = jax.vjp(_lower_bounds, small["hgrn_lb_logits"])
    h, hb = x, x.astype(BF16)
    wfull = list(wfull)
    saved, sps = [], []
    for l in range(DEPTH):
        sp = dict(b_fgate=small["b_fgate"][l], lb=lbs[l], norm_g=small["hgrn_norm_g"][l], ln1_g=small["ln1_g"][l],
                  ln1_b=small["ln1_b"][l], ln2_g=small["ln2_g"][l], ln2_b=small["ln2_b"][l])
        h, hb, s, late = _layer_fwd(h, hb, wfull[l], sp, l, rides[l] if rides else (), late_weights)
        if late is not None:
            wfull = [{**wfull[k], **late[k]} for k in range(DEPTH)]
        saved.append(s)
        sps.append(sp)
    dy, lpart = _loss_head(h, target)
    dys, coefs = [dy], [1.0]
    grads = [None] * DEPTH
    for l in reversed(range(DEPTH)):
        dys, coefs, grads[l] = _layer_bwd(dys, coefs, wfull[l], sps[l], saved[l], l)
    grad_x = dys[0]
    d_logits = lb_vjp(jnp.stack([grads[l]["lb"] for l in range(DEPTH)]))[0]
    return lpart[0, 0], grad_x, grads, d_logits


_BIG = [("w_in", "w_in", (D_MODEL, IN_TOTAL), 1), ("w_branch_a", "w_pa", (A_WIDTH, D_MODEL), 1),
        ("w_branch_b", "w_pb", (B_WIDTH, D_MODEL), 1), ("w_out", "w_out", (D_MODEL, D_MODEL), 0),
        ("w_ff_in", "w_ff_in", (D_MODEL, 2 * FFN_HIDDEN), 1), ("w_ff_out", "w_ff_out", (FFN_HIDDEN, D_MODEL), 0)]
_SMALL = [("b_fgate", A_HEADS), ("hgrn_lb_logits", B_WIDTH), ("hgrn_norm_g", HD), ("ln1_g", D_MODEL),
          ("ln1_b", D_MODEL), ("ln2_g", D_MODEL), ("ln2_b", D_MODEL)]
N_BIG = len(_BIG)
SMALL_ROWS = 80


def _by_chip(full, axis):
    if axis == 0:
        return full.reshape(N_CHIPS, full.shape[0] // N_CHIPS, full.shape[1])
    n = full.shape[1] // N_CHIPS
    return jnp.stack([full[:, q * n:(q + 1) * n] for q in range(N_CHIPS)])


def _from_chips(shards, axis):
    if axis == 0:
        return shards.reshape(N_CHIPS * shards.shape[1], shards.shape[2])
    return jnp.concatenate([shards[q] for q in range(N_CHIPS)], axis=1)


def _pack_small(per_name):
    flat = jnp.concatenate([per_name[name].reshape(-1) for name, _ in _SMALL])
    return jnp.pad(flat, (0, SMALL_ROWS * 128 - flat.shape[0])).reshape(SMALL_ROWS, 128)


def _unpack_small(slab):
    flat, out, r = slab.reshape(-1), {}, 0
    for name, n in _SMALL:
        out[name] = flat[r:r + DEPTH * n].reshape(DEPTH, n)
        r += DEPTH * n
    return out


_ANY = pl.BlockSpec(memory_space=pl.ANY)


def _place():
    return lax.axis_index("x"), lax.axis_index("y"), lax.axis_index("c")


def _other_chips(x, y):
    return [(1 - x, y), (x, 1 - y), (1 - x, 1 - y)]


def _chip_exchange(mine_of, out_refs, send_sems, recv_sems, local_sems):
    _chip_exchange_start(mine_of, out_refs, send_sems, recv_sems, local_sems)
    _chip_exchange_wait(mine_of, out_refs, send_sems, recv_sems, local_sems)


def _chip_exchange_copies(mine_of, out_refs, send_sems, recv_sems, local_sems):
    x, y, c = _place()
    q = 2 * x + y
    local = [pltpu.make_async_copy(mine_of(w, q), out_ref.at[q], local_sems.at[w]) for w, out_ref in enumerate(out_refs)]
    sends, recvs = [], []
    for k, (px, py) in enumerate(_other_chips(x, y)):
        for w, out_ref in enumerate(out_refs):
            sems = dict(send_sem=send_sems.at[3 * w + k], recv_sem=recv_sems.at[3 * w + k], device_id=(px, py, c),
                        device_id_type=MESH)
            sends.append(pltpu.make_async_remote_copy(src_ref=mine_of(w, 2 * px + py), dst_ref=out_ref.at[q], **sems))
            recvs.append(pltpu.make_async_remote_copy(src_ref=mine_of(w, q), dst_ref=out_ref.at[2 * px + py], **sems))
    return local, sends, recvs


def _chip_exchange_start(*args):
    local, sends, _ = _chip_exchange_copies(*args)
    for cp in local + sends:
        cp.start()


def _chip_exchange_wait(*args):
    local, sends, recvs = _chip_exchange_copies(*args)
    for cp in recvs:
        cp.wait_recv()
    for cp in sends:
        cp.wait_send()
    for cp in local:
        cp.wait()


def _sem_scratch(n):
    return [pltpu.SemaphoreType.DMA((3 * n,)), pltpu.SemaphoreType.DMA((3 * n,)), pltpu.SemaphoreType.DMA((n,))]


def _gather_scratch(n):
    return _sem_scratch(n) + [pltpu.SemaphoreType.DMA((n,)), pltpu.SemaphoreType.DMA((n,))]


def _gather_out_shapes(mine):
    return [jax.ShapeDtypeStruct((DEPTH, N_CHIPS) + m.shape[1:], m.dtype) for m in mine]


def _gather_start(in_refs, out_refs, sems):
    c = lax.axis_index("c")
    _chip_exchange_start(lambda w, q: in_refs[w].at[c], [o.at[c] for o in out_refs], *sems[:3])


def _gather_finish(in_refs, out_refs, sems):
    x, y, c = _place()
    _chip_exchange_wait(lambda w, q: in_refs[w].at[c], [o.at[c] for o in out_refs], *sems[:3])
    pair_send, pair_recv = sems[3:]
    sibling = (x, y, 1 - c)
    fwds = []
    for w, o in enumerate(out_refs):
        cp = pltpu.make_async_remote_copy(src_ref=o.at[c], dst_ref=o.at[c], send_sem=pair_send.at[w],
                                          recv_sem=pair_recv.at[w], device_id=sibling, device_id_type=MESH)
        cp.start()
        fwds.append(cp)
    for w, o in enumerate(out_refs):
        pltpu.make_async_remote_copy(src_ref=o.at[1 - c], dst_ref=o.at[1 - c], send_sem=pair_send.at[w],
                                     recv_sem=pair_recv.at[w], device_id=sibling, device_id_type=MESH).wait_recv()
    for cp in fwds:
        cp.wait_send()


def _gather_weights(mine):
    n = len(mine)

    def body(*refs):
        in_refs, out_refs, sems = refs[:n], refs[n:2 * n], refs[2 * n:]
        _gather_start(in_refs, out_refs, sems)
        _gather_finish(in_refs, out_refs, sems)

    return pl.pallas_call(
        body, name="gather_weights", in_specs=[_ANY] * n, out_specs=[_ANY] * n,
        out_shape=_gather_out_shapes(mine), scratch_shapes=_gather_scratch(n),
    )(*mine)


def _pair_exchange(gs):
    n = len(gs)

    def body(*refs):
        g_refs, a_refs, send_sems, recv_sems = refs[:n], refs[n:2 * n], refs[2 * n], refs[2 * n + 1]
        x, y, c = _place()
        cps = []
        for w in range(n):
            cp = pltpu.make_async_remote_copy(src_ref=g_refs[w].at[1 - c], dst_ref=a_refs[w], send_sem=send_sems.at[w],
                                              recv_sem=recv_sems.at[w], device_id=(x, y, 1 - c), device_id_type=MESH)
            cp.start()
            cps.append(cp)
        for cp in cps:
            cp.wait()

    return pl.pallas_call(
        body, name="grad_pair_exchange", in_specs=[_ANY] * n, out_specs=[_ANY] * n,
        out_shape=[jax.ShapeDtypeStruct(g.shape[1:], g.dtype) for g in gs],
        scratch_shapes=[pltpu.SemaphoreType.DMA((n,)), pltpu.SemaphoreType.DMA((n,))],
    )(*gs)


def _row_block(rows):
    return math.gcd(rows, 256)


def _pair_sum(g, a, layer, name):
    _, nq, rows, cols = g.shape
    tb = _row_block(rows)

    def body(l_ref, g_ref, a_ref, o_ref):
        o_ref[...] = (g_ref[...].astype(F32) + a_ref[...].astype(F32)).astype(BF16)

    return pl.pallas_call(
        body, name=name,
        grid_spec=pltpu.PrefetchScalarGridSpec(
            num_scalar_prefetch=1, grid=(nq, rows // tb),
            in_specs=[pl.BlockSpec((None, None, tb, cols), lambda q, i, l_ref: (l_ref[0], q, i, 0)),
                      pl.BlockSpec((None, tb, cols), lambda q, i, l_ref: (q, i, 0))],
            out_specs=pl.BlockSpec((None, tb, cols), lambda q, i, l_ref: (q, i, 0))),
        out_shape=jax.ShapeDtypeStruct((nq, rows, cols), BF16),
        compiler_params=_params("parallel", "parallel"),
    )(layer.reshape(1).astype(jnp.int32), g, a)


def _shard_exchange(ps):
    n = len(ps)

    def body(*refs):
        p_refs, b_refs = refs[:n], refs[n:2 * n]
        send_sems, recv_sems, local_sems = refs[2 * n:]
        _chip_exchange(lambda w, q: p_refs[w].at[q], b_refs, send_sems, recv_sems, local_sems)

    return pl.pallas_call(
        body, name="grad_shard_exchange", in_specs=[_ANY] * n, out_specs=[_ANY] * n,
        out_shape=[jax.ShapeDtypeStruct(p.shape, p.dtype) for p in ps],
        scratch_shapes=_sem_scratch(n),
    )(*ps)


def _sum4(b, name):
    _, rows, cols = b.shape
    tb = _row_block(rows)

    def body(b_ref, o_ref):
        o_ref[...] = ((b_ref[0].astype(F32) + b_ref[1].astype(F32)) + b_ref[2].astype(F32)) + b_ref[3].astype(F32)

    return pl.pallas_call(
        body, name=name, grid=(rows // tb,),
        in_specs=[pl.BlockSpec((N_CHIPS, tb, cols), lambda i: (0, i, 0))],
        out_specs=pl.BlockSpec((tb, cols), lambda i: (i, 0)),
        out_shape=jax.ShapeDtypeStruct((rows, cols), F32),
        compiler_params=_params("parallel"),
    )(b)


def _result_exchange(gcs):
    n = len(gcs)

    def body(*refs):
        g_refs, o_refs, send_sems, recv_sems = refs[:n], refs[n:2 * n], refs[2 * n], refs[2 * n + 1]
        x, y, c = _place()
        cps = []
        for w in range(n):
            cp = pltpu.make_async_remote_copy(src_ref=g_refs[w], dst_ref=o_refs[w], send_sem=send_sems.at[w],
                                              recv_sem=recv_sems.at[w], device_id=(x, y, 1 - c), device_id_type=MESH)
            cp.start()
            cps.append(cp)
        for cp in cps:
            cp.wait()

    return pl.pallas_call(
        body, name="grad_result_exchange", in_specs=[_ANY] * n, out_specs=[_ANY] * n,
        out_shape=[jax.ShapeDtypeStruct(g.shape, g.dtype) for g in gcs],
        scratch_shapes=[pltpu.SemaphoreType.DMA((n,)), pltpu.SemaphoreType.DMA((n,))],
    )(*gcs)


def _allreduce_small(v):
    def body(v_ref, o_ref, buf, send_sems, recv_sems):
        x, y, c = _place()
        me = 4 * x + 2 * y + c
        buf[me] = v_ref[...]
        peers = []
        for k in range(1, N_DEV):
            px = 1 - x if k & 4 else x
            py = 1 - y if k & 2 else y
            pc = 1 - c if k & 1 else c
            peers.append((px, py, pc))
        sends = []
        for k, peer in enumerate(peers):
            cp = pltpu.make_async_remote_copy(src_ref=v_ref, dst_ref=buf.at[me], send_sem=send_sems.at[k],
                                              recv_sem=recv_sems.at[k], device_id=peer, device_id_type=MESH)
            cp.start()
            sends.append(cp)
        for k, (px, py, pc) in enumerate(peers):
            pltpu.make_async_remote_copy(src_ref=v_ref, dst_ref=buf.at[4 * px + 2 * py + pc], send_sem=send_sems.at[k],
                                         recv_sem=recv_sems.at[k], device_id=(px, py, pc),
                                         device_id_type=MESH).wait_recv()
        for cp in sends:
            cp.wait_send()
        acc = buf[0]
        for i in range(1, N_DEV):
            acc = acc + buf[i]
        o_ref[...] = acc

    vm = pl.BlockSpec(memory_space=pltpu.VMEM)
    return pl.pallas_call(
        body, name="small_allreduce", in_specs=[vm], out_specs=vm,
        out_shape=jax.ShapeDtypeStruct(v.shape, F32),
        scratch_shapes=[pltpu.VMEM((N_DEV,) + v.shape, F32), pltpu.SemaphoreType.DMA((N_DEV - 1,)),
                        pltpu.SemaphoreType.DMA((N_DEV - 1,))],
    )(v)


def _adam_update(w, g, m, v):
    nm = ADAM_B1 * m + (1.0 - ADAM_B1) * g
    nv = ADAM_B2 * v + (1.0 - ADAM_B2) * (g * g)
    m_hat = nm / (1.0 - ADAM_B1 ** ADAM_STEP)
    v_hat = nv / (1.0 - ADAM_B2 ** ADAM_STEP)
    return -ADAM_LR * (m_hat / (jnp.sqrt(v_hat) + ADAM_EPS) + ADAM_WD * w), nm, nv


def _adamw_small(w, g, m, v, name):
    def body(w_ref, g_ref, m_ref, v_ref, d_ref, nm_ref, nv_ref):
        d_ref[...], nm_ref[...], nv_ref[...] = _adam_update(w_ref[...], g_ref[...], m_ref[...], v_ref[...])

    vm = pl.BlockSpec(memory_space=pltpu.VMEM)
    return pl.pallas_call(
        body, name=name, in_specs=[vm] * 4, out_specs=[vm] * 3,
        out_shape=[jax.ShapeDtypeStruct(w.shape, F32)] * 3,
    )(w, g, m, v)


def _adamw_big(w, m, v, g_own, g_other, layer, name):
    _, rows, cols = w.shape
    tb = _row_block(rows)

    def body(l_ref, w_ref, m_ref, v_ref, go_ref, gx_ref, g_ref, d_ref, nm_ref, nv_ref):
        gv = jnp.where(pl.program_id(0) == l_ref[0], go_ref[...], gx_ref[...])
        g_ref[...] = gv
        d_ref[...], nm_ref[...], nv_ref[...] = _adam_update(w_ref[...], gv, m_ref[...], v_ref[...])

    per_layer = pl.BlockSpec((None, tb, cols), lambda l, i, l_ref: (l, i, 0))
    shared = pl.BlockSpec((tb, cols), lambda l, i, l_ref: (i, 0))
    return pl.pallas_call(
        body, name=name,
        grid_spec=pltpu.PrefetchScalarGridSpec(
            num_scalar_prefetch=1, grid=(DEPTH, rows // tb),
            in_specs=[per_layer, per_layer, per_layer, shared, shared], out_specs=[per_layer] * 4),
        out_shape=[jax.ShapeDtypeStruct(w.shape, F32)] * 4,
        compiler_params=_params("parallel", "parallel"),
    )(layer.reshape(1).astype(jnp.int32), w, m, v, g_own, g_other)


def kernel(x, w_in, b_fgate, hgrn_lb_logits, hgrn_norm_g, w_branch_a, w_branch_b, w_out, ln1_g, ln1_b, w_ff_in, w_ff_out, ln2_g, ln2_b, loss_target, m_w_in, m_b_fgate, m_hgrn_lb_logits, m_hgrn_norm_g, m_w_branch_a, m_w_branch_b, m_w_out, m_ln1_g, m_ln1_b, m_w_ff_in, m_w_ff_out, m_ln2_g, m_ln2_b, v_w_in, v_b_fgate, v_hgrn_lb_logits, v_hgrn_norm_g, v_w_branch_a, v_w_branch_b, v_w_out, v_ln1_g, v_ln1_b, v_w_ff_in, v_w_ff_out, v_ln2_g, v_ln2_b):
    weights = dict(w_in=w_in, b_fgate=b_fgate, hgrn_lb_logits=hgrn_lb_logits, hgrn_norm_g=hgrn_norm_g,
                   w_branch_a=w_branch_a, w_branch_b=w_branch_b, w_out=w_out, ln1_g=ln1_g, ln1_b=ln1_b,
                   w_ff_in=w_ff_in, w_ff_out=w_ff_out, ln2_g=ln2_g, ln2_b=ln2_b)
    mom1 = dict(w_in=m_w_in, b_fgate=m_b_fgate, hgrn_lb_logits=m_hgrn_lb_logits, hgrn_norm_g=m_hgrn_norm_g,
                w_branch_a=m_w_branch_a, w_branch_b=m_w_branch_b, w_out=m_w_out, ln1_g=m_ln1_g, ln1_b=m_ln1_b,
                w_ff_in=m_w_ff_in, w_ff_out=m_w_ff_out, ln2_g=m_ln2_g, ln2_b=m_ln2_b)
    mom2 = dict(w_in=v_w_in, b_fgate=v_b_fgate, hgrn_lb_logits=v_hgrn_lb_logits, hgrn_norm_g=v_hgrn_norm_g,
                w_branch_a=v_w_branch_a, w_branch_b=v_w_branch_b, w_out=v_w_out, ln1_g=v_ln1_g, ln1_b=v_ln1_b,
                w_ff_in=v_w_ff_in, w_ff_out=v_w_ff_out, ln2_g=v_ln2_g, ln2_b=v_ln2_b)
    core = lax.axis_index("c")

    def halves(shard):
        return shard.astype(BF16).reshape(2, shard.shape[0] // 2, shard.shape[1])

    def from_halves(gathered, axis):
        return _from_chips(jnp.concatenate([gathered[0], gathered[1]], axis=1), axis)

    mid, ffn = _BIG[1:4], _BIG[4:]

    def late_weights(l, gathered):
        if l == 1:
            return [{}, {key: from_halves(gathered[w], axis) for w, (_, key, _, axis) in enumerate(ffn)}]
        upd = [{key: _from_chips(gathered[1 + w][k], axis) for w, (_, key, _, axis) in enumerate(mid)}
               for k in range(DEPTH)]
        upd[1]["w_in"] = from_halves(gathered[0], 1)
        upd[0].update({key: from_halves(gathered[4 + w], axis) for w, (_, key, _, axis) in enumerate(ffn)})
        return upd

    wfull = [{"w_in": from_halves(_gather_weights([halves(weights["w_in"][0])])[0], 1)}, {}]
    rides = [[halves(weights["w_in"][1])] + [weights[name].astype(BF16) for name, _, _, _ in mid]
             + [halves(weights[name][0]) for name, _, _, _ in ffn],
             [halves(weights[name][1]) for name, _, _, _ in ffn]]
    small = {name: weights[name] for name, _ in _SMALL}

    loss_part, grad_x, grads, d_logits = _local_step(x[0], loss_target[0], wfull, small, rides, late_weights)

    g_all = [jnp.stack([_by_chip(grads[l][key], axis) for l in range(DEPTH)]).astype(BF16)
             for _, key, _, axis in _BIG]
    received = _pair_exchange(g_all)
    pair = [_pair_sum(g_all[w], received[w], core, f"grad_pair_sum_{w}") for w in range(N_BIG)]
    by_chip = _shard_exchange(pair)
    g_layer = [_sum4(by_chip[w], f"grad_chip_sum_{w}") for w in range(N_BIG)]
    g_other = _result_exchange(g_layer)
    out_g, out_d, out_m, out_v = {}, {}, {}, {}
    for w, (name, _, _, _) in enumerate(_BIG):
        out_g[name], out_d[name], out_m[name], out_v[name] = _adamw_big(
            weights[name], mom1[name], mom2[name], g_layer[w], g_other[w], core, f"adamw_{name}")

    small_grads = {name: jnp.stack([grads[l][key] for l in range(DEPTH)])
                   for name, key in [("b_fgate", "b_fgate"), ("hgrn_norm_g", "norm_g"), ("ln1_g", "ln1_g"),
                                     ("ln1_b", "ln1_b"), ("ln2_g", "ln2_g"), ("ln2_b", "ln2_b")]}
    small_grads["hgrn_lb_logits"] = d_logits
    gs = _allreduce_small(_pack_small(small_grads))
    ds, ms, vs = _adamw_small(_pack_small(small), gs, _pack_small({n: mom1[n] for n, _ in _SMALL}),
                              _pack_small({n: mom2[n] for n, _ in _SMALL}), "adamw_small")
    for tree, slab in ((out_g, gs), (out_d, ds), (out_m, ms), (out_v, vs)):
        tree.update(_unpack_small(slab))

    loss = lax.psum(loss_part, ("x", "y", "c"))
    order = ["w_in", "b_fgate", "hgrn_lb_logits", "hgrn_norm_g", "w_branch_a", "w_branch_b", "w_out", "ln1_g", "ln1_b",
             "w_ff_in", "w_ff_out", "ln2_g", "ln2_b"]
    return (loss, grad_x[None], *[out_g[n] for n in order], *[out_d[n] for n in order],
            *[out_m[n] for n in order], *[out_v[n] for n in order])
```

```python
import math

import jax
import jax.numpy as jnp
from jax import lax
from jax.experimental import pallas as pl
from jax.experimental.pallas import tpu as pltpu

F32 = jnp.float32
BF16 = jnp.bfloat16

D_MODEL = 1024
DEPTH = 2
A_HEADS = 8
A_WIDTH = 512
B_WIDTH = 512
B_HEADS = 4
HD = 128
CHUNK = 64
SUB = 16
FFN_HIDDEN = 2816
IN_TOTAL = 5640
ALPHA = (2 * DEPTH) ** 0.25
LN_EPS = 1e-5
RMS_EPS = 1e-6
ADAM_LR = 0.001
ADAM_B1 = 0.9
ADAM_B2 = 0.999
ADAM_EPS = 1e-08
ADAM_WD = 0.01
ADAM_STEP = 10
EXP_CLAMP = 60.0

VMEM_LIMIT_BYTES = 56 * 1024 * 1024
MM_ROWS = 1024
DW_ROWS = 2048
N_CHIPS = 4
N_DEV = 8
MESH = pl.DeviceIdType.MESH

_DN = {
    "nn": (((1,), (0,)), ((), ())),
    "nt": (((1,), (1,)), ((), ())),
    "tn": (((0,), (0,)), ((), ())),
}


def _dot(a, b, mode="nn"):
    return lax.dot_general(a.astype(BF16), b.astype(BF16), _DN[mode], preferred_element_type=F32)


def _pieces(x):
    h = x.astype(BF16)
    r = x - h.astype(F32)
    m = r.astype(BF16)
    return h, m, (r - m.astype(F32)).astype(BF16)


def _dot_hi(a, b, mode="nn", exact="a"):
    if exact == "a":
        h, m, l = _pieces(b)
        return (_dot(a, l, mode) + _dot(a, m, mode)) + _dot(a, h, mode)
    h, m, l = _pieces(a)
    return (_dot(l, b, mode) + _dot(m, b, mode)) + _dot(h, b, mode)


def _hdot(a, b, mode="nn"):
    bh, bl, _ = _pieces(b)
    return _dot(a, bl, mode) + _dot(a, bh, mode)


def _params(*sem):
    return pltpu.CompilerParams(dimension_semantics=sem, vmem_limit_bytes=VMEM_LIMIT_BYTES)


def _sigmoid(x):
    return 1.0 / (1.0 + jnp.exp(-x))


def _matmul(a, b, mode, out_dtype, tm, tn, tk, name):
    if mode == "nn":
        (m, k), (k2, n) = a.shape, b.shape
    elif mode == "nt":
        (m, k), (n, k2) = a.shape, b.shape
    else:
        (k, m), (k2, n) = a.shape, b.shape
    assert k == k2, (a.shape, b.shape, mode)
    tm, tn, tk = min(tm, m), min(tn, n), min(tk, k)
    assert m % tm == 0 and n % tn == 0 and k % tk == 0, (a.shape, b.shape, tm, tn, tk)
    nk = k // tk
    if mode == "tn":
        a_spec = pl.BlockSpec((tk, tm), lambda j, i, kk: (kk, i))
    else:
        a_spec = pl.BlockSpec((tm, tk), lambda j, i, kk: (i, kk))
    if mode == "nt":
        b_spec = pl.BlockSpec((tn, tk), lambda j, i, kk: (j, kk))
    else:
        b_spec = pl.BlockSpec((tk, tn), lambda j, i, kk: (kk, j))
    use_acc = nk > 1 and out_dtype != F32

    def body(a_ref, b_ref, o_ref, *scratch):
        p = _dot(a_ref[...], b_ref[...], mode)
        if nk == 1:
            o_ref[...] = p.astype(out_dtype)
            return
        acc_ref = scratch[0] if use_acc else o_ref
        kk = pl.program_id(2)

        @pl.when(kk == 0)
        def _():
            acc_ref[...] = p

        @pl.when(kk > 0)
        def _():
            acc_ref[...] += p

        if use_acc:
            @pl.when(kk == nk - 1)
            def _():
                o_ref[...] = acc_ref[...].astype(out_dtype)

    return pl.pallas_call(
        body,
        name=name,
        grid=(n // tn, m // tm, nk),
        in_specs=[a_spec, b_spec],
        out_specs=pl.BlockSpec((tm, tn), lambda j, i, kk: (i, j)),
        out_shape=jax.ShapeDtypeStruct((m, n), out_dtype),
        scratch_shapes=[pltpu.VMEM((tm, tn), F32)] if use_acc else [],
        compiler_params=_params("parallel", "parallel", "arbitrary"),
    )(a, b)


def _matmul_nt_sum(a_list, b_list, name, plus=None, tm=512):
    m = a_list[0].shape[0]
    n = b_list[0].shape[0]
    tm = min(tm, m)
    np_ = len(a_list)
    extra = [] if plus is None else [plus[1]]

    def body(*refs):
        o_ref = refs[-1]
        acc = _dot(refs[0][...], refs[np_][...], "nt")
        for p in range(1, np_):
            acc = acc + _dot(refs[p][...], refs[np_ + p][...], "nt")
        if plus is not None:
            acc = acc + plus[0] * refs[2 * np_][...]
        o_ref[...] = acc

    row = pl.BlockSpec((tm, n), lambda i: (i, 0))
    return pl.pallas_call(
        body,
        name=name,
        grid=(m // tm,),
        in_specs=[pl.BlockSpec((tm, a.shape[1]), lambda i: (i, 0)) for a in a_list]
        + [pl.BlockSpec(b.shape, lambda i: (0, 0)) for b in b_list] + [row] * len(extra),
        out_specs=row,
        out_shape=jax.ShapeDtypeStruct((m, n), F32),
        compiler_params=_params("parallel"),
    )(*a_list, *b_list, *extra)


def _mm_res_ln(a, w, res, g, b, name, tm=512):
    t, k = a.shape
    d = w.shape[1]
    tm = min(tm, t)

    def body(a_ref, w_ref, r_ref, g_ref, b_ref, y_ref, yb_ref, xh_ref, rs_ref):
        z = ALPHA * r_ref[...] + _dot(a_ref[...], w_ref[...])
        mu = jnp.mean(z, axis=-1, keepdims=True)
        zc = z - mu
        var = jnp.mean(zc * zc, axis=-1, keepdims=True)
        rstd = lax.rsqrt(var + LN_EPS)
        xh = zc * rstd
        y = xh * g_ref[...] + b_ref[...]
        y_ref[...] = y
        yb_ref[...] = y.astype(BF16)
        xh_ref[...] = xh
        rs_ref[...] = rstd

    row = lambda i: (i, 0)
    fix = lambda i: (0, 0)
    return pl.pallas_call(
        body,
        name=name,
        grid=(t // tm,),
        in_specs=[pl.BlockSpec((tm, k), row), pl.BlockSpec((k, d), fix), pl.BlockSpec((tm, d), row),
                  pl.BlockSpec((1, d), fix), pl.BlockSpec((1, d), fix)],
        out_specs=[pl.BlockSpec((tm, d), row), pl.BlockSpec((tm, d), row), pl.BlockSpec((tm, d), row),
                   pl.BlockSpec((tm, 1), row)],
        out_shape=[jax.ShapeDtypeStruct((t, d), F32), jax.ShapeDtypeStruct((t, d), BF16),
                   jax.ShapeDtypeStruct((t, d), F32), jax.ShapeDtypeStruct((t, 1), F32)],
        compiler_params=_params("parallel"),
    )(a, w, res, g.reshape(1, d), b.reshape(1, d))


def _ln_bwd(dys, coefs, xhat, rstd, g, name, tm=512):
    t, d = xhat.shape
    tm = min(tm, t)
    n_in = len(dys)

    def body(*refs):
        dy_refs = refs[:n_in]
        xh_ref, rs_ref, g_ref, dz_ref, dzb_ref, dg_ref, db_ref = refs[n_in:]
        dy = coefs[0] * dy_refs[0][...].astype(F32)
        for c, r in zip(coefs[1:], dy_refs[1:]):
            dy = dy + c * r[...].astype(F32)
        xh = xh_ref[...]
        dxh = dy * g_ref[...]
        m1 = jnp.mean(dxh, axis=-1, keepdims=True)
        m2 = jnp.mean(dxh * xh, axis=-1, keepdims=True)
        dz = rs_ref[...] * (dxh - m1 - xh * m2)
        dz_ref[...] = dz
        dzb_ref[...] = dz.astype(BF16)
        pg = jnp.sum(dy * xh, axis=0, keepdims=True)
        pb = jnp.sum(dy, axis=0, keepdims=True)

        @pl.when(pl.program_id(0) == 0)
        def _():
            dg_ref[...] = pg
            db_ref[...] = pb

        @pl.when(pl.program_id(0) > 0)
        def _():
            dg_ref[...] += pg
            db_ref[...] += pb

    row = lambda i: (i, 0)
    fix = lambda i: (0, 0)
    return pl.pallas_call(
        body,
        name=name,
        grid=(t // tm,),
        in_specs=[pl.BlockSpec((tm, d), row)] * n_in
        + [pl.BlockSpec((tm, d), row), pl.BlockSpec((tm, 1), row), pl.BlockSpec((1, d), fix)],
        out_specs=[pl.BlockSpec((tm, d), row), pl.BlockSpec((tm, d), row), pl.BlockSpec((1, d), fix),
                   pl.BlockSpec((1, d), fix)],
        out_shape=[jax.ShapeDtypeStruct((t, d), F32), jax.ShapeDtypeStruct((t, d), BF16),
                   jax.ShapeDtypeStruct((1, d), F32), jax.ShapeDtypeStruct((1, d), F32)],
        compiler_params=_params("arbitrary"),
    )(*dys, xhat, rstd, g.reshape(1, d))


def _loss_head(y, target, name="loss_head", tm=512):
    t, d = y.shape
    tm = min(tm, t)

    def body(y_ref, t_ref, dy_ref, l_ref):
        e = y_ref[...] - t_ref[...]
        dy_ref[...] = e * (1.0 / d)
        part = jnp.full((8, 128), 0.5 / d, F32) * jnp.sum(e * e)

        @pl.when(pl.program_id(0) == 0)
        def _():
            l_ref[...] = part

        @pl.when(pl.program_id(0) > 0)
        def _():
            l_ref[...] += part

    row = lambda i: (i, 0)
    return pl.pallas_call(
        body,
        name=name,
        grid=(t // tm,),
        in_specs=[pl.BlockSpec((tm, d), row), pl.BlockSpec((tm, d), row)],
        out_specs=[pl.BlockSpec((tm, d), row), pl.BlockSpec((8, 128), lambda i: (0, 0))],
        out_shape=[jax.ShapeDtypeStruct((t, d), F32), jax.ShapeDtypeStruct((8, 128), F32)],
        compiler_params=_params("arbitrary"),
    )(y, target)


FFN_COLS = FFN_HIDDEN // 2


def _ffn_in_swiglu(xb, wu, wg, name, tm=MM_ROWS):
    t, d = xb.shape
    tm = min(tm, t)

    def body(x_ref, wu_ref, wg_ref, a_ref, u_ref, g_ref):
        x = x_ref[...]
        u = _dot(x, wu_ref[...])
        g = _dot(x, wg_ref[...])
        u_ref[...] = u.astype(BF16)
        g_ref[...] = g.astype(BF16)
        a_ref[...] = (g * _sigmoid(g) * u).astype(BF16)

    wspec = pl.BlockSpec((d, FFN_COLS), lambda j, i: (0, j))
    out = pl.BlockSpec((tm, FFN_COLS), lambda j, i: (i, j))
    return pl.pallas_call(
        body,
        name=name,
        grid=(FFN_HIDDEN // FFN_COLS, t // tm),
        in_specs=[pl.BlockSpec((tm, d), lambda j, i: (i, 0)), wspec, wspec],
        out_specs=[out, out, out],
        out_shape=[jax.ShapeDtypeStruct((t, FFN_HIDDEN), BF16)] * 3,
        compiler_params=_params("parallel", "parallel"),
    )(xb, wu, wg)


def _ffn_out_dx_swiglu(dzb, w_ff_out, u, g, name, tm=MM_ROWS):
    t, d = dzb.shape
    tm = min(tm, t)

    def body(dz_ref, w_ref, u_ref, g_ref, du_ref, dg_ref):
        da = _dot(dz_ref[...], w_ref[...], "nt")
        gv = g_ref[...].astype(F32)
        sg = _sigmoid(gv)
        du_ref[...] = (da * gv * sg).astype(BF16)
        dg_ref[...] = (da * u_ref[...].astype(F32) * (sg * (1.0 + gv * (1.0 - sg)))).astype(BF16)

    blk = pl.BlockSpec((tm, FFN_COLS), lambda j, i: (i, j))
    return pl.pallas_call(
        body,
        name=name,
        grid=(FFN_HIDDEN // FFN_COLS, t // tm),
        in_specs=[pl.BlockSpec((tm, d), lambda j, i: (i, 0)), pl.BlockSpec((FFN_COLS, d), lambda j, i: (j, 0)), blk, blk],
        out_specs=[blk, blk],
        out_shape=[jax.ShapeDtypeStruct((t, FFN_HIDDEN), BF16)] * 2,
        compiler_params=_params("parallel", "parallel"),
    )(dzb, w_ff_out, u, g)


def _merge_fwd(ya, yb, wpa, wpb, gates, name, tm=512):
    t = ya.shape[0]
    tm = min(tm, t)

    def body(ya_ref, yb_ref, wa_ref, wb_ref, ga_ref, gb_ref, o_ref):
        pa = _dot(ya_ref[...], wa_ref[...])
        pb = _dot(yb_ref[...], wb_ref[...])
        o_ref[...] = (_sigmoid(ga_ref[...].astype(F32)) * pa + _sigmoid(gb_ref[...].astype(F32)) * pb).astype(BF16)

    row = lambda i: (i, 0)
    fix = lambda i: (0, 0)
    return pl.pallas_call(
        body,
        name=name,
        grid=(t // tm,),
        in_specs=[pl.BlockSpec((tm, A_WIDTH), row), pl.BlockSpec((tm, B_WIDTH), row),
                  pl.BlockSpec((A_WIDTH, D_MODEL), fix), pl.BlockSpec((B_WIDTH, D_MODEL), fix),
                  pl.BlockSpec((tm, D_MODEL), lambda i: (i, 0)), pl.BlockSpec((tm, D_MODEL), lambda i: (i, 1))],
        out_specs=pl.BlockSpec((tm, D_MODEL), row),
        out_shape=jax.ShapeDtypeStruct((t, D_MODEL), BF16),
        compiler_params=_params("parallel"),
    )(ya, yb, wpa, wpb, gates, gates)


def _merge_bwd(dzb, w_out, ya, yb, wpa, wpb, gates, name, tm=512):
    t = ya.shape[0]
    tm = min(tm, t)

    def body(dz_ref, wo_ref, ya_ref, yb_ref, wa_ref, wb_ref, ga_ref, gb_ref, dg_ref, dpa_ref, dpb_ref, dya_ref,
             dyb_ref):
        dm_v = _dot(dz_ref[...], wo_ref[...], "nt")
        pa = _dot(ya_ref[...], wa_ref[...])
        pb = _dot(yb_ref[...], wb_ref[...])
        sa = _sigmoid(ga_ref[...].astype(F32))
        sb = _sigmoid(gb_ref[...].astype(F32))
        dg_ref[:, :D_MODEL] = (dm_v * pa * sa * (1.0 - sa)).astype(BF16)
        dg_ref[:, D_MODEL:] = (dm_v * pb * sb * (1.0 - sb)).astype(BF16)
        dpa = (dm_v * sa).astype(BF16)
        dpb = (dm_v * sb).astype(BF16)
        dpa_ref[...] = dpa
        dpb_ref[...] = dpb
        dya_ref[...] = _dot(dpa, wa_ref[...], "nt").astype(BF16)
        dyb_ref[...] = _dot(dpb, wb_ref[...], "nt")

    row = lambda i: (i, 0)
    fix = lambda i: (0, 0)
    return pl.pallas_call(
        body,
        name=name,
        grid=(t // tm,),
        in_specs=[pl.BlockSpec((tm, D_MODEL), row), pl.BlockSpec((D_MODEL, D_MODEL), fix),
                  pl.BlockSpec((tm, A_WIDTH), row), pl.BlockSpec((tm, B_WIDTH), row),
                  pl.BlockSpec((A_WIDTH, D_MODEL), fix), pl.BlockSpec((B_WIDTH, D_MODEL), fix),
                  pl.BlockSpec((tm, D_MODEL), lambda i: (i, 0)), pl.BlockSpec((tm, D_MODEL), lambda i: (i, 1))],
        out_specs=[pl.BlockSpec((tm, 2 * D_MODEL), row), pl.BlockSpec((tm, D_MODEL), row),
                   pl.BlockSpec((tm, D_MODEL), row), pl.BlockSpec((tm, A_WIDTH), row), pl.BlockSpec((tm, B_WIDTH), row)],
        out_shape=[jax.ShapeDtypeStruct((t, 2 * D_MODEL), BF16), jax.ShapeDtypeStruct((t, D_MODEL), BF16),
                   jax.ShapeDtypeStruct((t, D_MODEL), BF16), jax.ShapeDtypeStruct((t, A_WIDTH), BF16),
                   jax.ShapeDtypeStruct((t, B_WIDTH), F32)],
        compiler_params=_params("parallel"),
    )(dzb, w_out, ya, yb, wpa, wpb, gates, gates)


FA_BLOCK = 4 * B_WIDTH // 128


def _tri(n, lower):
    r = lax.broadcasted_iota(jnp.int32, (n, n), 0)
    c = lax.broadcasted_iota(jnp.int32, (n, n), 1)
    return jnp.where((r >= c) if lower else (r <= c), 1.0, 0.0).astype(F32)


def _head_spread(expand):
    shape = (128, A_WIDTH) if expand else (A_WIDTH, 128)
    r = lax.broadcasted_iota(jnp.int32, shape, 0)
    c = lax.broadcasted_iota(jnp.int32, shape, 1)
    hit = ((c >= 64 * r) & (c < 64 * r + 64)) if expand else (r == 64 * c)
    return jnp.where(hit, 1.0, 0.0).astype(F32)


def _fox_gate_fwd(rest, bf, name, tb=512):
    t = rest.shape[0]
    tb = min(tb, t)

    def body(fa_ref, bf_ref, f_ref, fc_ref, carry):
        @pl.when(pl.program_id(0) == 0)
        def _():
            carry[...] = jnp.zeros_like(carry)

        z = fa_ref[...] + bf_ref[...]
        logf = jnp.minimum(z, 0.0) - jnp.log(1.0 + jnp.exp(-jnp.abs(z)))
        f = _dot_hi(_tri(tb, True), logf) + carry[...]
        f_ref[...] = f
        fc_ref[...] = _dot_hi(f, _head_spread(True), exact="b")
        carry[...] = f[tb - 1:tb, :]

    return pl.pallas_call(
        body,
        name=name,
        grid=(t // tb,),
        in_specs=[pl.BlockSpec((tb, 128), lambda i: (i, FA_BLOCK)), pl.BlockSpec((1, 128), lambda i: (0, 0))],
        out_specs=[pl.BlockSpec((tb, 128), lambda i: (i, 0)), pl.BlockSpec((tb, A_WIDTH), lambda i: (i, 0))],
        out_shape=[jax.ShapeDtypeStruct((t, 128), F32), jax.ShapeDtypeStruct((t, A_WIDTH), F32)],
        scratch_shapes=[pltpu.VMEM((1, 128), F32)],
        compiler_params=_params("arbitrary"),
    )(rest, bf)


def _fox_gate_bwd(rsum, csum, rest, bf, name, tb=512):
    t = rest.shape[0]
    tb = min(tb, t)
    nb = t // tb

    def body(rs_ref, cs_ref, fa_ref, bf_ref, dfa_ref, dbf_ref, carry):
        @pl.when(pl.program_id(0) == 0)
        def _():
            carry[...] = jnp.zeros_like(carry)

        d_f = _dot_hi(rs_ref[...] - cs_ref[...], _head_spread(False), exact="b")
        dlogf = _dot_hi(_tri(tb, False), d_f) + carry[...]
        carry[...] = dlogf[0:1, :]
        z = fa_ref[...] + bf_ref[...]
        dz = dlogf * _sigmoid(-z)
        dfa_ref[...] = dz.astype(BF16)
        part = jnp.sum(dz, axis=0, keepdims=True)

        @pl.when(pl.program_id(0) == 0)
        def _():
            dbf_ref[...] = part

        @pl.when(pl.program_id(0) > 0)
        def _():
            dbf_ref[...] += part

    return pl.pallas_call(
        body,
        name=name,
        grid=(nb,),
        in_specs=[pl.BlockSpec((tb, A_WIDTH), lambda i: (nb - 1 - i, 0)),
                  pl.BlockSpec((tb, A_WIDTH), lambda i: (nb - 1 - i, 0)),
                  pl.BlockSpec((tb, 128), lambda i: (nb - 1 - i, FA_BLOCK)),
                  pl.BlockSpec((1, 128), lambda i: (0, 0))],
        out_specs=[pl.BlockSpec((tb, 128), lambda i: (nb - 1 - i, 0)), pl.BlockSpec((1, 128), lambda i: (0, 0))],
        out_shape=[jax.ShapeDtypeStruct((t, 128), BF16), jax.ShapeDtypeStruct((1, 128), F32)],
        scratch_shapes=[pltpu.VMEM((1, 128), F32)],
        compiler_params=_params("arbitrary"),
    )(rsum, csum, rest, bf)


ATT_BLOCK = 512
FWD_Q_BLOCKS = 2
BWD_K_BLOCKS = 2


def _head_mask(shape, j):
    lane = lax.broadcasted_iota(jnp.int32, shape, 1)
    return (lane < 64) if j == 0 else (lane >= 64)


def _aug_lanes(tb, j):
    lane = lax.broadcasted_iota(jnp.int32, (tb, 128), 1)
    own = (lane < 64) if j == 0 else (lane >= 64)
    return own, lane - 64 * (1 - j)


def _aug_query(own, li, q, pieces):
    h, m, l = pieces
    one, zero = jnp.ones_like(h), jnp.zeros_like(h)
    spare = jnp.where(li == 0, h, jnp.where(li == 1, m, jnp.where(li == 2, l, jnp.where(li < 6, one, zero))))
    return jnp.where(own, q, spare)


def _fox_prep_fwd(qkv, fcol, name, tb=2048):
    t = qkv.shape[0]
    tb = min(tb, t)

    def body(q_ref, k_ref, v_ref, fc_ref, qa_ref, ka_ref, va_ref, qn_ref, kn_ref):
        pieces = _pieces(pltpu.roll(fc_ref[...], 64, 1))
        h, m, l = pieces
        q, k, v = q_ref[...], k_ref[...], v_ref[...]
        first = _head_mask((tb, 128), 0)
        for nrm_ref, x in ((qn_ref, q.astype(F32)), (kn_ref, k.astype(F32))):
            n0 = jnp.max(jnp.sum(jnp.where(first, x * x, 0.0), axis=1, keepdims=True))
            n1 = jnp.max(jnp.sum(jnp.where(first, 0.0, x * x), axis=1, keepdims=True))
            nrm_ref[...] = jnp.where(_head_mask((8, 128), 0), n0, n1)
        one, zero = jnp.ones_like(h), jnp.zeros_like(h)
        for j in (0, 1):
            own, li = _aug_lanes(tb, j)
            cols = slice(128 * j, 128 * (j + 1))
            qa_ref[:, cols] = _aug_query(own, li, q * 0.125, pieces)
            ks = jnp.where(li < 3, one, jnp.where(li == 3, -h, jnp.where(li == 4, -m, jnp.where(li == 5, -l, zero))))
            ka_ref[:, cols] = jnp.where(own, k, ks)
            va_ref[:, cols] = jnp.where(own, v, one)

    blk = pl.BlockSpec((tb, 256), lambda i, h: (i, h))
    nrm = pl.BlockSpec((None, None, 8, 128), lambda i, h: (i, h, 0, 0))
    return pl.pallas_call(
        body, name=name, grid=(t // tb, 4),
        in_specs=[pl.BlockSpec((tb, 128), lambda i, h: (i, h)), pl.BlockSpec((tb, 128), lambda i, h: (i, 4 + h)),
                  pl.BlockSpec((tb, 128), lambda i, h: (i, 8 + h)), pl.BlockSpec((tb, 128), lambda i, h: (i, h))],
        out_specs=[blk, blk, blk, nrm, nrm],
        out_shape=[jax.ShapeDtypeStruct((t, 2 * A_WIDTH), BF16)] * 3
        + [jax.ShapeDtypeStruct((t // tb, 4, 8, 128), F32)] * 2,
        compiler_params=_params("parallel", "parallel"),
    )(qkv, qkv, qkv, fcol)


def _fox_prep_bwd(qkv, fcol, lse, do, o, name, tb=2048):
    t = qkv.shape[0]
    tb = min(tb, t)

    def body(q_ref, fc_ref, lse_ref, do_ref, o_ref, qb_ref, dob_ref):
        pieces = _pieces(pltpu.roll(fc_ref[...] - lse_ref[...], 64, 1))
        q = q_ref[...] * 0.125
        do_v = do_ref[...]
        prod = do_v.astype(F32) * o_ref[...].astype(F32)
        for j in (0, 1):
            own, li = _aug_lanes(tb, j)
            cols = slice(128 * j, 128 * (j + 1))
            qb_ref[:, cols] = _aug_query(own, li, q, pieces)
            delta = jnp.sum(jnp.where(own, prod, 0.0), axis=1, keepdims=True)
            h, m, l = _pieces(jnp.broadcast_to(delta, (tb, 128)))
            ds = jnp.where(li == 0, -h, jnp.where(li == 1, -m, jnp.where(li == 2, -l, jnp.zeros_like(h))))
            dob_ref[:, cols] = jnp.where(own, do_v, ds)

    pair = pl.BlockSpec((tb, 128), lambda i, h: (i, h))
    blk = pl.BlockSpec((tb, 256), lambda i, h: (i, h))
    return pl.pallas_call(
        body, name=name, grid=(t // tb, 4),
        in_specs=[pair, pair, pair, pair, pair],
        out_specs=[blk, blk],
        out_shape=[jax.ShapeDtypeStruct((t, 2 * A_WIDTH), BF16)] * 2,
        compiler_params=_params("parallel", "parallel"),
    )(qkv, fcol, lse, do, o)


def _tile_mask(n, transposed):
    r = lax.broadcasted_iota(jnp.int32, (n, n), 0)
    c = lax.broadcasted_iota(jnp.int32, (n, n), 1)
    return (c >= r) if transposed else (r >= c)


UNDERFLOW = -110.0


def _fox_block_ranges(qn, kn, fcum):
    t = fcum.shape[0]
    blk = min(ATT_BLOCK, t)
    nb = t // blk
    q2 = jnp.max(qn[:, :, 0, ::64].reshape(-1, A_HEADS), axis=0)
    k2 = jnp.max(kn[:, :, 0, ::64].reshape(-1, A_HEADS), axis=0)
    bound = 2.0 * jnp.sqrt(q2 * k2) * 0.125
    f = fcum[:, :A_HEADS]
    first = f[0::blk].T
    last = f[blk - 1::blk].T
    dead = (bound[:, None, None] + first[:, :, None] - last[:, None, :]) < UNDERFLOW
    qi = jnp.arange(nb)[None, :, None]
    kj = jnp.arange(nb)[None, None, :]
    dead = dead & (kj < qi)
    kstart = jnp.sum(dead, axis=2).astype(jnp.int32)
    qend = (kj[0] + jnp.sum((~dead) & (qi > kj), axis=1)).astype(jnp.int32)
    return kstart.reshape(-1), qend.reshape(-1)


def _fox_fwd(qa, ka, va, kstart, name):
    t = qa.shape[0]
    bk = min(ATT_BLOCK, t)
    nk = t // bk
    qf = FWD_Q_BLOCKS if t % (FWD_Q_BLOCKS * bk) == 0 else 1
    bq = qf * bk
    nq = t // bq

    def body(ks_ref, q_ref, k_ref, v_ref, o_ref, lse_ref):
        i = pl.program_id(1)
        hp = pl.program_id(0)
        k0 = [ks_ref[(2 * hp + j) * nk + qf * i] for j in (0, 1)]
        both0 = jnp.maximum(k0[0], k0[1])

        def head(j, kb, m, acc, diag):
            rows = pl.ds(pl.multiple_of(kb * bk, bk), bk)
            cols = slice(128 * j, 128 * (j + 1))
            s = _dot(q_ref[:, cols], k_ref[rows, cols], "nt")
            if diag is not None:
                r = lax.broadcasted_iota(jnp.int32, (bq, bk), 0)
                c = lax.broadcasted_iota(jnp.int32, (bq, bk), 1)
                s = jnp.where(r - c >= diag, s, -jnp.inf)
            m_new = jnp.maximum(m, jnp.max(s, axis=1, keepdims=True))
            return m_new, jnp.exp(m - m_new) * acc + _dot(jnp.exp(s - m_new), v_ref[rows, cols])

        def pair(kb, carry, diag):
            return head(0, kb, carry[0], carry[1], diag) + head(1, kb, carry[2], carry[3], diag)

        init = (jnp.full((bq, 1), -jnp.inf, F32), jnp.zeros((bq, 128), F32))
        alone = [lax.fori_loop(k0[j], both0, lambda kb, c, j=j: head(j, kb, c[0], c[1], None), init) for j in (0, 1)]
        carry = lax.fori_loop(both0, qf * i, lambda kb, c: pair(kb, c, None), alone[0] + alone[1])
        for d in range(qf):
            carry = pair(qf * i + d, carry, d * bk)
        outs = []
        for j in (0, 1):
            m, acc = carry[2 * j], carry[2 * j + 1]
            spare = 64 * (1 - j)
            l = acc[:, spare:spare + 1]
            outs.append((acc / l, m + jnp.log(l)))
        msk = _head_mask((bq, 128), 0)
        o_ref[...] = jnp.where(msk, outs[0][0], outs[1][0]).astype(BF16)
        lse_ref[...] = jnp.where(msk, outs[0][1], outs[1][1])

    res = pl.BlockSpec((t, 256), lambda h, i, tbl: (0, h))
    out = pl.BlockSpec((bq, 128), lambda h, i, tbl: (i, h))
    return pl.pallas_call(
        body,
        name=name,
        grid_spec=pltpu.PrefetchScalarGridSpec(
            num_scalar_prefetch=1, grid=(4, nq),
            in_specs=[pl.BlockSpec((bq, 256), lambda h, i, tbl: (i, h)), res, res],
            out_specs=[out, out]),
        out_shape=[jax.ShapeDtypeStruct((t, A_WIDTH), BF16), jax.ShapeDtypeStruct((t, A_WIDTH), F32)],
        compiler_params=_params("parallel", "parallel"),
    )(kstart, qa, ka, va)


def _fox_bwd(qb, ka, va, dob, qend, name):
    t = qb.shape[0]
    bq = min(ATT_BLOCK, t)
    nq = t // bq
    kf = BWD_K_BLOCKS if t % (BWD_K_BLOCKS * bq) == 0 else 1
    bk = kf * bq
    nk = t // bk

    def body(qe_ref, k_ref, v_ref, q_hbm, do_hbm, dk_ref, dv_ref, cs_ref, dq_hbm, rs_hbm, q_scr, do_scr, dq_scr,
             sems):
        jb = pl.program_id(1)
        hp = pl.program_id(0)
        pair_cols = pl.ds(pl.multiple_of(hp * 256, 256), 256)

        @pl.when(jb == 0)
        def _():
            loads = [pltpu.make_async_copy(q_hbm.at[:, pair_cols], q_scr, sems.at[0]),
                     pltpu.make_async_copy(do_hbm.at[:, pair_cols], do_scr, sems.at[1])]
            for cp in loads:
                cp.start()
            dq_scr[...] = jnp.zeros_like(dq_scr)
            for cp in loads:
                cp.wait()

        i1 = [qe_ref[(2 * hp + j) * nq + kf * jb + kf - 1] + 1 for j in (0, 1)]
        both1 = jnp.minimum(i1[0], i1[1])

        def head(j, ib, dk_acc, dv_acc, diag):
            rows = pl.ds(pl.multiple_of(ib * bq, bq), bq)
            cols = slice(128 * j, 128 * (j + 1))
            qs = q_scr[rows, cols]
            dos = do_scr[rows, cols]
            kj = k_ref[:, cols]
            st = _dot(kj, qs, "nt")
            if diag is not None:
                r = lax.broadcasted_iota(jnp.int32, (bk, bq), 0)
                c = lax.broadcasted_iota(jnp.int32, (bk, bq), 1)
                st = jnp.where(c - r >= -diag, st, -jnp.inf)
            pt = jnp.exp(st)
            dst = (pt * _dot(v_ref[:, cols], dos, "nt")).astype(BF16)
            dq_scr[rows, cols] += _dot(dst, kj, "tn")
            return dk_acc + _dot(dst, qs), dv_acc + _dot(pt, dos)

        def pair(ib, carry, diag):
            return head(0, ib, carry[0], carry[1], diag) + head(1, ib, carry[2], carry[3], diag)

        carry = (jnp.zeros((bk, 128), F32),) * 4
        for d in range(kf):
            carry = pair(kf * jb + d, carry, d * bq)
        first = kf * jb + kf
        carry = lax.fori_loop(first, both1, lambda ib, c: pair(ib, c, None), carry)
        alone = [lax.fori_loop(jnp.maximum(both1, first), i1[j],
                               lambda ib, c, j=j: head(j, ib, c[0], c[1], None), carry[2 * j:2 * j + 2])
                 for j in (0, 1)]
        carry = alone[0] + alone[1]
        outs = []
        for j in (0, 1):
            spare = 64 * (1 - j)
            dk_acc, dv_acc = carry[2 * j], carry[2 * j + 1]
            outs.append((dk_acc, dv_acc, dk_acc[:, spare + 3:spare + 4]))
        msk = _head_mask((bk, 128), 0)
        dk_ref[...] = jnp.where(msk, outs[0][0], outs[1][0]).astype(BF16)
        dv_ref[...] = jnp.where(msk, outs[0][1], outs[1][1]).astype(BF16)
        cs_ref[...] = jnp.where(msk, outs[0][2], outs[1][2])

        @pl.when(jb == nk - 1)
        def _():
            first_head = _head_mask((bq, 128), 0)

            def finish(r, carry):
                rows = pl.ds(pl.multiple_of(r * bq, bq), bq)
                x0, x1 = dq_scr[rows, 0:128], dq_scr[rows, 128:256]
                q_scr[rows, 0:128] = (jnp.where(first_head, x0, x1) * 0.125).astype(BF16)
                dq_scr[rows, 0:128] = jnp.where(first_head, x0[:, 64:65], x1[:, 0:1])
                return carry

            lax.fori_loop(0, nq, finish, 0)
            head_cols = pl.ds(pl.multiple_of(hp * 128, 128), 128)
            stores = [pltpu.make_async_copy(q_scr.at[:, 0:128], dq_hbm.at[:, head_cols], sems.at[0]),
                      pltpu.make_async_copy(dq_scr.at[:, 0:128], rs_hbm.at[:, head_cols], sems.at[1])]
            for cp in stores:
                cp.start()
            for cp in stores:
                cp.wait()

    blk = pl.BlockSpec((bk, 256), lambda h, i, tbl: (i, h))
    out = pl.BlockSpec((bk, 128), lambda h, i, tbl: (i, h))
    return pl.pallas_call(
        body,
        name=name,
        grid_spec=pltpu.PrefetchScalarGridSpec(
            num_scalar_prefetch=1, grid=(4, nk), in_specs=[blk, blk, _ANY, _ANY],
            out_specs=[out, out, out, _ANY, _ANY],
            scratch_shapes=[pltpu.VMEM((t, 256), BF16), pltpu.VMEM((t, 256), BF16), pltpu.VMEM((t, 256), F32),
                            pltpu.SemaphoreType.DMA((2,))]),
        out_shape=[jax.ShapeDtypeStruct((t, A_WIDTH), BF16), jax.ShapeDtypeStruct((t, A_WIDTH), BF16),
                   jax.ShapeDtypeStruct((t, A_WIDTH), F32), jax.ShapeDtypeStruct((t, A_WIDTH), BF16),
                   jax.ShapeDtypeStruct((t, A_WIDTH), F32)],
        compiler_params=_params("arbitrary", "arbitrary"),
    )(qend, ka, va, qb, dob)


HG_ROWS = 256


def _hg_gates(hb_ref, rows, lbv):
    qb = hb_ref[rows, 0:B_WIDTH]
    fb = hb_ref[rows, B_WIDTH:2 * B_WIDTH]
    v = hb_ref[rows, 2 * B_WIDTH:3 * B_WIDTH]
    gb = hb_ref[rows, 3 * B_WIDTH:4 * B_WIDTH]
    sg = _sigmoid(fb)
    f = lbv + (1.0 - lbv) * sg
    sq = _sigmoid(qb)
    return qb, sq, qb * sq, sg, f, 1.0 - f, jnp.log(f), v, gb


def _hg_intra_factors(q, k, b):
    fac = []
    for i in range(CHUNK // SUB):
        bi = b[SUB * i:SUB * i + 1, :]
        eq = jnp.exp(b[SUB * i:SUB * (i + 1), :] - bi)
        ek = jnp.exp(jnp.minimum(bi - b, EXP_CLAMP))
        fac.append((eq, ek, q[SUB * i:SUB * (i + 1), :] * eq, k * ek))
    return fac


def _causal(n):
    r = lax.broadcasted_iota(jnp.int32, (n, n), 0)
    c = lax.broadcasted_iota(jnp.int32, (n, n), 1)
    return r >= c


def _hgrn_fwd(rest, lb, ng, name, ride=()):
    t = rest.shape[0]
    bt = min(HG_ROWS, t)
    ncb = bt // CHUNK
    n = len(ride)
    nsteps = t // bt

    def body(hb_ref, lb_ref, ng_ref, *refs):
        ride_in, (y_ref, o_ref, st_ref), ride_out = refs[:n], refs[n:n + 3], refs[n + 3:2 * n + 3]
        s_scr, sems = refs[2 * n + 3], refs[2 * n + 4:]

        @pl.when(pl.program_id(0) == 0)
        def _():
            s_scr[...] = jnp.zeros_like(s_scr)
            if n:
                _gather_start(ride_in, ride_out, sems)

        tril = _tri(CHUNK, True)
        causal = _causal(CHUNK)
        ones = jnp.ones((CHUNK, HD), F32)

        def chunk(c, carry):
            rows = pl.ds(pl.multiple_of(c * CHUNK, CHUNK), CHUNK)
            _, _, q_all, _, _, k_all, g_all, v_all, gb_all = _hg_gates(hb_ref, rows, lb_ref[...])
            b_all = _dot_hi(tril, g_all)
            qd_all = q_all * jnp.exp(b_all)
            kd_all = k_all * jnp.exp(b_all[CHUNK - 1:CHUNK, :] - b_all)
            eb_all = jnp.exp(_dot_hi(g_all, ones, "tn", exact="b"))
            sgb_all = _sigmoid(gb_all)
            for h in range(B_HEADS):
                cols = slice(h * HD, (h + 1) * HD)
                v = v_all[:, cols]
                s0 = s_scr[h]
                st_ref[c, h] = s0
                o = _dot(qd_all[:, cols], s0)
                fac = _hg_intra_factors(q_all[:, cols], k_all[:, cols], b_all[:, cols])
                a = jnp.concatenate([_dot(qe, ke, "nt") for _, _, qe, ke in fac], axis=0)
                o = o + _dot(jnp.where(causal, a, 0.0), v)
                s_scr[h] = eb_all[h * HD:(h + 1) * HD, :] * s0 + _dot(kd_all[:, cols], v, "tn")
                r = lax.rsqrt(jnp.mean(o * o, axis=-1, keepdims=True) + RMS_EPS)
                o_ref[rows, cols] = o
                y_ref[rows, cols] = (o * r * ng_ref[...] * sgb_all[:, cols]).astype(BF16)
            return carry

        lax.fori_loop(0, ncb, chunk, 0, unroll=True)

        if n:
            @pl.when(pl.program_id(0) == nsteps - 1)
            def _():
                _gather_finish(ride_in, ride_out, sems)

    res = pl.pallas_call(
        body,
        name=name,
        grid=(nsteps,),
        in_specs=[pl.BlockSpec((bt, 4 * B_WIDTH), lambda i: (i, 0)), pl.BlockSpec((1, B_WIDTH), lambda i: (0, 0)),
                  pl.BlockSpec((1, HD), lambda i: (0, 0))] + [_ANY] * n,
        out_specs=[pl.BlockSpec((bt, B_WIDTH), lambda i: (i, 0)), pl.BlockSpec((bt, B_WIDTH), lambda i: (i, 0)),
                   pl.BlockSpec((ncb, B_HEADS, HD, HD), lambda i: (i, 0, 0, 0))] + [_ANY] * n,
        out_shape=[jax.ShapeDtypeStruct((t, B_WIDTH), BF16), jax.ShapeDtypeStruct((t, B_WIDTH), F32),
                   jax.ShapeDtypeStruct((t // CHUNK, B_HEADS, HD, HD), F32)] + _gather_out_shapes(ride),
        scratch_shapes=[pltpu.VMEM((B_HEADS, HD, HD), F32)] + (_gather_scratch(n) if n else []),
        compiler_params=_params("arbitrary"),
    )(rest, lb, ng, *ride)
    return res[:3], res[3:]


def _hgrn_bwd(dy, rest, o_saved, states, lb, ng, name):
    t = rest.shape[0]
    bt = min(HG_ROWS, t)
    ncb = bt // CHUNK
    nb = t // bt

    def body(dy_ref, hb_ref, o_ref, st_ref, lb_ref, ng_ref, dh_ref, dlb_ref, dng_ref, ds_scr):
        @pl.when(pl.program_id(0) == 0)
        def _():
            ds_scr[...] = jnp.zeros_like(ds_scr)
            dlb_ref[...] = jnp.zeros_like(dlb_ref)
            dng_ref[...] = jnp.zeros_like(dng_ref)

        tril = _tri(CHUNK, True)
        triu = _tri(CHUNK, False)
        causal = _causal(CHUNK)
        ones = jnp.ones((CHUNK, HD), F32)
        ones8 = jnp.ones((8, HD), F32)
        last_row = lax.broadcasted_iota(jnp.int32, (CHUNK, B_WIDTH), 0) == CHUNK - 1

        def chunk(cc, carry):
            dng_acc, dlb_acc = carry
            c = ncb - 1 - cc
            rows = pl.ds(pl.multiple_of(c * CHUNK, CHUNK), CHUNK)
            lbv = lb_ref[...]
            qb, sq, q_all, sg, f, k_all, g_all, v_all, gb = _hg_gates(hb_ref, rows, lbv)
            b_all = _dot_hi(tril, g_all)
            ebt_all = jnp.exp(b_all)
            blast = b_all[CHUNK - 1:CHUNK, :]
            ekd_all = jnp.exp(blast - b_all)
            eb_all = jnp.exp(_dot_hi(g_all, ones, "tn", exact="b"))
            sgb = _sigmoid(gb)
            dy_all = dy_ref[rows, :].astype(F32)
            don_all = dy_all * sgb
            ngv = ng_ref[...]
            dq_l, dk_l, dks_l, dv_l, on_l, prod_l = [], [], [], [], [], []
            for h in range(B_HEADS):
                cols = slice(h * HD, (h + 1) * HD)
                q, k, v = q_all[:, cols], k_all[:, cols], v_all[:, cols]
                o = o_ref[rows, cols]
                don = don_all[:, cols]
                r = lax.rsqrt(jnp.mean(o * o, axis=-1, keepdims=True) + RMS_EPS)
                on_l.append(o * r * ngv)
                dng_acc = dng_acc + jnp.sum(don * o * r, axis=0, keepdims=True)
                doh = don * ngv
                do = r * (doh - o * (r * r) * jnp.mean(doh * o, axis=-1, keepdims=True))
                ebt, ekd = ebt_all[:, cols], ekd_all[:, cols]
                s0 = st_ref[c, h]
                ds1 = ds_scr[h]
                fac = _hg_intra_factors(q, k, b_all[:, cols])
                a = jnp.concatenate([_dot(qe, ke, "nt") for _, _, qe, ke in fac], axis=0)
                a = jnp.where(causal, a, 0.0)
                da = jnp.where(causal, _dot(do, v, "nt"), 0.0)
                dv_l.append(_dot(a, do, "tn") + _dot(k * ekd, ds1))
                dq = ebt * _dot(do, s0, "nt")
                dq_l.append(dq + jnp.concatenate(
                    [eq * _hdot(da[SUB * i:SUB * (i + 1), :], ke) for i, (eq, _, _, ke) in enumerate(fac)], axis=0))
                dk_state = ekd * _dot(v, ds1, "nt")
                dk = dk_state
                for i, (_, ek, qe, _) in enumerate(fac):
                    dk = dk + ek * _hdot(da[SUB * i:SUB * (i + 1), :], qe, "tn")
                dk_l.append(dk)
                dks_l.append(dk_state)
                prod_l.append(ds1 * s0)
                ds_scr[h] = _dot(q * ebt, do, "tn") + eb_all[h * HD:(h + 1) * HD, :] * ds1
            dq_all, dk_all = jnp.concatenate(dq_l, axis=1), jnp.concatenate(dk_l, axis=1)
            extra = jnp.exp(blast) * _dot_hi(ones8, jnp.concatenate(prod_l, axis=0), "nt")[0:1, :] \
                + jnp.sum(k_all * jnp.concatenate(dks_l, axis=1), axis=0, keepdims=True)
            db = q_all * dq_all - k_all * dk_all + jnp.where(last_row, extra, 0.0)
            df = _dot_hi(triu, db) / f - dk_all
            dlb_acc = dlb_acc + jnp.sum(df * (1.0 - sg), axis=0, keepdims=True)
            dh_ref[rows, 0:B_WIDTH] = (dq_all * (sq * (1.0 + qb * (1.0 - sq)))).astype(BF16)
            dh_ref[rows, B_WIDTH:2 * B_WIDTH] = (df * (1.0 - lbv) * sg * (1.0 - sg)).astype(BF16)
            dh_ref[rows, 2 * B_WIDTH:3 * B_WIDTH] = jnp.concatenate(dv_l, axis=1).astype(BF16)
            dh_ref[rows, 3 * B_WIDTH:4 * B_WIDTH] = (dy_all * jnp.concatenate(on_l, axis=1)
                                                     * sgb * (1.0 - sgb)).astype(BF16)
            return dng_acc, dlb_acc

        dng_sum, dlb_sum = lax.fori_loop(0, ncb, chunk, (jnp.zeros((1, HD), F32), jnp.zeros((1, B_WIDTH), F32)))
        dng_ref[...] += dng_sum
        dlb_ref[...] += dlb_sum

    rev = lambda i: (nb - 1 - i, 0)
    return pl.pallas_call(
        body,
        name=name,
        grid=(nb,),
        in_specs=[pl.BlockSpec((bt, B_WIDTH), rev), pl.BlockSpec((bt, 4 * B_WIDTH), rev),
                  pl.BlockSpec((bt, B_WIDTH), rev),
                  pl.BlockSpec((ncb, B_HEADS, HD, HD), lambda i: (nb - 1 - i, 0, 0, 0)),
                  pl.BlockSpec((1, B_WIDTH), lambda i: (0, 0)), pl.BlockSpec((1, HD), lambda i: (0, 0))],
        out_specs=[pl.BlockSpec((bt, 4 * B_WIDTH), rev), pl.BlockSpec((1, B_WIDTH), lambda i: (0, 0)),
                   pl.BlockSpec((1, HD), lambda i: (0, 0))],
        out_shape=[jax.ShapeDtypeStruct((t, 4 * B_WIDTH), BF16), jax.ShapeDtypeStruct((1, B_WIDTH), F32),
                   jax.ShapeDtypeStruct((1, HD), F32)],
        scratch_shapes=[pltpu.VMEM((B_HEADS, HD, HD), F32)],
        compiler_params=_params("arbitrary"),
    )(dy, rest, o_saved, states, lb, ng)


def _split_w_in(w_in_l):
    wqkv = w_in_l[:, :3 * A_WIDTH]
    wfa = jnp.pad(w_in_l[:, 3 * A_WIDTH:3 * A_WIDTH + A_HEADS], ((0, 0), (0, 128 - A_HEADS)))
    whb = w_in_l[:, 3 * A_WIDTH + A_HEADS:3 * A_WIDTH + A_HEADS + 4 * B_WIDTH]
    wgt = w_in_l[:, 3 * A_WIDTH + A_HEADS + 4 * B_WIDTH:]
    return wqkv, wgt, jnp.concatenate([whb, wfa], axis=1), (jnp.concatenate([wqkv, wfa], axis=1), wgt, whb)


def _merge_w_in_grad(d_att, d_gates, d_hb):
    o = 3 * A_WIDTH
    return jnp.concatenate([d_att[:, :o + A_HEADS], d_hb, d_gates], axis=1)


def _layer_fwd(x, xb, w, sp, l, ride=(), late_weights=None):
    n = f"l{l}_"
    wqkv, wgates, wrest, wgroups = _split_w_in(w["w_in"])
    qkv = _matmul(xb, wqkv, "nn", BF16, MM_ROWS, 768, D_MODEL, n + "proj_qkv")
    gates = _matmul(xb, wgates, "nn", BF16, MM_ROWS, 1024, D_MODEL, n + "proj_gates")
    rest = _matmul(xb, wrest, "nn", F32, MM_ROWS, 4 * B_WIDTH + 128, D_MODEL, n + "proj_rest")
    bf = jnp.pad(sp["b_fgate"], (0, 128 - A_HEADS)).reshape(1, 128)
    fcum, fcol = _fox_gate_fwd(rest, bf, n + "fox_gate_fwd")
    qa, ka, va, qn, kn = _fox_prep_fwd(qkv, fcol, n + "fox_prep_fwd")
    kstart, qend = _fox_block_ranges(qn, kn, fcum)
    ya, lse = _fox_fwd(qa, ka, va, kstart, n + "fox_fwd")
    lb = sp["lb"].reshape(1, B_WIDTH)
    ng = sp["norm_g"].reshape(1, HD)
    (yb, ob, states), gathered = _hgrn_fwd(rest, lb, ng, n + "hgrn_fwd", ride)
    if ride:
        late = late_weights(l, gathered)
        w = {**w, **late[l]}
    merged = _merge_fwd(ya, yb, w["w_pa"], w["w_pb"], gates, n + "merge_fwd")
    x1, x1b, xh1, rs1 = _mm_res_ln(merged, w["w_out"], x, sp["ln1_g"], sp["ln1_b"], n + "out_ln1")
    wu, wg = w["w_ff_in"][:, :FFN_HIDDEN], w["w_ff_in"][:, FFN_HIDDEN:]
    a, hu, hg = _ffn_in_swiglu(x1b, wu, wg, n + "ffn_in_swiglu")
    x2, x2b, xh2, rs2 = _mm_res_ln(a, w["w_ff_out"], x1, sp["ln2_g"], sp["ln2_b"], n + "ffn_out_ln2")
    saved = dict(xb=xb, wgroups=wgroups, qkv=qkv, rest=rest, gates=gates, bf=bf, fcol=fcol, ka=ka, va=va, ya=ya, lse=lse,
                 qend=qend,
                 lb=lb, ng=ng, yb=yb, ob=ob, states=states, merged=merged, x1b=x1b, xh1=xh1, rs1=rs1, a=a,
                 wu=wu, wg=wg, hu=hu, hg=hg,
                 xh2=xh2, rs2=rs2)
    return x2, x2b, saved, (late if ride else None)


def _layer_bwd(dys, coefs, w, sp, s, l):
    n = f"l{l}_"
    dz2, dz2b, dg2, db2 = _ln_bwd(dys, coefs, s["xh2"], s["rs2"], sp["ln2_g"], n + "ln2_bwd")
    du, dg = _ffn_out_dx_swiglu(dz2b, w["w_ff_out"], s["hu"], s["hg"], n + "ffn_out_dx_swiglu")
    d_wffout = _matmul(s["a"], dz2b, "tn", F32, 1408, 1024, DW_ROWS, n + "ffn_out_dw")
    dx1 = _matmul_nt_sum([du, dg], [s["wu"], s["wg"]], n + "ffn_in_dx")
    d_wffin = jnp.concatenate([_matmul(s["x1b"], du, "tn", F32, 1024, 1408, DW_ROWS, n + "ffn_in_dw_u"),
                               _matmul(s["x1b"], dg, "tn", F32, 1024, 1408, DW_ROWS, n + "ffn_in_dw_g")], axis=1)
    dz1, dz1b, dg1, db1 = _ln_bwd([dz2, dx1], [ALPHA, 1.0], s["xh1"], s["rs1"], sp["ln1_g"], n + "ln1_bwd")
    d_wout = _matmul(s["merged"], dz1b, "tn", F32, 1024, 1024, DW_ROWS, n + "out_dw")
    dgates, dpa, dpb, dya, dyb = _merge_bwd(dz1b, w["w_out"], s["ya"], s["yb"], w["w_pa"], w["w_pb"], s["gates"],
                                  n + "merge_bwd")
    d_wpa = _matmul(s["ya"], dpa, "tn", F32, 512, 1024, DW_ROWS, n + "pa_dw")
    d_wpb = _matmul(s["yb"], dpb, "tn", F32, 512, 1024, DW_ROWS, n + "pb_dw")
    qb, dob = _fox_prep_bwd(s["qkv"], s["fcol"], s["lse"], dya, s["ya"], n + "fox_prep_bwd")
    dk, dv, csum, dq, rsum = _fox_bwd(qb, s["ka"], s["va"], dob, s["qend"], n + "fox_bwd")
    dfa, dbf = _fox_gate_bwd(rsum, csum, s["rest"], s["bf"], n + "fox_gate_bwd")
    dhb, dlb, dng = _hgrn_bwd(dyb, s["rest"], s["ob"], s["states"], s["lb"], s["ng"], n + "hgrn_bwd")
    datt = jnp.concatenate([dq, dk, dv, dfa], axis=1)
    dxp = _matmul_nt_sum([datt, dgates, dhb], list(s["wgroups"]), n + "proj_dx", (ALPHA, dz1) if l == 0 else None)
    d_w_in = _merge_w_in_grad(_matmul(s["xb"], datt, "tn", F32, 1024, 1664, DW_ROWS, n + "proj_dw_att"),
                              _matmul(s["xb"], dgates, "tn", F32, 1024, 1024, DW_ROWS, n + "proj_dw_gates"),
                              _matmul(s["xb"], dhb, "tn", F32, 1024, 1024, DW_ROWS, n + "proj_dw_hgrn"))
    grads = dict(w_in=d_w_in, w_pa=d_wpa, w_pb=d_wpb, w_out=d_wout, w_ff_in=d_wffin,
                 w_ff_out=d_wffout, b_fgate=dbf[0, :A_HEADS], lb=dlb[0], norm_g=dng[0], ln1_g=dg1[0], ln1_b=db1[0],
                 ln2_g=dg2[0], ln2_b=db2[0])
    if l == 0:
        return [dxp], [1.0], grads
    return [dz1, dxp], [ALPHA, 1.0], grads


def _lower_bounds(logits):
    sm = jax.nn.softmax(logits.astype(F32), axis=0)
    return jnp.cumsum(sm, axis=0) - sm[0:1]


def _local_step(x, target, wfull, small, rides=None, late_weights=None):
    lbs, lb_vjp = jax.vjp(_lower_bounds, small["hgrn_lb_logits"])
    h, hb = x, x.astype(BF16)
    wfull = list(wfull)
    saved, sps = [], []
    for l in range(DEPTH):
        sp = dict(b_fgate=small["b_fgate"][l], lb=lbs[l], norm_g=small["hgrn_norm_g"][l], ln1_g=small["ln1_g"][l],
                  ln1_b=small["ln1_b"][l], ln2_g=small["ln2_g"][l], ln2_b=small["ln2_b"][l])
        h, hb, s, late = _layer_fwd(h, hb, wfull[l], sp, l, rides[l] if rides else (), late_weights)
        if late is not None:
            wfull = [{**wfull[k], **late[k]} for k in range(DEPTH)]
        saved.append(s)
        sps.append(sp)
    dy, lpart = _loss_head(h, target)
    dys, coefs = [dy], [1.0]
    grads = [None] * DEPTH
    for l in reversed(range(DEPTH)):
        dys, coefs, grads[l] = _layer_bwd(dys, coefs, wfull[l], sps[l], saved[l], l)
    grad_x = dys[0]
    d_logits = lb_vjp(jnp.stack([grads[l]["lb"] for l in range(DEPTH)]))[0]
    return lpart[0, 0], grad_x, grads, d_logits


_BIG = [("w_in", "w_in", (D_MODEL, IN_TOTAL), 1), ("w_branch_a", "w_pa", (A_WIDTH, D_MODEL), 1),
        ("w_branch_b", "w_pb", (B_WIDTH, D_MODEL), 1), ("w_out", "w_out", (D_MODEL, D_MODEL), 0),
        ("w_ff_in", "w_ff_in", (D_MODEL, 2 * FFN_HIDDEN), 1), ("w_ff_out", "w_ff_out", (FFN_HIDDEN, D_MODEL), 0)]
_SMALL = [("b_fgate", A_HEADS), ("hgrn_lb_logits", B_WIDTH), ("hgrn_norm_g", HD), ("ln1_g", D_MODEL),
          ("ln1_b", D_MODEL), ("ln2_g", D_MODEL), ("ln2_b", D_MODEL)]
N_BIG = len(_BIG)
SMALL_ROWS = 80


def _by_chip(full, axis):
    if axis == 0:
        return full.reshape(N_CHIPS, full.shape[0] // N_CHIPS, full.shape[1])
    n = full.shape[1] // N_CHIPS
    return jnp.stack([full[:, q * n:(q + 1) * n] for q in range(N_CHIPS)])


def _from_chips(shards, axis):
    if axis == 0:
        return shards.reshape(N_CHIPS * shards.shape[1], shards.shape[2])
    return jnp.concatenate([shards[q] for q in range(N_CHIPS)], axis=1)


def _pack_small(per_name):
    flat = jnp.concatenate([per_name[name].reshape(-1) for name, _ in _SMALL])
    return jnp.pad(flat, (0, SMALL_ROWS * 128 - flat.shape[0])).reshape(SMALL_ROWS, 128)


def _unpack_small(slab):
    flat, out, r = slab.reshape(-1), {}, 0
    for name, n in _SMALL:
        out[name] = flat[r:r + DEPTH * n].reshape(DEPTH, n)
        r += DEPTH * n
    return out


_ANY = pl.BlockSpec(memory_space=pl.ANY)


def _place():
    return lax.axis_index("x"), lax.axis_index("y"), lax.axis_index("c")


def _other_chips(x, y):
    return [(1 - x, y), (x, 1 - y), (1 - x, 1 - y)]


def _chip_exchange(mine_of, out_refs, send_sems, recv_sems, local_sems):
    _chip_exchange_start(mine_of, out_refs, send_sems, recv_sems, local_sems)
    _chip_exchange_wait(mine_of, out_refs, send_sems, recv_sems, local_sems)


def _chip_exchange_copies(mine_of, out_refs, send_sems, recv_sems, local_sems):
    x, y, c = _place()
    q = 2 * x + y
    local = [pltpu.make_async_copy(mine_of(w, q), out_ref.at[q], local_sems.at[w]) for w, out_ref in enumerate(out_refs)]
    sends, recvs = [], []
    for k, (px, py) in enumerate(_other_chips(x, y)):
        for w, out_ref in enumerate(out_refs):
            sems = dict(send_sem=send_sems.at[3 * w + k], recv_sem=recv_sems.at[3 * w + k], device_id=(px, py, c),
                        device_id_type=MESH)
            sends.append(pltpu.make_async_remote_copy(src_ref=mine_of(w, 2 * px + py), dst_ref=out_ref.at[q], **sems))
            recvs.append(pltpu.make_async_remote_copy(src_ref=mine_of(w, q), dst_ref=out_ref.at[2 * px + py], **sems))
    return local, sends, recvs


def _chip_exchange_start(*args):
    local, sends, _ = _chip_exchange_copies(*args)
    for cp in local + sends:
        cp.start()


def _chip_exchange_wait(*args):
    local, sends, recvs = _chip_exchange_copies(*args)
    for cp in recvs:
        cp.wait_recv()
    for cp in sends:
        cp.wait_send()
    for cp in local:
        cp.wait()


def _sem_scratch(n):
    return [pltpu.SemaphoreType.DMA((3 * n,)), pltpu.SemaphoreType.DMA((3 * n,)), pltpu.SemaphoreType.DMA((n,))]


def _gather_scratch(n):
    return _sem_scratch(n) + [pltpu.SemaphoreType.DMA((n,)), pltpu.SemaphoreType.DMA((n,))]


def _gather_out_shapes(mine):
    return [jax.ShapeDtypeStruct((DEPTH, N_CHIPS) + m.shape[1:], m.dtype) for m in mine]


def _gather_start(in_refs, out_refs, sems):
    c = lax.axis_index("c")
    _chip_exchange_start(lambda w, q: in_refs[w].at[c], [o.at[c] for o in out_refs], *sems[:3])


def _gather_finish(in_refs, out_refs, sems):
    x, y, c = _place()
    _chip_exchange_wait(lambda w, q: in_refs[w].at[c], [o.at[c] for o in out_refs], *sems[:3])
    pair_send, pair_recv = sems[3:]
    sibling = (x, y, 1 - c)
    fwds = []
    for w, o in enumerate(out_refs):
        cp = pltpu.make_async_remote_copy(src_ref=o.at[c], dst_ref=o.at[c], send_sem=pair_send.at[w],
                                          recv_sem=pair_recv.at[w], device_id=sibling, device_id_type=MESH)
        cp.start()
        fwds.append(cp)
    for w, o in enumerate(out_refs):
        pltpu.make_async_remote_copy(src_ref=o.at[1 - c], dst_ref=o.at[1 - c], send_sem=pair_send.at[w],
                                     recv_sem=pair_recv.at[w], device_id=sibling, device_id_type=MESH).wait_recv()
    for cp in fwds:
        cp.wait_send()


def _gather_weights(mine):
    n = len(mine)

    def body(*refs):
        in_refs, out_refs, sems = refs[:n], refs[n:2 * n], refs[2 * n:]
        _gather_start(in_refs, out_refs, sems)
        _gather_finish(in_refs, out_refs, sems)

    return pl.pallas_call(
        body, name="gather_weights", in_specs=[_ANY] * n, out_specs=[_ANY] * n,
        out_shape=_gather_out_shapes(mine), scratch_shapes=_gather_scratch(n),
    )(*mine)


def _pair_exchange(gs):
    n = len(gs)

    def body(*refs):
        g_refs, a_refs, send_sems, recv_sems = refs[:n], refs[n:2 * n], refs[2 * n], refs[2 * n + 1]
        x, y, c = _place()
        cps = []
        for w in range(n):
            cp = pltpu.make_async_remote_copy(src_ref=g_refs[w].at[1 - c], dst_ref=a_refs[w], send_sem=send_sems.at[w],
                                              recv_sem=recv_sems.at[w], device_id=(x, y, 1 - c), device_id_type=MESH)
            cp.start()
            cps.append(cp)
        for cp in cps:
            cp.wait()

    return pl.pallas_call(
        body, name="grad_pair_exchange", in_specs=[_ANY] * n, out_specs=[_ANY] * n,
        out_shape=[jax.ShapeDtypeStruct(g.shape[1:], g.dtype) for g in gs],
        scratch_shapes=[pltpu.SemaphoreType.DMA((n,)), pltpu.SemaphoreType.DMA((n,))],
    )(*gs)


def _row_block(rows):
    return math.gcd(rows, 256)


def _pair_sum(g, a, layer, name):
    _, nq, rows, cols = g.shape
    tb = _row_block(rows)

    def body(l_ref, g_ref, a_ref, o_ref):
        o_ref[...] = (g_ref[...].astype(F32) + a_ref[...].astype(F32)).astype(BF16)

    return pl.pallas_call(
        body, name=name,
        grid_spec=pltpu.PrefetchScalarGridSpec(
            num_scalar_prefetch=1, grid=(nq, rows // tb),
            in_specs=[pl.BlockSpec((None, None, tb, cols), lambda q, i, l_ref: (l_ref[0], q, i, 0)),
                      pl.BlockSpec((None, tb, cols), lambda q, i, l_ref: (q, i, 0))],
            out_specs=pl.BlockSpec((None, tb, cols), lambda q, i, l_ref: (q, i, 0))),
        out_shape=jax.ShapeDtypeStruct((nq, rows, cols), BF16),
        compiler_params=_params("parallel", "parallel"),
    )(layer.reshape(1).astype(jnp.int32), g, a)


def _shard_exchange(ps):
    n = len(ps)

    def body(*refs):
        p_refs, b_refs = refs[:n], refs[n:2 * n]
        send_sems, recv_sems, local_sems = refs[2 * n:]
        _chip_exchange(lambda w, q: p_refs[w].at[q], b_refs, send_sems, recv_sems, local_sems)

    return pl.pallas_call(
        body, name="grad_shard_exchange", in_specs=[_ANY] * n, out_specs=[_ANY] * n,
        out_shape=[jax.ShapeDtypeStruct(p.shape, p.dtype) for p in ps],
        scratch_shapes=_sem_scratch(n),
    )(*ps)


def _sum4(b, name):
    _, rows, cols = b.shape
    tb = _row_block(rows)

    def body(b_ref, o_ref):
        o_ref[...] = ((b_ref[0].astype(F32) + b_ref[1].astype(F32)) + b_ref[2].astype(F32)) + b_ref[3].astype(F32)

    return pl.pallas_call(
        body, name=name, grid=(rows // tb,),
        in_specs=[pl.BlockSpec((N_CHIPS, tb, cols), lambda i: (0, i, 0))],
        out_specs=pl.BlockSpec((tb, cols), lambda i: (i, 0)),
        out_shape=jax.ShapeDtypeStruct((rows, cols), F32),
        compiler_params=_params("parallel"),
    )(b)


def _result_exchange(gcs):
    n = len(gcs)

    def body(*refs):
        g_refs, o_refs, send_sems, recv_sems = refs[:n], refs[n:2 * n], refs[2 * n], refs[2 * n + 1]
        x, y, c = _place()
        cps = []
        for w in range(n):
            cp = pltpu.make_async_remote_copy(src_ref=g_refs[w], dst_ref=o_refs[w], send_sem=send_sems.at[w],
                                              recv_sem=recv_sems.at[w], device_id=(x, y, 1 - c), device_id_type=MESH)
            cp.start()
            cps.append(cp)
        for cp in cps:
            cp.wait()

    return pl.pallas_call(
        body, name="grad_result_exchange", in_specs=[_ANY] * n, out_specs=[_ANY] * n,
        out_shape=[jax.ShapeDtypeStruct(g.shape, g.dtype) for g in gcs],
        scratch_shapes=[pltpu.SemaphoreType.DMA((n,)), pltpu.SemaphoreType.DMA((n,))],
    )(*gcs)


def _allreduce_small(v):
    def body(v_ref, o_ref, buf, send_sems, recv_sems):
        x, y, c = _place()
        me = 4 * x + 2 * y + c
        buf[me] = v_ref[...]
        peers = []
        for k in range(1, N_DEV):
            px = 1 - x if k & 4 else x
            py = 1 - y if k & 2 else y
            pc = 1 - c if k & 1 else c
            peers.append((px, py, pc))
        sends = []
        for k, peer in enumerate(peers):
            cp = pltpu.make_async_remote_copy(src_ref=v_ref, dst_ref=buf.at[me], send_sem=send_sems.at[k],
                                              recv_sem=recv_sems.at[k], device_id=peer, device_id_type=MESH)
            cp.start()
            sends.append(cp)
        for k, (px, py, pc) in enumerate(peers):
            pltpu.make_async_remote_copy(src_ref=v_ref, dst_ref=buf.at[4 * px + 2 * py + pc], send_sem=send_sems.at[k],
                                         recv_sem=recv_sems.at[k], device_id=(px, py, pc),
                                         device_id_type=MESH).wait_recv()
        for cp in sends:
            cp.wait_send()
        acc = buf[0]
        for i in range(1, N_DEV):
            acc = acc + buf[i]
        o_ref[...] = acc

    vm = pl.BlockSpec(memory_space=pltpu.VMEM)
    return pl.pallas_call(
        body, name="small_allreduce", in_specs=[vm], out_specs=vm,
        out_shape=jax.ShapeDtypeStruct(v.shape, F32),
        scratch_shapes=[pltpu.VMEM((N_DEV,) + v.shape, F32), pltpu.SemaphoreType.DMA((N_DEV - 1,)),
                        pltpu.SemaphoreType.DMA((N_DEV - 1,))],
    )(v)


def _adam_update(w, g, m, v):
    nm = ADAM_B1 * m + (1.0 - ADAM_B1) * g
    nv = ADAM_B2 * v + (1.0 - ADAM_B2) * (g * g)
    m_hat = nm / (1.0 - ADAM_B1 ** ADAM_STEP)
    v_hat = nv / (1.0 - ADAM_B2 ** ADAM_STEP)
    return -ADAM_LR * (m_hat / (jnp.sqrt(v_hat) + ADAM_EPS) + ADAM_WD * w), nm, nv


def _adamw_small(w, g, m, v, name):
    def body(w_ref, g_ref, m_ref, v_ref, d_ref, nm_ref, nv_ref):
        d_ref[...], nm_ref[...], nv_ref[...] = _adam_update(w_ref[...], g_ref[...], m_ref[...], v_ref[...])

    vm = pl.BlockSpec(memory_space=pltpu.VMEM)
    return pl.pallas_call(
        body, name=name, in_specs=[vm] * 4, out_specs=[vm] * 3,
        out_shape=[jax.ShapeDtypeStruct(w.shape, F32)] * 3,
    )(w, g, m, v)


def _adamw_big(w, m, v, g_own, g_other, layer, name):
    _, rows, cols = w.shape
    tb = _row_block(rows)

    def body(l_ref, w_ref, m_ref, v_ref, go_ref, gx_ref, g_ref, d_ref, nm_ref, nv_ref):
        gv = jnp.where(pl.program_id(0) == l_ref[0], go_ref[...], gx_ref[...])
        g_ref[...] = gv
        d_ref[...], nm_ref[...], nv_ref[...] = _adam_update(w_ref[...], gv, m_ref[...], v_ref[...])

    per_layer = pl.BlockSpec((None, tb, cols), lambda l, i, l_ref: (l, i, 0))
    shared = pl.BlockSpec((tb, cols), lambda l, i, l_ref: (i, 0))
    return pl.pallas_call(
        body, name=name,
        grid_spec=pltpu.PrefetchScalarGridSpec(
            num_scalar_prefetch=1, grid=(DEPTH, rows // tb),
            in_specs=[per_layer, per_layer, per_layer, shared, shared], out_specs=[per_layer] * 4),
        out_shape=[jax.ShapeDtypeStruct(w.shape, F32)] * 4,
        compiler_params=_params("parallel", "parallel"),
    )(layer.reshape(1).astype(jnp.int32), w, m, v, g_own, g_other)


def kernel(x, w_in, b_fgate, hgrn_lb_logits, hgrn_norm_g, w_branch_a, w_branch_b, w_out, ln1_g, ln1_b, w_ff_in, w_ff_out, ln2_g, ln2_b, loss_target, m_w_in, m_b_fgate, m_hgrn_lb_logits, m_hgrn_norm_g, m_w_branch_a, m_w_branch_b, m_w_out, m_ln1_g, m_ln1_b, m_w_ff_in, m_w_ff_out, m_ln2_g, m_ln2_b, v_w_in, v_b_fgate, v_hgrn_lb_logits, v_hgrn_norm_g, v_w_branch_a, v_w_branch_b, v_w_out, v_ln1_g, v_ln1_b, v_w_ff_in, v_w_ff_out, v_ln2_g, v_ln2_b):
    weights = dict(w_in=w_in, b_fgate=b_fgate, hgrn_lb_logits=hgrn_lb_logits, hgrn_norm_g=hgrn_norm_g,
                   w_branch_a=w_branch_a, w_branch_b=w_branch_b, w_out=w_out, ln1_g=ln1_g, ln1_b=ln1_b,
                   w_ff_in=w_ff_in, w_ff_out=w_ff_out, ln2_g=ln2_g, ln2_b=ln2_b)
    mom1 = dict(w_in=m_w_in, b_fgate=m_b_fgate, hgrn_lb_logits=m_hgrn_lb_logits, hgrn_norm_g=m_hgrn_norm_g,
                w_branch_a=m_w_branch_a, w_branch_b=m_w_branch_b, w_out=m_w_out, ln1_g=m_ln1_g, ln1_b=m_ln1_b,
                w_ff_in=m_w_ff_in, w_ff_out=m_w_ff_out, ln2_g=m_ln2_g, ln2_b=m_ln2_b)
    mom2 = dict(w_in=v_w_in, b_fgate=v_b_fgate, hgrn_lb_logits=v_hgrn_lb_logits, hgrn_norm_g=v_hgrn_norm_g,
                w_branch_a=v_w_branch_a, w_branch_b=v_w_branch_b, w_out=v_w_out, ln1_g=v_ln1_g, ln1_b=v_ln1_b,
                w_ff_in=v_w_ff_in, w_ff_out=v_w_ff_out, ln2_g=v_ln2_g, ln2_b=v_ln2_b)
    core = lax.axis_index("c")

    def halves(shard):
        return shard.astype(BF16).reshape(2, shard.shape[0] // 2, shard.shape[1])

    def from_halves(gathered, axis):
        return _from_chips(jnp.concatenate([gathered[0], gathered[1]], axis=1), axis)

    mid, ffn = _BIG[1:4], _BIG[4:]

    def late_weights(l, gathered):
        if l == 1:
            return [{}, {key: from_halves(gathered[w], axis) for w, (_, key, _, axis) in enumerate(ffn)}]
        upd = [{key: _from_chips(gathered[1 + w][k], axis) for w, (_, key, _, axis) in enumerate(mid)}
               for k in range(DEPTH)]
        upd[1]["w_in"] = from_halves(gathered[0], 1)
        upd[0].update({key: from_halves(gathered[4 + w], axis) for w, (_, key, _, axis) in enumerate(ffn)})
        return upd

    wfull = [{"w_in": from_halves(_gather_weights([halves(weights["w_in"][0])])[0], 1)}, {}]
    rides = [[halves(weights["w_in"][1])] + [weights[name].astype(BF16) for name, _, _, _ in mid]
             + [halves(weights[name][0]) for name, _, _, _ in ffn],
             [halves(weights[name][1]) for name, _, _, _ in ffn]]
    small = {name: weights[name] for name, _ in _SMALL}

    loss_part, grad_x, grads, d_logits = _local_step(x[0], loss_target[0], wfull, small, rides, late_weights)

    g_all = [jnp.stack([_by_chip(grads[l][key], axis) for l in range(DEPTH)]).astype(BF16)
             for _, key, _, axis in _BIG]
    received = _pair_exchange(g_all)
    pair = [_pair_sum(g_all[w], received[w], core, f"grad_pair_sum_{w}") for w in range(N_BIG)]
    by_chip = _shard_exchange(pair)
    g_layer = [_sum4(by_chip[w], f"grad_chip_sum_{w}") for w in range(N_BIG)]
    g_other = _result_exchange(g_layer)
    out_g, out_d, out_m, out_v = {}, {}, {}, {}
    for w, (name, _, _, _) in enumerate(_BIG):
        out_g[name], out_d[name], out_m[name], out_v[name] = _adamw_big(
            weights[name], mom1[name], mom2[name], g_layer[w], g_other[w], core, f"adamw_{name}")

    small_grads = {name: jnp.stack([grads[l][key] for l in range(DEPTH)])
                   for name, key in [("b_fgate", "b_fgate"), ("hgrn_norm_g", "norm_g"), ("ln1_g", "ln1_g"),
                                     ("ln1_b", "ln1_b"), ("ln2_g", "ln2_g"), ("ln2_b", "ln2_b")]}
    small_grads["hgrn_lb_logits"] = d_logits
    gs = _allreduce_small(_pack_small(small_grads))
    ds, ms, vs = _adamw_small(_pack_small(small), gs, _pack_small({n: mom1[n] for n, _ in _SMALL}),
                              _pack_small({n: mom2[n] for n, _ in _SMALL}), "adamw_small")
    for tree, slab in ((out_g, gs), (out_d, ds), (out_m, ms), (out_v, vs)):
        tree.update(_unpack_small(slab))

    loss = lax.psum(loss_part, ("x", "y", "c"))
    order = ["w_in", "b_fgate", "hgrn_lb_logits", "hgrn_norm_g", "w_branch_a", "w_branch_b", "w_out", "ln1_g", "ln1_b",
             "w_ff_in", "w_ff_out", "ln2_g", "ln2_b"]
    return (loss, grad_x[None], *[out_g[n] for n in order], *[out_d[n] for n in order],
            *[out_m[n] for n in order], *[out_v[n] for n in order])
```

```python
import math

import jax
import jax.numpy as jnp
from jax import lax
from jax.experimental import pallas as pl
from jax.experimental.pallas import tpu as pltpu

F32 = jnp.float32
BF16 = jnp.bfloat16

D_MODEL = 1024
DEPTH = 2
A_HEADS = 8
A_WIDTH = 512
B_WIDTH = 512
B_HEADS = 4
HD = 128
CHUNK = 64
SUB = 16
FFN_HIDDEN = 2816
IN_TOTAL = 5640
ALPHA = (2 * DEPTH) ** 0.25
LN_EPS = 1e-5
RMS_EPS = 1e-6
ADAM_LR = 0.001
ADAM_B1 = 0.9
ADAM_B2 = 0.999
ADAM_EPS = 1e-08
ADAM_WD = 0.01
ADAM_STEP = 10
EXP_CLAMP = 60.0

VMEM_LIMIT_BYTES = 56 * 1024 * 1024
MM_ROWS = 1024
DW_ROWS = 2048
N_CHIPS = 4
N_DEV = 8
MESH = pl.DeviceIdType.MESH

_DN = {
    "nn": (((1,), (0,)), ((), ())),
    "nt": (((1,), (1,)), ((), ())),
    "tn": (((0,), (0,)), ((), ())),
}


def _dot(a, b, mode="nn"):
    return lax.dot_general(a.astype(BF16), b.astype(BF16), _DN[mode], preferred_element_type=F32)


def _pieces(x):
    h = x.astype(BF16)
    r = x - h.astype(F32)
    m = r.astype(BF16)
    return h, m, (r - m.astype(F32)).astype(BF16)


def _dot_hi(a, b, mode="nn", exact="a"):
    if exact == "a":
        h, m, l = _pieces(b)
        return (_dot(a, l, mode) + _dot(a, m, mode)) + _dot(a, h, mode)
    h, m, l = _pieces(a)
    return (_dot(l, b, mode) + _dot(m, b, mode)) + _dot(h, b, mode)


def _hdot(a, b, mode="nn"):
    bh, bl, _ = _pieces(b)
    return _dot(a, bl, mode) + _dot(a, bh, mode)


def _params(*sem):
    return pltpu.CompilerParams(dimension_semantics=sem, vmem_limit_bytes=VMEM_LIMIT_BYTES)


def _sigmoid(x):
    return 1.0 / (1.0 + jnp.exp(-x))


def _matmul(a, b, mode, out_dtype, tm, tn, tk, name):
    if mode == "nn":
        (m, k), (k2, n) = a.shape, b.shape
    elif mode == "nt":
        (m, k), (n, k2) = a.shape, b.shape
    else:
        (k, m), (k2, n) = a.shape, b.shape
    assert k == k2, (a.shape, b.shape, mode)
    tm, tn, tk = min(tm, m), min(tn, n), min(tk, k)
    assert m % tm == 0 and n % tn == 0 and k % tk == 0, (a.shape, b.shape, tm, tn, tk)
    nk = k // tk
    if mode == "tn":
        a_spec = pl.BlockSpec((tk, tm), lambda j, i, kk: (kk, i))
    else:
        a_spec = pl.BlockSpec((tm, tk), lambda j, i, kk: (i, kk))
    if mode == "nt":
        b_spec = pl.BlockSpec((tn, tk), lambda j, i, kk: (j, kk))
    else:
        b_spec = pl.BlockSpec((tk, tn), lambda j, i, kk: (kk, j))
    use_acc = nk > 1 and out_dtype != F32

    def body(a_ref, b_ref, o_ref, *scratch):
        p = _dot(a_ref[...], b_ref[...], mode)
        if nk == 1:
            o_ref[...] = p.astype(out_dtype)
            return
        acc_ref = scratch[0] if use_acc else o_ref
        kk = pl.program_id(2)

        @pl.when(kk == 0)
        def _():
            acc_ref[...] = p

        @pl.when(kk > 0)
        def _():
            acc_ref[...] += p

        if use_acc:
            @pl.when(kk == nk - 1)
            def _():
                o_ref[...] = acc_ref[...].astype(out_dtype)

    return pl.pallas_call(
        body,
        name=name,
        grid=(n // tn, m // tm, nk),
        in_specs=[a_spec, b_spec],
        out_specs=pl.BlockSpec((tm, tn), lambda j, i, kk: (i, j)),
        out_shape=jax.ShapeDtypeStruct((m, n), out_dtype),
        scratch_shapes=[pltpu.VMEM((tm, tn), F32)] if use_acc else [],
        compiler_params=_params("parallel", "parallel", "arbitrary"),
    )(a, b)


def _matmul_nt_sum(a_list, b_list, name, plus=None, tm=512):
    m = a_list[0].shape[0]
    n = b_list[0].shape[0]
    tm = min(tm, m)
    np_ = len(a_list)
    extra = [] if plus is None else [plus[1]]

    def body(*refs):
        o_ref = refs[-1]
        acc = _dot(refs[0][...], refs[np_][...], "nt")
        for p in range(1, np_):
            acc = acc + _dot(refs[p][...], refs[np_ + p][...], "nt")
        if plus is not None:
            acc = acc + plus[0] * refs[2 * np_][...]
        o_ref[...] = acc

    row = pl.BlockSpec((tm, n), lambda i: (i, 0))
    return pl.pallas_call(
        body,
        name=name,
        grid=(m // tm,),
        in_specs=[pl.BlockSpec((tm, a.shape[1]), lambda i: (i, 0)) for a in a_list]
        + [pl.BlockSpec(b.shape, lambda i: (0, 0)) for b in b_list] + [row] * len(extra),
        out_specs=row,
        out_shape=jax.ShapeDtypeStruct((m, n), F32),
        compiler_params=_params("parallel"),
    )(*a_list, *b_list, *extra)


def _mm_res_ln(a, w, res, g, b, name, tm=512):
    t, k = a.shape
    d = w.shape[1]
    tm = min(tm, t)

    def body(a_ref, w_ref, r_ref, g_ref, b_ref, y_ref, yb_ref, xh_ref, rs_ref):
        z = ALPHA * r_ref[...] + _dot(a_ref[...], w_ref[...])
        mu = jnp.mean(z, axis=-1, keepdims=True)
        zc = z - mu
        var = jnp.mean(zc * zc, axis=-1, keepdims=True)
        rstd = lax.rsqrt(var + LN_EPS)
        xh = zc * rstd
        y = xh * g_ref[...] + b_ref[...]
        y_ref[...] = y
        yb_ref[...] = y.astype(BF16)
        xh_ref[...] = xh
        rs_ref[...] = rstd

    row = lambda i: (i, 0)
    fix = lambda i: (0, 0)
    return pl.pallas_call(
        body,
        name=name,
        grid=(t // tm,),
        in_specs=[pl.BlockSpec((tm, k), row), pl.BlockSpec((k, d), fix), pl.BlockSpec((tm, d), row),
                  pl.BlockSpec((1, d), fix), pl.BlockSpec((1, d), fix)],
        out_specs=[pl.BlockSpec((tm, d), row), pl.BlockSpec((tm, d), row), pl.BlockSpec((tm, d), row),
                   pl.BlockSpec((tm, 1), row)],
        out_shape=[jax.ShapeDtypeStruct((t, d), F32), jax.ShapeDtypeStruct((t, d), BF16),
                   jax.ShapeDtypeStruct((t, d), F32), jax.ShapeDtypeStruct((t, 1), F32)],
        compiler_params=_params("parallel"),
    )(a, w, res, g.reshape(1, d), b.reshape(1, d))


def _ln_bwd(dys, coefs, xhat, rstd, g, name, tm=512):
    t, d = xhat.shape
    tm = min(tm, t)
    n_in = len(dys)

    def body(*refs):
        dy_refs = refs[:n_in]
        xh_ref, rs_ref, g_ref, dz_ref, dzb_ref, dg_ref, db_ref = refs[n_in:]
        dy = coefs[0] * dy_refs[0][...].astype(F32)
        for c, r in zip(coefs[1:], dy_refs[1:]):
            dy = dy + c * r[...].astype(F32)
        xh = xh_ref[...]
        dxh = dy * g_ref[...]
        m1 = jnp.mean(dxh, axis=-1, keepdims=True)
        m2 = jnp.mean(dxh * xh, axis=-1, keepdims=True)
        dz = rs_ref[...] * (dxh - m1 - xh * m2)
        dz_ref[...] = dz
        dzb_ref[...] = dz.astype(BF16)
        pg = jnp.sum(dy * xh, axis=0, keepdims=True)
        pb = jnp.sum(dy, axis=0, keepdims=True)

        @pl.when(pl.program_id(0) == 0)
        def _():
            dg_ref[...] = pg
            db_ref[...] = pb

        @pl.when(pl.program_id(0) > 0)
        def _():
            dg_ref[...] += pg
            db_ref[...] += pb

    row = lambda i: (i, 0)
    fix = lambda i: (0, 0)
    return pl.pallas_call(
        body,
        name=name,
        grid=(t // tm,),
        in_specs=[pl.BlockSpec((tm, d), row)] * n_in
        + [pl.BlockSpec((tm, d), row), pl.BlockSpec((tm, 1), row), pl.BlockSpec((1, d), fix)],
        out_specs=[pl.BlockSpec((tm, d), row), pl.BlockSpec((tm, d), row), pl.BlockSpec((1, d), fix),
                   pl.BlockSpec((1, d), fix)],
        out_shape=[jax.ShapeDtypeStruct((t, d), F32), jax.ShapeDtypeStruct((t, d), BF16),
                   jax.ShapeDtypeStruct((1, d), F32), jax.ShapeDtypeStruct((1, d), F32)],
        compiler_params=_params("arbitrary"),
    )(*dys, xhat, rstd, g.reshape(1, d))


def _loss_head(y, target, name="loss_head", tm=512):
    t, d = y.shape
    tm = min(tm, t)

    def body(y_ref, t_ref, dy_ref, l_ref):
        e = y_ref[...] - t_ref[...]
        dy_ref[...] = e * (1.0 / d)
        part = jnp.full((8, 128), 0.5 / d, F32) * jnp.sum(e * e)

        @pl.when(pl.program_id(0) == 0)
        def _():
            l_ref[...] = part

        @pl.when(pl.program_id(0) > 0)
        def _():
            l_ref[...] += part

    row = lambda i: (i, 0)
    return pl.pallas_call(
        body,
        name=name,
        grid=(t // tm,),
        in_specs=[pl.BlockSpec((tm, d), row), pl.BlockSpec((tm, d), row)],
        out_specs=[pl.BlockSpec((tm, d), row), pl.BlockSpec((8, 128), lambda i: (0, 0))],
        out_shape=[jax.ShapeDtypeStruct((t, d), F32), jax.ShapeDtypeStruct((8, 128), F32)],
        compiler_params=_params("arbitrary"),
    )(y, target)


FFN_COLS = FFN_HIDDEN // 2


def _ffn_in_swiglu(xb, wu, wg, name, tm=MM_ROWS):
    t, d = xb.shape
    tm = min(tm, t)

    def body(x_ref, wu_ref, wg_ref, a_ref, u_ref, g_ref):
        x = x_ref[...]
        u = _dot(x, wu_ref[...])
        g = _dot(x, wg_ref[...])
        u_ref[...] = u.astype(BF16)
        g_ref[...] = g.astype(BF16)
        a_ref[...] = (g * _sigmoid(g) * u).astype(BF16)

    wspec = pl.BlockSpec((d, FFN_COLS), lambda j, i: (0, j))
    out = pl.BlockSpec((tm, FFN_COLS), lambda j, i: (i, j))
    return pl.pallas_call(
        body,
        name=name,
        grid=(FFN_HIDDEN // FFN_COLS, t // tm),
        in_specs=[pl.BlockSpec((tm, d), lambda j, i: (i, 0)), wspec, wspec],
        out_specs=[out, out, out],
        out_shape=[jax.ShapeDtypeStruct((t, FFN_HIDDEN), BF16)] * 3,
        compiler_params=_params("parallel", "parallel"),
    )(xb, wu, wg)


def _ffn_out_dx_swiglu(dzb, w_ff_out, u, g, name, tm=MM_ROWS):
    t, d = dzb.shape
    tm = min(tm, t)

    def body(dz_ref, w_ref, u_ref, g_ref, du_ref, dg_ref):
        da = _dot(dz_ref[...], w_ref[...], "nt")
        gv = g_ref[...].astype(F32)
        sg = _sigmoid(gv)
        du_ref[...] = (da * gv * sg).astype(BF16)
        dg_ref[...] = (da * u_ref[...].astype(F32) * (sg * (1.0 + gv * (1.0 - sg)))).astype(BF16)

    blk = pl.BlockSpec((tm, FFN_COLS), lambda j, i: (i, j))
    return pl.pallas_call(
        body,
        name=name,
        grid=(FFN_HIDDEN // FFN_COLS, t // tm),
        in_specs=[pl.BlockSpec((tm, d), lambda j, i: (i, 0)), pl.BlockSpec((FFN_COLS, d), lambda j, i: (j, 0)), blk, blk],
        out_specs=[blk, blk],
        out_shape=[jax.ShapeDtypeStruct((t, FFN_HIDDEN), BF16)] * 2,
        compiler_params=_params("parallel", "parallel"),
    )(dzb, w_ff_out, u, g)


def _merge_fwd(ya, yb, wpa, wpb, gates, name, tm=512):
    t = ya.shape[0]
    tm = min(tm, t)

    def body(ya_ref, yb_ref, wa_ref, wb_ref, ga_ref, gb_ref, o_ref):
        pa = _dot(ya_ref[...], wa_ref[...])
        pb = _dot(yb_ref[...], wb_ref[...])
        o_ref[...] = (_sigmoid(ga_ref[...].astype(F32)) * pa + _sigmoid(gb_ref[...].astype(F32)) * pb).astype(BF16)

    row = lambda i: (i, 0)
    fix = lambda i: (0, 0)
    return pl.pallas_call(
        body,
        name=name,
        grid=(t // tm,),
        in_specs=[pl.BlockSpec((tm, A_WIDTH), row), pl.BlockSpec((tm, B_WIDTH), row),
                  pl.BlockSpec((A_WIDTH, D_MODEL), fix), pl.BlockSpec((B_WIDTH, D_MODEL), fix),
                  pl.BlockSpec((tm, D_MODEL), lambda i: (i, 0)), pl.BlockSpec((tm, D_MODEL), lambda i: (i, 1))],
        out_specs=pl.BlockSpec((tm, D_MODEL), row),
        out_shape=jax.ShapeDtypeStruct((t, D_MODEL), BF16),
        compiler_params=_params("parallel"),
    )(ya, yb, wpa, wpb, gates, gates)


def _merge_bwd(dzb, w_out, ya, yb, wpa, wpb, gates, name, tm=512):
    t = ya.shape[0]
    tm = min(tm, t)

    def body(dz_ref, wo_ref, ya_ref, yb_ref, wa_ref, wb_ref, ga_ref, gb_ref, dg_ref, dpa_ref, dpb_ref, dya_ref,
             dyb_ref):
        dm_v = _dot(dz_ref[...], wo_ref[...], "nt")
        pa = _dot(ya_ref[...], wa_ref[...])
        pb = _dot(yb_ref[...], wb_ref[...])
        sa = _sigmoid(ga_ref[...].astype(F32))
        sb = _sigmoid(gb_ref[...].astype(F32))
        dg_ref[:, :D_MODEL] = (dm_v * pa * sa * (1.0 - sa)).astype(BF16)
        dg_ref[:, D_MODEL:] = (dm_v * pb * sb * (1.0 - sb)).astype(BF16)
        dpa = (dm_v * sa).astype(BF16)
        dpb = (dm_v * sb).astype(BF16)
        dpa_ref[...] = dpa
        dpb_ref[...] = dpb
        dya_ref[...] = _dot(dpa, wa_ref[...], "nt").astype(BF16)
        dyb_ref[...] = _dot(dpb, wb_ref[...], "nt")

    row = lambda i: (i, 0)
    fix = lambda i: (0, 0)
    return pl.pallas_call(
        body,
        name=name,
        grid=(t // tm,),
        in_specs=[pl.BlockSpec((tm, D_MODEL), row), pl.BlockSpec((D_MODEL, D_MODEL), fix),
                  pl.BlockSpec((tm, A_WIDTH), row), pl.BlockSpec((tm, B_WIDTH), row),
                  pl.BlockSpec((A_WIDTH, D_MODEL), fix), pl.BlockSpec((B_WIDTH, D_MODEL), fix),
                  pl.BlockSpec((tm, D_MODEL), lambda i: (i, 0)), pl.BlockSpec((tm, D_MODEL), lambda i: (i, 1))],
        out_specs=[pl.BlockSpec((tm, 2 * D_MODEL), row), pl.BlockSpec((tm, D_MODEL), row),
                   pl.BlockSpec((tm, D_MODEL), row), pl.BlockSpec((tm, A_WIDTH), row), pl.BlockSpec((tm, B_WIDTH), row)],
        out_shape=[jax.ShapeDtypeStruct((t, 2 * D_MODEL), BF16), jax.ShapeDtypeStruct((t, D_MODEL), BF16),
                   jax.ShapeDtypeStruct((t, D_MODEL), BF16), jax.ShapeDtypeStruct((t, A_WIDTH), BF16),
                   jax.ShapeDtypeStruct((t, B_WIDTH), F32)],
        compiler_params=_params("parallel"),
    )(dzb, w_out, ya, yb, wpa, wpb, gates, gates)


FA_BLOCK = 4 * B_WIDTH // 128


def _tri(n, lower):
    r = lax.broadcasted_iota(jnp.int32, (n, n), 0)
    c = lax.broadcasted_iota(jnp.int32, (n, n), 1)
    return jnp.where((r >= c) if lower else (r <= c), 1.0, 0.0).astype(F32)


def _head_spread(expand):
    shape = (128, A_WIDTH) if expand else (A_WIDTH, 128)
    r = lax.broadcasted_iota(jnp.int32, shape, 0)
    c = lax.broadcasted_iota(jnp.int32, shape, 1)
    hit = ((c >= 64 * r) & (c < 64 * r + 64)) if expand else (r == 64 * c)
    return jnp.where(hit, 1.0, 0.0).astype(F32)


def _fox_gate_fwd(rest, bf, name, tb=512):
    t = rest.shape[0]
    tb = min(tb, t)

    def body(fa_ref, bf_ref, f_ref, fc_ref, carry):
        @pl.when(pl.program_id(0) == 0)
        def _():
            carry[...] = jnp.zeros_like(carry)

        z = fa_ref[...] + bf_ref[...]
        logf = jnp.minimum(z, 0.0) - jnp.log(1.0 + jnp.exp(-jnp.abs(z)))
        f = _dot_hi(_tri(tb, True), logf) + carry[...]
        f_ref[...] = f
        fc_ref[...] = _dot_hi(f, _head_spread(True), exact="b")
        carry[...] = f[tb - 1:tb, :]

    return pl.pallas_call(
        body,
        name=name,
        grid=(t // tb,),
        in_specs=[pl.BlockSpec((tb, 128), lambda i: (i, FA_BLOCK)), pl.BlockSpec((1, 128), lambda i: (0, 0))],
        out_specs=[pl.BlockSpec((tb, 128), lambda i: (i, 0)), pl.BlockSpec((tb, A_WIDTH), lambda i: (i, 0))],
        out_shape=[jax.ShapeDtypeStruct((t, 128), F32), jax.ShapeDtypeStruct((t, A_WIDTH), F32)],
        scratch_shapes=[pltpu.VMEM((1, 128), F32)],
        compiler_params=_params("arbitrary"),
    )(rest, bf)


def _fox_gate_bwd(rsum, csum, rest, bf, name, tb=512):
    t = rest.shape[0]
    tb = min(tb, t)
    nb = t // tb

    def body(rs_ref, cs_ref, fa_ref, bf_ref, dfa_ref, dbf_ref, carry):
        @pl.when(pl.program_id(0) == 0)
        def _():
            carry[...] = jnp.zeros_like(carry)

        d_f = _dot_hi(rs_ref[...] - cs_ref[...], _head_spread(False), exact="b")
        dlogf = _dot_hi(_tri(tb, False), d_f) + carry[...]
        carry[...] = dlogf[0:1, :]
        z = fa_ref[...] + bf_ref[...]
        dz = dlogf * _sigmoid(-z)
        dfa_ref[...] = dz.astype(BF16)
        part = jnp.sum(dz, axis=0, keepdims=True)

        @pl.when(pl.program_id(0) == 0)
        def _():
            dbf_ref[...] = part

        @pl.when(pl.program_id(0) > 0)
        def _():
            dbf_ref[...] += part

    return pl.pallas_call(
        body,
        name=name,
        grid=(nb,),
        in_specs=[pl.BlockSpec((tb, A_WIDTH), lambda i: (nb - 1 - i, 0)),
                  pl.BlockSpec((tb, A_WIDTH), lambda i: (nb - 1 - i, 0)),
                  pl.BlockSpec((tb, 128), lambda i: (nb - 1 - i, FA_BLOCK)),
                  pl.BlockSpec((1, 128), lambda i: (0, 0))],
        out_specs=[pl.BlockSpec((tb, 128), lambda i: (nb - 1 - i, 0)), pl.BlockSpec((1, 128), lambda i: (0, 0))],
        out_shape=[jax.ShapeDtypeStruct((t, 128), BF16), jax.ShapeDtypeStruct((1, 128), F32)],
        scratch_shapes=[pltpu.VMEM((1, 128), F32)],
        compiler_params=_params("arbitrary"),
    )(rsum, csum, rest, bf)


ATT_BLOCK = 512
FWD_Q_BLOCKS = 2
BWD_K_BLOCKS = 2


def _head_mask(shape, j):
    lane = lax.broadcasted_iota(jnp.int32, shape, 1)
    return (lane < 64) if j == 0 else (lane >= 64)


def _aug_lanes(tb, j):
    lane = lax.broadcasted_iota(jnp.int32, (tb, 128), 1)
    own = (lane < 64) if j == 0 else (lane >= 64)
    return own, lane - 64 * (1 - j)


def _aug_query(own, li, q, pieces):
    h, m, l = pieces
    one, zero = jnp.ones_like(h), jnp.zeros_like(h)
    spare = jnp.where(li == 0, h, jnp.where(li == 1, m, jnp.where(li == 2, l, jnp.where(li < 6, one, zero))))
    return jnp.where(own, q, spare)


def _fox_prep_fwd(qkv, fcol, name, tb=2048):
    t = qkv.shape[0]
    tb = min(tb, t)

    def body(q_ref, k_ref, v_ref, fc_ref, qa_ref, ka_ref, va_ref, qn_ref, kn_ref):
        pieces = _pieces(pltpu.roll(fc_ref[...], 64, 1))
        h, m, l = pieces
        q, k, v = q_ref[...], k_ref[...], v_ref[...]
        first = _head_mask((tb, 128), 0)
        for nrm_ref, x in ((qn_ref, q.astype(F32)), (kn_ref, k.astype(F32))):
            n0 = jnp.max(jnp.sum(jnp.where(first, x * x, 0.0), axis=1, keepdims=True))
            n1 = jnp.max(jnp.sum(jnp.where(first, 0.0, x * x), axis=1, keepdims=True))
            nrm_ref[...] = jnp.where(_head_mask((8, 128), 0), n0, n1)
        one, zero = jnp.ones_like(h), jnp.zeros_like(h)
        for j in (0, 1):
            own, li = _aug_lanes(tb, j)
            cols = slice(128 * j, 128 * (j + 1))
            qa_ref[:, cols] = _aug_query(own, li, q * 0.125, pieces)
            ks = jnp.where(li < 3, one, jnp.where(li == 3, -h, jnp.where(li == 4, -m, jnp.where(li == 5, -l, zero))))
            ka_ref[:, cols] = jnp.where(own, k, ks)
            va_ref[:, cols] = jnp.where(own, v, one)

    blk = pl.BlockSpec((tb, 256), lambda i, h: (i, h))
    nrm = pl.BlockSpec((None, None, 8, 128), lambda i, h: (i, h, 0, 0))
    return pl.pallas_call(
        body, name=name, grid=(t // tb, 4),
        in_specs=[pl.BlockSpec((tb, 128), lambda i, h: (i, h)), pl.BlockSpec((tb, 128), lambda i, h: (i, 4 + h)),
                  pl.BlockSpec((tb, 128), lambda i, h: (i, 8 + h)), pl.BlockSpec((tb, 128), lambda i, h: (i, h))],
        out_specs=[blk, blk, blk, nrm, nrm],
        out_shape=[jax.ShapeDtypeStruct((t, 2 * A_WIDTH), BF16)] * 3
        + [jax.ShapeDtypeStruct((t // tb, 4, 8, 128), F32)] * 2,
        compiler_params=_params("parallel", "parallel"),
    )(qkv, qkv, qkv, fcol)


def _fox_prep_bwd(qkv, fcol, lse, do, o, name, tb=2048):
    t = qkv.shape[0]
    tb = min(tb, t)

    def body(q_ref, fc_ref, lse_ref, do_ref, o_ref, qb_ref, dob_ref):
        pieces = _pieces(pltpu.roll(fc_ref[...] - lse_ref[...], 64, 1))
        q = q_ref[...] * 0.125
        do_v = do_ref[...]
        prod = do_v.astype(F32) * o_ref[...].astype(F32)
        for j in (0, 1):
            own, li = _aug_lanes(tb, j)
            cols = slice(128 * j, 128 * (j + 1))
            qb_ref[:, cols] = _aug_query(own, li, q, pieces)
            delta = jnp.sum(jnp.where(own, prod, 0.0), axis=1, keepdims=True)
            h, m, l = _pieces(jnp.broadcast_to(delta, (tb, 128)))
            ds = jnp.where(li == 0, -h, jnp.where(li == 1, -m, jnp.where(li == 2, -l, jnp.zeros_like(h))))
            dob_ref[:, cols] = jnp.where(own, do_v, ds)

    pair = pl.BlockSpec((tb, 128), lambda i, h: (i, h))
    blk = pl.BlockSpec((tb, 256), lambda i, h: (i, h))
    return pl.pallas_call(
        body, name=name, grid=(t // tb, 4),
        in_specs=[pair, pair, pair, pair, pair],
        out_specs=[blk, blk],
        out_shape=[jax.ShapeDtypeStruct((t, 2 * A_WIDTH), BF16)] * 2,
        compiler_params=_params("parallel", "parallel"),
    )(qkv, fcol, lse, do, o)


def _tile_mask(n, transposed):
    r = lax.broadcasted_iota(jnp.int32, (n, n), 0)
    c = lax.broadcasted_iota(jnp.int32, (n, n), 1)
    return (c >= r) if transposed else (r >= c)


UNDERFLOW = -110.0


def _fox_block_ranges(qn, kn, fcum):
    t = fcum.shape[0]
    blk = min(ATT_BLOCK, t)
    nb = t // blk
    q2 = jnp.max(qn[:, :, 0, ::64].reshape(-1, A_HEADS), axis=0)
    k2 = jnp.max(kn[:, :, 0, ::64].reshape(-1, A_HEADS), axis=0)
    bound = 2.0 * jnp.sqrt(q2 * k2) * 0.125
    f = fcum[:, :A_HEADS]
    first = f[0::blk].T
    last = f[blk - 1::blk].T
    dead = (bound[:, None, None] + first[:, :, None] - last[:, None, :]) < UNDERFLOW
    qi = jnp.arange(nb)[None, :, None]
    kj = jnp.arange(nb)[None, None, :]
    dead = dead & (kj < qi)
    kstart = jnp.sum(dead, axis=2).astype(jnp.int32)
    qend = (kj[0] + jnp.sum((~dead) & (qi > kj), axis=1)).astype(jnp.int32)
    return kstart.reshape(-1), qend.reshape(-1)


def _fox_fwd(qa, ka, va, kstart, name):
    t = qa.shape[0]
    bk = min(ATT_BLOCK, t)
    nk = t // bk
    qf = FWD_Q_BLOCKS if t % (FWD_Q_BLOCKS * bk) == 0 else 1
    bq = qf * bk
    nq = t // bq

    def body(ks_ref, q_ref, k_ref, v_ref, o_ref, lse_ref):
        i = pl.program_id(1)
        hp = pl.program_id(0)
        k0 = [ks_ref[(2 * hp + j) * nk + qf * i] for j in (0, 1)]
        both0 = jnp.maximum(k0[0], k0[1])

        def head(j, kb, m, acc, diag):
            rows = pl.ds(pl.multiple_of(kb * bk, bk), bk)
            cols = slice(128 * j, 128 * (j + 1))
            s = _dot(q_ref[:, cols], k_ref[rows, cols], "nt")
            if diag is not None:
                r = lax.broadcasted_iota(jnp.int32, (bq, bk), 0)
                c = lax.broadcasted_iota(jnp.int32, (bq, bk), 1)
                s = jnp.where(r - c >= diag, s, -jnp.inf)
            m_new = jnp.maximum(m, jnp.max(s, axis=1, keepdims=True))
            return m_new, jnp.exp(m - m_new) * acc + _dot(jnp.exp(s - m_new), v_ref[rows, cols])

        def pair(kb, carry, diag):
            return head(0, kb, carry[0], carry[1], diag) + head(1, kb, carry[2], carry[3], diag)

        init = (jnp.full((bq, 1), -jnp.inf, F32), jnp.zeros((bq, 128), F32))
        alone = [lax.fori_loop(k0[j], both0, lambda kb, c, j=j: head(j, kb, c[0], c[1], None), init) for j in (0, 1)]
        carry = lax.fori_loop(both0, qf * i, lambda kb, c: pair(kb, c, None), alone[0] + alone[1])
        for d in range(qf):
            carry = pair(qf * i + d, carry, d * bk)
        outs = []
        for j in (0, 1):
            m, acc = carry[2 * j], carry[2 * j + 1]
            spare = 64 * (1 - j)
            l = acc[:, spare:spare + 1]
            outs.append((acc / l, m + jnp.log(l)))
        msk = _head_mask((bq, 128), 0)
        o_ref[...] = jnp.where(msk, outs[0][0], outs[1][0]).astype(BF16)
        lse_ref[...] = jnp.where(msk, outs[0][1], outs[1][1])

    res = pl.BlockSpec((t, 256), lambda h, i, tbl: (0, h))
    out = pl.BlockSpec((bq, 128), lambda h, i, tbl: (i, h))
    return pl.pallas_call(
        body,
        name=name,
        grid_spec=pltpu.PrefetchScalarGridSpec(
            num_scalar_prefetch=1, grid=(4, nq),
            in_specs=[pl.BlockSpec((bq, 256), lambda h, i, tbl: (i, h)), res, res],
            out_specs=[out, out]),
        out_shape=[jax.ShapeDtypeStruct((t, A_WIDTH), BF16), jax.ShapeDtypeStruct((t, A_WIDTH), F32)],
        compiler_params=_params("parallel", "parallel"),
    )(kstart, qa, ka, va)


def _fox_bwd(qb, ka, va, dob, qend, name):
    t = qb.shape[0]
    bq = min(ATT_BLOCK, t)
    nq = t // bq
    kf = BWD_K_BLOCKS if t % (BWD_K_BLOCKS * bq) == 0 else 1
    bk = kf * bq
    nk = t // bk

    def body(qe_ref, k_ref, v_ref, q_hbm, do_hbm, dk_ref, dv_ref, cs_ref, dq_hbm, rs_hbm, q_scr, do_scr, dq_scr,
             sems):
        jb = pl.program_id(1)
        hp = pl.program_id(0)
        pair_cols = pl.ds(pl.multiple_of(hp * 256, 256), 256)

        @pl.when(jb == 0)
        def _():
            loads = [pltpu.make_async_copy(q_hbm.at[:, pair_cols], q_scr, sems.at[0]),
                     pltpu.make_async_copy(do_hbm.at[:, pair_cols], do_scr, sems.at[1])]
            for cp in loads:
                cp.start()
            dq_scr[...] = jnp.zeros_like(dq_scr)
            for cp in loads:
                cp.wait()

        i1 = [qe_ref[(2 * hp + j) * nq + kf * jb + kf - 1] + 1 for j in (0, 1)]
        both1 = jnp.minimum(i1[0], i1[1])

        def head(j, ib, dk_acc, dv_acc, diag):
            rows = pl.ds(pl.multiple_of(ib * bq, bq), bq)
            cols = slice(128 * j, 128 * (j + 1))
            qs = q_scr[rows, cols]
            dos = do_scr[rows, cols]
            kj = k_ref[:, cols]
            st = _dot(kj, qs, "nt")
            if diag is not None:
                r = lax.broadcasted_iota(jnp.int32, (bk, bq), 0)
                c = lax.broadcasted_iota(jnp.int32, (bk, bq), 1)
                st = jnp.where(c - r >= -diag, st, -jnp.inf)
            pt = jnp.exp(st)
            dst = (pt * _dot(v_ref[:, cols], dos, "nt")).astype(BF16)
            dq_scr[rows, cols] += _dot(dst, kj, "tn")
            return dk_acc + _dot(dst, qs), dv_acc + _dot(pt, dos)

        def pair(ib, carry, diag):
            return head(0, ib, carry[0], carry[1], diag) + head(1, ib, carry[2], carry[3], diag)

        carry = (jnp.zeros((bk, 128), F32),) * 4
        for d in range(kf):
            carry = pair(kf * jb + d, carry, d * bq)
        first = kf * jb + kf
        carry = lax.fori_loop(first, both1, lambda ib, c: pair(ib, c, None), carry)
        alone = [lax.fori_loop(jnp.maximum(both1, first), i1[j],
                               lambda ib, c, j=j: head(j, ib, c[0], c[1], None), carry[2 * j:2 * j + 2])
                 for j in (0, 1)]
        carry = alone[0] + alone[1]
        outs = []
        for j in (0, 1):
            spare = 64 * (1 - j)
            dk_acc, dv_acc = carry[2 * j], carry[2 * j + 1]
            outs.append((dk_acc, dv_acc, dk_acc[:, spare + 3:spare + 4]))
        msk = _head_mask((bk, 128), 0)
        dk_ref[...] = jnp.where(msk, outs[0][0], outs[1][0]).astype(BF16)
        dv_ref[...] = jnp.where(msk, outs[0][1], outs[1][1]).astype(BF16)
        cs_ref[...] = jnp.where(msk, outs[0][2], outs[1][2])

        @pl.when(jb == nk - 1)
        def _():
            first_head = _head_mask((bq, 128), 0)

            def finish(r, carry):
                rows = pl.ds(pl.multiple_of(r * bq, bq), bq)
                x0, x1 = dq_scr[rows, 0:128], dq_scr[rows, 128:256]
                q_scr[rows, 0:128] = (jnp.where(first_head, x0, x1) * 0.125).astype(BF16)
                dq_scr[rows, 0:128] = jnp.where(first_head, x0[:, 64:65], x1[:, 0:1])
                return carry

            lax.fori_loop(0, nq, finish, 0)
            head_cols = pl.ds(pl.multiple_of(hp * 128, 128), 128)
            stores = [pltpu.make_async_copy(q_scr.at[:, 0:128], dq_hbm.at[:, head_cols], sems.at[0]),
                      pltpu.make_async_copy(dq_scr.at[:, 0:128], rs_hbm.at[:, head_cols], sems.at[1])]
            for cp in stores:
                cp.start()
            for cp in stores:
                cp.wait()

    blk = pl.BlockSpec((bk, 256), lambda h, i, tbl: (i, h))
    out = pl.BlockSpec((bk, 128), lambda h, i, tbl: (i, h))
    return pl.pallas_call(
        body,
        name=name,
        grid_spec=pltpu.PrefetchScalarGridSpec(
            num_scalar_prefetch=1, grid=(4, nk), in_specs=[blk, blk, _ANY, _ANY],
            out_specs=[out, out, out, _ANY, _ANY],
            scratch_shapes=[pltpu.VMEM((t, 256), BF16), pltpu.VMEM((t, 256), BF16), pltpu.VMEM((t, 256), F32),
                            pltpu.SemaphoreType.DMA((2,))]),
        out_shape=[jax.ShapeDtypeStruct((t, A_WIDTH), BF16), jax.ShapeDtypeStruct((t, A_WIDTH), BF16),
                   jax.ShapeDtypeStruct((t, A_WIDTH), F32), jax.ShapeDtypeStruct((t, A_WIDTH), BF16),
                   jax.ShapeDtypeStruct((t, A_WIDTH), F32)],
        compiler_params=_params("arbitrary", "arbitrary"),
    )(qend, ka, va, qb, dob)


HG_ROWS = 256


def _hg_gates(hb_ref, rows, lbv):
    qb = hb_ref[rows, 0:B_WIDTH]
    fb = hb_ref[rows, B_WIDTH:2 * B_WIDTH]
    v = hb_ref[rows, 2 * B_WIDTH:3 * B_WIDTH]
    gb = hb_ref[rows, 3 * B_WIDTH:4 * B_WIDTH]
    sg = _sigmoid(fb)
    f = lbv + (1.0 - lbv) * sg
    sq = _sigmoid(qb)
    return qb, sq, qb * sq, sg, f, 1.0 - f, jnp.log(f), v, gb


def _hg_intra_factors(q, k, b):
    fac = []
    for i in range(CHUNK // SUB):
        bi = b[SUB * i:SUB * i + 1, :]
        eq = jnp.exp(b[SUB * i:SUB * (i + 1), :] - bi)
        ek = jnp.exp(jnp.minimum(bi - b, EXP_CLAMP))
        fac.append((eq, ek, q[SUB * i:SUB * (i + 1), :] * eq, k * ek))
    return fac


def _causal(n):
    r = lax.broadcasted_iota(jnp.int32, (n, n), 0)
    c = lax.broadcasted_iota(jnp.int32, (n, n), 1)
    return r >= c


def _hgrn_fwd(rest, lb, ng, name, ride=()):
    t = rest.shape[0]
    bt = min(HG_ROWS, t)
    ncb = bt // CHUNK
    n = len(ride)
    nsteps = t // bt

    def body(hb_ref, lb_ref, ng_ref, *refs):
        ride_in, (y_ref, o_ref, st_ref), ride_out = refs[:n], refs[n:n + 3], refs[n + 3:2 * n + 3]
        s_scr, sems = refs[2 * n + 3], refs[2 * n + 4:]

        @pl.when(pl.program_id(0) == 0)
        def _():
            s_scr[...] = jnp.zeros_like(s_scr)
            if n:
                _gather_start(ride_in, ride_out, sems)

        tril = _tri(CHUNK, True)
        causal = _causal(CHUNK)
        ones = jnp.ones((CHUNK, HD), F32)

        def chunk(c, carry):
            rows = pl.ds(pl.multiple_of(c * CHUNK, CHUNK), CHUNK)
            _, _, q_all, _, _, k_all, g_all, v_all, gb_all = _hg_gates(hb_ref, rows, lb_ref[...])
            b_all = _dot_hi(tril, g_all)
            qd_all = q_all * jnp.exp(b_all)
            kd_all = k_all * jnp.exp(b_all[CHUNK - 1:CHUNK, :] - b_all)
            eb_all = jnp.exp(_dot_hi(g_all, ones, "tn", exact="b"))
            sgb_all = _sigmoid(gb_all)
            for h in range(B_HEADS):
                cols = slice(h * HD, (h + 1) * HD)
                v = v_all[:, cols]
                s0 = s_scr[h]
                st_ref[c, h] = s0
                o = _dot(qd_all[:, cols], s0)
                fac = _hg_intra_factors(q_all[:, cols], k_all[:, cols], b_all[:, cols])
                a = jnp.concatenate([_dot(qe, ke, "nt") for _, _, qe, ke in fac], axis=0)
                o = o + _dot(jnp.where(causal, a, 0.0), v)
                s_scr[h] = eb_all[h * HD:(h + 1) * HD, :] * s0 + _dot(kd_all[:, cols], v, "tn")
                r = lax.rsqrt(jnp.mean(o * o, axis=-1, keepdims=True) + RMS_EPS)
                o_ref[rows, cols] = o
                y_ref[rows, cols] = (o * r * ng_ref[...] * sgb_all[:, cols]).astype(BF16)
            return carry

        lax.fori_loop(0, ncb, chunk, 0, unroll=True)

        if n:
            @pl.when(pl.program_id(0) == nsteps - 1)
            def _():
                _gather_finish(ride_in, ride_out, sems)

    res = pl.pallas_call(
        body,
        name=name,
        grid=(nsteps,),
        in_specs=[pl.BlockSpec((bt, 4 * B_WIDTH), lambda i: (i, 0)), pl.BlockSpec((1, B_WIDTH), lambda i: (0, 0)),
                  pl.BlockSpec((1, HD), lambda i: (0, 0))] + [_ANY] * n,
        out_specs=[pl.BlockSpec((bt, B_WIDTH), lambda i: (i, 0)), pl.BlockSpec((bt, B_WIDTH), lambda i: (i, 0)),
                   pl.BlockSpec((ncb, B_HEADS, HD, HD), lambda i: (i, 0, 0, 0))] + [_ANY] * n,
        out_shape=[jax.ShapeDtypeStruct((t, B_WIDTH), BF16), jax.ShapeDtypeStruct((t, B_WIDTH), F32),
                   jax.ShapeDtypeStruct((t // CHUNK, B_HEADS, HD, HD), F32)] + _gather_out_shapes(ride),
        scratch_shapes=[pltpu.VMEM((B_HEADS, HD, HD), F32)] + (_gather_scratch(n) if n else []),
        compiler_params=_params("arbitrary"),
    )(rest, lb, ng, *ride)
    return res[:3], res[3:]


def _hgrn_bwd(dy, rest, o_saved, states, lb, ng, name):
    t = rest.shape[0]
    bt = min(HG_ROWS, t)
    ncb = bt // CHUNK
    nb = t // bt

    def body(dy_ref, hb_ref, o_ref, st_ref, lb_ref, ng_ref, dh_ref, dlb_ref, dng_ref, ds_scr):
        @pl.when(pl.program_id(0) == 0)
        def _():
            ds_scr[...] = jnp.zeros_like(ds_scr)
            dlb_ref[...] = jnp.zeros_like(dlb_ref)
            dng_ref[...] = jnp.zeros_like(dng_ref)

        tril = _tri(CHUNK, True)
        triu = _tri(CHUNK, False)
        causal = _causal(CHUNK)
        ones = jnp.ones((CHUNK, HD), F32)
        ones8 = jnp.ones((8, HD), F32)
        last_row = lax.broadcasted_iota(jnp.int32, (CHUNK, B_WIDTH), 0) == CHUNK - 1

        def chunk(cc, carry):
            dng_acc, dlb_acc = carry
            c = ncb - 1 - cc
            rows = pl.ds(pl.multiple_of(c * CHUNK, CHUNK), CHUNK)
            lbv = lb_ref[...]
            qb, sq, q_all, sg, f, k_all, g_all, v_all, gb = _hg_gates(hb_ref, rows, lbv)
            b_all = _dot_hi(tril, g_all)
            ebt_all = jnp.exp(b_all)
            blast = b_all[CHUNK - 1:CHUNK, :]
            ekd_all = jnp.exp(blast - b_all)
            eb_all = jnp.exp(_dot_hi(g_all, ones, "tn", exact="b"))
            sgb = _sigmoid(gb)
            dy_all = dy_ref[rows, :].astype(F32)
            don_all = dy_all * sgb
            ngv = ng_ref[...]
            dq_l, dk_l, dks_l, dv_l, on_l, prod_l = [], [], [], [], [], []
            for h in range(B_HEADS):
                cols = slice(h * HD, (h + 1) * HD)
                q, k, v = q_all[:, cols], k_all[:, cols], v_all[:, cols]
                o = o_ref[rows, cols]
                don = don_all[:, cols]
                r = lax.rsqrt(jnp.mean(o * o, axis=-1, keepdims=True) + RMS_EPS)
                on_l.append(o * r * ngv)
                dng_acc = dng_acc + jnp.sum(don * o * r, axis=0, keepdims=True)
                doh = don * ngv
                do = r * (doh - o * (r * r) * jnp.mean(doh * o, axis=-1, keepdims=True))
                ebt, ekd = ebt_all[:, cols], ekd_all[:, cols]
                s0 = st_ref[c, h]
                ds1 = ds_scr[h]
                fac = _hg_intra_factors(q, k, b_all[:, cols])
                a = jnp.concatenate([_dot(qe, ke, "nt") for _, _, qe, ke in fac], axis=0)
                a = jnp.where(causal, a, 0.0)
                da = jnp.where(causal, _dot(do, v, "nt"), 0.0)
                dv_l.append(_dot(a, do, "tn") + _dot(k * ekd, ds1))
                dq = ebt * _dot(do, s0, "nt")
                dq_l.append(dq + jnp.concatenate(
                    [eq * _hdot(da[SUB * i:SUB * (i + 1), :], ke) for i, (eq, _, _, ke) in enumerate(fac)], axis=0))
                dk_state = ekd * _dot(v, ds1, "nt")
                dk = dk_state
                for i, (_, ek, qe, _) in enumerate(fac):
                    dk = dk + ek * _hdot(da[SUB * i:SUB * (i + 1), :], qe, "tn")
                dk_l.append(dk)
                dks_l.append(dk_state)
                prod_l.append(ds1 * s0)
                ds_scr[h] = _dot(q * ebt, do, "tn") + eb_all[h * HD:(h + 1) * HD, :] * ds1
            dq_all, dk_all = jnp.concatenate(dq_l, axis=1), jnp.concatenate(dk_l, axis=1)
            extra = jnp.exp(blast) * _dot_hi(ones8, jnp.concatenate(prod_l, axis=0), "nt")[0:1, :] \
                + jnp.sum(k_all * jnp.concatenate(dks_l, axis=1), axis=0, keepdims=True)
            db = q_all * dq_all - k_all * dk_all + jnp.where(last_row, extra, 0.0)
            df = _dot_hi(triu, db) / f - dk_all
            dlb_acc = dlb_acc + jnp.sum(df * (1.0 - sg), axis=0, keepdims=True)
            dh_ref[rows, 0:B_WIDTH] = (dq_all * (sq * (1.0 + qb * (1.0 - sq)))).astype(BF16)
            dh_ref[rows, B_WIDTH:2 * B_WIDTH] = (df * (1.0 - lbv) * sg * (1.0 - sg)).astype(BF16)
            dh_ref[rows, 2 * B_WIDTH:3 * B_WIDTH] = jnp.concatenate(dv_l, axis=1).astype(BF16)
            dh_ref[rows, 3 * B_WIDTH:4 * B_WIDTH] = (dy_all * jnp.concatenate(on_l, axis=1)
                                                     * sgb * (1.0 - sgb)).astype(BF16)
            return dng_acc, dlb_acc

        dng_sum, dlb_sum = lax.fori_loop(0, ncb, chunk, (jnp.zeros((1, HD), F32), jnp.zeros((1, B_WIDTH), F32)),
                                         unroll=True)
        dng_ref[...] += dng_sum
        dlb_ref[...] += dlb_sum

    rev = lambda i: (nb - 1 - i, 0)
    return pl.pallas_call(
        body,
        name=name,
        grid=(nb,),
        in_specs=[pl.BlockSpec((bt, B_WIDTH), rev), pl.BlockSpec((bt, 4 * B_WIDTH), rev),
                  pl.BlockSpec((bt, B_WIDTH), rev),
                  pl.BlockSpec((ncb, B_HEADS, HD, HD), lambda i: (nb - 1 - i, 0, 0, 0)),
                  pl.BlockSpec((1, B_WIDTH), lambda i: (0, 0)), pl.BlockSpec((1, HD), lambda i: (0, 0))],
        out_specs=[pl.BlockSpec((bt, 4 * B_WIDTH), rev), pl.BlockSpec((1, B_WIDTH), lambda i: (0, 0)),
                   pl.BlockSpec((1, HD), lambda i: (0, 0))],
        out_shape=[jax.ShapeDtypeStruct((t, 4 * B_WIDTH), BF16), jax.ShapeDtypeStruct((1, B_WIDTH), F32),
                   jax.ShapeDtypeStruct((1, HD), F32)],
        scratch_shapes=[pltpu.VMEM((B_HEADS, HD, HD), F32)],
        compiler_params=_params("arbitrary"),
    )(dy, rest, o_saved, states, lb, ng)


def _split_w_in(w_in_l):
    wqkv = w_in_l[:, :3 * A_WIDTH]
    wfa = jnp.pad(w_in_l[:, 3 * A_WIDTH:3 * A_WIDTH + A_HEADS], ((0, 0), (0, 128 - A_HEADS)))
    whb = w_in_l[:, 3 * A_WIDTH + A_HEADS:3 * A_WIDTH + A_HEADS + 4 * B_WIDTH]
    wgt = w_in_l[:, 3 * A_WIDTH + A_HEADS + 4 * B_WIDTH:]
    return wqkv, wgt, jnp.concatenate([whb, wfa], axis=1), (jnp.concatenate([wqkv, wfa], axis=1), wgt, whb)


def _merge_w_in_grad(d_att, d_gates, d_hb):
    o = 3 * A_WIDTH
    return jnp.concatenate([d_att[:, :o + A_HEADS], d_hb, d_gates], axis=1)


def _layer_fwd(x, xb, w, sp, l, ride=(), late_weights=None):
    n = f"l{l}_"
    wqkv, wgates, wrest, wgroups = _split_w_in(w["w_in"])
    qkv = _matmul(xb, wqkv, "nn", BF16, MM_ROWS, 768, D_MODEL, n + "proj_qkv")
    gates = _matmul(xb, wgates, "nn", BF16, MM_ROWS, 1024, D_MODEL, n + "proj_gates")
    rest = _matmul(xb, wrest, "nn", F32, MM_ROWS, 4 * B_WIDTH + 128, D_MODEL, n + "proj_rest")
    bf = jnp.pad(sp["b_fgate"], (0, 128 - A_HEADS)).reshape(1, 128)
    fcum, fcol = _fox_gate_fwd(rest, bf, n + "fox_gate_fwd")
    qa, ka, va, qn, kn = _fox_prep_fwd(qkv, fcol, n + "fox_prep_fwd")
    kstart, qend = _fox_block_ranges(qn, kn, fcum)
    ya, lse = _fox_fwd(qa, ka, va, kstart, n + "fox_fwd")
    lb = sp["lb"].reshape(1, B_WIDTH)
    ng = sp["norm_g"].reshape(1, HD)
    (yb, ob, states), gathered = _hgrn_fwd(rest, lb, ng, n + "hgrn_fwd", ride)
    if ride:
        late = late_weights(l, gathered)
        w = {**w, **late[l]}
    merged = _merge_fwd(ya, yb, w["w_pa"], w["w_pb"], gates, n + "merge_fwd")
    x1, x1b, xh1, rs1 = _mm_res_ln(merged, w["w_out"], x, sp["ln1_g"], sp["ln1_b"], n + "out_ln1")
    wu, wg = w["w_ff_in"][:, :FFN_HIDDEN], w["w_ff_in"][:, FFN_HIDDEN:]
    a, hu, hg = _ffn_in_swiglu(x1b, wu, wg, n + "ffn_in_swiglu")
    x2, x2b, xh2, rs2 = _mm_res_ln(a, w["w_ff_out"], x1, sp["ln2_g"], sp["ln2_b"], n + "ffn_out_ln2")
    saved = dict(xb=xb, wgroups=wgroups, qkv=qkv, rest=rest, gates=gates, bf=bf, fcol=fcol, ka=ka, va=va, ya=ya, lse=lse,
                 qend=qend,
                 lb=lb, ng=ng, yb=yb, ob=ob, states=states, merged=merged, x1b=x1b, xh1=xh1, rs1=rs1, a=a,
                 wu=wu, wg=wg, hu=hu, hg=hg,
                 xh2=xh2, rs2=rs2)
    return x2, x2b, saved, (late if ride else None)


def _layer_bwd(dys, coefs, w, sp, s, l):
    n = f"l{l}_"
    dz2, dz2b, dg2, db2 = _ln_bwd(dys, coefs, s["xh2"], s["rs2"], sp["ln2_g"], n + "ln2_bwd")
    du, dg = _ffn_out_dx_swiglu(dz2b, w["w_ff_out"], s["hu"], s["hg"], n + "ffn_out_dx_swiglu")
    d_wffout = _matmul(s["a"], dz2b, "tn", F32, 1408, 1024, DW_ROWS, n + "ffn_out_dw")
    dx1 = _matmul_nt_sum([du, dg], [s["wu"], s["wg"]], n + "ffn_in_dx")
    d_wffin = jnp.concatenate([_matmul(s["x1b"], du, "tn", F32, 1024, 1408, DW_ROWS, n + "ffn_in_dw_u"),
                               _matmul(s["x1b"], dg, "tn", F32, 1024, 1408, DW_ROWS, n + "ffn_in_dw_g")], axis=1)
    dz1, dz1b, dg1, db1 = _ln_bwd([dz2, dx1], [ALPHA, 1.0], s["xh1"], s["rs1"], sp["ln1_g"], n + "ln1_bwd")
    d_wout = _matmul(s["merged"], dz1b, "tn", F32, 1024, 1024, DW_ROWS, n + "out_dw")
    dgates, dpa, dpb, dya, dyb = _merge_bwd(dz1b, w["w_out"], s["ya"], s["yb"], w["w_pa"], w["w_pb"], s["gates"],
                                  n + "merge_bwd")
    d_wpa = _matmul(s["ya"], dpa, "tn", F32, 512, 1024, DW_ROWS, n + "pa_dw")
    d_wpb = _matmul(s["yb"], dpb, "tn", F32, 512, 1024, DW_ROWS, n + "pb_dw")
    qb, dob = _fox_prep_bwd(s["qkv"], s["fcol"], s["lse"], dya, s["ya"], n + "fox_prep_bwd")
    dk, dv, csum, dq, rsum = _fox_bwd(qb, s["ka"], s["va"], dob, s["qend"], n + "fox_bwd")
    dfa, dbf = _fox_gate_bwd(rsum, csum, s["rest"], s["bf"], n + "fox_gate_bwd")
    dhb, dlb, dng = _hgrn_bwd(dyb, s["rest"], s["ob"], s["states"], s["lb"], s["ng"], n + "hgrn_bwd")
    datt = jnp.concatenate([dq, dk, dv, dfa], axis=1)
    dxp = _matmul_nt_sum([datt, dgates, dhb], list(s["wgroups"]), n + "proj_dx", (ALPHA, dz1) if l == 0 else None)
    d_w_in = _merge_w_in_grad(_matmul(s["xb"], datt, "tn", F32, 1024, 1664, DW_ROWS, n + "proj_dw_att"),
                              _matmul(s["xb"], dgates, "tn", F32, 1024, 1024, DW_ROWS, n + "proj_dw_gates"),
                              _matmul(s["xb"], dhb, "tn", F32, 1024, 1024, DW_ROWS, n + "proj_dw_hgrn"))
    grads = dict(w_in=d_w_in, w_pa=d_wpa, w_pb=d_wpb, w_out=d_wout, w_ff_in=d_wffin,
                 w_ff_out=d_wffout, b_fgate=dbf[0, :A_HEADS], lb=dlb[0], norm_g=dng[0], ln1_g=dg1[0], ln1_b=db1[0],
                 ln2_g=dg2[0], ln2_b=db2[0])
    if l == 0:
        return [dxp], [1.0], grads
    return [dz1, dxp], [ALPHA, 1.0], grads


def _lower_bounds(logits):
    sm = jax.nn.softmax(logits.astype(F32), axis=0)
    return jnp.cumsum(sm, axis=0) - sm[0:1]


def _local_step(x, target, wfull, small, rides=None, late_weights=None):
    lbs, lb_vjp = jax.vjp(_lower_bounds, small["hgrn_lb_logits"])
    h, hb = x, x.astype(BF16)
    wfull = list(wfull)
    saved, sps = [], []
    for l in range(DEPTH):
        sp = dict(b_fgate=small["b_fgate"][l], lb=lbs[l], norm_g=small["hgrn_norm_g"][l], ln1_g=small["ln1_g"][l],
                  ln1_b=small["ln1_b"][l], ln2_g=small["ln2_g"][l], ln2_b=small["ln2_b"][l])
        h, hb, s, late = _layer_fwd(h, hb, wfull[l], sp, l, rides[l] if rides else (), late_weights)
        if late is not None:
            wfull = [{**wfull[k], **late[k]} for k in range(DEPTH)]
        saved.append(s)
        sps.append(sp)
    dy, lpart = _loss_head(h, target)
    dys, coefs = [dy], [1.0]
    grads = [None] * DEPTH
    for l in reversed(range(DEPTH)):
        dys, coefs, grads[l] = _layer_bwd(dys, coefs, wfull[l], sps[l], saved[l], l)
    grad_x = dys[0]
    d_logits = lb_vjp(jnp.stack([grads[l]["lb"] for l in range(DEPTH)]))[0]
    return lpart[0, 0], grad_x, grads, d_logits


_BIG = [("w_in", "w_in", (D_MODEL, IN_TOTAL), 1), ("w_branch_a", "w_pa", (A_WIDTH, D_MODEL), 1),
        ("w_branch_b", "w_pb", (B_WIDTH, D_MODEL), 1), ("w_out", "w_out", (D_MODEL, D_MODEL), 0),
        ("w_ff_in", "w_ff_in", (D_MODEL, 2 * FFN_HIDDEN), 1), ("w_ff_out", "w_ff_out", (FFN_HIDDEN, D_MODEL), 0)]
_SMALL = [("b_fgate", A_HEADS), ("hgrn_lb_logits", B_WIDTH), ("hgrn_norm_g", HD), ("ln1_g", D_MODEL),
          ("ln1_b", D_MODEL), ("ln2_g", D_MODEL), ("ln2_b", D_MODEL)]
N_BIG = len(_BIG)
SMALL_ROWS = 80


def _by_chip(full, axis):
    if axis == 0:
        return full.reshape(N_CHIPS, full.shape[0] // N_CHIPS, full.shape[1])
    n = full.shape[1] // N_CHIPS
    return jnp.stack([full[:, q * n:(q + 1) * n] for q in range(N_CHIPS)])


def _from_chips(shards, axis):
    if axis == 0:
        return shards.reshape(N_CHIPS * shards.shape[1], shards.shape[2])
    return jnp.concatenate([shards[q] for q in range(N_CHIPS)], axis=1)


def _pack_small(per_name):
    flat = jnp.concatenate([per_name[name].reshape(-1) for name, _ in _SMALL])
    return jnp.pad(flat, (0, SMALL_ROWS * 128 - flat.shape[0])).reshape(SMALL_ROWS, 128)


def _unpack_small(slab):
    flat, out, r = slab.reshape(-1), {}, 0
    for name, n in _SMALL:
        out[name] = flat[r:r + DEPTH * n].reshape(DEPTH, n)
        r += DEPTH * n
    return out


_ANY = pl.BlockSpec(memory_space=pl.ANY)


def _place():
    return lax.axis_index("x"), lax.axis_index("y"), lax.axis_index("c")


def _other_chips(x, y):
    return [(1 - x, y), (x, 1 - y), (1 - x, 1 - y)]


def _chip_exchange(mine_of, out_refs, send_sems, recv_sems, local_sems):
    _chip_exchange_start(mine_of, out_refs, send_sems, recv_sems, local_sems)
    _chip_exchange_wait(mine_of, out_refs, send_sems, recv_sems, local_sems)


def _chip_exchange_copies(mine_of, out_refs, send_sems, recv_sems, local_sems):
    x, y, c = _place()
    q = 2 * x + y
    local = [pltpu.make_async_copy(mine_of(w, q), out_ref.at[q], local_sems.at[w]) for w, out_ref in enumerate(out_refs)]
    sends, recvs = [], []
    for k, (px, py) in enumerate(_other_chips(x, y)):
        for w, out_ref in enumerate(out_refs):
            sems = dict(send_sem=send_sems.at[3 * w + k], recv_sem=recv_sems.at[3 * w + k], device_id=(px, py, c),
                        device_id_type=MESH)
            sends.append(pltpu.make_async_remote_copy(src_ref=mine_of(w, 2 * px + py), dst_ref=out_ref.at[q], **sems))
            recvs.append(pltpu.make_async_remote_copy(src_ref=mine_of(w, q), dst_ref=out_ref.at[2 * px + py], **sems))
    return local, sends, recvs


def _chip_exchange_start(*args):
    local, sends, _ = _chip_exchange_copies(*args)
    for cp in local + sends:
        cp.start()


def _chip_exchange_wait(*args):
    local, sends, recvs = _chip_exchange_copies(*args)
    for cp in recvs:
        cp.wait_recv()
    for cp in sends:
        cp.wait_send()
    for cp in local:
        cp.wait()


def _sem_scratch(n):
    return [pltpu.SemaphoreType.DMA((3 * n,)), pltpu.SemaphoreType.DMA((3 * n,)), pltpu.SemaphoreType.DMA((n,))]


def _gather_scratch(n):
    return _sem_scratch(n) + [pltpu.SemaphoreType.DMA((n,)), pltpu.SemaphoreType.DMA((n,))]


def _gather_out_shapes(mine):
    return [jax.ShapeDtypeStruct((DEPTH, N_CHIPS) + m.shape[1:], m.dtype) for m in mine]


def _gather_start(in_refs, out_refs, sems):
    c = lax.axis_index("c")
    _chip_exchange_start(lambda w, q: in_refs[w].at[c], [o.at[c] for o in out_refs], *sems[:3])


def _gather_finish(in_refs, out_refs, sems):
    x, y, c = _place()
    _chip_exchange_wait(lambda w, q: in_refs[w].at[c], [o.at[c] for o in out_refs], *sems[:3])
    pair_send, pair_recv = sems[3:]
    sibling = (x, y, 1 - c)
    fwds = []
    for w, o in enumerate(out_refs):
        cp = pltpu.make_async_remote_copy(src_ref=o.at[c], dst_ref=o.at[c], send_sem=pair_send.at[w],
                                          recv_sem=pair_recv.at[w], device_id=sibling, device_id_type=MESH)
        cp.start()
        fwds.append(cp)
    for w, o in enumerate(out_refs):
        pltpu.make_async_remote_copy(src_ref=o.at[1 - c], dst_ref=o.at[1 - c], send_sem=pair_send.at[w],
                                     recv_sem=pair_recv.at[w], device_id=sibling, device_id_type=MESH).wait_recv()
    for cp in fwds:
        cp.wait_send()


def _gather_weights(mine):
    n = len(mine)

    def body(*refs):
        in_refs, out_refs, sems = refs[:n], refs[n:2 * n], refs[2 * n:]
        _gather_start(in_refs, out_refs, sems)
        _gather_finish(in_refs, out_refs, sems)

    return pl.pallas_call(
        body, name="gather_weights", in_specs=[_ANY] * n, out_specs=[_ANY] * n,
        out_shape=_gather_out_shapes(mine), scratch_shapes=_gather_scratch(n),
    )(*mine)


def _pair_exchange(gs):
    n = len(gs)

    def body(*refs):
        g_refs, a_refs, send_sems, recv_sems = refs[:n], refs[n:2 * n], refs[2 * n], refs[2 * n + 1]
        x, y, c = _place()
        cps = []
        for w in range(n):
            cp = pltpu.make_async_remote_copy(src_ref=g_refs[w].at[1 - c], dst_ref=a_refs[w], send_sem=send_sems.at[w],
                                              recv_sem=recv_sems.at[w], device_id=(x, y, 1 - c), device_id_type=MESH)
            cp.start()
            cps.append(cp)
        for cp in cps:
            cp.wait()

    return pl.pallas_call(
        body, name="grad_pair_exchange", in_specs=[_ANY] * n, out_specs=[_ANY] * n,
        out_shape=[jax.ShapeDtypeStruct(g.shape[1:], g.dtype) for g in gs],
        scratch_shapes=[pltpu.SemaphoreType.DMA((n,)), pltpu.SemaphoreType.DMA((n,))],
    )(*gs)


def _row_block(rows):
    return math.gcd(rows, 256)


def _pair_sum(g, a, layer, name):
    _, nq, rows, cols = g.shape
    tb = _row_block(rows)

    def body(l_ref, g_ref, a_ref, o_ref):
        o_ref[...] = (g_ref[...].astype(F32) + a_ref[...].astype(F32)).astype(BF16)

    return pl.pallas_call(
        body, name=name,
        grid_spec=pltpu.PrefetchScalarGridSpec(
            num_scalar_prefetch=1, grid=(nq, rows // tb),
            in_specs=[pl.BlockSpec((None, None, tb, cols), lambda q, i, l_ref: (l_ref[0], q, i, 0)),
                      pl.BlockSpec((None, tb, cols), lambda q, i, l_ref: (q, i, 0))],
            out_specs=pl.BlockSpec((None, tb, cols), lambda q, i, l_ref: (q, i, 0))),
        out_shape=jax.ShapeDtypeStruct((nq, rows, cols), BF16),
        compiler_params=_params("parallel", "parallel"),
    )(layer.reshape(1).astype(jnp.int32), g, a)


def _shard_exchange(ps):
    n = len(ps)

    def body(*refs):
        p_refs, b_refs = refs[:n], refs[n:2 * n]
        send_sems, recv_sems, local_sems = refs[2 * n:]
        _chip_exchange(lambda w, q: p_refs[w].at[q], b_refs, send_sems, recv_sems, local_sems)

    return pl.pallas_call(
        body, name="grad_shard_exchange", in_specs=[_ANY] * n, out_specs=[_ANY] * n,
        out_shape=[jax.ShapeDtypeStruct(p.shape, p.dtype) for p in ps],
        scratch_shapes=_sem_scratch(n),
    )(*ps)


def _sum4(b, name):
    _, rows, cols = b.shape
    tb = _row_block(rows)

    def body(b_ref, o_ref):
        o_ref[...] = ((b_ref[0].astype(F32) + b_ref[1].astype(F32)) + b_ref[2].astype(F32)) + b_ref[3].astype(F32)

    return pl.pallas_call(
        body, name=name, grid=(rows // tb,),
        in_specs=[pl.BlockSpec((N_CHIPS, tb, cols), lambda i: (0, i, 0))],
        out_specs=pl.BlockSpec((tb, cols), lambda i: (i, 0)),
        out_shape=jax.ShapeDtypeStruct((rows, cols), F32),
        compiler_params=_params("parallel"),
    )(b)


def _result_exchange(gcs):
    n = len(gcs)

    def body(*refs):
        g_refs, o_refs, send_sems, recv_sems = refs[:n], refs[n:2 * n], refs[2 * n], refs[2 * n + 1]
        x, y, c = _place()
        cps = []
        for w in range(n):
            cp = pltpu.make_async_remote_copy(src_ref=g_refs[w], dst_ref=o_refs[w], send_sem=send_sems.at[w],
                                              recv_sem=recv_sems.at[w], device_id=(x, y, 1 - c), device_id_type=MESH)
            cp.start()
            cps.append(cp)
        for cp in cps:
            cp.wait()

    return pl.pallas_call(
        body, name="grad_result_exchange", in_specs=[_ANY] * n, out_specs=[_ANY] * n,
        out_shape=[jax.ShapeDtypeStruct(g.shape, g.dtype) for g in gcs],
        scratch_shapes=[pltpu.SemaphoreType.DMA((n,)), pltpu.SemaphoreType.DMA((n,))],
    )(*gcs)


def _allreduce_small(v):
    def body(v_ref, o_ref, buf, send_sems, recv_sems):
        x, y, c = _place()
        me = 4 * x + 2 * y + c
        buf[me] = v_ref[...]
        peers = []
        for k in range(1, N_DEV):
            px = 1 - x if k & 4 else x
            py = 1 - y if k & 2 else y
            pc = 1 - c if k & 1 else c
            peers.append((px, py, pc))
        sends = []
        for k, peer in enumerate(peers):
            cp = pltpu.make_async_remote_copy(src_ref=v_ref, dst_ref=buf.at[me], send_sem=send_sems.at[k],
                                              recv_sem=recv_sems.at[k], device_id=peer, device_id_type=MESH)
            cp.start()
            sends.append(cp)
        for k, (px, py, pc) in enumerate(peers):
            pltpu.make_async_remote_copy(src_ref=v_ref, dst_ref=buf.at[4 * px + 2 * py + pc], send_sem=send_sems.at[k],
                                         recv_sem=recv_sems.at[k], device_id=(px, py, pc),
                                         device_id_type=MESH).wait_recv()
        for cp in sends:
            cp.wait_send()
        acc = buf[0]
        for i in range(1, N_DEV):
            acc = acc + buf[i]
        o_ref[...] = acc

    vm = pl.BlockSpec(memory_space=pltpu.VMEM)
    return pl.pallas_call(
        body, name="small_allreduce", in_specs=[vm], out_specs=vm,
        out_shape=jax.ShapeDtypeStruct(v.shape, F32),
        scratch_shapes=[pltpu.VMEM((N_DEV,) + v.shape, F32), pltpu.SemaphoreType.DMA((N_DEV - 1,)),
                        pltpu.SemaphoreType.DMA((N_DEV - 1,))],
    )(v)


def _adam_update(w, g, m, v):
    nm = ADAM_B1 * m + (1.0 - ADAM_B1) * g
    nv = ADAM_B2 * v + (1.0 - ADAM_B2) * (g * g)
    m_hat = nm / (1.0 - ADAM_B1 ** ADAM_STEP)
    v_hat = nv / (1.0 - ADAM_B2 ** ADAM_STEP)
    return -ADAM_LR * (m_hat / (jnp.sqrt(v_hat) + ADAM_EPS) + ADAM_WD * w), nm, nv


def _adamw_small(w, g, m, v, name):
    def body(w_ref, g_ref, m_ref, v_ref, d_ref, nm_ref, nv_ref):
        d_ref[...], nm_ref[...], nv_ref[...] = _adam_update(w_ref[...], g_ref[...], m_ref[...], v_ref[...])

    vm = pl.BlockSpec(memory_space=pltpu.VMEM)
    return pl.pallas_call(
        body, name=name, in_specs=[vm] * 4, out_specs=[vm] * 3,
        out_shape=[jax.ShapeDtypeStruct(w.shape, F32)] * 3,
    )(w, g, m, v)


def _adamw_big(w, m, v, g_own, g_other, layer, name):
    _, rows, cols = w.shape
    tb = _row_block(rows)

    def body(l_ref, w_ref, m_ref, v_ref, go_ref, gx_ref, g_ref, d_ref, nm_ref, nv_ref):
        gv = jnp.where(pl.program_id(0) == l_ref[0], go_ref[...], gx_ref[...])
        g_ref[...] = gv
        d_ref[...], nm_ref[...], nv_ref[...] = _adam_update(w_ref[...], gv, m_ref[...], v_ref[...])

    per_layer = pl.BlockSpec((None, tb, cols), lambda l, i, l_ref: (l, i, 0))
    shared = pl.BlockSpec((tb, cols), lambda l, i, l_ref: (i, 0))
    return pl.pallas_call(
        body, name=name,
        grid_spec=pltpu.PrefetchScalarGridSpec(
            num_scalar_prefetch=1, grid=(DEPTH, rows // tb),
            in_specs=[per_layer, per_layer, per_layer, shared, shared], out_specs=[per_layer] * 4),
        out_shape=[jax.ShapeDtypeStruct(w.shape, F32)] * 4,
        compiler_params=_params("parallel", "parallel"),
    )(layer.reshape(1).astype(jnp.int32), w, m, v, g_own, g_other)


def kernel(x, w_in, b_fgate, hgrn_lb_logits, hgrn_norm_g, w_branch_a, w_branch_b, w_out, ln1_g, ln1_b, w_ff_in, w_ff_out, ln2_g, ln2_b, loss_target, m_w_in, m_b_fgate, m_hgrn_lb_logits, m_hgrn_norm_g, m_w_branch_a, m_w_branch_b, m_w_out, m_ln1_g, m_ln1_b, m_w_ff_in, m_w_ff_out, m_ln2_g, m_ln2_b, v_w_in, v_b_fgate, v_hgrn_lb_logits, v_hgrn_norm_g, v_w_branch_a, v_w_branch_b, v_w_out, v_ln1_g, v_ln1_b, v_w_ff_in, v_w_ff_out, v_ln2_g, v_ln2_b):
    weights = dict(w_in=w_in, b_fgate=b_fgate, hgrn_lb_logits=hgrn_lb_logits, hgrn_norm_g=hgrn_norm_g,
                   w_branch_a=w_branch_a, w_branch_b=w_branch_b, w_out=w_out, ln1_g=ln1_g, ln1_b=ln1_b,
                   w_ff_in=w_ff_in, w_ff_out=w_ff_out, ln2_g=ln2_g, ln2_b=ln2_b)
    mom1 = dict(w_in=m_w_in, b_fgate=m_b_fgate, hgrn_lb_logits=m_hgrn_lb_logits, hgrn_norm_g=m_hgrn_norm_g,
                w_branch_a=m_w_branch_a, w_branch_b=m_w_branch_b, w_out=m_w_out, ln1_g=m_ln1_g, ln1_b=m_ln1_b,
                w_ff_in=m_w_ff_in, w_ff_out=m_w_ff_out, ln2_g=m_ln2_g, ln2_b=m_ln2_b)
    mom2 = dict(w_in=v_w_in, b_fgate=v_b_fgate, hgrn_lb_logits=v_hgrn_lb_logits, hgrn_norm_g=v_hgrn_norm_g,
                w_branch_a=v_w_branch_a, w_branch_b=v_w_branch_b, w_out=v_w_out, ln1_g=v_ln1_g, ln1_b=v_ln1_b,
                w_ff_in=v_w_ff_in, w_ff_out=v_w_ff_out, ln2_g=v_ln2_g, ln2_b=v_ln2_b)
    core = lax.axis_index("c")

    def halves(shard):
        return shard.astype(BF16).reshape(2, shard.shape[0] // 2, shard.shape[1])

    def from_halves(gathered, axis):
        return _from_chips(jnp.concatenate([gathered[0], gathered[1]], axis=1), axis)

    mid, ffn = _BIG[1:4], _BIG[4:]

    def late_weights(l, gathered):
        if l == 1:
            return [{}, {key: from_halves(gathered[w], axis) for w, (_, key, _, axis) in enumerate(ffn)}]
        upd = [{key: _from_chips(gathered[1 + w][k], axis) for w, (_, key, _, axis) in enumerate(mid)}
               for k in range(DEPTH)]
        upd[1]["w_in"] = from_halves(gathered[0], 1)
        upd[0].update({key: from_halves(gathered[4 + w], axis) for w, (_, key, _, axis) in enumerate(ffn)})
        return upd

    wfull = [{"w_in": from_halves(_gather_weights([halves(weights["w_in"][0])])[0], 1)}, {}]
    rides = [[halves(weights["w_in"][1])] + [weights[name].astype(BF16) for name, _, _, _ in mid]
             + [halves(weights[name][0]) for name, _, _, _ in ffn],
             [halves(weights[name][1]) for name, _, _, _ in ffn]]
    small = {name: weights[name] for name, _ in _SMALL}

    loss_part, grad_x, grads, d_logits = _local_step(x[0], loss_target[0], wfull, small, rides, late_weights)

    g_all = [jnp.stack([_by_chip(grads[l][key], axis) for l in range(DEPTH)]).astype(BF16)
             for _, key, _, axis in _BIG]
    received = _pair_exchange(g_all)
    pair = [_pair_sum(g_all[w], received[w], core, f"grad_pair_sum_{w}") for w in range(N_BIG)]
    by_chip = _shard_exchange(pair)
    g_layer = [_sum4(by_chip[w], f"grad_chip_sum_{w}") for w in range(N_BIG)]
    g_other = _result_exchange(g_layer)
    out_g, out_d, out_m, out_v = {}, {}, {}, {}
    for w, (name, _, _, _) in enumerate(_BIG):
        out_g[name], out_d[name], out_m[name], out_v[name] = _adamw_big(
            weights[name], mom1[name], mom2[name], g_layer[w], g_other[w], core, f"adamw_{name}")

    small_grads = {name: jnp.stack([grads[l][key] for l in range(DEPTH)])
                   for name, key in [("b_fgate", "b_fgate"), ("hgrn_norm_g", "norm_g"), ("ln1_g", "ln1_g"),
                                     ("ln1_b", "ln1_b"), ("ln2_g", "ln2_g"), ("ln2_b", "ln2_b")]}
    small_grads["hgrn_lb_logits"] = d_logits
    gs = _allreduce_small(_pack_small(small_grads))
    ds, ms, vs = _adamw_small(_pack_small(small), gs, _pack_small({n: mom1[n] for n, _ in _SMALL}),
                              _pack_small({n: mom2[n] for n, _ in _SMALL}), "adamw_small")
    for tree, slab in ((out_g, gs), (out_d, ds), (out_m, ms), (out_v, vs)):
        tree.update(_unpack_small(slab))

    loss = lax.psum(loss_part, ("x", "y", "c"))
    order = ["w_in", "b_fgate", "hgrn_lb_logits", "hgrn_norm_g", "w_branch_a", "w_branch_b", "w_out", "ln1_g", "ln1_b",
             "w_ff_in", "w_ff_out", "ln2_g", "ln2_b"]
    return (loss, grad_x[None], *[out_g[n] for n in order], *[out_d[n] for n in order],
            *[out_m[n] for n in order], *[out_v[n] for n in order])
```
